```python
import jax
import jax.numpy as jnp
from jax import lax
import numpy as np

D_MODEL = 2048
BATCH = 8
SEQ = 2048
DEPTH = 4

CHUNK = 64
QBLOCK = 128
ROPE_BASE = 10000.0
MAX_POS_OFFSET = 4096
NORM_EPS = 1e-6

RET_HEADS = 8
RET_QK_DIM = 128
RET_V_DIM = 128
RET_WIDTH = RET_HEADS * RET_V_DIM

LRU_WIDTH = 1024
LRU_BLOCKS = 8
LRU_BLOCK_DIM = LRU_WIDTH // LRU_BLOCKS
CONV_WIDTH = 4
LRU_C = 8.0

MLA_HEADS = 8
MLA_NOPE_DIM = 128
MLA_ROPE_DIM = 64
MLA_V_DIM = 128
MLA_Q_LORA = 512
MLA_KV_LORA = 512
MLA_WIDTH = MLA_HEADS * MLA_V_DIM

N_BRANCH = 3
MIX_WIDTH = RET_WIDTH + LRU_WIDTH + MLA_WIDTH
IN_SPLITS = (
    RET_HEADS * RET_QK_DIM,
    RET_HEADS * RET_QK_DIM,
    RET_WIDTH,
    RET_WIDTH,
    LRU_WIDTH,
    LRU_WIDTH,
    MLA_Q_LORA,
    MLA_KV_LORA,
    MLA_ROPE_DIM,
    MLA_WIDTH,
    N_BRANCH * D_MODEL,
)
IN_WIDTH = sum(IN_SPLITS)

kernel_name = 'hybrid_retention_rglru_mla_streaming_block'


def rms_norm(x, gain):
    xf = x.astype(jnp.float32)
    y = xf * lax.rsqrt(jnp.mean(xf * xf, axis=-1, keepdims=True) + NORM_EPS)
    return (y * gain.astype(jnp.float32)).astype(x.dtype)


def rope_tables(positions, dim):
    inv_freq = ROPE_BASE ** (-jnp.arange(0, dim, 2, dtype=jnp.float32) / dim)
    ang = positions.astype(jnp.float32)[:, :, None, None] * inv_freq
    return jnp.cos(ang), jnp.sin(ang)


def apply_rope(x, cos, sin):
    half = x.shape[-1] // 2
    xf = x.astype(jnp.float32)
    x1, x2 = xf[..., :half], xf[..., half:]
    return jnp.concatenate([x1 * cos - x2 * sin, x2 * cos + x1 * sin], axis=-1).astype(x.dtype)


def retention_branch(q, k, v, gate, gn, cos, sin):
    B, S = q.shape[:2]
    NC = S // CHUNK
    q = apply_rope(q.reshape(B, S, RET_HEADS, RET_QK_DIM), cos, sin) * (RET_QK_DIM ** -0.5)
    k = apply_rope(k.reshape(B, S, RET_HEADS, RET_QK_DIM), cos, sin)
    q = q.reshape(B, NC, CHUNK, RET_HEADS, RET_QK_DIM)
    k = k.reshape(B, NC, CHUNK, RET_HEADS, RET_QK_DIM)
    v = v.reshape(B, NC, CHUNK, RET_HEADS, RET_V_DIM)

    log_gamma = jnp.log1p(-jnp.exp2(-5.0 - jnp.arange(RET_HEADS, dtype=jnp.float32)))
    idx = jnp.arange(CHUNK, dtype=jnp.float32)
    intra_decay = jnp.exp(log_gamma[:, None, None] * jnp.abs(idx[:, None] - idx[None, :]))

    scores = jnp.einsum('bnihd,bnjhd->bhnij', q, k) * intra_decay[None, :, None]
    o_intra = jnp.einsum('bhnij,bnjhe->bnihe', scores, v)

    k_dec = k * jnp.exp(log_gamma[None, :] * (CHUNK - 1 - idx)[:, None])[None, None, :, :, None]
    kv_chunk = jnp.einsum('bnjhd,bnjhe->nbhde', k_dec, v)
    chunk_decay = jnp.exp(log_gamma * CHUNK)[None, :, None, None]

    def step(state, kv_n):
        return state * chunk_decay + kv_n, state

    _, prev_state = lax.scan(step, jnp.zeros(kv_chunk.shape[1:], kv_chunk.dtype), kv_chunk)
    q_dec = q * jnp.exp(log_gamma[None, :] * (idx + 1.0)[:, None])[None, None, :, :, None]
    o_inter = jnp.einsum('bnihd,nbhde->bnihe', q_dec, prev_state)

    o = (o_intra + o_inter).reshape(B, S, RET_HEADS, RET_V_DIM).astype(jnp.float32)
    mean = jnp.mean(o, axis=-1, keepdims=True)
    var = jnp.mean(jnp.square(o - mean), axis=-1, keepdims=True)
    o = ((o - mean) * lax.rsqrt(var + NORM_EPS)).reshape(B, S, RET_WIDTH) * gn.astype(jnp.float32)
    return o.astype(gate.dtype) * jax.nn.silu(gate)


def rglru_branch(xb, gate, conv_w, conv_b, wa, ba, wx, bx, lam):
    B, S, W = xb.shape
    xc = lax.conv_general_dilated(
        xb, conv_w[:, None, :].astype(xb.dtype), window_strides=(1,),
        padding=[(CONV_WIDTH - 1, 0)], dimension_numbers=('NWC', 'WIO', 'NWC'),
        feature_group_count=W) + conv_b
    xr = xc.reshape(B, S, LRU_BLOCKS, LRU_BLOCK_DIM)
    r = jax.nn.sigmoid(jnp.einsum('bsnc,ncd->bsnd', xr, wa).reshape(B, S, W) + ba)
    i = jax.nn.sigmoid(jnp.einsum('bsnc,ncd->bsnd', xr, wx).reshape(B, S, W) + bx)
    log_a = -LRU_C * r.astype(jnp.float32) * jax.nn.softplus(-lam.astype(jnp.float32))
    a = jnp.exp(log_a)
    b = jnp.sqrt(-jnp.expm1(2.0 * log_a)) * (i * xc).astype(jnp.float32)

    def combine(left, right):
        a1, b1 = left
        a2, b2 = right
        return a1 * a2, a2 * b1 + b2

    _, h = lax.associative_scan(combine, (a, b), axis=1)
    return h.astype(xb.dtype) * jax.nn.silu(gate)


def mla_branch(q_lat, kv_lat, k_rope, gate, q_norm, w_uq, kv_norm, w_ukv, cos, sin):
    B, S = q_lat.shape[:2]
    q = (rms_norm(q_lat, q_norm) @ w_uq).reshape(B, S, MLA_HEADS, MLA_NOPE_DIM + MLA_ROPE_DIM)
    q_nope = q[..., :MLA_NOPE_DIM]
    q_rope = apply_rope(q[..., MLA_NOPE_DIM:], cos, sin)
    kv = (rms_norm(kv_lat, kv_norm) @ w_ukv).reshape(B, S, MLA_HEADS, MLA_NOPE_DIM + MLA_V_DIM)
    k_nope, v = kv[..., :MLA_NOPE_DIM], kv[..., MLA_NOPE_DIM:]
    k_rope = apply_rope(k_rope[:, :, None, :], cos, sin)[:, :, 0]
    scale = (MLA_NOPE_DIM + MLA_ROPE_DIM) ** -0.5

    outs = []
    for qb in range(S // QBLOCK):
        qs, qe = qb * QBLOCK, (qb + 1) * QBLOCK
        s = (jnp.einsum('bqhd,bkhd->bhqk', q_nope[:, qs:qe], k_nope[:, :qe])
             + jnp.einsum('bqhr,bkr->bhqk', q_rope[:, qs:qe], k_rope[:, :qe]))
        s = s.astype(jnp.float32) * scale
        q_chunk = (qs + jnp.arange(QBLOCK)) // CHUNK
        k_chunk = jnp.arange(qe) // CHUNK
        mask = k_chunk[None, :] <= q_chunk[:, None]
        p = jax.nn.softmax(jnp.where(mask, s, -1e30), axis=-1).astype(v.dtype)
        outs.append(jnp.einsum('bhqk,bkhd->bqhd', p, v[:, :qe]))
    o = jnp.concatenate(outs, axis=1).reshape(B, S, MLA_WIDTH)
    return o * jax.nn.silu(gate)


def hybrid_layer(x, c_act, ada_w, ada_b, norm_pre, norm_post, w_in, ret_gn,
                 lru_conv_w, lru_conv_b, lru_wa, lru_ba, lru_wx, lru_bx, lru_lambda,
                 mla_q_norm, mla_w_uq, mla_kv_norm, mla_w_ukv, w_branch, w_out,
                 cos_ret, sin_ret, cos_mla, sin_mla):
    B, S, _ = x.shape
    mod = c_act @ ada_w + ada_b
    shift, scale, res_gate = jnp.split(mod, 3, axis=-1)
    h = rms_norm(x, norm_pre) * (1.0 + scale[:, None, :]) + shift[:, None, :]

    proj = h @ w_in
    offsets = [int(o) for o in np.cumsum(IN_SPLITS)[:-1]]
    (rq, rk, rv, rg, lx, lg, mq, mkv, mkr, mg, merge_logits) = jnp.split(proj, offsets, axis=-1)

    y_ret = retention_branch(rq, rk, rv, rg, ret_gn, cos_ret, sin_ret)
    y_lru = rglru_branch(lx, lg, lru_conv_w, lru_conv_b, lru_wa, lru_ba, lru_wx, lru_bx, lru_lambda)
    y_mla = mla_branch(mq, mkv, mkr, mg, mla_q_norm, mla_w_uq, mla_kv_norm, mla_w_ukv, cos_mla, sin_mla)

    gates = jax.nn.sigmoid(merge_logits.astype(jnp.float32)).astype(x.dtype).reshape(B, S, N_BRANCH, D_MODEL)
    wb_ret = w_branch[:RET_WIDTH]
    wb_lru = w_branch[RET_WIDTH:RET_WIDTH + LRU_WIDTH]
    wb_mla = w_branch[RET_WIDTH + LRU_WIDTH:]
    merged = (gates[:, :, 0] * (y_ret @ wb_ret)
              + gates[:, :, 1] * (y_lru @ wb_lru)
              + gates[:, :, 2] * (y_mla @ wb_mla))
    y = merged @ w_out
    return x + (1.0 + res_gate[:, None, :]) * rms_norm(y, norm_post)


def _fwd_setup_inputs(seed: int = 0) -> dict:
    key = jax.random.key(seed)
    ks = jax.random.split(key, 24)
    f32 = jnp.float32

    def nrm(k, shape, s):
        return jax.random.normal(k, shape, f32) * s

    x = nrm(ks[0], (BATCH, SEQ, D_MODEL), 1.0)
    c = nrm(ks[1], (BATCH, D_MODEL), 1.0)
    positions = (jnp.arange(SEQ, dtype=jnp.int32)[None, :]
                 + jax.random.randint(ks[2], (BATCH, 1), 0, MAX_POS_OFFSET, dtype=jnp.int32))
    ada_w = nrm(ks[3], (DEPTH, D_MODEL, 3 * D_MODEL), 0.5 * D_MODEL ** -0.5)
    ada_b = nrm(ks[4], (DEPTH, 3 * D_MODEL), 0.01)
    norm_pre = 1.0 + nrm(ks[5], (DEPTH, D_MODEL), 0.01)
    norm_post = 1.0 + nrm(ks[6], (DEPTH, D_MODEL), 0.01)
    w_in = nrm(ks[7], (DEPTH, D_MODEL, IN_WIDTH), D_MODEL ** -0.5)
    ret_gn = 1.0 + nrm(ks[8], (DEPTH, RET_WIDTH), 0.01)
    lru_conv_w = nrm(ks[9], (DEPTH, CONV_WIDTH, LRU_WIDTH), CONV_WIDTH ** -0.5)
    lru_conv_b = nrm(ks[10], (DEPTH, LRU_WIDTH), 0.01)
    lru_wa = nrm(ks[11], (DEPTH, LRU_BLOCKS, LRU_BLOCK_DIM, LRU_BLOCK_DIM), LRU_BLOCK_DIM ** -0.5)
    lru_ba = nrm(ks[12], (DEPTH, LRU_WIDTH), 0.01)
    lru_wx = nrm(ks[13], (DEPTH, LRU_BLOCKS, LRU_BLOCK_DIM, LRU_BLOCK_DIM), LRU_BLOCK_DIM ** -0.5)
    lru_bx = nrm(ks[14], (DEPTH, LRU_WIDTH), 0.01)
    u = jax.random.uniform(ks[15], (DEPTH, LRU_WIDTH), f32, 0.9, 0.999)
    a0 = u ** (1.0 / LRU_C)
    lru_lambda = jnp.log(a0) - jnp.log1p(-a0)
    mla_q_norm = 1.0 + nrm(ks[16], (DEPTH, MLA_Q_LORA), 0.01)
    mla_w_uq = nrm(ks[17], (DEPTH, MLA_Q_LORA, MLA_HEADS * (MLA_NOPE_DIM + MLA_ROPE_DIM)), MLA_Q_LORA ** -0.5)
    mla_kv_norm = 1.0 + nrm(ks[18], (DEPTH, MLA_KV_LORA), 0.01)
    mla_w_ukv = nrm(ks[19], (DEPTH, MLA_KV_LORA, MLA_HEADS * (MLA_NOPE_DIM + MLA_V_DIM)), MLA_KV_LORA ** -0.5)
    w_branch = nrm(ks[20], (DEPTH, MIX_WIDTH, D_MODEL), (MIX_WIDTH // N_BRANCH) ** -0.5)
    w_out = nrm(ks[21], (DEPTH, D_MODEL, D_MODEL), D_MODEL ** -0.5)
    return {'x': x, 'c': c, 'positions': positions, 'ada_w': ada_w, 'ada_b': ada_b,
            'norm_pre': norm_pre, 'norm_post': norm_post, 'w_in': w_in, 'ret_gn': ret_gn,
            'lru_conv_w': lru_conv_w, 'lru_conv_b': lru_conv_b, 'lru_wa': lru_wa, 'lru_ba': lru_ba,
            'lru_wx': lru_wx, 'lru_bx': lru_bx, 'lru_lambda': lru_lambda,
            'mla_q_norm': mla_q_norm, 'mla_w_uq': mla_w_uq, 'mla_kv_norm': mla_kv_norm,
            'mla_w_ukv': mla_w_ukv, 'w_branch': w_branch, 'w_out': w_out}


def _fwd_reference(x, c, positions, ada_w, ada_b, norm_pre, norm_post, w_in, ret_gn,
              lru_conv_w, lru_conv_b, lru_wa, lru_ba, lru_wx, lru_bx, lru_lambda,
              mla_q_norm, mla_w_uq, mla_kv_norm, mla_w_ukv, w_branch, w_out):
    c_act = jax.nn.silu(c)
    cos_ret, sin_ret = rope_tables(positions, RET_QK_DIM)
    cos_mla, sin_mla = rope_tables(positions, MLA_ROPE_DIM)
    for l in range(DEPTH):
        x = hybrid_layer(x, c_act, ada_w[l], ada_b[l], norm_pre[l], norm_post[l], w_in[l], ret_gn[l],
                         lru_conv_w[l], lru_conv_b[l], lru_wa[l], lru_ba[l], lru_wx[l], lru_bx[l],
                         lru_lambda[l], mla_q_norm[l], mla_w_uq[l], mla_kv_norm[l], mla_w_ukv[l],
                         w_branch[l], w_out[l], cos_ret, sin_ret, cos_mla, sin_mla)
    return x


import jax as _jax
import jax.numpy as _jnp

TWIN_FORMAT = 'train_step'
FWD_PARAMS = ['x', 'c', 'positions', 'ada_w', 'ada_b', 'norm_pre', 'norm_post', 'w_in', 'ret_gn', 'lru_conv_w', 'lru_conv_b', 'lru_wa', 'lru_ba', 'lru_wx', 'lru_bx', 'lru_lambda', 'mla_q_norm', 'mla_w_uq', 'mla_kv_norm', 'mla_w_ukv', 'w_branch', 'w_out']
TWIN_WEIGHTS = ['ada_w', 'ada_b', 'norm_pre', 'norm_post', 'w_in', 'ret_gn', 'lru_conv_w', 'lru_conv_b', 'lru_wa', 'lru_ba', 'lru_wx', 'lru_bx', 'lru_lambda', 'mla_q_norm', 'mla_w_uq', 'mla_kv_norm', 'mla_w_ukv', 'w_branch', 'w_out']
TWIN_DIFF_INPUT = 'x'
TWIN_INPUTS = ['x', 'c', 'positions', 'ada_w', 'ada_b', 'norm_pre', 'norm_post', 'w_in', 'ret_gn', 'lru_conv_w', 'lru_conv_b', 'lru_wa', 'lru_ba', 'lru_wx', 'lru_bx', 'lru_lambda', 'mla_q_norm', 'mla_w_uq', 'mla_kv_norm', 'mla_w_ukv', 'w_branch', 'w_out', 'loss_target', 'm_ada_w', 'm_ada_b', 'm_norm_pre', 'm_norm_post', 'm_w_in', 'm_ret_gn', 'm_lru_conv_w', 'm_lru_conv_b', 'm_lru_wa', 'm_lru_ba', 'm_lru_wx', 'm_lru_bx', 'm_lru_lambda', 'm_mla_q_norm', 'm_mla_w_uq', 'm_mla_kv_norm', 'm_mla_w_ukv', 'm_w_branch', 'm_w_out', 'v_ada_w', 'v_ada_b', 'v_norm_pre', 'v_norm_post', 'v_w_in', 'v_ret_gn', 'v_lru_conv_w', 'v_lru_conv_b', 'v_lru_wa', 'v_lru_ba', 'v_lru_wx', 'v_lru_bx', 'v_lru_lambda', 'v_mla_q_norm', 'v_mla_w_uq', 'v_mla_kv_norm', 'v_mla_w_ukv', 'v_w_branch', 'v_w_out']
TWIN_OUTPUTS = ['loss', 'grad_x', 'grad_ada_w', 'grad_ada_b', 'grad_norm_pre', 'grad_norm_post', 'grad_w_in', 'grad_ret_gn', 'grad_lru_conv_w', 'grad_lru_conv_b', 'grad_lru_wa', 'grad_lru_ba', 'grad_lru_wx', 'grad_lru_bx', 'grad_lru_lambda', 'grad_mla_q_norm', 'grad_mla_w_uq', 'grad_mla_kv_norm', 'grad_mla_w_ukv', 'grad_w_branch', 'grad_w_out', 'delta_ada_w', 'delta_ada_b', 'delta_norm_pre', 'delta_norm_post', 'delta_w_in', 'delta_ret_gn', 'delta_lru_conv_w', 'delta_lru_conv_b', 'delta_lru_wa', 'delta_lru_ba', 'delta_lru_wx', 'delta_lru_bx', 'delta_lru_lambda', 'delta_mla_q_norm', 'delta_mla_w_uq', 'delta_mla_kv_norm', 'delta_mla_w_ukv', 'delta_w_branch', 'delta_w_out', 'new_m_ada_w', 'new_m_ada_b', 'new_m_norm_pre', 'new_m_norm_post', 'new_m_w_in', 'new_m_ret_gn', 'new_m_lru_conv_w', 'new_m_lru_conv_b', 'new_m_lru_wa', 'new_m_lru_ba', 'new_m_lru_wx', 'new_m_lru_bx', 'new_m_lru_lambda', 'new_m_mla_q_norm', 'new_m_mla_w_uq', 'new_m_mla_kv_norm', 'new_m_mla_w_ukv', 'new_m_w_branch', 'new_m_w_out', 'new_v_ada_w', 'new_v_ada_b', 'new_v_norm_pre', 'new_v_norm_post', 'new_v_w_in', 'new_v_ret_gn', 'new_v_lru_conv_w', 'new_v_lru_conv_b', 'new_v_lru_wa', 'new_v_lru_ba', 'new_v_lru_wx', 'new_v_lru_bx', 'new_v_lru_lambda', 'new_v_mla_q_norm', 'new_v_mla_w_uq', 'new_v_mla_kv_norm', 'new_v_mla_w_ukv', 'new_v_w_branch', 'new_v_w_out']
TWIN_LEAF_KINDS = {'loss': 'loss', 'grad_x': 'grad_x', 'grad_ada_w': 'grad_w', 'grad_ada_b': 'grad_w', 'grad_norm_pre': 'grad_w', 'grad_norm_post': 'grad_w', 'grad_w_in': 'grad_w', 'grad_ret_gn': 'grad_w', 'grad_lru_conv_w': 'grad_w', 'grad_lru_conv_b': 'grad_w', 'grad_lru_wa': 'grad_w', 'grad_lru_ba': 'grad_w', 'grad_lru_wx': 'grad_w', 'grad_lru_bx': 'grad_w', 'grad_lru_lambda': 'grad_w', 'grad_mla_q_norm': 'grad_w', 'grad_mla_w_uq': 'grad_w', 'grad_mla_kv_norm': 'grad_w', 'grad_mla_w_ukv': 'grad_w', 'grad_w_branch': 'grad_w', 'grad_w_out': 'grad_w', 'delta_ada_w': 'delta_w', 'delta_ada_b': 'delta_w', 'delta_norm_pre': 'delta_w', 'delta_norm_post': 'delta_w', 'delta_w_in': 'delta_w', 'delta_ret_gn': 'delta_w', 'delta_lru_conv_w': 'delta_w', 'delta_lru_conv_b': 'delta_w', 'delta_lru_wa': 'delta_w', 'delta_lru_ba': 'delta_w', 'delta_lru_wx': 'delta_w', 'delta_lru_bx': 'delta_w', 'delta_lru_lambda': 'delta_w', 'delta_mla_q_norm': 'delta_w', 'delta_mla_w_uq': 'delta_w', 'delta_mla_kv_norm': 'delta_w', 'delta_mla_w_ukv': 'delta_w', 'delta_w_branch': 'delta_w', 'delta_w_out': 'delta_w', 'new_m_ada_w': 'new_m', 'new_m_ada_b': 'new_m', 'new_m_norm_pre': 'new_m', 'new_m_norm_post': 'new_m', 'new_m_w_in': 'new_m', 'new_m_ret_gn': 'new_m', 'new_m_lru_conv_w': 'new_m', 'new_m_lru_conv_b': 'new_m', 'new_m_lru_wa': 'new_m', 'new_m_lru_ba': 'new_m', 'new_m_lru_wx': 'new_m', 'new_m_lru_bx': 'new_m', 'new_m_lru_lambda': 'new_m', 'new_m_mla_q_norm': 'new_m', 'new_m_mla_w_uq': 'new_m', 'new_m_mla_kv_norm': 'new_m', 'new_m_mla_w_ukv': 'new_m', 'new_m_w_branch': 'new_m', 'new_m_w_out': 'new_m', 'new_v_ada_w': 'new_v', 'new_v_ada_b': 'new_v', 'new_v_norm_pre': 'new_v', 'new_v_norm_post': 'new_v', 'new_v_w_in': 'new_v', 'new_v_ret_gn': 'new_v', 'new_v_lru_conv_w': 'new_v', 'new_v_lru_conv_b': 'new_v', 'new_v_lru_wa': 'new_v', 'new_v_lru_ba': 'new_v', 'new_v_lru_wx': 'new_v', 'new_v_lru_bx': 'new_v', 'new_v_lru_lambda': 'new_v', 'new_v_mla_q_norm': 'new_v', 'new_v_mla_w_uq': 'new_v', 'new_v_mla_kv_norm': 'new_v', 'new_v_mla_w_ukv': 'new_v', 'new_v_w_branch': 'new_v', 'new_v_w_out': 'new_v'}


def _forward(args):
    return _fwd_reference(*[args[k] for k in FWD_PARAMS])


def _output_shape():
    out = _jax.eval_shape(lambda: _forward(_fwd_setup_inputs(0)))
    return out.shape, out.dtype

N_MICROBATCH = 1
ADAM_LR = 0.001
ADAM_B1 = 0.9
ADAM_B2 = 0.999
ADAM_EPS = 1e-08
ADAM_WD = 0.01
ADAM_STEP = 10
PER_EXAMPLE_BATCH_AXIS = {'x': 0, 'c': 0, 'positions': 0, 'loss_target': 0}
SHARED_INPUTS = []
_WEIGHT_DTYPES = {'ada_w': _jnp.float32, 'ada_b': _jnp.float32, 'norm_pre': _jnp.float32, 'norm_post': _jnp.float32, 'w_in': _jnp.float32, 'ret_gn': _jnp.float32, 'lru_conv_w': _jnp.float32, 'lru_conv_b': _jnp.float32, 'lru_wa': _jnp.float32, 'lru_ba': _jnp.float32, 'lru_wx': _jnp.float32, 'lru_bx': _jnp.float32, 'lru_lambda': _jnp.float32, 'mla_q_norm': _jnp.float32, 'mla_w_uq': _jnp.float32, 'mla_kv_norm': _jnp.float32, 'mla_w_ukv': _jnp.float32, 'w_branch': _jnp.float32, 'w_out': _jnp.float32}
MOMENT_SCALE = {'ada_w': 2.547020e+00, 'ada_b': 5.680263e+00, 'norm_pre': 9.026939e-01, 'norm_post': 9.365547e+00, 'w_in': 6.562278e-01, 'ret_gn': 2.539944e-01, 'lru_conv_w': 1.910806e+00, 'lru_conv_b': 5.488610e+00, 'lru_wa': 2.154791e-01, 'lru_ba': 2.351586e-01, 'lru_wx': 4.583727e-01, 'lru_bx': 7.172623e-01, 'lru_lambda': 5.701537e-01, 'mla_q_norm': 5.367516e-02, 'mla_w_uq': 3.067795e-02, 'mla_kv_norm': 4.214874e-01, 'mla_w_ukv': 2.177389e-01, 'w_branch': 9.346449e-01, 'w_out': 1.625005e+00}


def _to_microbatches(a, axis):
    t = _jnp.moveaxis(a, axis, 0)
    t = t.reshape((N_MICROBATCH, t.shape[0] // N_MICROBATCH) + t.shape[1:])
    return _jnp.moveaxis(t, 1, axis + 1)


def setup_inputs(seed: int = 0) -> dict:
    inp = _fwd_setup_inputs(seed)
    key = _jax.random.fold_in(_jax.random.key(seed), 7919)
    shape, _ = _output_shape()
    out = dict(inp)
    out["loss_target"] = _jax.random.normal(_jax.random.fold_in(key, 0), shape, _jnp.float32)
    for i, name in enumerate(TWIN_WEIGHTS):
        w = inp[name].astype(_jnp.float32)
        if MOMENT_SCALE is None:
            s = _jnp.sqrt(_jnp.mean(_jnp.square(w)) + 1e-30)
        else:
            s = MOMENT_SCALE[name]
        km, kv = _jax.random.split(_jax.random.fold_in(key, i + 1))
        out[name] = w
        out["m_" + name] = s * _jax.random.normal(km, w.shape, _jnp.float32)
        out["v_" + name] = (s * s) * _jax.random.uniform(kv, w.shape, _jnp.float32, 0.5, 1.5)
    if N_MICROBATCH > 1:
        for name, axis in PER_EXAMPLE_BATCH_AXIS.items():
            out[name] = _to_microbatches(out[name], axis)
    return {'x': out['x'], 'c': out['c'], 'positions': out['positions'], 'ada_w': out['ada_w'], 'ada_b': out['ada_b'], 'norm_pre': out['norm_pre'], 'norm_post': out['norm_post'], 'w_in': out['w_in'], 'ret_gn': out['ret_gn'], 'lru_conv_w': out['lru_conv_w'], 'lru_conv_b': out['lru_conv_b'], 'lru_wa': out['lru_wa'], 'lru_ba': out['lru_ba'], 'lru_wx': out['lru_wx'], 'lru_bx': out['lru_bx'], 'lru_lambda': out['lru_lambda'], 'mla_q_norm': out['mla_q_norm'], 'mla_w_uq': out['mla_w_uq'], 'mla_kv_norm': out['mla_kv_norm'], 'mla_w_ukv': out['mla_w_ukv'], 'w_branch': out['w_branch'], 'w_out': out['w_out'], 'loss_target': out['loss_target'], 'm_ada_w': out['m_ada_w'], 'm_ada_b': out['m_ada_b'], 'm_norm_pre': out['m_norm_pre'], 'm_norm_post': out['m_norm_post'], 'm_w_in': out['m_w_in'], 'm_ret_gn': out['m_ret_gn'], 'm_lru_conv_w': out['m_lru_conv_w'], 'm_lru_conv_b': out['m_lru_conv_b'], 'm_lru_wa': out['m_lru_wa'], 'm_lru_ba': out['m_lru_ba'], 'm_lru_wx': out['m_lru_wx'], 'm_lru_bx': out['m_lru_bx'], 'm_lru_lambda': out['m_lru_lambda'], 'm_mla_q_norm': out['m_mla_q_norm'], 'm_mla_w_uq': out['m_mla_w_uq'], 'm_mla_kv_norm': out['m_mla_kv_norm'], 'm_mla_w_ukv': out['m_mla_w_ukv'], 'm_w_branch': out['m_w_branch'], 'm_w_out': out['m_w_out'], 'v_ada_w': out['v_ada_w'], 'v_ada_b': out['v_ada_b'], 'v_norm_pre': out['v_norm_pre'], 'v_norm_post': out['v_norm_post'], 'v_w_in': out['v_w_in'], 'v_ret_gn': out['v_ret_gn'], 'v_lru_conv_w': out['v_lru_conv_w'], 'v_lru_conv_b': out['v_lru_conv_b'], 'v_lru_wa': out['v_lru_wa'], 'v_lru_ba': out['v_lru_ba'], 'v_lru_wx': out['v_lru_wx'], 'v_lru_bx': out['v_lru_bx'], 'v_lru_lambda': out['v_lru_lambda'], 'v_mla_q_norm': out['v_mla_q_norm'], 'v_mla_w_uq': out['v_mla_w_uq'], 'v_mla_kv_norm': out['v_mla_kv_norm'], 'v_mla_w_ukv': out['v_mla_w_ukv'], 'v_w_branch': out['v_w_branch'], 'v_w_out': out['v_w_out']}


def _loss(weights, diff, rest, loss_target):
    with _jax.named_scope("forward"):
        args = {**rest, TWIN_DIFF_INPUT: diff, **{k: w.astype(_WEIGHT_DTYPES[k]) for k, w in weights.items()}}
        y = _forward(args)
    with _jax.named_scope("loss_head"):
        err = _jnp.square(y.astype(_jnp.float32) - loss_target)
        return 0.5 * _jnp.sum(_jnp.mean(err, axis=-1)) if err.ndim else 0.5 * err


def _adamw(w, g, m, v):
    m = ADAM_B1 * m + (1.0 - ADAM_B1) * g
    v = ADAM_B2 * v + (1.0 - ADAM_B2) * _jnp.square(g)
    m_hat = m / (1.0 - ADAM_B1 ** ADAM_STEP)
    v_hat = v / (1.0 - ADAM_B2 ** ADAM_STEP)
    delta = -ADAM_LR * (m_hat / (_jnp.sqrt(v_hat) + ADAM_EPS) + ADAM_WD * w)
    return delta, m, v


def reference(x, c, positions, ada_w, ada_b, norm_pre, norm_post, w_in, ret_gn, lru_conv_w, lru_conv_b, lru_wa, lru_ba, lru_wx, lru_bx, lru_lambda, mla_q_norm, mla_w_uq, mla_kv_norm, mla_w_ukv, w_branch, w_out, loss_target, m_ada_w, m_ada_b, m_norm_pre, m_norm_post, m_w_in, m_ret_gn, m_lru_conv_w, m_lru_conv_b, m_lru_wa, m_lru_ba, m_lru_wx, m_lru_bx, m_lru_lambda, m_mla_q_norm, m_mla_w_uq, m_mla_kv_norm, m_mla_w_ukv, m_w_branch, m_w_out, v_ada_w, v_ada_b, v_norm_pre, v_norm_post, v_w_in, v_ret_gn, v_lru_conv_w, v_lru_conv_b, v_lru_wa, v_lru_ba, v_lru_wx, v_lru_bx, v_lru_lambda, v_mla_q_norm, v_mla_w_uq, v_mla_kv_norm, v_mla_w_ukv, v_w_branch, v_w_out):
    given = dict(x=x, c=c, positions=positions, ada_w=ada_w, ada_b=ada_b, norm_pre=norm_pre, norm_post=norm_post, w_in=w_in, ret_gn=ret_gn, lru_conv_w=lru_conv_w, lru_conv_b=lru_conv_b, lru_wa=lru_wa, lru_ba=lru_ba, lru_wx=lru_wx, lru_bx=lru_bx, lru_lambda=lru_lambda, mla_q_norm=mla_q_norm, mla_w_uq=mla_w_uq, mla_kv_norm=mla_kv_norm, mla_w_ukv=mla_w_ukv, w_branch=w_branch, w_out=w_out, loss_target=loss_target, m_ada_w=m_ada_w, m_ada_b=m_ada_b, m_norm_pre=m_norm_pre, m_norm_post=m_norm_post, m_w_in=m_w_in, m_ret_gn=m_ret_gn, m_lru_conv_w=m_lru_conv_w, m_lru_conv_b=m_lru_conv_b, m_lru_wa=m_lru_wa, m_lru_ba=m_lru_ba, m_lru_wx=m_lru_wx, m_lru_bx=m_lru_bx, m_lru_lambda=m_lru_lambda, m_mla_q_norm=m_mla_q_norm, m_mla_w_uq=m_mla_w_uq, m_mla_kv_norm=m_mla_kv_norm, m_mla_w_ukv=m_mla_w_ukv, m_w_branch=m_w_branch, m_w_out=m_w_out, v_ada_w=v_ada_w, v_ada_b=v_ada_b, v_norm_pre=v_norm_pre, v_norm_post=v_norm_post, v_w_in=v_w_in, v_ret_gn=v_ret_gn, v_lru_conv_w=v_lru_conv_w, v_lru_conv_b=v_lru_conv_b, v_lru_wa=v_lru_wa, v_lru_ba=v_lru_ba, v_lru_wx=v_lru_wx, v_lru_bx=v_lru_bx, v_lru_lambda=v_lru_lambda, v_mla_q_norm=v_mla_q_norm, v_mla_w_uq=v_mla_w_uq, v_mla_kv_norm=v_mla_kv_norm, v_mla_w_ukv=v_mla_w_ukv, v_w_branch=v_w_branch, v_w_out=v_w_out)
    weights = {n: given[n] for n in TWIN_WEIGHTS}
    shared = {n: given[n] for n in SHARED_INPUTS}
    per_example = {n: given[n] for n in ['x', 'c', 'positions']}
    grad_fn = _jax.value_and_grad(_loss, argnums=(0, 1))

    def one_microbatch(ex, loss_target):
        ex = dict(ex)
        diff = ex.pop(TWIN_DIFF_INPUT)
        return grad_fn(weights, diff, {**shared, **ex}, loss_target)

    if N_MICROBATCH == 1:
        loss, (grad_w, grad_x) = one_microbatch(per_example, given["loss_target"])
    else:
        def body(carry, xs):
            loss_sum, grad_sum = carry
            l_k, (gw_k, gx_k) = one_microbatch(xs[0], xs[1])
            with _jax.named_scope("update"):
                return (loss_sum + l_k, _jax.tree.map(_jnp.add, grad_sum, gw_k)), gx_k

        init = (_jnp.zeros((), _jnp.float32), _jax.tree.map(_jnp.zeros_like, weights))
        (loss, grad_w), grad_x = _jax.lax.scan(body, init, (per_example, given["loss_target"]))
    with _jax.named_scope("update"):
        delta_w, new_m, new_v = {}, {}, {}
        for n in TWIN_WEIGHTS:
            delta_w[n], new_m[n], new_v[n] = _adamw(weights[n], grad_w[n], given["m_" + n], given["v_" + n])
    return (loss, grad_x, *[grad_w[n] for n in TWIN_WEIGHTS], *[delta_w[n] for n in TWIN_WEIGHTS],
            *[new_m[n] for n in TWIN_WEIGHTS], *[new_v[n] for n in TWIN_WEIGHTS])
```

```python
import functools
import math
from typing import NamedTuple

import numpy as np
import jax
import jax.numpy as jnp
from jax import lax
from jax.experimental import pallas as pl
from jax.experimental.pallas import tpu as pltpu

f32 = jnp.float32
bf16 = jnp.bfloat16

NORM_EPS = 1e-6
ROPE_BASE = 10000.0
CHUNK = 64
HEAD = 128
ROPE = 64
CONV = 4
LRU_C = 8.0
ADAM_LR, ADAM_B1, ADAM_B2, ADAM_EPS, ADAM_WD, ADAM_STEP = 0.001, 0.9, 0.999, 1e-08, 0.01, 10

LANES = 128
SUBLANES = 8
VMEM_LIMIT = 56 * 1024 * 1024
MM_BUDGET = 40 * 1024 * 1024
N_DEV = 8
MESH = pl.DeviceIdType.MESH


class Cfg(NamedTuple):
    D: int = 2048
    S: int = 2048
    L: int = 4
    H: int = 8
    NB: int = 8
    MH: int = 8
    QL: int = 512
    KL: int = 512
    TR: int = 256
    TQ: int = 256

    @property
    def RW(self): return self.H * HEAD
    @property
    def LW(self): return self.NB * HEAD
    @property
    def MW(self): return self.MH * HEAD
    @property
    def o_rk(self): return self.RW
    @property
    def o_rv(self): return 2 * self.RW
    @property
    def o_rg(self): return 3 * self.RW
    @property
    def o_lx(self): return 4 * self.RW
    @property
    def o_lg(self): return 4 * self.RW + self.LW
    @property
    def o_mq(self): return 4 * self.RW + 2 * self.LW
    @property
    def o_mkv(self): return self.o_mq + self.QL
    @property
    def o_mg(self): return self.o_mkv + self.KL
    @property
    def o_merge(self): return self.o_mg + self.MW
    @property
    def o_mkr(self): return self.o_merge + 3 * self.D
    @property
    def NP(self): return -(-(self.o_mkr + ROPE) // 512) * 512
    @property
    def IN_WIDTH(self): return self.o_mkr + ROPE
    @property
    def QW(self): return self.MH * (HEAD + ROPE)
    @property
    def KVW(self): return self.MH * 2 * HEAD


_CFG = Cfg()


def _cparams(sem=None):
    return pltpu.CompilerParams(dimension_semantics=sem, vmem_limit_bytes=VMEM_LIMIT)


def _sigmoid(x):
    return jax.nn.sigmoid(x)


def _silu(x):
    return x * _sigmoid(x)


def _dsilu(x):
    s = _sigmoid(x)
    return s * (1.0 + x * (1.0 - s))


def _slab(rows, width, off):
    assert off % width == 0
    return pl.BlockSpec((rows, width), lambda i, _c=off // width: (i, _c))


def _row(width):
    return pl.BlockSpec((1, width), lambda i: (0, 0))


def _mm(a, b, out_dtype=f32, name="mm"):
    M, K = a.shape
    K2, N = b.shape
    assert K == K2
    tn = N if N <= 2048 else 512
    tk = K if K <= 2048 else 512
    assert N % tn == 0 and K % tk == 0
    osz = jnp.dtype(out_dtype).itemsize
    tm = M
    while 2 * tm * tk * 2 + 2 * tk * tn * 2 + 2 * tm * tn * osz + tm * tn * 4 > MM_BUDGET and tm % 16 == 0:
        tm //= 2
    assert M % tm == 0
    nk = K // tk

    if nk == 1:
        def body(a_ref, b_ref, o_ref):
            o_ref[...] = jnp.dot(a_ref[...].astype(bf16), b_ref[...].astype(bf16),
                                 preferred_element_type=f32).astype(o_ref.dtype)
        scratch = []
    else:
        def body(a_ref, b_ref, o_ref, acc_ref):
            k = pl.program_id(2)

            @pl.when(k == 0)
            def _():
                acc_ref[...] = jnp.zeros_like(acc_ref)

            acc_ref[...] += jnp.dot(a_ref[...].astype(bf16), b_ref[...].astype(bf16), preferred_element_type=f32)

            @pl.when(k == nk - 1)
            def _():
                o_ref[...] = acc_ref[...].astype(o_ref.dtype)
        scratch = [pltpu.VMEM((tm, tn), f32)]

    return pl.pallas_call(
        body, name=name,
        grid=(M // tm, N // tn, nk),
        in_specs=[pl.BlockSpec((tm, tk), lambda i, j, k: (i, k)), pl.BlockSpec((tk, tn), lambda i, j, k: (k, j))],
        out_specs=pl.BlockSpec((tm, tn), lambda i, j, k: (i, j)),
        out_shape=jax.ShapeDtypeStruct((M, N), out_dtype),
        scratch_shapes=scratch,
        compiler_params=_cparams(("parallel", "parallel", "arbitrary")),
    )(a, b)


def _ada_fwd(cfg, c_all, ada_w):
    L, D, n = ada_w.shape
    tn = n // 2 if (n // 2) % LANES == 0 else n

    def body(c_ref, w_ref, o_ref, ca_ref):
        ca = _silu(c_ref[...])
        ca_ref[...] = ca
        o_ref[0] = jnp.dot(ca.astype(bf16), w_ref[0].astype(bf16), preferred_element_type=f32)

    return pl.pallas_call(
        body, name="ada_fwd", grid=(L, n // tn),
        in_specs=[pl.BlockSpec((N_DEV, D), lambda l, j: (0, 0)), pl.BlockSpec((1, D, tn), lambda l, j: (l, 0, j))],
        out_specs=(pl.BlockSpec((1, N_DEV, tn), lambda l, j: (l, 0, j)), pl.BlockSpec((N_DEV, D), lambda l, j: (0, 0))),
        out_shape=(jax.ShapeDtypeStruct((L, N_DEV, n), f32), jax.ShapeDtypeStruct((N_DEV, D), f32)),
        compiler_params=_cparams(("arbitrary", "arbitrary")),
    )(c_all, ada_w)


def _ada_bwd(cfg, c_act_t, dmod):
    L, _, n = dmod.shape
    D = c_act_t.shape[0]
    tn = n // 2 if (n // 2) % LANES == 0 else n

    def body(c_ref, d_ref, o_ref):
        o_ref[0] = jnp.dot(c_ref[...].astype(bf16), d_ref[0].astype(bf16), preferred_element_type=f32)

    return pl.pallas_call(
        body, name="ada_bwd", grid=(L, n // tn),
        in_specs=[pl.BlockSpec((D, N_DEV), lambda l, j: (0, 0)), pl.BlockSpec((1, N_DEV, tn), lambda l, j: (l, 0, j))],
        out_specs=pl.BlockSpec((1, D, tn), lambda l, j: (l, 0, j)),
        out_shape=jax.ShapeDtypeStruct((L, D, n), f32),
        compiler_params=_cparams(("parallel", "parallel")),
    )(c_act_t, dmod)


def _prenorm_fwd(cfg, x, mod, gain):
    S, D, TR = cfg.S, cfg.D, cfg.TR

    def body(x_ref, mod_ref, g_ref, h_ref):
        x = x_ref[...]
        r = lax.rsqrt(jnp.mean(x * x, axis=-1, keepdims=True) + NORM_EPS)
        shift, scale = mod_ref[:, 0:D], mod_ref[:, D:2 * D]
        h_ref[...] = ((x * r) * g_ref[...] * (1.0 + scale) + shift).astype(bf16)

    return pl.pallas_call(
        body, name="prenorm_fwd", grid=(S // TR,),
        in_specs=[_slab(TR, D, 0), _row(3 * D), _row(D)],
        out_specs=_slab(TR, D, 0), out_shape=jax.ShapeDtypeStruct((S, D), bf16),
        compiler_params=_cparams(("parallel",)),
    )(x, mod, gain)


def _prenorm_bwd(cfg, x, dh, dres, mod, gain):
    S, D, TR = cfg.S, cfg.D, cfg.TR

    def body(x_ref, dh_ref, dres_ref, mod_ref, g_ref, dx_ref, sum_ref):
        i = pl.program_id(0)
        x, dh, g = x_ref[...], dh_ref[...], g_ref[...]
        scale = mod_ref[:, D:2 * D]
        r = lax.rsqrt(jnp.mean(x * x, axis=-1, keepdims=True) + NORM_EPS)
        xn = x * r
        t = dh * xn
        dxn = dh * (g * (1.0 + scale))
        dx_ref[...] = r * (dxn - xn * jnp.mean(dxn * xn, axis=-1, keepdims=True)) + dres_ref[...]
        part = jnp.concatenate([jnp.sum(dh, axis=0, keepdims=True), jnp.sum(t * g, axis=0, keepdims=True),
                                jnp.sum(t * (1.0 + scale), axis=0, keepdims=True), jnp.zeros((SUBLANES - 3, D), f32)], axis=0)

        @pl.when(i == 0)
        def _():
            sum_ref[...] = part

        @pl.when(i > 0)
        def _():
            sum_ref[...] += part

    return pl.pallas_call(
        body, name="prenorm_bwd", grid=(S // TR,),
        in_specs=[_slab(TR, D, 0), _slab(TR, D, 0), _slab(TR, D, 0), _row(3 * D), _row(D)],
        out_specs=(_slab(TR, D, 0), pl.BlockSpec((SUBLANES, D), lambda i: (0, 0))),
        out_shape=(jax.ShapeDtypeStruct((S, D), f32), jax.ShapeDtypeStruct((SUBLANES, D), f32)),
        compiler_params=_cparams(("arbitrary",)),
    )(x, dh, dres, mod, gain)


def _postnorm_fwd(cfg, x, y, mod, gain):
    S, D, TR = cfg.S, cfg.D, cfg.TR

    def body(x_ref, y_ref, mod_ref, g_ref, o_ref):
        y = y_ref[...]
        r = lax.rsqrt(jnp.mean(y * y, axis=-1, keepdims=True) + NORM_EPS)
        rg = mod_ref[:, 2 * D:3 * D]
        o_ref[...] = x_ref[...] + (1.0 + rg) * ((y * r) * g_ref[...])

    return pl.pallas_call(
        body, name="postnorm_fwd", grid=(S // TR,),
        in_specs=[_slab(TR, D, 0), _slab(TR, D, 0), _row(3 * D), _row(D)],
        out_specs=_slab(TR, D, 0), out_shape=jax.ShapeDtypeStruct((S, D), f32),
        compiler_params=_cparams(("parallel",)),
    )(x, y, mod, gain)


def _postnorm_bwd(cfg, dout, y, mod, gain):
    S, D, TR = cfg.S, cfg.D, cfg.TR

    def body(do_ref, y_ref, mod_ref, g_ref, dy_ref, sum_ref):
        i = pl.program_id(0)
        do, y, g = do_ref[...], y_ref[...], g_ref[...]
        rg = mod_ref[:, 2 * D:3 * D]
        r = lax.rsqrt(jnp.mean(y * y, axis=-1, keepdims=True) + NORM_EPS)
        yn = y * r
        t = do * yn
        dyn = do * ((1.0 + rg) * g)
        dy_ref[...] = (r * (dyn - yn * jnp.mean(dyn * yn, axis=-1, keepdims=True))).astype(bf16)
        part = jnp.concatenate([jnp.sum(t * g, axis=0, keepdims=True), jnp.sum(t * (1.0 + rg), axis=0, keepdims=True),
                                jnp.zeros((SUBLANES - 2, D), f32)], axis=0)

        @pl.when(i == 0)
        def _():
            sum_ref[...] = part

        @pl.when(i > 0)
        def _():
            sum_ref[...] += part

    return pl.pallas_call(
        body, name="postnorm_bwd", grid=(S // TR,),
        in_specs=[_slab(TR, D, 0), _slab(TR, D, 0), _row(3 * D), _row(D)],
        out_specs=(_slab(TR, D, 0), pl.BlockSpec((SUBLANES, D), lambda i: (0, 0))),
        out_shape=(jax.ShapeDtypeStruct((S, D), bf16), jax.ShapeDtypeStruct((SUBLANES, D), f32)),
        compiler_params=_cparams(("arbitrary",)),
    )(dout, y, mod, gain)


def _loss_head(cfg, y, target):
    S, D, TR = cfg.S, cfg.D, cfg.TR

    def body(y_ref, t_ref, d_ref, l_ref):
        i = pl.program_id(0)
        err = y_ref[...] - t_ref[...]
        d_ref[...] = err / D
        part = jnp.zeros((SUBLANES, LANES), f32) + 0.5 * jnp.sum(jnp.mean(err * err, axis=-1, keepdims=True))

        @pl.when(i == 0)
        def _():
            l_ref[...] = part

        @pl.when(i > 0)
        def _():
            l_ref[...] += part

    return pl.pallas_call(
        body, name="loss_head", grid=(S // TR,),
        in_specs=[_slab(TR, D, 0), _slab(TR, D, 0)],
        out_specs=(_slab(TR, D, 0), pl.BlockSpec((SUBLANES, LANES), lambda i: (0, 0))),
        out_shape=(jax.ShapeDtypeStruct((S, D), f32), jax.ShapeDtypeStruct((SUBLANES, LANES), f32)),
        compiler_params=_cparams(("arbitrary",)),
    )(y, target)


def _merge_fwd(cfg, proj, u0, u1, u2):
    S, D, TR = cfg.S, cfg.D, cfg.TR

    def body(l0, l1, l2, u0_ref, u1_ref, u2_ref, o_ref):
        o_ref[...] = (_sigmoid(l0[...]) * u0_ref[...] + _sigmoid(l1[...]) * u1_ref[...]
                      + _sigmoid(l2[...]) * u2_ref[...]).astype(bf16)

    return pl.pallas_call(
        body, name="merge_fwd", grid=(S // TR,),
        in_specs=[_slab(TR, D, cfg.o_merge + b * D) for b in range(3)] + [_slab(TR, D, 0)] * 3,
        out_specs=_slab(TR, D, 0), out_shape=jax.ShapeDtypeStruct((S, D), bf16),
        compiler_params=_cparams(("parallel",)),
    )(proj, proj, proj, u0, u1, u2)


def _merge_bwd(cfg, proj, dmerged, u0, u1, u2):
    S, D, TR = cfg.S, cfg.D, cfg.TR

    def body(l0, l1, l2, dm_ref, u0_ref, u1_ref, u2_ref, du0, du1, du2, dl_ref):
        dm = dm_ref[...]
        for b, (l, u, du) in enumerate(((l0, u0_ref, du0), (l1, u1_ref, du1), (l2, u2_ref, du2))):
            g = _sigmoid(l[...])
            du[...] = (dm * g).astype(bf16)
            dl_ref[:, b * D:(b + 1) * D] = (dm * u[...] * (g * (1.0 - g))).astype(bf16)

    return pl.pallas_call(
        body, name="merge_bwd", grid=(S // TR,),
        in_specs=[_slab(TR, D, cfg.o_merge + b * D) for b in range(3)] + [_slab(TR, D, 0)] * 4,
        out_specs=(_slab(TR, D, 0),) * 3 + (_slab(TR, 3 * D, 0),),
        out_shape=(jax.ShapeDtypeStruct((S, D), bf16),) * 3 + (jax.ShapeDtypeStruct((S, 3 * D), bf16),),
        compiler_params=_cparams(("parallel",)),
    )(proj, proj, proj, dmerged, u0, u1, u2)


def _rope128(x, c, s):
    return x * c + pltpu.roll(x, 64, axis=1) * s


def _rope128_t(dy, c, s):
    return dy * c + pltpu.roll(dy * s, 64, axis=1)


def _swap32(x):
    w = x.shape[1]
    lane = lax.broadcasted_iota(jnp.int32, x.shape, 1)
    return jnp.where((lane % 64) < 32, pltpu.roll(x, w - 32, axis=1), pltpu.roll(x, 32, axis=1))


def _rope64(x, c, s):
    return x * c + _swap32(x) * s


def _rope64_t(dy, c, s):
    return dy * c + _swap32(dy * s)


def _rope_tables(cfg, positions):
    pos = positions.astype(f32)[0][:, None]

    def tab(dim):
        inv_freq = ROPE_BASE ** (-jnp.arange(0, dim, 2, dtype=f32) / dim)
        ang = pos * inv_freq
        cos, sin = jnp.cos(ang), jnp.sin(ang)
        return jnp.concatenate([cos, cos], axis=1), jnp.concatenate([-sin, sin], axis=1)

    return tab(HEAD), tab(ROPE)


def _ret_consts(cfg):
    h = np.arange(cfg.H, dtype=np.float64)
    log_gamma = np.log1p(-np.exp2(-5.0 - h)).astype(np.float32)
    idx = np.arange(CHUNK, dtype=np.float32)
    intra = np.exp(log_gamma[:, None, None] * np.abs(idx[:, None] - idx[None, :]))
    kdec = np.exp(log_gamma[:, None] * (CHUNK - 1 - idx)[None, :])
    qdec = np.exp(log_gamma[:, None] * (idx + 1.0)[None, :])
    cdec = np.exp(log_gamma * CHUNK)
    bc = lambda a: jnp.asarray(np.broadcast_to(a[..., None], a.shape + (HEAD,)).astype(np.float32))
    return jnp.asarray(intra.astype(np.float32)), bc(kdec), bc(qdec), bc(cdec[:, None])


def _ret_core(cfg, q_raw, k_raw, v_raw, cos, sin, intra, kdec, qdec, cdec, p_ref):
    S = cfg.S
    NC = S // CHUNK
    q = _rope128(q_raw, cos, sin) * (HEAD ** -0.5)
    k = _rope128(k_raw, cos, sin)
    q3 = q.reshape(NC, CHUNK, HEAD)
    k3 = k.reshape(NC, CHUNK, HEAD)
    qb, kb = q3.astype(bf16), k3.astype(bf16)
    vb = v_raw.reshape(NC, CHUNK, HEAD).astype(bf16)
    sdb = (jnp.einsum('nid,njd->nij', qb, kb, preferred_element_type=f32) * intra[None]).astype(bf16)
    o_intra = jnp.einsum('nij,nje->nie', sdb, vb, preferred_element_type=f32)
    kdb = (k3 * kdec[None]).astype(bf16)
    kv = jnp.einsum('njd,nje->nde', kdb, vb, preferred_element_type=f32)
    p_ref[0] = jnp.zeros((HEAD, HEAD), f32)
    for n in range(1, NC):
        p_ref[n] = p_ref[n - 1] * cdec + kv[n - 1]
    pb = p_ref[...].astype(bf16)
    qdb = (q3 * qdec[None]).astype(bf16)
    o_inter = jnp.einsum('nid,nde->nie', qdb, pb, preferred_element_type=f32)
    o = (o_intra + o_inter).reshape(S, HEAD)
    return o, (qb, kb, vb, sdb, kdb, qdb, pb)


def _ret_specs(cfg):
    S = cfg.S
    hs = lambda off: pl.BlockSpec((S, HEAD), lambda h, _c=off // HEAD: (0, _c + h))
    full = pl.BlockSpec((S, HEAD), lambda h: (0, 0))
    consts = [pl.BlockSpec((None, CHUNK, CHUNK), lambda h: (h, 0, 0)), pl.BlockSpec((None, CHUNK, HEAD), lambda h: (h, 0, 0)),
              pl.BlockSpec((None, CHUNK, HEAD), lambda h: (h, 0, 0)), pl.BlockSpec((None, 1, HEAD), lambda h: (h, 0, 0))]
    gn = pl.BlockSpec((1, HEAD), lambda h: (0, h))
    return hs, full, consts, gn


def _ret_fwd(cfg, proj, gn, cos, sin, consts):
    S, NC = cfg.S, cfg.S // CHUNK
    hs, full, cspecs, gspec = _ret_specs(cfg)

    def body(q_ref, k_ref, v_ref, g_ref, gn_ref, cos_ref, sin_ref, intra, kdec, qdec, cdec, y_ref, p_ref):
        o, _ = _ret_core(cfg, q_ref[...], k_ref[...], v_ref[...], cos_ref[...], sin_ref[...],
                         intra[...], kdec[...], qdec[...], cdec[...], p_ref)
        mean = jnp.mean(o, axis=-1, keepdims=True)
        var = jnp.mean(jnp.square(o - mean), axis=-1, keepdims=True)
        z = ((o - mean) * lax.rsqrt(var + NORM_EPS)) * gn_ref[...]
        y_ref[...] = (z * _silu(g_ref[...])).astype(bf16)

    return pl.pallas_call(
        body, name="ret_fwd", grid=(cfg.H,),
        in_specs=[hs(0), hs(cfg.o_rk), hs(cfg.o_rv), hs(cfg.o_rg), gspec, full, full] + cspecs,
        out_specs=hs(0), out_shape=jax.ShapeDtypeStruct((S, cfg.RW), bf16),
        scratch_shapes=[pltpu.VMEM((NC, HEAD, HEAD), f32)],
        compiler_params=_cparams(("arbitrary",)),
    )(proj, proj, proj, proj, gn, cos, sin, *consts)


def _ret_bwd(cfg, proj, dy, gn, cos, sin, consts):
    S, NC = cfg.S, cfg.S // CHUNK
    hs, full, cspecs, gspec = _ret_specs(cfg)

    def body(q_ref, k_ref, v_ref, g_ref, dy_ref, gn_ref, cos_ref, sin_ref, intra_ref, kdec_ref, qdec_ref, cdec_ref,
             dq_ref, dk_ref, dv_ref, dg_ref, dgn_ref, p_ref, g_scr):
        cos, sin = cos_ref[...], sin_ref[...]
        intra, kdec, qdec, cdec = intra_ref[...], kdec_ref[...], qdec_ref[...], cdec_ref[...]
        o, (qb, kb, vb, sdb, kdb, qdb, pb) = _ret_core(cfg, q_ref[...], k_ref[...], v_ref[...], cos, sin,
                                                     intra, kdec, qdec, cdec, p_ref)
        gate, dy, gnv = g_ref[...], dy_ref[...], gn_ref[...]
        mean = jnp.mean(o, axis=-1, keepdims=True)
        rstd = lax.rsqrt(jnp.mean(jnp.square(o - mean), axis=-1, keepdims=True) + NORM_EPS)
        on = (o - mean) * rstd
        dz = dy * _silu(gate)
        dg_ref[...] = (dy * (on * gnv) * _dsilu(gate)).astype(bf16)
        dgn_ref[...] = jnp.sum(dz * on, axis=0, keepdims=True)
        don = dz * gnv
        do = rstd * (don - jnp.mean(don, axis=-1, keepdims=True) - on * jnp.mean(don * on, axis=-1, keepdims=True))
        dob = do.reshape(NC, CHUNK, HEAD).astype(bf16)
        dsb = (jnp.einsum('nie,nje->nij', dob, vb, preferred_element_type=f32) * intra[None]).astype(bf16)
        dv = jnp.einsum('nij,nie->nje', sdb, dob, preferred_element_type=f32)
        dq = jnp.einsum('nij,njd->nid', dsb, kb, preferred_element_type=f32)
        dk = jnp.einsum('nij,nid->njd', dsb, qb, preferred_element_type=f32)
        dq = dq + jnp.einsum('nie,nde->nid', dob, pb, preferred_element_type=f32) * qdec[None]
        dp = jnp.einsum('nid,nie->nde', qdb, dob, preferred_element_type=f32)
        g_scr[NC - 1] = jnp.zeros((HEAD, HEAD), f32)
        for n in range(NC - 2, -1, -1):
            g_scr[n] = dp[n + 1] + g_scr[n + 1] * cdec
        gb = g_scr[...].astype(bf16)
        dk = dk + jnp.einsum('nje,nde->njd', vb, gb, preferred_element_type=f32) * kdec[None]
        dv = dv + jnp.einsum('njd,nde->nje', kdb, gb, preferred_element_type=f32)
        dq_ref[...] = _rope128_t(dq.reshape(S, HEAD) * (HEAD ** -0.5), cos, sin).astype(bf16)
        dk_ref[...] = _rope128_t(dk.reshape(S, HEAD), cos, sin).astype(bf16)
        dv_ref[...] = dv.reshape(S, HEAD).astype(bf16)

    return pl.pallas_call(
        body, name="ret_bwd", grid=(cfg.H,),
        in_specs=[hs(0), hs(cfg.o_rk), hs(cfg.o_rv), hs(cfg.o_rg), hs(0), gspec, full, full] + cspecs,
        out_specs=(hs(0),) * 4 + (gspec,),
        out_shape=(jax.ShapeDtypeStruct((S, cfg.RW), bf16),) * 4 + (jax.ShapeDtypeStruct((1, cfg.RW), f32),),
        scratch_shapes=[pltpu.VMEM((NC, HEAD, HEAD), f32), pltpu.VMEM((NC, HEAD, HEAD), f32)],
        compiler_params=_cparams(("arbitrary",)),
    )(proj, proj, proj, proj, dy, gn, cos, sin, *consts)


def _expm1(x):
    small = x * (1.0 + x * (0.5 + x * (1.0 / 6.0 + x * (1.0 / 24.0 + x * (1.0 / 120.0)))))
    return jnp.where(jnp.abs(x) < 0.1, small, jnp.exp(x) - 1.0)


def _softplus(z):
    return jnp.maximum(z, 0.0) + jnp.log1p(jnp.exp(-jnp.abs(z)))


def _lru_conv(cfg, x_ref, halo_ref, cw, scr, first):
    TR = cfg.TR
    scr[0:SUBLANES, :] = jnp.where(first, 0.0, halo_ref[...])
    scr[SUBLANES:SUBLANES + TR, :] = x_ref[...]
    xc = scr[pl.ds(SUBLANES - (CONV - 1), TR), :] * cw[0:1, :]
    for j in range(1, CONV):
        xc = xc + scr[pl.ds(SUBLANES - (CONV - 1) + j, TR), :] * cw[j:j + 1, :]
    return xc


def _lru_pre(cfg, xc, wa_ref, wx_ref, ba, bx):
    xb = xc.astype(bf16)
    pa = jnp.concatenate([jnp.dot(xb[:, n * HEAD:(n + 1) * HEAD], wa_ref[n].astype(bf16), preferred_element_type=f32)
                          for n in range(cfg.NB)], axis=1) + ba
    px = jnp.concatenate([jnp.dot(xb[:, n * HEAD:(n + 1) * HEAD], wx_ref[n].astype(bf16), preferred_element_type=f32)
                          for n in range(cfg.NB)], axis=1) + bx
    return pa, px


def _lru_ab(pa, px, xc, lam):
    r, i = _sigmoid(pa), _sigmoid(px)
    log_a = (-LRU_C * r) * _softplus(-lam)
    a = jnp.exp(log_a)
    b = jnp.sqrt(-_expm1(2.0 * log_a)) * (i * xc)
    return a, b


def _lru_halo_specs(cfg, off, W):
    TR, S = cfg.TR, cfg.S
    nb = TR // SUBLANES
    cb = off // W
    main = pl.BlockSpec((TR, W), lambda i: (i, cb))
    prev = pl.BlockSpec((SUBLANES, W), lambda i: (jnp.maximum(i * nb - 1, 0), cb))
    nxt = pl.BlockSpec((SUBLANES, W), lambda i: (jnp.minimum((i + 1) * nb, S // SUBLANES - 1), cb))
    return main, prev, nxt


def _lru_gates(cfg, proj, cw, cb, wa, ba, wx, bx, lam):
    S, W, TR, NB = cfg.S, cfg.LW, cfg.TR, cfg.NB
    assert cfg.o_lx % W == 0
    main, prev, _ = _lru_halo_specs(cfg, cfg.o_lx, W)
    wspec = pl.BlockSpec((NB, HEAD, HEAD), lambda i: (0, 0, 0))

    def body(x_ref, halo_ref, cw_ref, cb_ref, wa_ref, ba_ref, wx_ref, bx_ref, lam_ref, a_ref, b_ref, scr):
        xc = _lru_conv(cfg, x_ref, halo_ref, cw_ref[...], scr, pl.program_id(0) == 0) + cb_ref[...]
        pa, px = _lru_pre(cfg, xc, wa_ref, wx_ref, ba_ref[...], bx_ref[...])
        a, b = _lru_ab(pa, px, xc, lam_ref[...])
        a_ref[...] = a
        b_ref[...] = b

    return pl.pallas_call(
        body, name="lru_gates", grid=(S // TR,),
        in_specs=[main, prev, pl.BlockSpec((CONV, W), lambda i: (0, 0)), _row(W), wspec, _row(W), wspec, _row(W), _row(W)],
        out_specs=(_slab(TR, W, 0),) * 2, out_shape=(jax.ShapeDtypeStruct((S, W), f32),) * 2,
        scratch_shapes=[pltpu.VMEM((TR + SUBLANES, W), f32)],
        compiler_params=_cparams(("parallel",)),
    )(proj, proj, cw, cb, wa, ba, wx, bx, lam)


def _lru_lane_block(cfg):
    return 256 if cfg.LW % 256 == 0 else LANES


def _lru_scan_fwd(cfg, proj, a, b):
    S, W = cfg.S, cfg.LW
    LB = _lru_lane_block(cfg)
    assert cfg.o_lg % LB == 0
    col = lambda off: pl.BlockSpec((S, LB), lambda j, _c=off // LB: (0, _c + j))

    def body(a_ref, b_ref, g_ref, h_ref, y_ref):
        def blk(t, h):
            r0 = pl.multiple_of(t * SUBLANES, SUBLANES)
            at, bt = a_ref[pl.ds(r0, SUBLANES), :], b_ref[pl.ds(r0, SUBLANES), :]
            rows = []
            for j in range(SUBLANES):
                h = at[j:j + 1, :] * h + bt[j:j + 1, :]
                rows.append(h)
            h_ref[pl.ds(r0, SUBLANES), :] = jnp.concatenate(rows, axis=0)
            return h

        lax.fori_loop(0, S // SUBLANES, blk, jnp.zeros((1, LB), f32))
        y_ref[...] = (h_ref[...] * _silu(g_ref[...])).astype(bf16)

    return pl.pallas_call(
        body, name="lru_scan_fwd", grid=(W // LB,),
        in_specs=[col(0), col(0), col(cfg.o_lg)],
        out_specs=(col(0), col(0)),
        out_shape=(jax.ShapeDtypeStruct((S, W), f32), jax.ShapeDtypeStruct((S, W), bf16)),
        compiler_params=_cparams(("parallel",)),
    )(a, b, proj)


def _lru_scan_bwd(cfg, proj, a, h, dy):
    S, W = cfg.S, cfg.LW
    LB = _lru_lane_block(cfg)
    col = lambda off: pl.BlockSpec((S, LB), lambda j, _c=off // LB: (0, _c + j))

    def body(a_ref, h_ref, dy_ref, g_ref, da_ref, db_ref, dg_ref):
        gate, dy = g_ref[...], dy_ref[...]
        dg_ref[...] = (dy * h_ref[...] * _dsilu(gate)).astype(bf16)
        da_ref[...] = dy * _silu(gate)

        def blk(t, carry):
            dh_next, a_next = carry
            r0 = pl.multiple_of((S // SUBLANES - 1 - t) * SUBLANES, SUBLANES)
            at, ct = a_ref[pl.ds(r0, SUBLANES), :], da_ref[pl.ds(r0, SUBLANES), :]
            rows = [None] * SUBLANES
            for j in range(SUBLANES - 1, -1, -1):
                dh_next = ct[j:j + 1, :] + a_next * dh_next
                a_next = at[j:j + 1, :]
                rows[j] = dh_next
            db_ref[pl.ds(r0, SUBLANES), :] = jnp.concatenate(rows, axis=0)
            return dh_next, a_next

        z = jnp.zeros((1, LB), f32)
        lax.fori_loop(0, S // SUBLANES, blk, (z, z))
        row = lax.broadcasted_iota(jnp.int32, (S, LB), 0)
        hprev = jnp.where(row == 0, 0.0, pltpu.roll(h_ref[...], 1, axis=0))
        da_ref[...] = db_ref[...] * hprev

    return pl.pallas_call(
        body, name="lru_scan_bwd", grid=(W // LB,),
        in_specs=[col(0), col(0), col(0), col(cfg.o_lg)],
        out_specs=(col(0),) * 3,
        out_shape=(jax.ShapeDtypeStruct((S, W), f32),) * 2 + (jax.ShapeDtypeStruct((S, W), bf16),),
        compiler_params=_cparams(("parallel",)),
    )(a, h, dy, proj)


def _lru_gates_bwd(cfg, proj, da, db, cw, cb, wa, ba, wx, bx, lam):
    S, W, TR, NB = cfg.S, cfg.LW, cfg.TR, cfg.NB
    main, prev, _ = _lru_halo_specs(cfg, cfg.o_lx, W)
    wspec = pl.BlockSpec((NB, HEAD, HEAD), lambda i: (0, 0, 0))

    def body(x_ref, halo_ref, da_ref, db_ref, cw_ref, cb_ref, wa_ref, ba_ref, wx_ref, bx_ref, lam_ref,
             dxc_ref, dwa_ref, dwx_ref, sum_ref, scr):
        i = pl.program_id(0)
        lam = lam_ref[...]
        xc = _lru_conv(cfg, x_ref, halo_ref, cw_ref[...], scr, i == 0) + cb_ref[...]
        pa, px = _lru_pre(cfg, xc, wa_ref, wx_ref, ba_ref[...], bx_ref[...])
        _, vjp = jax.vjp(_lru_ab, pa, px, xc, lam)
        dpa, dpx, dxc, dlam = vjp((da_ref[...], db_ref[...]))
        xb, dpab, dpxb = xc.astype(bf16), dpa.astype(bf16), dpx.astype(bf16)
        nt = (((1,), (1,)), ((), ()))
        tn = (((0,), (0,)), ((), ()))
        back = []
        dwa, dwx = [], []
        for n in range(NB):
            sl = slice(n * HEAD, (n + 1) * HEAD)
            back.append(lax.dot_general(dpab[:, sl], wa_ref[n].astype(bf16), nt, preferred_element_type=f32)
                        + lax.dot_general(dpxb[:, sl], wx_ref[n].astype(bf16), nt, preferred_element_type=f32))
            dwa.append(lax.dot_general(xb[:, sl], dpab[:, sl], tn, preferred_element_type=f32))
            dwx.append(lax.dot_general(xb[:, sl], dpxb[:, sl], tn, preferred_element_type=f32))
        dxc_ref[...] = dxc + jnp.concatenate(back, axis=1)
        part = jnp.concatenate([jnp.sum(dpa, axis=0, keepdims=True), jnp.sum(dpx, axis=0, keepdims=True), dlam,
                                jnp.zeros((SUBLANES - 3, W), f32)], axis=0)

        @pl.when(i == 0)
        def _():
            sum_ref[...] = part
            for n in range(NB):
                dwa_ref[n] = dwa[n]
                dwx_ref[n] = dwx[n]

        @pl.when(i > 0)
        def _():
            sum_ref[...] += part
            for n in range(NB):
                dwa_ref[n] += dwa[n]
                dwx_ref[n] += dwx[n]

    return pl.pallas_call(
        body, name="lru_gates_bwd", grid=(S // TR,),
        in_specs=[main, prev, _slab(TR, W, 0), _slab(TR, W, 0), pl.BlockSpec((CONV, W), lambda i: (0, 0)), _row(W),
                  wspec, _row(W), wspec, _row(W), _row(W)],
        out_specs=(_slab(TR, W, 0), wspec, wspec, pl.BlockSpec((SUBLANES, W), lambda i: (0, 0))),
        out_shape=(jax.ShapeDtypeStruct((S, W), f32), jax.ShapeDtypeStruct((NB, HEAD, HEAD), f32),
                   jax.ShapeDtypeStruct((NB, HEAD, HEAD), f32), jax.ShapeDtypeStruct((SUBLANES, W), f32)),
        scratch_shapes=[pltpu.VMEM((TR + SUBLANES, W), f32)],
        compiler_params=_cparams(("arbitrary",)),
    )(proj, proj, da, db, cw, cb, wa, ba, wx, bx, lam)


def _lru_conv_bwd(cfg, proj, dxc, cw):
    S, W, TR = cfg.S, cfg.LW, cfg.TR
    main, prev, _ = _lru_halo_specs(cfg, cfg.o_lx, W)
    dmain, _, dnext = _lru_halo_specs(cfg, 0, W)

    def body(x_ref, xhalo_ref, d_ref, dhalo_ref, cw_ref, dx_ref, sum_ref, xs, ds):
        i = pl.program_id(0)
        cw = cw_ref[...]
        d = d_ref[...]
        xs[0:SUBLANES, :] = jnp.where(i == 0, 0.0, xhalo_ref[...])
        xs[SUBLANES:SUBLANES + TR, :] = x_ref[...]
        ds[0:TR, :] = d
        ds[TR:TR + SUBLANES, :] = jnp.where(i == pl.num_programs(0) - 1, 0.0, dhalo_ref[...])
        dx = ds[pl.ds(CONV - 1, TR), :] * cw[0:1, :]
        parts = [jnp.sum(d * xs[pl.ds(SUBLANES - (CONV - 1), TR), :], axis=0, keepdims=True)]
        for j in range(1, CONV):
            dx = dx + ds[pl.ds(CONV - 1 - j, TR), :] * cw[j:j + 1, :]
            parts.append(jnp.sum(d * xs[pl.ds(SUBLANES - (CONV - 1) + j, TR), :], axis=0, keepdims=True))
        dx_ref[...] = dx.astype(bf16)
        part = jnp.concatenate(parts + [jnp.sum(d, axis=0, keepdims=True), jnp.zeros((SUBLANES - CONV - 1, W), f32)], axis=0)

        @pl.when(i == 0)
        def _():
            sum_ref[...] = part

        @pl.when(i > 0)
        def _():
            sum_ref[...] += part

    return pl.pallas_call(
        body, name="lru_conv_bwd", grid=(S // TR,),
        in_specs=[main, prev, dmain, dnext, pl.BlockSpec((CONV, W), lambda i: (0, 0))],
        out_specs=(_slab(TR, W, 0), pl.BlockSpec((SUBLANES, W), lambda i: (0, 0))),
        out_shape=(jax.ShapeDtypeStruct((S, W), bf16), jax.ShapeDtypeStruct((SUBLANES, W), f32)),
        scratch_shapes=[pltpu.VMEM((TR + SUBLANES, W), f32), pltpu.VMEM((TR + SUBLANES, W), f32)],
        compiler_params=_cparams(("arbitrary",)),
    )(proj, proj, dxc, dxc, cw)


def _rms(x, g):
    r = lax.rsqrt(jnp.mean(x * x, axis=-1, keepdims=True) + NORM_EPS)
    return (x * r) * g, r


def _mla_norm(cfg, proj, qg, kg):
    S, TR = cfg.S, cfg.TR

    def body(q_ref, k_ref, qg_ref, kg_ref, qn_ref, kn_ref):
        qn_ref[...] = _rms(q_ref[...], qg_ref[...])[0].astype(bf16)
        kn_ref[...] = _rms(k_ref[...], kg_ref[...])[0].astype(bf16)

    return pl.pallas_call(
        body, name="mla_norm", grid=(S // TR,),
        in_specs=[_slab(TR, cfg.QL, cfg.o_mq), _slab(TR, cfg.KL, cfg.o_mkv), _row(cfg.QL), _row(cfg.KL)],
        out_specs=(_slab(TR, cfg.QL, 0), _slab(TR, cfg.KL, 0)),
        out_shape=(jax.ShapeDtypeStruct((S, cfg.QL), bf16), jax.ShapeDtypeStruct((S, cfg.KL), bf16)),
        compiler_params=_cparams(("parallel",)),
    )(proj, proj, qg, kg)


def _mla_norm_bwd(cfg, proj, dqn, dkn, qg, kg):
    S, TR = cfg.S, cfg.TR

    def one(x, g, dn):
        r = lax.rsqrt(jnp.mean(x * x, axis=-1, keepdims=True) + NORM_EPS)
        xn = x * r
        dxn = dn * g
        dx = r * (dxn - xn * jnp.mean(dxn * xn, axis=-1, keepdims=True))
        return dx, jnp.sum(dn * xn, axis=0, keepdims=True)

    def body(q_ref, k_ref, dq_ref, dk_ref, qg_ref, kg_ref, dmq_ref, dmk_ref, sq_ref, sk_ref):
        i = pl.program_id(0)
        dq, gq = one(q_ref[...], qg_ref[...], dq_ref[...])
        dk, gk = one(k_ref[...], kg_ref[...], dk_ref[...])
        dmq_ref[...] = dq.astype(bf16)
        dmk_ref[...] = dk.astype(bf16)
        pq = jnp.concatenate([gq, jnp.zeros((SUBLANES - 1, cfg.QL), f32)], axis=0)
        pk = jnp.concatenate([gk, jnp.zeros((SUBLANES - 1, cfg.KL), f32)], axis=0)

        @pl.when(i == 0)
        def _():
            sq_ref[...] = pq
            sk_ref[...] = pk

        @pl.when(i > 0)
        def _():
            sq_ref[...] += pq
            sk_ref[...] += pk

    return pl.pallas_call(
        body, name="mla_norm_bwd", grid=(S // TR,),
        in_specs=[_slab(TR, cfg.QL, cfg.o_mq), _slab(TR, cfg.KL, cfg.o_mkv), _slab(TR, cfg.QL, 0), _slab(TR, cfg.KL, 0),
                  _row(cfg.QL), _row(cfg.KL)],
        out_specs=(_slab(TR, cfg.QL, 0), _slab(TR, cfg.KL, 0), pl.BlockSpec((SUBLANES, cfg.QL), lambda i: (0, 0)),
                   pl.BlockSpec((SUBLANES, cfg.KL), lambda i: (0, 0))),
        out_shape=(jax.ShapeDtypeStruct((S, cfg.QL), bf16), jax.ShapeDtypeStruct((S, cfg.KL), bf16),
                   jax.ShapeDtypeStruct((SUBLANES, cfg.QL), f32), jax.ShapeDtypeStruct((SUBLANES, cfg.KL), f32)),
        compiler_params=_cparams(("arbitrary",)),
    )(proj, proj, dqn, dkn, qg, kg)


def _mla_pack(cfg, proj, q, kv, cq, sq, ck, sk):
    S, TR, MH = cfg.S, cfg.TR, cfg.MH
    NW, RWD = MH * HEAD, MH * ROPE

    def body(q_ref, kv_ref, kr_ref, cq_ref, sq_ref, ck_ref, sk_ref, qo_ref, ko_ref, vo_ref):
        q, kv = q_ref[...], kv_ref[...]
        qr = _rope64(q[:, NW:], cq_ref[...], sq_ref[...])
        kr = _rope64(kr_ref[...], ck_ref[...], sk_ref[...]).astype(bf16)
        lane = lax.broadcasted_iota(jnp.int32, (TR, HEAD), 1)
        for h in range(MH):
            grp = qr[:, (h // 2) * HEAD:(h // 2 + 1) * HEAD]
            if h % 2:
                grp = pltpu.roll(grp, 64, axis=1)
            qo_ref[h] = jnp.concatenate([q[:, h * HEAD:(h + 1) * HEAD], jnp.where(lane < ROPE, grp, 0.0)], axis=1).astype(bf16)
            ko_ref[h] = jnp.concatenate([kv[:, 2 * h * HEAD:(2 * h + 1) * HEAD].astype(bf16), kr], axis=1)
            vo_ref[h] = kv[:, (2 * h + 1) * HEAD:(2 * h + 2) * HEAD].astype(bf16)

    hspec = lambda w: pl.BlockSpec((MH, TR, w), lambda i: (0, i, 0))
    return pl.pallas_call(
        body, name="mla_pack", grid=(S // TR,),
        in_specs=[_slab(TR, cfg.QW, 0), _slab(TR, cfg.KVW, 0), _slab(TR, HEAD, cfg.o_mkr),
                  _slab(TR, RWD, 0), _slab(TR, RWD, 0), _slab(TR, HEAD, 0), _slab(TR, HEAD, 0)],
        out_specs=(hspec(2 * HEAD), hspec(2 * HEAD), hspec(HEAD)),
        out_shape=(jax.ShapeDtypeStruct((MH, S, 2 * HEAD), bf16), jax.ShapeDtypeStruct((MH, S, 2 * HEAD), bf16),
                   jax.ShapeDtypeStruct((MH, S, HEAD), bf16)),
        compiler_params=_cparams(("parallel",)),
    )(q, kv, proj, cq, sq, ck, sk)


def _mla_unpack_bwd(cfg, dq3, dk3, dv3, cq, sq, ck, sk):
    S, TR, MH = cfg.S, cfg.TR, cfg.MH
    RWD = MH * ROPE

    def body(dq_ref, dk_ref, dv_ref, cq_ref, sq_ref, ck_ref, sk_ref, q_ref, kv_ref, kr_ref):
        lane = lax.broadcasted_iota(jnp.int32, (TR, HEAD), 1)
        nope, ropes, kvs = [], [], []
        dkr = jnp.zeros((TR, HEAD), f32)
        for h in range(MH):
            dq = dq_ref[h]
            nope.append(dq[:, :HEAD])
            part = jnp.where(lane < ROPE, dq[:, HEAD:], 0.0)
            if h % 2:
                ropes[-1] = ropes[-1] + pltpu.roll(part, 64, axis=1)
            else:
                ropes.append(part)
            dk = dk_ref[h]
            kvs += [dk[:, :HEAD], dv_ref[h]]
            dkr = dkr + dk[:, HEAD:]
        dqr = _rope64_t(jnp.concatenate(ropes, axis=1), cq_ref[...], sq_ref[...])
        q_ref[...] = jnp.concatenate(nope + [dqr], axis=1).astype(bf16)
        kv_ref[...] = jnp.concatenate(kvs, axis=1).astype(bf16)
        dkr = jnp.where(lane < ROPE, dkr, 0.0)
        kr_ref[...] = _rope64_t(dkr, ck_ref[...], sk_ref[...]).astype(bf16)

    hspec = lambda w: pl.BlockSpec((MH, TR, w), lambda i: (0, i, 0))
    return pl.pallas_call(
        body, name="mla_unpack_bwd", grid=(S // TR,),
        in_specs=[hspec(2 * HEAD), hspec(2 * HEAD), hspec(HEAD), _slab(TR, RWD, 0), _slab(TR, RWD, 0),
                  _slab(TR, HEAD, 0), _slab(TR, HEAD, 0)],
        out_specs=(_slab(TR, cfg.QW, 0), _slab(TR, cfg.KVW, 0), _slab(TR, HEAD, 0)),
        out_shape=(jax.ShapeDtypeStruct((S, cfg.QW), bf16), jax.ShapeDtypeStruct((S, cfg.KVW), bf16),
                   jax.ShapeDtypeStruct((S, HEAD), bf16)),
        compiler_params=_cparams(("parallel",)),
    )(dq3, dk3, dv3, cq, sq, ck, sk)


def _mla_probs(cfg, q, k, i):
    TQ, S = cfg.TQ, cfg.S
    nt = (((1,), (1,)), ((), ()))
    s = lax.dot_general(q, k, nt, preferred_element_type=f32) * ((HEAD + ROPE) ** -0.5)
    qc = (i * TQ + lax.broadcasted_iota(jnp.int32, (TQ, S), 0)) // CHUNK
    kc = lax.broadcasted_iota(jnp.int32, (TQ, S), 1) // CHUNK
    s = jnp.where(kc <= qc, s, -1e30)
    m = jnp.max(s, axis=-1, keepdims=True)
    e = jnp.exp(s - m)
    return e / jnp.sum(e, axis=-1, keepdims=True)


def _mla_attn_specs(cfg):
    S, TQ = cfg.S, cfg.TQ
    qs = lambda w: pl.BlockSpec((None, TQ, w), lambda h, i: (h, i, 0))
    ks = lambda w: pl.BlockSpec((None, S, w), lambda h, i: (h, 0, 0))
    hs = lambda off: pl.BlockSpec((TQ, HEAD), lambda h, i, _c=off // HEAD: (i, _c + h))
    return qs, ks, hs


def _mla_attn_fwd(cfg, proj, q3, k3, v3):
    S, TQ, MH = cfg.S, cfg.TQ, cfg.MH
    qs, ks, hs = _mla_attn_specs(cfg)

    def body(q_ref, k_ref, v_ref, g_ref, o_ref, y_ref):
        p = _mla_probs(cfg, q_ref[...], k_ref[...], pl.program_id(1))
        o = jnp.dot(p.astype(bf16), v_ref[...], preferred_element_type=f32)
        o_ref[...] = o
        y_ref[...] = (o * _silu(g_ref[...])).astype(bf16)

    return pl.pallas_call(
        body, name="mla_attn_fwd", grid=(MH, S // TQ),
        in_specs=[qs(2 * HEAD), ks(2 * HEAD), ks(HEAD), hs(cfg.o_mg)],
        out_specs=(hs(0), hs(0)),
        out_shape=(jax.ShapeDtypeStruct((S, cfg.MW), f32), jax.ShapeDtypeStruct((S, cfg.MW), bf16)),
        compiler_params=_cparams(("parallel", "parallel")),
    )(q3, k3, v3, proj)


def _mla_attn_bwd(cfg, proj, q3, k3, v3, o, dy):
    S, TQ, MH = cfg.S, cfg.TQ, cfg.MH
    qs, ks, hs = _mla_attn_specs(cfg)

    def body(q_ref, k_ref, v_ref, g_ref, o_ref, dy_ref, dq_ref, dk_ref, dv_ref, dg_ref):
        i = pl.program_id(1)
        q, k, v = q_ref[...], k_ref[...], v_ref[...]
        gate, dy, o = g_ref[...], dy_ref[...], o_ref[...]
        dg_ref[...] = (dy * o * _dsilu(gate)).astype(bf16)
        dob = (dy * _silu(gate)).astype(bf16)
        p = _mla_probs(cfg, q, k, i)
        nt = (((1,), (1,)), ((), ()))
        tn = (((0,), (0,)), ((), ()))
        dv = lax.dot_general(p.astype(bf16), dob, tn, preferred_element_type=f32)
        dp = lax.dot_general(dob, v, nt, preferred_element_type=f32)
        ds = (p * (dp - jnp.sum(dp * p, axis=-1, keepdims=True)) * ((HEAD + ROPE) ** -0.5)).astype(bf16)
        dq_ref[...] = jnp.dot(ds, k, preferred_element_type=f32)
        dk = lax.dot_general(ds, q, tn, preferred_element_type=f32)

        @pl.when(i == 0)
        def _():
            dk_ref[...] = dk
            dv_ref[...] = dv

        @pl.when(i > 0)
        def _():
            dk_ref[...] += dk
            dv_ref[...] += dv

    return pl.pallas_call(
        body, name="mla_attn_bwd", grid=(MH, S // TQ),
        in_specs=[qs(2 * HEAD), ks(2 * HEAD), ks(HEAD), hs(cfg.o_mg), hs(0), hs(0)],
        out_specs=(qs(2 * HEAD), ks(2 * HEAD), ks(HEAD), hs(0)),
        out_shape=(jax.ShapeDtypeStruct((MH, S, 2 * HEAD), f32), jax.ShapeDtypeStruct((MH, S, 2 * HEAD), f32),
                   jax.ShapeDtypeStruct((MH, S, HEAD), f32), jax.ShapeDtypeStruct((S, cfg.MW), bf16)),
        compiler_params=_cparams(("parallel", "arbitrary")),
    )(q3, k3, v3, proj, o, dy)


def _pick_rows(R, bytes_per_row):
    if R * bytes_per_row <= MM_BUDGET:
        return R
    best = None
    for t in range(16, R, 16):
        if R % t == 0 and t * bytes_per_row <= MM_BUDGET:
            best = t
    assert best is not None, (R, bytes_per_row)
    return best


def _adamw(w, g, m, v, name="adamw"):
    R, C = w.shape
    tr = _pick_rows(R, C * 4 * 7 * 2)
    c1 =1.0 - ADAM_B1 ** ADAM_STEP
    c2 = 1.0 - ADAM_B2 ** ADAM_STEP

    def body(w_ref, g_ref, m_ref, v_ref, d_ref, mo_ref, vo_ref):
        g = g_ref[...]
        m = ADAM_B1 * m_ref[...] + (1.0 - ADAM_B1) * g
        v = ADAM_B2 * v_ref[...] + (1.0 - ADAM_B2) * jnp.square(g)
        d_ref[...] = -ADAM_LR * ((m / c1) / (jnp.sqrt(v / c2) + ADAM_EPS) + ADAM_WD * w_ref[...])
        mo_ref[...] = m
        vo_ref[...] = v

    spec = pl.BlockSpec((tr, C), lambda i: (i, 0))
    return pl.pallas_call(
        body, name=name, grid=(R // tr,), in_specs=[spec] * 4, out_specs=(spec,) * 3,
        out_shape=(jax.ShapeDtypeStruct((R, C), f32),) * 3,
        compiler_params=_cparams(("parallel",)),
    )(w, g, m, v)


def _adamw_layer(l, w, g, m, v, prev, name):
    L, R, C = w.shape
    tr = _pick_rows(R, C * 4 * 8 * 2)
    c1 = 1.0 - ADAM_B1 ** ADAM_STEP
    c2 = 1.0 - ADAM_B2 ** ADAM_STEP
    n_prev = 0 if prev is None else 4

    def body(w_ref, g_ref, m_ref, v_ref, *rest):
        go_ref, d_ref, mo_ref, vo_ref = rest[n_prev:]
        g = g_ref[...]
        m = ADAM_B1 * m_ref[...] + (1.0 - ADAM_B1) * g
        v = ADAM_B2 * v_ref[...] + (1.0 - ADAM_B2) * jnp.square(g)
        go_ref[...] = g
        d_ref[...] = -ADAM_LR * ((m / c1) / (jnp.sqrt(v / c2) + ADAM_EPS) + ADAM_WD * w_ref[...])
        mo_ref[...] = m
        vo_ref[...] = v

    lay = pl.BlockSpec((None, tr, C), lambda i: (l, i, 0))
    return pl.pallas_call(
        body, name=name, grid=(R // tr,),
        in_specs=[lay, pl.BlockSpec((tr, C), lambda i: (i, 0)), lay, lay] + _hbm_specs(n_prev),
        out_specs=(lay,) * 4, out_shape=(jax.ShapeDtypeStruct((L, R, C), f32),) * 4,
        input_output_aliases={4 + k: k for k in range(n_prev)},
        compiler_params=_cparams(("parallel",)),
    )(w, g, m, v, *(prev or ()))


def _sum_blocks(x, out_dtype, name):
    n, R, C = x.shape
    tr = _pick_rows(R, C * 4 * (n + 1) * 2)

    def body(x_ref, o_ref):
        acc = x_ref[0].astype(f32)
        for k in range(1, n):
            acc = acc + x_ref[k].astype(f32)
        o_ref[...] = acc.astype(o_ref.dtype)

    return pl.pallas_call(
        body, name=name, grid=(R // tr,),
        in_specs=[pl.BlockSpec((n, tr, C), lambda i: (0, i, 0))], out_specs=pl.BlockSpec((tr, C), lambda i: (i, 0)),
        out_shape=jax.ShapeDtypeStruct((R, C), out_dtype),
        compiler_params=_cparams(("parallel",)),
    )(x)


def _hbm_specs(n):
    return [pl.BlockSpec(memory_space=pl.ANY)] * n


def _allgather8(shards, name):
    na = len(shards)

    def body(*refs):
        x_refs, out_refs = refs[:na], refs[na:2 * na]
        send_sems, recv_sems, local_sems = refs[2 * na:]
        x, y, c = lax.axis_index("x"), lax.axis_index("y"), lax.axis_index("c")
        me, sibling = (x, y, c), (x, y, 1 - c)
        chips = [(1 - x, y), (x, 1 - y), (1 - x, 1 - y)]

        def rows(a, px, py, pc):
            m = shards[a].shape[0]
            return out_refs[a].at[pl.ds((4 * px + 2 * py + pc) * m, m), :]

        def copy(a, k, block, to, src=None):
            return pltpu.make_async_remote_copy(
                src_ref=rows(a, *block) if src is None else src, dst_ref=rows(a, *block),
                send_sem=send_sems.at[a, k], recv_sem=recv_sems.at[a, k], device_id=to, device_id_type=MESH)

        mine = [pltpu.make_async_copy(x_refs[a], rows(a, *me), local_sems.at[a]) for a in range(na)]
        for cp in mine:
            cp.start()
        first = []
        for a in range(na):
            first.append(copy(a, 0, me, sibling, src=x_refs[a]))
            first += [copy(a, 1 + j, me, (*chip, c), src=x_refs[a]) for j, chip in enumerate(chips)]
        for cp in first:
            cp.start()
        passed = []
        for j, chip in enumerate(chips):
            for a in range(na):
                copy(a, 1 + j, (*chip, c), me).wait_recv()
                passed.append(copy(a, 4 + j, (*chip, c), sibling))
                passed[-1].start()
        for a in range(na):
            copy(a, 0, sibling, me).wait_recv()
        for j, chip in enumerate(chips):
            for a in range(na):
                copy(a, 4 + j, (*chip, 1 - c), me).wait_recv()
        for cp in first + passed:
            cp.wait_send()
        for cp in mine:
            cp.wait()

    return pl.pallas_call(
        body, name=name,
        out_shape=[jax.ShapeDtypeStruct((N_DEV * s.shape[0], s.shape[1]), s.dtype) for s in shards],
        in_specs=_hbm_specs(na), out_specs=_hbm_specs(na),
        scratch_shapes=[pltpu.SemaphoreType.DMA((na, 7)), pltpu.SemaphoreType.DMA((na, 7)), pltpu.SemaphoreType.DMA((na,))],
    )(*shards)


def _send_sibling(arrays, name):
    na = len(arrays)

    def body(*refs):
        x_refs, out_refs = refs[:na], refs[na:2 * na]
        send_sems, recv_sems = refs[2 * na:]
        sibling = (lax.axis_index("x"), lax.axis_index("y"), 1 - lax.axis_index("c"))
        cps = [pltpu.make_async_remote_copy(src_ref=x_refs[a], dst_ref=out_refs[a], send_sem=send_sems.at[a],
                                            recv_sem=recv_sems.at[a], device_id=sibling, device_id_type=MESH)
               for a in range(na)]
        for cp in cps:
            cp.start()
        for cp in cps:
            cp.wait()

    return pl.pallas_call(
        body, name=name, out_shape=[jax.ShapeDtypeStruct(x.shape, x.dtype) for x in arrays],
        in_specs=_hbm_specs(na), out_specs=_hbm_specs(na),
        scratch_shapes=[pltpu.SemaphoreType.DMA((na,)), pltpu.SemaphoreType.DMA((na,))],
    )(*arrays)


def _share_halves(arrays, name):
    na = len(arrays)

    def body(*refs):
        x_refs, out_refs = refs[:na], refs[na:2 * na]
        send_sems, recv_sems, local_sems = refs[2 * na:]
        x, y, c = lax.axis_index("x"), lax.axis_index("y"), lax.axis_index("c")

        def half(a, h):
            hr = arrays[a].shape[0]
            return out_refs[a].at[pl.ds(h * hr, hr), :]

        mine = [pltpu.make_async_copy(x_refs[a], half(a, c), local_sems.at[a]) for a in range(na)]
        cps = [pltpu.make_async_remote_copy(src_ref=x_refs[a], dst_ref=half(a, c), send_sem=send_sems.at[a],
                                            recv_sem=recv_sems.at[a], device_id=(x, y, 1 - c), device_id_type=MESH)
               for a in range(na)]
        for cp in mine + cps:
            cp.start()
        for a in range(na):
            pltpu.make_async_remote_copy(src_ref=x_refs[a], dst_ref=half(a, 1 - c), send_sem=send_sems.at[a],
                                         recv_sem=recv_sems.at[a], device_id=(x, y, 1 - c), device_id_type=MESH).wait_recv()
        for cp in cps:
            cp.wait_send()
        for cp in mine:
            cp.wait()

    return pl.pallas_call(
        body, name=name, out_shape=[jax.ShapeDtypeStruct((2 * x.shape[0], x.shape[1]), x.dtype) for x in arrays],
        in_specs=_hbm_specs(na), out_specs=_hbm_specs(na),
        scratch_shapes=[pltpu.SemaphoreType.DMA((na,)), pltpu.SemaphoreType.DMA((na,)), pltpu.SemaphoreType.DMA((na,))],
    )(*arrays)


def _scatter_chips(arrays, name):
    na = len(arrays)

    def body(*refs):
        p_refs, out_refs = refs[:na], refs[na:2 * na]
        send_sems, recv_sems, local_sems = refs[2 * na:]
        x, y, c = lax.axis_index("x"), lax.axis_index("y"), lax.axis_index("c")
        mychip = 2 * x + y
        chips = [(1 - x, y), (x, 1 - y), (1 - x, 1 - y)]
        mine = [pltpu.make_async_copy(p_refs[a].at[mychip], out_refs[a].at[mychip], local_sems.at[a]) for a in range(na)]
        cps = [pltpu.make_async_remote_copy(src_ref=p_refs[a].at[2 * cx + cy], dst_ref=out_refs[a].at[mychip],
                                            send_sem=send_sems.at[a, j], recv_sem=recv_sems.at[a, j],
                                            device_id=(cx, cy, c), device_id_type=MESH)
               for j, (cx, cy) in enumerate(chips) for a in range(na)]
        for cp in mine + cps:
            cp.start()
        for j, (cx, cy) in enumerate(chips):
            for a in range(na):
                pltpu.make_async_remote_copy(src_ref=p_refs[a].at[mychip], dst_ref=out_refs[a].at[2 * cx + cy],
                                             send_sem=send_sems.at[a, j], recv_sem=recv_sems.at[a, j],
                                             device_id=(cx, cy, c), device_id_type=MESH).wait_recv()
        for cp in cps:
            cp.wait_send()
        for cp in mine:
            cp.wait()

    return pl.pallas_call(
        body, name=name, out_shape=[jax.ShapeDtypeStruct(p.shape, p.dtype) for p in arrays],
        in_specs=_hbm_specs(na), out_specs=_hbm_specs(na),
        scratch_shapes=[pltpu.SemaphoreType.DMA((na, 3)), pltpu.SemaphoreType.DMA((na, 3)), pltpu.SemaphoreType.DMA((na,))],
    )(*arrays)


def _add2(a, b, out_dtype, name):
    R, C = a.shape
    tr = _pick_rows(R, C * 4 * 3 * 2)

    def body(a_ref, b_ref, o_ref):
        o_ref[...] = (a_ref[...].astype(f32) + b_ref[...].astype(f32)).astype(o_ref.dtype)

    spec = pl.BlockSpec((tr, C), lambda i: (i, 0))
    return pl.pallas_call(body, name=name, grid=(R // tr,), in_specs=[spec, spec], out_specs=spec,
                          out_shape=jax.ShapeDtypeStruct((R, C), out_dtype), compiler_params=_cparams(("parallel",)))(a, b)


def _reduce_scatter(grads):
    c = lax.axis_index("c")
    shp = [g.shape[2:] for g in grads]
    keep = [lax.dynamic_index_in_dim(g, c, axis=1, keepdims=False).reshape(4 * hr, nc) for g, (hr, nc) in zip(grads, shp)]
    give = [lax.dynamic_index_in_dim(g, 1 - c, axis=1, keepdims=False).reshape(4 * hr, nc) for g, (hr, nc) in zip(grads, shp)]
    got = _send_sibling(give, "rs_pair")
    part = [_add2(k, g, bf16, "rs_add_pair").reshape(4, hr, nc) for k, g, (hr, nc) in zip(keep, got, shp)]
    slots = _scatter_chips(part, "rs_chips")
    mine = [_sum_blocks(s, f32, "rs_add_chips") for s in slots]
    return _share_halves(mine, "rs_halves")


def _big_weights(cfg):
    return (("w_in", cfg.D, cfg.IN_WIDTH, 1), ("mla_w_uq", cfg.QL, cfg.QW, 1), ("mla_w_ukv", cfg.KL, cfg.KVW, 1),
            ("w_branch", cfg.RW + cfg.LW + cfg.MW, cfg.D, 0), ("w_out", cfg.D, cfg.D, 0))


def _half_shapes(cfg):
    out = []
    for _, r, c, ax in _big_weights(cfg):
        out.append((r // 2, c // 4) if ax == 1 else (r // 8, c))
    return out


def _my_halves(cfg, shards, c):
    return [lax.dynamic_slice_in_dim(w, c * hr, hr, axis=0).astype(bf16) for w, (hr, nc) in zip(shards, _half_shapes(cfg))]


def _cols(blocks, a, b):
    nc = blocks.shape[2]
    out = []
    for q in range(blocks.shape[0]):
        lo, hi = max(a, q * nc), min(b, (q + 1) * nc)
        if lo < hi:
            out.append(blocks[q, :, lo - q * nc:hi - q * nc])
    return out


def _w_in_padded(cfg, blocks):
    k0 = cfg.o_mg
    parts = _cols(blocks, 0, k0) + _cols(blocks, k0 + ROPE, cfg.IN_WIDTH) + _cols(blocks, k0, k0 + ROPE)
    return jnp.concatenate(parts + [jnp.zeros((blocks.shape[1], cfg.NP - cfg.IN_WIDTH), blocks.dtype)], axis=1)


def _w_in_blocks(cfg, gp):
    k0, nc = cfg.o_mg, cfg.IN_WIDTH // 4

    def orig(a, b):
        segs = ((0, k0, 0), (k0, k0 + ROPE, cfg.o_mkr), (k0 + ROPE, cfg.IN_WIDTH, k0))
        out = []
        for s0, s1, p0 in segs:
            lo, hi = max(a, s0), min(b, s1)
            if lo < hi:
                out.append(gp[:, p0 + lo - s0:p0 + hi - s0])
        return out

    return jnp.stack([jnp.concatenate(orig(q * nc, (q + 1) * nc), axis=1) for q in range(4)])


def _uq_split(cfg, w):
    hw = HEAD + ROPE
    return jnp.concatenate([w[:, h * hw:h * hw + HEAD] for h in range(cfg.MH)]
                           + [w[:, h * hw + HEAD:(h + 1) * hw] for h in range(cfg.MH)], axis=1)


def _uq_join(cfg, g):
    n = cfg.MH * HEAD
    parts = []
    for h in range(cfg.MH):
        parts += [g[:, h * HEAD:(h + 1) * HEAD], g[:, n + h * ROPE:n + (h + 1) * ROPE]]
    return jnp.concatenate(parts, axis=1)


def _col_blocks(g):
    nc = g.shape[1] // 4
    return jnp.stack([g[:, q * nc:(q + 1) * nc] for q in range(4)])


def _row_pack(parts):
    rows = []
    for p in parts:
        r = p.reshape(-1, LANES)
        pad = -r.shape[0] % SUBLANES
        rows.append(jnp.concatenate([r, jnp.zeros((pad, LANES), r.dtype)], axis=0) if pad else r)
    return jnp.concatenate(rows, axis=0)


def _row_unpack(packed, like):
    out, off = [], 0
    for p in like:
        n = p.size // LANES
        out.append(packed[off:off + n].reshape(p.shape))
        off += -(-n // SUBLANES) * SUBLANES
    return out


def _prep_layer(cfg, full, small):
    w_in, w_uq, w_ukv, w_branch, w_out = full
    RW, LW = cfg.RW, cfg.LW
    P = dict(small)
    P["w_in"] = _w_in_padded(cfg, w_in.reshape(4, cfg.D, -1))
    P["w_uq"] = _uq_split(cfg, jnp.concatenate(list(w_uq.reshape(4, cfg.QL, -1)), axis=1))
    w_ukv = jnp.concatenate(list(w_ukv.reshape(4, cfg.KL, -1)), axis=1)
    P["w_ukv"] = w_ukv
    P["wb"] = (w_branch[:RW], w_branch[RW:RW + LW], w_branch[RW + LW:])
    P["w_out"] = w_out
    P["w_in_t"] = P["w_in"].T
    P["w_uq_t"] = P["w_uq"].T
    P["w_ukv_t"] = w_ukv.T
    P["wb_t"] = tuple(w.T for w in P["wb"])
    P["w_out_t"] = w_out.T
    return P


def _layer_fwd(cfg, x, mod, P, T):
    h = _prenorm_fwd(cfg, x, mod, P["norm_pre"])
    proj = _mm(h, P["w_in"], f32, "mm_proj")
    y_ret = _ret_fwd(cfg, proj, P["ret_gn"], T["cos_r"], T["sin_r"], T["ret_consts"])
    a, b = _lru_gates(cfg, proj, P["lru_conv_w"], P["lru_conv_b"], P["lru_wa"], P["lru_ba"], P["lru_wx"], P["lru_bx"],
                      P["lru_lambda"])
    hl, y_lru = _lru_scan_fwd(cfg, proj, a, b)
    qn, kn = _mla_norm(cfg, proj, P["mla_q_norm"], P["mla_kv_norm"])
    q = _mm(qn, P["w_uq"], f32, "mm_uq")
    kv = _mm(kn, P["w_ukv"], f32, "mm_ukv")
    q3, k3, v3 = _mla_pack(cfg, proj, q, kv, T["cos_q"], T["sin_q"], T["cos_k"], T["sin_k"])
    o, y_mla = _mla_attn_fwd(cfg, proj, q3, k3, v3)
    ys = (y_ret, y_lru, y_mla)
    us = tuple(_mm(yb, wb, f32, "mm_branch") for yb, wb in zip(ys, P["wb"]))
    merged = _merge_fwd(cfg, proj, *us)
    y = _mm(merged, P["w_out"], f32, "mm_out")
    out = _postnorm_fwd(cfg, x, y, mod, P["norm_post"])
    R = dict(x=x, h=h, proj=proj, ys=ys, a=a, hl=hl, qn=qn, kn=kn, q3=q3, k3=k3, v3=v3, o=o, us=us, merged=merged, y=y)
    return out, R


def _layer_bwd(cfg, dout, R, mod, P, T):
    proj = R["proj"]
    dy, s_post = _postnorm_bwd(cfg, dout, R["y"], mod, P["norm_post"])
    dmerged = _mm(dy, P["w_out_t"], f32, "mm_dmerged")
    g_out = _mm(R["merged"].T, dy, bf16, "mm_gw_out")
    du0, du1, du2, dlog = _merge_bwd(cfg, proj, dmerged, *R["us"])
    dus = (du0, du1, du2)
    dys = tuple(_mm(du, wt, f32, "mm_dbranch") for du, wt in zip(dus, P["wb_t"]))
    g_branch = jnp.concatenate([_mm(yb.T, du, bf16, "mm_gw_branch") for yb, du in zip(R["ys"], dus)], axis=0)
    drq, drk, drv, drg, dgn = _ret_bwd(cfg, proj, dys[0], P["ret_gn"], T["cos_r"], T["sin_r"], T["ret_consts"])
    da, db, dlg = _lru_scan_bwd(cfg, proj, R["a"], R["hl"], dys[1])
    dxc, dwa, dwx, s_lru = _lru_gates_bwd(cfg, proj, da, db, P["lru_conv_w"], P["lru_conv_b"], P["lru_wa"], P["lru_ba"],
                                          P["lru_wx"], P["lru_bx"], P["lru_lambda"])
    dlx, s_conv = _lru_conv_bwd(cfg, proj, dxc, P["lru_conv_w"])
    dq3, dk3, dv3, dmg = _mla_attn_bwd(cfg, proj, R["q3"], R["k3"], R["v3"], R["o"], dys[2])
    dq, dkv, dmkr = _mla_unpack_bwd(cfg, dq3, dk3, dv3, T["cos_q"], T["sin_q"], T["cos_k"], T["sin_k"])
    dqn = _mm(dq, P["w_uq_t"], f32, "mm_dqn")
    dkn = _mm(dkv, P["w_ukv_t"], f32, "mm_dkn")
    g_uq = _uq_join(cfg, _mm(R["qn"].T, dq, bf16, "mm_gw_uq"))
    g_ukv = _mm(R["kn"].T, dkv, bf16, "mm_gw_ukv")
    dmq, dmkv, s_q, s_k = _mla_norm_bwd(cfg, proj, dqn, dkn, P["mla_q_norm"], P["mla_kv_norm"])
    dproj = jnp.concatenate([drq, drk, drv, drg, dlx, dlg, dmq, dmkv, dmg, dlog, dmkr,
                             jnp.zeros((cfg.S, cfg.NP - cfg.o_mkr - HEAD), bf16)], axis=1)
    dh = _mm(dproj, P["w_in_t"], f32, "mm_dh")
    g_in = _w_in_blocks(cfg, _mm(R["h"].T, dproj, bf16, "mm_gw_in"))
    dx, s_pre = _prenorm_bwd(cfg, R["x"], dh, dout, mod, P["norm_pre"])
    big = [g_in, _col_blocks(g_uq), _col_blocks(g_ukv), g_branch, g_out]
    big = [g.reshape(4, 2, hr, nc) for g, (hr, nc) in zip(big, _half_shapes(cfg))]
    small = dict(norm_pre=s_pre[2:3], norm_post=s_post[1:2], ret_gn=dgn, lru_conv_w=s_conv[0:CONV], lru_conv_b=s_conv[CONV:CONV + 1],
                 lru_wa=dwa, lru_ba=s_lru[0:1], lru_wx=dwx, lru_bx=s_lru[1:2], lru_lambda=s_lru[2:3],
                 mla_q_norm=s_q[0:1], mla_kv_norm=s_k[0:1])
    dmod = jnp.concatenate([s_pre[0:1], s_pre[1:2], s_post[0:1]], axis=1)
    return dx, big, small, dmod


_SMALL = ("norm_pre", "norm_post", "ret_gn", "lru_conv_w", "lru_conv_b", "lru_wa", "lru_ba", "lru_wx", "lru_bx", "lru_lambda",
          "mla_q_norm", "mla_kv_norm")
_WEIGHTS = ("ada_w", "ada_b", "norm_pre", "norm_post", "w_in", "ret_gn", "lru_conv_w", "lru_conv_b", "lru_wa", "lru_ba", "lru_wx",
            "lru_bx", "lru_lambda", "mla_q_norm", "mla_w_uq", "mla_kv_norm", "mla_w_ukv", "w_branch", "w_out")


def _step(cfg, x, c, positions, W, target, M1, V1):
    L, D = cfg.L, cfg.D
    xi, yi, ci = lax.axis_index("x"), lax.axis_index("y"), lax.axis_index("c")
    chip = 2 * xi + yi
    me = 2 * chip + ci

    c8 = jnp.concatenate([c, jnp.zeros((SUBLANES - 1, D), f32)], axis=0)
    c_all = _allgather8([c8], "gather_c")[0].reshape(N_DEV, SUBLANES, D)[:, 0]
    mod_sh, c_act = _ada_fwd(cfg, c_all, W["ada_w"])
    n_sh = mod_sh.shape[2]
    mod_half = lax.dynamic_slice_in_dim(mod_sh, ci * (n_sh // 2), n_sh // 2, axis=2).reshape(L * N_DEV, n_sh // 2)
    mod_all = _allgather8([mod_half], "gather_mod")[0].reshape(N_DEV, L, N_DEV, n_sh // 2)
    mod_all = mod_all.transpose(1, 2, 0, 3).reshape(L, N_DEV, 3 * D)
    mods = lax.dynamic_index_in_dim(mod_all, me, axis=1, keepdims=False) + W["ada_b"]

    (cos_r, sin_r), (cos_m, sin_m) = _rope_tables(cfg, positions)
    T = dict(cos_r=cos_r, sin_r=sin_r, cos_q=jnp.tile(cos_m, (1, cfg.MH)), sin_q=jnp.tile(sin_m, (1, cfg.MH)),
             cos_k=jnp.tile(cos_m, (1, 2)), sin_k=jnp.tile(sin_m, (1, 2)), ret_consts=_ret_consts(cfg))

    Ps, Rs = [], []
    act = x[0]
    for l in range(L):
        shards = [W[name][l] for name, *_ in _big_weights(cfg)]
        gathered = _allgather8(_my_halves(cfg, shards, ci), "gather_w")
        small = {k: (W[k][l] if W[k][l].ndim > 1 else W[k][l][None, :]) for k in _SMALL if k != "lru_conv_w"}
        P = _prep_layer(cfg, gathered, small)
        P["lru_conv_w"] = None
        Ps.append(P)
    cw_all = _allgather8([_pad_rows(W["lru_conv_w"].reshape(L * CONV, -1))], "gather_conv")[0]
    cw_rows = cw_all.shape[0] // N_DEV
    cw_all = cw_all.reshape(4, 2, cw_rows, -1)[:, 0, :L * CONV].transpose(1, 0, 2).reshape(L, CONV, cfg.LW)
    for l in range(L):
        Ps[l]["lru_conv_w"] = cw_all[l]
    for l in range(L):
        act, R = _layer_fwd(cfg, act, mods[l:l + 1], Ps[l], T)
        Rs.append(R)

    dact, lsum = _loss_head(cfg, act, target[0])
    loss = lax.psum(lsum[0, 0], ("x", "y", "c"))

    big_out = [None] * len(_big_weights(cfg))
    small_g = [None] * L
    dmods = [None] * L
    for l in range(L - 1, -1, -1):
        dact, grads, small_g[l], dmods[l] = _layer_bwd(cfg, dact, Rs[l], mods[l:l + 1], Ps[l], T)
        for i, (g, (name, *_)) in enumerate(zip(_reduce_scatter(grads), _big_weights(cfg))):
            big_out[i] = _adamw_layer(l, W[name], g, M1[name], V1[name], big_out[i], "adamw_" + name)

    dmod = jnp.concatenate(dmods, axis=0)
    parts = [dmod] + [small_g[l][k] for l in range(L) for k in _SMALL]
    packed = _row_pack(parts)
    allf = _allgather8([packed], "gather_small")[0].reshape(N_DEV, packed.shape[0], LANES)
    summed = _row_unpack(_sum_blocks(allf, f32, "sum_small"), parts)
    gsm = {k: jnp.stack([summed[1 + l * len(_SMALL) + i].reshape(W[k].shape[1:] if k != "lru_conv_w" else (CONV, cfg.LW))
                         for l in range(L)]) for i, k in enumerate(_SMALL)}
    ncw = cfg.LW // 4
    gsm["lru_conv_w"] = lax.dynamic_slice_in_dim(gsm["lru_conv_w"], chip * ncw, ncw, axis=2)
    gsm["ada_b"] = summed[0]
    dmod_all = allf[:, :dmod.size // LANES].reshape(N_DEV, L, 3 * D)
    dmod_sh = lax.dynamic_slice_in_dim(dmod_all, chip * n_sh, n_sh, axis=2).transpose(1, 0, 2)
    G = dict(gsm)
    G["ada_w"] = _ada_bwd(cfg, c_act.T, dmod_sh)
    delta, new_m, new_v = {}, {}, {}
    for i, (name, *_) in enumerate(_big_weights(cfg)):
        G[name], delta[name], new_m[name], new_v[name] = big_out[i]
    bigs = ("ada_w",) + tuple(name for name, *_ in _big_weights(cfg))
    shp = W["ada_w"].shape
    two = lambda a: a.reshape(-1, shp[-1])
    d, m_, v_ = _adamw(two(W["ada_w"]), two(G["ada_w"]), two(M1["ada_w"]), two(V1["ada_w"]), "adamw_ada_w")
    delta["ada_w"], new_m["ada_w"], new_v["ada_w"] = d.reshape(shp), m_.reshape(shp), v_.reshape(shp)
    smalls = [k for k in _WEIGHTS if k not in bigs]
    packs = [_row_pack([src[k] for k in smalls]) for src in (W, G, M1, V1)]
    outs = _adamw(*packs, "adamw_small")
    for dst, o in zip((delta, new_m, new_v), outs):
        for k, val in zip(smalls, _row_unpack(o, [W[k] for k in smalls])):
            dst[k] = val

    grad_x = dact[None]
    return (loss, grad_x, *[G[k] for k in _WEIGHTS], *[delta[k] for k in _WEIGHTS], *[new_m[k] for k in _WEIGHTS],
            *[new_v[k] for k in _WEIGHTS])


def _pad_rows(a):
    pad = -a.shape[0] % SUBLANES
    return jnp.concatenate([a, jnp.zeros((pad, a.shape[1]), a.dtype)], axis=0) if pad else a


def kernel(x, c, positions, ada_w, ada_b, norm_pre, norm_post, w_in, ret_gn, lru_conv_w, lru_conv_b, lru_wa, lru_ba, lru_wx, lru_bx, lru_lambda, mla_q_norm, mla_w_uq, mla_kv_norm, mla_w_ukv, w_branch, w_out, loss_target, m_ada_w, m_ada_b, m_norm_pre, m_norm_post, m_w_in, m_ret_gn, m_lru_conv_w, m_lru_conv_b, m_lru_wa, m_lru_ba, m_lru_wx, m_lru_bx, m_lru_lambda, m_mla_q_norm, m_mla_w_uq, m_mla_kv_norm, m_mla_w_ukv, m_w_branch, m_w_out, v_ada_w, v_ada_b, v_norm_pre, v_norm_post, v_w_in, v_ret_gn, v_lru_conv_w, v_lru_conv_b, v_lru_wa, v_lru_ba, v_lru_wx, v_lru_bx, v_lru_lambda, v_mla_q_norm, v_mla_w_uq, v_mla_kv_norm, v_mla_w_ukv, v_w_branch, v_w_out):
    W = dict(ada_w=ada_w, ada_b=ada_b, norm_pre=norm_pre, norm_post=norm_post, w_in=w_in, ret_gn=ret_gn, lru_conv_w=lru_conv_w,
             lru_conv_b=lru_conv_b, lru_wa=lru_wa, lru_ba=lru_ba, lru_wx=lru_wx, lru_bx=lru_bx, lru_lambda=lru_lambda,
             mla_q_norm=mla_q_norm, mla_w_uq=mla_w_uq, mla_kv_norm=mla_kv_norm, mla_w_ukv=mla_w_ukv, w_branch=w_branch, w_out=w_out)
    M1 = dict(ada_w=m_ada_w, ada_b=m_ada_b, norm_pre=m_norm_pre, norm_post=m_norm_post, w_in=m_w_in, ret_gn=m_ret_gn,
              lru_conv_w=m_lru_conv_w, lru_conv_b=m_lru_conv_b, lru_wa=m_lru_wa, lru_ba=m_lru_ba, lru_wx=m_lru_wx, lru_bx=m_lru_bx,
              lru_lambda=m_lru_lambda, mla_q_norm=m_mla_q_norm, mla_w_uq=m_mla_w_uq, mla_kv_norm=m_mla_kv_norm,
              mla_w_ukv=m_mla_w_ukv, w_branch=m_w_branch, w_out=m_w_out)
    V1 = dict(ada_w=v_ada_w, ada_b=v_ada_b, norm_pre=v_norm_pre, norm_post=v_norm_post, w_in=v_w_in, ret_gn=v_ret_gn,
              lru_conv_w=v_lru_conv_w, lru_conv_b=v_lru_conv_b, lru_wa=v_lru_wa, lru_ba=v_lru_ba, lru_wx=v_lru_wx, lru_bx=v_lru_bx,
              lru_lambda=v_lru_lambda, mla_q_norm=v_mla_q_norm, mla_w_uq=v_mla_w_uq, mla_kv_norm=v_mla_kv_norm,
              mla_w_ukv=v_mla_w_ukv, w_branch=v_w_branch, w_out=v_w_out)
    return _step(_CFG, x, c, positions, W, loss_target, M1, V1)
```

```python
import functools
import math
from typing import NamedTuple

import numpy as np
import jax
import jax.numpy as jnp
from jax import lax
from jax.experimental import pallas as pl
from jax.experimental.pallas import tpu as pltpu

f32 = jnp.float32
bf16 = jnp.bfloat16

NORM_EPS = 1e-6
ROPE_BASE = 10000.0
CHUNK = 64
HEAD = 128
ROPE = 64
CONV = 4
LRU_C = 8.0
ADAM_LR, ADAM_B1, ADAM_B2, ADAM_EPS, ADAM_WD, ADAM_STEP = 0.001, 0.9, 0.999, 1e-08, 0.01, 10

LANES = 128
SUBLANES = 8
VMEM_LIMIT = 56 * 1024 * 1024
MM_BUDGET = 40 * 1024 * 1024
N_DEV = 8
MESH = pl.DeviceIdType.MESH


class Cfg(NamedTuple):
    D: int = 2048
    S: int = 2048
    L: int = 4
    H: int = 8
    NB: int = 8
    MH: int = 8
    QL: int = 512
    KL: int = 512
    TR: int = 256
    TQ: int = 256

    @property
    def RW(self): return self.H * HEAD
    @property
    def LW(self): return self.NB * HEAD
    @property
    def MW(self): return self.MH * HEAD
    @property
    def o_rk(self): return self.RW
    @property
    def o_rv(self): return 2 * self.RW
    @property
    def o_rg(self): return 3 * self.RW
    @property
    def o_lx(self): return 4 * self.RW
    @property
    def o_lg(self): return 4 * self.RW + self.LW
    @property
    def o_mq(self): return 4 * self.RW + 2 * self.LW
    @property
    def o_mkv(self): return self.o_mq + self.QL
    @property
    def o_mg(self): return self.o_mkv + self.KL
    @property
    def o_merge(self): return self.o_mg + self.MW
    @property
    def o_mkr(self): return self.o_merge + 3 * self.D
    @property
    def NP(self): return -(-(self.o_mkr + ROPE) // 512) * 512
    @property
    def IN_WIDTH(self): return self.o_mkr + ROPE
    @property
    def QW(self): return self.MH * (HEAD + ROPE)
    @property
    def KVW(self): return self.MH * 2 * HEAD


_CFG = Cfg()


def _cparams(sem=None):
    return pltpu.CompilerParams(dimension_semantics=sem, vmem_limit_bytes=VMEM_LIMIT)


def _sigmoid(x):
    return jax.nn.sigmoid(x)


def _silu(x):
    return x * _sigmoid(x)


def _dsilu(x):
    s = _sigmoid(x)
    return s * (1.0 + x * (1.0 - s))


def _slab(rows, width, off):
    assert off % width == 0
    return pl.BlockSpec((rows, width), lambda i, _c=off // width: (i, _c))


def _row(width):
    return pl.BlockSpec((1, width), lambda i: (0, 0))


def _mm(a, b, out_dtype=f32, name="mm"):
    M, K = a.shape
    K2, N = b.shape
    assert K == K2
    tn = N if N <= 2048 else 512
    tk = K if K <= 2048 else 512
    assert N % tn == 0 and K % tk == 0
    osz = jnp.dtype(out_dtype).itemsize
    tm = M
    while 2 * tm * tk * 2 + 2 * tk * tn * 2 + 2 * tm * tn * osz + tm * tn * 4 > MM_BUDGET and tm % 16 == 0:
        tm //= 2
    assert M % tm == 0
    nk = K // tk

    if nk == 1:
        def body(a_ref, b_ref, o_ref):
            o_ref[...] = jnp.dot(a_ref[...].astype(bf16), b_ref[...].astype(bf16),
                                 preferred_element_type=f32).astype(o_ref.dtype)
        scratch = []
    else:
        def body(a_ref, b_ref, o_ref, acc_ref):
            k = pl.program_id(2)

            @pl.when(k == 0)
            def _():
                acc_ref[...] = jnp.zeros_like(acc_ref)

            acc_ref[...] += jnp.dot(a_ref[...].astype(bf16), b_ref[...].astype(bf16), preferred_element_type=f32)

            @pl.when(k == nk - 1)
            def _():
                o_ref[...] = acc_ref[...].astype(o_ref.dtype)
        scratch = [pltpu.VMEM((tm, tn), f32)]

    return pl.pallas_call(
        body, name=name,
        grid=(M // tm, N // tn, nk),
        in_specs=[pl.BlockSpec((tm, tk), lambda i, j, k: (i, k)), pl.BlockSpec((tk, tn), lambda i, j, k: (k, j))],
        out_specs=pl.BlockSpec((tm, tn), lambda i, j, k: (i, j)),
        out_shape=jax.ShapeDtypeStruct((M, N), out_dtype),
        scratch_shapes=scratch,
        compiler_params=_cparams(("parallel", "parallel", "arbitrary")),
    )(a, b)


def _ada_fwd(cfg, c_all, ada_w):
    L, D, n = ada_w.shape
    tn = n // 2 if (n // 2) % LANES == 0 else n

    def body(c_ref, w_ref, o_ref, ca_ref):
        ca = _silu(c_ref[...])
        ca_ref[...] = ca
        o_ref[0] = jnp.dot(ca.astype(bf16), w_ref[0].astype(bf16), preferred_element_type=f32)

    return pl.pallas_call(
        body, name="ada_fwd", grid=(L, n // tn),
        in_specs=[pl.BlockSpec((N_DEV, D), lambda l, j: (0, 0)), pl.BlockSpec((1, D, tn), lambda l, j: (l, 0, j))],
        out_specs=(pl.BlockSpec((1, N_DEV, tn), lambda l, j: (l, 0, j)), pl.BlockSpec((N_DEV, D), lambda l, j: (0, 0))),
        out_shape=(jax.ShapeDtypeStruct((L, N_DEV, n), f32), jax.ShapeDtypeStruct((N_DEV, D), f32)),
        compiler_params=_cparams(("arbitrary", "arbitrary")),
    )(c_all, ada_w)


def _ada_bwd(cfg, c_act_t, dmod):
    L, _, n = dmod.shape
    D = c_act_t.shape[0]
    tn = n // 2 if (n // 2) % LANES == 0 else n

    def body(c_ref, d_ref, o_ref):
        o_ref[0] = jnp.dot(c_ref[...].astype(bf16), d_ref[0].astype(bf16), preferred_element_type=f32)

    return pl.pallas_call(
        body, name="ada_bwd", grid=(L, n // tn),
        in_specs=[pl.BlockSpec((D, N_DEV), lambda l, j: (0, 0)), pl.BlockSpec((1, N_DEV, tn), lambda l, j: (l, 0, j))],
        out_specs=pl.BlockSpec((1, D, tn), lambda l, j: (l, 0, j)),
        out_shape=jax.ShapeDtypeStruct((L, D, n), f32),
        compiler_params=_cparams(("parallel", "parallel")),
    )(c_act_t, dmod)


def _prenorm_fwd(cfg, x, mod, gain):
    S, D, TR = cfg.S, cfg.D, cfg.TR

    def body(x_ref, mod_ref, g_ref, h_ref):
        x = x_ref[...]
        r = lax.rsqrt(jnp.mean(x * x, axis=-1, keepdims=True) + NORM_EPS)
        shift, scale = mod_ref[:, 0:D], mod_ref[:, D:2 * D]
        h_ref[...] = ((x * r) * g_ref[...] * (1.0 + scale) + shift).astype(bf16)

    return pl.pallas_call(
        body, name="prenorm_fwd", grid=(S // TR,),
        in_specs=[_slab(TR, D, 0), _row(3 * D), _row(D)],
        out_specs=_slab(TR, D, 0), out_shape=jax.ShapeDtypeStruct((S, D), bf16),
        compiler_params=_cparams(("parallel",)),
    )(x, mod, gain)


def _prenorm_bwd(cfg, x, dh, dres, mod, gain):
    S, D, TR = cfg.S, cfg.D, cfg.TR

    def body(x_ref, dh_ref, dres_ref, mod_ref, g_ref, dx_ref, sum_ref):
        i = pl.program_id(0)
        x, dh, g = x_ref[...], dh_ref[...], g_ref[...]
        scale = mod_ref[:, D:2 * D]
        r = lax.rsqrt(jnp.mean(x * x, axis=-1, keepdims=True) + NORM_EPS)
        xn = x * r
        t = dh * xn
        dxn = dh * (g * (1.0 + scale))
        dx_ref[...] = r * (dxn - xn * jnp.mean(dxn * xn, axis=-1, keepdims=True)) + dres_ref[...]
        part = jnp.concatenate([jnp.sum(dh, axis=0, keepdims=True), jnp.sum(t * g, axis=0, keepdims=True),
                                jnp.sum(t * (1.0 + scale), axis=0, keepdims=True), jnp.zeros((SUBLANES - 3, D), f32)], axis=0)

        @pl.when(i == 0)
        def _():
            sum_ref[...] = part

        @pl.when(i > 0)
        def _():
            sum_ref[...] += part

    return pl.pallas_call(
        body, name="prenorm_bwd", grid=(S // TR,),
        in_specs=[_slab(TR, D, 0), _slab(TR, D, 0), _slab(TR, D, 0), _row(3 * D), _row(D)],
        out_specs=(_slab(TR, D, 0), pl.BlockSpec((SUBLANES, D), lambda i: (0, 0))),
        out_shape=(jax.ShapeDtypeStruct((S, D), f32), jax.ShapeDtypeStruct((SUBLANES, D), f32)),
        compiler_params=_cparams(("arbitrary",)),
    )(x, dh, dres, mod, gain)


def _postnorm_fwd(cfg, x, y, mod, gain):
    S, D, TR = cfg.S, cfg.D, cfg.TR

    def body(x_ref, y_ref, mod_ref, g_ref, o_ref):
        y = y_ref[...]
        r = lax.rsqrt(jnp.mean(y * y, axis=-1, keepdims=True) + NORM_EPS)
        rg = mod_ref[:, 2 * D:3 * D]
        o_ref[...] = x_ref[...] + (1.0 + rg) * ((y * r) * g_ref[...])

    return pl.pallas_call(
        body, name="postnorm_fwd", grid=(S // TR,),
        in_specs=[_slab(TR, D, 0), _slab(TR, D, 0), _row(3 * D), _row(D)],
        out_specs=_slab(TR, D, 0), out_shape=jax.ShapeDtypeStruct((S, D), f32),
        compiler_params=_cparams(("parallel",)),
    )(x, y, mod, gain)


def _postnorm_bwd(cfg, dout, y, mod, gain):
    S, D, TR = cfg.S, cfg.D, cfg.TR

    def body(do_ref, y_ref, mod_ref, g_ref, dy_ref, sum_ref):
        i = pl.program_id(0)
        do, y, g = do_ref[...], y_ref[...], g_ref[...]
        rg = mod_ref[:, 2 * D:3 * D]
        r = lax.rsqrt(jnp.mean(y * y, axis=-1, keepdims=True) + NORM_EPS)
        yn = y * r
        t = do * yn
        dyn = do * ((1.0 + rg) * g)
        dy_ref[...] = (r * (dyn - yn * jnp.mean(dyn * yn, axis=-1, keepdims=True))).astype(bf16)
        part = jnp.concatenate([jnp.sum(t * g, axis=0, keepdims=True), jnp.sum(t * (1.0 + rg), axis=0, keepdims=True),
                                jnp.zeros((SUBLANES - 2, D), f32)], axis=0)

        @pl.when(i == 0)
        def _():
            sum_ref[...] = part

        @pl.when(i > 0)
        def _():
            sum_ref[...] += part

    return pl.pallas_call(
        body, name="postnorm_bwd", grid=(S // TR,),
        in_specs=[_slab(TR, D, 0), _slab(TR, D, 0), _row(3 * D), _row(D)],
        out_specs=(_slab(TR, D, 0), pl.BlockSpec((SUBLANES, D), lambda i: (0, 0))),
        out_shape=(jax.ShapeDtypeStruct((S, D), bf16), jax.ShapeDtypeStruct((SUBLANES, D), f32)),
        compiler_params=_cparams(("arbitrary",)),
    )(dout, y, mod, gain)


def _loss_head(cfg, y, target):
    S, D, TR = cfg.S, cfg.D, cfg.TR

    def body(y_ref, t_ref, d_ref, l_ref):
        i = pl.program_id(0)
        err = y_ref[...] - t_ref[...]
        d_ref[...] = err / D
        part = jnp.zeros((SUBLANES, LANES), f32) + 0.5 * jnp.sum(jnp.mean(err * err, axis=-1, keepdims=True))

        @pl.when(i == 0)
        def _():
            l_ref[...] = part

        @pl.when(i > 0)
        def _():
            l_ref[...] += part

    return pl.pallas_call(
        body, name="loss_head", grid=(S // TR,),
        in_specs=[_slab(TR, D, 0), _slab(TR, D, 0)],
        out_specs=(_slab(TR, D, 0), pl.BlockSpec((SUBLANES, LANES), lambda i: (0, 0))),
        out_shape=(jax.ShapeDtypeStruct((S, D), f32), jax.ShapeDtypeStruct((SUBLANES, LANES), f32)),
        compiler_params=_cparams(("arbitrary",)),
    )(y, target)


def _merge_fwd(cfg, proj, u0, u1, u2):
    S, D, TR = cfg.S, cfg.D, cfg.TR

    def body(l0, l1, l2, u0_ref, u1_ref, u2_ref, o_ref):
        o_ref[...] = (_sigmoid(l0[...]) * u0_ref[...] + _sigmoid(l1[...]) * u1_ref[...]
                      + _sigmoid(l2[...]) * u2_ref[...]).astype(bf16)

    return pl.pallas_call(
        body, name="merge_fwd", grid=(S // TR,),
        in_specs=[_slab(TR, D, cfg.o_merge + b * D) for b in range(3)] + [_slab(TR, D, 0)] * 3,
        out_specs=_slab(TR, D, 0), out_shape=jax.ShapeDtypeStruct((S, D), bf16),
        compiler_params=_cparams(("parallel",)),
    )(proj, proj, proj, u0, u1, u2)


def _merge_bwd(cfg, proj, dmerged, u0, u1, u2):
    S, D, TR = cfg.S, cfg.D, cfg.TR

    def body(l0, l1, l2, dm_ref, u0_ref, u1_ref, u2_ref, du0, du1, du2, dl_ref):
        dm = dm_ref[...]
        for b, (l, u, du) in enumerate(((l0, u0_ref, du0), (l1, u1_ref, du1), (l2, u2_ref, du2))):
            g = _sigmoid(l[...])
            du[...] = (dm * g).astype(bf16)
            dl_ref[:, b * D:(b + 1) * D] = (dm * u[...] * (g * (1.0 - g))).astype(bf16)

    return pl.pallas_call(
        body, name="merge_bwd", grid=(S // TR,),
        in_specs=[_slab(TR, D, cfg.o_merge + b * D) for b in range(3)] + [_slab(TR, D, 0)] * 4,
        out_specs=(_slab(TR, D, 0),) * 3 + (_slab(TR, 3 * D, 0),),
        out_shape=(jax.ShapeDtypeStruct((S, D), bf16),) * 3 + (jax.ShapeDtypeStruct((S, 3 * D), bf16),),
        compiler_params=_cparams(("parallel",)),
    )(proj, proj, proj, dmerged, u0, u1, u2)


def _rope128(x, c, s):
    return x * c + pltpu.roll(x, 64, axis=1) * s


def _rope128_t(dy, c, s):
    return dy * c + pltpu.roll(dy * s, 64, axis=1)


def _swap32(x):
    w = x.shape[1]
    lane = lax.broadcasted_iota(jnp.int32, x.shape, 1)
    return jnp.where((lane % 64) < 32, pltpu.roll(x, w - 32, axis=1), pltpu.roll(x, 32, axis=1))


def _rope64(x, c, s):
    return x * c + _swap32(x) * s


def _rope64_t(dy, c, s):
    return dy * c + _swap32(dy * s)


def _rope_tables(cfg, positions):
    pos = positions.astype(f32)[0][:, None]

    def tab(dim):
        inv_freq = ROPE_BASE ** (-jnp.arange(0, dim, 2, dtype=f32) / dim)
        ang = pos * inv_freq
        cos, sin = jnp.cos(ang), jnp.sin(ang)
        return jnp.concatenate([cos, cos], axis=1), jnp.concatenate([-sin, sin], axis=1)

    return tab(HEAD), tab(ROPE)


def _ret_consts(cfg):
    h = np.arange(cfg.H, dtype=np.float64)
    log_gamma = np.log1p(-np.exp2(-5.0 - h)).astype(np.float32)
    idx = np.arange(CHUNK, dtype=np.float32)
    intra = np.exp(log_gamma[:, None, None] * np.abs(idx[:, None] - idx[None, :]))
    kdec = np.exp(log_gamma[:, None] * (CHUNK - 1 - idx)[None, :])
    qdec = np.exp(log_gamma[:, None] * (idx + 1.0)[None, :])
    cdec = np.exp(log_gamma * CHUNK)
    bc = lambda a: jnp.asarray(np.broadcast_to(a[..., None], a.shape + (HEAD,)).astype(np.float32))
    return jnp.asarray(intra.astype(np.float32)), bc(kdec), bc(qdec), bc(cdec[:, None])


def _ret_core(cfg, q_raw, k_raw, v_raw, cos, sin, intra, kdec, qdec, cdec, p_ref):
    S = cfg.S
    NC = S // CHUNK
    q = _rope128(q_raw, cos, sin) * (HEAD ** -0.5)
    k = _rope128(k_raw, cos, sin)
    q3 = q.reshape(NC, CHUNK, HEAD)
    k3 = k.reshape(NC, CHUNK, HEAD)
    qb, kb = q3.astype(bf16), k3.astype(bf16)
    vb = v_raw.reshape(NC, CHUNK, HEAD).astype(bf16)
    sdb = (jnp.einsum('nid,njd->nij', qb, kb, preferred_element_type=f32) * intra[None]).astype(bf16)
    o_intra = jnp.einsum('nij,nje->nie', sdb, vb, preferred_element_type=f32)
    kdb = (k3 * kdec[None]).astype(bf16)
    kv = jnp.einsum('njd,nje->nde', kdb, vb, preferred_element_type=f32)
    p_ref[0] = jnp.zeros((HEAD, HEAD), f32)
    for n in range(1, NC):
        p_ref[n] = p_ref[n - 1] * cdec + kv[n - 1]
    pb = p_ref[...].astype(bf16)
    qdb = (q3 * qdec[None]).astype(bf16)
    o_inter = jnp.einsum('nid,nde->nie', qdb, pb, preferred_element_type=f32)
    o = (o_intra + o_inter).reshape(S, HEAD)
    return o, (qb, kb, vb, sdb, kdb, qdb, pb)


def _ret_specs(cfg):
    S = cfg.S
    hs = lambda off: pl.BlockSpec((S, HEAD), lambda h, _c=off // HEAD: (0, _c + h))
    full = pl.BlockSpec((S, HEAD), lambda h: (0, 0))
    consts = [pl.BlockSpec((None, CHUNK, CHUNK), lambda h: (h, 0, 0)), pl.BlockSpec((None, CHUNK, HEAD), lambda h: (h, 0, 0)),
              pl.BlockSpec((None, CHUNK, HEAD), lambda h: (h, 0, 0)), pl.BlockSpec((None, 1, HEAD), lambda h: (h, 0, 0))]
    gn = pl.BlockSpec((1, HEAD), lambda h: (0, h))
    return hs, full, consts, gn


def _ret_fwd(cfg, proj, gn, cos, sin, consts):
    S, NC = cfg.S, cfg.S // CHUNK
    hs, full, cspecs, gspec = _ret_specs(cfg)

    def body(q_ref, k_ref, v_ref, g_ref, gn_ref, cos_ref, sin_ref, intra, kdec, qdec, cdec, y_ref, p_ref):
        o, _ = _ret_core(cfg, q_ref[...], k_ref[...], v_ref[...], cos_ref[...], sin_ref[...],
                         intra[...], kdec[...], qdec[...], cdec[...], p_ref)
        mean = jnp.mean(o, axis=-1, keepdims=True)
        var = jnp.mean(jnp.square(o - mean), axis=-1, keepdims=True)
        z = ((o - mean) * lax.rsqrt(var + NORM_EPS)) * gn_ref[...]
        y_ref[...] = (z * _silu(g_ref[...])).astype(bf16)

    return pl.pallas_call(
        body, name="ret_fwd", grid=(cfg.H,),
        in_specs=[hs(0), hs(cfg.o_rk), hs(cfg.o_rv), hs(cfg.o_rg), gspec, full, full] + cspecs,
        out_specs=hs(0), out_shape=jax.ShapeDtypeStruct((S, cfg.RW), bf16),
        scratch_shapes=[pltpu.VMEM((NC, HEAD, HEAD), f32)],
        compiler_params=_cparams(("arbitrary",)),
    )(proj, proj, proj, proj, gn, cos, sin, *consts)


def _ret_bwd(cfg, proj, dy, gn, cos, sin, consts):
    S, NC = cfg.S, cfg.S // CHUNK
    hs, full, cspecs, gspec = _ret_specs(cfg)

    def body(q_ref, k_ref, v_ref, g_ref, dy_ref, gn_ref, cos_ref, sin_ref, intra_ref, kdec_ref, qdec_ref, cdec_ref,
             dq_ref, dk_ref, dv_ref, dg_ref, dgn_ref, p_ref, g_scr):
        cos, sin = cos_ref[...], sin_ref[...]
        intra, kdec, qdec, cdec = intra_ref[...], kdec_ref[...], qdec_ref[...], cdec_ref[...]
        o, (qb, kb, vb, sdb, kdb, qdb, pb) = _ret_core(cfg, q_ref[...], k_ref[...], v_ref[...], cos, sin,
                                                     intra, kdec, qdec, cdec, p_ref)
        gate, dy, gnv = g_ref[...], dy_ref[...], gn_ref[...]
        mean = jnp.mean(o, axis=-1, keepdims=True)
        rstd = lax.rsqrt(jnp.mean(jnp.square(o - mean), axis=-1, keepdims=True) + NORM_EPS)
        on = (o - mean) * rstd
        dz = dy * _silu(gate)
        dg_ref[...] = (dy * (on * gnv) * _dsilu(gate)).astype(bf16)
        dgn_ref[...] = jnp.sum(dz * on, axis=0, keepdims=True)
        don = dz * gnv
        do = rstd * (don - jnp.mean(don, axis=-1, keepdims=True) - on * jnp.mean(don * on, axis=-1, keepdims=True))
        dob = do.reshape(NC, CHUNK, HEAD).astype(bf16)
        dsb = (jnp.einsum('nie,nje->nij', dob, vb, preferred_element_type=f32) * intra[None]).astype(bf16)
        dv = jnp.einsum('nij,nie->nje', sdb, dob, preferred_element_type=f32)
        dq = jnp.einsum('nij,njd->nid', dsb, kb, preferred_element_type=f32)
        dk = jnp.einsum('nij,nid->njd', dsb, qb, preferred_element_type=f32)
        dq = dq + jnp.einsum('nie,nde->nid', dob, pb, preferred_element_type=f32) * qdec[None]
        dp = jnp.einsum('nid,nie->nde', qdb, dob, preferred_element_type=f32)
        g_scr[NC - 1] = jnp.zeros((HEAD, HEAD), f32)
        for n in range(NC - 2, -1, -1):
            g_scr[n] = dp[n + 1] + g_scr[n + 1] * cdec
        gb = g_scr[...].astype(bf16)
        dk = dk + jnp.einsum('nje,nde->njd', vb, gb, preferred_element_type=f32) * kdec[None]
        dv = dv + jnp.einsum('njd,nde->nje', kdb, gb, preferred_element_type=f32)
        dq_ref[...] = _rope128_t(dq.reshape(S, HEAD) * (HEAD ** -0.5), cos, sin).astype(bf16)
        dk_ref[...] = _rope128_t(dk.reshape(S, HEAD), cos, sin).astype(bf16)
        dv_ref[...] = dv.reshape(S, HEAD).astype(bf16)

    return pl.pallas_call(
        body, name="ret_bwd", grid=(cfg.H,),
        in_specs=[hs(0), hs(cfg.o_rk), hs(cfg.o_rv), hs(cfg.o_rg), hs(0), gspec, full, full] + cspecs,
        out_specs=(hs(0),) * 4 + (gspec,),
        out_shape=(jax.ShapeDtypeStruct((S, cfg.RW), bf16),) * 4 + (jax.ShapeDtypeStruct((1, cfg.RW), f32),),
        scratch_shapes=[pltpu.VMEM((NC, HEAD, HEAD), f32), pltpu.VMEM((NC, HEAD, HEAD), f32)],
        compiler_params=_cparams(("arbitrary",)),
    )(proj, proj, proj, proj, dy, gn, cos, sin, *consts)


def _expm1(x):
    small = x * (1.0 + x * (0.5 + x * (1.0 / 6.0 + x * (1.0 / 24.0 + x * (1.0 / 120.0)))))
    return jnp.where(jnp.abs(x) < 0.1, small, jnp.exp(x) - 1.0)


def _softplus(z):
    return jnp.maximum(z, 0.0) + jnp.log1p(jnp.exp(-jnp.abs(z)))


def _lru_conv(cfg, x_ref, halo_ref, cw, scr, first):
    TR = cfg.TR
    scr[0:SUBLANES, :] = jnp.where(first, 0.0, halo_ref[...])
    scr[SUBLANES:SUBLANES + TR, :] = x_ref[...]
    xc = scr[pl.ds(SUBLANES - (CONV - 1), TR), :] * cw[0:1, :]
    for j in range(1, CONV):
        xc = xc + scr[pl.ds(SUBLANES - (CONV - 1) + j, TR), :] * cw[j:j + 1, :]
    return xc


def _lru_pre(cfg, xc, wa_ref, wx_ref, ba, bx):
    xb = xc.astype(bf16)
    pa = jnp.concatenate([jnp.dot(xb[:, n * HEAD:(n + 1) * HEAD], wa_ref[n].astype(bf16), preferred_element_type=f32)
                          for n in range(cfg.NB)], axis=1) + ba
    px = jnp.concatenate([jnp.dot(xb[:, n * HEAD:(n + 1) * HEAD], wx_ref[n].astype(bf16), preferred_element_type=f32)
                          for n in range(cfg.NB)], axis=1) + bx
    return pa, px


def _lru_ab(pa, px, xc, lam):
    r, i = _sigmoid(pa), _sigmoid(px)
    log_a = (-LRU_C * r) * _softplus(-lam)
    a = jnp.exp(log_a)
    b = jnp.sqrt(-_expm1(2.0 * log_a)) * (i * xc)
    return a, b


def _lru_halo_specs(cfg, off, W):
    TR, S = cfg.TR, cfg.S
    nb = TR // SUBLANES
    cb = off // W
    main = pl.BlockSpec((TR, W), lambda i: (i, cb))
    prev = pl.BlockSpec((SUBLANES, W), lambda i: (jnp.maximum(i * nb - 1, 0), cb))
    nxt = pl.BlockSpec((SUBLANES, W), lambda i: (jnp.minimum((i + 1) * nb, S // SUBLANES - 1), cb))
    return main, prev, nxt


def _lru_gates(cfg, proj, cw, cb, wa, ba, wx, bx, lam):
    S, W, TR, NB = cfg.S, cfg.LW, cfg.TR, cfg.NB
    assert cfg.o_lx % W == 0
    main, prev, _ = _lru_halo_specs(cfg, cfg.o_lx, W)
    wspec = pl.BlockSpec((NB, HEAD, HEAD), lambda i: (0, 0, 0))

    def body(x_ref, halo_ref, cw_ref, cb_ref, wa_ref, ba_ref, wx_ref, bx_ref, lam_ref, a_ref, b_ref, scr):
        xc = _lru_conv(cfg, x_ref, halo_ref, cw_ref[...], scr, pl.program_id(0) == 0) + cb_ref[...]
        pa, px = _lru_pre(cfg, xc, wa_ref, wx_ref, ba_ref[...], bx_ref[...])
        a, b = _lru_ab(pa, px, xc, lam_ref[...])
        a_ref[...] = a
        b_ref[...] = b

    return pl.pallas_call(
        body, name="lru_gates", grid=(S // TR,),
        in_specs=[main, prev, pl.BlockSpec((CONV, W), lambda i: (0, 0)), _row(W), wspec, _row(W), wspec, _row(W), _row(W)],
        out_specs=(_slab(TR, W, 0),) * 2, out_shape=(jax.ShapeDtypeStruct((S, W), f32),) * 2,
        scratch_shapes=[pltpu.VMEM((TR + SUBLANES, W), f32)],
        compiler_params=_cparams(("parallel",)),
    )(proj, proj, cw, cb, wa, ba, wx, bx, lam)


def _lru_lane_block(cfg):
    return 256 if cfg.LW % 256 == 0 else LANES


def _lru_scan_fwd(cfg, proj, a, b):
    S, W = cfg.S, cfg.LW
    LB = _lru_lane_block(cfg)
    assert cfg.o_lg % LB == 0
    col = lambda off: pl.BlockSpec((S, LB), lambda j, _c=off // LB: (0, _c + j))

    def body(a_ref, b_ref, g_ref, h_ref, y_ref):
        def blk(t, h):
            r0 = pl.multiple_of(t * SUBLANES, SUBLANES)
            at, bt = a_ref[pl.ds(r0, SUBLANES), :], b_ref[pl.ds(r0, SUBLANES), :]
            rows = []
            for j in range(SUBLANES):
                h = at[j:j + 1, :] * h + bt[j:j + 1, :]
                rows.append(h)
            h_ref[pl.ds(r0, SUBLANES), :] = jnp.concatenate(rows, axis=0)
            return h

        lax.fori_loop(0, S // SUBLANES, blk, jnp.zeros((1, LB), f32))
        y_ref[...] = (h_ref[...] * _silu(g_ref[...])).astype(bf16)

    return pl.pallas_call(
        body, name="lru_scan_fwd", grid=(W // LB,),
        in_specs=[col(0), col(0), col(cfg.o_lg)],
        out_specs=(col(0), col(0)),
        out_shape=(jax.ShapeDtypeStruct((S, W), f32), jax.ShapeDtypeStruct((S, W), bf16)),
        compiler_params=_cparams(("parallel",)),
    )(a, b, proj)


def _lru_scan_bwd(cfg, proj, a, h, dy):
    S, W = cfg.S, cfg.LW
    LB = _lru_lane_block(cfg)
    col = lambda off: pl.BlockSpec((S, LB), lambda j, _c=off // LB: (0, _c + j))

    def body(a_ref, h_ref, dy_ref, g_ref, da_ref, db_ref, dg_ref):
        gate, dy = g_ref[...], dy_ref[...]
        dg_ref[...] = (dy * h_ref[...] * _dsilu(gate)).astype(bf16)
        da_ref[...] = dy * _silu(gate)

        def blk(t, carry):
            dh_next, a_next = carry
            r0 = pl.multiple_of((S // SUBLANES - 1 - t) * SUBLANES, SUBLANES)
            at, ct = a_ref[pl.ds(r0, SUBLANES), :], da_ref[pl.ds(r0, SUBLANES), :]
            rows = [None] * SUBLANES
            for j in range(SUBLANES - 1, -1, -1):
                dh_next = ct[j:j + 1, :] + a_next * dh_next
                a_next = at[j:j + 1, :]
                rows[j] = dh_next
            db_ref[pl.ds(r0, SUBLANES), :] = jnp.concatenate(rows, axis=0)
            return dh_next, a_next

        z = jnp.zeros((1, LB), f32)
        lax.fori_loop(0, S // SUBLANES, blk, (z, z))
        row = lax.broadcasted_iota(jnp.int32, (S, LB), 0)
        hprev = jnp.where(row == 0, 0.0, pltpu.roll(h_ref[...], 1, axis=0))
        da_ref[...] = db_ref[...] * hprev

    return pl.pallas_call(
        body, name="lru_scan_bwd", grid=(W // LB,),
        in_specs=[col(0), col(0), col(0), col(cfg.o_lg)],
        out_specs=(col(0),) * 3,
        out_shape=(jax.ShapeDtypeStruct((S, W), f32),) * 2 + (jax.ShapeDtypeStruct((S, W), bf16),),
        compiler_params=_cparams(("parallel",)),
    )(a, h, dy, proj)


def _lru_gates_bwd(cfg, proj, da, db, cw, cb, wa, ba, wx, bx, lam):
    S, W, TR, NB = cfg.S, cfg.LW, cfg.TR, cfg.NB
    main, prev, _ = _lru_halo_specs(cfg, cfg.o_lx, W)
    wspec = pl.BlockSpec((NB, HEAD, HEAD), lambda i: (0, 0, 0))

    def body(x_ref, halo_ref, da_ref, db_ref, cw_ref, cb_ref, wa_ref, ba_ref, wx_ref, bx_ref, lam_ref,
             dxc_ref, dwa_ref, dwx_ref, sum_ref, scr):
        i = pl.program_id(0)
        lam = lam_ref[...]
        xc = _lru_conv(cfg, x_ref, halo_ref, cw_ref[...], scr, i == 0) + cb_ref[...]
        pa, px = _lru_pre(cfg, xc, wa_ref, wx_ref, ba_ref[...], bx_ref[...])
        _, vjp = jax.vjp(_lru_ab, pa, px, xc, lam)
        dpa, dpx, dxc, dlam = vjp((da_ref[...], db_ref[...]))
        xb, dpab, dpxb = xc.astype(bf16), dpa.astype(bf16), dpx.astype(bf16)
        nt = (((1,), (1,)), ((), ()))
        tn = (((0,), (0,)), ((), ()))
        back = []
        dwa, dwx = [], []
        for n in range(NB):
            sl = slice(n * HEAD, (n + 1) * HEAD)
            back.append(lax.dot_general(dpab[:, sl], wa_ref[n].astype(bf16), nt, preferred_element_type=f32)
                        + lax.dot_general(dpxb[:, sl], wx_ref[n].astype(bf16), nt, preferred_element_type=f32))
            dwa.append(lax.dot_general(xb[:, sl], dpab[:, sl], tn, preferred_element_type=f32))
            dwx.append(lax.dot_general(xb[:, sl], dpxb[:, sl], tn, preferred_element_type=f32))
        dxc_ref[...] = dxc + jnp.concatenate(back, axis=1)
        part = jnp.concatenate([jnp.sum(dpa, axis=0, keepdims=True), jnp.sum(dpx, axis=0, keepdims=True), dlam,
                                jnp.zeros((SUBLANES - 3, W), f32)], axis=0)

        @pl.when(i == 0)
        def _():
            sum_ref[...] = part
            for n in range(NB):
                dwa_ref[n] = dwa[n]
                dwx_ref[n] = dwx[n]

        @pl.when(i > 0)
        def _():
            sum_ref[...] += part
            for n in range(NB):
                dwa_ref[n] += dwa[n]
                dwx_ref[n] += dwx[n]

    return pl.pallas_call(
        body, name="lru_gates_bwd", grid=(S // TR,),
        in_specs=[main, prev, _slab(TR, W, 0), _slab(TR, W, 0), pl.BlockSpec((CONV, W), lambda i: (0, 0)), _row(W),
                  wspec, _row(W), wspec, _row(W), _row(W)],
        out_specs=(_slab(TR, W, 0), wspec, wspec, pl.BlockSpec((SUBLANES, W), lambda i: (0, 0))),
        out_shape=(jax.ShapeDtypeStruct((S, W), f32), jax.ShapeDtypeStruct((NB, HEAD, HEAD), f32),
                   jax.ShapeDtypeStruct((NB, HEAD, HEAD), f32), jax.ShapeDtypeStruct((SUBLANES, W), f32)),
        scratch_shapes=[pltpu.VMEM((TR + SUBLANES, W), f32)],
        compiler_params=_cparams(("arbitrary",)),
    )(proj, proj, da, db, cw, cb, wa, ba, wx, bx, lam)


def _lru_conv_bwd(cfg, proj, dxc, cw):
    S, W, TR = cfg.S, cfg.LW, cfg.TR
    main, prev, _ = _lru_halo_specs(cfg, cfg.o_lx, W)
    dmain, _, dnext = _lru_halo_specs(cfg, 0, W)

    def body(x_ref, xhalo_ref, d_ref, dhalo_ref, cw_ref, dx_ref, sum_ref, xs, ds):
        i = pl.program_id(0)
        cw = cw_ref[...]
        d = d_ref[...]
        xs[0:SUBLANES, :] = jnp.where(i == 0, 0.0, xhalo_ref[...])
        xs[SUBLANES:SUBLANES + TR, :] = x_ref[...]
        ds[0:TR, :] = d
        ds[TR:TR + SUBLANES, :] = jnp.where(i == pl.num_programs(0) - 1, 0.0, dhalo_ref[...])
        dx = ds[pl.ds(CONV - 1, TR), :] * cw[0:1, :]
        parts = [jnp.sum(d * xs[pl.ds(SUBLANES - (CONV - 1), TR), :], axis=0, keepdims=True)]
        for j in range(1, CONV):
            dx = dx + ds[pl.ds(CONV - 1 - j, TR), :] * cw[j:j + 1, :]
            parts.append(jnp.sum(d * xs[pl.ds(SUBLANES - (CONV - 1) + j, TR), :], axis=0, keepdims=True))
        dx_ref[...] = dx.astype(bf16)
        part = jnp.concatenate(parts + [jnp.sum(d, axis=0, keepdims=True), jnp.zeros((SUBLANES - CONV - 1, W), f32)], axis=0)

        @pl.when(i == 0)
        def _():
            sum_ref[...] = part

        @pl.when(i > 0)
        def _():
            sum_ref[...] += part

    return pl.pallas_call(
        body, name="lru_conv_bwd", grid=(S // TR,),
        in_specs=[main, prev, dmain, dnext, pl.BlockSpec((CONV, W), lambda i: (0, 0))],
        out_specs=(_slab(TR, W, 0), pl.BlockSpec((SUBLANES, W), lambda i: (0, 0))),
        out_shape=(jax.ShapeDtypeStruct((S, W), bf16), jax.ShapeDtypeStruct((SUBLANES, W), f32)),
        scratch_shapes=[pltpu.VMEM((TR + SUBLANES, W), f32), pltpu.VMEM((TR + SUBLANES, W), f32)],
        compiler_params=_cparams(("arbitrary",)),
    )(proj, proj, dxc, dxc, cw)


def _rms(x, g):
    r = lax.rsqrt(jnp.mean(x * x, axis=-1, keepdims=True) + NORM_EPS)
    return (x * r) * g, r


def _mla_norm(cfg, proj, qg, kg):
    S, TR = cfg.S, cfg.TR

    def body(q_ref, k_ref, qg_ref, kg_ref, qn_ref, kn_ref):
        qn_ref[...] = _rms(q_ref[...], qg_ref[...])[0].astype(bf16)
        kn_ref[...] = _rms(k_ref[...], kg_ref[...])[0].astype(bf16)

    return pl.pallas_call(
        body, name="mla_norm", grid=(S // TR,),
        in_specs=[_slab(TR, cfg.QL, cfg.o_mq), _slab(TR, cfg.KL, cfg.o_mkv), _row(cfg.QL), _row(cfg.KL)],
        out_specs=(_slab(TR, cfg.QL, 0), _slab(TR, cfg.KL, 0)),
        out_shape=(jax.ShapeDtypeStruct((S, cfg.QL), bf16), jax.ShapeDtypeStruct((S, cfg.KL), bf16)),
        compiler_params=_cparams(("parallel",)),
    )(proj, proj, qg, kg)


def _mla_norm_bwd(cfg, proj, dqn, dkn, qg, kg):
    S, TR = cfg.S, cfg.TR

    def one(x, g, dn):
        r = lax.rsqrt(jnp.mean(x * x, axis=-1, keepdims=True) + NORM_EPS)
        xn = x * r
        dxn = dn * g
        dx = r * (dxn - xn * jnp.mean(dxn * xn, axis=-1, keepdims=True))
        return dx, jnp.sum(dn * xn, axis=0, keepdims=True)

    def body(q_ref, k_ref, dq_ref, dk_ref, qg_ref, kg_ref, dmq_ref, dmk_ref, sq_ref, sk_ref):
        i = pl.program_id(0)
        dq, gq = one(q_ref[...], qg_ref[...], dq_ref[...])
        dk, gk = one(k_ref[...], kg_ref[...], dk_ref[...])
        dmq_ref[...] = dq.astype(bf16)
        dmk_ref[...] = dk.astype(bf16)
        pq = jnp.concatenate([gq, jnp.zeros((SUBLANES - 1, cfg.QL), f32)], axis=0)
        pk = jnp.concatenate([gk, jnp.zeros((SUBLANES - 1, cfg.KL), f32)], axis=0)

        @pl.when(i == 0)
        def _():
            sq_ref[...] = pq
            sk_ref[...] = pk

        @pl.when(i > 0)
        def _():
            sq_ref[...] += pq
            sk_ref[...] += pk

    return pl.pallas_call(
        body, name="mla_norm_bwd", grid=(S // TR,),
        in_specs=[_slab(TR, cfg.QL, cfg.o_mq), _slab(TR, cfg.KL, cfg.o_mkv), _slab(TR, cfg.QL, 0), _slab(TR, cfg.KL, 0),
                  _row(cfg.QL), _row(cfg.KL)],
        out_specs=(_slab(TR, cfg.QL, 0), _slab(TR, cfg.KL, 0), pl.BlockSpec((SUBLANES, cfg.QL), lambda i: (0, 0)),
                   pl.BlockSpec((SUBLANES, cfg.KL), lambda i: (0, 0))),
        out_shape=(jax.ShapeDtypeStruct((S, cfg.QL), bf16), jax.ShapeDtypeStruct((S, cfg.KL), bf16),
                   jax.ShapeDtypeStruct((SUBLANES, cfg.QL), f32), jax.ShapeDtypeStruct((SUBLANES, cfg.KL), f32)),
        compiler_params=_cparams(("arbitrary",)),
    )(proj, proj, dqn, dkn, qg, kg)


def _mla_pack(cfg, proj, q, kv, cq, sq, ck, sk):
    S, TR, MH = cfg.S, cfg.TR, cfg.MH
    NW, RWD = MH * HEAD, MH * ROPE

    def body(q_ref, kv_ref, kr_ref, cq_ref, sq_ref, ck_ref, sk_ref, qo_ref, ko_ref, vo_ref):
        q, kv = q_ref[...], kv_ref[...]
        qr = _rope64(q[:, NW:], cq_ref[...], sq_ref[...])
        kr = _rope64(kr_ref[...], ck_ref[...], sk_ref[...]).astype(bf16)
        lane = lax.broadcasted_iota(jnp.int32, (TR, HEAD), 1)
        for h in range(MH):
            grp = qr[:, (h // 2) * HEAD:(h // 2 + 1) * HEAD]
            if h % 2:
                grp = pltpu.roll(grp, 64, axis=1)
            qo_ref[h] = jnp.concatenate([q[:, h * HEAD:(h + 1) * HEAD], jnp.where(lane < ROPE, grp, 0.0)], axis=1).astype(bf16)
            ko_ref[h] = jnp.concatenate([kv[:, 2 * h * HEAD:(2 * h + 1) * HEAD].astype(bf16), kr], axis=1)
            vo_ref[h] = kv[:, (2 * h + 1) * HEAD:(2 * h + 2) * HEAD].astype(bf16)

    hspec = lambda w: pl.BlockSpec((MH, TR, w), lambda i: (0, i, 0))
    return pl.pallas_call(
        body, name="mla_pack", grid=(S // TR,),
        in_specs=[_slab(TR, cfg.QW, 0), _slab(TR, cfg.KVW, 0), _slab(TR, HEAD, cfg.o_mkr),
                  _slab(TR, RWD, 0), _slab(TR, RWD, 0), _slab(TR, HEAD, 0), _slab(TR, HEAD, 0)],
        out_specs=(hspec(2 * HEAD), hspec(2 * HEAD), hspec(HEAD)),
        out_shape=(jax.ShapeDtypeStruct((MH, S, 2 * HEAD), bf16), jax.ShapeDtypeStruct((MH, S, 2 * HEAD), bf16),
                   jax.ShapeDtypeStruct((MH, S, HEAD), bf16)),
        compiler_params=_cparams(("parallel",)),
    )(q, kv, proj, cq, sq, ck, sk)


def _mla_unpack_bwd(cfg, dq3, dk3, dv3, cq, sq, ck, sk):
    S, TR, MH = cfg.S, cfg.TR, cfg.MH
    RWD = MH * ROPE

    def body(dq_ref, dk_ref, dv_ref, cq_ref, sq_ref, ck_ref, sk_ref, q_ref, kv_ref, kr_ref):
        lane = lax.broadcasted_iota(jnp.int32, (TR, HEAD), 1)
        nope, ropes, kvs = [], [], []
        dkr = jnp.zeros((TR, HEAD), f32)
        for h in range(MH):
            dq = dq_ref[h]
            nope.append(dq[:, :HEAD])
            part = jnp.where(lane < ROPE, dq[:, HEAD:], 0.0)
            if h % 2:
                ropes[-1] = ropes[-1] + pltpu.roll(part, 64, axis=1)
            else:
                ropes.append(part)
            dk = dk_ref[h]
            kvs += [dk[:, :HEAD], dv_ref[h]]
            dkr = dkr + dk[:, HEAD:]
        dqr = _rope64_t(jnp.concatenate(ropes, axis=1), cq_ref[...], sq_ref[...])
        q_ref[...] = jnp.concatenate(nope + [dqr], axis=1).astype(bf16)
        kv_ref[...] = jnp.concatenate(kvs, axis=1).astype(bf16)
        dkr = jnp.where(lane < ROPE, dkr, 0.0)
        kr_ref[...] = _rope64_t(dkr, ck_ref[...], sk_ref[...]).astype(bf16)

    hspec = lambda w: pl.BlockSpec((MH, TR, w), lambda i: (0, i, 0))
    return pl.pallas_call(
        body, name="mla_unpack_bwd", grid=(S // TR,),
        in_specs=[hspec(2 * HEAD), hspec(2 * HEAD), hspec(HEAD), _slab(TR, RWD, 0), _slab(TR, RWD, 0),
                  _slab(TR, HEAD, 0), _slab(TR, HEAD, 0)],
        out_specs=(_slab(TR, cfg.QW, 0), _slab(TR, cfg.KVW, 0), _slab(TR, HEAD, 0)),
        out_shape=(jax.ShapeDtypeStruct((S, cfg.QW), bf16), jax.ShapeDtypeStruct((S, cfg.KVW), bf16),
                   jax.ShapeDtypeStruct((S, HEAD), bf16)),
        compiler_params=_cparams(("parallel",)),
    )(dq3, dk3, dv3, cq, sq, ck, sk)


def _mla_probs(cfg, q, k, i):
    TQ, S = cfg.TQ, cfg.S
    nt = (((1,), (1,)), ((), ()))
    s = lax.dot_general(q, k, nt, preferred_element_type=f32) * ((HEAD + ROPE) ** -0.5)
    qc = (i * TQ + lax.broadcasted_iota(jnp.int32, (TQ, S), 0)) // CHUNK
    kc = lax.broadcasted_iota(jnp.int32, (TQ, S), 1) // CHUNK
    s = jnp.where(kc <= qc, s, -1e30)
    m = jnp.max(s, axis=-1, keepdims=True)
    e = jnp.exp(s - m)
    return e / jnp.sum(e, axis=-1, keepdims=True)


def _mla_attn_specs(cfg):
    S, TQ = cfg.S, cfg.TQ
    qs = lambda w: pl.BlockSpec((None, TQ, w), lambda h, i: (h, i, 0))
    ks = lambda w: pl.BlockSpec((None, S, w), lambda h, i: (h, 0, 0))
    hs = lambda off: pl.BlockSpec((TQ, HEAD), lambda h, i, _c=off // HEAD: (i, _c + h))
    return qs, ks, hs


def _mla_attn_fwd(cfg, proj, q3, k3, v3):
    S, TQ, MH = cfg.S, cfg.TQ, cfg.MH
    qs, ks, hs = _mla_attn_specs(cfg)

    def body(q_ref, k_ref, v_ref, g_ref, o_ref, y_ref):
        p = _mla_probs(cfg, q_ref[...], k_ref[...], pl.program_id(1))
        o = jnp.dot(p.astype(bf16), v_ref[...], preferred_element_type=f32)
        o_ref[...] = o
        y_ref[...] = (o * _silu(g_ref[...])).astype(bf16)

    return pl.pallas_call(
        body, name="mla_attn_fwd", grid=(MH, S // TQ),
        in_specs=[qs(2 * HEAD), ks(2 * HEAD), ks(HEAD), hs(cfg.o_mg)],
        out_specs=(hs(0), hs(0)),
        out_shape=(jax.ShapeDtypeStruct((S, cfg.MW), f32), jax.ShapeDtypeStruct((S, cfg.MW), bf16)),
        compiler_params=_cparams(("parallel", "parallel")),
    )(q3, k3, v3, proj)


def _mla_attn_bwd(cfg, proj, q3, k3, v3, o, dy):
    S, TQ, MH = cfg.S, cfg.TQ, cfg.MH
    qs, ks, hs = _mla_attn_specs(cfg)

    def body(q_ref, k_ref, v_ref, g_ref, o_ref, dy_ref, dq_ref, dk_ref, dv_ref, dg_ref):
        i = pl.program_id(1)
        q, k, v = q_ref[...], k_ref[...], v_ref[...]
        gate, dy, o = g_ref[...], dy_ref[...], o_ref[...]
        dg_ref[...] = (dy * o * _dsilu(gate)).astype(bf16)
        dob = (dy * _silu(gate)).astype(bf16)
        p = _mla_probs(cfg, q, k, i)
        nt = (((1,), (1,)), ((), ()))
        tn = (((0,), (0,)), ((), ()))
        dv = lax.dot_general(p.astype(bf16), dob, tn, preferred_element_type=f32)
        dp = lax.dot_general(dob, v, nt, preferred_element_type=f32)
        ds = (p * (dp - jnp.sum(dp * p, axis=-1, keepdims=True)) * ((HEAD + ROPE) ** -0.5)).astype(bf16)
        dq_ref[...] = jnp.dot(ds, k, preferred_element_type=f32)
        dk = lax.dot_general(ds, q, tn, preferred_element_type=f32)

        @pl.when(i == 0)
        def _():
            dk_ref[...] = dk
            dv_ref[...] = dv

        @pl.when(i > 0)
        def _():
            dk_ref[...] += dk
            dv_ref[...] += dv

    return pl.pallas_call(
        body, name="mla_attn_bwd", grid=(MH, S // TQ),
        in_specs=[qs(2 * HEAD), ks(2 * HEAD), ks(HEAD), hs(cfg.o_mg), hs(0), hs(0)],
        out_specs=(qs(2 * HEAD), ks(2 * HEAD), ks(HEAD), hs(0)),
        out_shape=(jax.ShapeDtypeStruct((MH, S, 2 * HEAD), f32), jax.ShapeDtypeStruct((MH, S, 2 * HEAD), f32),
                   jax.ShapeDtypeStruct((MH, S, HEAD), f32), jax.ShapeDtypeStruct((S, cfg.MW), bf16)),
        compiler_params=_cparams(("parallel", "arbitrary")),
    )(q3, k3, v3, proj, o, dy)


def _pick_rows(R, bytes_per_row):
    if R * bytes_per_row <= MM_BUDGET:
        return R
    best = None
    for t in range(16, R, 16):
        if R % t == 0 and t * bytes_per_row <= MM_BUDGET:
            best = t
    assert best is not None, (R, bytes_per_row)
    return best


def _adamw(w, g, m, v, name="adamw"):
    R, C = w.shape
    tr = _pick_rows(R, C * 4 * 7 * 2)
    c1 =1.0 - ADAM_B1 ** ADAM_STEP
    c2 = 1.0 - ADAM_B2 ** ADAM_STEP

    def body(w_ref, g_ref, m_ref, v_ref, d_ref, mo_ref, vo_ref):
        g = g_ref[...]
        m = ADAM_B1 * m_ref[...] + (1.0 - ADAM_B1) * g
        v = ADAM_B2 * v_ref[...] + (1.0 - ADAM_B2) * jnp.square(g)
        d_ref[...] = -ADAM_LR * ((m / c1) / (jnp.sqrt(v / c2) + ADAM_EPS) + ADAM_WD * w_ref[...])
        mo_ref[...] = m
        vo_ref[...] = v

    spec = pl.BlockSpec((tr, C), lambda i: (i, 0))
    return pl.pallas_call(
        body, name=name, grid=(R // tr,), in_specs=[spec] * 4, out_specs=(spec,) * 3,
        out_shape=(jax.ShapeDtypeStruct((R, C), f32),) * 3,
        compiler_params=_cparams(("parallel",)),
    )(w, g, m, v)


def _adamw_big(w, m, v, mines, others, core, name):
    L, R, C = w.shape
    hr = R // 2
    tr = _pick_rows(hr, C * 4 * (7 + 2 * L) * 2)
    nt = hr // tr
    c1 = 1.0 - ADAM_B1 ** ADAM_STEP
    c2 = 1.0 - ADAM_B2 ** ADAM_STEP

    def body(core_ref, w_ref, m_ref, v_ref, *rest):
        g_refs, (go_ref, d_ref, mo_ref, vo_ref) = rest[:2 * L], rest[2 * L:]
        l, h = pl.program_id(0), pl.program_id(1)
        own = h == core_ref[0]
        g = jnp.where(own, g_refs[0][...], g_refs[L][...])
        for k in range(1, L):
            g = jnp.where(l == k, jnp.where(own, g_refs[k][...], g_refs[L + k][...]), g)
        m = ADAM_B1 * m_ref[...] + (1.0 - ADAM_B1) * g
        v = ADAM_B2 * v_ref[...] + (1.0 - ADAM_B2) * jnp.square(g)
        go_ref[...] = g
        d_ref[...] = -ADAM_LR * ((m / c1) / (jnp.sqrt(v / c2) + ADAM_EPS) + ADAM_WD * w_ref[...])
        mo_ref[...] = m
        vo_ref[...] = v

    lay = pl.BlockSpec((None, tr, C), lambda l, h, i, core_ref: (l, h * nt + i, 0))
    gspec = lambda k: pl.BlockSpec((tr, C), lambda l, h, i, core_ref: (jnp.where(l == k, i, 0), 0))
    return pl.pallas_call(
        body, name=name,
        grid_spec=pltpu.PrefetchScalarGridSpec(
            num_scalar_prefetch=1, grid=(L, 2, nt),
            in_specs=[lay, lay, lay] + [gspec(k) for k in range(L)] * 2, out_specs=(lay,) * 4),
        out_shape=(jax.ShapeDtypeStruct((L, R, C), f32),) * 4,
        compiler_params=_cparams(("arbitrary", "arbitrary", "arbitrary")),
    )(core, w, m, v, *mines, *others)


def _sum_blocks(x, out_dtype, name):
    n, R, C = x.shape
    tr = _pick_rows(R, C * 4 * (n + 1) * 2)

    def body(x_ref, o_ref):
        acc = x_ref[0].astype(f32)
        for k in range(1, n):
            acc = acc + x_ref[k].astype(f32)
        o_ref[...] = acc.astype(o_ref.dtype)

    return pl.pallas_call(
        body, name=name, grid=(R // tr,),
        in_specs=[pl.BlockSpec((n, tr, C), lambda i: (0, i, 0))], out_specs=pl.BlockSpec((tr, C), lambda i: (i, 0)),
        out_shape=jax.ShapeDtypeStruct((R, C), out_dtype),
        compiler_params=_cparams(("parallel",)),
    )(x)


def _hbm_specs(n):
    return [pl.BlockSpec(memory_space=pl.ANY)] * n


def _allgather8(shards, name):
    na = len(shards)

    def body(*refs):
        x_refs, out_refs = refs[:na], refs[na:2 * na]
        send_sems, recv_sems, local_sems = refs[2 * na:]
        x, y, c = lax.axis_index("x"), lax.axis_index("y"), lax.axis_index("c")
        me, sibling = (x, y, c), (x, y, 1 - c)
        chips = [(1 - x, y), (x, 1 - y), (1 - x, 1 - y)]

        def rows(a, px, py, pc):
            m = shards[a].shape[0]
            return out_refs[a].at[pl.ds((4 * px + 2 * py + pc) * m, m), :]

        def copy(a, k, block, to, src=None):
            return pltpu.make_async_remote_copy(
                src_ref=rows(a, *block) if src is None else src, dst_ref=rows(a, *block),
                send_sem=send_sems.at[a, k], recv_sem=recv_sems.at[a, k], device_id=to, device_id_type=MESH)

        mine = [pltpu.make_async_copy(x_refs[a], rows(a, *me), local_sems.at[a]) for a in range(na)]
        for cp in mine:
            cp.start()
        first = []
        for a in range(na):
            first.append(copy(a, 0, me, sibling, src=x_refs[a]))
            first += [copy(a, 1 + j, me, (*chip, c), src=x_refs[a]) for j, chip in enumerate(chips)]
        for cp in first:
            cp.start()
        passed = []
        for j, chip in enumerate(chips):
            for a in range(na):
                copy(a, 1 + j, (*chip, c), me).wait_recv()
                passed.append(copy(a, 4 + j, (*chip, c), sibling))
                passed[-1].start()
        for a in range(na):
            copy(a, 0, sibling, me).wait_recv()
        for j, chip in enumerate(chips):
            for a in range(na):
                copy(a, 4 + j, (*chip, 1 - c), me).wait_recv()
        for cp in first + passed:
            cp.wait_send()
        for cp in mine:
            cp.wait()

    return pl.pallas_call(
        body, name=name,
        out_shape=[jax.ShapeDtypeStruct((N_DEV * s.shape[0], s.shape[1]), s.dtype) for s in shards],
        in_specs=_hbm_specs(na), out_specs=_hbm_specs(na),
        scratch_shapes=[pltpu.SemaphoreType.DMA((na, 7)), pltpu.SemaphoreType.DMA((na, 7)), pltpu.SemaphoreType.DMA((na,))],
    )(*shards)


def _send_sibling(arrays, name):
    na = len(arrays)

    def body(*refs):
        x_refs, out_refs = refs[:na], refs[na:2 * na]
        send_sems, recv_sems = refs[2 * na:]
        sibling = (lax.axis_index("x"), lax.axis_index("y"), 1 - lax.axis_index("c"))
        cps = [pltpu.make_async_remote_copy(src_ref=x_refs[a], dst_ref=out_refs[a], send_sem=send_sems.at[a],
                                            recv_sem=recv_sems.at[a], device_id=sibling, device_id_type=MESH)
               for a in range(na)]
        for cp in cps:
            cp.start()
        for cp in cps:
            cp.wait()

    return pl.pallas_call(
        body, name=name, out_shape=[jax.ShapeDtypeStruct(x.shape, x.dtype) for x in arrays],
        in_specs=_hbm_specs(na), out_specs=_hbm_specs(na),
        scratch_shapes=[pltpu.SemaphoreType.DMA((na,)), pltpu.SemaphoreType.DMA((na,))],
    )(*arrays)


def _scatter_chips(arrays, name):
    na = len(arrays)

    def body(*refs):
        p_refs, out_refs = refs[:na], refs[na:2 * na]
        send_sems, recv_sems, local_sems = refs[2 * na:]
        x, y, c = lax.axis_index("x"), lax.axis_index("y"), lax.axis_index("c")
        mychip = 2 * x + y
        chips = [(1 - x, y), (x, 1 - y), (1 - x, 1 - y)]
        mine = [pltpu.make_async_copy(p_refs[a].at[mychip], out_refs[a].at[mychip], local_sems.at[a]) for a in range(na)]
        cps = [pltpu.make_async_remote_copy(src_ref=p_refs[a].at[2 * cx + cy], dst_ref=out_refs[a].at[mychip],
                                            send_sem=send_sems.at[a, j], recv_sem=recv_sems.at[a, j],
                                            device_id=(cx, cy, c), device_id_type=MESH)
               for j, (cx, cy) in enumerate(chips) for a in range(na)]
        for cp in mine + cps:
            cp.start()
        for j, (cx, cy) in enumerate(chips):
            for a in range(na):
                pltpu.make_async_remote_copy(src_ref=p_refs[a].at[mychip], dst_ref=out_refs[a].at[2 * cx + cy],
                                             send_sem=send_sems.at[a, j], recv_sem=recv_sems.at[a, j],
                                             device_id=(cx, cy, c), device_id_type=MESH).wait_recv()
        for cp in cps:
            cp.wait_send()
        for cp in mine:
            cp.wait()

    return pl.pallas_call(
        body, name=name, out_shape=[jax.ShapeDtypeStruct(p.shape, p.dtype) for p in arrays],
        in_specs=_hbm_specs(na), out_specs=_hbm_specs(na),
        scratch_shapes=[pltpu.SemaphoreType.DMA((na, 3)), pltpu.SemaphoreType.DMA((na, 3)), pltpu.SemaphoreType.DMA((na,))],
    )(*arrays)


def _add2(a, b, out_dtype, name):
    R, C = a.shape
    tr = _pick_rows(R, C * 4 * 3 * 2)

    def body(a_ref, b_ref, o_ref):
        o_ref[...] = (a_ref[...].astype(f32) + b_ref[...].astype(f32)).astype(o_ref.dtype)

    spec = pl.BlockSpec((tr, C), lambda i: (i, 0))
    return pl.pallas_call(body, name=name, grid=(R // tr,), in_specs=[spec, spec], out_specs=spec,
                          out_shape=jax.ShapeDtypeStruct((R, C), out_dtype), compiler_params=_cparams(("parallel",)))(a, b)


def _reduce_scatter(grads):
    c = lax.axis_index("c")
    shp = [g.shape[2:] for g in grads]
    keep = [lax.dynamic_index_in_dim(g, c, axis=1, keepdims=False).reshape(4 * hr, nc) for g, (hr, nc) in zip(grads, shp)]
    give = [lax.dynamic_index_in_dim(g, 1 - c, axis=1, keepdims=False).reshape(4 * hr, nc) for g, (hr, nc) in zip(grads, shp)]
    got = _send_sibling(give, "rs_pair")
    part = [_add2(k, g, bf16, "rs_add_pair").reshape(4, hr, nc) for k, g, (hr, nc) in zip(keep, got, shp)]
    slots = _scatter_chips(part, "rs_chips")
    mine = [_sum_blocks(s, f32, "rs_add_chips") for s in slots]
    return mine, _send_sibling(mine, "rs_halves")


def _big_weights(cfg):
    return (("w_in", cfg.D, cfg.IN_WIDTH, 1), ("mla_w_uq", cfg.QL, cfg.QW, 1), ("mla_w_ukv", cfg.KL, cfg.KVW, 1),
            ("w_branch", cfg.RW + cfg.LW + cfg.MW, cfg.D, 0), ("w_out", cfg.D, cfg.D, 0))


def _half_shapes(cfg):
    out = []
    for _, r, c, ax in _big_weights(cfg):
        out.append((r // 2, c // 4) if ax == 1 else (r // 8, c))
    return out


def _my_halves(cfg, shards, c):
    return [lax.dynamic_slice_in_dim(w, c * hr, hr, axis=0).astype(bf16) for w, (hr, nc) in zip(shards, _half_shapes(cfg))]


def _cols(blocks, a, b):
    nc = blocks.shape[2]
    out = []
    for q in range(blocks.shape[0]):
        lo, hi = max(a, q * nc), min(b, (q + 1) * nc)
        if lo < hi:
            out.append(blocks[q, :, lo - q * nc:hi - q * nc])
    return out


def _w_in_padded(cfg, blocks):
    k0 = cfg.o_mg
    parts = _cols(blocks, 0, k0) + _cols(blocks, k0 + ROPE, cfg.IN_WIDTH) + _cols(blocks, k0, k0 + ROPE)
    return jnp.concatenate(parts + [jnp.zeros((blocks.shape[1], cfg.NP - cfg.IN_WIDTH), blocks.dtype)], axis=1)


def _w_in_blocks(cfg, gp):
    k0, nc = cfg.o_mg, cfg.IN_WIDTH // 4

    def orig(a, b):
        segs = ((0, k0, 0), (k0, k0 + ROPE, cfg.o_mkr), (k0 + ROPE, cfg.IN_WIDTH, k0))
        out = []
        for s0, s1, p0 in segs:
            lo, hi = max(a, s0), min(b, s1)
            if lo < hi:
                out.append(gp[:, p0 + lo - s0:p0 + hi - s0])
        return out

    return jnp.stack([jnp.concatenate(orig(q * nc, (q + 1) * nc), axis=1) for q in range(4)])


def _uq_split(cfg, w):
    hw = HEAD + ROPE
    return jnp.concatenate([w[:, h * hw:h * hw + HEAD] for h in range(cfg.MH)]
                           + [w[:, h * hw + HEAD:(h + 1) * hw] for h in range(cfg.MH)], axis=1)


def _uq_join(cfg, g):
    n = cfg.MH * HEAD
    parts = []
    for h in range(cfg.MH):
        parts += [g[:, h * HEAD:(h + 1) * HEAD], g[:, n + h * ROPE:n + (h + 1) * ROPE]]
    return jnp.concatenate(parts, axis=1)


def _col_blocks(g):
    nc = g.shape[1] // 4
    return jnp.stack([g[:, q * nc:(q + 1) * nc] for q in range(4)])


def _row_pack(parts):
    rows = []
    for p in parts:
        r = p.reshape(-1, LANES)
        pad = -r.shape[0] % SUBLANES
        rows.append(jnp.concatenate([r, jnp.zeros((pad, LANES), r.dtype)], axis=0) if pad else r)
    return jnp.concatenate(rows, axis=0)


def _row_unpack(packed, like):
    out, off = [], 0
    for p in like:
        n = p.size // LANES
        out.append(packed[off:off + n].reshape(p.shape))
        off += -(-n // SUBLANES) * SUBLANES
    return out


def _prep_layer(cfg, full, small):
    w_in, w_uq, w_ukv, w_branch, w_out = full
    RW, LW = cfg.RW, cfg.LW
    P = dict(small)
    P["w_in"] = _w_in_padded(cfg, w_in.reshape(4, cfg.D, -1))
    P["w_uq"] = _uq_split(cfg, jnp.concatenate(list(w_uq.reshape(4, cfg.QL, -1)), axis=1))
    w_ukv = jnp.concatenate(list(w_ukv.reshape(4, cfg.KL, -1)), axis=1)
    P["w_ukv"] = w_ukv
    P["wb"] = (w_branch[:RW], w_branch[RW:RW + LW], w_branch[RW + LW:])
    P["w_out"] = w_out
    P["w_in_t"] = P["w_in"].T
    P["w_uq_t"] = P["w_uq"].T
    P["w_ukv_t"] = w_ukv.T
    P["wb_t"] = tuple(w.T for w in P["wb"])
    P["w_out_t"] = w_out.T
    return P


def _layer_fwd(cfg, x, mod, P, T):
    h = _prenorm_fwd(cfg, x, mod, P["norm_pre"])
    proj = _mm(h, P["w_in"], f32, "mm_proj")
    y_ret = _ret_fwd(cfg, proj, P["ret_gn"], T["cos_r"], T["sin_r"], T["ret_consts"])
    a, b = _lru_gates(cfg, proj, P["lru_conv_w"], P["lru_conv_b"], P["lru_wa"], P["lru_ba"], P["lru_wx"], P["lru_bx"],
                      P["lru_lambda"])
    hl, y_lru = _lru_scan_fwd(cfg, proj, a, b)
    qn, kn = _mla_norm(cfg, proj, P["mla_q_norm"], P["mla_kv_norm"])
    q = _mm(qn, P["w_uq"], f32, "mm_uq")
    kv = _mm(kn, P["w_ukv"], f32, "mm_ukv")
    q3, k3, v3 = _mla_pack(cfg, proj, q, kv, T["cos_q"], T["sin_q"], T["cos_k"], T["sin_k"])
    o, y_mla = _mla_attn_fwd(cfg, proj, q3, k3, v3)
    ys = (y_ret, y_lru, y_mla)
    us = tuple(_mm(yb, wb, f32, "mm_branch") for yb, wb in zip(ys, P["wb"]))
    merged = _merge_fwd(cfg, proj, *us)
    y = _mm(merged, P["w_out"], f32, "mm_out")
    out = _postnorm_fwd(cfg, x, y, mod, P["norm_post"])
    R = dict(x=x, h=h, proj=proj, ys=ys, a=a, hl=hl, qn=qn, kn=kn, q3=q3, k3=k3, v3=v3, o=o, us=us, merged=merged, y=y)
    return out, R


def _layer_bwd(cfg, dout, R, mod, P, T):
    proj = R["proj"]
    dy, s_post = _postnorm_bwd(cfg, dout, R["y"], mod, P["norm_post"])
    dmerged = _mm(dy, P["w_out_t"], f32, "mm_dmerged")
    g_out = _mm(R["merged"].T, dy, bf16, "mm_gw_out")
    du0, du1, du2, dlog = _merge_bwd(cfg, proj, dmerged, *R["us"])
    dus = (du0, du1, du2)
    dys = tuple(_mm(du, wt, f32, "mm_dbranch") for du, wt in zip(dus, P["wb_t"]))
    g_branch = jnp.concatenate([_mm(yb.T, du, bf16, "mm_gw_branch") for yb, du in zip(R["ys"], dus)], axis=0)
    drq, drk, drv, drg, dgn = _ret_bwd(cfg, proj, dys[0], P["ret_gn"], T["cos_r"], T["sin_r"], T["ret_consts"])
    da, db, dlg = _lru_scan_bwd(cfg, proj, R["a"], R["hl"], dys[1])
    dxc, dwa, dwx, s_lru = _lru_gates_bwd(cfg, proj, da, db, P["lru_conv_w"], P["lru_conv_b"], P["lru_wa"], P["lru_ba"],
                                          P["lru_wx"], P["lru_bx"], P["lru_lambda"])
    dlx, s_conv = _lru_conv_bwd(cfg, proj, dxc, P["lru_conv_w"])
    dq3, dk3, dv3, dmg = _mla_attn_bwd(cfg, proj, R["q3"], R["k3"], R["v3"], R["o"], dys[2])
    dq, dkv, dmkr = _mla_unpack_bwd(cfg, dq3, dk3, dv3, T["cos_q"], T["sin_q"], T["cos_k"], T["sin_k"])
    dqn = _mm(dq, P["w_uq_t"], f32, "mm_dqn")
    dkn = _mm(dkv, P["w_ukv_t"], f32, "mm_dkn")
    g_uq = _uq_join(cfg, _mm(R["qn"].T, dq, bf16, "mm_gw_uq"))
    g_ukv = _mm(R["kn"].T, dkv, bf16, "mm_gw_ukv")
    dmq, dmkv, s_q, s_k = _mla_norm_bwd(cfg, proj, dqn, dkn, P["mla_q_norm"], P["mla_kv_norm"])
    dproj = jnp.concatenate([drq, drk, drv, drg, dlx, dlg, dmq, dmkv, dmg, dlog, dmkr,
                             jnp.zeros((cfg.S, cfg.NP - cfg.o_mkr - HEAD), bf16)], axis=1)
    dh = _mm(dproj, P["w_in_t"], f32, "mm_dh")
    g_in = _w_in_blocks(cfg, _mm(R["h"].T, dproj, bf16, "mm_gw_in"))
    dx, s_pre = _prenorm_bwd(cfg, R["x"], dh, dout, mod, P["norm_pre"])
    big = [g_in, _col_blocks(g_uq), _col_blocks(g_ukv), g_branch, g_out]
    big = [g.reshape(4, 2, hr, nc) for g, (hr, nc) in zip(big, _half_shapes(cfg))]
    small = dict(norm_pre=s_pre[2:3], norm_post=s_post[1:2], ret_gn=dgn, lru_conv_w=s_conv[0:CONV], lru_conv_b=s_conv[CONV:CONV + 1],
                 lru_wa=dwa, lru_ba=s_lru[0:1], lru_wx=dwx, lru_bx=s_lru[1:2], lru_lambda=s_lru[2:3],
                 mla_q_norm=s_q[0:1], mla_kv_norm=s_k[0:1])
    dmod = jnp.concatenate([s_pre[0:1], s_pre[1:2], s_post[0:1]], axis=1)
    return dx, big, small, dmod


_SMALL = ("norm_pre", "norm_post", "ret_gn", "lru_conv_w", "lru_conv_b", "lru_wa", "lru_ba", "lru_wx", "lru_bx", "lru_lambda",
          "mla_q_norm", "mla_kv_norm")
_WEIGHTS = ("ada_w", "ada_b", "norm_pre", "norm_post", "w_in", "ret_gn", "lru_conv_w", "lru_conv_b", "lru_wa", "lru_ba", "lru_wx",
            "lru_bx", "lru_lambda", "mla_q_norm", "mla_w_uq", "mla_kv_norm", "mla_w_ukv", "w_branch", "w_out")


def _step(cfg, x, c, positions, W, target, M1, V1):
    L, D = cfg.L, cfg.D
    xi, yi, ci = lax.axis_index("x"), lax.axis_index("y"), lax.axis_index("c")
    chip = 2 * xi + yi
    me = 2 * chip + ci

    c8 = jnp.concatenate([c, jnp.zeros((SUBLANES - 1, D), f32)], axis=0)
    c_all = _allgather8([c8], "gather_c")[0].reshape(N_DEV, SUBLANES, D)[:, 0]
    mod_sh, c_act = _ada_fwd(cfg, c_all, W["ada_w"])
    n_sh = mod_sh.shape[2]
    mod_half = lax.dynamic_slice_in_dim(mod_sh, ci * (n_sh // 2), n_sh // 2, axis=2).reshape(L * N_DEV, n_sh // 2)
    mod_all = _allgather8([mod_half], "gather_mod")[0].reshape(N_DEV, L, N_DEV, n_sh // 2)
    mod_all = mod_all.transpose(1, 2, 0, 3).reshape(L, N_DEV, 3 * D)
    mods = lax.dynamic_index_in_dim(mod_all, me, axis=1, keepdims=False) + W["ada_b"]

    (cos_r, sin_r), (cos_m, sin_m) = _rope_tables(cfg, positions)
    T = dict(cos_r=cos_r, sin_r=sin_r, cos_q=jnp.tile(cos_m, (1, cfg.MH)), sin_q=jnp.tile(sin_m, (1, cfg.MH)),
             cos_k=jnp.tile(cos_m, (1, 2)), sin_k=jnp.tile(sin_m, (1, 2)), ret_consts=_ret_consts(cfg))

    Ps, Rs = [], []
    act = x[0]
    for l in range(L):
        shards = [W[name][l] for name, *_ in _big_weights(cfg)]
        gathered = _allgather8(_my_halves(cfg, shards, ci), "gather_w")
        small = {k: (W[k][l] if W[k][l].ndim > 1 else W[k][l][None, :]) for k in _SMALL if k != "lru_conv_w"}
        P = _prep_layer(cfg, gathered, small)
        P["lru_conv_w"] = None
        Ps.append(P)
    cw_all = _allgather8([_pad_rows(W["lru_conv_w"].reshape(L * CONV, -1))], "gather_conv")[0]
    cw_rows = cw_all.shape[0] // N_DEV
    cw_all = cw_all.reshape(4, 2, cw_rows, -1)[:, 0, :L * CONV].transpose(1, 0, 2).reshape(L, CONV, cfg.LW)
    for l in range(L):
        Ps[l]["lru_conv_w"] = cw_all[l]
    for l in range(L):
        act, R = _layer_fwd(cfg, act, mods[l:l + 1], Ps[l], T)
        Rs.append(R)

    dact, lsum = _loss_head(cfg, act, target[0])
    loss = lax.psum(lsum[0, 0], ("x", "y", "c"))

    big_g = [None] * L
    small_g = [None] * L
    dmods = [None] * L
    for l in range(L - 1, -1, -1):
        dact, grads, small_g[l], dmods[l] = _layer_bwd(cfg, dact, Rs[l], mods[l:l + 1], Ps[l], T)
        big_g[l] = _reduce_scatter(grads)

    dmod = jnp.concatenate(dmods, axis=0)
    parts = [dmod] + [small_g[l][k] for l in range(L) for k in _SMALL]
    packed = _row_pack(parts)
    allf = _allgather8([packed], "gather_small")[0].reshape(N_DEV, packed.shape[0], LANES)
    summed = _row_unpack(_sum_blocks(allf, f32, "sum_small"), parts)
    gsm = {k: jnp.stack([summed[1 + l * len(_SMALL) + i].reshape(W[k].shape[1:] if k != "lru_conv_w" else (CONV, cfg.LW))
                         for l in range(L)]) for i, k in enumerate(_SMALL)}
    ncw = cfg.LW // 4
    gsm["lru_conv_w"] = lax.dynamic_slice_in_dim(gsm["lru_conv_w"], chip * ncw, ncw, axis=2)
    gsm["ada_b"] = summed[0]
    dmod_all = allf[:, :dmod.size // LANES].reshape(N_DEV, L, 3 * D)
    dmod_sh = lax.dynamic_slice_in_dim(dmod_all, chip * n_sh, n_sh, axis=2).transpose(1, 0, 2)
    G = dict(gsm)
    G["ada_w"] = _ada_bwd(cfg, c_act.T, dmod_sh)
    delta, new_m, new_v = {}, {}, {}
    core = ci.astype(jnp.int32).reshape(1)
    for i, (name, *_) in enumerate(_big_weights(cfg)):
        G[name], delta[name], new_m[name], new_v[name] = _adamw_big(
            W[name], M1[name], V1[name], [big_g[l][0][i] for l in range(L)], [big_g[l][1][i] for l in range(L)], core,
            "adamw_" + name)
    bigs = ("ada_w",) + tuple(name for name, *_ in _big_weights(cfg))
    shp = W["ada_w"].shape
    two = lambda a: a.reshape(-1, shp[-1])
    d, m_, v_ = _adamw(two(W["ada_w"]), two(G["ada_w"]), two(M1["ada_w"]), two(V1["ada_w"]), "adamw_ada_w")
    delta["ada_w"], new_m["ada_w"], new_v["ada_w"] = d.reshape(shp), m_.reshape(shp), v_.reshape(shp)
    smalls = [k for k in _WEIGHTS if k not in bigs]
    packs = [_row_pack([src[k] for k in smalls]) for src in (W, G, M1, V1)]
    outs = _adamw(*packs, "adamw_small")
    for dst, o in zip((delta, new_m, new_v), outs):
        for k, val in zip(smalls, _row_unpack(o, [W[k] for k in smalls])):
            dst[k] = val

    grad_x = dact[None]
    return (loss, grad_x, *[G[k] for k in _WEIGHTS], *[delta[k] for k in _WEIGHTS], *[new_m[k] for k in _WEIGHTS],
            *[new_v[k] for k in _WEIGHTS])


def _pad_rows(a):
    pad = -a.shape[0] % SUBLANES
    return jnp.concatenate([a, jnp.zeros((pad, a.shape[1]), a.dtype)], axis=0) if pad else a


def kernel(x, c, positions, ada_w, ada_b, norm_pre, norm_post, w_in, ret_gn, lru_conv_w, lru_conv_b, lru_wa, lru_ba, lru_wx, lru_bx, lru_lambda, mla_q_norm, mla_w_uq, mla_kv_norm, mla_w_ukv, w_branch, w_out, loss_target, m_ada_w, m_ada_b, m_norm_pre, m_norm_post, m_w_in, m_ret_gn, m_lru_conv_w, m_lru_conv_b, m_lru_wa, m_lru_ba, m_lru_wx, m_lru_bx, m_lru_lambda, m_mla_q_norm, m_mla_w_uq, m_mla_kv_norm, m_mla_w_ukv, m_w_branch, m_w_out, v_ada_w, v_ada_b, v_norm_pre, v_norm_post, v_w_in, v_ret_gn, v_lru_conv_w, v_lru_conv_b, v_lru_wa, v_lru_ba, v_lru_wx, v_lru_bx, v_lru_lambda, v_mla_q_norm, v_mla_w_uq, v_mla_kv_norm, v_mla_w_ukv, v_w_branch, v_w_out):
    W = dict(ada_w=ada_w, ada_b=ada_b, norm_pre=norm_pre, norm_post=norm_post, w_in=w_in, ret_gn=ret_gn, lru_conv_w=lru_conv_w,
             lru_conv_b=lru_conv_b, lru_wa=lru_wa, lru_ba=lru_ba, lru_wx=lru_wx, lru_bx=lru_bx, lru_lambda=lru_lambda,
             mla_q_norm=mla_q_norm, mla_w_uq=mla_w_uq, mla_kv_norm=mla_kv_norm, mla_w_ukv=mla_w_ukv, w_branch=w_branch, w_out=w_out)
    M1 = dict(ada_w=m_ada_w, ada_b=m_ada_b, norm_pre=m_norm_pre, norm_post=m_norm_post, w_in=m_w_in, ret_gn=m_ret_gn,
              lru_conv_w=m_lru_conv_w, lru_conv_b=m_lru_conv_b, lru_wa=m_lru_wa, lru_ba=m_lru_ba, lru_wx=m_lru_wx, lru_bx=m_lru_bx,
              lru_lambda=m_lru_lambda, mla_q_norm=m_mla_q_norm, mla_w_uq=m_mla_w_uq, mla_kv_norm=m_mla_kv_norm,
              mla_w_ukv=m_mla_w_ukv, w_branch=m_w_branch, w_out=m_w_out)
    V1 = dict(ada_w=v_ada_w, ada_b=v_ada_b, norm_pre=v_norm_pre, norm_post=v_norm_post, w_in=v_w_in, ret_gn=v_ret_gn,
              lru_conv_w=v_lru_conv_w, lru_conv_b=v_lru_conv_b, lru_wa=v_lru_wa, lru_ba=v_lru_ba, lru_wx=v_lru_wx, lru_bx=v_lru_bx,
              lru_lambda=v_lru_lambda, mla_q_norm=v_mla_q_norm, mla_w_uq=v_mla_w_uq, mla_kv_norm=v_mla_kv_norm,
              mla_w_ukv=v_mla_w_ukv, w_branch=v_w_branch, w_out=v_w_out)
    return _step(_CFG, x, c, positions, W, loss_target, M1, V1)
```

```python
import functools
import math
from typing import NamedTuple

import numpy as np
import jax
import jax.numpy as jnp
from jax import lax
from jax.experimental import pallas as pl
from jax.experimental.pallas import tpu as pltpu

f32 = jnp.float32
bf16 = jnp.bfloat16

NORM_EPS = 1e-6
ROPE_BASE = 10000.0
CHUNK = 64
HEAD = 128
ROPE = 64
CONV = 4
LRU_C = 8.0
ADAM_LR, ADAM_B1, ADAM_B2, ADAM_EPS, ADAM_WD, ADAM_STEP = 0.001, 0.9, 0.999, 1e-08, 0.01, 10

LANES = 128
SUBLANES = 8
VMEM_LIMIT = 56 * 1024 * 1024
MM_BUDGET = 40 * 1024 * 1024
N_DEV = 8
MESH = pl.DeviceIdType.MESH


class Cfg(NamedTuple):
    D: int = 2048
    S: int = 2048
    L: int = 4
    H: int = 8
    NB: int = 8
    MH: int = 8
    QL: int = 512
    KL: int = 512
    TR: int = 256
    TQ: int = 256

    @property
    def RW(self): return self.H * HEAD
    @property
    def LW(self): return self.NB * HEAD
    @property
    def MW(self): return self.MH * HEAD
    @property
    def o_rk(self): return self.RW
    @property
    def o_rv(self): return 2 * self.RW
    @property
    def o_rg(self): return 3 * self.RW
    @property
    def o_lx(self): return 4 * self.RW
    @property
    def o_lg(self): return 4 * self.RW + self.LW
    @property
    def o_mq(self): return 4 * self.RW + 2 * self.LW
    @property
    def o_mkv(self): return self.o_mq + self.QL
    @property
    def o_mg(self): return self.o_mkv + self.KL
    @property
    def o_merge(self): return self.o_mg + self.MW
    @property
    def o_mkr(self): return self.o_merge + 3 * self.D
    @property
    def NP(self): return -(-(self.o_mkr + ROPE) // 512) * 512
    @property
    def IN_WIDTH(self): return self.o_mkr + ROPE
    @property
    def QW(self): return self.MH * (HEAD + ROPE)
    @property
    def KVW(self): return self.MH * 2 * HEAD


_CFG = Cfg()


def _cparams(sem=None):
    return pltpu.CompilerParams(dimension_semantics=sem, vmem_limit_bytes=VMEM_LIMIT)


def _sigmoid(x):
    return jax.nn.sigmoid(x)


def _silu(x):
    return x * _sigmoid(x)


def _dsilu(x):
    s = _sigmoid(x)
    return s * (1.0 + x * (1.0 - s))


def _slab(rows, width, off):
    assert off % width == 0
    return pl.BlockSpec((rows, width), lambda i, _c=off // width: (i, _c))


def _row(width):
    return pl.BlockSpec((1, width), lambda i: (0, 0))


def _mm(a, b, out_dtype=f32, name="mm", mode="nn", tm=None):
    (M, K) = a.shape if mode != "tn" else a.shape[::-1]
    (K2, N) = b.shape if mode != "nt" else b.shape[::-1]
    assert K == K2
    tn = N if N <= 2048 else 512
    tk = K if K <= 2048 else 512
    assert N % tn == 0 and K % tk == 0
    osz = jnp.dtype(out_dtype).itemsize
    if tm is None:
        tm = M
        while 2 * tm * tk * 2 + 2 * tk * tn * 2 + 2 * tm * tn * osz + tm * tn * 4 > MM_BUDGET and tm % 16 == 0:
            tm //= 2
    assert M % tm == 0
    nk = K // tk
    dims = {"nn": (((1,), (0,)), ((), ())), "nt": (((1,), (1,)), ((), ())), "tn": (((0,), (0,)), ((), ()))}[mode]

    def dot(a_ref, b_ref):
        return lax.dot_general(a_ref[...].astype(bf16), b_ref[...].astype(bf16), dims, preferred_element_type=f32)

    if nk == 1:
        def body(a_ref, b_ref, o_ref):
            o_ref[...] = dot(a_ref, b_ref).astype(o_ref.dtype)
        scratch = []
    else:
        def body(a_ref, b_ref, o_ref, acc_ref):
            k = pl.program_id(2)

            @pl.when(k == 0)
            def _():
                acc_ref[...] = jnp.zeros_like(acc_ref)

            acc_ref[...] += dot(a_ref, b_ref)

            @pl.when(k == nk - 1)
            def _():
                o_ref[...] = acc_ref[...].astype(o_ref.dtype)
        scratch = [pltpu.VMEM((tm, tn), f32)]

    a_spec = pl.BlockSpec((tk, tm), lambda i, j, k: (k, i)) if mode == "tn" else pl.BlockSpec((tm, tk), lambda i, j, k: (i, k))
    b_spec = pl.BlockSpec((tn, tk), lambda i, j, k: (j, k)) if mode == "nt" else pl.BlockSpec((tk, tn), lambda i, j, k: (k, j))
    return pl.pallas_call(
        body, name=name,
        grid=(M // tm, N // tn, nk),
        in_specs=[a_spec, b_spec],
        out_specs=pl.BlockSpec((tm, tn), lambda i, j, k: (i, j)),
        out_shape=jax.ShapeDtypeStruct((M, N), out_dtype),
        scratch_shapes=scratch,
        compiler_params=_cparams(("parallel", "parallel", "arbitrary")),
    )(a, b)


def _ada_fwd(cfg, c_all, ada_w):
    L, D, n = ada_w.shape
    tn = n // 2 if (n // 2) % LANES == 0 else n

    def body(c_ref, w_ref, o_ref, ca_ref):
        ca = _silu(c_ref[...])
        ca_ref[...] = ca
        o_ref[0] = jnp.dot(ca.astype(bf16), w_ref[0].astype(bf16), preferred_element_type=f32)

    return pl.pallas_call(
        body, name="ada_fwd", grid=(L, n // tn),
        in_specs=[pl.BlockSpec((N_DEV, D), lambda l, j: (0, 0)), pl.BlockSpec((1, D, tn), lambda l, j: (l, 0, j))],
        out_specs=(pl.BlockSpec((1, N_DEV, tn), lambda l, j: (l, 0, j)), pl.BlockSpec((N_DEV, D), lambda l, j: (0, 0))),
        out_shape=(jax.ShapeDtypeStruct((L, N_DEV, n), f32), jax.ShapeDtypeStruct((N_DEV, D), f32)),
        compiler_params=_cparams(("arbitrary", "arbitrary")),
    )(c_all, ada_w)


def _ada_bwd(cfg, c_act_t, dmod):
    L, _, n = dmod.shape
    D = c_act_t.shape[0]
    tn = n // 2 if (n // 2) % LANES == 0 else n

    def body(c_ref, d_ref, o_ref):
        o_ref[0] = jnp.dot(c_ref[...].astype(bf16), d_ref[0].astype(bf16), preferred_element_type=f32)

    return pl.pallas_call(
        body, name="ada_bwd", grid=(L, n // tn),
        in_specs=[pl.BlockSpec((D, N_DEV), lambda l, j: (0, 0)), pl.BlockSpec((1, N_DEV, tn), lambda l, j: (l, 0, j))],
        out_specs=pl.BlockSpec((1, D, tn), lambda l, j: (l, 0, j)),
        out_shape=jax.ShapeDtypeStruct((L, D, n), f32),
        compiler_params=_cparams(("parallel", "parallel")),
    )(c_act_t, dmod)


def _prenorm_fwd(cfg, x, mod, gain):
    S, D, TR = cfg.S, cfg.D, cfg.TR

    def body(x_ref, mod_ref, g_ref, h_ref):
        x = x_ref[...]
        r = lax.rsqrt(jnp.mean(x * x, axis=-1, keepdims=True) + NORM_EPS)
        shift, scale = mod_ref[:, 0:D], mod_ref[:, D:2 * D]
        h_ref[...] = ((x * r) * g_ref[...] * (1.0 + scale) + shift).astype(bf16)

    return pl.pallas_call(
        body, name="prenorm_fwd", grid=(S // TR,),
        in_specs=[_slab(TR, D, 0), _row(3 * D), _row(D)],
        out_specs=_slab(TR, D, 0), out_shape=jax.ShapeDtypeStruct((S, D), bf16),
        compiler_params=_cparams(("parallel",)),
    )(x, mod, gain)


def _prenorm_bwd(cfg, x, dh, dres, mod, gain):
    S, D, TR = cfg.S, cfg.D, cfg.TR

    def body(x_ref, dh_ref, dres_ref, mod_ref, g_ref, dx_ref, sum_ref):
        i = pl.program_id(0)
        x, dh, g = x_ref[...], dh_ref[...], g_ref[...]
        scale = mod_ref[:, D:2 * D]
        r = lax.rsqrt(jnp.mean(x * x, axis=-1, keepdims=True) + NORM_EPS)
        xn = x * r
        t = dh * xn
        dxn = dh * (g * (1.0 + scale))
        dx_ref[...] = r * (dxn - xn * jnp.mean(dxn * xn, axis=-1, keepdims=True)) + dres_ref[...]
        part = jnp.concatenate([jnp.sum(dh, axis=0, keepdims=True), jnp.sum(t * g, axis=0, keepdims=True),
                                jnp.sum(t * (1.0 + scale), axis=0, keepdims=True), jnp.zeros((SUBLANES - 3, D), f32)], axis=0)

        @pl.when(i == 0)
        def _():
            sum_ref[...] = part

        @pl.when(i > 0)
        def _():
            sum_ref[...] += part

    return pl.pallas_call(
        body, name="prenorm_bwd", grid=(S // TR,),
        in_specs=[_slab(TR, D, 0), _slab(TR, D, 0), _slab(TR, D, 0), _row(3 * D), _row(D)],
        out_specs=(_slab(TR, D, 0), pl.BlockSpec((SUBLANES, D), lambda i: (0, 0))),
        out_shape=(jax.ShapeDtypeStruct((S, D), f32), jax.ShapeDtypeStruct((SUBLANES, D), f32)),
        compiler_params=_cparams(("arbitrary",)),
    )(x, dh, dres, mod, gain)


def _postnorm_fwd(cfg, x, y, mod, gain):
    S, D, TR = cfg.S, cfg.D, cfg.TR

    def body(x_ref, y_ref, mod_ref, g_ref, o_ref):
        y = y_ref[...]
        r = lax.rsqrt(jnp.mean(y * y, axis=-1, keepdims=True) + NORM_EPS)
        rg = mod_ref[:, 2 * D:3 * D]
        o_ref[...] = x_ref[...] + (1.0 + rg) * ((y * r) * g_ref[...])

    return pl.pallas_call(
        body, name="postnorm_fwd", grid=(S // TR,),
        in_specs=[_slab(TR, D, 0), _slab(TR, D, 0), _row(3 * D), _row(D)],
        out_specs=_slab(TR, D, 0), out_shape=jax.ShapeDtypeStruct((S, D), f32),
        compiler_params=_cparams(("parallel",)),
    )(x, y, mod, gain)


def _postnorm_bwd(cfg, dout, y, mod, gain):
    S, D, TR = cfg.S, cfg.D, cfg.TR

    def body(do_ref, y_ref, mod_ref, g_ref, dy_ref, sum_ref):
        i = pl.program_id(0)
        do, y, g = do_ref[...], y_ref[...], g_ref[...]
        rg = mod_ref[:, 2 * D:3 * D]
        r = lax.rsqrt(jnp.mean(y * y, axis=-1, keepdims=True) + NORM_EPS)
        yn = y * r
        t = do * yn
        dyn = do * ((1.0 + rg) * g)
        dy_ref[...] = (r * (dyn - yn * jnp.mean(dyn * yn, axis=-1, keepdims=True))).astype(bf16)
        part = jnp.concatenate([jnp.sum(t * g, axis=0, keepdims=True), jnp.sum(t * (1.0 + rg), axis=0, keepdims=True),
                                jnp.zeros((SUBLANES - 2, D), f32)], axis=0)

        @pl.when(i == 0)
        def _():
            sum_ref[...] = part

        @pl.when(i > 0)
        def _():
            sum_ref[...] += part

    return pl.pallas_call(
        body, name="postnorm_bwd", grid=(S // TR,),
        in_specs=[_slab(TR, D, 0), _slab(TR, D, 0), _row(3 * D), _row(D)],
        out_specs=(_slab(TR, D, 0), pl.BlockSpec((SUBLANES, D), lambda i: (0, 0))),
        out_shape=(jax.ShapeDtypeStruct((S, D), bf16), jax.ShapeDtypeStruct((SUBLANES, D), f32)),
        compiler_params=_cparams(("arbitrary",)),
    )(dout, y, mod, gain)


def _loss_head(cfg, y, target):
    S, D, TR = cfg.S, cfg.D, cfg.TR

    def body(y_ref, t_ref, d_ref, l_ref):
        i = pl.program_id(0)
        err = y_ref[...] - t_ref[...]
        d_ref[...] = err / D
        part = jnp.zeros((SUBLANES, LANES), f32) + 0.5 * jnp.sum(jnp.mean(err * err, axis=-1, keepdims=True))

        @pl.when(i == 0)
        def _():
            l_ref[...] = part

        @pl.when(i > 0)
        def _():
            l_ref[...] += part

    return pl.pallas_call(
        body, name="loss_head", grid=(S // TR,),
        in_specs=[_slab(TR, D, 0), _slab(TR, D, 0)],
        out_specs=(_slab(TR, D, 0), pl.BlockSpec((SUBLANES, LANES), lambda i: (0, 0))),
        out_shape=(jax.ShapeDtypeStruct((S, D), f32), jax.ShapeDtypeStruct((SUBLANES, LANES), f32)),
        compiler_params=_cparams(("arbitrary",)),
    )(y, target)


def _merge_fwd(cfg, proj, u0, u1, u2):
    S, D, TR = cfg.S, cfg.D, cfg.TR

    def body(l0, l1, l2, u0_ref, u1_ref, u2_ref, o_ref):
        o_ref[...] = (_sigmoid(l0[...]) * u0_ref[...] + _sigmoid(l1[...]) * u1_ref[...]
                      + _sigmoid(l2[...]) * u2_ref[...]).astype(bf16)

    return pl.pallas_call(
        body, name="merge_fwd", grid=(S // TR,),
        in_specs=[_slab(TR, D, cfg.o_merge + b * D) for b in range(3)] + [_slab(TR, D, 0)] * 3,
        out_specs=_slab(TR, D, 0), out_shape=jax.ShapeDtypeStruct((S, D), bf16),
        compiler_params=_cparams(("parallel",)),
    )(proj, proj, proj, u0, u1, u2)


def _merge_bwd(cfg, proj, dmerged, u0, u1, u2):
    S, D, TR = cfg.S, cfg.D, cfg.TR

    def body(l0, l1, l2, dm_ref, u0_ref, u1_ref, u2_ref, du0, du1, du2, dl_ref):
        dm = dm_ref[...]
        for b, (l, u, du) in enumerate(((l0, u0_ref, du0), (l1, u1_ref, du1), (l2, u2_ref, du2))):
            g = _sigmoid(l[...])
            du[...] = (dm * g).astype(bf16)
            dl_ref[:, b * D:(b + 1) * D] = (dm * u[...] * (g * (1.0 - g))).astype(bf16)

    return pl.pallas_call(
        body, name="merge_bwd", grid=(S // TR,),
        in_specs=[_slab(TR, D, cfg.o_merge + b * D) for b in range(3)] + [_slab(TR, D, 0)] * 4,
        out_specs=(_slab(TR, D, 0),) * 3 + (_slab(TR, 3 * D, 0),),
        out_shape=(jax.ShapeDtypeStruct((S, D), bf16),) * 3 + (jax.ShapeDtypeStruct((S, 3 * D), bf16),),
        compiler_params=_cparams(("parallel",)),
    )(proj, proj, proj, dmerged, u0, u1, u2)


def _rope128(x, c, s):
    return x * c + pltpu.roll(x, 64, axis=1) * s


def _rope128_t(dy, c, s):
    return dy * c + pltpu.roll(dy * s, 64, axis=1)


def _swap32(x):
    w = x.shape[1]
    lane = lax.broadcasted_iota(jnp.int32, x.shape, 1)
    return jnp.where((lane % 64) < 32, pltpu.roll(x, w - 32, axis=1), pltpu.roll(x, 32, axis=1))


def _rope64(x, c, s):
    return x * c + _swap32(x) * s


def _rope64_t(dy, c, s):
    return dy * c + _swap32(dy * s)


def _rope_tables(cfg, positions):
    pos = positions.astype(f32)[0][:, None]

    def tab(dim):
        inv_freq = ROPE_BASE ** (-jnp.arange(0, dim, 2, dtype=f32) / dim)
        ang = pos * inv_freq
        cos, sin = jnp.cos(ang), jnp.sin(ang)
        return jnp.concatenate([cos, cos], axis=1), jnp.concatenate([-sin, sin], axis=1)

    return tab(HEAD), tab(ROPE)


def _ret_consts(cfg):
    h = np.arange(cfg.H, dtype=np.float64)
    log_gamma = np.log1p(-np.exp2(-5.0 - h)).astype(np.float32)
    idx = np.arange(CHUNK, dtype=np.float32)
    intra = np.exp(log_gamma[:, None, None] * np.abs(idx[:, None] - idx[None, :]))
    kdec = np.exp(log_gamma[:, None] * (CHUNK - 1 - idx)[None, :])
    qdec = np.exp(log_gamma[:, None] * (idx + 1.0)[None, :])
    cdec = np.exp(log_gamma * CHUNK)
    bc = lambda a: jnp.asarray(np.broadcast_to(a[..., None], a.shape + (HEAD,)).astype(np.float32))
    return jnp.asarray(intra.astype(np.float32)), bc(kdec), bc(qdec), bc(cdec[:, None])


def _ret_core(cfg, q_raw, k_raw, v_raw, cos, sin, intra, kdec, qdec, cdec, p_ref):
    S = cfg.S
    NC = S // CHUNK
    q = _rope128(q_raw, cos, sin) * (HEAD ** -0.5)
    k = _rope128(k_raw, cos, sin)
    q3 = q.reshape(NC, CHUNK, HEAD)
    k3 = k.reshape(NC, CHUNK, HEAD)
    qb, kb = q3.astype(bf16), k3.astype(bf16)
    vb = v_raw.reshape(NC, CHUNK, HEAD).astype(bf16)
    sdb = (jnp.einsum('nid,njd->nij', qb, kb, preferred_element_type=f32) * intra[None]).astype(bf16)
    o_intra = jnp.einsum('nij,nje->nie', sdb, vb, preferred_element_type=f32)
    kdb = (k3 * kdec[None]).astype(bf16)
    kv = jnp.einsum('njd,nje->nde', kdb, vb, preferred_element_type=f32)
    p_ref[0] = jnp.zeros((HEAD, HEAD), f32)
    for n in range(1, NC):
        p_ref[n] = p_ref[n - 1] * cdec + kv[n - 1]
    pb = p_ref[...].astype(bf16)
    qdb = (q3 * qdec[None]).astype(bf16)
    o_inter = jnp.einsum('nid,nde->nie', qdb, pb, preferred_element_type=f32)
    o = (o_intra + o_inter).reshape(S, HEAD)
    return o, (qb, kb, vb, sdb, kdb, qdb, pb)


def _ret_specs(cfg):
    S = cfg.S
    hs = lambda off: pl.BlockSpec((S, HEAD), lambda h, _c=off // HEAD: (0, _c + h))
    full = pl.BlockSpec((S, HEAD), lambda h: (0, 0))
    consts = [pl.BlockSpec((None, CHUNK, CHUNK), lambda h: (h, 0, 0)), pl.BlockSpec((None, CHUNK, HEAD), lambda h: (h, 0, 0)),
              pl.BlockSpec((None, CHUNK, HEAD), lambda h: (h, 0, 0)), pl.BlockSpec((None, 1, HEAD), lambda h: (h, 0, 0))]
    gn = pl.BlockSpec((1, HEAD), lambda h: (0, h))
    return hs, full, consts, gn


def _ret_fwd(cfg, proj, gn, cos, sin, consts):
    S, NC = cfg.S, cfg.S // CHUNK
    hs, full, cspecs, gspec = _ret_specs(cfg)

    def body(q_ref, k_ref, v_ref, g_ref, gn_ref, cos_ref, sin_ref, intra, kdec, qdec, cdec, y_ref, p_ref):
        o, _ = _ret_core(cfg, q_ref[...], k_ref[...], v_ref[...], cos_ref[...], sin_ref[...],
                         intra[...], kdec[...], qdec[...], cdec[...], p_ref)
        mean = jnp.mean(o, axis=-1, keepdims=True)
        var = jnp.mean(jnp.square(o - mean), axis=-1, keepdims=True)
        z = ((o - mean) * lax.rsqrt(var + NORM_EPS)) * gn_ref[...]
        y_ref[...] = (z * _silu(g_ref[...])).astype(bf16)

    return pl.pallas_call(
        body, name="ret_fwd", grid=(cfg.H,),
        in_specs=[hs(0), hs(cfg.o_rk), hs(cfg.o_rv), hs(cfg.o_rg), gspec, full, full] + cspecs,
        out_specs=hs(0), out_shape=jax.ShapeDtypeStruct((S, cfg.RW), bf16),
        scratch_shapes=[pltpu.VMEM((NC, HEAD, HEAD), f32)],
        compiler_params=_cparams(("arbitrary",)),
    )(proj, proj, proj, proj, gn, cos, sin, *consts)


def _ret_bwd(cfg, proj, dy, gn, cos, sin, consts):
    S, NC = cfg.S, cfg.S // CHUNK
    hs, full, cspecs, gspec = _ret_specs(cfg)

    def body(q_ref, k_ref, v_ref, g_ref, dy_ref, gn_ref, cos_ref, sin_ref, intra_ref, kdec_ref, qdec_ref, cdec_ref,
             dq_ref, dk_ref, dv_ref, dg_ref, dgn_ref, p_ref, g_scr):
        cos, sin = cos_ref[...], sin_ref[...]
        intra, kdec, qdec, cdec = intra_ref[...], kdec_ref[...], qdec_ref[...], cdec_ref[...]
        o, (qb, kb, vb, sdb, kdb, qdb, pb) = _ret_core(cfg, q_ref[...], k_ref[...], v_ref[...], cos, sin,
                                                     intra, kdec, qdec, cdec, p_ref)
        gate, dy, gnv = g_ref[...], dy_ref[...], gn_ref[...]
        mean = jnp.mean(o, axis=-1, keepdims=True)
        rstd = lax.rsqrt(jnp.mean(jnp.square(o - mean), axis=-1, keepdims=True) + NORM_EPS)
        on = (o - mean) * rstd
        dz = dy * _silu(gate)
        dg_ref[...] = (dy * (on * gnv) * _dsilu(gate)).astype(bf16)
        dgn_ref[...] = jnp.sum(dz * on, axis=0, keepdims=True)
        don = dz * gnv
        do = rstd * (don - jnp.mean(don, axis=-1, keepdims=True) - on * jnp.mean(don * on, axis=-1, keepdims=True))
        dob = do.reshape(NC, CHUNK, HEAD).astype(bf16)
        dsb = (jnp.einsum('nie,nje->nij', dob, vb, preferred_element_type=f32) * intra[None]).astype(bf16)
        dv = jnp.einsum('nij,nie->nje', sdb, dob, preferred_element_type=f32)
        dq = jnp.einsum('nij,njd->nid', dsb, kb, preferred_element_type=f32)
        dk = jnp.einsum('nij,nid->njd', dsb, qb, preferred_element_type=f32)
        dq = dq + jnp.einsum('nie,nde->nid', dob, pb, preferred_element_type=f32) * qdec[None]
        dp = jnp.einsum('nid,nie->nde', qdb, dob, preferred_element_type=f32)
        g_scr[NC - 1] = jnp.zeros((HEAD, HEAD), f32)
        for n in range(NC - 2, -1, -1):
            g_scr[n] = dp[n + 1] + g_scr[n + 1] * cdec
        gb = g_scr[...].astype(bf16)
        dk = dk + jnp.einsum('nje,nde->njd', vb, gb, preferred_element_type=f32) * kdec[None]
        dv = dv + jnp.einsum('njd,nde->nje', kdb, gb, preferred_element_type=f32)
        dq_ref[...] = _rope128_t(dq.reshape(S, HEAD) * (HEAD ** -0.5), cos, sin).astype(bf16)
        dk_ref[...] = _rope128_t(dk.reshape(S, HEAD), cos, sin).astype(bf16)
        dv_ref[...] = dv.reshape(S, HEAD).astype(bf16)

    return pl.pallas_call(
        body, name="ret_bwd", grid=(cfg.H,),
        in_specs=[hs(0), hs(cfg.o_rk), hs(cfg.o_rv), hs(cfg.o_rg), hs(0), gspec, full, full] + cspecs,
        out_specs=(hs(0),) * 4 + (gspec,),
        out_shape=(jax.ShapeDtypeStruct((S, cfg.RW), bf16),) * 4 + (jax.ShapeDtypeStruct((1, cfg.RW), f32),),
        scratch_shapes=[pltpu.VMEM((NC, HEAD, HEAD), f32), pltpu.VMEM((NC, HEAD, HEAD), f32)],
        compiler_params=_cparams(("arbitrary",)),
    )(proj, proj, proj, proj, dy, gn, cos, sin, *consts)


def _expm1(x):
    small = x * (1.0 + x * (0.5 + x * (1.0 / 6.0 + x * (1.0 / 24.0 + x * (1.0 / 120.0)))))
    return jnp.where(jnp.abs(x) < 0.1, small, jnp.exp(x) - 1.0)


def _softplus(z):
    return jnp.maximum(z, 0.0) + jnp.log1p(jnp.exp(-jnp.abs(z)))


def _lru_conv(cfg, x_ref, halo_ref, cw, scr, first):
    TR = cfg.TR
    scr[0:SUBLANES, :] = jnp.where(first, 0.0, halo_ref[...])
    scr[SUBLANES:SUBLANES + TR, :] = x_ref[...]
    xc = scr[pl.ds(SUBLANES - (CONV - 1), TR), :] * cw[0:1, :]
    for j in range(1, CONV):
        xc = xc + scr[pl.ds(SUBLANES - (CONV - 1) + j, TR), :] * cw[j:j + 1, :]
    return xc


def _lru_pre(cfg, xc, wa_ref, wx_ref, ba, bx):
    xb = xc.astype(bf16)
    pa = jnp.concatenate([jnp.dot(xb[:, n * HEAD:(n + 1) * HEAD], wa_ref[n].astype(bf16), preferred_element_type=f32)
                          for n in range(cfg.NB)], axis=1) + ba
    px = jnp.concatenate([jnp.dot(xb[:, n * HEAD:(n + 1) * HEAD], wx_ref[n].astype(bf16), preferred_element_type=f32)
                          for n in range(cfg.NB)], axis=1) + bx
    return pa, px


def _lru_ab(pa, px, xc, lam):
    r, i = _sigmoid(pa), _sigmoid(px)
    log_a = (-LRU_C * r) * _softplus(-lam)
    a = jnp.exp(log_a)
    b = jnp.sqrt(-_expm1(2.0 * log_a)) * (i * xc)
    return a, b


def _lru_halo_specs(cfg, off, W):
    TR, S = cfg.TR, cfg.S
    nb = TR // SUBLANES
    cb = off // W
    main = pl.BlockSpec((TR, W), lambda i: (i, cb))
    prev = pl.BlockSpec((SUBLANES, W), lambda i: (jnp.maximum(i * nb - 1, 0), cb))
    nxt = pl.BlockSpec((SUBLANES, W), lambda i: (jnp.minimum((i + 1) * nb, S // SUBLANES - 1), cb))
    return main, prev, nxt


def _lru_gates(cfg, proj, cw, cb, wa, ba, wx, bx, lam):
    S, W, TR, NB = cfg.S, cfg.LW, cfg.TR, cfg.NB
    assert cfg.o_lx % W == 0
    main, prev, _ = _lru_halo_specs(cfg, cfg.o_lx, W)
    wspec = pl.BlockSpec((NB, HEAD, HEAD), lambda i: (0, 0, 0))

    def body(x_ref, halo_ref, cw_ref, cb_ref, wa_ref, ba_ref, wx_ref, bx_ref, lam_ref, a_ref, b_ref, scr):
        xc = _lru_conv(cfg, x_ref, halo_ref, cw_ref[...], scr, pl.program_id(0) == 0) + cb_ref[...]
        pa, px = _lru_pre(cfg, xc, wa_ref, wx_ref, ba_ref[...], bx_ref[...])
        a, b = _lru_ab(pa, px, xc, lam_ref[...])
        a_ref[...] = a
        b_ref[...] = b

    return pl.pallas_call(
        body, name="lru_gates", grid=(S // TR,),
        in_specs=[main, prev, pl.BlockSpec((CONV, W), lambda i: (0, 0)), _row(W), wspec, _row(W), wspec, _row(W), _row(W)],
        out_specs=(_slab(TR, W, 0),) * 2, out_shape=(jax.ShapeDtypeStruct((S, W), f32),) * 2,
        scratch_shapes=[pltpu.VMEM((TR + SUBLANES, W), f32)],
        compiler_params=_cparams(("parallel",)),
    )(proj, proj, cw, cb, wa, ba, wx, bx, lam)


def _lru_lane_block(cfg):
    return 256 if cfg.LW % 256 == 0 else LANES


def _lru_scan_fwd(cfg, proj, a, b):
    S, W = cfg.S, cfg.LW
    LB = _lru_lane_block(cfg)
    assert cfg.o_lg % LB == 0
    col = lambda off: pl.BlockSpec((S, LB), lambda j, _c=off // LB: (0, _c + j))

    def body(a_ref, b_ref, g_ref, h_ref, y_ref):
        def blk(t, h):
            r0 = pl.multiple_of(t * SUBLANES, SUBLANES)
            at, bt = a_ref[pl.ds(r0, SUBLANES), :], b_ref[pl.ds(r0, SUBLANES), :]
            rows = []
            for j in range(SUBLANES):
                h = at[j:j + 1, :] * h + bt[j:j + 1, :]
                rows.append(h)
            h_ref[pl.ds(r0, SUBLANES), :] = jnp.concatenate(rows, axis=0)
            return h

        lax.fori_loop(0, S // SUBLANES, blk, jnp.zeros((1, LB), f32))
        y_ref[...] = (h_ref[...] * _silu(g_ref[...])).astype(bf16)

    return pl.pallas_call(
        body, name="lru_scan_fwd", grid=(W // LB,),
        in_specs=[col(0), col(0), col(cfg.o_lg)],
        out_specs=(col(0), col(0)),
        out_shape=(jax.ShapeDtypeStruct((S, W), f32), jax.ShapeDtypeStruct((S, W), bf16)),
        compiler_params=_cparams(("parallel",)),
    )(a, b, proj)


def _lru_scan_bwd(cfg, proj, a, h, dy):
    S, W = cfg.S, cfg.LW
    LB = _lru_lane_block(cfg)
    col = lambda off: pl.BlockSpec((S, LB), lambda j, _c=off // LB: (0, _c + j))

    def body(a_ref, h_ref, dy_ref, g_ref, da_ref, db_ref, dg_ref):
        gate, dy = g_ref[...], dy_ref[...]
        dg_ref[...] = (dy * h_ref[...] * _dsilu(gate)).astype(bf16)
        da_ref[...] = dy * _silu(gate)

        def blk(t, carry):
            dh_next, a_next = carry
            r0 = pl.multiple_of((S // SUBLANES - 1 - t) * SUBLANES, SUBLANES)
            at, ct = a_ref[pl.ds(r0, SUBLANES), :], da_ref[pl.ds(r0, SUBLANES), :]
            rows = [None] * SUBLANES
            for j in range(SUBLANES - 1, -1, -1):
                dh_next = ct[j:j + 1, :] + a_next * dh_next
                a_next = at[j:j + 1, :]
                rows[j] = dh_next
            db_ref[pl.ds(r0, SUBLANES), :] = jnp.concatenate(rows, axis=0)
            return dh_next, a_next

        z = jnp.zeros((1, LB), f32)
        lax.fori_loop(0, S // SUBLANES, blk, (z, z))
        row = lax.broadcasted_iota(jnp.int32, (S, LB), 0)
        hprev = jnp.where(row == 0, 0.0, pltpu.roll(h_ref[...], 1, axis=0))
        da_ref[...] = db_ref[...] * hprev

    return pl.pallas_call(
        body, name="lru_scan_bwd", grid=(W // LB,),
        in_specs=[col(0), col(0), col(0), col(cfg.o_lg)],
        out_specs=(col(0),) * 3,
        out_shape=(jax.ShapeDtypeStruct((S, W), f32),) * 2 + (jax.ShapeDtypeStruct((S, W), bf16),),
        compiler_params=_cparams(("parallel",)),
    )(a, h, dy, proj)


def _lru_gates_bwd(cfg, proj, da, db, cw, cb, wa, ba, wx, bx, lam):
    S, W, TR, NB = cfg.S, cfg.LW, cfg.TR, cfg.NB
    main, prev, _ = _lru_halo_specs(cfg, cfg.o_lx, W)
    wspec = pl.BlockSpec((NB, HEAD, HEAD), lambda i: (0, 0, 0))

    def body(x_ref, halo_ref, da_ref, db_ref, cw_ref, cb_ref, wa_ref, ba_ref, wx_ref, bx_ref, lam_ref,
             dxc_ref, dwa_ref, dwx_ref, sum_ref, scr):
        i = pl.program_id(0)
        lam = lam_ref[...]
        xc = _lru_conv(cfg, x_ref, halo_ref, cw_ref[...], scr, i == 0) + cb_ref[...]
        pa, px = _lru_pre(cfg, xc, wa_ref, wx_ref, ba_ref[...], bx_ref[...])
        _, vjp = jax.vjp(_lru_ab, pa, px, xc, lam)
        dpa, dpx, dxc, dlam = vjp((da_ref[...], db_ref[...]))
        xb, dpab, dpxb = xc.astype(bf16), dpa.astype(bf16), dpx.astype(bf16)
        nt = (((1,), (1,)), ((), ()))
        tn = (((0,), (0,)), ((), ()))
        back = []
        dwa, dwx = [], []
        for n in range(NB):
            sl = slice(n * HEAD, (n + 1) * HEAD)
            back.append(lax.dot_general(dpab[:, sl], wa_ref[n].astype(bf16), nt, preferred_element_type=f32)
                        + lax.dot_general(dpxb[:, sl], wx_ref[n].astype(bf16), nt, preferred_element_type=f32))
            dwa.append(lax.dot_general(xb[:, sl], dpab[:, sl], tn, preferred_element_type=f32))
            dwx.append(lax.dot_general(xb[:, sl], dpxb[:, sl], tn, preferred_element_type=f32))
        dxc_ref[...] = dxc + jnp.concatenate(back, axis=1)
        part = jnp.concatenate([jnp.sum(dpa, axis=0, keepdims=True), jnp.sum(dpx, axis=0, keepdims=True), dlam,
                                jnp.zeros((SUBLANES - 3, W), f32)], axis=0)

        @pl.when(i == 0)
        def _():
            sum_ref[...] = part
            for n in range(NB):
                dwa_ref[n] = dwa[n]
                dwx_ref[n] = dwx[n]

        @pl.when(i > 0)
        def _():
            sum_ref[...] += part
            for n in range(NB):
                dwa_ref[n] += dwa[n]
                dwx_ref[n] += dwx[n]

    return pl.pallas_call(
        body, name="lru_gates_bwd", grid=(S // TR,),
        in_specs=[main, prev, _slab(TR, W, 0), _slab(TR, W, 0), pl.BlockSpec((CONV, W), lambda i: (0, 0)), _row(W),
                  wspec, _row(W), wspec, _row(W), _row(W)],
        out_specs=(_slab(TR, W, 0), wspec, wspec, pl.BlockSpec((SUBLANES, W), lambda i: (0, 0))),
        out_shape=(jax.ShapeDtypeStruct((S, W), f32), jax.ShapeDtypeStruct((NB, HEAD, HEAD), f32),
                   jax.ShapeDtypeStruct((NB, HEAD, HEAD), f32), jax.ShapeDtypeStruct((SUBLANES, W), f32)),
        scratch_shapes=[pltpu.VMEM((TR + SUBLANES, W), f32)],
        compiler_params=_cparams(("arbitrary",)),
    )(proj, proj, da, db, cw, cb, wa, ba, wx, bx, lam)


def _lru_conv_bwd(cfg, proj, dxc, cw):
    S, W, TR = cfg.S, cfg.LW, cfg.TR
    main, prev, _ = _lru_halo_specs(cfg, cfg.o_lx, W)
    dmain, _, dnext = _lru_halo_specs(cfg, 0, W)

    def body(x_ref, xhalo_ref, d_ref, dhalo_ref, cw_ref, dx_ref, sum_ref, xs, ds):
        i = pl.program_id(0)
        cw = cw_ref[...]
        d = d_ref[...]
        xs[0:SUBLANES, :] = jnp.where(i == 0, 0.0, xhalo_ref[...])
        xs[SUBLANES:SUBLANES + TR, :] = x_ref[...]
        ds[0:TR, :] = d
        ds[TR:TR + SUBLANES, :] = jnp.where(i == pl.num_programs(0) - 1, 0.0, dhalo_ref[...])
        dx = ds[pl.ds(CONV - 1, TR), :] * cw[0:1, :]
        parts = [jnp.sum(d * xs[pl.ds(SUBLANES - (CONV - 1), TR), :], axis=0, keepdims=True)]
        for j in range(1, CONV):
            dx = dx + ds[pl.ds(CONV - 1 - j, TR), :] * cw[j:j + 1, :]
            parts.append(jnp.sum(d * xs[pl.ds(SUBLANES - (CONV - 1) + j, TR), :], axis=0, keepdims=True))
        dx_ref[...] = dx.astype(bf16)
        part = jnp.concatenate(parts + [jnp.sum(d, axis=0, keepdims=True), jnp.zeros((SUBLANES - CONV - 1, W), f32)], axis=0)

        @pl.when(i == 0)
        def _():
            sum_ref[...] = part

        @pl.when(i > 0)
        def _():
            sum_ref[...] += part

    return pl.pallas_call(
        body, name="lru_conv_bwd", grid=(S // TR,),
        in_specs=[main, prev, dmain, dnext, pl.BlockSpec((CONV, W), lambda i: (0, 0))],
        out_specs=(_slab(TR, W, 0), pl.BlockSpec((SUBLANES, W), lambda i: (0, 0))),
        out_shape=(jax.ShapeDtypeStruct((S, W), bf16), jax.ShapeDtypeStruct((SUBLANES, W), f32)),
        scratch_shapes=[pltpu.VMEM((TR + SUBLANES, W), f32), pltpu.VMEM((TR + SUBLANES, W), f32)],
        compiler_params=_cparams(("arbitrary",)),
    )(proj, proj, dxc, dxc, cw)


def _rms(x, g):
    r = lax.rsqrt(jnp.mean(x * x, axis=-1, keepdims=True) + NORM_EPS)
    return (x * r) * g, r


def _mla_norm(cfg, proj, qg, kg):
    S, TR = cfg.S, cfg.TR

    def body(q_ref, k_ref, qg_ref, kg_ref, qn_ref, kn_ref):
        qn_ref[...] = _rms(q_ref[...], qg_ref[...])[0].astype(bf16)
        kn_ref[...] = _rms(k_ref[...], kg_ref[...])[0].astype(bf16)

    return pl.pallas_call(
        body, name="mla_norm", grid=(S // TR,),
        in_specs=[_slab(TR, cfg.QL, cfg.o_mq), _slab(TR, cfg.KL, cfg.o_mkv), _row(cfg.QL), _row(cfg.KL)],
        out_specs=(_slab(TR, cfg.QL, 0), _slab(TR, cfg.KL, 0)),
        out_shape=(jax.ShapeDtypeStruct((S, cfg.QL), bf16), jax.ShapeDtypeStruct((S, cfg.KL), bf16)),
        compiler_params=_cparams(("parallel",)),
    )(proj, proj, qg, kg)


def _mla_norm_bwd(cfg, proj, dqn, dkn, qg, kg):
    S, TR = cfg.S, cfg.TR

    def one(x, g, dn):
        r = lax.rsqrt(jnp.mean(x * x, axis=-1, keepdims=True) + NORM_EPS)
        xn = x * r
        dxn = dn * g
        dx = r * (dxn - xn * jnp.mean(dxn * xn, axis=-1, keepdims=True))
        return dx, jnp.sum(dn * xn, axis=0, keepdims=True)

    def body(q_ref, k_ref, dq_ref, dk_ref, qg_ref, kg_ref, dmq_ref, dmk_ref, sq_ref, sk_ref):
        i = pl.program_id(0)
        dq, gq = one(q_ref[...], qg_ref[...], dq_ref[...])
        dk, gk = one(k_ref[...], kg_ref[...], dk_ref[...])
        dmq_ref[...] = dq.astype(bf16)
        dmk_ref[...] = dk.astype(bf16)
        pq = jnp.concatenate([gq, jnp.zeros((SUBLANES - 1, cfg.QL), f32)], axis=0)
        pk = jnp.concatenate([gk, jnp.zeros((SUBLANES - 1, cfg.KL), f32)], axis=0)

        @pl.when(i == 0)
        def _():
            sq_ref[...] = pq
            sk_ref[...] = pk

        @pl.when(i > 0)
        def _():
            sq_ref[...] += pq
            sk_ref[...] += pk

    return pl.pallas_call(
        body, name="mla_norm_bwd", grid=(S // TR,),
        in_specs=[_slab(TR, cfg.QL, cfg.o_mq), _slab(TR, cfg.KL, cfg.o_mkv), _slab(TR, cfg.QL, 0), _slab(TR, cfg.KL, 0),
                  _row(cfg.QL), _row(cfg.KL)],
        out_specs=(_slab(TR, cfg.QL, 0), _slab(TR, cfg.KL, 0), pl.BlockSpec((SUBLANES, cfg.QL), lambda i: (0, 0)),
                   pl.BlockSpec((SUBLANES, cfg.KL), lambda i: (0, 0))),
        out_shape=(jax.ShapeDtypeStruct((S, cfg.QL), bf16), jax.ShapeDtypeStruct((S, cfg.KL), bf16),
                   jax.ShapeDtypeStruct((SUBLANES, cfg.QL), f32), jax.ShapeDtypeStruct((SUBLANES, cfg.KL), f32)),
        compiler_params=_cparams(("arbitrary",)),
    )(proj, proj, dqn, dkn, qg, kg)


def _mla_pack(cfg, proj, q, kv, cq, sq, ck, sk):
    S, TR, MH = cfg.S, cfg.TR, cfg.MH
    NW, RWD = MH * HEAD, MH * ROPE

    def body(q_ref, kv_ref, kr_ref, cq_ref, sq_ref, ck_ref, sk_ref, qo_ref, ko_ref, vo_ref):
        q, kv = q_ref[...], kv_ref[...]
        qr = _rope64(q[:, NW:], cq_ref[...], sq_ref[...])
        kr = _rope64(kr_ref[...], ck_ref[...], sk_ref[...]).astype(bf16)
        lane = lax.broadcasted_iota(jnp.int32, (TR, HEAD), 1)
        for h in range(MH):
            grp = qr[:, (h // 2) * HEAD:(h // 2 + 1) * HEAD]
            if h % 2:
                grp = pltpu.roll(grp, 64, axis=1)
            qo_ref[h] = jnp.concatenate([q[:, h * HEAD:(h + 1) * HEAD], jnp.where(lane < ROPE, grp, 0.0)], axis=1).astype(bf16)
            ko_ref[h] = jnp.concatenate([kv[:, 2 * h * HEAD:(2 * h + 1) * HEAD].astype(bf16), kr], axis=1)
            vo_ref[h] = kv[:, (2 * h + 1) * HEAD:(2 * h + 2) * HEAD].astype(bf16)

    hspec = lambda w: pl.BlockSpec((MH, TR, w), lambda i: (0, i, 0))
    return pl.pallas_call(
        body, name="mla_pack", grid=(S // TR,),
        in_specs=[_slab(TR, cfg.QW, 0), _slab(TR, cfg.KVW, 0), _slab(TR, HEAD, cfg.o_mkr),
                  _slab(TR, RWD, 0), _slab(TR, RWD, 0), _slab(TR, HEAD, 0), _slab(TR, HEAD, 0)],
        out_specs=(hspec(2 * HEAD), hspec(2 * HEAD), hspec(HEAD)),
        out_shape=(jax.ShapeDtypeStruct((MH, S, 2 * HEAD), bf16), jax.ShapeDtypeStruct((MH, S, 2 * HEAD), bf16),
                   jax.ShapeDtypeStruct((MH, S, HEAD), bf16)),
        compiler_params=_cparams(("parallel",)),
    )(q, kv, proj, cq, sq, ck, sk)


def _mla_unpack_bwd(cfg, dq3, dk3, dv3, cq, sq, ck, sk):
    S, TR, MH = cfg.S, cfg.TR, cfg.MH
    RWD = MH * ROPE

    def body(dq_ref, dk_ref, dv_ref, cq_ref, sq_ref, ck_ref, sk_ref, q_ref, kv_ref, kr_ref):
        lane = lax.broadcasted_iota(jnp.int32, (TR, HEAD), 1)
        nope, ropes, kvs = [], [], []
        dkr = jnp.zeros((TR, HEAD), f32)
        for h in range(MH):
            dq = dq_ref[h]
            nope.append(dq[:, :HEAD])
            part = jnp.where(lane < ROPE, dq[:, HEAD:], 0.0)
            if h % 2:
                ropes[-1] = ropes[-1] + pltpu.roll(part, 64, axis=1)
            else:
                ropes.append(part)
            dk = dk_ref[h]
            kvs += [dk[:, :HEAD], dv_ref[h]]
            dkr = dkr + dk[:, HEAD:]
        dqr = _rope64_t(jnp.concatenate(ropes, axis=1), cq_ref[...], sq_ref[...])
        q_ref[...] = jnp.concatenate(nope + [dqr], axis=1).astype(bf16)
        kv_ref[...] = jnp.concatenate(kvs, axis=1).astype(bf16)
        dkr = jnp.where(lane < ROPE, dkr, 0.0)
        kr_ref[...] = _rope64_t(dkr, ck_ref[...], sk_ref[...]).astype(bf16)

    hspec = lambda w: pl.BlockSpec((MH, TR, w), lambda i: (0, i, 0))
    return pl.pallas_call(
        body, name="mla_unpack_bwd", grid=(S // TR,),
        in_specs=[hspec(2 * HEAD), hspec(2 * HEAD), hspec(HEAD), _slab(TR, RWD, 0), _slab(TR, RWD, 0),
                  _slab(TR, HEAD, 0), _slab(TR, HEAD, 0)],
        out_specs=(_slab(TR, cfg.QW, 0), _slab(TR, cfg.KVW, 0), _slab(TR, HEAD, 0)),
        out_shape=(jax.ShapeDtypeStruct((S, cfg.QW), bf16), jax.ShapeDtypeStruct((S, cfg.KVW), bf16),
                   jax.ShapeDtypeStruct((S, HEAD), bf16)),
        compiler_params=_cparams(("parallel",)),
    )(dq3, dk3, dv3, cq, sq, ck, sk)


def _mla_probs(cfg, q, k, i):
    TQ, S = cfg.TQ, cfg.S
    nt = (((1,), (1,)), ((), ()))
    s = lax.dot_general(q, k, nt, preferred_element_type=f32) * ((HEAD + ROPE) ** -0.5)
    qc = (i * TQ + lax.broadcasted_iota(jnp.int32, (TQ, S), 0)) // CHUNK
    kc = lax.broadcasted_iota(jnp.int32, (TQ, S), 1) // CHUNK
    s = jnp.where(kc <= qc, s, -1e30)
    m = jnp.max(s, axis=-1, keepdims=True)
    e = jnp.exp(s - m)
    return e / jnp.sum(e, axis=-1, keepdims=True)


def _mla_attn_specs(cfg):
    S, TQ = cfg.S, cfg.TQ
    qs = lambda w: pl.BlockSpec((None, TQ, w), lambda h, i: (h, i, 0))
    ks = lambda w: pl.BlockSpec((None, S, w), lambda h, i: (h, 0, 0))
    hs = lambda off: pl.BlockSpec((TQ, HEAD), lambda h, i, _c=off // HEAD: (i, _c + h))
    return qs, ks, hs


def _mla_attn_fwd(cfg, proj, q3, k3, v3):
    S, TQ, MH = cfg.S, cfg.TQ, cfg.MH
    qs, ks, hs = _mla_attn_specs(cfg)

    def body(q_ref, k_ref, v_ref, g_ref, o_ref, y_ref):
        p = _mla_probs(cfg, q_ref[...], k_ref[...], pl.program_id(1))
        o = jnp.dot(p.astype(bf16), v_ref[...], preferred_element_type=f32)
        o_ref[...] = o
        y_ref[...] = (o * _silu(g_ref[...])).astype(bf16)

    return pl.pallas_call(
        body, name="mla_attn_fwd", grid=(MH, S // TQ),
        in_specs=[qs(2 * HEAD), ks(2 * HEAD), ks(HEAD), hs(cfg.o_mg)],
        out_specs=(hs(0), hs(0)),
        out_shape=(jax.ShapeDtypeStruct((S, cfg.MW), f32), jax.ShapeDtypeStruct((S, cfg.MW), bf16)),
        compiler_params=_cparams(("parallel", "parallel")),
    )(q3, k3, v3, proj)


def _mla_attn_bwd(cfg, proj, q3, k3, v3, o, dy):
    S, TQ, MH = cfg.S, cfg.TQ, cfg.MH
    qs, ks, hs = _mla_attn_specs(cfg)

    def body(q_ref, k_ref, v_ref, g_ref, o_ref, dy_ref, dq_ref, dk_ref, dv_ref, dg_ref):
        i = pl.program_id(1)
        q, k, v = q_ref[...], k_ref[...], v_ref[...]
        gate, dy, o = g_ref[...], dy_ref[...], o_ref[...]
        dg_ref[...] = (dy * o * _dsilu(gate)).astype(bf16)
        dob = (dy * _silu(gate)).astype(bf16)
        p = _mla_probs(cfg, q, k, i)
        nt = (((1,), (1,)), ((), ()))
        tn = (((0,), (0,)), ((), ()))
        dv = lax.dot_general(p.astype(bf16), dob, tn, preferred_element_type=f32)
        dp = lax.dot_general(dob, v, nt, preferred_element_type=f32)
        ds = (p * (dp - jnp.sum(dp * p, axis=-1, keepdims=True)) * ((HEAD + ROPE) ** -0.5)).astype(bf16)
        dq_ref[...] = jnp.dot(ds, k, preferred_element_type=f32)
        dk = lax.dot_general(ds, q, tn, preferred_element_type=f32)

        @pl.when(i == 0)
        def _():
            dk_ref[...] = dk
            dv_ref[...] = dv

        @pl.when(i > 0)
        def _():
            dk_ref[...] += dk
            dv_ref[...] += dv

    return pl.pallas_call(
        body, name="mla_attn_bwd", grid=(MH, S // TQ),
        in_specs=[qs(2 * HEAD), ks(2 * HEAD), ks(HEAD), hs(cfg.o_mg), hs(0), hs(0)],
        out_specs=(qs(2 * HEAD), ks(2 * HEAD), ks(HEAD), hs(0)),
        out_shape=(jax.ShapeDtypeStruct((MH, S, 2 * HEAD), f32), jax.ShapeDtypeStruct((MH, S, 2 * HEAD), f32),
                   jax.ShapeDtypeStruct((MH, S, HEAD), f32), jax.ShapeDtypeStruct((S, cfg.MW), bf16)),
        compiler_params=_cparams(("parallel", "arbitrary")),
    )(q3, k3, v3, proj, o, dy)


def _pick_rows(R, bytes_per_row):
    if R * bytes_per_row <= MM_BUDGET:
        return R
    best = None
    for t in range(16, R, 16):
        if R % t == 0 and t * bytes_per_row <= MM_BUDGET:
            best = t
    assert best is not None, (R, bytes_per_row)
    return best


def _adamw(w, g, m, v, name="adamw"):
    R, C = w.shape
    tr = _pick_rows(R, C * 4 * 7 * 2)
    c1 =1.0 - ADAM_B1 ** ADAM_STEP
    c2 = 1.0 - ADAM_B2 ** ADAM_STEP

    def body(w_ref, g_ref, m_ref, v_ref, d_ref, mo_ref, vo_ref):
        g = g_ref[...]
        m = ADAM_B1 * m_ref[...] + (1.0 - ADAM_B1) * g
        v = ADAM_B2 * v_ref[...] + (1.0 - ADAM_B2) * jnp.square(g)
        d_ref[...] = -ADAM_LR * ((m / c1) / (jnp.sqrt(v / c2) + ADAM_EPS) + ADAM_WD * w_ref[...])
        mo_ref[...] = m
        vo_ref[...] = v

    spec = pl.BlockSpec((tr, C), lambda i: (i, 0))
    return pl.pallas_call(
        body, name=name, grid=(R // tr,), in_specs=[spec] * 4, out_specs=(spec,) * 3,
        out_shape=(jax.ShapeDtypeStruct((R, C), f32),) * 3,
        compiler_params=_cparams(("parallel",)),
    )(w, g, m, v)


def _adamw_big(w, m, v, mines, others, core, name, half_cols=False):
    L, R, C = w.shape
    hr, hc = (R, C // 2) if half_cols else (R // 2, C)
    tr = _pick_rows(hr, hc * 4 * (7 + 2 * L) * 2)
    nt = hr // tr
    c1 = 1.0 - ADAM_B1 ** ADAM_STEP
    c2 = 1.0 - ADAM_B2 ** ADAM_STEP

    def body(core_ref, w_ref, m_ref, v_ref, *rest):
        g_refs, (go_ref, d_ref, mo_ref, vo_ref) = rest[:2 * L], rest[2 * L:]
        l, h = pl.program_id(0), pl.program_id(1)
        own = h == core_ref[0]
        g = jnp.where(own, g_refs[0][...], g_refs[L][...])
        for k in range(1, L):
            g = jnp.where(l == k, jnp.where(own, g_refs[k][...], g_refs[L + k][...]), g)
        m = ADAM_B1 * m_ref[...] + (1.0 - ADAM_B1) * g
        v = ADAM_B2 * v_ref[...] + (1.0 - ADAM_B2) * jnp.square(g)
        go_ref[...] = g
        d_ref[...] = -ADAM_LR * ((m / c1) / (jnp.sqrt(v / c2) + ADAM_EPS) + ADAM_WD * w_ref[...])
        mo_ref[...] = m
        vo_ref[...] = v

    if half_cols:
        lay = pl.BlockSpec((None, tr, hc), lambda l, h, i, core_ref: (l, i, h))
    else:
        lay = pl.BlockSpec((None, tr, hc), lambda l, h, i, core_ref: (l, h * nt + i, 0))
    gspec = lambda k: pl.BlockSpec((tr, hc), lambda l, h, i, core_ref: (jnp.where(l == k, i, 0), 0))
    return pl.pallas_call(
        body, name=name,
        grid_spec=pltpu.PrefetchScalarGridSpec(
            num_scalar_prefetch=1, grid=(L, 2, nt),
            in_specs=[lay, lay, lay] + [gspec(k) for k in range(L)] * 2, out_specs=(lay,) * 4),
        out_shape=(jax.ShapeDtypeStruct((L, R, C), f32),) * 4,
        compiler_params=_cparams(("arbitrary", "arbitrary", "arbitrary")),
    )(core, w, m, v, *mines, *others)


def _sum_blocks(x, out_dtype, name):
    n, R, C = x.shape
    tr = _pick_rows(R, C * 4 * (n + 1) * 2)

    def body(x_ref, o_ref):
        acc = x_ref[0].astype(f32)
        for k in range(1, n):
            acc = acc + x_ref[k].astype(f32)
        o_ref[...] = acc.astype(o_ref.dtype)

    return pl.pallas_call(
        body, name=name, grid=(R // tr,),
        in_specs=[pl.BlockSpec((n, tr, C), lambda i: (0, i, 0))], out_specs=pl.BlockSpec((tr, C), lambda i: (i, 0)),
        out_shape=jax.ShapeDtypeStruct((R, C), out_dtype),
        compiler_params=_cparams(("parallel",)),
    )(x)


def _hbm_specs(n):
    return [pl.BlockSpec(memory_space=pl.ANY)] * n


def _copy_rows(src, pieces, out_rows, zeros, name):
    n_cols = src.shape[1]

    def body(src_ref, z_ref, out_ref, sems):
        cps = [pltpu.make_async_copy(src_ref.at[pl.ds(s0, n), :], out_ref.at[pl.ds(d0, n), :], sems.at[k])
               for k, (s0, d0, n) in enumerate(pieces)]
        if zeros is not None:
            nz = zeros.shape[0]
            cps.append(pltpu.make_async_copy(z_ref, out_ref.at[pl.ds(out_rows - nz, nz), :], sems.at[len(pieces)]))
        for cp in cps:
            cp.start()
        for cp in cps:
            cp.wait()

    z = zeros if zeros is not None else jnp.zeros((16, n_cols), src.dtype)
    return pl.pallas_call(
        body, name=name, out_shape=jax.ShapeDtypeStruct((out_rows, n_cols), src.dtype),
        in_specs=_hbm_specs(2), out_specs=pl.BlockSpec(memory_space=pl.ANY),
        scratch_shapes=[pltpu.SemaphoreType.DMA((len(pieces) + 1,))],
    )(src, z)


def _allgather8(shards, name, col_half=()):
    na = len(shards)

    def body(*refs):
        x_refs, out_refs = refs[:na], refs[na:2 * na]
        send_sems, recv_sems, local_sems = refs[2 * na:]
        x, y, c = lax.axis_index("x"), lax.axis_index("y"), lax.axis_index("c")
        me, sibling = (x, y, c), (x, y, 1 - c)
        chips = [(1 - x, y), (x, 1 - y), (1 - x, 1 - y)]

        def rows(a, px, py, pc):
            m, n = shards[a].shape
            if a in col_half:
                return out_refs[a].at[pl.ds((2 * px + py) * m, m), pl.ds(pl.multiple_of(pc * n, n), n)]
            return out_refs[a].at[pl.ds((4 * px + 2 * py + pc) * m, m), :]

        def copy(a, k, block, to, src=None):
            return pltpu.make_async_remote_copy(
                src_ref=rows(a, *block) if src is None else src, dst_ref=rows(a, *block),
                send_sem=send_sems.at[a, k], recv_sem=recv_sems.at[a, k], device_id=to, device_id_type=MESH)

        mine = [pltpu.make_async_copy(x_refs[a], rows(a, *me), local_sems.at[a]) for a in range(na)]
        for cp in mine:
            cp.start()
        first = []
        for a in range(na):
            first.append(copy(a, 0, me, sibling, src=x_refs[a]))
            first += [copy(a, 1 + j, me, (*chip, c), src=x_refs[a]) for j, chip in enumerate(chips)]
        for cp in first:
            cp.start()
        passed = []
        for j, chip in enumerate(chips):
            for a in range(na):
                copy(a, 1 + j, (*chip, c), me).wait_recv()
                passed.append(copy(a, 4 + j, (*chip, c), sibling))
                passed[-1].start()
        for a in range(na):
            copy(a, 0, sibling, me).wait_recv()
        for j, chip in enumerate(chips):
            for a in range(na):
                copy(a, 4 + j, (*chip, 1 - c), me).wait_recv()
        for cp in first + passed:
            cp.wait_send()
        for cp in mine:
            cp.wait()

    return pl.pallas_call(
        body, name=name,
        out_shape=[jax.ShapeDtypeStruct((4 * s.shape[0], 2 * s.shape[1]) if a in col_half else (N_DEV * s.shape[0], s.shape[1]),
                                        s.dtype) for a, s in enumerate(shards)],
        in_specs=_hbm_specs(na), out_specs=_hbm_specs(na),
        scratch_shapes=[pltpu.SemaphoreType.DMA((na, 7)), pltpu.SemaphoreType.DMA((na, 7)), pltpu.SemaphoreType.DMA((na,))],
    )(*shards)


def _send_sibling(arrays, name):
    na = len(arrays)

    def body(*refs):
        x_refs, out_refs = refs[:na], refs[na:2 * na]
        send_sems, recv_sems = refs[2 * na:]
        sibling = (lax.axis_index("x"), lax.axis_index("y"), 1 - lax.axis_index("c"))
        cps = [pltpu.make_async_remote_copy(src_ref=x_refs[a], dst_ref=out_refs[a], send_sem=send_sems.at[a],
                                            recv_sem=recv_sems.at[a], device_id=sibling, device_id_type=MESH)
               for a in range(na)]
        for cp in cps:
            cp.start()
        for cp in cps:
            cp.wait()

    return pl.pallas_call(
        body, name=name, out_shape=[jax.ShapeDtypeStruct(x.shape, x.dtype) for x in arrays],
        in_specs=_hbm_specs(na), out_specs=_hbm_specs(na),
        scratch_shapes=[pltpu.SemaphoreType.DMA((na,)), pltpu.SemaphoreType.DMA((na,))],
    )(*arrays)


def _scatter_chips(arrays, name):
    na = len(arrays)

    def body(*refs):
        p_refs, out_refs = refs[:na], refs[na:2 * na]
        send_sems, recv_sems, local_sems = refs[2 * na:]
        x, y, c = lax.axis_index("x"), lax.axis_index("y"), lax.axis_index("c")
        mychip = 2 * x + y
        chips = [(1 - x, y), (x, 1 - y), (1 - x, 1 - y)]
        mine = [pltpu.make_async_copy(p_refs[a].at[mychip], out_refs[a].at[mychip], local_sems.at[a]) for a in range(na)]
        cps = [pltpu.make_async_remote_copy(src_ref=p_refs[a].at[2 * cx + cy], dst_ref=out_refs[a].at[mychip],
                                            send_sem=send_sems.at[a, j], recv_sem=recv_sems.at[a, j],
                                            device_id=(cx, cy, c), device_id_type=MESH)
               for j, (cx, cy) in enumerate(chips) for a in range(na)]
        for cp in mine + cps:
            cp.start()
        for j, (cx, cy) in enumerate(chips):
            for a in range(na):
                pltpu.make_async_remote_copy(src_ref=p_refs[a].at[mychip], dst_ref=out_refs[a].at[2 * cx + cy],
                                             send_sem=send_sems.at[a, j], recv_sem=recv_sems.at[a, j],
                                             device_id=(cx, cy, c), device_id_type=MESH).wait_recv()
        for cp in cps:
            cp.wait_send()
        for cp in mine:
            cp.wait()

    return pl.pallas_call(
        body, name=name, out_shape=[jax.ShapeDtypeStruct(p.shape, p.dtype) for p in arrays],
        in_specs=_hbm_specs(na), out_specs=_hbm_specs(na),
        scratch_shapes=[pltpu.SemaphoreType.DMA((na, 3)), pltpu.SemaphoreType.DMA((na, 3)), pltpu.SemaphoreType.DMA((na,))],
    )(*arrays)


def _add2(a, b, out_dtype, name):
    R, C = a.shape
    tr = _pick_rows(R, C * 4 * 3 * 2)

    def body(a_ref, b_ref, o_ref):
        o_ref[...] = (a_ref[...].astype(f32) + b_ref[...].astype(f32)).astype(o_ref.dtype)

    spec = pl.BlockSpec((tr, C), lambda i: (i, 0))
    return pl.pallas_call(body, name=name, grid=(R // tr,), in_specs=[spec, spec], out_specs=spec,
                          out_shape=jax.ShapeDtypeStruct((R, C), out_dtype), compiler_params=_cparams(("parallel",)))(a, b)


def _reduce_scatter(cfg, g_in_t, grads):
    c = lax.axis_index("c")
    hd = cfg.D // 2
    shp = [g.shape[2:] for g in grads]
    keep = [lax.dynamic_slice_in_dim(g_in_t, c * hd, hd, axis=1)]
    give = [lax.dynamic_slice_in_dim(g_in_t, (1 - c) * hd, hd, axis=1)]
    keep += [lax.dynamic_index_in_dim(g, c, axis=1, keepdims=False).reshape(4 * hr, nc) for g, (hr, nc) in zip(grads, shp)]
    give += [lax.dynamic_index_in_dim(g, 1 - c, axis=1, keepdims=False).reshape(4 * hr, nc) for g, (hr, nc) in zip(grads, shp)]
    got = _send_sibling(give, "rs_pair")
    part = [_add2(k, g, bf16, "rs_add_pair") for k, g in zip(keep, got)]
    k0, n_in = cfg.o_mg, cfg.IN_WIDTH
    rows = _copy_rows(part[0], [(0, 0, k0), (cfg.o_mkr, k0, ROPE), (k0, k0 + ROPE, n_in - k0 - ROPE)], n_in, None, "rs_rows")
    part = [rows.reshape(4, n_in // 4, hd)] + [p.reshape(4, hr, nc) for p, (hr, nc) in zip(part[1:], shp)]
    slots = _scatter_chips(part, "rs_chips")
    mine = [_sum_blocks(s, f32, "rs_add_chips") for s in slots]
    return mine, _send_sibling(mine, "rs_halves")


def _big_weights(cfg):
    return (("mla_w_uq", cfg.QL, cfg.QW, 1), ("mla_w_ukv", cfg.KL, cfg.KVW, 1),
            ("w_branch", cfg.RW + cfg.LW + cfg.MW, cfg.D, 0), ("w_out", cfg.D, cfg.D, 0))


def _half_shapes(cfg):
    out = []
    for _, r, c, ax in _big_weights(cfg):
        out.append((r // 2, c // 4) if ax == 1 else (r // 8, c))
    return out


def _my_halves(cfg, W, l, c):
    hd = cfg.D // 2
    out = [lax.dynamic_slice_in_dim(W["w_in"][l].T, c * hd, hd, axis=1).astype(bf16)]
    for (name, *_), (hr, nc) in zip(_big_weights(cfg), _half_shapes(cfg)):
        out.append(lax.dynamic_slice_in_dim(W[name][l], c * hr, hr, axis=0).astype(bf16))
    return out


def _uq_split(cfg, w):
    hw = HEAD + ROPE
    return jnp.concatenate([w[:, h * hw:h * hw + HEAD] for h in range(cfg.MH)]
                           + [w[:, h * hw + HEAD:(h + 1) * hw] for h in range(cfg.MH)], axis=1)


def _uq_join(cfg, g):
    n = cfg.MH * HEAD
    parts = []
    for h in range(cfg.MH):
        parts += [g[:, h * HEAD:(h + 1) * HEAD], g[:, n + h * ROPE:n + (h + 1) * ROPE]]
    return jnp.concatenate(parts, axis=1)


def _col_blocks(g):
    nc = g.shape[1] // 4
    return jnp.stack([g[:, q * nc:(q + 1) * nc] for q in range(4)])


def _row_pack(parts):
    rows = []
    for p in parts:
        r = p.reshape(-1, LANES)
        pad = -r.shape[0] % SUBLANES
        rows.append(jnp.concatenate([r, jnp.zeros((pad, LANES), r.dtype)], axis=0) if pad else r)
    return jnp.concatenate(rows, axis=0)


def _row_unpack(packed, like):
    out, off = [], 0
    for p in like:
        n = p.size // LANES
        out.append(packed[off:off + n].reshape(p.shape))
        off += -(-n // SUBLANES) * SUBLANES
    return out


def _prep_layer(cfg, full, small, zeros):
    w_in_t, w_uq, w_ukv, w_branch, w_out = full
    RW, LW = cfg.RW, cfg.LW
    k0 = cfg.o_mg
    P = dict(small)
    P["w_in_t"] = _copy_rows(w_in_t, [(0, 0, k0), (k0, cfg.o_mkr, ROPE), (k0 + ROPE, k0, cfg.IN_WIDTH - k0 - ROPE)],
                             cfg.NP, zeros, "w_in_rows")
    P["w_uq"] = _uq_split(cfg, jnp.concatenate(list(w_uq.reshape(4, cfg.QL, -1)), axis=1))
    P["w_ukv"] = jnp.concatenate(list(w_ukv.reshape(4, cfg.KL, -1)), axis=1)
    P["wb"] = (w_branch[:RW], w_branch[RW:RW + LW], w_branch[RW + LW:])
    P["w_out"] = w_out
    return P


def _layer_fwd(cfg, x, mod, P, T):
    h = _prenorm_fwd(cfg, x, mod, P["norm_pre"])
    proj = _mm(h, P["w_in_t"], f32, "mm_proj", mode="nt")
    y_ret = _ret_fwd(cfg, proj, P["ret_gn"], T["cos_r"], T["sin_r"], T["ret_consts"])
    a, b = _lru_gates(cfg, proj, P["lru_conv_w"], P["lru_conv_b"], P["lru_wa"], P["lru_ba"], P["lru_wx"], P["lru_bx"],
                      P["lru_lambda"])
    hl, y_lru = _lru_scan_fwd(cfg, proj, a, b)
    qn, kn = _mla_norm(cfg, proj, P["mla_q_norm"], P["mla_kv_norm"])
    q = _mm(qn, P["w_uq"], f32, "mm_uq")
    kv = _mm(kn, P["w_ukv"], f32, "mm_ukv")
    q3, k3, v3 = _mla_pack(cfg, proj, q, kv, T["cos_q"], T["sin_q"], T["cos_k"], T["sin_k"])
    o, y_mla = _mla_attn_fwd(cfg, proj, q3, k3, v3)
    ys = (y_ret, y_lru, y_mla)
    us = tuple(_mm(yb, wb, f32, "mm_branch") for yb, wb in zip(ys, P["wb"]))
    merged = _merge_fwd(cfg, proj, *us)
    y = _mm(merged, P["w_out"], f32, "mm_out")
    out = _postnorm_fwd(cfg, x, y, mod, P["norm_post"])
    R = dict(x=x, h=h, proj=proj, ys=ys, a=a, hl=hl, qn=qn, kn=kn, q3=q3, k3=k3, v3=v3, o=o, us=us, merged=merged, y=y)
    return out, R


def _layer_bwd(cfg, dout, R, mod, P, T):
    proj = R["proj"]
    dy, s_post = _postnorm_bwd(cfg, dout, R["y"], mod, P["norm_post"])
    dmerged = _mm(dy, P["w_out"], f32, "mm_dmerged", mode="nt")
    g_out = _mm(R["merged"], dy, bf16, "mm_gw_out", mode="tn")
    du0, du1, du2, dlog = _merge_bwd(cfg, proj, dmerged, *R["us"])
    dus = (du0, du1, du2)
    dys = tuple(_mm(du, wb, f32, "mm_dbranch", mode="nt") for du, wb in zip(dus, P["wb"]))
    g_branch = jnp.concatenate([_mm(yb, du, bf16, "mm_gw_branch", mode="tn") for yb, du in zip(R["ys"], dus)], axis=0)
    drq, drk, drv, drg, dgn = _ret_bwd(cfg, proj, dys[0], P["ret_gn"], T["cos_r"], T["sin_r"], T["ret_consts"])
    da, db, dlg = _lru_scan_bwd(cfg, proj, R["a"], R["hl"], dys[1])
    dxc, dwa, dwx, s_lru = _lru_gates_bwd(cfg, proj, da, db, P["lru_conv_w"], P["lru_conv_b"], P["lru_wa"], P["lru_ba"],
                                          P["lru_wx"], P["lru_bx"], P["lru_lambda"])
    dlx, s_conv = _lru_conv_bwd(cfg, proj, dxc, P["lru_conv_w"])
    dq3, dk3, dv3, dmg = _mla_attn_bwd(cfg, proj, R["q3"], R["k3"], R["v3"], R["o"], dys[2])
    dq, dkv, dmkr = _mla_unpack_bwd(cfg, dq3, dk3, dv3, T["cos_q"], T["sin_q"], T["cos_k"], T["sin_k"])
    dqn = _mm(dq, P["w_uq"], f32, "mm_dqn", mode="nt")
    dkn = _mm(dkv, P["w_ukv"], f32, "mm_dkn", mode="nt")
    g_uq = _uq_join(cfg, _mm(R["qn"], dq, bf16, "mm_gw_uq", mode="tn"))
    g_ukv = _mm(R["kn"], dkv, bf16, "mm_gw_ukv", mode="tn")
    dmq, dmkv, s_q, s_k = _mla_norm_bwd(cfg, proj, dqn, dkn, P["mla_q_norm"], P["mla_kv_norm"])
    dproj = jnp.concatenate([drq, drk, drv, drg, dlx, dlg, dmq, dmkv, dmg, dlog, dmkr,
                             jnp.zeros((cfg.S, cfg.NP - cfg.o_mkr - HEAD), bf16)], axis=1)
    dh = _mm(dproj, P["w_in_t"], f32, "mm_dh")
    g_in_t = _mm(dproj, R["h"], bf16, "mm_gw_in", mode="tn", tm=512)
    dx, s_pre = _prenorm_bwd(cfg, R["x"], dh, dout, mod, P["norm_pre"])
    big = [_col_blocks(g_uq), _col_blocks(g_ukv), g_branch, g_out]
    big = (g_in_t, [g.reshape(4, 2, hr, nc) for g, (hr, nc) in zip(big, _half_shapes(cfg))])
    small = dict(norm_pre=s_pre[2:3], norm_post=s_post[1:2], ret_gn=dgn, lru_conv_w=s_conv[0:CONV], lru_conv_b=s_conv[CONV:CONV + 1],
                 lru_wa=dwa, lru_ba=s_lru[0:1], lru_wx=dwx, lru_bx=s_lru[1:2], lru_lambda=s_lru[2:3],
                 mla_q_norm=s_q[0:1], mla_kv_norm=s_k[0:1])
    dmod = jnp.concatenate([s_pre[0:1], s_pre[1:2], s_post[0:1]], axis=1)
    return dx, big, small, dmod


_SMALL = ("norm_pre", "norm_post", "ret_gn", "lru_conv_w", "lru_conv_b", "lru_wa", "lru_ba", "lru_wx", "lru_bx", "lru_lambda",
          "mla_q_norm", "mla_kv_norm")
_WEIGHTS = ("ada_w", "ada_b", "norm_pre", "norm_post", "w_in", "ret_gn", "lru_conv_w", "lru_conv_b", "lru_wa", "lru_ba", "lru_wx",
            "lru_bx", "lru_lambda", "mla_q_norm", "mla_w_uq", "mla_kv_norm", "mla_w_ukv", "w_branch", "w_out")


def _step(cfg, x, c, positions, W, target, M1, V1):
    L, D = cfg.L, cfg.D
    xi, yi, ci = lax.axis_index("x"), lax.axis_index("y"), lax.axis_index("c")
    chip = 2 * xi + yi
    me = 2 * chip + ci

    c8 = jnp.concatenate([c, jnp.zeros((SUBLANES - 1, D), f32)], axis=0)
    c_all = _allgather8([c8], "gather_c")[0].reshape(N_DEV, SUBLANES, D)[:, 0]
    mod_sh, c_act = _ada_fwd(cfg, c_all, W["ada_w"])
    n_sh = mod_sh.shape[2]
    mod_half = lax.dynamic_slice_in_dim(mod_sh, ci * (n_sh // 2), n_sh // 2, axis=2).reshape(L * N_DEV, n_sh // 2)
    mod_all = _allgather8([mod_half], "gather_mod")[0].reshape(N_DEV, L, N_DEV, n_sh // 2)
    mod_all = mod_all.transpose(1, 2, 0, 3).reshape(L, N_DEV, 3 * D)
    mods = lax.dynamic_index_in_dim(mod_all, me, axis=1, keepdims=False) + W["ada_b"]

    (cos_r, sin_r), (cos_m, sin_m) = _rope_tables(cfg, positions)
    T = dict(cos_r=cos_r, sin_r=sin_r, cos_q=jnp.tile(cos_m, (1, cfg.MH)), sin_q=jnp.tile(sin_m, (1, cfg.MH)),
             cos_k=jnp.tile(cos_m, (1, 2)), sin_k=jnp.tile(sin_m, (1, 2)), ret_consts=_ret_consts(cfg))

    Ps, Rs = [], []
    act = x[0]
    zeros = jnp.zeros((cfg.NP - cfg.o_mkr - ROPE, D), bf16)
    for l in range(L):
        gathered = _allgather8(_my_halves(cfg, W, l, ci), "gather_w", col_half=(0,))
        small = {k: (W[k][l] if W[k][l].ndim > 1 else W[k][l][None, :]) for k in _SMALL if k != "lru_conv_w"}
        P = _prep_layer(cfg, gathered, small, zeros)
        P["lru_conv_w"] = None
        Ps.append(P)
    cw_all = _allgather8([_pad_rows(W["lru_conv_w"].reshape(L * CONV, -1))], "gather_conv")[0]
    cw_rows = cw_all.shape[0] // N_DEV
    cw_all = cw_all.reshape(4, 2, cw_rows, -1)[:, 0, :L * CONV].transpose(1, 0, 2).reshape(L, CONV, cfg.LW)
    for l in range(L):
        Ps[l]["lru_conv_w"] = cw_all[l]
    for l in range(L):
        act, R = _layer_fwd(cfg, act, mods[l:l + 1], Ps[l], T)
        Rs.append(R)

    dact, lsum = _loss_head(cfg, act, target[0])
    loss = lax.psum(lsum[0, 0], ("x", "y", "c"))

    big_g = [None] * L
    small_g = [None] * L
    dmods = [None] * L
    for l in range(L - 1, -1, -1):
        dact, grads, small_g[l], dmods[l] = _layer_bwd(cfg, dact, Rs[l], mods[l:l + 1], Ps[l], T)
        big_g[l] = _reduce_scatter(cfg, *grads)

    dmod = jnp.concatenate(dmods, axis=0)
    parts = [dmod] + [small_g[l][k] for l in range(L) for k in _SMALL]
    packed = _row_pack(parts)
    allf = _allgather8([packed], "gather_small")[0].reshape(N_DEV, packed.shape[0], LANES)
    summed = _row_unpack(_sum_blocks(allf, f32, "sum_small"), parts)
    gsm = {k: jnp.stack([summed[1 + l * len(_SMALL) + i].reshape(W[k].shape[1:] if k != "lru_conv_w" else (CONV, cfg.LW))
                         for l in range(L)]) for i, k in enumerate(_SMALL)}
    ncw = cfg.LW // 4
    gsm["lru_conv_w"] = lax.dynamic_slice_in_dim(gsm["lru_conv_w"], chip * ncw, ncw, axis=2)
    gsm["ada_b"] = summed[0]
    dmod_all = allf[:, :dmod.size // LANES].reshape(N_DEV, L, 3 * D)
    dmod_sh = lax.dynamic_slice_in_dim(dmod_all, chip * n_sh, n_sh, axis=2).transpose(1, 0, 2)
    G = dict(gsm)
    G["ada_w"] = _ada_bwd(cfg, c_act.T, dmod_sh)
    delta, new_m, new_v = {}, {}, {}
    core = ci.astype(jnp.int32).reshape(1)
    for i, (name, *_) in enumerate(_big_weights(cfg)):
        G[name], delta[name], new_m[name], new_v[name] = _adamw_big(
            W[name], M1[name], V1[name], [big_g[l][0][i + 1] for l in range(L)], [big_g[l][1][i + 1] for l in range(L)], core,
            "adamw_" + name)
    tr_ = lambda a: a.transpose(0, 2, 1)
    outs = _adamw_big(tr_(W["w_in"]), tr_(M1["w_in"]), tr_(V1["w_in"]), [big_g[l][0][0] for l in range(L)],
                      [big_g[l][1][0] for l in range(L)], core, "adamw_w_in", half_cols=True)
    G["w_in"], delta["w_in"], new_m["w_in"], new_v["w_in"] = [tr_(o) for o in outs]
    bigs = ("ada_w", "w_in") + tuple(name for name, *_ in _big_weights(cfg))
    shp = W["ada_w"].shape
    two = lambda a: a.reshape(-1, shp[-1])
    d, m_, v_ = _adamw(two(W["ada_w"]), two(G["ada_w"]), two(M1["ada_w"]), two(V1["ada_w"]), "adamw_ada_w")
    delta["ada_w"], new_m["ada_w"], new_v["ada_w"] = d.reshape(shp), m_.reshape(shp), v_.reshape(shp)
    smalls = [k for k in _WEIGHTS if k not in bigs]
    packs = [_row_pack([src[k] for k in smalls]) for src in (W, G, M1, V1)]
    outs = _adamw(*packs, "adamw_small")
    for dst, o in zip((delta, new_m, new_v), outs):
        for k, val in zip(smalls, _row_unpack(o, [W[k] for k in smalls])):
            dst[k] = val

    grad_x = dact[None]
    return (loss, grad_x, *[G[k] for k in _WEIGHTS], *[delta[k] for k in _WEIGHTS], *[new_m[k] for k in _WEIGHTS],
            *[new_v[k] for k in _WEIGHTS])


def _pad_rows(a):
    pad = -a.shape[0] % SUBLANES
    return jnp.concatenate([a, jnp.zeros((pad, a.shape[1]), a.dtype)], axis=0) if pad else a


def kernel(x, c, positions, ada_w, ada_b, norm_pre, norm_post, w_in, ret_gn, lru_conv_w, lru_conv_b, lru_wa, lru_ba, lru_wx, lru_bx, lru_lambda, mla_q_norm, mla_w_uq, mla_kv_norm, mla_w_ukv, w_branch, w_out, loss_target, m_ada_w, m_ada_b, m_norm_pre, m_norm_post, m_w_in, m_ret_gn, m_lru_conv_w, m_lru_conv_b, m_lru_wa, m_lru_ba, m_lru_wx, m_lru_bx, m_lru_lambda, m_mla_q_norm, m_mla_w_uq, m_mla_kv_norm, m_mla_w_ukv, m_w_branch, m_w_out, v_ada_w, v_ada_b, v_norm_pre, v_norm_post, v_w_in, v_ret_gn, v_lru_conv_w, v_lru_conv_b, v_lru_wa, v_lru_ba, v_lru_wx, v_lru_bx, v_lru_lambda, v_mla_q_norm, v_mla_w_uq, v_mla_kv_norm, v_mla_w_ukv, v_w_branch, v_w_out):
    W = dict(ada_w=ada_w, ada_b=ada_b, norm_pre=norm_pre, norm_post=norm_post, w_in=w_in, ret_gn=ret_gn, lru_conv_w=lru_conv_w,
             lru_conv_b=lru_conv_b, lru_wa=lru_wa, lru_ba=lru_ba, lru_wx=lru_wx, lru_bx=lru_bx, lru_lambda=lru_lambda,
             mla_q_norm=mla_q_norm, mla_w_uq=mla_w_uq, mla_kv_norm=mla_kv_norm, mla_w_ukv=mla_w_ukv, w_branch=w_branch, w_out=w_out)
    M1 = dict(ada_w=m_ada_w, ada_b=m_ada_b, norm_pre=m_norm_pre, norm_post=m_norm_post, w_in=m_w_in, ret_gn=m_ret_gn,
              lru_conv_w=m_lru_conv_w, lru_conv_b=m_lru_conv_b, lru_wa=m_lru_wa, lru_ba=m_lru_ba, lru_wx=m_lru_wx, lru_bx=m_lru_bx,
              lru_lambda=m_lru_lambda, mla_q_norm=m_mla_q_norm, mla_w_uq=m_mla_w_uq, mla_kv_norm=m_mla_kv_norm,
              mla_w_ukv=m_mla_w_ukv, w_branch=m_w_branch, w_out=m_w_out)
    V1 = dict(ada_w=v_ada_w, ada_b=v_ada_b, norm_pre=v_norm_pre, norm_post=v_norm_post, w_in=v_w_in, ret_gn=v_ret_gn,
              lru_conv_w=v_lru_conv_w, lru_conv_b=v_lru_conv_b, lru_wa=v_lru_wa, lru_ba=v_lru_ba, lru_wx=v_lru_wx, lru_bx=v_lru_bx,
              lru_lambda=v_lru_lambda, mla_q_norm=v_mla_q_norm, mla_w_uq=v_mla_w_uq, mla_kv_norm=v_mla_kv_norm,
              mla_w_ukv=v_mla_w_ukv, w_branch=v_w_branch, w_out=v_w_out)
    return _step(_CFG, x, c, positions, W, loss_target, M1, V1)
```

```python
import functools
import math
from typing import NamedTuple

import numpy as np
import jax
import jax.numpy as jnp
from jax import lax
from jax.experimental import pallas as pl
from jax.experimental.pallas import tpu as pltpu

f32 = jnp.float32
bf16 = jnp.bfloat16

NORM_EPS = 1e-6
ROPE_BASE = 10000.0
CHUNK = 64
HEAD = 128
ROPE = 64
CONV = 4
LRU_C = 8.0
ADAM_LR, ADAM_B1, ADAM_B2, ADAM_EPS, ADAM_WD, ADAM_STEP = 0.001, 0.9, 0.999, 1e-08, 0.01, 10

LANES = 128
SUBLANES = 8
VMEM_LIMIT = 56 * 1024 * 1024
MM_BUDGET = 40 * 1024 * 1024
N_DEV = 8
MESH = pl.DeviceIdType.MESH
COPY_ROWS = 128


class Cfg(NamedTuple):
    D: int = 2048
    S: int = 2048
    L: int = 4
    H: int = 8
    NB: int = 8
    MH: int = 8
    QL: int = 512
    KL: int = 512
    TR: int = 256
    TQ: int = 256

    @property
    def RW(self): return self.H * HEAD
    @property
    def LW(self): return self.NB * HEAD
    @property
    def MW(self): return self.MH * HEAD
    @property
    def o_rk(self): return self.RW
    @property
    def o_rv(self): return 2 * self.RW
    @property
    def o_rg(self): return 3 * self.RW
    @property
    def o_lx(self): return 4 * self.RW
    @property
    def o_lg(self): return 4 * self.RW + self.LW
    @property
    def o_mq(self): return 4 * self.RW + 2 * self.LW
    @property
    def o_mkv(self): return self.o_mq + self.QL
    @property
    def o_mg(self): return self.o_mkv + self.KL
    @property
    def o_merge(self): return self.o_mg + self.MW
    @property
    def o_mkr(self): return self.o_merge + 3 * self.D
    @property
    def NP(self): return -(-(self.o_mkr + ROPE) // 512) * 512
    @property
    def IN_WIDTH(self): return self.o_mkr + ROPE
    @property
    def QW(self): return self.MH * (HEAD + ROPE)
    @property
    def KVW(self): return self.MH * 2 * HEAD


_CFG = Cfg()


def _cparams(sem=None):
    return pltpu.CompilerParams(dimension_semantics=sem, vmem_limit_bytes=VMEM_LIMIT)


def _sigmoid(x):
    return jax.nn.sigmoid(x)


def _silu(x):
    return x * _sigmoid(x)


def _dsilu(x):
    s = _sigmoid(x)
    return s * (1.0 + x * (1.0 - s))


def _slab(rows, width, off):
    assert off % width == 0
    return pl.BlockSpec((rows, width), lambda i, _c=off // width: (i, _c))


def _row(width):
    return pl.BlockSpec((1, width), lambda i: (0, 0))


def _mm(a, b, out_dtype=f32, name="mm", mode="nn", tm=None):
    (M, K) = a.shape if mode != "tn" else a.shape[::-1]
    (K2, N) = b.shape if mode != "nt" else b.shape[::-1]
    assert K == K2
    tn = N if N <= 2048 else 512
    tk = K if K <= 2048 else 512
    assert N % tn == 0 and K % tk == 0
    osz = jnp.dtype(out_dtype).itemsize
    if tm is None:
        tm = M
        while 2 * tm * tk * 2 + 2 * tk * tn * 2 + 2 * tm * tn * osz + tm * tn * 4 > MM_BUDGET and tm % 16 == 0:
            tm //= 2
    assert M % tm == 0
    nk = K // tk
    dims = {"nn": (((1,), (0,)), ((), ())), "nt": (((1,), (1,)), ((), ())), "tn": (((0,), (0,)), ((), ()))}[mode]

    def dot(a_ref, b_ref):
        return lax.dot_general(a_ref[...].astype(bf16), b_ref[...].astype(bf16), dims, preferred_element_type=f32)

    if nk == 1:
        def body(a_ref, b_ref, o_ref):
            o_ref[...] = dot(a_ref, b_ref).astype(o_ref.dtype)
        scratch = []
    else:
        def body(a_ref, b_ref, o_ref, acc_ref):
            k = pl.program_id(2)

            @pl.when(k == 0)
            def _():
                acc_ref[...] = jnp.zeros_like(acc_ref)

            acc_ref[...] += dot(a_ref, b_ref)

            @pl.when(k == nk - 1)
            def _():
                o_ref[...] = acc_ref[...].astype(o_ref.dtype)
        scratch = [pltpu.VMEM((tm, tn), f32)]

    a_spec = pl.BlockSpec((tk, tm), lambda i, j, k: (k, i)) if mode == "tn" else pl.BlockSpec((tm, tk), lambda i, j, k: (i, k))
    b_spec = pl.BlockSpec((tn, tk), lambda i, j, k: (j, k)) if mode == "nt" else pl.BlockSpec((tk, tn), lambda i, j, k: (k, j))
    return pl.pallas_call(
        body, name=name,
        grid=(M // tm, N // tn, nk),
        in_specs=[a_spec, b_spec],
        out_specs=pl.BlockSpec((tm, tn), lambda i, j, k: (i, j)),
        out_shape=jax.ShapeDtypeStruct((M, N), out_dtype),
        scratch_shapes=scratch,
        compiler_params=_cparams(("parallel", "parallel", "arbitrary")),
    )(a, b)


def _ada_fwd(cfg, c_all, ada_w):
    L, D, n = ada_w.shape
    tn = n // 2 if (n // 2) % LANES == 0 else n

    def body(c_ref, w_ref, o_ref, ca_ref):
        ca = _silu(c_ref[...])
        ca_ref[...] = ca
        o_ref[0] = jnp.dot(ca.astype(bf16), w_ref[0].astype(bf16), preferred_element_type=f32)

    return pl.pallas_call(
        body, name="ada_fwd", grid=(L, n // tn),
        in_specs=[pl.BlockSpec((N_DEV, D), lambda l, j: (0, 0)), pl.BlockSpec((1, D, tn), lambda l, j: (l, 0, j))],
        out_specs=(pl.BlockSpec((1, N_DEV, tn), lambda l, j: (l, 0, j)), pl.BlockSpec((N_DEV, D), lambda l, j: (0, 0))),
        out_shape=(jax.ShapeDtypeStruct((L, N_DEV, n), f32), jax.ShapeDtypeStruct((N_DEV, D), f32)),
        compiler_params=_cparams(("arbitrary", "arbitrary")),
    )(c_all, ada_w)


def _ada_bwd(cfg, c_act_t, dmod):
    L, _, n = dmod.shape
    D = c_act_t.shape[0]
    tn = n // 2 if (n // 2) % LANES == 0 else n

    def body(c_ref, d_ref, o_ref):
        o_ref[0] = jnp.dot(c_ref[...].astype(bf16), d_ref[0].astype(bf16), preferred_element_type=f32)

    return pl.pallas_call(
        body, name="ada_bwd", grid=(L, n // tn),
        in_specs=[pl.BlockSpec((D, N_DEV), lambda l, j: (0, 0)), pl.BlockSpec((1, N_DEV, tn), lambda l, j: (l, 0, j))],
        out_specs=pl.BlockSpec((1, D, tn), lambda l, j: (l, 0, j)),
        out_shape=jax.ShapeDtypeStruct((L, D, n), f32),
        compiler_params=_cparams(("parallel", "parallel")),
    )(c_act_t, dmod)


def _prenorm_fwd(cfg, x, mod, gain):
    S, D, TR = cfg.S, cfg.D, cfg.TR

    def body(x_ref, mod_ref, g_ref, h_ref):
        x = x_ref[...]
        r = lax.rsqrt(jnp.mean(x * x, axis=-1, keepdims=True) + NORM_EPS)
        shift, scale = mod_ref[:, 0:D], mod_ref[:, D:2 * D]
        h_ref[...] = ((x * r) * g_ref[...] * (1.0 + scale) + shift).astype(bf16)

    return pl.pallas_call(
        body, name="prenorm_fwd", grid=(S // TR,),
        in_specs=[_slab(TR, D, 0), _row(3 * D), _row(D)],
        out_specs=_slab(TR, D, 0), out_shape=jax.ShapeDtypeStruct((S, D), bf16),
        compiler_params=_cparams(("parallel",)),
    )(x, mod, gain)


def _prenorm_bwd(cfg, x, dh, dres, mod, gain):
    S, D, TR = cfg.S, cfg.D, cfg.TR

    def body(x_ref, dh_ref, dres_ref, mod_ref, g_ref, dx_ref, sum_ref):
        i = pl.program_id(0)
        x, dh, g = x_ref[...], dh_ref[...], g_ref[...]
        scale = mod_ref[:, D:2 * D]
        r = lax.rsqrt(jnp.mean(x * x, axis=-1, keepdims=True) + NORM_EPS)
        xn = x * r
        t = dh * xn
        dxn = dh * (g * (1.0 + scale))
        dx_ref[...] = r * (dxn - xn * jnp.mean(dxn * xn, axis=-1, keepdims=True)) + dres_ref[...]
        part = jnp.concatenate([jnp.sum(dh, axis=0, keepdims=True), jnp.sum(t * g, axis=0, keepdims=True),
                                jnp.sum(t * (1.0 + scale), axis=0, keepdims=True), jnp.zeros((SUBLANES - 3, D), f32)], axis=0)

        @pl.when(i == 0)
        def _():
            sum_ref[...] = part

        @pl.when(i > 0)
        def _():
            sum_ref[...] += part

    return pl.pallas_call(
        body, name="prenorm_bwd", grid=(S // TR,),
        in_specs=[_slab(TR, D, 0), _slab(TR, D, 0), _slab(TR, D, 0), _row(3 * D), _row(D)],
        out_specs=(_slab(TR, D, 0), pl.BlockSpec((SUBLANES, D), lambda i: (0, 0))),
        out_shape=(jax.ShapeDtypeStruct((S, D), f32), jax.ShapeDtypeStruct((SUBLANES, D), f32)),
        compiler_params=_cparams(("arbitrary",)),
    )(x, dh, dres, mod, gain)


def _postnorm_fwd(cfg, x, y, mod, gain):
    S, D, TR = cfg.S, cfg.D, cfg.TR

    def body(x_ref, y_ref, mod_ref, g_ref, o_ref):
        y = y_ref[...]
        r = lax.rsqrt(jnp.mean(y * y, axis=-1, keepdims=True) + NORM_EPS)
        rg = mod_ref[:, 2 * D:3 * D]
        o_ref[...] = x_ref[...] + (1.0 + rg) * ((y * r) * g_ref[...])

    return pl.pallas_call(
        body, name="postnorm_fwd", grid=(S // TR,),
        in_specs=[_slab(TR, D, 0), _slab(TR, D, 0), _row(3 * D), _row(D)],
        out_specs=_slab(TR, D, 0), out_shape=jax.ShapeDtypeStruct((S, D), f32),
        compiler_params=_cparams(("parallel",)),
    )(x, y, mod, gain)


def _postnorm_bwd(cfg, dout, y, mod, gain):
    S, D, TR = cfg.S, cfg.D, cfg.TR

    def body(do_ref, y_ref, mod_ref, g_ref, dy_ref, sum_ref):
        i = pl.program_id(0)
        do, y, g = do_ref[...], y_ref[...], g_ref[...]
        rg = mod_ref[:, 2 * D:3 * D]
        r = lax.rsqrt(jnp.mean(y * y, axis=-1, keepdims=True) + NORM_EPS)
        yn = y * r
        t = do * yn
        dyn = do * ((1.0 + rg) * g)
        dy_ref[...] = (r * (dyn - yn * jnp.mean(dyn * yn, axis=-1, keepdims=True))).astype(bf16)
        part = jnp.concatenate([jnp.sum(t * g, axis=0, keepdims=True), jnp.sum(t * (1.0 + rg), axis=0, keepdims=True),
                                jnp.zeros((SUBLANES - 2, D), f32)], axis=0)

        @pl.when(i == 0)
        def _():
            sum_ref[...] = part

        @pl.when(i > 0)
        def _():
            sum_ref[...] += part

    return pl.pallas_call(
        body, name="postnorm_bwd", grid=(S // TR,),
        in_specs=[_slab(TR, D, 0), _slab(TR, D, 0), _row(3 * D), _row(D)],
        out_specs=(_slab(TR, D, 0), pl.BlockSpec((SUBLANES, D), lambda i: (0, 0))),
        out_shape=(jax.ShapeDtypeStruct((S, D), bf16), jax.ShapeDtypeStruct((SUBLANES, D), f32)),
        compiler_params=_cparams(("arbitrary",)),
    )(dout, y, mod, gain)


def _loss_head(cfg, y, target):
    S, D, TR = cfg.S, cfg.D, cfg.TR

    def body(y_ref, t_ref, d_ref, l_ref):
        i = pl.program_id(0)
        err = y_ref[...] - t_ref[...]
        d_ref[...] = err / D
        part = jnp.zeros((SUBLANES, LANES), f32) + 0.5 * jnp.sum(jnp.mean(err * err, axis=-1, keepdims=True))

        @pl.when(i == 0)
        def _():
            l_ref[...] = part

        @pl.when(i > 0)
        def _():
            l_ref[...] += part

    return pl.pallas_call(
        body, name="loss_head", grid=(S // TR,),
        in_specs=[_slab(TR, D, 0), _slab(TR, D, 0)],
        out_specs=(_slab(TR, D, 0), pl.BlockSpec((SUBLANES, LANES), lambda i: (0, 0))),
        out_shape=(jax.ShapeDtypeStruct((S, D), f32), jax.ShapeDtypeStruct((SUBLANES, LANES), f32)),
        compiler_params=_cparams(("arbitrary",)),
    )(y, target)


def _merge_fwd(cfg, proj, u0, u1, u2):
    S, D, TR = cfg.S, cfg.D, cfg.TR

    def body(l0, l1, l2, u0_ref, u1_ref, u2_ref, o_ref):
        o_ref[...] = (_sigmoid(l0[...]) * u0_ref[...] + _sigmoid(l1[...]) * u1_ref[...]
                      + _sigmoid(l2[...]) * u2_ref[...]).astype(bf16)

    return pl.pallas_call(
        body, name="merge_fwd", grid=(S // TR,),
        in_specs=[_slab(TR, D, cfg.o_merge + b * D) for b in range(3)] + [_slab(TR, D, 0)] * 3,
        out_specs=_slab(TR, D, 0), out_shape=jax.ShapeDtypeStruct((S, D), bf16),
        compiler_params=_cparams(("parallel",)),
    )(proj, proj, proj, u0, u1, u2)


def _merge_bwd(cfg, proj, dmerged, u0, u1, u2):
    S, D, TR = cfg.S, cfg.D, cfg.TR

    def body(l0, l1, l2, dm_ref, u0_ref, u1_ref, u2_ref, du0, du1, du2, dl_ref):
        dm = dm_ref[...]
        for b, (l, u, du) in enumerate(((l0, u0_ref, du0), (l1, u1_ref, du1), (l2, u2_ref, du2))):
            g = _sigmoid(l[...])
            du[...] = (dm * g).astype(bf16)
            dl_ref[:, b * D:(b + 1) * D] = (dm * u[...] * (g * (1.0 - g))).astype(bf16)

    return pl.pallas_call(
        body, name="merge_bwd", grid=(S // TR,),
        in_specs=[_slab(TR, D, cfg.o_merge + b * D) for b in range(3)] + [_slab(TR, D, 0)] * 4,
        out_specs=(_slab(TR, D, 0),) * 3 + (_slab(TR, 3 * D, 0),),
        out_shape=(jax.ShapeDtypeStruct((S, D), bf16),) * 3 + (jax.ShapeDtypeStruct((S, 3 * D), bf16),),
        compiler_params=_cparams(("parallel",)),
    )(proj, proj, proj, dmerged, u0, u1, u2)


def _rope128(x, c, s):
    return x * c + pltpu.roll(x, 64, axis=1) * s


def _rope128_t(dy, c, s):
    return dy * c + pltpu.roll(dy * s, 64, axis=1)


def _swap32(x):
    w = x.shape[1]
    lane = lax.broadcasted_iota(jnp.int32, x.shape, 1)
    return jnp.where((lane % 64) < 32, pltpu.roll(x, w - 32, axis=1), pltpu.roll(x, 32, axis=1))


def _rope64(x, c, s):
    return x * c + _swap32(x) * s


def _rope64_t(dy, c, s):
    return dy * c + _swap32(dy * s)


def _rope_tables(cfg, positions):
    pos = positions.astype(f32)[0][:, None]

    def tab(dim):
        inv_freq = ROPE_BASE ** (-jnp.arange(0, dim, 2, dtype=f32) / dim)
        ang = pos * inv_freq
        cos, sin = jnp.cos(ang), jnp.sin(ang)
        return jnp.concatenate([cos, cos], axis=1), jnp.concatenate([-sin, sin], axis=1)

    return tab(HEAD), tab(ROPE)


def _ret_consts(cfg):
    h = np.arange(cfg.H, dtype=np.float64)
    log_gamma = np.log1p(-np.exp2(-5.0 - h)).astype(np.float32)
    idx = np.arange(CHUNK, dtype=np.float32)
    intra = np.exp(log_gamma[:, None, None] * np.abs(idx[:, None] - idx[None, :]))
    kdec = np.exp(log_gamma[:, None] * (CHUNK - 1 - idx)[None, :])
    qdec = np.exp(log_gamma[:, None] * (idx + 1.0)[None, :])
    cdec = np.exp(log_gamma * CHUNK)
    bc = lambda a: jnp.asarray(np.broadcast_to(a[..., None], a.shape + (HEAD,)).astype(np.float32))
    return jnp.asarray(intra.astype(np.float32)), bc(kdec), bc(qdec), bc(cdec[:, None])


def _ret_core(cfg, q_raw, k_raw, v_raw, cos, sin, intra, kdec, qdec, cdec, p_ref):
    S = cfg.S
    NC = S // CHUNK
    q = _rope128(q_raw, cos, sin) * (HEAD ** -0.5)
    k = _rope128(k_raw, cos, sin)
    q3 = q.reshape(NC, CHUNK, HEAD)
    k3 = k.reshape(NC, CHUNK, HEAD)
    qb, kb = q3.astype(bf16), k3.astype(bf16)
    vb = v_raw.reshape(NC, CHUNK, HEAD).astype(bf16)
    sdb = (jnp.einsum('nid,njd->nij', qb, kb, preferred_element_type=f32) * intra[None]).astype(bf16)
    o_intra = jnp.einsum('nij,nje->nie', sdb, vb, preferred_element_type=f32)
    kdb = (k3 * kdec[None]).astype(bf16)
    kv = jnp.einsum('njd,nje->nde', kdb, vb, preferred_element_type=f32)
    p_ref[0] = jnp.zeros((HEAD, HEAD), f32)
    for n in range(1, NC):
        p_ref[n] = p_ref[n - 1] * cdec + kv[n - 1]
    pb = p_ref[...].astype(bf16)
    qdb = (q3 * qdec[None]).astype(bf16)
    o_inter = jnp.einsum('nid,nde->nie', qdb, pb, preferred_element_type=f32)
    o = (o_intra + o_inter).reshape(S, HEAD)
    return o, (qb, kb, vb, sdb, kdb, qdb, pb)


def _ret_specs(cfg):
    S = cfg.S
    hs = lambda off: pl.BlockSpec((S, HEAD), lambda h, _c=off // HEAD: (0, _c + h))
    full = pl.BlockSpec((S, HEAD), lambda h: (0, 0))
    consts = [pl.BlockSpec((None, CHUNK, CHUNK), lambda h: (h, 0, 0)), pl.BlockSpec((None, CHUNK, HEAD), lambda h: (h, 0, 0)),
              pl.BlockSpec((None, CHUNK, HEAD), lambda h: (h, 0, 0)), pl.BlockSpec((None, 1, HEAD), lambda h: (h, 0, 0))]
    gn = pl.BlockSpec((1, HEAD), lambda h: (0, h))
    return hs, full, consts, gn


def _ret_fwd(cfg, proj, gn, cos, sin, consts):
    S, NC = cfg.S, cfg.S // CHUNK
    hs, full, cspecs, gspec = _ret_specs(cfg)

    def body(q_ref, k_ref, v_ref, g_ref, gn_ref, cos_ref, sin_ref, intra, kdec, qdec, cdec, y_ref, p_ref):
        o, _ = _ret_core(cfg, q_ref[...], k_ref[...], v_ref[...], cos_ref[...], sin_ref[...],
                         intra[...], kdec[...], qdec[...], cdec[...], p_ref)
        mean = jnp.mean(o, axis=-1, keepdims=True)
        var = jnp.mean(jnp.square(o - mean), axis=-1, keepdims=True)
        z = ((o - mean) * lax.rsqrt(var + NORM_EPS)) * gn_ref[...]
        y_ref[...] = (z * _silu(g_ref[...])).astype(bf16)

    return pl.pallas_call(
        body, name="ret_fwd", grid=(cfg.H,),
        in_specs=[hs(0), hs(cfg.o_rk), hs(cfg.o_rv), hs(cfg.o_rg), gspec, full, full] + cspecs,
        out_specs=hs(0), out_shape=jax.ShapeDtypeStruct((S, cfg.RW), bf16),
        scratch_shapes=[pltpu.VMEM((NC, HEAD, HEAD), f32)],
        compiler_params=_cparams(("arbitrary",)),
    )(proj, proj, proj, proj, gn, cos, sin, *consts)


def _ret_bwd(cfg, proj, dy, gn, cos, sin, consts):
    S, NC = cfg.S, cfg.S // CHUNK
    hs, full, cspecs, gspec = _ret_specs(cfg)

    def body(q_ref, k_ref, v_ref, g_ref, dy_ref, gn_ref, cos_ref, sin_ref, intra_ref, kdec_ref, qdec_ref, cdec_ref,
             dq_ref, dk_ref, dv_ref, dg_ref, dgn_ref, p_ref, g_scr):
        cos, sin = cos_ref[...], sin_ref[...]
        intra, kdec, qdec, cdec = intra_ref[...], kdec_ref[...], qdec_ref[...], cdec_ref[...]
        o, (qb, kb, vb, sdb, kdb, qdb, pb) = _ret_core(cfg, q_ref[...], k_ref[...], v_ref[...], cos, sin,
                                                     intra, kdec, qdec, cdec, p_ref)
        gate, dy, gnv = g_ref[...], dy_ref[...], gn_ref[...]
        mean = jnp.mean(o, axis=-1, keepdims=True)
        rstd = lax.rsqrt(jnp.mean(jnp.square(o - mean), axis=-1, keepdims=True) + NORM_EPS)
        on = (o - mean) * rstd
        dz = dy * _silu(gate)
        dg_ref[...] = (dy * (on * gnv) * _dsilu(gate)).astype(bf16)
        dgn_ref[...] = jnp.sum(dz * on, axis=0, keepdims=True)
        don = dz * gnv
        do = rstd * (don - jnp.mean(don, axis=-1, keepdims=True) - on * jnp.mean(don * on, axis=-1, keepdims=True))
        dob = do.reshape(NC, CHUNK, HEAD).astype(bf16)
        dsb = (jnp.einsum('nie,nje->nij', dob, vb, preferred_element_type=f32) * intra[None]).astype(bf16)
        dv = jnp.einsum('nij,nie->nje', sdb, dob, preferred_element_type=f32)
        dq = jnp.einsum('nij,njd->nid', dsb, kb, preferred_element_type=f32)
        dk = jnp.einsum('nij,nid->njd', dsb, qb, preferred_element_type=f32)
        dq = dq + jnp.einsum('nie,nde->nid', dob, pb, preferred_element_type=f32) * qdec[None]
        dp = jnp.einsum('nid,nie->nde', qdb, dob, preferred_element_type=f32)
        g_scr[NC - 1] = jnp.zeros((HEAD, HEAD), f32)
        for n in range(NC - 2, -1, -1):
            g_scr[n] = dp[n + 1] + g_scr[n + 1] * cdec
        gb = g_scr[...].astype(bf16)
        dk = dk + jnp.einsum('nje,nde->njd', vb, gb, preferred_element_type=f32) * kdec[None]
        dv = dv + jnp.einsum('njd,nde->nje', kdb, gb, preferred_element_type=f32)
        dq_ref[...] = _rope128_t(dq.reshape(S, HEAD) * (HEAD ** -0.5), cos, sin).astype(bf16)
        dk_ref[...] = _rope128_t(dk.reshape(S, HEAD), cos, sin).astype(bf16)
        dv_ref[...] = dv.reshape(S, HEAD).astype(bf16)

    return pl.pallas_call(
        body, name="ret_bwd", grid=(cfg.H,),
        in_specs=[hs(0), hs(cfg.o_rk), hs(cfg.o_rv), hs(cfg.o_rg), hs(0), gspec, full, full] + cspecs,
        out_specs=(hs(0),) * 4 + (gspec,),
        out_shape=(jax.ShapeDtypeStruct((S, cfg.RW), bf16),) * 4 + (jax.ShapeDtypeStruct((1, cfg.RW), f32),),
        scratch_shapes=[pltpu.VMEM((NC, HEAD, HEAD), f32), pltpu.VMEM((NC, HEAD, HEAD), f32)],
        compiler_params=_cparams(("arbitrary",)),
    )(proj, proj, proj, proj, dy, gn, cos, sin, *consts)


def _expm1(x):
    small = x * (1.0 + x * (0.5 + x * (1.0 / 6.0 + x * (1.0 / 24.0 + x * (1.0 / 120.0)))))
    return jnp.where(jnp.abs(x) < 0.1, small, jnp.exp(x) - 1.0)


def _softplus(z):
    return jnp.maximum(z, 0.0) + jnp.log1p(jnp.exp(-jnp.abs(z)))


def _lru_conv(cfg, x_ref, halo_ref, cw, scr, first):
    TR = cfg.TR
    scr[0:SUBLANES, :] = jnp.where(first, 0.0, halo_ref[...])
    scr[SUBLANES:SUBLANES + TR, :] = x_ref[...]
    xc = scr[pl.ds(SUBLANES - (CONV - 1), TR), :] * cw[0:1, :]
    for j in range(1, CONV):
        xc = xc + scr[pl.ds(SUBLANES - (CONV - 1) + j, TR), :] * cw[j:j + 1, :]
    return xc


def _lru_pre(cfg, xc, wa_ref, wx_ref, ba, bx):
    xb = xc.astype(bf16)
    pa = jnp.concatenate([jnp.dot(xb[:, n * HEAD:(n + 1) * HEAD], wa_ref[n].astype(bf16), preferred_element_type=f32)
                          for n in range(cfg.NB)], axis=1) + ba
    px = jnp.concatenate([jnp.dot(xb[:, n * HEAD:(n + 1) * HEAD], wx_ref[n].astype(bf16), preferred_element_type=f32)
                          for n in range(cfg.NB)], axis=1) + bx
    return pa, px


def _lru_ab(pa, px, xc, lam):
    r, i = _sigmoid(pa), _sigmoid(px)
    log_a = (-LRU_C * r) * _softplus(-lam)
    a = jnp.exp(log_a)
    b = jnp.sqrt(-_expm1(2.0 * log_a)) * (i * xc)
    return a, b


def _lru_halo_specs(cfg, off, W):
    TR, S = cfg.TR, cfg.S
    nb = TR // SUBLANES
    cb = off // W
    main = pl.BlockSpec((TR, W), lambda i: (i, cb))
    prev = pl.BlockSpec((SUBLANES, W), lambda i: (jnp.maximum(i * nb - 1, 0), cb))
    nxt = pl.BlockSpec((SUBLANES, W), lambda i: (jnp.minimum((i + 1) * nb, S // SUBLANES - 1), cb))
    return main, prev, nxt


def _lru_gates(cfg, proj, cw, cb, wa, ba, wx, bx, lam):
    S, W, TR, NB = cfg.S, cfg.LW, cfg.TR, cfg.NB
    assert cfg.o_lx % W == 0
    main, prev, _ = _lru_halo_specs(cfg, cfg.o_lx, W)
    wspec = pl.BlockSpec((NB, HEAD, HEAD), lambda i: (0, 0, 0))

    def body(x_ref, halo_ref, cw_ref, cb_ref, wa_ref, ba_ref, wx_ref, bx_ref, lam_ref, a_ref, b_ref, scr):
        xc = _lru_conv(cfg, x_ref, halo_ref, cw_ref[...], scr, pl.program_id(0) == 0) + cb_ref[...]
        pa, px = _lru_pre(cfg, xc, wa_ref, wx_ref, ba_ref[...], bx_ref[...])
        a, b = _lru_ab(pa, px, xc, lam_ref[...])
        a_ref[...] = a
        b_ref[...] = b

    return pl.pallas_call(
        body, name="lru_gates", grid=(S // TR,),
        in_specs=[main, prev, pl.BlockSpec((CONV, W), lambda i: (0, 0)), _row(W), wspec, _row(W), wspec, _row(W), _row(W)],
        out_specs=(_slab(TR, W, 0),) * 2, out_shape=(jax.ShapeDtypeStruct((S, W), f32),) * 2,
        scratch_shapes=[pltpu.VMEM((TR + SUBLANES, W), f32)],
        compiler_params=_cparams(("parallel",)),
    )(proj, proj, cw, cb, wa, ba, wx, bx, lam)


def _lru_lane_block(cfg):
    return 256 if cfg.LW % 256 == 0 else LANES


def _lru_scan_fwd(cfg, proj, a, b):
    S, W = cfg.S, cfg.LW
    LB = _lru_lane_block(cfg)
    assert cfg.o_lg % LB == 0
    col = lambda off: pl.BlockSpec((S, LB), lambda j, _c=off // LB: (0, _c + j))

    def body(a_ref, b_ref, g_ref, h_ref, y_ref):
        def blk(t, h):
            r0 = pl.multiple_of(t * SUBLANES, SUBLANES)
            at, bt = a_ref[pl.ds(r0, SUBLANES), :], b_ref[pl.ds(r0, SUBLANES), :]
            rows = []
            for j in range(SUBLANES):
                h = at[j:j + 1, :] * h + bt[j:j + 1, :]
                rows.append(h)
            h_ref[pl.ds(r0, SUBLANES), :] = jnp.concatenate(rows, axis=0)
            return h

        lax.fori_loop(0, S // SUBLANES, blk, jnp.zeros((1, LB), f32))
        y_ref[...] = (h_ref[...] * _silu(g_ref[...])).astype(bf16)

    return pl.pallas_call(
        body, name="lru_scan_fwd", grid=(W // LB,),
        in_specs=[col(0), col(0), col(cfg.o_lg)],
        out_specs=(col(0), col(0)),
        out_shape=(jax.ShapeDtypeStruct((S, W), f32), jax.ShapeDtypeStruct((S, W), bf16)),
        compiler_params=_cparams(("parallel",)),
    )(a, b, proj)


def _lru_scan_bwd(cfg, proj, a, h, dy):
    S, W = cfg.S, cfg.LW
    LB = _lru_lane_block(cfg)
    col = lambda off: pl.BlockSpec((S, LB), lambda j, _c=off // LB: (0, _c + j))

    def body(a_ref, h_ref, dy_ref, g_ref, da_ref, db_ref, dg_ref):
        gate, dy = g_ref[...], dy_ref[...]
        dg_ref[...] = (dy * h_ref[...] * _dsilu(gate)).astype(bf16)
        da_ref[...] = dy * _silu(gate)

        def blk(t, carry):
            dh_next, a_next = carry
            r0 = pl.multiple_of((S // SUBLANES - 1 - t) * SUBLANES, SUBLANES)
            at, ct = a_ref[pl.ds(r0, SUBLANES), :], da_ref[pl.ds(r0, SUBLANES), :]
            rows = [None] * SUBLANES
            for j in range(SUBLANES - 1, -1, -1):
                dh_next = ct[j:j + 1, :] + a_next * dh_next
                a_next = at[j:j + 1, :]
                rows[j] = dh_next
            db_ref[pl.ds(r0, SUBLANES), :] = jnp.concatenate(rows, axis=0)
            return dh_next, a_next

        z = jnp.zeros((1, LB), f32)
        lax.fori_loop(0, S // SUBLANES, blk, (z, z))
        row = lax.broadcasted_iota(jnp.int32, (S, LB), 0)
        hprev = jnp.where(row == 0, 0.0, pltpu.roll(h_ref[...], 1, axis=0))
        da_ref[...] = db_ref[...] * hprev

    return pl.pallas_call(
        body, name="lru_scan_bwd", grid=(W // LB,),
        in_specs=[col(0), col(0), col(0), col(cfg.o_lg)],
        out_specs=(col(0),) * 3,
        out_shape=(jax.ShapeDtypeStruct((S, W), f32),) * 2 + (jax.ShapeDtypeStruct((S, W), bf16),),
        compiler_params=_cparams(("parallel",)),
    )(a, h, dy, proj)


def _lru_gates_bwd(cfg, proj, da, db, cw, cb, wa, ba, wx, bx, lam):
    S, W, TR, NB = cfg.S, cfg.LW, cfg.TR, cfg.NB
    main, prev, _ = _lru_halo_specs(cfg, cfg.o_lx, W)
    wspec = pl.BlockSpec((NB, HEAD, HEAD), lambda i: (0, 0, 0))

    def body(x_ref, halo_ref, da_ref, db_ref, cw_ref, cb_ref, wa_ref, ba_ref, wx_ref, bx_ref, lam_ref,
             dxc_ref, dwa_ref, dwx_ref, sum_ref, scr):
        i = pl.program_id(0)
        lam = lam_ref[...]
        xc = _lru_conv(cfg, x_ref, halo_ref, cw_ref[...], scr, i == 0) + cb_ref[...]
        pa, px = _lru_pre(cfg, xc, wa_ref, wx_ref, ba_ref[...], bx_ref[...])
        _, vjp = jax.vjp(_lru_ab, pa, px, xc, lam)
        dpa, dpx, dxc, dlam = vjp((da_ref[...], db_ref[...]))
        xb, dpab, dpxb = xc.astype(bf16), dpa.astype(bf16), dpx.astype(bf16)
        nt = (((1,), (1,)), ((), ()))
        tn = (((0,), (0,)), ((), ()))
        back = []
        dwa, dwx = [], []
        for n in range(NB):
            sl = slice(n * HEAD, (n + 1) * HEAD)
            back.append(lax.dot_general(dpab[:, sl], wa_ref[n].astype(bf16), nt, preferred_element_type=f32)
                        + lax.dot_general(dpxb[:, sl], wx_ref[n].astype(bf16), nt, preferred_element_type=f32))
            dwa.append(lax.dot_general(xb[:, sl], dpab[:, sl], tn, preferred_element_type=f32))
            dwx.append(lax.dot_general(xb[:, sl], dpxb[:, sl], tn, preferred_element_type=f32))
        dxc_ref[...] = dxc + jnp.concatenate(back, axis=1)
        part = jnp.concatenate([jnp.sum(dpa, axis=0, keepdims=True), jnp.sum(dpx, axis=0, keepdims=True), dlam,
                                jnp.zeros((SUBLANES - 3, W), f32)], axis=0)

        @pl.when(i == 0)
        def _():
            sum_ref[...] = part
            for n in range(NB):
                dwa_ref[n] = dwa[n]
                dwx_ref[n] = dwx[n]

        @pl.when(i > 0)
        def _():
            sum_ref[...] += part
            for n in range(NB):
                dwa_ref[n] += dwa[n]
                dwx_ref[n] += dwx[n]

    return pl.pallas_call(
        body, name="lru_gates_bwd", grid=(S // TR,),
        in_specs=[main, prev, _slab(TR, W, 0), _slab(TR, W, 0), pl.BlockSpec((CONV, W), lambda i: (0, 0)), _row(W),
                  wspec, _row(W), wspec, _row(W), _row(W)],
        out_specs=(_slab(TR, W, 0), wspec, wspec, pl.BlockSpec((SUBLANES, W), lambda i: (0, 0))),
        out_shape=(jax.ShapeDtypeStruct((S, W), f32), jax.ShapeDtypeStruct((NB, HEAD, HEAD), f32),
                   jax.ShapeDtypeStruct((NB, HEAD, HEAD), f32), jax.ShapeDtypeStruct((SUBLANES, W), f32)),
        scratch_shapes=[pltpu.VMEM((TR + SUBLANES, W), f32)],
        compiler_params=_cparams(("arbitrary",)),
    )(proj, proj, da, db, cw, cb, wa, ba, wx, bx, lam)


def _lru_conv_bwd(cfg, proj, dxc, cw):
    S, W, TR = cfg.S, cfg.LW, cfg.TR
    main, prev, _ = _lru_halo_specs(cfg, cfg.o_lx, W)
    dmain, _, dnext = _lru_halo_specs(cfg, 0, W)

    def body(x_ref, xhalo_ref, d_ref, dhalo_ref, cw_ref, dx_ref, sum_ref, xs, ds):
        i = pl.program_id(0)
        cw = cw_ref[...]
        d = d_ref[...]
        xs[0:SUBLANES, :] = jnp.where(i == 0, 0.0, xhalo_ref[...])
        xs[SUBLANES:SUBLANES + TR, :] = x_ref[...]
        ds[0:TR, :] = d
        ds[TR:TR + SUBLANES, :] = jnp.where(i == pl.num_programs(0) - 1, 0.0, dhalo_ref[...])
        dx = ds[pl.ds(CONV - 1, TR), :] * cw[0:1, :]
        parts = [jnp.sum(d * xs[pl.ds(SUBLANES - (CONV - 1), TR), :], axis=0, keepdims=True)]
        for j in range(1, CONV):
            dx = dx + ds[pl.ds(CONV - 1 - j, TR), :] * cw[j:j + 1, :]
            parts.append(jnp.sum(d * xs[pl.ds(SUBLANES - (CONV - 1) + j, TR), :], axis=0, keepdims=True))
        dx_ref[...] = dx.astype(bf16)
        part = jnp.concatenate(parts + [jnp.sum(d, axis=0, keepdims=True), jnp.zeros((SUBLANES - CONV - 1, W), f32)], axis=0)

        @pl.when(i == 0)
        def _():
            sum_ref[...] = part

        @pl.when(i > 0)
        def _():
            sum_ref[...] += part

    return pl.pallas_call(
        body, name="lru_conv_bwd", grid=(S // TR,),
        in_specs=[main, prev, dmain, dnext, pl.BlockSpec((CONV, W), lambda i: (0, 0))],
        out_specs=(_slab(TR, W, 0), pl.BlockSpec((SUBLANES, W), lambda i: (0, 0))),
        out_shape=(jax.ShapeDtypeStruct((S, W), bf16), jax.ShapeDtypeStruct((SUBLANES, W), f32)),
        scratch_shapes=[pltpu.VMEM((TR + SUBLANES, W), f32), pltpu.VMEM((TR + SUBLANES, W), f32)],
        compiler_params=_cparams(("arbitrary",)),
    )(proj, proj, dxc, dxc, cw)


def _rms(x, g):
    r = lax.rsqrt(jnp.mean(x * x, axis=-1, keepdims=True) + NORM_EPS)
    return (x * r) * g, r


def _mla_norm(cfg, proj, qg, kg):
    S, TR = cfg.S, cfg.TR

    def body(q_ref, k_ref, qg_ref, kg_ref, qn_ref, kn_ref):
        qn_ref[...] = _rms(q_ref[...], qg_ref[...])[0].astype(bf16)
        kn_ref[...] = _rms(k_ref[...], kg_ref[...])[0].astype(bf16)

    return pl.pallas_call(
        body, name="mla_norm", grid=(S // TR,),
        in_specs=[_slab(TR, cfg.QL, cfg.o_mq), _slab(TR, cfg.KL, cfg.o_mkv), _row(cfg.QL), _row(cfg.KL)],
        out_specs=(_slab(TR, cfg.QL, 0), _slab(TR, cfg.KL, 0)),
        out_shape=(jax.ShapeDtypeStruct((S, cfg.QL), bf16), jax.ShapeDtypeStruct((S, cfg.KL), bf16)),
        compiler_params=_cparams(("parallel",)),
    )(proj, proj, qg, kg)


def _mla_norm_bwd(cfg, proj, dqn, dkn, qg, kg):
    S, TR = cfg.S, cfg.TR

    def one(x, g, dn):
        r = lax.rsqrt(jnp.mean(x * x, axis=-1, keepdims=True) + NORM_EPS)
        xn = x * r
        dxn = dn * g
        dx = r * (dxn - xn * jnp.mean(dxn * xn, axis=-1, keepdims=True))
        return dx, jnp.sum(dn * xn, axis=0, keepdims=True)

    def body(q_ref, k_ref, dq_ref, dk_ref, qg_ref, kg_ref, dmq_ref, dmk_ref, sq_ref, sk_ref):
        i = pl.program_id(0)
        dq, gq = one(q_ref[...], qg_ref[...], dq_ref[...])
        dk, gk = one(k_ref[...], kg_ref[...], dk_ref[...])
        dmq_ref[...] = dq.astype(bf16)
        dmk_ref[...] = dk.astype(bf16)
        pq = jnp.concatenate([gq, jnp.zeros((SUBLANES - 1, cfg.QL), f32)], axis=0)
        pk = jnp.concatenate([gk, jnp.zeros((SUBLANES - 1, cfg.KL), f32)], axis=0)

        @pl.when(i == 0)
        def _():
            sq_ref[...] = pq
            sk_ref[...] = pk

        @pl.when(i > 0)
        def _():
            sq_ref[...] += pq
            sk_ref[...] += pk

    return pl.pallas_call(
        body, name="mla_norm_bwd", grid=(S // TR,),
        in_specs=[_slab(TR, cfg.QL, cfg.o_mq), _slab(TR, cfg.KL, cfg.o_mkv), _slab(TR, cfg.QL, 0), _slab(TR, cfg.KL, 0),
                  _row(cfg.QL), _row(cfg.KL)],
        out_specs=(_slab(TR, cfg.QL, 0), _slab(TR, cfg.KL, 0), pl.BlockSpec((SUBLANES, cfg.QL), lambda i: (0, 0)),
                   pl.BlockSpec((SUBLANES, cfg.KL), lambda i: (0, 0))),
        out_shape=(jax.ShapeDtypeStruct((S, cfg.QL), bf16), jax.ShapeDtypeStruct((S, cfg.KL), bf16),
                   jax.ShapeDtypeStruct((SUBLANES, cfg.QL), f32), jax.ShapeDtypeStruct((SUBLANES, cfg.KL), f32)),
        compiler_params=_cparams(("arbitrary",)),
    )(proj, proj, dqn, dkn, qg, kg)


def _mla_pack(cfg, proj, q, kv, cq, sq, ck, sk):
    S, TR, MH = cfg.S, cfg.TR, cfg.MH
    NW, RWD = MH * HEAD, MH * ROPE

    def body(q_ref, kv_ref, kr_ref, cq_ref, sq_ref, ck_ref, sk_ref, qo_ref, ko_ref, vo_ref):
        q, kv = q_ref[...], kv_ref[...]
        qr = _rope64(q[:, NW:], cq_ref[...], sq_ref[...])
        kr = _rope64(kr_ref[...], ck_ref[...], sk_ref[...]).astype(bf16)
        lane = lax.broadcasted_iota(jnp.int32, (TR, HEAD), 1)
        for h in range(MH):
            grp = qr[:, (h // 2) * HEAD:(h // 2 + 1) * HEAD]
            if h % 2:
                grp = pltpu.roll(grp, 64, axis=1)
            qo_ref[h] = jnp.concatenate([q[:, h * HEAD:(h + 1) * HEAD], jnp.where(lane < ROPE, grp, 0.0)], axis=1).astype(bf16)
            ko_ref[h] = jnp.concatenate([kv[:, 2 * h * HEAD:(2 * h + 1) * HEAD].astype(bf16), kr], axis=1)
            vo_ref[h] = kv[:, (2 * h + 1) * HEAD:(2 * h + 2) * HEAD].astype(bf16)

    hspec = lambda w: pl.BlockSpec((MH, TR, w), lambda i: (0, i, 0))
    return pl.pallas_call(
        body, name="mla_pack", grid=(S // TR,),
        in_specs=[_slab(TR, cfg.QW, 0), _slab(TR, cfg.KVW, 0), _slab(TR, HEAD, cfg.o_mkr),
                  _slab(TR, RWD, 0), _slab(TR, RWD, 0), _slab(TR, HEAD, 0), _slab(TR, HEAD, 0)],
        out_specs=(hspec(2 * HEAD), hspec(2 * HEAD), hspec(HEAD)),
        out_shape=(jax.ShapeDtypeStruct((MH, S, 2 * HEAD), bf16), jax.ShapeDtypeStruct((MH, S, 2 * HEAD), bf16),
                   jax.ShapeDtypeStruct((MH, S, HEAD), bf16)),
        compiler_params=_cparams(("parallel",)),
    )(q, kv, proj, cq, sq, ck, sk)


def _mla_unpack_bwd(cfg, dq3, dk3, dv3, cq, sq, ck, sk):
    S, TR, MH = cfg.S, cfg.TR, cfg.MH
    RWD = MH * ROPE

    def body(dq_ref, dk_ref, dv_ref, cq_ref, sq_ref, ck_ref, sk_ref, q_ref, kv_ref, kr_ref):
        lane = lax.broadcasted_iota(jnp.int32, (TR, HEAD), 1)
        nope, ropes, kvs = [], [], []
        dkr = jnp.zeros((TR, HEAD), f32)
        for h in range(MH):
            dq = dq_ref[h]
            nope.append(dq[:, :HEAD])
            part = jnp.where(lane < ROPE, dq[:, HEAD:], 0.0)
            if h % 2:
                ropes[-1] = ropes[-1] + pltpu.roll(part, 64, axis=1)
            else:
                ropes.append(part)
            dk = dk_ref[h]
            kvs += [dk[:, :HEAD], dv_ref[h]]
            dkr = dkr + dk[:, HEAD:]
        dqr = _rope64_t(jnp.concatenate(ropes, axis=1), cq_ref[...], sq_ref[...])
        q_ref[...] = jnp.concatenate(nope + [dqr], axis=1).astype(bf16)
        kv_ref[...] = jnp.concatenate(kvs, axis=1).astype(bf16)
        dkr = jnp.where(lane < ROPE, dkr, 0.0)
        kr_ref[...] = _rope64_t(dkr, ck_ref[...], sk_ref[...]).astype(bf16)

    hspec = lambda w: pl.BlockSpec((MH, TR, w), lambda i: (0, i, 0))
    return pl.pallas_call(
        body, name="mla_unpack_bwd", grid=(S // TR,),
        in_specs=[hspec(2 * HEAD), hspec(2 * HEAD), hspec(HEAD), _slab(TR, RWD, 0), _slab(TR, RWD, 0),
                  _slab(TR, HEAD, 0), _slab(TR, HEAD, 0)],
        out_specs=(_slab(TR, cfg.QW, 0), _slab(TR, cfg.KVW, 0), _slab(TR, HEAD, 0)),
        out_shape=(jax.ShapeDtypeStruct((S, cfg.QW), bf16), jax.ShapeDtypeStruct((S, cfg.KVW), bf16),
                   jax.ShapeDtypeStruct((S, HEAD), bf16)),
        compiler_params=_cparams(("parallel",)),
    )(dq3, dk3, dv3, cq, sq, ck, sk)


def _mla_probs(cfg, q, k, i):
    TQ, S = cfg.TQ, cfg.S
    nt = (((1,), (1,)), ((), ()))
    s = lax.dot_general(q, k, nt, preferred_element_type=f32) * ((HEAD + ROPE) ** -0.5)
    qc = (i * TQ + lax.broadcasted_iota(jnp.int32, (TQ, S), 0)) // CHUNK
    kc = lax.broadcasted_iota(jnp.int32, (TQ, S), 1) // CHUNK
    s = jnp.where(kc <= qc, s, -1e30)
    m = jnp.max(s, axis=-1, keepdims=True)
    e = jnp.exp(s - m)
    return e / jnp.sum(e, axis=-1, keepdims=True)


def _mla_attn_specs(cfg):
    S, TQ = cfg.S, cfg.TQ
    qs = lambda w: pl.BlockSpec((None, TQ, w), lambda h, i: (h, i, 0))
    ks = lambda w: pl.BlockSpec((None, S, w), lambda h, i: (h, 0, 0))
    hs = lambda off: pl.BlockSpec((TQ, HEAD), lambda h, i, _c=off // HEAD: (i, _c + h))
    return qs, ks, hs


def _mla_attn_fwd(cfg, proj, q3, k3, v3):
    S, TQ, MH = cfg.S, cfg.TQ, cfg.MH
    qs, ks, hs = _mla_attn_specs(cfg)

    def body(q_ref, k_ref, v_ref, g_ref, o_ref, y_ref):
        p = _mla_probs(cfg, q_ref[...], k_ref[...], pl.program_id(1))
        o = jnp.dot(p.astype(bf16), v_ref[...], preferred_element_type=f32)
        o_ref[...] = o
        y_ref[...] = (o * _silu(g_ref[...])).astype(bf16)

    return pl.pallas_call(
        body, name="mla_attn_fwd", grid=(MH, S // TQ),
        in_specs=[qs(2 * HEAD), ks(2 * HEAD), ks(HEAD), hs(cfg.o_mg)],
        out_specs=(hs(0), hs(0)),
        out_shape=(jax.ShapeDtypeStruct((S, cfg.MW), f32), jax.ShapeDtypeStruct((S, cfg.MW), bf16)),
        compiler_params=_cparams(("parallel", "parallel")),
    )(q3, k3, v3, proj)


def _mla_attn_bwd(cfg, proj, q3, k3, v3, o, dy):
    S, TQ, MH = cfg.S, cfg.TQ, cfg.MH
    qs, ks, hs = _mla_attn_specs(cfg)

    def body(q_ref, k_ref, v_ref, g_ref, o_ref, dy_ref, dq_ref, dk_ref, dv_ref, dg_ref):
        i = pl.program_id(1)
        q, k, v = q_ref[...], k_ref[...], v_ref[...]
        gate, dy, o = g_ref[...], dy_ref[...], o_ref[...]
        dg_ref[...] = (dy * o * _dsilu(gate)).astype(bf16)
        dob = (dy * _silu(gate)).astype(bf16)
        p = _mla_probs(cfg, q, k, i)
        nt = (((1,), (1,)), ((), ()))
        tn = (((0,), (0,)), ((), ()))
        dv = lax.dot_general(p.astype(bf16), dob, tn, preferred_element_type=f32)
        dp = lax.dot_general(dob, v, nt, preferred_element_type=f32)
        ds = (p * (dp - jnp.sum(dp * p, axis=-1, keepdims=True)) * ((HEAD + ROPE) ** -0.5)).astype(bf16)
        dq_ref[...] = jnp.dot(ds, k, preferred_element_type=f32)
        dk = lax.dot_general(ds, q, tn, preferred_element_type=f32)

        @pl.when(i == 0)
        def _():
            dk_ref[...] = dk
            dv_ref[...] = dv

        @pl.when(i > 0)
        def _():
            dk_ref[...] += dk
            dv_ref[...] += dv

    return pl.pallas_call(
        body, name="mla_attn_bwd", grid=(MH, S // TQ),
        in_specs=[qs(2 * HEAD), ks(2 * HEAD), ks(HEAD), hs(cfg.o_mg), hs(0), hs(0)],
        out_specs=(qs(2 * HEAD), ks(2 * HEAD), ks(HEAD), hs(0)),
        out_shape=(jax.ShapeDtypeStruct((MH, S, 2 * HEAD), f32), jax.ShapeDtypeStruct((MH, S, 2 * HEAD), f32),
                   jax.ShapeDtypeStruct((MH, S, HEAD), f32), jax.ShapeDtypeStruct((S, cfg.MW), bf16)),
        compiler_params=_cparams(("parallel", "arbitrary")),
    )(q3, k3, v3, proj, o, dy)


def _pick_rows(R, bytes_per_row):
    if R * bytes_per_row <= MM_BUDGET:
        return R
    best = None
    for t in range(16, R, 16):
        if R % t == 0 and t * bytes_per_row <= MM_BUDGET:
            best = t
    assert best is not None, (R, bytes_per_row)
    return best


def _adamw(w, g, m, v, name="adamw"):
    R, C = w.shape
    tr = _pick_rows(R, C * 4 * 7 * 2)
    c1 =1.0 - ADAM_B1 ** ADAM_STEP
    c2 = 1.0 - ADAM_B2 ** ADAM_STEP

    def body(w_ref, g_ref, m_ref, v_ref, d_ref, mo_ref, vo_ref):
        g = g_ref[...]
        m = ADAM_B1 * m_ref[...] + (1.0 - ADAM_B1) * g
        v = ADAM_B2 * v_ref[...] + (1.0 - ADAM_B2) * jnp.square(g)
        d_ref[...] = -ADAM_LR * ((m / c1) / (jnp.sqrt(v / c2) + ADAM_EPS) + ADAM_WD * w_ref[...])
        mo_ref[...] = m
        vo_ref[...] = v

    spec = pl.BlockSpec((tr, C), lambda i: (i, 0))
    return pl.pallas_call(
        body, name=name, grid=(R // tr,), in_specs=[spec] * 4, out_specs=(spec,) * 3,
        out_shape=(jax.ShapeDtypeStruct((R, C), f32),) * 3,
        compiler_params=_cparams(("parallel",)),
    )(w, g, m, v)


def _adamw_big(w, m, v, mines, others, core, name, half_cols=False):
    L, R, C = w.shape
    hr, hc = (R, C // 2) if half_cols else (R // 2, C)
    tr = _pick_rows(hr, hc * 4 * (7 + 2 * L) * 2)
    nt = hr // tr
    c1 = 1.0 - ADAM_B1 ** ADAM_STEP
    c2 = 1.0 - ADAM_B2 ** ADAM_STEP

    def body(core_ref, w_ref, m_ref, v_ref, *rest):
        g_refs, (go_ref, d_ref, mo_ref, vo_ref) = rest[:2 * L], rest[2 * L:]
        l, h = pl.program_id(0), pl.program_id(1)
        own = h == core_ref[0]
        g = jnp.where(own, g_refs[0][...], g_refs[L][...])
        for k in range(1, L):
            g = jnp.where(l == k, jnp.where(own, g_refs[k][...], g_refs[L + k][...]), g)
        m = ADAM_B1 * m_ref[...] + (1.0 - ADAM_B1) * g
        v = ADAM_B2 * v_ref[...] + (1.0 - ADAM_B2) * jnp.square(g)
        go_ref[...] = g
        d_ref[...] = -ADAM_LR * ((m / c1) / (jnp.sqrt(v / c2) + ADAM_EPS) + ADAM_WD * w_ref[...])
        mo_ref[...] = m
        vo_ref[...] = v

    if half_cols:
        lay = pl.BlockSpec((None, tr, hc), lambda l, h, i, core_ref: (l, i, h))
    else:
        lay = pl.BlockSpec((None, tr, hc), lambda l, h, i, core_ref: (l, h * nt + i, 0))
    gspec = lambda k: pl.BlockSpec((tr, hc), lambda l, h, i, core_ref: (jnp.where(l == k, i, 0), 0))
    return pl.pallas_call(
        body, name=name,
        grid_spec=pltpu.PrefetchScalarGridSpec(
            num_scalar_prefetch=1, grid=(L, 2, nt),
            in_specs=[lay, lay, lay] + [gspec(k) for k in range(L)] * 2, out_specs=(lay,) * 4),
        out_shape=(jax.ShapeDtypeStruct((L, R, C), f32),) * 4,
        compiler_params=_cparams(("arbitrary", "arbitrary", "arbitrary")),
    )(core, w, m, v, *mines, *others)


def _sum_blocks(x, out_dtype, name):
    n, R, C = x.shape
    tr = _pick_rows(R, C * 4 * (n + 1) * 2)

    def body(x_ref, o_ref):
        acc = x_ref[0].astype(f32)
        for k in range(1, n):
            acc = acc + x_ref[k].astype(f32)
        o_ref[...] = acc.astype(o_ref.dtype)

    return pl.pallas_call(
        body, name=name, grid=(R // tr,),
        in_specs=[pl.BlockSpec((n, tr, C), lambda i: (0, i, 0))], out_specs=pl.BlockSpec((tr, C), lambda i: (i, 0)),
        out_shape=jax.ShapeDtypeStruct((R, C), out_dtype),
        compiler_params=_cparams(("parallel",)),
    )(x)


def _hbm_specs(n):
    return [pl.BlockSpec(memory_space=pl.ANY)] * n


def _row_map(cfg):
    nc, k0 = cfg.IN_WIDTH // 4, cfg.o_mg

    def padded(o):
        return o if o < k0 else (cfg.o_mkr + o - k0 if o < k0 + ROPE else o - ROPE)

    cuts = {0, nc}
    for q in range(4):
        cuts |= {b - q * nc for b in (k0, k0 + ROPE) if q * nc < b < (q + 1) * nc}
    cuts = sorted(cuts)
    return [((l0, l1 - l0), tuple(padded(q * nc + l0) for q in range(4))) for l0, l1 in zip(cuts[:-1], cuts[1:])]


def _chip_start(q, starts):
    st = starts[0]
    for i in range(1, 4):
        st = jnp.where(q == i, starts[i], st)
    return pl.multiple_of(st, 16)


def _allgather8(shards, name, cfg=None, zeros=None):
    na = len(shards)
    rmap = _row_map(cfg) if cfg is not None else []
    npc = max(len(rmap), 1)

    def body(*refs):
        x_refs, out_refs = refs[:na], refs[na + 1:2 * na + 1]
        send_sems, recv_sems, local_sems = refs[2 * na + 1:]
        x, y, c = lax.axis_index("x"), lax.axis_index("y"), lax.axis_index("c")
        me, sibling = (x, y, c), (x, y, 1 - c)
        chips = [(1 - x, y), (x, 1 - y), (1 - x, 1 - y)]

        def wins(a, px, py, pc):
            m, n = shards[a].shape
            if a == 0 and rmap:
                cols = pl.ds(pl.multiple_of(pc * n, n), n)
                return [(pl.ds(l0, cnt), out_refs[0].at[pl.ds(_chip_start(2 * px + py, starts), cnt), cols])
                        for (l0, cnt), starts in rmap]
            return [(pl.ds(0, m), out_refs[a].at[pl.ds((4 * px + 2 * py + pc) * m, m), :])]

        def copies(a, k, block, to, from_x):
            return [pltpu.make_async_remote_copy(
                src_ref=x_refs[a].at[rows, :] if from_x else win, dst_ref=win, send_sem=send_sems.at[a, k, p],
                recv_sem=recv_sems.at[a, k, p], device_id=to, device_id_type=MESH)
                for p, (rows, win) in enumerate(wins(a, *block))]

        mine = [pltpu.make_async_copy(x_refs[a].at[rows, :], win, local_sems.at[a, p])
                for a in range(na) for p, (rows, win) in enumerate(wins(a, *me))]
        if zeros is not None:
            nz = zeros.shape[0]
            mine.append(pltpu.make_async_copy(refs[na], out_refs[0].at[pl.ds(cfg.NP - nz, nz), :], local_sems.at[0, npc]))
        for cp in mine:
            cp.start()
        first = []
        for a in range(na):
            first += copies(a, 0, me, sibling, True)
            for j, chip in enumerate(chips):
                first += copies(a, 1 + j, me, (*chip, c), True)
        for cp in first:
            cp.start()
        passed = []
        for j, chip in enumerate(chips):
            for a in range(na):
                for cp in copies(a, 1 + j, (*chip, c), me, False):
                    cp.wait_recv()
                fwd = copies(a, 4 + j, (*chip, c), sibling, False)
                for cp in fwd:
                    cp.start()
                passed += fwd
        for a in range(na):
            for cp in copies(a, 0, sibling, me, False):
                cp.wait_recv()
        for j, chip in enumerate(chips):
            for a in range(na):
                for cp in copies(a, 4 + j, (*chip, 1 - c), me, False):
                    cp.wait_recv()
        for cp in first + passed:
            cp.wait_send()
        for cp in mine:
            cp.wait()

    out_shape = [jax.ShapeDtypeStruct((N_DEV * s.shape[0], s.shape[1]), s.dtype) for s in shards]
    if rmap:
        out_shape[0] = jax.ShapeDtypeStruct((cfg.NP, cfg.D), shards[0].dtype)
    z = zeros if zeros is not None else jnp.zeros((SUBLANES, LANES), f32)
    return pl.pallas_call(
        body, name=name, out_shape=out_shape,
        in_specs=_hbm_specs(na + 1), out_specs=_hbm_specs(na),
        scratch_shapes=[pltpu.SemaphoreType.DMA((na, 7, npc)), pltpu.SemaphoreType.DMA((na, 7, npc)),
                        pltpu.SemaphoreType.DMA((na, npc + 1))],
    )(*shards, z)


def _send_sibling(arrays, name):
    na = len(arrays)

    def body(*refs):
        x_refs, out_refs = refs[:na], refs[na:2 * na]
        send_sems, recv_sems = refs[2 * na:]
        sibling = (lax.axis_index("x"), lax.axis_index("y"), 1 - lax.axis_index("c"))
        cps = [pltpu.make_async_remote_copy(src_ref=x_refs[a], dst_ref=out_refs[a], send_sem=send_sems.at[a],
                                            recv_sem=recv_sems.at[a], device_id=sibling, device_id_type=MESH)
               for a in range(na)]
        for cp in cps:
            cp.start()
        for cp in cps:
            cp.wait()

    return pl.pallas_call(
        body, name=name, out_shape=[jax.ShapeDtypeStruct(x.shape, x.dtype) for x in arrays],
        in_specs=_hbm_specs(na), out_specs=_hbm_specs(na),
        scratch_shapes=[pltpu.SemaphoreType.DMA((na,)), pltpu.SemaphoreType.DMA((na,))],
    )(*arrays)


def _scatter_chips(cfg, arrays, name):
    na = len(arrays)
    rmap = _row_map(cfg)
    npc = len(rmap)
    out_shape = [jax.ShapeDtypeStruct((4, cfg.IN_WIDTH // 4, arrays[0].shape[1]), arrays[0].dtype)]
    out_shape += [jax.ShapeDtypeStruct(p.shape, p.dtype) for p in arrays[1:]]

    def body(*refs):
        p_refs, out_refs = refs[:na], refs[na:2 * na]
        send_sems, recv_sems, local_sems = refs[2 * na:]
        x, y, c = lax.axis_index("x"), lax.axis_index("y"), lax.axis_index("c")
        mychip = 2 * x + y
        chips = [(1 - x, y), (x, 1 - y), (1 - x, 1 - y)]

        def pairs(a, to_chip, slot):
            if a == 0:
                return [(p_refs[0].at[pl.ds(_chip_start(to_chip, starts), cnt), :], out_refs[0].at[slot, pl.ds(l0, cnt), :])
                        for (l0, cnt), starts in rmap]
            return [(p_refs[a].at[to_chip], out_refs[a].at[slot])]

        mine = [pltpu.make_async_copy(src, dst, local_sems.at[a, p])
                for a in range(na) for p, (src, dst) in enumerate(pairs(a, mychip, mychip))]
        cps = [pltpu.make_async_remote_copy(src_ref=src, dst_ref=dst, send_sem=send_sems.at[a, j, p], recv_sem=recv_sems.at[a, j, p],
                                            device_id=(cx, cy, c), device_id_type=MESH)
               for j, (cx, cy) in enumerate(chips) for a in range(na) for p, (src, dst) in enumerate(pairs(a, 2 * cx + cy, mychip))]
        for cp in mine + cps:
            cp.start()
        for j, (cx, cy) in enumerate(chips):
            for a in range(na):
                for p, (src, dst) in enumerate(pairs(a, mychip, 2 * cx + cy)):
                    pltpu.make_async_remote_copy(src_ref=src, dst_ref=dst, send_sem=send_sems.at[a, j, p],
                                                 recv_sem=recv_sems.at[a, j, p], device_id=(cx, cy, c),
                                                 device_id_type=MESH).wait_recv()
        for cp in cps:
            cp.wait_send()
        for cp in mine:
            cp.wait()

    return pl.pallas_call(
        body, name=name, out_shape=out_shape, in_specs=_hbm_specs(na), out_specs=_hbm_specs(na),
        scratch_shapes=[pltpu.SemaphoreType.DMA((na, 3, npc)), pltpu.SemaphoreType.DMA((na, 3, npc)),
                        pltpu.SemaphoreType.DMA((na, npc))],
    )(*arrays)


def _add2(a, b, out_dtype, name):
    R, C = a.shape
    tr = _pick_rows(R, C * 4 * 3 * 2)

    def body(a_ref, b_ref, o_ref):
        o_ref[...] = (a_ref[...].astype(f32) + b_ref[...].astype(f32)).astype(o_ref.dtype)

    spec = pl.BlockSpec((tr, C), lambda i: (i, 0))
    return pl.pallas_call(body, name=name, grid=(R // tr,), in_specs=[spec, spec], out_specs=spec,
                          out_shape=jax.ShapeDtypeStruct((R, C), out_dtype), compiler_params=_cparams(("parallel",)))(a, b)


def _reduce_scatter(cfg, g_in_t, grads):
    c = lax.axis_index("c")
    hd = cfg.D // 2
    shp = [g.shape[2:] for g in grads]
    keep = [lax.dynamic_slice_in_dim(g_in_t, c * hd, hd, axis=1)]
    give = [lax.dynamic_slice_in_dim(g_in_t, (1 - c) * hd, hd, axis=1)]
    keep += [lax.dynamic_index_in_dim(g, c, axis=1, keepdims=False).reshape(4 * hr, nc) for g, (hr, nc) in zip(grads, shp)]
    give += [lax.dynamic_index_in_dim(g, 1 - c, axis=1, keepdims=False).reshape(4 * hr, nc) for g, (hr, nc) in zip(grads, shp)]
    got = _send_sibling(give, "rs_pair")
    part = [_add2(k, g, bf16, "rs_add_pair") for k, g in zip(keep, got)]
    part = part[:1] + [p.reshape(4, hr, nc) for p, (hr, nc) in zip(part[1:], shp)]
    slots = _scatter_chips(cfg, part, "rs_chips")
    mine = [_sum_blocks(s, f32, "rs_add_chips") for s in slots]
    return mine, _send_sibling(mine, "rs_halves")


def _big_weights(cfg):
    return (("mla_w_uq", cfg.QL, cfg.QW, 1), ("mla_w_ukv", cfg.KL, cfg.KVW, 1),
            ("w_branch", cfg.RW + cfg.LW + cfg.MW, cfg.D, 0), ("w_out", cfg.D, cfg.D, 0))


def _half_shapes(cfg):
    out = []
    for _, r, c, ax in _big_weights(cfg):
        out.append((r // 2, c // 4) if ax == 1 else (r // 8, c))
    return out


def _my_halves(cfg, W, l, c):
    hd = cfg.D // 2
    out = [lax.dynamic_slice_in_dim(W["w_in"][l].T, c * hd, hd, axis=1).astype(bf16)]
    for (name, *_), (hr, nc) in zip(_big_weights(cfg), _half_shapes(cfg)):
        out.append(lax.dynamic_slice_in_dim(W[name][l], c * hr, hr, axis=0).astype(bf16))
    return out


def _uq_split(cfg, w):
    hw = HEAD + ROPE
    return jnp.concatenate([w[:, h * hw:h * hw + HEAD] for h in range(cfg.MH)]
                           + [w[:, h * hw + HEAD:(h + 1) * hw] for h in range(cfg.MH)], axis=1)


def _uq_join(cfg, g):
    n = cfg.MH * HEAD
    parts = []
    for h in range(cfg.MH):
        parts += [g[:, h * HEAD:(h + 1) * HEAD], g[:, n + h * ROPE:n + (h + 1) * ROPE]]
    return jnp.concatenate(parts, axis=1)


def _col_blocks(g):
    nc = g.shape[1] // 4
    return jnp.stack([g[:, q * nc:(q + 1) * nc] for q in range(4)])


def _row_pack(parts):
    rows = []
    for p in parts:
        r = p.reshape(-1, LANES)
        pad = -r.shape[0] % SUBLANES
        rows.append(jnp.concatenate([r, jnp.zeros((pad, LANES), r.dtype)], axis=0) if pad else r)
    return jnp.concatenate(rows, axis=0)


def _row_unpack(packed, like):
    out, off = [], 0
    for p in like:
        n = p.size // LANES
        out.append(packed[off:off + n].reshape(p.shape))
        off += -(-n // SUBLANES) * SUBLANES
    return out


def _prep_layer(cfg, full, small):
    w_in_t, w_uq, w_ukv, w_branch, w_out = full
    RW, LW = cfg.RW, cfg.LW
    P = dict(small)
    P["w_in_t"] = w_in_t
    P["w_uq"] = _uq_split(cfg, jnp.concatenate(list(w_uq.reshape(4, cfg.QL, -1)), axis=1))
    P["w_ukv"] = jnp.concatenate(list(w_ukv.reshape(4, cfg.KL, -1)), axis=1)
    P["wb"] = (w_branch[:RW], w_branch[RW:RW + LW], w_branch[RW + LW:])
    P["w_out"] = w_out
    return P


def _layer_fwd(cfg, x, mod, P, T):
    h = _prenorm_fwd(cfg, x, mod, P["norm_pre"])
    proj = _mm(h, P["w_in_t"], f32, "mm_proj", mode="nt")
    y_ret = _ret_fwd(cfg, proj, P["ret_gn"], T["cos_r"], T["sin_r"], T["ret_consts"])
    a, b = _lru_gates(cfg, proj, P["lru_conv_w"], P["lru_conv_b"], P["lru_wa"], P["lru_ba"], P["lru_wx"], P["lru_bx"],
                      P["lru_lambda"])
    hl, y_lru = _lru_scan_fwd(cfg, proj, a, b)
    qn, kn = _mla_norm(cfg, proj, P["mla_q_norm"], P["mla_kv_norm"])
    q = _mm(qn, P["w_uq"], f32, "mm_uq")
    kv = _mm(kn, P["w_ukv"], f32, "mm_ukv")
    q3, k3, v3 = _mla_pack(cfg, proj, q, kv, T["cos_q"], T["sin_q"], T["cos_k"], T["sin_k"])
    o, y_mla = _mla_attn_fwd(cfg, proj, q3, k3, v3)
    ys = (y_ret, y_lru, y_mla)
    us = tuple(_mm(yb, wb, f32, "mm_branch") for yb, wb in zip(ys, P["wb"]))
    merged = _merge_fwd(cfg, proj, *us)
    y = _mm(merged, P["w_out"], f32, "mm_out")
    out = _postnorm_fwd(cfg, x, y, mod, P["norm_post"])
    R = dict(x=x, h=h, proj=proj, ys=ys, a=a, hl=hl, qn=qn, kn=kn, q3=q3, k3=k3, v3=v3, o=o, us=us, merged=merged, y=y)
    return out, R


def _layer_bwd(cfg, dout, R, mod, P, T):
    proj = R["proj"]
    dy, s_post = _postnorm_bwd(cfg, dout, R["y"], mod, P["norm_post"])
    dmerged = _mm(dy, P["w_out"], f32, "mm_dmerged", mode="nt")
    g_out = _mm(R["merged"], dy, bf16, "mm_gw_out", mode="tn")
    du0, du1, du2, dlog = _merge_bwd(cfg, proj, dmerged, *R["us"])
    dus = (du0, du1, du2)
    dys = tuple(_mm(du, wb, f32, "mm_dbranch", mode="nt") for du, wb in zip(dus, P["wb"]))
    g_branch = jnp.concatenate([_mm(yb, du, bf16, "mm_gw_branch", mode="tn") for yb, du in zip(R["ys"], dus)], axis=0)
    drq, drk, drv, drg, dgn = _ret_bwd(cfg, proj, dys[0], P["ret_gn"], T["cos_r"], T["sin_r"], T["ret_consts"])
    da, db, dlg = _lru_scan_bwd(cfg, proj, R["a"], R["hl"], dys[1])
    dxc, dwa, dwx, s_lru = _lru_gates_bwd(cfg, proj, da, db, P["lru_conv_w"], P["lru_conv_b"], P["lru_wa"], P["lru_ba"],
                                          P["lru_wx"], P["lru_bx"], P["lru_lambda"])
    dlx, s_conv = _lru_conv_bwd(cfg, proj, dxc, P["lru_conv_w"])
    dq3, dk3, dv3, dmg = _mla_attn_bwd(cfg, proj, R["q3"], R["k3"], R["v3"], R["o"], dys[2])
    dq, dkv, dmkr = _mla_unpack_bwd(cfg, dq3, dk3, dv3, T["cos_q"], T["sin_q"], T["cos_k"], T["sin_k"])
    dqn = _mm(dq, P["w_uq"], f32, "mm_dqn", mode="nt")
    dkn = _mm(dkv, P["w_ukv"], f32, "mm_dkn", mode="nt")
    g_uq = _uq_join(cfg, _mm(R["qn"], dq, bf16, "mm_gw_uq", mode="tn"))
    g_ukv = _mm(R["kn"], dkv, bf16, "mm_gw_ukv", mode="tn")
    dmq, dmkv, s_q, s_k = _mla_norm_bwd(cfg, proj, dqn, dkn, P["mla_q_norm"], P["mla_kv_norm"])
    dproj = jnp.concatenate([drq, drk, drv, drg, dlx, dlg, dmq, dmkv, dmg, dlog, dmkr,
                             jnp.zeros((cfg.S, cfg.NP - cfg.o_mkr - HEAD), bf16)], axis=1)
    dh = _mm(dproj, P["w_in_t"], f32, "mm_dh")
    g_in_t = _mm(dproj, R["h"], bf16, "mm_gw_in", mode="tn", tm=512)
    dx, s_pre = _prenorm_bwd(cfg, R["x"], dh, dout, mod, P["norm_pre"])
    big = [_col_blocks(g_uq), _col_blocks(g_ukv), g_branch, g_out]
    big = (g_in_t, [g.reshape(4, 2, hr, nc) for g, (hr, nc) in zip(big, _half_shapes(cfg))])
    small = dict(norm_pre=s_pre[2:3], norm_post=s_post[1:2], ret_gn=dgn, lru_conv_w=s_conv[0:CONV], lru_conv_b=s_conv[CONV:CONV + 1],
                 lru_wa=dwa, lru_ba=s_lru[0:1], lru_wx=dwx, lru_bx=s_lru[1:2], lru_lambda=s_lru[2:3],
                 mla_q_norm=s_q[0:1], mla_kv_norm=s_k[0:1])
    dmod = jnp.concatenate([s_pre[0:1], s_pre[1:2], s_post[0:1]], axis=1)
    return dx, big, small, dmod


_SMALL = ("norm_pre", "norm_post", "ret_gn", "lru_conv_w", "lru_conv_b", "lru_wa", "lru_ba", "lru_wx", "lru_bx", "lru_lambda",
          "mla_q_norm", "mla_kv_norm")
_WEIGHTS = ("ada_w", "ada_b", "norm_pre", "norm_post", "w_in", "ret_gn", "lru_conv_w", "lru_conv_b", "lru_wa", "lru_ba", "lru_wx",
            "lru_bx", "lru_lambda", "mla_q_norm", "mla_w_uq", "mla_kv_norm", "mla_w_ukv", "w_branch", "w_out")


def _step(cfg, x, c, positions, W, target, M1, V1):
    L, D = cfg.L, cfg.D
    xi, yi, ci = lax.axis_index("x"), lax.axis_index("y"), lax.axis_index("c")
    chip = 2 * xi + yi
    me = 2 * chip + ci

    c8 = jnp.concatenate([c, jnp.zeros((SUBLANES - 1, D), f32)], axis=0)
    c_all = _allgather8([c8], "gather_c")[0].reshape(N_DEV, SUBLANES, D)[:, 0]
    mod_sh, c_act = _ada_fwd(cfg, c_all, W["ada_w"])
    n_sh = mod_sh.shape[2]
    mod_half = lax.dynamic_slice_in_dim(mod_sh, ci * (n_sh // 2), n_sh // 2, axis=2).reshape(L * N_DEV, n_sh // 2)
    mod_all = _allgather8([mod_half], "gather_mod")[0].reshape(N_DEV, L, N_DEV, n_sh // 2)
    mod_all = mod_all.transpose(1, 2, 0, 3).reshape(L, N_DEV, 3 * D)
    mods = lax.dynamic_index_in_dim(mod_all, me, axis=1, keepdims=False) + W["ada_b"]

    (cos_r, sin_r), (cos_m, sin_m) = _rope_tables(cfg, positions)
    T = dict(cos_r=cos_r, sin_r=sin_r, cos_q=jnp.tile(cos_m, (1, cfg.MH)), sin_q=jnp.tile(sin_m, (1, cfg.MH)),
             cos_k=jnp.tile(cos_m, (1, 2)), sin_k=jnp.tile(sin_m, (1, 2)), ret_consts=_ret_consts(cfg))

    Ps, Rs = [], []
    act = x[0]
    zeros = jnp.zeros((cfg.NP - cfg.o_mkr - ROPE, D), bf16)
    for l in range(L):
        gathered = _allgather8(_my_halves(cfg, W, l, ci), "gather_w", cfg, zeros)
        small = {k: (W[k][l] if W[k][l].ndim > 1 else W[k][l][None, :]) for k in _SMALL if k != "lru_conv_w"}
        P = _prep_layer(cfg, gathered, small)
        P["lru_conv_w"] = None
        Ps.append(P)
    cw_all = _allgather8([_pad_rows(W["lru_conv_w"].reshape(L * CONV, -1))], "gather_conv")[0]
    cw_rows = cw_all.shape[0] // N_DEV
    cw_all = cw_all.reshape(4, 2, cw_rows, -1)[:, 0, :L * CONV].transpose(1, 0, 2).reshape(L, CONV, cfg.LW)
    for l in range(L):
        Ps[l]["lru_conv_w"] = cw_all[l]
    for l in range(L):
        act, R = _layer_fwd(cfg, act, mods[l:l + 1], Ps[l], T)
        Rs.append(R)

    dact, lsum = _loss_head(cfg, act, target[0])
    loss = lax.psum(lsum[0, 0], ("x", "y", "c"))

    big_g = [None] * L
    small_g = [None] * L
    dmods = [None] * L
    for l in range(L - 1, -1, -1):
        dact, grads, small_g[l], dmods[l] = _layer_bwd(cfg, dact, Rs[l], mods[l:l + 1], Ps[l], T)
        big_g[l] = _reduce_scatter(cfg, *grads)

    dmod = jnp.concatenate(dmods, axis=0)
    parts = [dmod] + [small_g[l][k] for l in range(L) for k in _SMALL]
    packed = _row_pack(parts)
    allf = _allgather8([packed], "gather_small")[0].reshape(N_DEV, packed.shape[0], LANES)
    summed = _row_unpack(_sum_blocks(allf, f32, "sum_small"), parts)
    gsm = {k: jnp.stack([summed[1 + l * len(_SMALL) + i].reshape(W[k].shape[1:] if k != "lru_conv_w" else (CONV, cfg.LW))
                         for l in range(L)]) for i, k in enumerate(_SMALL)}
    ncw = cfg.LW // 4
    gsm["lru_conv_w"] = lax.dynamic_slice_in_dim(gsm["lru_conv_w"], chip * ncw, ncw, axis=2)
    gsm["ada_b"] = summed[0]
    dmod_all = allf[:, :dmod.size // LANES].reshape(N_DEV, L, 3 * D)
    dmod_sh = lax.dynamic_slice_in_dim(dmod_all, chip * n_sh, n_sh, axis=2).transpose(1, 0, 2)
    G = dict(gsm)
    G["ada_w"] = _ada_bwd(cfg, c_act.T, dmod_sh)
    delta, new_m, new_v = {}, {}, {}
    core = ci.astype(jnp.int32).reshape(1)
    for i, (name, *_) in enumerate(_big_weights(cfg)):
        G[name], delta[name], new_m[name], new_v[name] = _adamw_big(
            W[name], M1[name], V1[name], [big_g[l][0][i + 1] for l in range(L)], [big_g[l][1][i + 1] for l in range(L)], core,
            "adamw_" + name)
    tr_ = lambda a: a.transpose(0, 2, 1)
    outs = _adamw_big(tr_(W["w_in"]), tr_(M1["w_in"]), tr_(V1["w_in"]), [big_g[l][0][0] for l in range(L)],
                      [big_g[l][1][0] for l in range(L)], core, "adamw_w_in", half_cols=True)
    G["w_in"], delta["w_in"], new_m["w_in"], new_v["w_in"] = [tr_(o) for o in outs]
    bigs = ("ada_w", "w_in") + tuple(name for name, *_ in _big_weights(cfg))
    shp = W["ada_w"].shape
    two = lambda a: a.reshape(-1, shp[-1])
    d, m_, v_ = _adamw(two(W["ada_w"]), two(G["ada_w"]), two(M1["ada_w"]), two(V1["ada_w"]), "adamw_ada_w")
    delta["ada_w"], new_m["ada_w"], new_v["ada_w"] = d.reshape(shp), m_.reshape(shp), v_.reshape(shp)
    smalls = [k for k in _WEIGHTS if k not in bigs]
    packs = [_row_pack([src[k] for k in smalls]) for src in (W, G, M1, V1)]
    outs = _adamw(*packs, "adamw_small")
    for dst, o in zip((delta, new_m, new_v), outs):
        for k, val in zip(smalls, _row_unpack(o, [W[k] for k in smalls])):
            dst[k] = val

    grad_x = dact[None]
    return (loss, grad_x, *[G[k] for k in _WEIGHTS], *[delta[k] for k in _WEIGHTS], *[new_m[k] for k in _WEIGHTS],
            *[new_v[k] for k in _WEIGHTS])


def _pad_rows(a):
    pad = -a.shape[0] % SUBLANES
    return jnp.concatenate([a, jnp.zeros((pad, a.shape[1]), a.dtype)], axis=0) if pad else a


def kernel(x, c, positions, ada_w, ada_b, norm_pre, norm_post, w_in, ret_gn, lru_conv_w, lru_conv_b, lru_wa, lru_ba, lru_wx, lru_bx, lru_lambda, mla_q_norm, mla_w_uq, mla_kv_norm, mla_w_ukv, w_branch, w_out, loss_target, m_ada_w, m_ada_b, m_norm_pre, m_norm_post, m_w_in, m_ret_gn, m_lru_conv_w, m_lru_conv_b, m_lru_wa, m_lru_ba, m_lru_wx, m_lru_bx, m_lru_lambda, m_mla_q_norm, m_mla_w_uq, m_mla_kv_norm, m_mla_w_ukv, m_w_branch, m_w_out, v_ada_w, v_ada_b, v_norm_pre, v_norm_post, v_w_in, v_ret_gn, v_lru_conv_w, v_lru_conv_b, v_lru_wa, v_lru_ba, v_lru_wx, v_lru_bx, v_lru_lambda, v_mla_q_norm, v_mla_w_uq, v_mla_kv_norm, v_mla_w_ukv, v_w_branch, v_w_out):
    W = dict(ada_w=ada_w, ada_b=ada_b, norm_pre=norm_pre, norm_post=norm_post, w_in=w_in, ret_gn=ret_gn, lru_conv_w=lru_conv_w,
             lru_conv_b=lru_conv_b, lru_wa=lru_wa, lru_ba=lru_ba, lru_wx=lru_wx, lru_bx=lru_bx, lru_lambda=lru_lambda,
             mla_q_norm=mla_q_norm, mla_w_uq=mla_w_uq, mla_kv_norm=mla_kv_norm, mla_w_ukv=mla_w_ukv, w_branch=w_branch, w_out=w_out)
    M1 = dict(ada_w=m_ada_w, ada_b=m_ada_b, norm_pre=m_norm_pre, norm_post=m_norm_post, w_in=m_w_in, ret_gn=m_ret_gn,
              lru_conv_w=m_lru_conv_w, lru_conv_b=m_lru_conv_b, lru_wa=m_lru_wa, lru_ba=m_lru_ba, lru_wx=m_lru_wx, lru_bx=m_lru_bx,
              lru_lambda=m_lru_lambda, mla_q_norm=m_mla_q_norm, mla_w_uq=m_mla_w_uq, mla_kv_norm=m_mla_kv_norm,
              mla_w_ukv=m_mla_w_ukv, w_branch=m_w_branch, w_out=m_w_out)
    V1 = dict(ada_w=v_ada_w, ada_b=v_ada_b, norm_pre=v_norm_pre, norm_post=v_norm_post, w_in=v_w_in, ret_gn=v_ret_gn,
              lru_conv_w=v_lru_conv_w, lru_conv_b=v_lru_conv_b, lru_wa=v_lru_wa, lru_ba=v_lru_ba, lru_wx=v_lru_wx, lru_bx=v_lru_bx,
              lru_lambda=v_lru_lambda, mla_q_norm=v_mla_q_norm, mla_w_uq=v_mla_w_uq, mla_kv_norm=v_mla_kv_norm,
              mla_w_ukv=v_mla_w_ukv, w_branch=v_w_branch, w_out=v_w_out)
    return _step(_CFG, x, c, positions, W, loss_target, M1, V1)
```

```python
import functools
import math
from typing import NamedTuple

import numpy as np
import jax
import jax.numpy as jnp
from jax import lax
from jax.experimental import pallas as pl
from jax.experimental.pallas import tpu as pltpu

f32 = jnp.float32
bf16 = jnp.bfloat16

NORM_EPS = 1e-6
ROPE_BASE = 10000.0
CHUNK = 64
HEAD = 128
ROPE = 64
CONV = 4
LRU_C = 8.0
ADAM_LR, ADAM_B1, ADAM_B2, ADAM_EPS, ADAM_WD, ADAM_STEP = 0.001, 0.9, 0.999, 1e-08, 0.01, 10

LANES = 128
SUBLANES = 8
VMEM_LIMIT = 56 * 1024 * 1024
MM_BUDGET = 40 * 1024 * 1024
N_DEV = 8
MESH = pl.DeviceIdType.MESH


class Cfg(NamedTuple):
    D: int = 2048
    S: int = 2048
    L: int = 4
    H: int = 8
    NB: int = 8
    MH: int = 8
    QL: int = 512
    KL: int = 512
    TR: int = 256
    TQ: int = 256

    @property
    def RW(self): return self.H * HEAD
    @property
    def LW(self): return self.NB * HEAD
    @property
    def MW(self): return self.MH * HEAD
    @property
    def o_rk(self): return self.RW
    @property
    def o_rv(self): return 2 * self.RW
    @property
    def o_rg(self): return 3 * self.RW
    @property
    def o_lx(self): return 4 * self.RW
    @property
    def o_lg(self): return 4 * self.RW + self.LW
    @property
    def o_mq(self): return 4 * self.RW + 2 * self.LW
    @property
    def o_mkv(self): return self.o_mq + self.QL
    @property
    def o_mg(self): return self.o_mkv + self.KL
    @property
    def o_merge(self): return self.o_mg + self.MW
    @property
    def o_mkr(self): return self.o_merge + 3 * self.D
    @property
    def NP(self): return -(-(self.o_mkr + ROPE) // 512) * 512
    @property
    def IN_WIDTH(self): return self.o_mkr + ROPE
    @property
    def QW(self): return self.MH * (HEAD + ROPE)
    @property
    def KVW(self): return self.MH * 2 * HEAD


_CFG = Cfg()


def _cparams(sem=None):
    return pltpu.CompilerParams(dimension_semantics=sem, vmem_limit_bytes=VMEM_LIMIT)


def _sigmoid(x):
    return jax.nn.sigmoid(x)


def _silu(x):
    return x * _sigmoid(x)


def _dsilu(x):
    s = _sigmoid(x)
    return s * (1.0 + x * (1.0 - s))


def _slab(rows, width, off):
    assert off % width == 0
    return pl.BlockSpec((rows, width), lambda i, _c=off // width: (i, _c))


def _row(width):
    return pl.BlockSpec((1, width), lambda i: (0, 0))


def _mm(a, b, out_dtype=f32, name="mm", mode="nn", tm=None):
    (M, K) = a.shape if mode != "tn" else a.shape[::-1]
    (K2, N) = b.shape if mode != "nt" else b.shape[::-1]
    assert K == K2
    tn = N if N <= 2048 else 512
    tk = K if K <= 2048 else 512
    assert N % tn == 0 and K % tk == 0
    osz = jnp.dtype(out_dtype).itemsize
    if tm is None:
        tm = M
        while 2 * tm * tk * 2 + 2 * tk * tn * 2 + 2 * tm * tn * osz + tm * tn * 4 > MM_BUDGET and tm % 16 == 0:
            tm //= 2
    assert M % tm == 0
    nk = K // tk
    dims = {"nn": (((1,), (0,)), ((), ())), "nt": (((1,), (1,)), ((), ())), "tn": (((0,), (0,)), ((), ()))}[mode]

    def dot(a_ref, b_ref):
        return lax.dot_general(a_ref[...].astype(bf16), b_ref[...].astype(bf16), dims, preferred_element_type=f32)

    if nk == 1:
        def body(a_ref, b_ref, o_ref):
            o_ref[...] = dot(a_ref, b_ref).astype(o_ref.dtype)
        scratch = []
    else:
        def body(a_ref, b_ref, o_ref, acc_ref):
            k = pl.program_id(2)

            @pl.when(k == 0)
            def _():
                acc_ref[...] = jnp.zeros_like(acc_ref)

            acc_ref[...] += dot(a_ref, b_ref)

            @pl.when(k == nk - 1)
            def _():
                o_ref[...] = acc_ref[...].astype(o_ref.dtype)
        scratch = [pltpu.VMEM((tm, tn), f32)]

    a_spec = pl.BlockSpec((tk, tm), lambda i, j, k: (k, i)) if mode == "tn" else pl.BlockSpec((tm, tk), lambda i, j, k: (i, k))
    b_spec = pl.BlockSpec((tn, tk), lambda i, j, k: (j, k)) if mode == "nt" else pl.BlockSpec((tk, tn), lambda i, j, k: (k, j))
    return pl.pallas_call(
        body, name=name,
        grid=(M // tm, N // tn, nk),
        in_specs=[a_spec, b_spec],
        out_specs=pl.BlockSpec((tm, tn), lambda i, j, k: (i, j)),
        out_shape=jax.ShapeDtypeStruct((M, N), out_dtype),
        scratch_shapes=scratch,
        compiler_params=_cparams(("parallel", "parallel", "arbitrary")),
    )(a, b)


def _ada_fwd(cfg, c_all, ada_w):
    L, D, n = ada_w.shape
    tn = n // 2 if (n // 2) % LANES == 0 else n

    def body(c_ref, w_ref, o_ref, ca_ref):
        ca = _silu(c_ref[...])
        ca_ref[...] = ca
        o_ref[0] = jnp.dot(ca.astype(bf16), w_ref[0].astype(bf16), preferred_element_type=f32)

    return pl.pallas_call(
        body, name="ada_fwd", grid=(L, n // tn),
        in_specs=[pl.BlockSpec((N_DEV, D), lambda l, j: (0, 0)), pl.BlockSpec((1, D, tn), lambda l, j: (l, 0, j))],
        out_specs=(pl.BlockSpec((1, N_DEV, tn), lambda l, j: (l, 0, j)), pl.BlockSpec((N_DEV, D), lambda l, j: (0, 0))),
        out_shape=(jax.ShapeDtypeStruct((L, N_DEV, n), f32), jax.ShapeDtypeStruct((N_DEV, D), f32)),
        compiler_params=_cparams(("arbitrary", "arbitrary")),
    )(c_all, ada_w)


def _ada_bwd(cfg, c_act_t, dmod):
    L, _, n = dmod.shape
    D = c_act_t.shape[0]
    tn = n // 2 if (n // 2) % LANES == 0 else n

    def body(c_ref, d_ref, o_ref):
        o_ref[0] = jnp.dot(c_ref[...].astype(bf16), d_ref[0].astype(bf16), preferred_element_type=f32)

    return pl.pallas_call(
        body, name="ada_bwd", grid=(L, n // tn),
        in_specs=[pl.BlockSpec((D, N_DEV), lambda l, j: (0, 0)), pl.BlockSpec((1, N_DEV, tn), lambda l, j: (l, 0, j))],
        out_specs=pl.BlockSpec((1, D, tn), lambda l, j: (l, 0, j)),
        out_shape=jax.ShapeDtypeStruct((L, D, n), f32),
        compiler_params=_cparams(("parallel", "parallel")),
    )(c_act_t, dmod)


def _prenorm_fwd(cfg, x, mod, gain):
    S, D, TR = cfg.S, cfg.D, cfg.TR

    def body(x_ref, mod_ref, g_ref, h_ref):
        x = x_ref[...]
        r = lax.rsqrt(jnp.mean(x * x, axis=-1, keepdims=True) + NORM_EPS)
        shift, scale = mod_ref[:, 0:D], mod_ref[:, D:2 * D]
        h_ref[...] = ((x * r) * g_ref[...] * (1.0 + scale) + shift).astype(bf16)

    return pl.pallas_call(
        body, name="prenorm_fwd", grid=(S // TR,),
        in_specs=[_slab(TR, D, 0), _row(3 * D), _row(D)],
        out_specs=_slab(TR, D, 0), out_shape=jax.ShapeDtypeStruct((S, D), bf16),
        compiler_params=_cparams(("parallel",)),
    )(x, mod, gain)


def _prenorm_bwd(cfg, x, dh, dres, mod, gain):
    S, D, TR = cfg.S, cfg.D, cfg.TR

    def body(x_ref, dh_ref, dres_ref, mod_ref, g_ref, dx_ref, sum_ref):
        i = pl.program_id(0)
        x, dh, g = x_ref[...], dh_ref[...], g_ref[...]
        scale = mod_ref[:, D:2 * D]
        r = lax.rsqrt(jnp.mean(x * x, axis=-1, keepdims=True) + NORM_EPS)
        xn = x * r
        t = dh * xn
        dxn = dh * (g * (1.0 + scale))
        dx_ref[...] = r * (dxn - xn * jnp.mean(dxn * xn, axis=-1, keepdims=True)) + dres_ref[...]
        part = jnp.concatenate([jnp.sum(dh, axis=0, keepdims=True), jnp.sum(t * g, axis=0, keepdims=True),
                                jnp.sum(t * (1.0 + scale), axis=0, keepdims=True), jnp.zeros((SUBLANES - 3, D), f32)], axis=0)

        @pl.when(i == 0)
        def _():
            sum_ref[...] = part

        @pl.when(i > 0)
        def _():
            sum_ref[...] += part

    return pl.pallas_call(
        body, name="prenorm_bwd", grid=(S // TR,),
        in_specs=[_slab(TR, D, 0), _slab(TR, D, 0), _slab(TR, D, 0), _row(3 * D), _row(D)],
        out_specs=(_slab(TR, D, 0), pl.BlockSpec((SUBLANES, D), lambda i: (0, 0))),
        out_shape=(jax.ShapeDtypeStruct((S, D), f32), jax.ShapeDtypeStruct((SUBLANES, D), f32)),
        compiler_params=_cparams(("arbitrary",)),
    )(x, dh, dres, mod, gain)


def _postnorm_fwd(cfg, x, y, mod, gain):
    S, D, TR = cfg.S, cfg.D, cfg.TR

    def body(x_ref, y_ref, mod_ref, g_ref, o_ref):
        y = y_ref[...]
        r = lax.rsqrt(jnp.mean(y * y, axis=-1, keepdims=True) + NORM_EPS)
        rg = mod_ref[:, 2 * D:3 * D]
        o_ref[...] = x_ref[...] + (1.0 + rg) * ((y * r) * g_ref[...])

    return pl.pallas_call(
        body, name="postnorm_fwd", grid=(S // TR,),
        in_specs=[_slab(TR, D, 0), _slab(TR, D, 0), _row(3 * D), _row(D)],
        out_specs=_slab(TR, D, 0), out_shape=jax.ShapeDtypeStruct((S, D), f32),
        compiler_params=_cparams(("parallel",)),
    )(x, y, mod, gain)


def _postnorm_bwd(cfg, dout, y, mod, gain):
    S, D, TR = cfg.S, cfg.D, cfg.TR

    def body(do_ref, y_ref, mod_ref, g_ref, dy_ref, sum_ref):
        i = pl.program_id(0)
        do, y, g = do_ref[...], y_ref[...], g_ref[...]
        rg = mod_ref[:, 2 * D:3 * D]
        r = lax.rsqrt(jnp.mean(y * y, axis=-1, keepdims=True) + NORM_EPS)
        yn = y * r
        t = do * yn
        dyn = do * ((1.0 + rg) * g)
        dy_ref[...] = (r * (dyn - yn * jnp.mean(dyn * yn, axis=-1, keepdims=True))).astype(bf16)
        part = jnp.concatenate([jnp.sum(t * g, axis=0, keepdims=True), jnp.sum(t * (1.0 + rg), axis=0, keepdims=True),
                                jnp.zeros((SUBLANES - 2, D), f32)], axis=0)

        @pl.when(i == 0)
        def _():
            sum_ref[...] = part

        @pl.when(i > 0)
        def _():
            sum_ref[...] += part

    return pl.pallas_call(
        body, name="postnorm_bwd", grid=(S // TR,),
        in_specs=[_slab(TR, D, 0), _slab(TR, D, 0), _row(3 * D), _row(D)],
        out_specs=(_slab(TR, D, 0), pl.BlockSpec((SUBLANES, D), lambda i: (0, 0))),
        out_shape=(jax.ShapeDtypeStruct((S, D), bf16), jax.ShapeDtypeStruct((SUBLANES, D), f32)),
        compiler_params=_cparams(("arbitrary",)),
    )(dout, y, mod, gain)


def _loss_head(cfg, y, target):
    S, D, TR = cfg.S, cfg.D, cfg.TR

    def body(y_ref, t_ref, d_ref, l_ref):
        i = pl.program_id(0)
        err = y_ref[...] - t_ref[...]
        d_ref[...] = err / D
        part = jnp.zeros((SUBLANES, LANES), f32) + 0.5 * jnp.sum(jnp.mean(err * err, axis=-1, keepdims=True))

        @pl.when(i == 0)
        def _():
            l_ref[...] = part

        @pl.when(i > 0)
        def _():
            l_ref[...] += part

    return pl.pallas_call(
        body, name="loss_head", grid=(S // TR,),
        in_specs=[_slab(TR, D, 0), _slab(TR, D, 0)],
        out_specs=(_slab(TR, D, 0), pl.BlockSpec((SUBLANES, LANES), lambda i: (0, 0))),
        out_shape=(jax.ShapeDtypeStruct((S, D), f32), jax.ShapeDtypeStruct((SUBLANES, LANES), f32)),
        compiler_params=_cparams(("arbitrary",)),
    )(y, target)


def _merge_fwd(cfg, proj, u0, u1, u2):
    S, D, TR = cfg.S, cfg.D, cfg.TR

    def body(l0, l1, l2, u0_ref, u1_ref, u2_ref, o_ref):
        o_ref[...] = (_sigmoid(l0[...]) * u0_ref[...] + _sigmoid(l1[...]) * u1_ref[...]
                      + _sigmoid(l2[...]) * u2_ref[...]).astype(bf16)

    return pl.pallas_call(
        body, name="merge_fwd", grid=(S // TR,),
        in_specs=[_slab(TR, D, cfg.o_merge + b * D) for b in range(3)] + [_slab(TR, D, 0)] * 3,
        out_specs=_slab(TR, D, 0), out_shape=jax.ShapeDtypeStruct((S, D), bf16),
        compiler_params=_cparams(("parallel",)),
    )(proj, proj, proj, u0, u1, u2)


def _merge_bwd(cfg, proj, dmerged, u0, u1, u2):
    S, D, TR = cfg.S, cfg.D, cfg.TR

    def body(l0, l1, l2, dm_ref, u0_ref, u1_ref, u2_ref, du0, du1, du2, dl_ref):
        dm = dm_ref[...]
        for b, (l, u, du) in enumerate(((l0, u0_ref, du0), (l1, u1_ref, du1), (l2, u2_ref, du2))):
            g = _sigmoid(l[...])
            du[...] = (dm * g).astype(bf16)
            dl_ref[:, b * D:(b + 1) * D] = (dm * u[...] * (g * (1.0 - g))).astype(bf16)

    return pl.pallas_call(
        body, name="merge_bwd", grid=(S // TR,),
        in_specs=[_slab(TR, D, cfg.o_merge + b * D) for b in range(3)] + [_slab(TR, D, 0)] * 4,
        out_specs=(_slab(TR, D, 0),) * 3 + (_slab(TR, 3 * D, 0),),
        out_shape=(jax.ShapeDtypeStruct((S, D), bf16),) * 3 + (jax.ShapeDtypeStruct((S, 3 * D), bf16),),
        compiler_params=_cparams(("parallel",)),
    )(proj, proj, proj, dmerged, u0, u1, u2)


def _rope128(x, c, s):
    return x * c + pltpu.roll(x, 64, axis=1) * s


def _rope128_t(dy, c, s):
    return dy * c + pltpu.roll(dy * s, 64, axis=1)


def _swap32(x):
    w = x.shape[1]
    lane = lax.broadcasted_iota(jnp.int32, x.shape, 1)
    return jnp.where((lane % 64) < 32, pltpu.roll(x, w - 32, axis=1), pltpu.roll(x, 32, axis=1))


def _rope64(x, c, s):
    return x * c + _swap32(x) * s


def _rope64_t(dy, c, s):
    return dy * c + _swap32(dy * s)


def _rope_tables(cfg, positions):
    pos = positions.astype(f32)[0][:, None]

    def tab(dim):
        inv_freq = ROPE_BASE ** (-jnp.arange(0, dim, 2, dtype=f32) / dim)
        ang = pos * inv_freq
        cos, sin = jnp.cos(ang), jnp.sin(ang)
        return jnp.concatenate([cos, cos], axis=1), jnp.concatenate([-sin, sin], axis=1)

    return tab(HEAD), tab(ROPE)


def _ret_consts(cfg):
    h = np.arange(cfg.H, dtype=np.float64)
    log_gamma = np.log1p(-np.exp2(-5.0 - h)).astype(np.float32)
    idx = np.arange(CHUNK, dtype=np.float32)
    intra = np.exp(log_gamma[:, None, None] * np.abs(idx[:, None] - idx[None, :]))
    kdec = np.exp(log_gamma[:, None] * (CHUNK - 1 - idx)[None, :])
    qdec = np.exp(log_gamma[:, None] * (idx + 1.0)[None, :])
    cdec = np.exp(log_gamma * CHUNK)
    bc = lambda a: jnp.asarray(np.broadcast_to(a[..., None], a.shape + (HEAD,)).astype(np.float32))
    return jnp.asarray(intra.astype(np.float32)), bc(kdec), bc(qdec), bc(cdec[:, None])


def _ret_core(cfg, q_raw, k_raw, v_raw, cos, sin, intra, kdec, qdec, cdec, p_ref):
    S = cfg.S
    NC = S // CHUNK
    q = _rope128(q_raw, cos, sin) * (HEAD ** -0.5)
    k = _rope128(k_raw, cos, sin)
    q3 = q.reshape(NC, CHUNK, HEAD)
    k3 = k.reshape(NC, CHUNK, HEAD)
    qb, kb = q3.astype(bf16), k3.astype(bf16)
    vb = v_raw.reshape(NC, CHUNK, HEAD).astype(bf16)
    sdb = (jnp.einsum('nid,njd->nij', qb, kb, preferred_element_type=f32) * intra[None]).astype(bf16)
    o_intra = jnp.einsum('nij,nje->nie', sdb, vb, preferred_element_type=f32)
    kdb = (k3 * kdec[None]).astype(bf16)
    kv = jnp.einsum('njd,nje->nde', kdb, vb, preferred_element_type=f32)
    p_ref[0] = jnp.zeros((HEAD, HEAD), f32)
    for n in range(1, NC):
        p_ref[n] = p_ref[n - 1] * cdec + kv[n - 1]
    pb = p_ref[...].astype(bf16)
    qdb = (q3 * qdec[None]).astype(bf16)
    o_inter = jnp.einsum('nid,nde->nie', qdb, pb, preferred_element_type=f32)
    o = (o_intra + o_inter).reshape(S, HEAD)
    return o, (qb, kb, vb, sdb, kdb, qdb, pb)


def _ret_specs(cfg):
    S = cfg.S
    hs = lambda off: pl.BlockSpec((S, HEAD), lambda h, _c=off // HEAD: (0, _c + h))
    full = pl.BlockSpec((S, HEAD), lambda h: (0, 0))
    consts = [pl.BlockSpec((None, CHUNK, CHUNK), lambda h: (h, 0, 0)), pl.BlockSpec((None, CHUNK, HEAD), lambda h: (h, 0, 0)),
              pl.BlockSpec((None, CHUNK, HEAD), lambda h: (h, 0, 0)), pl.BlockSpec((None, 1, HEAD), lambda h: (h, 0, 0))]
    gn = pl.BlockSpec((1, HEAD), lambda h: (0, h))
    return hs, full, consts, gn


def _ret_fwd(cfg, proj, gn, cos, sin, consts):
    S, NC = cfg.S, cfg.S // CHUNK
    hs, full, cspecs, gspec = _ret_specs(cfg)

    def body(q_ref, k_ref, v_ref, g_ref, gn_ref, cos_ref, sin_ref, intra, kdec, qdec, cdec, y_ref, p_ref):
        o, _ = _ret_core(cfg, q_ref[...], k_ref[...], v_ref[...], cos_ref[...], sin_ref[...],
                         intra[...], kdec[...], qdec[...], cdec[...], p_ref)
        mean = jnp.mean(o, axis=-1, keepdims=True)
        var = jnp.mean(jnp.square(o - mean), axis=-1, keepdims=True)
        z = ((o - mean) * lax.rsqrt(var + NORM_EPS)) * gn_ref[...]
        y_ref[...] = (z * _silu(g_ref[...])).astype(bf16)

    return pl.pallas_call(
        body, name="ret_fwd", grid=(cfg.H,),
        in_specs=[hs(0), hs(cfg.o_rk), hs(cfg.o_rv), hs(cfg.o_rg), gspec, full, full] + cspecs,
        out_specs=hs(0), out_shape=jax.ShapeDtypeStruct((S, cfg.RW), bf16),
        scratch_shapes=[pltpu.VMEM((NC, HEAD, HEAD), f32)],
        compiler_params=_cparams(("arbitrary",)),
    )(proj, proj, proj, proj, gn, cos, sin, *consts)


def _ret_bwd(cfg, proj, dy, gn, cos, sin, consts):
    S, NC = cfg.S, cfg.S // CHUNK
    hs, full, cspecs, gspec = _ret_specs(cfg)

    def body(q_ref, k_ref, v_ref, g_ref, dy_ref, gn_ref, cos_ref, sin_ref, intra_ref, kdec_ref, qdec_ref, cdec_ref,
             dq_ref, dk_ref, dv_ref, dg_ref, dgn_ref, p_ref, g_scr):
        cos, sin = cos_ref[...], sin_ref[...]
        intra, kdec, qdec, cdec = intra_ref[...], kdec_ref[...], qdec_ref[...], cdec_ref[...]
        o, (qb, kb, vb, sdb, kdb, qdb, pb) = _ret_core(cfg, q_ref[...], k_ref[...], v_ref[...], cos, sin,
                                                     intra, kdec, qdec, cdec, p_ref)
        gate, dy, gnv = g_ref[...], dy_ref[...], gn_ref[...]
        mean = jnp.mean(o, axis=-1, keepdims=True)
        rstd = lax.rsqrt(jnp.mean(jnp.square(o - mean), axis=-1, keepdims=True) + NORM_EPS)
        on = (o - mean) * rstd
        dz = dy * _silu(gate)
        dg_ref[...] = (dy * (on * gnv) * _dsilu(gate)).astype(bf16)
        dgn_ref[...] = jnp.sum(dz * on, axis=0, keepdims=True)
        don = dz * gnv
        do = rstd * (don - jnp.mean(don, axis=-1, keepdims=True) - on * jnp.mean(don * on, axis=-1, keepdims=True))
        dob = do.reshape(NC, CHUNK, HEAD).astype(bf16)
        dsb = (jnp.einsum('nie,nje->nij', dob, vb, preferred_element_type=f32) * intra[None]).astype(bf16)
        dv = jnp.einsum('nij,nie->nje', sdb, dob, preferred_element_type=f32)
        dq = jnp.einsum('nij,njd->nid', dsb, kb, preferred_element_type=f32)
        dk = jnp.einsum('nij,nid->njd', dsb, qb, preferred_element_type=f32)
        dq = dq + jnp.einsum('nie,nde->nid', dob, pb, preferred_element_type=f32) * qdec[None]
        dp = jnp.einsum('nid,nie->nde', qdb, dob, preferred_element_type=f32)
        g_scr[NC - 1] = jnp.zeros((HEAD, HEAD), f32)
        for n in range(NC - 2, -1, -1):
            g_scr[n] = dp[n + 1] + g_scr[n + 1] * cdec
        gb = g_scr[...].astype(bf16)
        dk = dk + jnp.einsum('nje,nde->njd', vb, gb, preferred_element_type=f32) * kdec[None]
        dv = dv + jnp.einsum('njd,nde->nje', kdb, gb, preferred_element_type=f32)
        dq_ref[...] = _rope128_t(dq.reshape(S, HEAD) * (HEAD ** -0.5), cos, sin).astype(bf16)
        dk_ref[...] = _rope128_t(dk.reshape(S, HEAD), cos, sin).astype(bf16)
        dv_ref[...] = dv.reshape(S, HEAD).astype(bf16)

    return pl.pallas_call(
        body, name="ret_bwd", grid=(cfg.H,),
        in_specs=[hs(0), hs(cfg.o_rk), hs(cfg.o_rv), hs(cfg.o_rg), hs(0), gspec, full, full] + cspecs,
        out_specs=(hs(0),) * 4 + (gspec,),
        out_shape=(jax.ShapeDtypeStruct((S, cfg.RW), bf16),) * 4 + (jax.ShapeDtypeStruct((1, cfg.RW), f32),),
        scratch_shapes=[pltpu.VMEM((NC, HEAD, HEAD), f32), pltpu.VMEM((NC, HEAD, HEAD), f32)],
        compiler_params=_cparams(("arbitrary",)),
    )(proj, proj, proj, proj, dy, gn, cos, sin, *consts)


def _expm1(x):
    small = x * (1.0 + x * (0.5 + x * (1.0 / 6.0 + x * (1.0 / 24.0 + x * (1.0 / 120.0)))))
    return jnp.where(jnp.abs(x) < 0.1, small, jnp.exp(x) - 1.0)


def _softplus(z):
    return jnp.maximum(z, 0.0) + jnp.log1p(jnp.exp(-jnp.abs(z)))


def _lru_conv(cfg, x_ref, halo_ref, cw, scr, first):
    TR = cfg.TR
    scr[0:SUBLANES, :] = jnp.where(first, 0.0, halo_ref[...])
    scr[SUBLANES:SUBLANES + TR, :] = x_ref[...]
    xc = scr[pl.ds(SUBLANES - (CONV - 1), TR), :] * cw[0:1, :]
    for j in range(1, CONV):
        xc = xc + scr[pl.ds(SUBLANES - (CONV - 1) + j, TR), :] * cw[j:j + 1, :]
    return xc


def _lru_pre(cfg, xc, wa_ref, wx_ref, ba, bx):
    xb = xc.astype(bf16)
    pa = jnp.concatenate([jnp.dot(xb[:, n * HEAD:(n + 1) * HEAD], wa_ref[n].astype(bf16), preferred_element_type=f32)
                          for n in range(cfg.NB)], axis=1) + ba
    px = jnp.concatenate([jnp.dot(xb[:, n * HEAD:(n + 1) * HEAD], wx_ref[n].astype(bf16), preferred_element_type=f32)
                          for n in range(cfg.NB)], axis=1) + bx
    return pa, px


def _lru_ab(pa, px, xc, lam):
    r, i = _sigmoid(pa), _sigmoid(px)
    log_a = (-LRU_C * r) * _softplus(-lam)
    a = jnp.exp(log_a)
    b = jnp.sqrt(-_expm1(2.0 * log_a)) * (i * xc)
    return a, b


def _lru_halo_specs(cfg, off, W):
    TR, S = cfg.TR, cfg.S
    nb = TR // SUBLANES
    cb = off // W
    main = pl.BlockSpec((TR, W), lambda i: (i, cb))
    prev = pl.BlockSpec((SUBLANES, W), lambda i: (jnp.maximum(i * nb - 1, 0), cb))
    nxt = pl.BlockSpec((SUBLANES, W), lambda i: (jnp.minimum((i + 1) * nb, S // SUBLANES - 1), cb))
    return main, prev, nxt


def _lru_gates(cfg, proj, cw, cb, wa, ba, wx, bx, lam):
    S, W, TR, NB = cfg.S, cfg.LW, cfg.TR, cfg.NB
    assert cfg.o_lx % W == 0
    main, prev, _ = _lru_halo_specs(cfg, cfg.o_lx, W)
    wspec = pl.BlockSpec((NB, HEAD, HEAD), lambda i: (0, 0, 0))

    def body(x_ref, halo_ref, cw_ref, cb_ref, wa_ref, ba_ref, wx_ref, bx_ref, lam_ref, a_ref, b_ref, scr):
        xc = _lru_conv(cfg, x_ref, halo_ref, cw_ref[...], scr, pl.program_id(0) == 0) + cb_ref[...]
        pa, px = _lru_pre(cfg, xc, wa_ref, wx_ref, ba_ref[...], bx_ref[...])
        a, b = _lru_ab(pa, px, xc, lam_ref[...])
        a_ref[...] = a
        b_ref[...] = b

    return pl.pallas_call(
        body, name="lru_gates", grid=(S // TR,),
        in_specs=[main, prev, pl.BlockSpec((CONV, W), lambda i: (0, 0)), _row(W), wspec, _row(W), wspec, _row(W), _row(W)],
        out_specs=(_slab(TR, W, 0),) * 2, out_shape=(jax.ShapeDtypeStruct((S, W), f32),) * 2,
        scratch_shapes=[pltpu.VMEM((TR + SUBLANES, W), f32)],
        compiler_params=_cparams(("parallel",)),
    )(proj, proj, cw, cb, wa, ba, wx, bx, lam)


def _lru_lane_block(cfg):
    return 256 if cfg.LW % 256 == 0 else LANES


def _lru_scan_fwd(cfg, proj, a, b):
    S, W = cfg.S, cfg.LW
    LB = _lru_lane_block(cfg)
    assert cfg.o_lg % LB == 0
    col = lambda off: pl.BlockSpec((S, LB), lambda j, _c=off // LB: (0, _c + j))

    def body(a_ref, b_ref, g_ref, h_ref, y_ref):
        def blk(t, h):
            r0 = pl.multiple_of(t * SUBLANES, SUBLANES)
            at, bt = a_ref[pl.ds(r0, SUBLANES), :], b_ref[pl.ds(r0, SUBLANES), :]
            rows = []
            for j in range(SUBLANES):
                h = at[j:j + 1, :] * h + bt[j:j + 1, :]
                rows.append(h)
            h_ref[pl.ds(r0, SUBLANES), :] = jnp.concatenate(rows, axis=0)
            return h

        lax.fori_loop(0, S // SUBLANES, blk, jnp.zeros((1, LB), f32))
        y_ref[...] = (h_ref[...] * _silu(g_ref[...])).astype(bf16)

    return pl.pallas_call(
        body, name="lru_scan_fwd", grid=(W // LB,),
        in_specs=[col(0), col(0), col(cfg.o_lg)],
        out_specs=(col(0), col(0)),
        out_shape=(jax.ShapeDtypeStruct((S, W), f32), jax.ShapeDtypeStruct((S, W), bf16)),
        compiler_params=_cparams(("parallel",)),
    )(a, b, proj)


def _lru_scan_bwd(cfg, proj, a, h, dy):
    S, W = cfg.S, cfg.LW
    LB = _lru_lane_block(cfg)
    col = lambda off: pl.BlockSpec((S, LB), lambda j, _c=off // LB: (0, _c + j))

    def body(a_ref, h_ref, dy_ref, g_ref, da_ref, db_ref, dg_ref):
        gate, dy = g_ref[...], dy_ref[...]
        dg_ref[...] = (dy * h_ref[...] * _dsilu(gate)).astype(bf16)
        da_ref[...] = dy * _silu(gate)

        def blk(t, carry):
            dh_next, a_next = carry
            r0 = pl.multiple_of((S // SUBLANES - 1 - t) * SUBLANES, SUBLANES)
            at, ct = a_ref[pl.ds(r0, SUBLANES), :], da_ref[pl.ds(r0, SUBLANES), :]
            rows = [None] * SUBLANES
            for j in range(SUBLANES - 1, -1, -1):
                dh_next = ct[j:j + 1, :] + a_next * dh_next
                a_next = at[j:j + 1, :]
                rows[j] = dh_next
            db_ref[pl.ds(r0, SUBLANES), :] = jnp.concatenate(rows, axis=0)
            return dh_next, a_next

        z = jnp.zeros((1, LB), f32)
        lax.fori_loop(0, S // SUBLANES, blk, (z, z))
        row = lax.broadcasted_iota(jnp.int32, (S, LB), 0)
        hprev = jnp.where(row == 0, 0.0, pltpu.roll(h_ref[...], 1, axis=0))
        da_ref[...] = db_ref[...] * hprev

    return pl.pallas_call(
        body, name="lru_scan_bwd", grid=(W // LB,),
        in_specs=[col(0), col(0), col(0), col(cfg.o_lg)],
        out_specs=(col(0),) * 3,
        out_shape=(jax.ShapeDtypeStruct((S, W), f32),) * 2 + (jax.ShapeDtypeStruct((S, W), bf16),),
        compiler_params=_cparams(("parallel",)),
    )(a, h, dy, proj)


def _lru_gates_bwd(cfg, proj, da, db, cw, cb, wa, ba, wx, bx, lam):
    S, W, TR, NB = cfg.S, cfg.LW, cfg.TR, cfg.NB
    main, prev, _ = _lru_halo_specs(cfg, cfg.o_lx, W)
    wspec = pl.BlockSpec((NB, HEAD, HEAD), lambda i: (0, 0, 0))

    def body(x_ref, halo_ref, da_ref, db_ref, cw_ref, cb_ref, wa_ref, ba_ref, wx_ref, bx_ref, lam_ref,
             dxc_ref, dwa_ref, dwx_ref, sum_ref, scr):
        i = pl.program_id(0)
        lam = lam_ref[...]
        xc = _lru_conv(cfg, x_ref, halo_ref, cw_ref[...], scr, i == 0) + cb_ref[...]
        pa, px = _lru_pre(cfg, xc, wa_ref, wx_ref, ba_ref[...], bx_ref[...])
        _, vjp = jax.vjp(_lru_ab, pa, px, xc, lam)
        dpa, dpx, dxc, dlam = vjp((da_ref[...], db_ref[...]))
        xb, dpab, dpxb = xc.astype(bf16), dpa.astype(bf16), dpx.astype(bf16)
        nt = (((1,), (1,)), ((), ()))
        tn = (((0,), (0,)), ((), ()))
        back = []
        dwa, dwx = [], []
        for n in range(NB):
            sl = slice(n * HEAD, (n + 1) * HEAD)
            back.append(lax.dot_general(dpab[:, sl], wa_ref[n].astype(bf16), nt, preferred_element_type=f32)
                        + lax.dot_general(dpxb[:, sl], wx_ref[n].astype(bf16), nt, preferred_element_type=f32))
            dwa.append(lax.dot_general(xb[:, sl], dpab[:, sl], tn, preferred_element_type=f32))
            dwx.append(lax.dot_general(xb[:, sl], dpxb[:, sl], tn, preferred_element_type=f32))
        dxc_ref[...] = dxc + jnp.concatenate(back, axis=1)
        part = jnp.concatenate([jnp.sum(dpa, axis=0, keepdims=True), jnp.sum(dpx, axis=0, keepdims=True), dlam,
                                jnp.zeros((SUBLANES - 3, W), f32)], axis=0)

        @pl.when(i == 0)
        def _():
            sum_ref[...] = part
            for n in range(NB):
                dwa_ref[n] = dwa[n]
                dwx_ref[n] = dwx[n]

        @pl.when(i > 0)
        def _():
            sum_ref[...] += part
            for n in range(NB):
                dwa_ref[n] += dwa[n]
                dwx_ref[n] += dwx[n]

    return pl.pallas_call(
        body, name="lru_gates_bwd", grid=(S // TR,),
        in_specs=[main, prev, _slab(TR, W, 0), _slab(TR, W, 0), pl.BlockSpec((CONV, W), lambda i: (0, 0)), _row(W),
                  wspec, _row(W), wspec, _row(W), _row(W)],
        out_specs=(_slab(TR, W, 0), wspec, wspec, pl.BlockSpec((SUBLANES, W), lambda i: (0, 0))),
        out_shape=(jax.ShapeDtypeStruct((S, W), f32), jax.ShapeDtypeStruct((NB, HEAD, HEAD), f32),
                   jax.ShapeDtypeStruct((NB, HEAD, HEAD), f32), jax.ShapeDtypeStruct((SUBLANES, W), f32)),
        scratch_shapes=[pltpu.VMEM((TR + SUBLANES, W), f32)],
        compiler_params=_cparams(("arbitrary",)),
    )(proj, proj, da, db, cw, cb, wa, ba, wx, bx, lam)


def _lru_conv_bwd(cfg, proj, dxc, cw):
    S, W, TR = cfg.S, cfg.LW, cfg.TR
    main, prev, _ = _lru_halo_specs(cfg, cfg.o_lx, W)
    dmain, _, dnext = _lru_halo_specs(cfg, 0, W)

    def body(x_ref, xhalo_ref, d_ref, dhalo_ref, cw_ref, dx_ref, sum_ref, xs, ds):
        i = pl.program_id(0)
        cw = cw_ref[...]
        d = d_ref[...]
        xs[0:SUBLANES, :] = jnp.where(i == 0, 0.0, xhalo_ref[...])
        xs[SUBLANES:SUBLANES + TR, :] = x_ref[...]
        ds[0:TR, :] = d
        ds[TR:TR + SUBLANES, :] = jnp.where(i == pl.num_programs(0) - 1, 0.0, dhalo_ref[...])
        dx = ds[pl.ds(CONV - 1, TR), :] * cw[0:1, :]
        parts = [jnp.sum(d * xs[pl.ds(SUBLANES - (CONV - 1), TR), :], axis=0, keepdims=True)]
        for j in range(1, CONV):
            dx = dx + ds[pl.ds(CONV - 1 - j, TR), :] * cw[j:j + 1, :]
            parts.append(jnp.sum(d * xs[pl.ds(SUBLANES - (CONV - 1) + j, TR), :], axis=0, keepdims=True))
        dx_ref[...] = dx.astype(bf16)
        part = jnp.concatenate(parts + [jnp.sum(d, axis=0, keepdims=True), jnp.zeros((SUBLANES - CONV - 1, W), f32)], axis=0)

        @pl.when(i == 0)
        def _():
            sum_ref[...] = part

        @pl.when(i > 0)
        def _():
            sum_ref[...] += part

    return pl.pallas_call(
        body, name="lru_conv_bwd", grid=(S // TR,),
        in_specs=[main, prev, dmain, dnext, pl.BlockSpec((CONV, W), lambda i: (0, 0))],
        out_specs=(_slab(TR, W, 0), pl.BlockSpec((SUBLANES, W), lambda i: (0, 0))),
        out_shape=(jax.ShapeDtypeStruct((S, W), bf16), jax.ShapeDtypeStruct((SUBLANES, W), f32)),
        scratch_shapes=[pltpu.VMEM((TR + SUBLANES, W), f32), pltpu.VMEM((TR + SUBLANES, W), f32)],
        compiler_params=_cparams(("arbitrary",)),
    )(proj, proj, dxc, dxc, cw)


def _rms(x, g):
    r = lax.rsqrt(jnp.mean(x * x, axis=-1, keepdims=True) + NORM_EPS)
    return (x * r) * g, r


def _mla_norm(cfg, proj, qg, kg):
    S, TR = cfg.S, cfg.TR

    def body(q_ref, k_ref, qg_ref, kg_ref, qn_ref, kn_ref):
        qn_ref[...] = _rms(q_ref[...], qg_ref[...])[0].astype(bf16)
        kn_ref[...] = _rms(k_ref[...], kg_ref[...])[0].astype(bf16)

    return pl.pallas_call(
        body, name="mla_norm", grid=(S // TR,),
        in_specs=[_slab(TR, cfg.QL, cfg.o_mq), _slab(TR, cfg.KL, cfg.o_mkv), _row(cfg.QL), _row(cfg.KL)],
        out_specs=(_slab(TR, cfg.QL, 0), _slab(TR, cfg.KL, 0)),
        out_shape=(jax.ShapeDtypeStruct((S, cfg.QL), bf16), jax.ShapeDtypeStruct((S, cfg.KL), bf16)),
        compiler_params=_cparams(("parallel",)),
    )(proj, proj, qg, kg)


def _mla_norm_bwd(cfg, proj, dqn, dkn, qg, kg):
    S, TR = cfg.S, cfg.TR

    def one(x, g, dn):
        r = lax.rsqrt(jnp.mean(x * x, axis=-1, keepdims=True) + NORM_EPS)
        xn = x * r
        dxn = dn * g
        dx = r * (dxn - xn * jnp.mean(dxn * xn, axis=-1, keepdims=True))
        return dx, jnp.sum(dn * xn, axis=0, keepdims=True)

    def body(q_ref, k_ref, dq_ref, dk_ref, qg_ref, kg_ref, dmq_ref, dmk_ref, sq_ref, sk_ref):
        i = pl.program_id(0)
        dq, gq = one(q_ref[...], qg_ref[...], dq_ref[...])
        dk, gk = one(k_ref[...], kg_ref[...], dk_ref[...])
        dmq_ref[...] = dq.astype(bf16)
        dmk_ref[...] = dk.astype(bf16)
        pq = jnp.concatenate([gq, jnp.zeros((SUBLANES - 1, cfg.QL), f32)], axis=0)
        pk = jnp.concatenate([gk, jnp.zeros((SUBLANES - 1, cfg.KL), f32)], axis=0)

        @pl.when(i == 0)
        def _():
            sq_ref[...] = pq
            sk_ref[...] = pk

        @pl.when(i > 0)
        def _():
            sq_ref[...] += pq
            sk_ref[...] += pk

    return pl.pallas_call(
        body, name="mla_norm_bwd", grid=(S // TR,),
        in_specs=[_slab(TR, cfg.QL, cfg.o_mq), _slab(TR, cfg.KL, cfg.o_mkv), _slab(TR, cfg.QL, 0), _slab(TR, cfg.KL, 0),
                  _row(cfg.QL), _row(cfg.KL)],
        out_specs=(_slab(TR, cfg.QL, 0), _slab(TR, cfg.KL, 0), pl.BlockSpec((SUBLANES, cfg.QL), lambda i: (0, 0)),
                   pl.BlockSpec((SUBLANES, cfg.KL), lambda i: (0, 0))),
        out_shape=(jax.ShapeDtypeStruct((S, cfg.QL), bf16), jax.ShapeDtypeStruct((S, cfg.KL), bf16),
                   jax.ShapeDtypeStruct((SUBLANES, cfg.QL), f32), jax.ShapeDtypeStruct((SUBLANES, cfg.KL), f32)),
        compiler_params=_cparams(("arbitrary",)),
    )(proj, proj, dqn, dkn, qg, kg)


def _mla_pack(cfg, proj, q, kv, cq, sq, ck, sk):
    S, TR, MH = cfg.S, cfg.TR, cfg.MH
    NW, RWD = MH * HEAD, MH * ROPE

    def body(q_ref, kv_ref, kr_ref, cq_ref, sq_ref, ck_ref, sk_ref, qo_ref, ko_ref, vo_ref):
        q, kv = q_ref[...], kv_ref[...]
        qr = _rope64(q[:, NW:], cq_ref[...], sq_ref[...])
        kr = _rope64(kr_ref[...], ck_ref[...], sk_ref[...]).astype(bf16)
        lane = lax.broadcasted_iota(jnp.int32, (TR, HEAD), 1)
        for h in range(MH):
            grp = qr[:, (h // 2) * HEAD:(h // 2 + 1) * HEAD]
            if h % 2:
                grp = pltpu.roll(grp, 64, axis=1)
            qo_ref[h] = jnp.concatenate([q[:, h * HEAD:(h + 1) * HEAD], jnp.where(lane < ROPE, grp, 0.0)], axis=1).astype(bf16)
            ko_ref[h] = jnp.concatenate([kv[:, 2 * h * HEAD:(2 * h + 1) * HEAD].astype(bf16), kr], axis=1)
            vo_ref[h] = kv[:, (2 * h + 1) * HEAD:(2 * h + 2) * HEAD].astype(bf16)

    hspec = lambda w: pl.BlockSpec((MH, TR, w), lambda i: (0, i, 0))
    return pl.pallas_call(
        body, name="mla_pack", grid=(S // TR,),
        in_specs=[_slab(TR, cfg.QW, 0), _slab(TR, cfg.KVW, 0), _slab(TR, HEAD, cfg.o_mkr),
                  _slab(TR, RWD, 0), _slab(TR, RWD, 0), _slab(TR, HEAD, 0), _slab(TR, HEAD, 0)],
        out_specs=(hspec(2 * HEAD), hspec(2 * HEAD), hspec(HEAD)),
        out_shape=(jax.ShapeDtypeStruct((MH, S, 2 * HEAD), bf16), jax.ShapeDtypeStruct((MH, S, 2 * HEAD), bf16),
                   jax.ShapeDtypeStruct((MH, S, HEAD), bf16)),
        compiler_params=_cparams(("parallel",)),
    )(q, kv, proj, cq, sq, ck, sk)


def _mla_unpack_bwd(cfg, dq3, dk3, dv3, cq, sq, ck, sk):
    S, TR, MH = cfg.S, cfg.TR, cfg.MH
    RWD = MH * ROPE

    def body(dq_ref, dk_ref, dv_ref, cq_ref, sq_ref, ck_ref, sk_ref, q_ref, kv_ref, kr_ref):
        lane = lax.broadcasted_iota(jnp.int32, (TR, HEAD), 1)
        nope, ropes, kvs = [], [], []
        dkr = jnp.zeros((TR, HEAD), f32)
        for h in range(MH):
            dq = dq_ref[h]
            nope.append(dq[:, :HEAD])
            part = jnp.where(lane < ROPE, dq[:, HEAD:], 0.0)
            if h % 2:
                ropes[-1] = ropes[-1] + pltpu.roll(part, 64, axis=1)
            else:
                ropes.append(part)
            dk = dk_ref[h]
            kvs += [dk[:, :HEAD], dv_ref[h]]
            dkr = dkr + dk[:, HEAD:]
        dqr = _rope64_t(jnp.concatenate(ropes, axis=1), cq_ref[...], sq_ref[...])
        q_ref[...] = jnp.concatenate(nope + [dqr], axis=1).astype(bf16)
        kv_ref[...] = jnp.concatenate(kvs, axis=1).astype(bf16)
        dkr = jnp.where(lane < ROPE, dkr, 0.0)
        kr_ref[...] = _rope64_t(dkr, ck_ref[...], sk_ref[...]).astype(bf16)

    hspec = lambda w: pl.BlockSpec((MH, TR, w), lambda i: (0, i, 0))
    return pl.pallas_call(
        body, name="mla_unpack_bwd", grid=(S // TR,),
        in_specs=[hspec(2 * HEAD), hspec(2 * HEAD), hspec(HEAD), _slab(TR, RWD, 0), _slab(TR, RWD, 0),
                  _slab(TR, HEAD, 0), _slab(TR, HEAD, 0)],
        out_specs=(_slab(TR, cfg.QW, 0), _slab(TR, cfg.KVW, 0), _slab(TR, HEAD, 0)),
        out_shape=(jax.ShapeDtypeStruct((S, cfg.QW), bf16), jax.ShapeDtypeStruct((S, cfg.KVW), bf16),
                   jax.ShapeDtypeStruct((S, HEAD), bf16)),
        compiler_params=_cparams(("parallel",)),
    )(dq3, dk3, dv3, cq, sq, ck, sk)


def _mla_probs(cfg, q, k, i):
    TQ, n = cfg.TQ, k.shape[0]
    nt = (((1,), (1,)), ((), ()))
    s = lax.dot_general(q, k, nt, preferred_element_type=f32) * ((HEAD + ROPE) ** -0.5)
    qc = (i * TQ + lax.broadcasted_iota(jnp.int32, (TQ, n), 0)) // CHUNK
    kc = lax.broadcasted_iota(jnp.int32, (TQ, n), 1) // CHUNK
    s = jnp.where(kc <= qc, s, -1e30)
    m = jnp.max(s, axis=-1, keepdims=True)
    e = jnp.exp(s - m)
    return e / jnp.sum(e, axis=-1, keepdims=True)


def _mla_attn_specs(cfg):
    S, TQ = cfg.S, cfg.TQ
    qs = lambda w: pl.BlockSpec((None, TQ, w), lambda h, i: (h, i, 0))
    ks = lambda w: pl.BlockSpec((None, S, w), lambda h, i: (h, 0, 0))
    hs = lambda off: pl.BlockSpec((TQ, HEAD), lambda h, i, _c=off // HEAD: (i, _c + h))
    return qs, ks, hs


def _mla_attn_fwd(cfg, proj, q3, k3, v3):
    S, TQ, MH = cfg.S, cfg.TQ, cfg.MH
    qs, ks, hs = _mla_attn_specs(cfg)

    def body(q_ref, k_ref, v_ref, g_ref, o_ref, y_ref):
        for i in range(S // TQ):
            @pl.when(pl.program_id(1) == i)
            def _(i=i):
                n = (i + 1) * TQ
                p = _mla_probs(cfg, q_ref[...], k_ref[0:n, :], i)
                o = jnp.dot(p.astype(bf16), v_ref[0:n, :], preferred_element_type=f32)
                o_ref[...] = o
                y_ref[...] = (o * _silu(g_ref[...])).astype(bf16)

    return pl.pallas_call(
        body, name="mla_attn_fwd", grid=(MH, S // TQ),
        in_specs=[qs(2 * HEAD), ks(2 * HEAD), ks(HEAD), hs(cfg.o_mg)],
        out_specs=(hs(0), hs(0)),
        out_shape=(jax.ShapeDtypeStruct((S, cfg.MW), f32), jax.ShapeDtypeStruct((S, cfg.MW), bf16)),
        compiler_params=_cparams(("parallel", "parallel")),
    )(q3, k3, v3, proj)


def _mla_attn_bwd(cfg, proj, q3, k3, v3, o, dy):
    S, TQ, MH = cfg.S, cfg.TQ, cfg.MH
    qs, ks, hs = _mla_attn_specs(cfg)

    def body(q_ref, k_ref, v_ref, g_ref, o_ref, dy_ref, dq_ref, dk_ref, dv_ref, dg_ref):
        q = q_ref[...]
        gate, dy, o = g_ref[...], dy_ref[...], o_ref[...]
        dg_ref[...] = (dy * o * _dsilu(gate)).astype(bf16)
        dob = (dy * _silu(gate)).astype(bf16)
        nt = (((1,), (1,)), ((), ()))
        tn = (((0,), (0,)), ((), ()))

        @pl.when(pl.program_id(1) == 0)
        def _():
            dk_ref[...] = jnp.zeros_like(dk_ref)
            dv_ref[...] = jnp.zeros_like(dv_ref)

        for i in range(S // TQ):
            @pl.when(pl.program_id(1) == i)
            def _(i=i):
                n = (i + 1) * TQ
                k, v = k_ref[0:n, :], v_ref[0:n, :]
                p = _mla_probs(cfg, q, k, i)
                dv_ref[0:n, :] += lax.dot_general(p.astype(bf16), dob, tn, preferred_element_type=f32)
                dp = lax.dot_general(dob, v, nt, preferred_element_type=f32)
                ds = (p * (dp - jnp.sum(dp * p, axis=-1, keepdims=True)) * ((HEAD + ROPE) ** -0.5)).astype(bf16)
                dq_ref[...] = jnp.dot(ds, k, preferred_element_type=f32)
                dk_ref[0:n, :] += lax.dot_general(ds, q, tn, preferred_element_type=f32)

    return pl.pallas_call(
        body, name="mla_attn_bwd", grid=(MH, S // TQ),
        in_specs=[qs(2 * HEAD), ks(2 * HEAD), ks(HEAD), hs(cfg.o_mg), hs(0), hs(0)],
        out_specs=(qs(2 * HEAD), ks(2 * HEAD), ks(HEAD), hs(0)),
        out_shape=(jax.ShapeDtypeStruct((MH, S, 2 * HEAD), f32), jax.ShapeDtypeStruct((MH, S, 2 * HEAD), f32),
                   jax.ShapeDtypeStruct((MH, S, HEAD), f32), jax.ShapeDtypeStruct((S, cfg.MW), bf16)),
        compiler_params=_cparams(("parallel", "arbitrary")),
    )(q3, k3, v3, proj, o, dy)


def _pick_rows(R, bytes_per_row):
    if R * bytes_per_row <= MM_BUDGET:
        return R
    best = None
    for t in range(16, R, 16):
        if R % t == 0 and t * bytes_per_row <= MM_BUDGET:
            best = t
    assert best is not None, (R, bytes_per_row)
    return best


def _adamw(w, g, m, v, name="adamw"):
    R, C = w.shape
    tr = _pick_rows(R, C * 4 * 7 * 2)
    c1 =1.0 - ADAM_B1 ** ADAM_STEP
    c2 = 1.0 - ADAM_B2 ** ADAM_STEP

    def body(w_ref, g_ref, m_ref, v_ref, d_ref, mo_ref, vo_ref):
        g = g_ref[...]
        m = ADAM_B1 * m_ref[...] + (1.0 - ADAM_B1) * g
        v = ADAM_B2 * v_ref[...] + (1.0 - ADAM_B2) * jnp.square(g)
        d_ref[...] = -ADAM_LR * ((m / c1) / (jnp.sqrt(v / c2) + ADAM_EPS) + ADAM_WD * w_ref[...])
        mo_ref[...] = m
        vo_ref[...] = v

    spec = pl.BlockSpec((tr, C), lambda i: (i, 0))
    return pl.pallas_call(
        body, name=name, grid=(R // tr,), in_specs=[spec] * 4, out_specs=(spec,) * 3,
        out_shape=(jax.ShapeDtypeStruct((R, C), f32),) * 3,
        compiler_params=_cparams(("parallel",)),
    )(w, g, m, v)


def _adamw_big(w, m, v, mines, others, core, name, half_cols=False):
    L, R, C = w.shape
    hr, hc = (R, C // 2) if half_cols else (R // 2, C)
    tr = _pick_rows(hr, hc * 4 * (7 + 2 * L) * 2)
    nt = hr // tr
    c1 = 1.0 - ADAM_B1 ** ADAM_STEP
    c2 = 1.0 - ADAM_B2 ** ADAM_STEP

    def body(core_ref, w_ref, m_ref, v_ref, *rest):
        g_refs, (go_ref, d_ref, mo_ref, vo_ref) = rest[:2 * L], rest[2 * L:]
        l, h = pl.program_id(0), pl.program_id(1)
        own = h == core_ref[0]
        g = jnp.where(own, g_refs[0][...], g_refs[L][...])
        for k in range(1, L):
            g = jnp.where(l == k, jnp.where(own, g_refs[k][...], g_refs[L + k][...]), g)
        m = ADAM_B1 * m_ref[...] + (1.0 - ADAM_B1) * g
        v = ADAM_B2 * v_ref[...] + (1.0 - ADAM_B2) * jnp.square(g)
        go_ref[...] = g
        d_ref[...] = -ADAM_LR * ((m / c1) / (jnp.sqrt(v / c2) + ADAM_EPS) + ADAM_WD * w_ref[...])
        mo_ref[...] = m
        vo_ref[...] = v

    if half_cols:
        lay = pl.BlockSpec((None, tr, hc), lambda l, h, i, core_ref: (l, i, h))
    else:
        lay = pl.BlockSpec((None, tr, hc), lambda l, h, i, core_ref: (l, h * nt + i, 0))
    gspec = lambda k: pl.BlockSpec((tr, hc), lambda l, h, i, core_ref: (jnp.where(l == k, i, 0), 0))
    return pl.pallas_call(
        body, name=name,
        grid_spec=pltpu.PrefetchScalarGridSpec(
            num_scalar_prefetch=1, grid=(L, 2, nt),
            in_specs=[lay, lay, lay] + [gspec(k) for k in range(L)] * 2, out_specs=(lay,) * 4),
        out_shape=(jax.ShapeDtypeStruct((L, R, C), f32),) * 4,
        compiler_params=_cparams(("arbitrary", "arbitrary", "arbitrary")),
    )(core, w, m, v, *mines, *others)


def _sum_blocks(x, out_dtype, name):
    n, R, C = x.shape
    tr = _pick_rows(R, C * 4 * (n + 1) * 2)

    def body(x_ref, o_ref):
        acc = x_ref[0].astype(f32)
        for k in range(1, n):
            acc = acc + x_ref[k].astype(f32)
        o_ref[...] = acc.astype(o_ref.dtype)

    return pl.pallas_call(
        body, name=name, grid=(R // tr,),
        in_specs=[pl.BlockSpec((n, tr, C), lambda i: (0, i, 0))], out_specs=pl.BlockSpec((tr, C), lambda i: (i, 0)),
        out_shape=jax.ShapeDtypeStruct((R, C), out_dtype),
        compiler_params=_cparams(("parallel",)),
    )(x)


def _hbm_specs(n):
    return [pl.BlockSpec(memory_space=pl.ANY)] * n


def _row_map(cfg):
    nc, k0 = cfg.IN_WIDTH // 4, cfg.o_mg

    def padded(o):
        return o if o < k0 else (cfg.o_mkr + o - k0 if o < k0 + ROPE else o - ROPE)

    cuts = {0, nc}
    for q in range(4):
        cuts |= {b - q * nc for b in (k0, k0 + ROPE) if q * nc < b < (q + 1) * nc}
    cuts = sorted(cuts)
    return [((l0, l1 - l0), tuple(padded(q * nc + l0) for q in range(4))) for l0, l1 in zip(cuts[:-1], cuts[1:])]


def _chip_start(q, starts):
    st = starts[0]
    for i in range(1, 4):
        st = jnp.where(q == i, starts[i], st)
    return pl.multiple_of(st, 16)


def _allgather8(shards, name, cfg=None, zeros=None):
    na = len(shards)
    rmap = _row_map(cfg) if cfg is not None else []
    npc = max(len(rmap), 1)

    def body(*refs):
        x_refs, out_refs = refs[:na], refs[na + 1:2 * na + 1]
        send_sems, recv_sems, local_sems = refs[2 * na + 1:]
        x, y, c = lax.axis_index("x"), lax.axis_index("y"), lax.axis_index("c")
        me, sibling = (x, y, c), (x, y, 1 - c)
        chips = [(1 - x, y), (x, 1 - y), (1 - x, 1 - y)]

        def wins(a, px, py, pc):
            m, n = shards[a].shape
            if a == 0 and rmap:
                cols = pl.ds(pl.multiple_of(pc * n, n), n)
                return [(pl.ds(l0, cnt), out_refs[0].at[pl.ds(_chip_start(2 * px + py, starts), cnt), cols])
                        for (l0, cnt), starts in rmap]
            return [(pl.ds(0, m), out_refs[a].at[pl.ds((4 * px + 2 * py + pc) * m, m), :])]

        def copies(a, k, block, to, from_x):
            return [pltpu.make_async_remote_copy(
                src_ref=x_refs[a].at[rows, :] if from_x else win, dst_ref=win, send_sem=send_sems.at[a, k, p],
                recv_sem=recv_sems.at[a, k, p], device_id=to, device_id_type=MESH)
                for p, (rows, win) in enumerate(wins(a, *block))]

        mine = [pltpu.make_async_copy(x_refs[a].at[rows, :], win, local_sems.at[a, p])
                for a in range(na) for p, (rows, win) in enumerate(wins(a, *me))]
        if zeros is not None:
            nz = zeros.shape[0]
            mine.append(pltpu.make_async_copy(refs[na], out_refs[0].at[pl.ds(cfg.NP - nz, nz), :], local_sems.at[0, npc]))
        for cp in mine:
            cp.start()
        first = []
        for a in range(na):
            first += copies(a, 0, me, sibling, True)
            for j, chip in enumerate(chips):
                first += copies(a, 1 + j, me, (*chip, c), True)
        for cp in first:
            cp.start()
        passed = []
        for j, chip in enumerate(chips):
            for a in range(na):
                for cp in copies(a, 1 + j, (*chip, c), me, False):
                    cp.wait_recv()
                fwd = copies(a, 4 + j, (*chip, c), sibling, False)
                for cp in fwd:
                    cp.start()
                passed += fwd
        for a in range(na):
            for cp in copies(a, 0, sibling, me, False):
                cp.wait_recv()
        for j, chip in enumerate(chips):
            for a in range(na):
                for cp in copies(a, 4 + j, (*chip, 1 - c), me, False):
                    cp.wait_recv()
        for cp in first + passed:
            cp.wait_send()
        for cp in mine:
            cp.wait()

    out_shape = [jax.ShapeDtypeStruct((N_DEV * s.shape[0], s.shape[1]), s.dtype) for s in shards]
    if rmap:
        out_shape[0] = jax.ShapeDtypeStruct((cfg.NP, cfg.D), shards[0].dtype)
    z = zeros if zeros is not None else jnp.zeros((SUBLANES, LANES), f32)
    return pl.pallas_call(
        body, name=name, out_shape=out_shape,
        in_specs=_hbm_specs(na + 1), out_specs=_hbm_specs(na),
        scratch_shapes=[pltpu.SemaphoreType.DMA((na, 7, npc)), pltpu.SemaphoreType.DMA((na, 7, npc)),
                        pltpu.SemaphoreType.DMA((na, npc + 1))],
    )(*shards, z)


_SEM = pl.BlockSpec(memory_space=pltpu.SEMAPHORE)
_HBM = pl.BlockSpec(memory_space=pltpu.HBM)
_EFFECT = pltpu.SideEffectType.DATAFLOW_SIDE_EFFECTING


def _split_start(srcs, lands, after, plan, n, name):
    bufs = list(srcs) + list(lands)
    nb, ns = len(bufs), len(srcs)

    def body(*refs):
        send_sems, recv_sems = refs[nb + 1], refs[nb + 2]
        for k, (src, dst, _, dev) in enumerate(plan(refs[:ns], refs[ns:nb])):
            pltpu.make_async_remote_copy(src_ref=src, dst_ref=dst, send_sem=send_sems.at[k], recv_sem=recv_sems.at[k],
                                         device_id=dev, device_id_type=MESH).start()
        refs[-1][...] = jnp.zeros_like(refs[-1])

    out = pl.pallas_call(
        body, name=name,
        out_shape=(pltpu.SemaphoreType.DMA((n,)), pltpu.SemaphoreType.DMA((n,)), *[pltpu.HBM(b.shape, b.dtype) for b in bufs],
                   jax.ShapeDtypeStruct((SUBLANES, LANES), f32)),
        in_specs=[_HBM] * nb + [pl.BlockSpec(memory_space=pl.ANY)],
        out_specs=(_SEM, _SEM, *[_HBM] * nb, pl.BlockSpec(memory_space=pltpu.VMEM)),
        input_output_aliases={i: 2 + i for i in range(nb)},
        compiler_params=pltpu.CompilerParams(has_side_effects=_EFFECT),
    )(*[pltpu.with_memory_space_constraint(b, pltpu.HBM) for b in bufs], after)
    return out[0], out[1], list(out[2:2 + ns]), list(out[2 + ns:2 + nb]), out[-1]


def _split_wait(srcs, lands, send_sems, recv_sems, after, plan, name):
    bufs = list(srcs) + list(lands)
    nb, ns = len(bufs), len(srcs)

    def body(*refs):
        send, recv = refs[nb], refs[nb + 1]
        for k, (src, _, dst, dev) in enumerate(plan(refs[:ns], refs[ns:nb])):
            cp = pltpu.make_async_remote_copy(src_ref=src, dst_ref=dst, send_sem=send.at[k], recv_sem=recv.at[k],
                                              device_id=dev, device_id_type=MESH)
            cp.wait_send()
            cp.wait_recv()

    out = pl.pallas_call(
        body, name=name, out_shape=tuple(pltpu.HBM(b.shape, b.dtype) for b in bufs),
        in_specs=[_HBM] * nb + [_SEM, _SEM, pl.BlockSpec(memory_space=pl.ANY)], out_specs=tuple([_HBM] * nb),
        input_output_aliases={i: i for i in range(nb)},
        compiler_params=pltpu.CompilerParams(has_side_effects=_EFFECT),
    )(*bufs, send_sems, recv_sems, after)
    return list(out[:ns]), list(out[ns:])


def _weight_windows(cfg, shards, out_refs, px, py, pc):
    rmap = _row_map(cfg)
    m, n = shards[0].shape
    cols = pl.ds(pl.multiple_of(pc * n, n), n)
    wins = [[(pl.ds(l0, cnt), out_refs[0].at[pl.ds(_chip_start(2 * px + py, starts), cnt), cols]) for (l0, cnt), starts in rmap]]
    for a in range(1, len(shards)):
        m = shards[a].shape[0]
        wins.append([(pl.ds(0, m), out_refs[a].at[pl.ds((4 * px + 2 * py + pc) * m, m), :])])
    return wins


def _gather_shapes(cfg, shards):
    return [jax.ShapeDtypeStruct((cfg.NP, cfg.D), shards[0].dtype)] + \
           [jax.ShapeDtypeStruct((N_DEV * s.shape[0], s.shape[1]), s.dtype) for s in shards[1:]]


def _gather_plan(cfg, shards):
    def plan(x_refs, land_refs):
        x, y, c = lax.axis_index("x"), lax.axis_index("y"), lax.axis_index("c")
        mine = _weight_windows(cfg, shards, land_refs, x, y, c)
        out = []
        for peer in [(x, y, 1 - c), (1 - x, y, c), (x, 1 - y, c), (1 - x, 1 - y, c)]:
            theirs = _weight_windows(cfg, shards, land_refs, *peer)
            for a in range(len(shards)):
                for (rows, win), (_, win_in) in zip(mine[a], theirs[a]):
                    out.append((x_refs[a].at[rows, :], win, win_in, peer))
        return out
    return plan


def _gather_finish(cfg, shards, lands, name):
    na = len(shards)
    rmap = _row_map(cfg)
    npc = len(rmap)
    nz = cfg.NP - cfg.o_mkr - ROPE

    def body(*refs):
        x_refs, out_refs = refs[:na], refs[2 * na:3 * na]
        stage, zbuf = refs[3 * na:4 * na], refs[4 * na]
        send_sems, recv_sems, local_sems = refs[4 * na + 1:]
        x, y, c = lax.axis_index("x"), lax.axis_index("y"), lax.axis_index("c")
        sibling = (x, y, 1 - c)
        chips = [(1 - x, y), (x, 1 - y), (1 - x, 1 - y)]
        load = [pltpu.make_async_copy(x_refs[a], stage[a], local_sems.at[a, npc]) for a in range(na)]
        for cp in load:
            cp.start()
        passed = []
        for j, chip in enumerate(chips):
            wins = _weight_windows(cfg, shards, out_refs, *chip, c)
            for a in range(na):
                passed += [pltpu.make_async_remote_copy(src_ref=win, dst_ref=win, send_sem=send_sems.at[a, j, p],
                                                        recv_sem=recv_sems.at[a, j, p], device_id=sibling, device_id_type=MESH)
                           for p, (_, win) in enumerate(wins[a])]
        for cp in passed:
            cp.start()
        zbuf[...] = jnp.zeros_like(zbuf)
        for cp in load:
            cp.wait()
        own = _weight_windows(cfg, shards, out_refs, x, y, c)
        store = [pltpu.make_async_copy(stage[a].at[rows, :], win, local_sems.at[a, p])
                 for a in range(na) for p, (rows, win) in enumerate(own[a])]
        store.append(pltpu.make_async_copy(zbuf, out_refs[0].at[pl.ds(cfg.NP - nz, nz), :], local_sems.at[0, npc + 1]))
        for cp in store:
            cp.start()
        for j, chip in enumerate(chips):
            wins = _weight_windows(cfg, shards, out_refs, *chip, 1 - c)
            for a in range(na):
                for p, (_, win) in enumerate(wins[a]):
                    pltpu.make_async_remote_copy(src_ref=win, dst_ref=win, send_sem=send_sems.at[a, j, p],
                                                 recv_sem=recv_sems.at[a, j, p], device_id=sibling,
                                                 device_id_type=MESH).wait_recv()
        for cp in passed:
            cp.wait_send()
        for cp in store:
            cp.wait()

    return pl.pallas_call(
        body, name=name, out_shape=_gather_shapes(cfg, shards),
        in_specs=_hbm_specs(2 * na), out_specs=_hbm_specs(na),
        input_output_aliases={na + a: a for a in range(na)},
        scratch_shapes=[pltpu.VMEM(s.shape, s.dtype) for s in shards] + [pltpu.VMEM((nz, cfg.D), shards[0].dtype)]
        + [pltpu.SemaphoreType.DMA((na, 3, npc)), pltpu.SemaphoreType.DMA((na, 3, npc)), pltpu.SemaphoreType.DMA((na, npc + 2))],
        compiler_params=pltpu.CompilerParams(vmem_limit_bytes=VMEM_LIMIT),
    )(*shards, *lands)


def _gather_weights_start(cfg, shards, after):
    lands = [lax.empty(s.shape, s.dtype) for s in _gather_shapes(cfg, shards)]
    n = 4 * (len(_row_map(cfg)) + len(shards) - 1)
    return _split_start(shards, lands, after, _gather_plan(cfg, shards), n, "gather_w_start")


def _gather_weights_end(cfg, shards, started, after):
    send_sems, recv_sems, srcs, lands, _ = started
    srcs, lands = _split_wait(srcs, lands, send_sems, recv_sems, after, _gather_plan(cfg, shards), "gather_w_wait")
    return _gather_finish(cfg, srcs, lands, "gather_w_finish")


def _send_sibling(arrays, name):
    na = len(arrays)

    def body(*refs):
        x_refs, out_refs = refs[:na], refs[na:2 * na]
        send_sems, recv_sems = refs[2 * na:]
        sibling = (lax.axis_index("x"), lax.axis_index("y"), 1 - lax.axis_index("c"))
        cps = [pltpu.make_async_remote_copy(src_ref=x_refs[a], dst_ref=out_refs[a], send_sem=send_sems.at[a],
                                            recv_sem=recv_sems.at[a], device_id=sibling, device_id_type=MESH)
               for a in range(na)]
        for cp in cps:
            cp.start()
        for cp in cps:
            cp.wait()

    return pl.pallas_call(
        body, name=name, out_shape=[jax.ShapeDtypeStruct(x.shape, x.dtype) for x in arrays],
        in_specs=_hbm_specs(na), out_specs=_hbm_specs(na),
        scratch_shapes=[pltpu.SemaphoreType.DMA((na,)), pltpu.SemaphoreType.DMA((na,))],
    )(*arrays)


def _add2(a, b, out_dtype, name):
    R, C = a.shape
    tr = _pick_rows(R, C * 4 * 3 * 2)

    def body(a_ref, b_ref, o_ref):
        o_ref[...] = (a_ref[...].astype(f32) + b_ref[...].astype(f32)).astype(o_ref.dtype)

    spec = pl.BlockSpec((tr, C), lambda i: (i, 0))
    return pl.pallas_call(body, name=name, grid=(R // tr,), in_specs=[spec, spec], out_specs=spec,
                          out_shape=jax.ShapeDtypeStruct((R, C), out_dtype), compiler_params=_cparams(("parallel",)))(a, b)


def _slot_pairs(cfg, p_refs, slot_refs, a, to_chip, slot):
    if a == 0:
        return [(p_refs[0].at[pl.ds(_chip_start(to_chip, starts), cnt), :], slot_refs[0].at[slot, pl.ds(l0, cnt), :])
                for (l0, cnt), starts in _row_map(cfg)]
    return [(p_refs[a].at[to_chip], slot_refs[a].at[slot])]


def _scatter_plan(cfg, na):
    def plan(p_refs, slot_refs):
        x, y, c = lax.axis_index("x"), lax.axis_index("y"), lax.axis_index("c")
        mychip = 2 * x + y
        out = []
        for cx, cy in [(1 - x, y), (x, 1 - y), (1 - x, 1 - y)]:
            q = 2 * cx + cy
            for a in range(na):
                for (src, dst), (_, dst_in) in zip(_slot_pairs(cfg, p_refs, slot_refs, a, q, mychip),
                                                   _slot_pairs(cfg, p_refs, slot_refs, a, mychip, q)):
                    out.append((src, dst, dst_in, (cx, cy, c)))
        return out
    return plan


def _slot_shapes(cfg, parts):
    return [jax.ShapeDtypeStruct((4, cfg.IN_WIDTH // 4, parts[0].shape[1]), parts[0].dtype)] + \
           [jax.ShapeDtypeStruct(p.shape, p.dtype) for p in parts[1:]]


def _place_own(cfg, parts, slots, name):
    na = len(parts)
    npc = len(_row_map(cfg))
    shapes = _slot_shapes(cfg, parts)

    def body(*refs):
        p_refs, out_refs = refs[:na], refs[2 * na:3 * na]
        stage, sems = refs[3 * na:4 * na], refs[4 * na]
        mychip = 2 * lax.axis_index("x") + lax.axis_index("y")
        moves = []
        for a in range(na):
            for p, (src, dst) in enumerate(_slot_pairs(cfg, p_refs, out_refs, a, mychip, mychip)):
                buf = stage[a].at[pl.ds(*_row_map(cfg)[p][0]), :] if a == 0 else stage[a]
                moves.append((pltpu.make_async_copy(src, buf, sems.at[a, p]), pltpu.make_async_copy(buf, dst, sems.at[a, npc + p])))
        for load, _ in moves:
            load.start()
        for load, store in moves:
            load.wait()
            store.start()
        for _, store in moves:
            store.wait()

    return pl.pallas_call(
        body, name=name, out_shape=shapes, in_specs=_hbm_specs(2 * na), out_specs=_hbm_specs(na),
        input_output_aliases={na + a: a for a in range(na)},
        scratch_shapes=[pltpu.VMEM(s.shape[1:], s.dtype) for s in shapes] + [pltpu.SemaphoreType.DMA((na, 2 * npc))],
        compiler_params=pltpu.CompilerParams(vmem_limit_bytes=VMEM_LIMIT),
    )(*parts, *slots)


def _reduce_scatter_start(cfg, g_in_t, grads):
    c = lax.axis_index("c")
    hd = cfg.D // 2
    shp = [g.shape[2:] for g in grads]
    keep = [lax.dynamic_slice_in_dim(g_in_t, c * hd, hd, axis=1)]
    give = [lax.dynamic_slice_in_dim(g_in_t, (1 - c) * hd, hd, axis=1)]
    keep += [lax.dynamic_index_in_dim(g, c, axis=1, keepdims=False).reshape(4 * hr, nc) for g, (hr, nc) in zip(grads, shp)]
    give += [lax.dynamic_index_in_dim(g, 1 - c, axis=1, keepdims=False).reshape(4 * hr, nc) for g, (hr, nc) in zip(grads, shp)]
    got = _send_sibling(give, "rs_pair")
    part = [_add2(k, g, bf16, "rs_add_pair") for k, g in zip(keep, got)]
    part = part[:1] + [p.reshape(4, hr, nc) for p, (hr, nc) in zip(part[1:], shp)]
    slots = [lax.empty(s.shape, s.dtype) for s in _slot_shapes(cfg, part)]
    n = 3 * (len(_row_map(cfg)) + len(part) - 1)
    return _split_start(part, slots, part[-1], _scatter_plan(cfg, len(part)), n, "rs_chips_start")


def _reduce_scatter_end(cfg, started, after):
    send_sems, recv_sems, parts, slots, _ = started
    parts, slots = _split_wait(parts, slots, send_sems, recv_sems, after, _scatter_plan(cfg, len(parts)), "rs_chips_wait")
    slots = _place_own(cfg, parts, slots, "rs_own")
    mine = [_sum_blocks(s, f32, "rs_add_chips") for s in slots]
    return mine, _send_sibling(mine, "rs_halves")


def _big_weights(cfg):
    return (("mla_w_uq", cfg.QL, cfg.QW, 1), ("mla_w_ukv", cfg.KL, cfg.KVW, 1),
            ("w_branch", cfg.RW + cfg.LW + cfg.MW, cfg.D, 0), ("w_out", cfg.D, cfg.D, 0))


def _half_shapes(cfg):
    out = []
    for _, r, c, ax in _big_weights(cfg):
        out.append((r // 2, c // 4) if ax == 1 else (r // 8, c))
    return out


def _my_halves(cfg, W, l, c):
    hd = cfg.D // 2
    out = [lax.dynamic_slice_in_dim(W["w_in"][l].T, c * hd, hd, axis=1).astype(bf16)]
    for (name, *_), (hr, nc) in zip(_big_weights(cfg), _half_shapes(cfg)):
        out.append(lax.dynamic_slice_in_dim(W[name][l], c * hr, hr, axis=0).astype(bf16))
    return out


def _uq_split(cfg, w):
    hw = HEAD + ROPE
    return jnp.concatenate([w[:, h * hw:h * hw + HEAD] for h in range(cfg.MH)]
                           + [w[:, h * hw + HEAD:(h + 1) * hw] for h in range(cfg.MH)], axis=1)


def _uq_join(cfg, g):
    n = cfg.MH * HEAD
    parts = []
    for h in range(cfg.MH):
        parts += [g[:, h * HEAD:(h + 1) * HEAD], g[:, n + h * ROPE:n + (h + 1) * ROPE]]
    return jnp.concatenate(parts, axis=1)


def _col_blocks(g):
    nc = g.shape[1] // 4
    return jnp.stack([g[:, q * nc:(q + 1) * nc] for q in range(4)])


def _row_pack(parts):
    rows = []
    for p in parts:
        r = p.reshape(-1, LANES)
        pad = -r.shape[0] % SUBLANES
        rows.append(jnp.concatenate([r, jnp.zeros((pad, LANES), r.dtype)], axis=0) if pad else r)
    return jnp.concatenate(rows, axis=0)


def _row_unpack(packed, like):
    out, off = [], 0
    for p in like:
        n = p.size // LANES
        out.append(packed[off:off + n].reshape(p.shape))
        off += -(-n // SUBLANES) * SUBLANES
    return out


def _prep_layer(cfg, full, small):
    w_in_t, w_uq, w_ukv, w_branch, w_out = full
    RW, LW = cfg.RW, cfg.LW
    P = dict(small)
    P["w_in_t"] = w_in_t
    P["w_uq"] = _uq_split(cfg, jnp.concatenate(list(w_uq.reshape(4, cfg.QL, -1)), axis=1))
    P["w_ukv"] = jnp.concatenate(list(w_ukv.reshape(4, cfg.KL, -1)), axis=1)
    P["wb"] = (w_branch[:RW], w_branch[RW:RW + LW], w_branch[RW + LW:])
    P["w_out"] = w_out
    return P


def _layer_fwd(cfg, x, mod, P, T):
    h = _prenorm_fwd(cfg, x, mod, P["norm_pre"])
    proj = _mm(h, P["w_in_t"], f32, "mm_proj", mode="nt")
    y_ret = _ret_fwd(cfg, proj, P["ret_gn"], T["cos_r"], T["sin_r"], T["ret_consts"])
    a, b = _lru_gates(cfg, proj, P["lru_conv_w"], P["lru_conv_b"], P["lru_wa"], P["lru_ba"], P["lru_wx"], P["lru_bx"],
                      P["lru_lambda"])
    hl, y_lru = _lru_scan_fwd(cfg, proj, a, b)
    qn, kn = _mla_norm(cfg, proj, P["mla_q_norm"], P["mla_kv_norm"])
    q = _mm(qn, P["w_uq"], f32, "mm_uq")
    kv = _mm(kn, P["w_ukv"], f32, "mm_ukv")
    q3, k3, v3 = _mla_pack(cfg, proj, q, kv, T["cos_q"], T["sin_q"], T["cos_k"], T["sin_k"])
    o, y_mla = _mla_attn_fwd(cfg, proj, q3, k3, v3)
    ys = (y_ret, y_lru, y_mla)
    us = tuple(_mm(yb, wb, f32, "mm_branch") for yb, wb in zip(ys, P["wb"]))
    merged = _merge_fwd(cfg, proj, *us)
    y = _mm(merged, P["w_out"], f32, "mm_out")
    out = _postnorm_fwd(cfg, x, y, mod, P["norm_post"])
    R = dict(x=x, h=h, proj=proj, ys=ys, a=a, hl=hl, qn=qn, kn=kn, q3=q3, k3=k3, v3=v3, o=o, us=us, merged=merged, y=y)
    return out, R


def _layer_bwd(cfg, dout, R, mod, P, T):
    proj = R["proj"]
    dy, s_post = _postnorm_bwd(cfg, dout, R["y"], mod, P["norm_post"])
    dmerged = _mm(dy, P["w_out"], f32, "mm_dmerged", mode="nt")
    g_out = _mm(R["merged"], dy, bf16, "mm_gw_out", mode="tn")
    du0, du1, du2, dlog = _merge_bwd(cfg, proj, dmerged, *R["us"])
    dus = (du0, du1, du2)
    dys = tuple(_mm(du, wb, f32, "mm_dbranch", mode="nt") for du, wb in zip(dus, P["wb"]))
    g_branch = jnp.concatenate([_mm(yb, du, bf16, "mm_gw_branch", mode="tn") for yb, du in zip(R["ys"], dus)], axis=0)
    drq, drk, drv, drg, dgn = _ret_bwd(cfg, proj, dys[0], P["ret_gn"], T["cos_r"], T["sin_r"], T["ret_consts"])
    da, db, dlg = _lru_scan_bwd(cfg, proj, R["a"], R["hl"], dys[1])
    dxc, dwa, dwx, s_lru = _lru_gates_bwd(cfg, proj, da, db, P["lru_conv_w"], P["lru_conv_b"], P["lru_wa"], P["lru_ba"],
                                          P["lru_wx"], P["lru_bx"], P["lru_lambda"])
    dlx, s_conv = _lru_conv_bwd(cfg, proj, dxc, P["lru_conv_w"])
    dq3, dk3, dv3, dmg = _mla_attn_bwd(cfg, proj, R["q3"], R["k3"], R["v3"], R["o"], dys[2])
    dq, dkv, dmkr = _mla_unpack_bwd(cfg, dq3, dk3, dv3, T["cos_q"], T["sin_q"], T["cos_k"], T["sin_k"])
    dqn = _mm(dq, P["w_uq"], f32, "mm_dqn", mode="nt")
    dkn = _mm(dkv, P["w_ukv"], f32, "mm_dkn", mode="nt")
    g_uq = _uq_join(cfg, _mm(R["qn"], dq, bf16, "mm_gw_uq", mode="tn"))
    g_ukv = _mm(R["kn"], dkv, bf16, "mm_gw_ukv", mode="tn")
    dmq, dmkv, s_q, s_k = _mla_norm_bwd(cfg, proj, dqn, dkn, P["mla_q_norm"], P["mla_kv_norm"])
    dproj = jnp.concatenate([drq, drk, drv, drg, dlx, dlg, dmq, dmkv, dmg, dlog, dmkr,
                             jnp.zeros((cfg.S, cfg.NP - cfg.o_mkr - HEAD), bf16)], axis=1)
    dh = _mm(dproj, P["w_in_t"], f32, "mm_dh")
    g_in_t = _mm(dproj, R["h"], bf16, "mm_gw_in", mode="tn", tm=512)
    dx, s_pre = _prenorm_bwd(cfg, R["x"], dh, dout, mod, P["norm_pre"])
    big = [_col_blocks(g_uq), _col_blocks(g_ukv), g_branch, g_out]
    big = (g_in_t, [g.reshape(4, 2, hr, nc) for g, (hr, nc) in zip(big, _half_shapes(cfg))])
    small = dict(norm_pre=s_pre[2:3], norm_post=s_post[1:2], ret_gn=dgn, lru_conv_w=s_conv[0:CONV], lru_conv_b=s_conv[CONV:CONV + 1],
                 lru_wa=dwa, lru_ba=s_lru[0:1], lru_wx=dwx, lru_bx=s_lru[1:2], lru_lambda=s_lru[2:3],
                 mla_q_norm=s_q[0:1], mla_kv_norm=s_k[0:1])
    dmod = jnp.concatenate([s_pre[0:1], s_pre[1:2], s_post[0:1]], axis=1)
    return dx, big, small, dmod


_SMALL = ("norm_pre", "norm_post", "ret_gn", "lru_conv_w", "lru_conv_b", "lru_wa", "lru_ba", "lru_wx", "lru_bx", "lru_lambda",
          "mla_q_norm", "mla_kv_norm")
_WEIGHTS = ("ada_w", "ada_b", "norm_pre", "norm_post", "w_in", "ret_gn", "lru_conv_w", "lru_conv_b", "lru_wa", "lru_ba", "lru_wx",
            "lru_bx", "lru_lambda", "mla_q_norm", "mla_w_uq", "mla_kv_norm", "mla_w_ukv", "w_branch", "w_out")


def _step(cfg, x, c, positions, W, target, M1, V1):
    L, D = cfg.L, cfg.D
    xi, yi, ci = lax.axis_index("x"), lax.axis_index("y"), lax.axis_index("c")
    chip = 2 * xi + yi
    me = 2 * chip + ci

    c8 = jnp.concatenate([c, jnp.zeros((SUBLANES - 1, D), f32)], axis=0)
    c_all = _allgather8([c8], "gather_c")[0].reshape(N_DEV, SUBLANES, D)[:, 0]
    mod_sh, c_act = _ada_fwd(cfg, c_all, W["ada_w"])
    n_sh = mod_sh.shape[2]
    mod_half = lax.dynamic_slice_in_dim(mod_sh, ci * (n_sh // 2), n_sh // 2, axis=2).reshape(L * N_DEV, n_sh // 2)
    mod_all = _allgather8([mod_half], "gather_mod")[0].reshape(N_DEV, L, N_DEV, n_sh // 2)
    mod_all = mod_all.transpose(1, 2, 0, 3).reshape(L, N_DEV, 3 * D)
    mods = lax.dynamic_index_in_dim(mod_all, me, axis=1, keepdims=False) + W["ada_b"]

    (cos_r, sin_r), (cos_m, sin_m) = _rope_tables(cfg, positions)
    T = dict(cos_r=cos_r, sin_r=sin_r, cos_q=jnp.tile(cos_m, (1, cfg.MH)), sin_q=jnp.tile(sin_m, (1, cfg.MH)),
             cos_k=jnp.tile(cos_m, (1, 2)), sin_k=jnp.tile(sin_m, (1, 2)), ret_consts=_ret_consts(cfg))

    Ps, Rs = [], []
    act = x[0]
    started = _gather_weights_start(cfg, _my_halves(cfg, W, 0, ci), c)
    cw_all = _allgather8([_pad_rows(W["lru_conv_w"].reshape(L * CONV, -1))], "gather_conv")[0]
    cw_rows = cw_all.shape[0] // N_DEV
    cw_all = cw_all.reshape(4, 2, cw_rows, -1)[:, 0, :L * CONV].transpose(1, 0, 2).reshape(L, CONV, cfg.LW)
    after = mods
    for l in range(L):
        gathered = _gather_weights_end(cfg, started[2], started, after)
        small = {k: (W[k][l] if W[k][l].ndim > 1 else W[k][l][None, :]) for k in _SMALL if k != "lru_conv_w"}
        P = _prep_layer(cfg, gathered, small)
        P["lru_conv_w"] = cw_all[l]
        Ps.append(P)
        mod = mods[l:l + 1]
        if l + 1 < L:
            started = _gather_weights_start(cfg, _my_halves(cfg, W, l + 1, ci), gathered[-1])
            mod = mod + started[4][0, 0]
        act, R = _layer_fwd(cfg, act, mod, P, T)
        Rs.append(R)
        after = act

    dact, lsum = _loss_head(cfg, act, target[0])
    loss = lax.psum(lsum[0, 0], ("x", "y", "c"))

    big_g = [None] * L
    small_g = [None] * L
    dmods = [None] * L
    pending = None
    for l in range(L - 1, -1, -1):
        mod = mods[l:l + 1]
        if pending is not None:
            mod = mod + pending[4][0, 0]
        dact, grads, small_g[l], dmods[l] = _layer_bwd(cfg, dact, Rs[l], mod, Ps[l], T)
        if pending is not None:
            big_g[l + 1] = _reduce_scatter_end(cfg, pending, dact)
        pending = _reduce_scatter_start(cfg, *grads)

    dmod = jnp.concatenate(dmods, axis=0)
    parts = [dmod] + [small_g[l][k] for l in range(L) for k in _SMALL]
    packed = _row_pack(parts)
    allf = _allgather8([packed], "gather_small")[0].reshape(N_DEV, packed.shape[0], LANES)
    summed = _row_unpack(_sum_blocks(allf, f32, "sum_small"), parts)
    gsm = {k: jnp.stack([summed[1 + l * len(_SMALL) + i].reshape(W[k].shape[1:] if k != "lru_conv_w" else (CONV, cfg.LW))
                         for l in range(L)]) for i, k in enumerate(_SMALL)}
    ncw = cfg.LW // 4
    gsm["lru_conv_w"] = lax.dynamic_slice_in_dim(gsm["lru_conv_w"], chip * ncw, ncw, axis=2)
    gsm["ada_b"] = summed[0]
    dmod_all = allf[:, :dmod.size // LANES].reshape(N_DEV, L, 3 * D)
    dmod_sh = lax.dynamic_slice_in_dim(dmod_all, chip * n_sh, n_sh, axis=2).transpose(1, 0, 2)
    G = dict(gsm)
    G["ada_w"] = _ada_bwd(cfg, c_act.T, dmod_sh)
    delta, new_m, new_v = {}, {}, {}
    bigs = ("ada_w", "w_in") + tuple(name for name, *_ in _big_weights(cfg))
    shp = W["ada_w"].shape
    two = lambda a: a.reshape(-1, shp[-1])
    d, m_, v_ = _adamw(two(W["ada_w"]), two(G["ada_w"]), two(M1["ada_w"]), two(V1["ada_w"]), "adamw_ada_w")
    delta["ada_w"], new_m["ada_w"], new_v["ada_w"] = d.reshape(shp), m_.reshape(shp), v_.reshape(shp)
    smalls = [k for k in _WEIGHTS if k not in bigs]
    packs = [_row_pack([src[k] for k in smalls]) for src in (W, G, M1, V1)]
    outs = _adamw(*packs, "adamw_small")
    for dst, o in zip((delta, new_m, new_v), outs):
        for k, val in zip(smalls, _row_unpack(o, [W[k] for k in smalls])):
            dst[k] = val
    big_g[0] = _reduce_scatter_end(cfg, pending, outs[0])
    core = ci.astype(jnp.int32).reshape(1)
    for i, (name, *_) in enumerate(_big_weights(cfg)):
        G[name], delta[name], new_m[name], new_v[name] = _adamw_big(
            W[name], M1[name], V1[name], [big_g[l][0][i + 1] for l in range(L)], [big_g[l][1][i + 1] for l in range(L)], core,
            "adamw_" + name)
    tr_ = lambda a: a.transpose(0, 2, 1)
    outs = _adamw_big(tr_(W["w_in"]), tr_(M1["w_in"]), tr_(V1["w_in"]), [big_g[l][0][0] for l in range(L)],
                      [big_g[l][1][0] for l in range(L)], core, "adamw_w_in", half_cols=True)
    G["w_in"], delta["w_in"], new_m["w_in"], new_v["w_in"] = [tr_(o) for o in outs]

    grad_x = dact[None]
    return (loss, grad_x, *[G[k] for k in _WEIGHTS], *[delta[k] for k in _WEIGHTS], *[new_m[k] for k in _WEIGHTS],
            *[new_v[k] for k in _WEIGHTS])


def _pad_rows(a):
    pad = -a.shape[0] % SUBLANES
    return jnp.concatenate([a, jnp.zeros((pad, a.shape[1]), a.dtype)], axis=0) if pad else a


def kernel(x, c, positions, ada_w, ada_b, norm_pre, norm_post, w_in, ret_gn, lru_conv_w, lru_conv_b, lru_wa, lru_ba, lru_wx, lru_bx, lru_lambda, mla_q_norm, mla_w_uq, mla_kv_norm, mla_w_ukv, w_branch, w_out, loss_target, m_ada_w, m_ada_b, m_norm_pre, m_norm_post, m_w_in, m_ret_gn, m_lru_conv_w, m_lru_conv_b, m_lru_wa, m_lru_ba, m_lru_wx, m_lru_bx, m_lru_lambda, m_mla_q_norm, m_mla_w_uq, m_mla_kv_norm, m_mla_w_ukv, m_w_branch, m_w_out, v_ada_w, v_ada_b, v_norm_pre, v_norm_post, v_w_in, v_ret_gn, v_lru_conv_w, v_lru_conv_b, v_lru_wa, v_lru_ba, v_lru_wx, v_lru_bx, v_lru_lambda, v_mla_q_norm, v_mla_w_uq, v_mla_kv_norm, v_mla_w_ukv, v_w_branch, v_w_out):
    W = dict(ada_w=ada_w, ada_b=ada_b, norm_pre=norm_pre, norm_post=norm_post, w_in=w_in, ret_gn=ret_gn, lru_conv_w=lru_conv_w,
             lru_conv_b=lru_conv_b, lru_wa=lru_wa, lru_ba=lru_ba, lru_wx=lru_wx, lru_bx=lru_bx, lru_lambda=lru_lambda,
             mla_q_norm=mla_q_norm, mla_w_uq=mla_w_uq, mla_kv_norm=mla_kv_norm, mla_w_ukv=mla_w_ukv, w_branch=w_branch, w_out=w_out)
    M1 = dict(ada_w=m_ada_w, ada_b=m_ada_b, norm_pre=m_norm_pre, norm_post=m_norm_post, w_in=m_w_in, ret_gn=m_ret_gn,
              lru_conv_w=m_lru_conv_w, lru_conv_b=m_lru_conv_b, lru_wa=m_lru_wa, lru_ba=m_lru_ba, lru_wx=m_lru_wx, lru_bx=m_lru_bx,
              lru_lambda=m_lru_lambda, mla_q_norm=m_mla_q_norm, mla_w_uq=m_mla_w_uq, mla_kv_norm=m_mla_kv_norm,
              mla_w_ukv=m_mla_w_ukv, w_branch=m_w_branch, w_out=m_w_out)
    V1 = dict(ada_w=v_ada_w, ada_b=v_ada_b, norm_pre=v_norm_pre, norm_post=v_norm_post, w_in=v_w_in, ret_gn=v_ret_gn,
              lru_conv_w=v_lru_conv_w, lru_conv_b=v_lru_conv_b, lru_wa=v_lru_wa, lru_ba=v_lru_ba, lru_wx=v_lru_wx, lru_bx=v_lru_bx,
              lru_lambda=v_lru_lambda, mla_q_norm=v_mla_q_norm, mla_w_uq=v_mla_w_uq, mla_kv_norm=v_mla_kv_norm,
              mla_w_ukv=v_mla_w_ukv, w_branch=v_w_branch, w_out=v_w_out)
    return _step(_CFG, x, c, positions, W, loss_target, M1, V1)
```

```python
import functools
import math
from typing import NamedTuple

import numpy as np
import jax
import jax.numpy as jnp
from jax import lax
from jax.experimental import pallas as pl
from jax.experimental.pallas import tpu as pltpu

f32 = jnp.float32
bf16 = jnp.bfloat16

NORM_EPS = 1e-6
ROPE_BASE = 10000.0
CHUNK = 64
HEAD = 128
ROPE = 64
CONV = 4
LRU_C = 8.0
ADAM_LR, ADAM_B1, ADAM_B2, ADAM_EPS, ADAM_WD, ADAM_STEP = 0.001, 0.9, 0.999, 1e-08, 0.01, 10

LANES = 128
SUBLANES = 8
VMEM_LIMIT = 56 * 1024 * 1024
MM_BUDGET = 40 * 1024 * 1024
N_DEV = 8
MESH = pl.DeviceIdType.MESH


class Cfg(NamedTuple):
    D: int = 2048
    S: int = 2048
    L: int = 4
    H: int = 8
    NB: int = 8
    MH: int = 8
    QL: int = 512
    KL: int = 512
    TR: int = 256
    TQ: int = 256

    @property
    def RW(self): return self.H * HEAD
    @property
    def LW(self): return self.NB * HEAD
    @property
    def MW(self): return self.MH * HEAD
    @property
    def o_rk(self): return self.RW
    @property
    def o_rv(self): return 2 * self.RW
    @property
    def o_rg(self): return 3 * self.RW
    @property
    def o_lx(self): return 4 * self.RW
    @property
    def o_lg(self): return 4 * self.RW + self.LW
    @property
    def o_mq(self): return 4 * self.RW + 2 * self.LW
    @property
    def o_mkv(self): return self.o_mq + self.QL
    @property
    def o_mg(self): return self.o_mkv + self.KL
    @property
    def o_merge(self): return self.o_mg + self.MW
    @property
    def o_mkr(self): return self.o_merge + 3 * self.D
    @property
    def NP(self): return -(-(self.o_mkr + ROPE) // 512) * 512
    @property
    def IN_WIDTH(self): return self.o_mkr + ROPE
    @property
    def QW(self): return self.MH * (HEAD + ROPE)
    @property
    def KVW(self): return self.MH * 2 * HEAD


_CFG = Cfg()


def _cparams(sem=None):
    return pltpu.CompilerParams(dimension_semantics=sem, vmem_limit_bytes=VMEM_LIMIT)


def _sigmoid(x):
    return jax.nn.sigmoid(x)


def _silu(x):
    return x * _sigmoid(x)


def _dsilu(x):
    s = _sigmoid(x)
    return s * (1.0 + x * (1.0 - s))


def _slab(rows, width, off):
    assert off % width == 0
    return pl.BlockSpec((rows, width), lambda i, _c=off // width: (i, _c))


def _row(width):
    return pl.BlockSpec((1, width), lambda i: (0, 0))


def _mm(a, b, out_dtype=f32, name="mm", mode="nn", tm=None):
    (M, K) = a.shape if mode != "tn" else a.shape[::-1]
    (K2, N) = b.shape if mode != "nt" else b.shape[::-1]
    assert K == K2
    tn = N if N <= 2048 else 512
    tk = K if K <= 2048 else 512
    assert N % tn == 0 and K % tk == 0
    osz = jnp.dtype(out_dtype).itemsize
    if tm is None:
        tm = M
        while 2 * tm * tk * 2 + 2 * tk * tn * 2 + 2 * tm * tn * osz + tm * tn * 4 > MM_BUDGET and tm % 16 == 0:
            tm //= 2
    assert M % tm == 0
    nk = K // tk
    dims = {"nn": (((1,), (0,)), ((), ())), "nt": (((1,), (1,)), ((), ())), "tn": (((0,), (0,)), ((), ()))}[mode]

    def dot(a_ref, b_ref):
        return lax.dot_general(a_ref[...].astype(bf16), b_ref[...].astype(bf16), dims, preferred_element_type=f32)

    if nk == 1:
        def body(a_ref, b_ref, o_ref):
            o_ref[...] = dot(a_ref, b_ref).astype(o_ref.dtype)
        scratch = []
    else:
        def body(a_ref, b_ref, o_ref, acc_ref):
            k = pl.program_id(2)

            @pl.when(k == 0)
            def _():
                acc_ref[...] = jnp.zeros_like(acc_ref)

            acc_ref[...] += dot(a_ref, b_ref)

            @pl.when(k == nk - 1)
            def _():
                o_ref[...] = acc_ref[...].astype(o_ref.dtype)
        scratch = [pltpu.VMEM((tm, tn), f32)]

    a_spec = pl.BlockSpec((tk, tm), lambda i, j, k: (k, i)) if mode == "tn" else pl.BlockSpec((tm, tk), lambda i, j, k: (i, k))
    b_spec = pl.BlockSpec((tn, tk), lambda i, j, k: (j, k)) if mode == "nt" else pl.BlockSpec((tk, tn), lambda i, j, k: (k, j))
    return pl.pallas_call(
        body, name=name,
        grid=(M // tm, N // tn, nk),
        in_specs=[a_spec, b_spec],
        out_specs=pl.BlockSpec((tm, tn), lambda i, j, k: (i, j)),
        out_shape=jax.ShapeDtypeStruct((M, N), out_dtype),
        scratch_shapes=scratch,
        compiler_params=_cparams(("parallel", "parallel", "arbitrary")),
    )(a, b)


def _ada_fwd(cfg, c_all, ada_w):
    L, D, n = ada_w.shape
    tn = n // 2 if (n // 2) % LANES == 0 else n

    def body(c_ref, w_ref, o_ref, ca_ref):
        ca = _silu(c_ref[...])
        ca_ref[...] = ca
        o_ref[0] = jnp.dot(ca.astype(bf16), w_ref[0].astype(bf16), preferred_element_type=f32)

    return pl.pallas_call(
        body, name="ada_fwd", grid=(L, n // tn),
        in_specs=[pl.BlockSpec((N_DEV, D), lambda l, j: (0, 0)), pl.BlockSpec((1, D, tn), lambda l, j: (l, 0, j))],
        out_specs=(pl.BlockSpec((1, N_DEV, tn), lambda l, j: (l, 0, j)), pl.BlockSpec((N_DEV, D), lambda l, j: (0, 0))),
        out_shape=(jax.ShapeDtypeStruct((L, N_DEV, n), f32), jax.ShapeDtypeStruct((N_DEV, D), f32)),
        compiler_params=_cparams(("arbitrary", "arbitrary")),
    )(c_all, ada_w)


def _ada_bwd(cfg, c_act_t, dmod):
    L, _, n = dmod.shape
    D = c_act_t.shape[0]
    tn = n // 2 if (n // 2) % LANES == 0 else n

    def body(c_ref, d_ref, o_ref):
        o_ref[0] = jnp.dot(c_ref[...].astype(bf16), d_ref[0].astype(bf16), preferred_element_type=f32)

    return pl.pallas_call(
        body, name="ada_bwd", grid=(L, n // tn),
        in_specs=[pl.BlockSpec((D, N_DEV), lambda l, j: (0, 0)), pl.BlockSpec((1, N_DEV, tn), lambda l, j: (l, 0, j))],
        out_specs=pl.BlockSpec((1, D, tn), lambda l, j: (l, 0, j)),
        out_shape=jax.ShapeDtypeStruct((L, D, n), f32),
        compiler_params=_cparams(("parallel", "parallel")),
    )(c_act_t, dmod)


def _prenorm_fwd(cfg, x, mod, gain):
    S, D, TR = cfg.S, cfg.D, cfg.TR

    def body(x_ref, mod_ref, g_ref, h_ref):
        x = x_ref[...]
        r = lax.rsqrt(jnp.mean(x * x, axis=-1, keepdims=True) + NORM_EPS)
        shift, scale = mod_ref[:, 0:D], mod_ref[:, D:2 * D]
        h_ref[...] = ((x * r) * g_ref[...] * (1.0 + scale) + shift).astype(bf16)

    return pl.pallas_call(
        body, name="prenorm_fwd", grid=(S // TR,),
        in_specs=[_slab(TR, D, 0), _row(3 * D), _row(D)],
        out_specs=_slab(TR, D, 0), out_shape=jax.ShapeDtypeStruct((S, D), bf16),
        compiler_params=_cparams(("parallel",)),
    )(x, mod, gain)


def _prenorm_bwd(cfg, x, dh, dres, mod, gain):
    S, D, TR = cfg.S, cfg.D, cfg.TR

    def body(x_ref, dh_ref, dres_ref, mod_ref, g_ref, dx_ref, sum_ref):
        i = pl.program_id(0)
        x, dh, g = x_ref[...], dh_ref[...], g_ref[...]
        scale = mod_ref[:, D:2 * D]
        r = lax.rsqrt(jnp.mean(x * x, axis=-1, keepdims=True) + NORM_EPS)
        xn = x * r
        t = dh * xn
        dxn = dh * (g * (1.0 + scale))
        dx_ref[...] = r * (dxn - xn * jnp.mean(dxn * xn, axis=-1, keepdims=True)) + dres_ref[...]
        part = jnp.concatenate([jnp.sum(dh, axis=0, keepdims=True), jnp.sum(t * g, axis=0, keepdims=True),
                                jnp.sum(t * (1.0 + scale), axis=0, keepdims=True), jnp.zeros((SUBLANES - 3, D), f32)], axis=0)

        @pl.when(i == 0)
        def _():
            sum_ref[...] = part

        @pl.when(i > 0)
        def _():
            sum_ref[...] += part

    return pl.pallas_call(
        body, name="prenorm_bwd", grid=(S // TR,),
        in_specs=[_slab(TR, D, 0), _slab(TR, D, 0), _slab(TR, D, 0), _row(3 * D), _row(D)],
        out_specs=(_slab(TR, D, 0), pl.BlockSpec((SUBLANES, D), lambda i: (0, 0))),
        out_shape=(jax.ShapeDtypeStruct((S, D), f32), jax.ShapeDtypeStruct((SUBLANES, D), f32)),
        compiler_params=_cparams(("arbitrary",)),
    )(x, dh, dres, mod, gain)


def _postnorm_fwd(cfg, x, y, mod, gain):
    S, D, TR = cfg.S, cfg.D, cfg.TR

    def body(x_ref, y_ref, mod_ref, g_ref, o_ref):
        y = y_ref[...]
        r = lax.rsqrt(jnp.mean(y * y, axis=-1, keepdims=True) + NORM_EPS)
        rg = mod_ref[:, 2 * D:3 * D]
        o_ref[...] = x_ref[...] + (1.0 + rg) * ((y * r) * g_ref[...])

    return pl.pallas_call(
        body, name="postnorm_fwd", grid=(S // TR,),
        in_specs=[_slab(TR, D, 0), _slab(TR, D, 0), _row(3 * D), _row(D)],
        out_specs=_slab(TR, D, 0), out_shape=jax.ShapeDtypeStruct((S, D), f32),
        compiler_params=_cparams(("parallel",)),
    )(x, y, mod, gain)


def _postnorm_bwd(cfg, dout, y, mod, gain):
    S, D, TR = cfg.S, cfg.D, cfg.TR

    def body(do_ref, y_ref, mod_ref, g_ref, dy_ref, sum_ref):
        i = pl.program_id(0)
        do, y, g = do_ref[...], y_ref[...], g_ref[...]
        rg = mod_ref[:, 2 * D:3 * D]
        r = lax.rsqrt(jnp.mean(y * y, axis=-1, keepdims=True) + NORM_EPS)
        yn = y * r
        t = do * yn
        dyn = do * ((1.0 + rg) * g)
        dy_ref[...] = (r * (dyn - yn * jnp.mean(dyn * yn, axis=-1, keepdims=True))).astype(bf16)
        part = jnp.concatenate([jnp.sum(t * g, axis=0, keepdims=True), jnp.sum(t * (1.0 + rg), axis=0, keepdims=True),
                                jnp.zeros((SUBLANES - 2, D), f32)], axis=0)

        @pl.when(i == 0)
        def _():
            sum_ref[...] = part

        @pl.when(i > 0)
        def _():
            sum_ref[...] += part

    return pl.pallas_call(
        body, name="postnorm_bwd", grid=(S // TR,),
        in_specs=[_slab(TR, D, 0), _slab(TR, D, 0), _row(3 * D), _row(D)],
        out_specs=(_slab(TR, D, 0), pl.BlockSpec((SUBLANES, D), lambda i: (0, 0))),
        out_shape=(jax.ShapeDtypeStruct((S, D), bf16), jax.ShapeDtypeStruct((SUBLANES, D), f32)),
        compiler_params=_cparams(("arbitrary",)),
    )(dout, y, mod, gain)


def _loss_head(cfg, y, target):
    S, D, TR = cfg.S, cfg.D, cfg.TR

    def body(y_ref, t_ref, d_ref, l_ref):
        i = pl.program_id(0)
        err = y_ref[...] - t_ref[...]
        d_ref[...] = err / D
        part = jnp.zeros((SUBLANES, LANES), f32) + 0.5 * jnp.sum(jnp.mean(err * err, axis=-1, keepdims=True))

        @pl.when(i == 0)
        def _():
            l_ref[...] = part

        @pl.when(i > 0)
        def _():
            l_ref[...] += part

    return pl.pallas_call(
        body, name="loss_head", grid=(S // TR,),
        in_specs=[_slab(TR, D, 0), _slab(TR, D, 0)],
        out_specs=(_slab(TR, D, 0), pl.BlockSpec((SUBLANES, LANES), lambda i: (0, 0))),
        out_shape=(jax.ShapeDtypeStruct((S, D), f32), jax.ShapeDtypeStruct((SUBLANES, LANES), f32)),
        compiler_params=_cparams(("arbitrary",)),
    )(y, target)


def _merge_fwd(cfg, proj, u0, u1, u2):
    S, D, TR = cfg.S, cfg.D, cfg.TR

    def body(l0, l1, l2, u0_ref, u1_ref, u2_ref, o_ref):
        o_ref[...] = (_sigmoid(l0[...]) * u0_ref[...] + _sigmoid(l1[...]) * u1_ref[...]
                      + _sigmoid(l2[...]) * u2_ref[...]).astype(bf16)

    return pl.pallas_call(
        body, name="merge_fwd", grid=(S // TR,),
        in_specs=[_slab(TR, D, cfg.o_merge + b * D) for b in range(3)] + [_slab(TR, D, 0)] * 3,
        out_specs=_slab(TR, D, 0), out_shape=jax.ShapeDtypeStruct((S, D), bf16),
        compiler_params=_cparams(("parallel",)),
    )(proj, proj, proj, u0, u1, u2)


def _merge_bwd(cfg, proj, dmerged, u0, u1, u2):
    S, D, TR = cfg.S, cfg.D, cfg.TR

    def body(l0, l1, l2, dm_ref, u0_ref, u1_ref, u2_ref, du0, du1, du2, dl_ref):
        dm = dm_ref[...]
        for b, (l, u, du) in enumerate(((l0, u0_ref, du0), (l1, u1_ref, du1), (l2, u2_ref, du2))):
            g = _sigmoid(l[...])
            du[...] = (dm * g).astype(bf16)
            dl_ref[:, b * D:(b + 1) * D] = (dm * u[...] * (g * (1.0 - g))).astype(bf16)

    return pl.pallas_call(
        body, name="merge_bwd", grid=(S // TR,),
        in_specs=[_slab(TR, D, cfg.o_merge + b * D) for b in range(3)] + [_slab(TR, D, 0)] * 4,
        out_specs=(_slab(TR, D, 0),) * 3 + (_slab(TR, 3 * D, 0),),
        out_shape=(jax.ShapeDtypeStruct((S, D), bf16),) * 3 + (jax.ShapeDtypeStruct((S, 3 * D), bf16),),
        compiler_params=_cparams(("parallel",)),
    )(proj, proj, proj, dmerged, u0, u1, u2)


def _rope128(x, c, s):
    return x * c + pltpu.roll(x, 64, axis=1) * s


def _rope128_t(dy, c, s):
    return dy * c + pltpu.roll(dy * s, 64, axis=1)


def _swap32(x):
    w = x.shape[1]
    lane = lax.broadcasted_iota(jnp.int32, x.shape, 1)
    return jnp.where((lane % 64) < 32, pltpu.roll(x, w - 32, axis=1), pltpu.roll(x, 32, axis=1))


def _rope64(x, c, s):
    return x * c + _swap32(x) * s


def _rope64_t(dy, c, s):
    return dy * c + _swap32(dy * s)


def _rope_tables(cfg, positions):
    pos = positions.astype(f32)[0][:, None]

    def tab(dim):
        inv_freq = ROPE_BASE ** (-jnp.arange(0, dim, 2, dtype=f32) / dim)
        ang = pos * inv_freq
        cos, sin = jnp.cos(ang), jnp.sin(ang)
        return jnp.concatenate([cos, cos], axis=1), jnp.concatenate([-sin, sin], axis=1)

    return tab(HEAD), tab(ROPE)


def _ret_consts(cfg):
    h = np.arange(cfg.H, dtype=np.float64)
    log_gamma = np.log1p(-np.exp2(-5.0 - h)).astype(np.float32)
    idx = np.arange(CHUNK, dtype=np.float32)
    intra = np.exp(log_gamma[:, None, None] * np.abs(idx[:, None] - idx[None, :]))
    kdec = np.exp(log_gamma[:, None] * (CHUNK - 1 - idx)[None, :])
    qdec = np.exp(log_gamma[:, None] * (idx + 1.0)[None, :])
    cdec = np.exp(log_gamma * CHUNK)
    bc = lambda a: jnp.asarray(np.broadcast_to(a[..., None], a.shape + (HEAD,)).astype(np.float32))
    return jnp.asarray(intra.astype(np.float32)), bc(kdec), bc(qdec), bc(cdec[:, None])


def _ret_core(cfg, q_raw, k_raw, v_raw, cos, sin, intra, kdec, qdec, cdec, p_ref):
    S = cfg.S
    NC = S // CHUNK
    q = _rope128(q_raw, cos, sin) * (HEAD ** -0.5)
    k = _rope128(k_raw, cos, sin)
    q3 = q.reshape(NC, CHUNK, HEAD)
    k3 = k.reshape(NC, CHUNK, HEAD)
    qb, kb = q3.astype(bf16), k3.astype(bf16)
    vb = v_raw.reshape(NC, CHUNK, HEAD).astype(bf16)
    sdb = (jnp.einsum('nid,njd->nij', qb, kb, preferred_element_type=f32) * intra[None]).astype(bf16)
    o_intra = jnp.einsum('nij,nje->nie', sdb, vb, preferred_element_type=f32)
    kdb = (k3 * kdec[None]).astype(bf16)
    kv = jnp.einsum('njd,nje->nde', kdb, vb, preferred_element_type=f32)
    p_ref[0] = jnp.zeros((HEAD, HEAD), f32)
    for n in range(1, NC):
        p_ref[n] = p_ref[n - 1] * cdec + kv[n - 1]
    pb = p_ref[...].astype(bf16)
    qdb = (q3 * qdec[None]).astype(bf16)
    o_inter = jnp.einsum('nid,nde->nie', qdb, pb, preferred_element_type=f32)
    o = (o_intra + o_inter).reshape(S, HEAD)
    return o, (qb, kb, vb, sdb, kdb, qdb, pb)


def _ret_specs(cfg):
    S = cfg.S
    hs = lambda off: pl.BlockSpec((S, HEAD), lambda h, _c=off // HEAD: (0, _c + h))
    full = pl.BlockSpec((S, HEAD), lambda h: (0, 0))
    consts = [pl.BlockSpec((None, CHUNK, CHUNK), lambda h: (h, 0, 0)), pl.BlockSpec((None, CHUNK, HEAD), lambda h: (h, 0, 0)),
              pl.BlockSpec((None, CHUNK, HEAD), lambda h: (h, 0, 0)), pl.BlockSpec((None, 1, HEAD), lambda h: (h, 0, 0))]
    gn = pl.BlockSpec((1, HEAD), lambda h: (0, h))
    return hs, full, consts, gn


def _ret_fwd(cfg, proj, gn, cos, sin, consts):
    S, NC = cfg.S, cfg.S // CHUNK
    hs, full, cspecs, gspec = _ret_specs(cfg)

    def body(q_ref, k_ref, v_ref, g_ref, gn_ref, cos_ref, sin_ref, intra, kdec, qdec, cdec, y_ref, p_ref):
        o, _ = _ret_core(cfg, q_ref[...], k_ref[...], v_ref[...], cos_ref[...], sin_ref[...],
                         intra[...], kdec[...], qdec[...], cdec[...], p_ref)
        mean = jnp.mean(o, axis=-1, keepdims=True)
        var = jnp.mean(jnp.square(o - mean), axis=-1, keepdims=True)
        z = ((o - mean) * lax.rsqrt(var + NORM_EPS)) * gn_ref[...]
        y_ref[...] = (z * _silu(g_ref[...])).astype(bf16)

    return pl.pallas_call(
        body, name="ret_fwd", grid=(cfg.H,),
        in_specs=[hs(0), hs(cfg.o_rk), hs(cfg.o_rv), hs(cfg.o_rg), gspec, full, full] + cspecs,
        out_specs=hs(0), out_shape=jax.ShapeDtypeStruct((S, cfg.RW), bf16),
        scratch_shapes=[pltpu.VMEM((NC, HEAD, HEAD), f32)],
        compiler_params=_cparams(("arbitrary",)),
    )(proj, proj, proj, proj, gn, cos, sin, *consts)


def _ret_bwd(cfg, proj, dy, gn, cos, sin, consts):
    S, NC = cfg.S, cfg.S // CHUNK
    hs, full, cspecs, gspec = _ret_specs(cfg)

    def body(q_ref, k_ref, v_ref, g_ref, dy_ref, gn_ref, cos_ref, sin_ref, intra_ref, kdec_ref, qdec_ref, cdec_ref,
             dq_ref, dk_ref, dv_ref, dg_ref, dgn_ref, p_ref, g_scr):
        cos, sin = cos_ref[...], sin_ref[...]
        intra, kdec, qdec, cdec = intra_ref[...], kdec_ref[...], qdec_ref[...], cdec_ref[...]
        o, (qb, kb, vb, sdb, kdb, qdb, pb) = _ret_core(cfg, q_ref[...], k_ref[...], v_ref[...], cos, sin,
                                                     intra, kdec, qdec, cdec, p_ref)
        gate, dy, gnv = g_ref[...], dy_ref[...], gn_ref[...]
        mean = jnp.mean(o, axis=-1, keepdims=True)
        rstd = lax.rsqrt(jnp.mean(jnp.square(o - mean), axis=-1, keepdims=True) + NORM_EPS)
        on = (o - mean) * rstd
        dz = dy * _silu(gate)
        dg_ref[...] = (dy * (on * gnv) * _dsilu(gate)).astype(bf16)
        dgn_ref[...] = jnp.sum(dz * on, axis=0, keepdims=True)
        don = dz * gnv
        do = rstd * (don - jnp.mean(don, axis=-1, keepdims=True) - on * jnp.mean(don * on, axis=-1, keepdims=True))
        dob = do.reshape(NC, CHUNK, HEAD).astype(bf16)
        dsb = (jnp.einsum('nie,nje->nij', dob, vb, preferred_element_type=f32) * intra[None]).astype(bf16)
        dv = jnp.einsum('nij,nie->nje', sdb, dob, preferred_element_type=f32)
        dq = jnp.einsum('nij,njd->nid', dsb, kb, preferred_element_type=f32)
        dk = jnp.einsum('nij,nid->njd', dsb, qb, preferred_element_type=f32)
        dq = dq + jnp.einsum('nie,nde->nid', dob, pb, preferred_element_type=f32) * qdec[None]
        dp = jnp.einsum('nid,nie->nde', qdb, dob, preferred_element_type=f32)
        g_scr[NC - 1] = jnp.zeros((HEAD, HEAD), f32)
        for n in range(NC - 2, -1, -1):
            g_scr[n] = dp[n + 1] + g_scr[n + 1] * cdec
        gb = g_scr[...].astype(bf16)
        dk = dk + jnp.einsum('nje,nde->njd', vb, gb, preferred_element_type=f32) * kdec[None]
        dv = dv + jnp.einsum('njd,nde->nje', kdb, gb, preferred_element_type=f32)
        dq_ref[...] = _rope128_t(dq.reshape(S, HEAD) * (HEAD ** -0.5), cos, sin).astype(bf16)
        dk_ref[...] = _rope128_t(dk.reshape(S, HEAD), cos, sin).astype(bf16)
        dv_ref[...] = dv.reshape(S, HEAD).astype(bf16)

    return pl.pallas_call(
        body, name="ret_bwd", grid=(cfg.H,),
        in_specs=[hs(0), hs(cfg.o_rk), hs(cfg.o_rv), hs(cfg.o_rg), hs(0), gspec, full, full] + cspecs,
        out_specs=(hs(0),) * 4 + (gspec,),
        out_shape=(jax.ShapeDtypeStruct((S, cfg.RW), bf16),) * 4 + (jax.ShapeDtypeStruct((1, cfg.RW), f32),),
        scratch_shapes=[pltpu.VMEM((NC, HEAD, HEAD), f32), pltpu.VMEM((NC, HEAD, HEAD), f32)],
        compiler_params=_cparams(("arbitrary",)),
    )(proj, proj, proj, proj, dy, gn, cos, sin, *consts)


def _expm1(x):
    small = x * (1.0 + x * (0.5 + x * (1.0 / 6.0 + x * (1.0 / 24.0 + x * (1.0 / 120.0)))))
    return jnp.where(jnp.abs(x) < 0.1, small, jnp.exp(x) - 1.0)


def _softplus(z):
    return jnp.maximum(z, 0.0) + jnp.log1p(jnp.exp(-jnp.abs(z)))


def _lru_conv(cfg, x_ref, halo_ref, cw, scr, first):
    TR = cfg.TR
    scr[0:SUBLANES, :] = jnp.where(first, 0.0, halo_ref[...])
    scr[SUBLANES:SUBLANES + TR, :] = x_ref[...]
    xc = scr[pl.ds(SUBLANES - (CONV - 1), TR), :] * cw[0:1, :]
    for j in range(1, CONV):
        xc = xc + scr[pl.ds(SUBLANES - (CONV - 1) + j, TR), :] * cw[j:j + 1, :]
    return xc


def _lru_pre(cfg, xc, wa_ref, wx_ref, ba, bx):
    xb = xc.astype(bf16)
    pa = jnp.concatenate([jnp.dot(xb[:, n * HEAD:(n + 1) * HEAD], wa_ref[n].astype(bf16), preferred_element_type=f32)
                          for n in range(cfg.NB)], axis=1) + ba
    px = jnp.concatenate([jnp.dot(xb[:, n * HEAD:(n + 1) * HEAD], wx_ref[n].astype(bf16), preferred_element_type=f32)
                          for n in range(cfg.NB)], axis=1) + bx
    return pa, px


def _lru_ab(pa, px, xc, lam):
    r, i = _sigmoid(pa), _sigmoid(px)
    log_a = (-LRU_C * r) * _softplus(-lam)
    a = jnp.exp(log_a)
    b = jnp.sqrt(-_expm1(2.0 * log_a)) * (i * xc)
    return a, b


def _lru_halo_specs(cfg, off, W):
    TR, S = cfg.TR, cfg.S
    nb = TR // SUBLANES
    cb = off // W
    main = pl.BlockSpec((TR, W), lambda i: (i, cb))
    prev = pl.BlockSpec((SUBLANES, W), lambda i: (jnp.maximum(i * nb - 1, 0), cb))
    nxt = pl.BlockSpec((SUBLANES, W), lambda i: (jnp.minimum((i + 1) * nb, S // SUBLANES - 1), cb))
    return main, prev, nxt


def _lru_gates(cfg, proj, cw, cb, wa, ba, wx, bx, lam):
    S, W, TR, NB = cfg.S, cfg.LW, cfg.TR, cfg.NB
    assert cfg.o_lx % W == 0
    main, prev, _ = _lru_halo_specs(cfg, cfg.o_lx, W)
    wspec = pl.BlockSpec((NB, HEAD, HEAD), lambda i: (0, 0, 0))

    def body(x_ref, halo_ref, cw_ref, cb_ref, wa_ref, ba_ref, wx_ref, bx_ref, lam_ref, a_ref, b_ref, scr):
        xc = _lru_conv(cfg, x_ref, halo_ref, cw_ref[...], scr, pl.program_id(0) == 0) + cb_ref[...]
        pa, px = _lru_pre(cfg, xc, wa_ref, wx_ref, ba_ref[...], bx_ref[...])
        a, b = _lru_ab(pa, px, xc, lam_ref[...])
        a_ref[...] = a
        b_ref[...] = b

    return pl.pallas_call(
        body, name="lru_gates", grid=(S // TR,),
        in_specs=[main, prev, pl.BlockSpec((CONV, W), lambda i: (0, 0)), _row(W), wspec, _row(W), wspec, _row(W), _row(W)],
        out_specs=(_slab(TR, W, 0),) * 2, out_shape=(jax.ShapeDtypeStruct((S, W), f32),) * 2,
        scratch_shapes=[pltpu.VMEM((TR + SUBLANES, W), f32)],
        compiler_params=_cparams(("parallel",)),
    )(proj, proj, cw, cb, wa, ba, wx, bx, lam)


def _lru_lane_block(cfg):
    return 256 if cfg.LW % 256 == 0 else LANES


def _lru_scan_fwd(cfg, proj, a, b):
    S, W = cfg.S, cfg.LW
    LB = _lru_lane_block(cfg)
    assert cfg.o_lg % LB == 0
    col = lambda off: pl.BlockSpec((S, LB), lambda j, _c=off // LB: (0, _c + j))

    def body(a_ref, b_ref, g_ref, h_ref, y_ref):
        def blk(t, h):
            r0 = pl.multiple_of(t * SUBLANES, SUBLANES)
            at, bt = a_ref[pl.ds(r0, SUBLANES), :], b_ref[pl.ds(r0, SUBLANES), :]
            rows = []
            for j in range(SUBLANES):
                h = at[j:j + 1, :] * h + bt[j:j + 1, :]
                rows.append(h)
            h_ref[pl.ds(r0, SUBLANES), :] = jnp.concatenate(rows, axis=0)
            return h

        lax.fori_loop(0, S // SUBLANES, blk, jnp.zeros((1, LB), f32))
        y_ref[...] = (h_ref[...] * _silu(g_ref[...])).astype(bf16)

    return pl.pallas_call(
        body, name="lru_scan_fwd", grid=(W // LB,),
        in_specs=[col(0), col(0), col(cfg.o_lg)],
        out_specs=(col(0), col(0)),
        out_shape=(jax.ShapeDtypeStruct((S, W), f32), jax.ShapeDtypeStruct((S, W), bf16)),
        compiler_params=_cparams(("parallel",)),
    )(a, b, proj)


def _lru_scan_bwd(cfg, proj, a, h, dy):
    S, W = cfg.S, cfg.LW
    LB = _lru_lane_block(cfg)
    col = lambda off: pl.BlockSpec((S, LB), lambda j, _c=off // LB: (0, _c + j))

    def body(a_ref, h_ref, dy_ref, g_ref, da_ref, db_ref, dg_ref):
        gate, dy = g_ref[...], dy_ref[...]
        dg_ref[...] = (dy * h_ref[...] * _dsilu(gate)).astype(bf16)
        da_ref[...] = dy * _silu(gate)

        def blk(t, carry):
            dh_next, a_next = carry
            r0 = pl.multiple_of((S // SUBLANES - 1 - t) * SUBLANES, SUBLANES)
            at, ct = a_ref[pl.ds(r0, SUBLANES), :], da_ref[pl.ds(r0, SUBLANES), :]
            rows = [None] * SUBLANES
            for j in range(SUBLANES - 1, -1, -1):
                dh_next = ct[j:j + 1, :] + a_next * dh_next
                a_next = at[j:j + 1, :]
                rows[j] = dh_next
            db_ref[pl.ds(r0, SUBLANES), :] = jnp.concatenate(rows, axis=0)
            return dh_next, a_next

        z = jnp.zeros((1, LB), f32)
        lax.fori_loop(0, S // SUBLANES, blk, (z, z))
        row = lax.broadcasted_iota(jnp.int32, (S, LB), 0)
        hprev = jnp.where(row == 0, 0.0, pltpu.roll(h_ref[...], 1, axis=0))
        da_ref[...] = db_ref[...] * hprev

    return pl.pallas_call(
        body, name="lru_scan_bwd", grid=(W // LB,),
        in_specs=[col(0), col(0), col(0), col(cfg.o_lg)],
        out_specs=(col(0),) * 3,
        out_shape=(jax.ShapeDtypeStruct((S, W), f32),) * 2 + (jax.ShapeDtypeStruct((S, W), bf16),),
        compiler_params=_cparams(("parallel",)),
    )(a, h, dy, proj)


def _lru_gates_bwd(cfg, proj, da, db, cw, cb, wa, ba, wx, bx, lam):
    S, W, TR, NB = cfg.S, cfg.LW, cfg.TR, cfg.NB
    main, prev, _ = _lru_halo_specs(cfg, cfg.o_lx, W)
    wspec = pl.BlockSpec((NB, HEAD, HEAD), lambda i: (0, 0, 0))

    def body(x_ref, halo_ref, da_ref, db_ref, cw_ref, cb_ref, wa_ref, ba_ref, wx_ref, bx_ref, lam_ref,
             dxc_ref, dwa_ref, dwx_ref, sum_ref, scr):
        i = pl.program_id(0)
        lam = lam_ref[...]
        xc = _lru_conv(cfg, x_ref, halo_ref, cw_ref[...], scr, i == 0) + cb_ref[...]
        pa, px = _lru_pre(cfg, xc, wa_ref, wx_ref, ba_ref[...], bx_ref[...])
        _, vjp = jax.vjp(_lru_ab, pa, px, xc, lam)
        dpa, dpx, dxc, dlam = vjp((da_ref[...], db_ref[...]))
        xb, dpab, dpxb = xc.astype(bf16), dpa.astype(bf16), dpx.astype(bf16)
        nt = (((1,), (1,)), ((), ()))
        tn = (((0,), (0,)), ((), ()))
        back = []
        dwa, dwx = [], []
        for n in range(NB):
            sl = slice(n * HEAD, (n + 1) * HEAD)
            back.append(lax.dot_general(dpab[:, sl], wa_ref[n].astype(bf16), nt, preferred_element_type=f32)
                        + lax.dot_general(dpxb[:, sl], wx_ref[n].astype(bf16), nt, preferred_element_type=f32))
            dwa.append(lax.dot_general(xb[:, sl], dpab[:, sl], tn, preferred_element_type=f32))
            dwx.append(lax.dot_general(xb[:, sl], dpxb[:, sl], tn, preferred_element_type=f32))
        dxc_ref[...] = dxc + jnp.concatenate(back, axis=1)
        part = jnp.concatenate([jnp.sum(dpa, axis=0, keepdims=True), jnp.sum(dpx, axis=0, keepdims=True), dlam,
                                jnp.zeros((SUBLANES - 3, W), f32)], axis=0)

        @pl.when(i == 0)
        def _():
            sum_ref[...] = part
            for n in range(NB):
                dwa_ref[n] = dwa[n]
                dwx_ref[n] = dwx[n]

        @pl.when(i > 0)
        def _():
            sum_ref[...] += part
            for n in range(NB):
                dwa_ref[n] += dwa[n]
                dwx_ref[n] += dwx[n]

    return pl.pallas_call(
        body, name="lru_gates_bwd", grid=(S // TR,),
        in_specs=[main, prev, _slab(TR, W, 0), _slab(TR, W, 0), pl.BlockSpec((CONV, W), lambda i: (0, 0)), _row(W),
                  wspec, _row(W), wspec, _row(W), _row(W)],
        out_specs=(_slab(TR, W, 0), wspec, wspec, pl.BlockSpec((SUBLANES, W), lambda i: (0, 0))),
        out_shape=(jax.ShapeDtypeStruct((S, W), f32), jax.ShapeDtypeStruct((NB, HEAD, HEAD), f32),
                   jax.ShapeDtypeStruct((NB, HEAD, HEAD), f32), jax.ShapeDtypeStruct((SUBLANES, W), f32)),
        scratch_shapes=[pltpu.VMEM((TR + SUBLANES, W), f32)],
        compiler_params=_cparams(("arbitrary",)),
    )(proj, proj, da, db, cw, cb, wa, ba, wx, bx, lam)


def _lru_conv_bwd(cfg, proj, dxc, cw):
    S, W, TR = cfg.S, cfg.LW, cfg.TR
    main, prev, _ = _lru_halo_specs(cfg, cfg.o_lx, W)
    dmain, _, dnext = _lru_halo_specs(cfg, 0, W)

    def body(x_ref, xhalo_ref, d_ref, dhalo_ref, cw_ref, dx_ref, sum_ref, xs, ds):
        i = pl.program_id(0)
        cw = cw_ref[...]
        d = d_ref[...]
        xs[0:SUBLANES, :] = jnp.where(i == 0, 0.0, xhalo_ref[...])
        xs[SUBLANES:SUBLANES + TR, :] = x_ref[...]
        ds[0:TR, :] = d
        ds[TR:TR + SUBLANES, :] = jnp.where(i == pl.num_programs(0) - 1, 0.0, dhalo_ref[...])
        dx = ds[pl.ds(CONV - 1, TR), :] * cw[0:1, :]
        parts = [jnp.sum(d * xs[pl.ds(SUBLANES - (CONV - 1), TR), :], axis=0, keepdims=True)]
        for j in range(1, CONV):
            dx = dx + ds[pl.ds(CONV - 1 - j, TR), :] * cw[j:j + 1, :]
            parts.append(jnp.sum(d * xs[pl.ds(SUBLANES - (CONV - 1) + j, TR), :], axis=0, keepdims=True))
        dx_ref[...] = dx.astype(bf16)
        part = jnp.concatenate(parts + [jnp.sum(d, axis=0, keepdims=True), jnp.zeros((SUBLANES - CONV - 1, W), f32)], axis=0)

        @pl.when(i == 0)
        def _():
            sum_ref[...] = part

        @pl.when(i > 0)
        def _():
            sum_ref[...] += part

    return pl.pallas_call(
        body, name="lru_conv_bwd", grid=(S // TR,),
        in_specs=[main, prev, dmain, dnext, pl.BlockSpec((CONV, W), lambda i: (0, 0))],
        out_specs=(_slab(TR, W, 0), pl.BlockSpec((SUBLANES, W), lambda i: (0, 0))),
        out_shape=(jax.ShapeDtypeStruct((S, W), bf16), jax.ShapeDtypeStruct((SUBLANES, W), f32)),
        scratch_shapes=[pltpu.VMEM((TR + SUBLANES, W), f32), pltpu.VMEM((TR + SUBLANES, W), f32)],
        compiler_params=_cparams(("arbitrary",)),
    )(proj, proj, dxc, dxc, cw)


def _rms(x, g):
    r = lax.rsqrt(jnp.mean(x * x, axis=-1, keepdims=True) + NORM_EPS)
    return (x * r) * g, r


def _mla_norm(cfg, proj, qg, kg):
    S, TR = cfg.S, cfg.TR

    def body(q_ref, k_ref, qg_ref, kg_ref, qn_ref, kn_ref):
        qn_ref[...] = _rms(q_ref[...], qg_ref[...])[0].astype(bf16)
        kn_ref[...] = _rms(k_ref[...], kg_ref[...])[0].astype(bf16)

    return pl.pallas_call(
        body, name="mla_norm", grid=(S // TR,),
        in_specs=[_slab(TR, cfg.QL, cfg.o_mq), _slab(TR, cfg.KL, cfg.o_mkv), _row(cfg.QL), _row(cfg.KL)],
        out_specs=(_slab(TR, cfg.QL, 0), _slab(TR, cfg.KL, 0)),
        out_shape=(jax.ShapeDtypeStruct((S, cfg.QL), bf16), jax.ShapeDtypeStruct((S, cfg.KL), bf16)),
        compiler_params=_cparams(("parallel",)),
    )(proj, proj, qg, kg)


def _mla_norm_bwd(cfg, proj, dqn, dkn, qg, kg):
    S, TR = cfg.S, cfg.TR

    def one(x, g, dn):
        r = lax.rsqrt(jnp.mean(x * x, axis=-1, keepdims=True) + NORM_EPS)
        xn = x * r
        dxn = dn * g
        dx = r * (dxn - xn * jnp.mean(dxn * xn, axis=-1, keepdims=True))
        return dx, jnp.sum(dn * xn, axis=0, keepdims=True)

    def body(q_ref, k_ref, dq_ref, dk_ref, qg_ref, kg_ref, dmq_ref, dmk_ref, sq_ref, sk_ref):
        i = pl.program_id(0)
        dq, gq = one(q_ref[...], qg_ref[...], dq_ref[...])
        dk, gk = one(k_ref[...], kg_ref[...], dk_ref[...])
        dmq_ref[...] = dq.astype(bf16)
        dmk_ref[...] = dk.astype(bf16)
        pq = jnp.concatenate([gq, jnp.zeros((SUBLANES - 1, cfg.QL), f32)], axis=0)
        pk = jnp.concatenate([gk, jnp.zeros((SUBLANES - 1, cfg.KL), f32)], axis=0)

        @pl.when(i == 0)
        def _():
            sq_ref[...] = pq
            sk_ref[...] = pk

        @pl.when(i > 0)
        def _():
            sq_ref[...] += pq
            sk_ref[...] += pk

    return pl.pallas_call(
        body, name="mla_norm_bwd", grid=(S // TR,),
        in_specs=[_slab(TR, cfg.QL, cfg.o_mq), _slab(TR, cfg.KL, cfg.o_mkv), _slab(TR, cfg.QL, 0), _slab(TR, cfg.KL, 0),
                  _row(cfg.QL), _row(cfg.KL)],
        out_specs=(_slab(TR, cfg.QL, 0), _slab(TR, cfg.KL, 0), pl.BlockSpec((SUBLANES, cfg.QL), lambda i: (0, 0)),
                   pl.BlockSpec((SUBLANES, cfg.KL), lambda i: (0, 0))),
        out_shape=(jax.ShapeDtypeStruct((S, cfg.QL), bf16), jax.ShapeDtypeStruct((S, cfg.KL), bf16),
                   jax.ShapeDtypeStruct((SUBLANES, cfg.QL), f32), jax.ShapeDtypeStruct((SUBLANES, cfg.KL), f32)),
        compiler_params=_cparams(("arbitrary",)),
    )(proj, proj, dqn, dkn, qg, kg)


def _mla_pack(cfg, proj, q, kv, cq, sq, ck, sk):
    S, TR, MH = cfg.S, cfg.TR, cfg.MH
    NW, RWD = MH * HEAD, MH * ROPE

    def body(q_ref, kv_ref, kr_ref, cq_ref, sq_ref, ck_ref, sk_ref, qo_ref, ko_ref, vo_ref):
        q, kv = q_ref[...], kv_ref[...]
        qr = _rope64(q[:, NW:], cq_ref[...], sq_ref[...])
        kr = _rope64(kr_ref[...], ck_ref[...], sk_ref[...]).astype(bf16)
        lane = lax.broadcasted_iota(jnp.int32, (TR, HEAD), 1)
        for h in range(MH):
            grp = qr[:, (h // 2) * HEAD:(h // 2 + 1) * HEAD]
            if h % 2:
                grp = pltpu.roll(grp, 64, axis=1)
            qo_ref[h] = jnp.concatenate([q[:, h * HEAD:(h + 1) * HEAD], jnp.where(lane < ROPE, grp, 0.0)], axis=1).astype(bf16)
            ko_ref[h] = jnp.concatenate([kv[:, 2 * h * HEAD:(2 * h + 1) * HEAD].astype(bf16), kr], axis=1)
            vo_ref[h] = kv[:, (2 * h + 1) * HEAD:(2 * h + 2) * HEAD].astype(bf16)

    hspec = lambda w: pl.BlockSpec((MH, TR, w), lambda i: (0, i, 0))
    return pl.pallas_call(
        body, name="mla_pack", grid=(S // TR,),
        in_specs=[_slab(TR, cfg.QW, 0), _slab(TR, cfg.KVW, 0), _slab(TR, HEAD, cfg.o_mkr),
                  _slab(TR, RWD, 0), _slab(TR, RWD, 0), _slab(TR, HEAD, 0), _slab(TR, HEAD, 0)],
        out_specs=(hspec(2 * HEAD), hspec(2 * HEAD), hspec(HEAD)),
        out_shape=(jax.ShapeDtypeStruct((MH, S, 2 * HEAD), bf16), jax.ShapeDtypeStruct((MH, S, 2 * HEAD), bf16),
                   jax.ShapeDtypeStruct((MH, S, HEAD), bf16)),
        compiler_params=_cparams(("parallel",)),
    )(q, kv, proj, cq, sq, ck, sk)


def _mla_unpack_bwd(cfg, dq3, dk3, dv3, cq, sq, ck, sk):
    S, TR, MH = cfg.S, cfg.TR, cfg.MH
    RWD = MH * ROPE

    def body(dq_ref, dk_ref, dv_ref, cq_ref, sq_ref, ck_ref, sk_ref, q_ref, kv_ref, kr_ref):
        lane = lax.broadcasted_iota(jnp.int32, (TR, HEAD), 1)
        nope, ropes, kvs = [], [], []
        dkr = jnp.zeros((TR, HEAD), f32)
        for h in range(MH):
            dq = dq_ref[h]
            nope.append(dq[:, :HEAD])
            part = jnp.where(lane < ROPE, dq[:, HEAD:], 0.0)
            if h % 2:
                ropes[-1] = ropes[-1] + pltpu.roll(part, 64, axis=1)
            else:
                ropes.append(part)
            dk = dk_ref[h]
            kvs += [dk[:, :HEAD], dv_ref[h]]
            dkr = dkr + dk[:, HEAD:]
        dqr = _rope64_t(jnp.concatenate(ropes, axis=1), cq_ref[...], sq_ref[...])
        q_ref[...] = jnp.concatenate(nope + [dqr], axis=1).astype(bf16)
        kv_ref[...] = jnp.concatenate(kvs, axis=1).astype(bf16)
        dkr = jnp.where(lane < ROPE, dkr, 0.0)
        kr_ref[...] = _rope64_t(dkr, ck_ref[...], sk_ref[...]).astype(bf16)

    hspec = lambda w: pl.BlockSpec((MH, TR, w), lambda i: (0, i, 0))
    return pl.pallas_call(
        body, name="mla_unpack_bwd", grid=(S // TR,),
        in_specs=[hspec(2 * HEAD), hspec(2 * HEAD), hspec(HEAD), _slab(TR, RWD, 0), _slab(TR, RWD, 0),
                  _slab(TR, HEAD, 0), _slab(TR, HEAD, 0)],
        out_specs=(_slab(TR, cfg.QW, 0), _slab(TR, cfg.KVW, 0), _slab(TR, HEAD, 0)),
        out_shape=(jax.ShapeDtypeStruct((S, cfg.QW), bf16), jax.ShapeDtypeStruct((S, cfg.KVW), bf16),
                   jax.ShapeDtypeStruct((S, HEAD), bf16)),
        compiler_params=_cparams(("parallel",)),
    )(dq3, dk3, dv3, cq, sq, ck, sk)


def _mla_probs(cfg, q, k, i):
    TQ, n = cfg.TQ, k.shape[0]
    nt = (((1,), (1,)), ((), ()))
    s = lax.dot_general(q, k, nt, preferred_element_type=f32) * ((HEAD + ROPE) ** -0.5)
    qc = (i * TQ + lax.broadcasted_iota(jnp.int32, (TQ, n), 0)) // CHUNK
    kc = lax.broadcasted_iota(jnp.int32, (TQ, n), 1) // CHUNK
    s = jnp.where(kc <= qc, s, -1e30)
    m = jnp.max(s, axis=-1, keepdims=True)
    e = jnp.exp(s - m)
    return e / jnp.sum(e, axis=-1, keepdims=True)


def _mla_attn_specs(cfg):
    S, TQ = cfg.S, cfg.TQ
    qs = lambda w: pl.BlockSpec((None, TQ, w), lambda h, i: (h, i, 0))
    ks = lambda w: pl.BlockSpec((None, S, w), lambda h, i: (h, 0, 0))
    hs = lambda off: pl.BlockSpec((TQ, HEAD), lambda h, i, _c=off // HEAD: (i, _c + h))
    return qs, ks, hs


def _mla_attn_fwd(cfg, proj, q3, k3, v3):
    S, TQ, MH = cfg.S, cfg.TQ, cfg.MH
    qs, ks, hs = _mla_attn_specs(cfg)

    def body(q_ref, k_ref, v_ref, g_ref, o_ref, y_ref):
        for i in range(S // TQ):
            @pl.when(pl.program_id(1) == i)
            def _(i=i):
                n = (i + 1) * TQ
                p = _mla_probs(cfg, q_ref[...], k_ref[0:n, :], i)
                o = jnp.dot(p.astype(bf16), v_ref[0:n, :], preferred_element_type=f32)
                o_ref[...] = o
                y_ref[...] = (o * _silu(g_ref[...])).astype(bf16)

    return pl.pallas_call(
        body, name="mla_attn_fwd", grid=(MH, S // TQ),
        in_specs=[qs(2 * HEAD), ks(2 * HEAD), ks(HEAD), hs(cfg.o_mg)],
        out_specs=(hs(0), hs(0)),
        out_shape=(jax.ShapeDtypeStruct((S, cfg.MW), f32), jax.ShapeDtypeStruct((S, cfg.MW), bf16)),
        compiler_params=_cparams(("parallel", "parallel")),
    )(q3, k3, v3, proj)


def _mla_attn_bwd(cfg, proj, q3, k3, v3, o, dy):
    S, TQ, MH = cfg.S, cfg.TQ, cfg.MH
    qs, ks, hs = _mla_attn_specs(cfg)

    def body(q_ref, k_ref, v_ref, g_ref, o_ref, dy_ref, dq_ref, dk_ref, dv_ref, dg_ref):
        q = q_ref[...]
        gate, dy, o = g_ref[...], dy_ref[...], o_ref[...]
        dg_ref[...] = (dy * o * _dsilu(gate)).astype(bf16)
        dob = (dy * _silu(gate)).astype(bf16)
        nt = (((1,), (1,)), ((), ()))
        tn = (((0,), (0,)), ((), ()))

        @pl.when(pl.program_id(1) == 0)
        def _():
            dk_ref[...] = jnp.zeros_like(dk_ref)
            dv_ref[...] = jnp.zeros_like(dv_ref)

        for i in range(S // TQ):
            @pl.when(pl.program_id(1) == i)
            def _(i=i):
                n = (i + 1) * TQ
                k, v = k_ref[0:n, :], v_ref[0:n, :]
                p = _mla_probs(cfg, q, k, i)
                dv_ref[0:n, :] += lax.dot_general(p.astype(bf16), dob, tn, preferred_element_type=f32)
                dp = lax.dot_general(dob, v, nt, preferred_element_type=f32)
                ds = (p * (dp - jnp.sum(dp * p, axis=-1, keepdims=True)) * ((HEAD + ROPE) ** -0.5)).astype(bf16)
                dq_ref[...] = jnp.dot(ds, k, preferred_element_type=f32)
                dk_ref[0:n, :] += lax.dot_general(ds, q, tn, preferred_element_type=f32)

    return pl.pallas_call(
        body, name="mla_attn_bwd", grid=(MH, S // TQ),
        in_specs=[qs(2 * HEAD), ks(2 * HEAD), ks(HEAD), hs(cfg.o_mg), hs(0), hs(0)],
        out_specs=(qs(2 * HEAD), ks(2 * HEAD), ks(HEAD), hs(0)),
        out_shape=(jax.ShapeDtypeStruct((MH, S, 2 * HEAD), f32), jax.ShapeDtypeStruct((MH, S, 2 * HEAD), f32),
                   jax.ShapeDtypeStruct((MH, S, HEAD), f32), jax.ShapeDtypeStruct((S, cfg.MW), bf16)),
        compiler_params=_cparams(("parallel", "arbitrary")),
    )(q3, k3, v3, proj, o, dy)


def _pick_rows(R, bytes_per_row):
    if R * bytes_per_row <= MM_BUDGET:
        return R
    best = None
    for t in range(16, R, 16):
        if R % t == 0 and t * bytes_per_row <= MM_BUDGET:
            best = t
    assert best is not None, (R, bytes_per_row)
    return best


def _adamw(w, g, m, v, name="adamw"):
    R, C = w.shape
    tr = _pick_rows(R, C * 4 * 7 * 2)
    c1 =1.0 - ADAM_B1 ** ADAM_STEP
    c2 = 1.0 - ADAM_B2 ** ADAM_STEP

    def body(w_ref, g_ref, m_ref, v_ref, d_ref, mo_ref, vo_ref):
        g = g_ref[...]
        m = ADAM_B1 * m_ref[...] + (1.0 - ADAM_B1) * g
        v = ADAM_B2 * v_ref[...] + (1.0 - ADAM_B2) * jnp.square(g)
        d_ref[...] = -ADAM_LR * ((m / c1) / (jnp.sqrt(v / c2) + ADAM_EPS) + ADAM_WD * w_ref[...])
        mo_ref[...] = m
        vo_ref[...] = v

    spec = pl.BlockSpec((tr, C), lambda i: (i, 0))
    return pl.pallas_call(
        body, name=name, grid=(R // tr,), in_specs=[spec] * 4, out_specs=(spec,) * 3,
        out_shape=(jax.ShapeDtypeStruct((R, C), f32),) * 3,
        compiler_params=_cparams(("parallel",)),
    )(w, g, m, v)


def _adamw_big(w, m, v, l0, mines, others, core, after, name, half_cols=False, prev=None):
    L, R, C = w.shape
    nl = len(mines)
    n_prev = 1 if prev is None else 5
    prev = (after,) + tuple(prev or ())
    hr, hc = (R, C // 2) if half_cols else (R // 2, C)
    tr = _pick_rows(hr, hc * 4 * (7 + 2 * nl) * 2)
    nt = hr // tr
    c1 = 1.0 - ADAM_B1 ** ADAM_STEP
    c2 = 1.0 - ADAM_B2 ** ADAM_STEP

    def body(core_ref, w_ref, m_ref, v_ref, *rest):
        g_refs, (go_ref, d_ref, mo_ref, vo_ref) = rest[:2 * nl], rest[2 * nl + n_prev:]
        l, h = pl.program_id(0), pl.program_id(1)
        own = h == core_ref[0]
        g = jnp.where(own, g_refs[0][...], g_refs[nl][...])
        for k in range(1, nl):
            g = jnp.where(l == k, jnp.where(own, g_refs[k][...], g_refs[nl + k][...]), g)
        m = ADAM_B1 * m_ref[...] + (1.0 - ADAM_B1) * g
        v = ADAM_B2 * v_ref[...] + (1.0 - ADAM_B2) * jnp.square(g)
        go_ref[...] = g
        d_ref[...] = -ADAM_LR * ((m / c1) / (jnp.sqrt(v / c2) + ADAM_EPS) + ADAM_WD * w_ref[...])
        mo_ref[...] = m
        vo_ref[...] = v

    if half_cols:
        lay = pl.BlockSpec((None, tr, hc), lambda l, h, i, core_ref: (l0 + l, i, h))
    else:
        lay = pl.BlockSpec((None, tr, hc), lambda l, h, i, core_ref: (l0 + l, h * nt + i, 0))
    gspec = lambda k: pl.BlockSpec((tr, hc), lambda l, h, i, core_ref: (jnp.where(l == k, i, 0), 0))
    return pl.pallas_call(
        body, name=name,
        grid_spec=pltpu.PrefetchScalarGridSpec(
            num_scalar_prefetch=1, grid=(nl, 2, nt),
            in_specs=[lay, lay, lay] + [gspec(k) for k in range(nl)] * 2 + _hbm_specs(n_prev), out_specs=(lay,) * 4),
        out_shape=(jax.ShapeDtypeStruct((L, R, C), f32),) * 4,
        input_output_aliases={5 + 2 * nl + k: k for k in range(n_prev - 1)},
        compiler_params=_cparams(("arbitrary", "arbitrary", "arbitrary")),
    )(core, w, m, v, *mines, *others, *prev)


def _sum_blocks(x, out_dtype, name):
    n, R, C = x.shape
    tr = _pick_rows(R, C * 4 * (n + 1) * 2)

    def body(x_ref, o_ref):
        acc = x_ref[0].astype(f32)
        for k in range(1, n):
            acc = acc + x_ref[k].astype(f32)
        o_ref[...] = acc.astype(o_ref.dtype)

    return pl.pallas_call(
        body, name=name, grid=(R // tr,),
        in_specs=[pl.BlockSpec((n, tr, C), lambda i: (0, i, 0))], out_specs=pl.BlockSpec((tr, C), lambda i: (i, 0)),
        out_shape=jax.ShapeDtypeStruct((R, C), out_dtype),
        compiler_params=_cparams(("parallel",)),
    )(x)


def _hbm_specs(n):
    return [pl.BlockSpec(memory_space=pl.ANY)] * n


def _row_map(cfg):
    nc, k0 = cfg.IN_WIDTH // 4, cfg.o_mg

    def padded(o):
        return o if o < k0 else (cfg.o_mkr + o - k0 if o < k0 + ROPE else o - ROPE)

    cuts = {0, nc}
    for q in range(4):
        cuts |= {b - q * nc for b in (k0, k0 + ROPE) if q * nc < b < (q + 1) * nc}
    cuts = sorted(cuts)
    return [((l0, l1 - l0), tuple(padded(q * nc + l0) for q in range(4))) for l0, l1 in zip(cuts[:-1], cuts[1:])]


def _chip_start(q, starts):
    st = starts[0]
    for i in range(1, 4):
        st = jnp.where(q == i, starts[i], st)
    return pl.multiple_of(st, 16)


def _allgather8(shards, name, cfg=None, zeros=None):
    na = len(shards)
    rmap = _row_map(cfg) if cfg is not None else []
    npc = max(len(rmap), 1)

    def body(*refs):
        x_refs, out_refs = refs[:na], refs[na + 1:2 * na + 1]
        send_sems, recv_sems, local_sems = refs[2 * na + 1:]
        x, y, c = lax.axis_index("x"), lax.axis_index("y"), lax.axis_index("c")
        me, sibling = (x, y, c), (x, y, 1 - c)
        chips = [(1 - x, y), (x, 1 - y), (1 - x, 1 - y)]

        def wins(a, px, py, pc):
            m, n = shards[a].shape
            if a == 0 and rmap:
                cols = pl.ds(pl.multiple_of(pc * n, n), n)
                return [(pl.ds(l0, cnt), out_refs[0].at[pl.ds(_chip_start(2 * px + py, starts), cnt), cols])
                        for (l0, cnt), starts in rmap]
            return [(pl.ds(0, m), out_refs[a].at[pl.ds((4 * px + 2 * py + pc) * m, m), :])]

        def copies(a, k, block, to, from_x):
            return [pltpu.make_async_remote_copy(
                src_ref=x_refs[a].at[rows, :] if from_x else win, dst_ref=win, send_sem=send_sems.at[a, k, p],
                recv_sem=recv_sems.at[a, k, p], device_id=to, device_id_type=MESH)
                for p, (rows, win) in enumerate(wins(a, *block))]

        mine = [pltpu.make_async_copy(x_refs[a].at[rows, :], win, local_sems.at[a, p])
                for a in range(na) for p, (rows, win) in enumerate(wins(a, *me))]
        if zeros is not None:
            nz = zeros.shape[0]
            mine.append(pltpu.make_async_copy(refs[na], out_refs[0].at[pl.ds(cfg.NP - nz, nz), :], local_sems.at[0, npc]))
        for cp in mine:
            cp.start()
        first = []
        for a in range(na):
            first += copies(a, 0, me, sibling, True)
            for j, chip in enumerate(chips):
                first += copies(a, 1 + j, me, (*chip, c), True)
        for cp in first:
            cp.start()
        passed = []
        for j, chip in enumerate(chips):
            for a in range(na):
                for cp in copies(a, 1 + j, (*chip, c), me, False):
                    cp.wait_recv()
                fwd = copies(a, 4 + j, (*chip, c), sibling, False)
                for cp in fwd:
                    cp.start()
                passed += fwd
        for a in range(na):
            for cp in copies(a, 0, sibling, me, False):
                cp.wait_recv()
        for j, chip in enumerate(chips):
            for a in range(na):
                for cp in copies(a, 4 + j, (*chip, 1 - c), me, False):
                    cp.wait_recv()
        for cp in first + passed:
            cp.wait_send()
        for cp in mine:
            cp.wait()

    out_shape = [jax.ShapeDtypeStruct((N_DEV * s.shape[0], s.shape[1]), s.dtype) for s in shards]
    if rmap:
        out_shape[0] = jax.ShapeDtypeStruct((cfg.NP, cfg.D), shards[0].dtype)
    z = zeros if zeros is not None else jnp.zeros((SUBLANES, LANES), f32)
    return pl.pallas_call(
        body, name=name, out_shape=out_shape,
        in_specs=_hbm_specs(na + 1), out_specs=_hbm_specs(na),
        scratch_shapes=[pltpu.SemaphoreType.DMA((na, 7, npc)), pltpu.SemaphoreType.DMA((na, 7, npc)),
                        pltpu.SemaphoreType.DMA((na, npc + 1))],
    )(*shards, z)


_SEM = pl.BlockSpec(memory_space=pltpu.SEMAPHORE)
_HBM = pl.BlockSpec(memory_space=pltpu.HBM)
_EFFECT = pltpu.SideEffectType.DATAFLOW_SIDE_EFFECTING


def _split_start(srcs, lands, after, plan, n, name):
    bufs = list(srcs) + list(lands)
    nb, ns = len(bufs), len(srcs)

    def body(*refs):
        send_sems, recv_sems = refs[nb + 1], refs[nb + 2]
        for k, (src, dst, _, dev) in enumerate(plan(refs[:ns], refs[ns:nb])):
            pltpu.make_async_remote_copy(src_ref=src, dst_ref=dst, send_sem=send_sems.at[k], recv_sem=recv_sems.at[k],
                                         device_id=dev, device_id_type=MESH).start()
        refs[-1][...] = jnp.zeros_like(refs[-1])

    out = pl.pallas_call(
        body, name=name,
        out_shape=(pltpu.SemaphoreType.DMA((n,)), pltpu.SemaphoreType.DMA((n,)), *[pltpu.HBM(b.shape, b.dtype) for b in bufs],
                   jax.ShapeDtypeStruct((SUBLANES, LANES), f32)),
        in_specs=[_HBM] * nb + [pl.BlockSpec(memory_space=pl.ANY)],
        out_specs=(_SEM, _SEM, *[_HBM] * nb, pl.BlockSpec(memory_space=pltpu.VMEM)),
        input_output_aliases={i: 2 + i for i in range(nb)},
        compiler_params=pltpu.CompilerParams(has_side_effects=_EFFECT),
    )(*[pltpu.with_memory_space_constraint(b, pltpu.HBM) for b in bufs], after)
    return out[0], out[1], list(out[2:2 + ns]), list(out[2 + ns:2 + nb]), out[-1]


def _split_wait(srcs, lands, send_sems, recv_sems, after, plan, name):
    bufs = list(srcs) + list(lands)
    nb, ns = len(bufs), len(srcs)

    def body(*refs):
        send, recv = refs[nb], refs[nb + 1]
        for k, (src, _, dst, dev) in enumerate(plan(refs[:ns], refs[ns:nb])):
            cp = pltpu.make_async_remote_copy(src_ref=src, dst_ref=dst, send_sem=send.at[k], recv_sem=recv.at[k],
                                              device_id=dev, device_id_type=MESH)
            cp.wait_send()
            cp.wait_recv()

    out = pl.pallas_call(
        body, name=name, out_shape=tuple(pltpu.HBM(b.shape, b.dtype) for b in bufs),
        in_specs=[_HBM] * nb + [_SEM, _SEM, pl.BlockSpec(memory_space=pl.ANY)], out_specs=tuple([_HBM] * nb),
        input_output_aliases={i: i for i in range(nb)},
        compiler_params=pltpu.CompilerParams(has_side_effects=_EFFECT),
    )(*bufs, send_sems, recv_sems, after)
    return list(out[:ns]), list(out[ns:])


def _weight_windows(cfg, shards, out_refs, px, py, pc):
    rmap = _row_map(cfg)
    m, n = shards[0].shape
    cols = pl.ds(pl.multiple_of(pc * n, n), n)
    wins = [[(pl.ds(l0, cnt), out_refs[0].at[pl.ds(_chip_start(2 * px + py, starts), cnt), cols]) for (l0, cnt), starts in rmap]]
    for a in range(1, len(shards)):
        m = shards[a].shape[0]
        wins.append([(pl.ds(0, m), out_refs[a].at[pl.ds((4 * px + 2 * py + pc) * m, m), :])])
    return wins


def _gather_shapes(cfg, shards):
    return [jax.ShapeDtypeStruct((cfg.NP, cfg.D), shards[0].dtype)] + \
           [jax.ShapeDtypeStruct((N_DEV * s.shape[0], s.shape[1]), s.dtype) for s in shards[1:]]


def _gather_plan(cfg, shards):
    def plan(x_refs, land_refs):
        x, y, c = lax.axis_index("x"), lax.axis_index("y"), lax.axis_index("c")
        mine = _weight_windows(cfg, shards, land_refs, x, y, c)
        out = []
        for peer in [(x, y, 1 - c), (1 - x, y, c), (x, 1 - y, c), (1 - x, 1 - y, c)]:
            theirs = _weight_windows(cfg, shards, land_refs, *peer)
            for a in range(len(shards)):
                for (rows, win), (_, win_in) in zip(mine[a], theirs[a]):
                    out.append((x_refs[a].at[rows, :], win, win_in, peer))
        return out
    return plan


def _gather_finish(cfg, shards, lands, name):
    na = len(shards)
    rmap = _row_map(cfg)
    npc = len(rmap)
    nz = cfg.NP - cfg.o_mkr - ROPE

    def body(*refs):
        x_refs, out_refs = refs[:na], refs[2 * na:3 * na]
        stage, zbuf = refs[3 * na:4 * na], refs[4 * na]
        send_sems, recv_sems, local_sems = refs[4 * na + 1:]
        x, y, c = lax.axis_index("x"), lax.axis_index("y"), lax.axis_index("c")
        sibling = (x, y, 1 - c)
        chips = [(1 - x, y), (x, 1 - y), (1 - x, 1 - y)]
        load = [pltpu.make_async_copy(x_refs[a], stage[a], local_sems.at[a, npc]) for a in range(na)]
        for cp in load:
            cp.start()
        passed = []
        for j, chip in enumerate(chips):
            wins = _weight_windows(cfg, shards, out_refs, *chip, c)
            for a in range(na):
                passed += [pltpu.make_async_remote_copy(src_ref=win, dst_ref=win, send_sem=send_sems.at[a, j, p],
                                                        recv_sem=recv_sems.at[a, j, p], device_id=sibling, device_id_type=MESH)
                           for p, (_, win) in enumerate(wins[a])]
        for cp in passed:
            cp.start()
        zbuf[...] = jnp.zeros_like(zbuf)
        for cp in load:
            cp.wait()
        own = _weight_windows(cfg, shards, out_refs, x, y, c)
        store = [pltpu.make_async_copy(stage[a].at[rows, :], win, local_sems.at[a, p])
                 for a in range(na) for p, (rows, win) in enumerate(own[a])]
        store.append(pltpu.make_async_copy(zbuf, out_refs[0].at[pl.ds(cfg.NP - nz, nz), :], local_sems.at[0, npc + 1]))
        for cp in store:
            cp.start()
        for j, chip in enumerate(chips):
            wins = _weight_windows(cfg, shards, out_refs, *chip, 1 - c)
            for a in range(na):
                for p, (_, win) in enumerate(wins[a]):
                    pltpu.make_async_remote_copy(src_ref=win, dst_ref=win, send_sem=send_sems.at[a, j, p],
                                                 recv_sem=recv_sems.at[a, j, p], device_id=sibling,
                                                 device_id_type=MESH).wait_recv()
        for cp in passed:
            cp.wait_send()
        for cp in store:
            cp.wait()

    return pl.pallas_call(
        body, name=name, out_shape=_gather_shapes(cfg, shards),
        in_specs=_hbm_specs(2 * na), out_specs=_hbm_specs(na),
        input_output_aliases={na + a: a for a in range(na)},
        scratch_shapes=[pltpu.VMEM(s.shape, s.dtype) for s in shards] + [pltpu.VMEM((nz, cfg.D), shards[0].dtype)]
        + [pltpu.SemaphoreType.DMA((na, 3, npc)), pltpu.SemaphoreType.DMA((na, 3, npc)), pltpu.SemaphoreType.DMA((na, npc + 2))],
        compiler_params=pltpu.CompilerParams(vmem_limit_bytes=VMEM_LIMIT),
    )(*shards, *lands)


def _gather_weights_start(cfg, shards, after):
    lands = [lax.empty(s.shape, s.dtype) for s in _gather_shapes(cfg, shards)]
    n = 4 * (len(_row_map(cfg)) + len(shards) - 1)
    return _split_start(shards, lands, after, _gather_plan(cfg, shards), n, "gather_w_start")


def _gather_weights_end(cfg, shards, started, after):
    send_sems, recv_sems, srcs, lands, _ = started
    srcs, lands = _split_wait(srcs, lands, send_sems, recv_sems, after, _gather_plan(cfg, shards), "gather_w_wait")
    return _gather_finish(cfg, srcs, lands, "gather_w_finish")


def _send_sibling(arrays, name):
    na = len(arrays)

    def body(*refs):
        x_refs, out_refs = refs[:na], refs[na:2 * na]
        send_sems, recv_sems = refs[2 * na:]
        sibling = (lax.axis_index("x"), lax.axis_index("y"), 1 - lax.axis_index("c"))
        cps = [pltpu.make_async_remote_copy(src_ref=x_refs[a], dst_ref=out_refs[a], send_sem=send_sems.at[a],
                                            recv_sem=recv_sems.at[a], device_id=sibling, device_id_type=MESH)
               for a in range(na)]
        for cp in cps:
            cp.start()
        for cp in cps:
            cp.wait()

    return pl.pallas_call(
        body, name=name, out_shape=[jax.ShapeDtypeStruct(x.shape, x.dtype) for x in arrays],
        in_specs=_hbm_specs(na), out_specs=_hbm_specs(na),
        scratch_shapes=[pltpu.SemaphoreType.DMA((na,)), pltpu.SemaphoreType.DMA((na,))],
    )(*arrays)


def _slot_pairs(cfg, p_refs, slot_refs, a, to_chip, slot):
    if a == 0:
        return [(p_refs[0].at[pl.ds(_chip_start(to_chip, starts), cnt), :], slot_refs[0].at[slot, pl.ds(l0, cnt), :])
                for (l0, cnt), starts in _row_map(cfg)]
    return [(p_refs[a].at[to_chip], slot_refs[a].at[slot])]


def _scatter_plan(cfg, na):
    def plan(p_refs, slot_refs):
        x, y, c = lax.axis_index("x"), lax.axis_index("y"), lax.axis_index("c")
        mychip = 2 * x + y
        out = []
        for cx, cy in [(1 - x, y), (x, 1 - y), (1 - x, 1 - y)]:
            q = 2 * cx + cy
            for a in range(na):
                for (src, dst), (_, dst_in) in zip(_slot_pairs(cfg, p_refs, slot_refs, a, q, mychip),
                                                   _slot_pairs(cfg, p_refs, slot_refs, a, mychip, q)):
                    out.append((src, dst, dst_in, (cx, cy, c)))
        return out
    return plan


def _slot_shapes(cfg, parts):
    return [jax.ShapeDtypeStruct((4, cfg.IN_WIDTH // 4, parts[0].shape[1]), parts[0].dtype)] + \
           [jax.ShapeDtypeStruct(p.shape, p.dtype) for p in parts[1:]]


def _place_own(cfg, parts, slots, name):
    na = len(parts)
    npc = len(_row_map(cfg))
    shapes = _slot_shapes(cfg, parts)

    def body(*refs):
        p_refs, out_refs = refs[:na], refs[2 * na:3 * na]
        stage, sems = refs[3 * na:4 * na], refs[4 * na]
        mychip = 2 * lax.axis_index("x") + lax.axis_index("y")
        moves = []
        for a in range(na):
            for p, (src, dst) in enumerate(_slot_pairs(cfg, p_refs, out_refs, a, mychip, mychip)):
                buf = stage[a].at[pl.ds(*_row_map(cfg)[p][0]), :] if a == 0 else stage[a]
                moves.append((pltpu.make_async_copy(src, buf, sems.at[a, p]), pltpu.make_async_copy(buf, dst, sems.at[a, npc + p])))
        for load, _ in moves:
            load.start()
        for load, store in moves:
            load.wait()
            store.start()
        for _, store in moves:
            store.wait()

    return pl.pallas_call(
        body, name=name, out_shape=shapes, in_specs=_hbm_specs(2 * na), out_specs=_hbm_specs(na),
        input_output_aliases={na + a: a for a in range(na)},
        scratch_shapes=[pltpu.VMEM(s.shape[1:], s.dtype) for s in shapes] + [pltpu.SemaphoreType.DMA((na, 2 * npc))],
        compiler_params=pltpu.CompilerParams(vmem_limit_bytes=VMEM_LIMIT),
    )(*parts, *slots)


def _pair_exchange(cfg, g_in_t, grads, name):
    na = 1 + len(grads)
    hd = cfg.D // 2

    def body(*refs):
        g_refs, out_refs = refs[:na], refs[na:2 * na]
        send_sems, recv_sems = refs[2 * na:]
        x, y, c = lax.axis_index("x"), lax.axis_index("y"), lax.axis_index("c")
        cps = [pltpu.make_async_remote_copy(
            src_ref=g_refs[0].at[:, pl.ds(pl.multiple_of((1 - c) * hd, hd), hd)], dst_ref=out_refs[0],
            send_sem=send_sems.at[0, 0], recv_sem=recv_sems.at[0, 0], device_id=(x, y, 1 - c), device_id_type=MESH)]
        for a in range(1, na):
            cps += [pltpu.make_async_remote_copy(src_ref=g_refs[a].at[q, 1 - c], dst_ref=out_refs[a].at[q],
                                                 send_sem=send_sems.at[a, q], recv_sem=recv_sems.at[a, q],
                                                 device_id=(x, y, 1 - c), device_id_type=MESH) for q in range(4)]
        for cp in cps:
            cp.start()
        for cp in cps:
            cp.wait()

    out_shape = [jax.ShapeDtypeStruct((cfg.NP, hd), g_in_t.dtype)] + \
                [jax.ShapeDtypeStruct((4,) + g.shape[2:], g.dtype) for g in grads]
    return pl.pallas_call(
        body, name=name, out_shape=out_shape, in_specs=_hbm_specs(na), out_specs=_hbm_specs(na),
        scratch_shapes=[pltpu.SemaphoreType.DMA((na, 4)), pltpu.SemaphoreType.DMA((na, 4))],
    )(g_in_t, *grads)


def _add_half(g, got, core, name):
    if g.ndim == 2:
        R, hd = got.shape
        tr = _pick_rows(R, hd * 4 * 3 * 2)
        grid = (R // tr,)
        g_spec = pl.BlockSpec((tr, hd), lambda i, core_ref: (i, core_ref[0]))
        o_spec = pl.BlockSpec((tr, hd), lambda i, core_ref: (i, 0))
    else:
        _, hr, nc = got.shape
        tr = _pick_rows(hr, nc * 4 * 3 * 2)
        grid = (4, hr // tr)
        g_spec = pl.BlockSpec((None, None, tr, nc), lambda q, i, core_ref: (q, core_ref[0], i, 0))
        o_spec = pl.BlockSpec((None, tr, nc), lambda q, i, core_ref: (q, i, 0))

    def body(core_ref, g_ref, got_ref, o_ref):
        o_ref[...] = (g_ref[...].astype(f32) + got_ref[...].astype(f32)).astype(o_ref.dtype)

    return pl.pallas_call(
        body, name=name,
        grid_spec=pltpu.PrefetchScalarGridSpec(num_scalar_prefetch=1, grid=grid, in_specs=[g_spec, o_spec], out_specs=o_spec),
        out_shape=jax.ShapeDtypeStruct(got.shape, bf16),
        compiler_params=_cparams(("arbitrary",) * len(grid)),
    )(core, g, got)


def _reduce_scatter_start(cfg, g_in_t, grads):
    core = lax.axis_index("c").astype(jnp.int32).reshape(1)
    got = _pair_exchange(cfg, g_in_t, grads, "rs_pair")
    part = [_add_half(g, h, core, "rs_add_pair") for g, h in zip([g_in_t] + list(grads), got)]
    slots = [lax.empty(s.shape, s.dtype) for s in _slot_shapes(cfg, part)]
    n = 3 * (len(_row_map(cfg)) + len(part) - 1)
    return _split_start(part, slots, part[-1], _scatter_plan(cfg, len(part)), n, "rs_chips_start")


def _reduce_scatter_end(cfg, started, after):
    send_sems, recv_sems, parts, slots, _ = started
    parts, slots = _split_wait(parts, slots, send_sems, recv_sems, after, _scatter_plan(cfg, len(parts)), "rs_chips_wait")
    slots = _place_own(cfg, parts, slots, "rs_own")
    mine = [_sum_blocks(s, f32, "rs_add_chips") for s in slots]
    return mine, _send_sibling(mine, "rs_halves")


def _big_weights(cfg):
    return (("mla_w_uq", cfg.QL, cfg.QW, 1), ("mla_w_ukv", cfg.KL, cfg.KVW, 1),
            ("w_branch", cfg.RW + cfg.LW + cfg.MW, cfg.D, 0), ("w_out", cfg.D, cfg.D, 0))


def _half_shapes(cfg):
    out = []
    for _, r, c, ax in _big_weights(cfg):
        out.append((r // 2, c // 4) if ax == 1 else (r // 8, c))
    return out


def _my_halves(cfg, W, l, c):
    hd = cfg.D // 2
    out = [lax.dynamic_slice_in_dim(W["w_in"][l].T, c * hd, hd, axis=1).astype(bf16)]
    for (name, *_), (hr, nc) in zip(_big_weights(cfg), _half_shapes(cfg)):
        out.append(lax.dynamic_slice_in_dim(W[name][l], c * hr, hr, axis=0).astype(bf16))
    return out


def _uq_split(cfg, w):
    hw = HEAD + ROPE
    return jnp.concatenate([w[:, h * hw:h * hw + HEAD] for h in range(cfg.MH)]
                           + [w[:, h * hw + HEAD:(h + 1) * hw] for h in range(cfg.MH)], axis=1)


def _uq_join(cfg, g):
    n = cfg.MH * HEAD
    parts = []
    for h in range(cfg.MH):
        parts += [g[:, h * HEAD:(h + 1) * HEAD], g[:, n + h * ROPE:n + (h + 1) * ROPE]]
    return jnp.concatenate(parts, axis=1)


def _col_blocks(g):
    nc = g.shape[1] // 4
    return jnp.stack([g[:, q * nc:(q + 1) * nc] for q in range(4)])


def _row_pack(parts):
    rows = []
    for p in parts:
        r = p.reshape(-1, LANES)
        pad = -r.shape[0] % SUBLANES
        rows.append(jnp.concatenate([r, jnp.zeros((pad, LANES), r.dtype)], axis=0) if pad else r)
    return jnp.concatenate(rows, axis=0)


def _row_unpack(packed, like):
    out, off = [], 0
    for p in like:
        n = p.size // LANES
        out.append(packed[off:off + n].reshape(p.shape))
        off += -(-n // SUBLANES) * SUBLANES
    return out


def _prep_layer(cfg, full, small):
    w_in_t, w_uq, w_ukv, w_branch, w_out = full
    RW, LW = cfg.RW, cfg.LW
    P = dict(small)
    P["w_in_t"] = w_in_t
    P["w_uq"] = _uq_split(cfg, jnp.concatenate(list(w_uq.reshape(4, cfg.QL, -1)), axis=1))
    P["w_ukv"] = jnp.concatenate(list(w_ukv.reshape(4, cfg.KL, -1)), axis=1)
    P["wb"] = (w_branch[:RW], w_branch[RW:RW + LW], w_branch[RW + LW:])
    P["w_out"] = w_out
    return P


def _layer_fwd(cfg, x, mod, P, T):
    h = _prenorm_fwd(cfg, x, mod, P["norm_pre"])
    proj = _mm(h, P["w_in_t"], f32, "mm_proj", mode="nt")
    y_ret = _ret_fwd(cfg, proj, P["ret_gn"], T["cos_r"], T["sin_r"], T["ret_consts"])
    a, b = _lru_gates(cfg, proj, P["lru_conv_w"], P["lru_conv_b"], P["lru_wa"], P["lru_ba"], P["lru_wx"], P["lru_bx"],
                      P["lru_lambda"])
    hl, y_lru = _lru_scan_fwd(cfg, proj, a, b)
    qn, kn = _mla_norm(cfg, proj, P["mla_q_norm"], P["mla_kv_norm"])
    q = _mm(qn, P["w_uq"], f32, "mm_uq")
    kv = _mm(kn, P["w_ukv"], f32, "mm_ukv")
    q3, k3, v3 = _mla_pack(cfg, proj, q, kv, T["cos_q"], T["sin_q"], T["cos_k"], T["sin_k"])
    o, y_mla = _mla_attn_fwd(cfg, proj, q3, k3, v3)
    ys = (y_ret, y_lru, y_mla)
    us = tuple(_mm(yb, wb, f32, "mm_branch") for yb, wb in zip(ys, P["wb"]))
    merged = _merge_fwd(cfg, proj, *us)
    y = _mm(merged, P["w_out"], f32, "mm_out")
    out = _postnorm_fwd(cfg, x, y, mod, P["norm_post"])
    R = dict(x=x, h=h, proj=proj, ys=ys, a=a, hl=hl, qn=qn, kn=kn, q3=q3, k3=k3, v3=v3, o=o, us=us, merged=merged, y=y)
    return out, R


def _layer_bwd(cfg, dout, R, mod, P, T):
    proj = R["proj"]
    dy, s_post = _postnorm_bwd(cfg, dout, R["y"], mod, P["norm_post"])
    dmerged = _mm(dy, P["w_out"], f32, "mm_dmerged", mode="nt")
    g_out = _mm(R["merged"], dy, bf16, "mm_gw_out", mode="tn")
    du0, du1, du2, dlog = _merge_bwd(cfg, proj, dmerged, *R["us"])
    dus = (du0, du1, du2)
    dys = tuple(_mm(du, wb, f32, "mm_dbranch", mode="nt") for du, wb in zip(dus, P["wb"]))
    g_branch = jnp.concatenate([_mm(yb, du, bf16, "mm_gw_branch", mode="tn") for yb, du in zip(R["ys"], dus)], axis=0)
    drq, drk, drv, drg, dgn = _ret_bwd(cfg, proj, dys[0], P["ret_gn"], T["cos_r"], T["sin_r"], T["ret_consts"])
    da, db, dlg = _lru_scan_bwd(cfg, proj, R["a"], R["hl"], dys[1])
    dxc, dwa, dwx, s_lru = _lru_gates_bwd(cfg, proj, da, db, P["lru_conv_w"], P["lru_conv_b"], P["lru_wa"], P["lru_ba"],
                                          P["lru_wx"], P["lru_bx"], P["lru_lambda"])
    dlx, s_conv = _lru_conv_bwd(cfg, proj, dxc, P["lru_conv_w"])
    dq3, dk3, dv3, dmg = _mla_attn_bwd(cfg, proj, R["q3"], R["k3"], R["v3"], R["o"], dys[2])
    dq, dkv, dmkr = _mla_unpack_bwd(cfg, dq3, dk3, dv3, T["cos_q"], T["sin_q"], T["cos_k"], T["sin_k"])
    dqn = _mm(dq, P["w_uq"], f32, "mm_dqn", mode="nt")
    dkn = _mm(dkv, P["w_ukv"], f32, "mm_dkn", mode="nt")
    g_uq = _uq_join(cfg, _mm(R["qn"], dq, bf16, "mm_gw_uq", mode="tn"))
    g_ukv = _mm(R["kn"], dkv, bf16, "mm_gw_ukv", mode="tn")
    dmq, dmkv, s_q, s_k = _mla_norm_bwd(cfg, proj, dqn, dkn, P["mla_q_norm"], P["mla_kv_norm"])
    dproj = jnp.concatenate([drq, drk, drv, drg, dlx, dlg, dmq, dmkv, dmg, dlog, dmkr,
                             jnp.zeros((cfg.S, cfg.NP - cfg.o_mkr - HEAD), bf16)], axis=1)
    dh = _mm(dproj, P["w_in_t"], f32, "mm_dh")
    g_in_t = _mm(dproj, R["h"], bf16, "mm_gw_in", mode="tn", tm=512)
    dx, s_pre = _prenorm_bwd(cfg, R["x"], dh, dout, mod, P["norm_pre"])
    big = [_col_blocks(g_uq), _col_blocks(g_ukv), g_branch, g_out]
    big = (g_in_t, [g.reshape(4, 2, hr, nc) for g, (hr, nc) in zip(big, _half_shapes(cfg))])
    small = dict(norm_pre=s_pre[2:3], norm_post=s_post[1:2], ret_gn=dgn, lru_conv_w=s_conv[0:CONV], lru_conv_b=s_conv[CONV:CONV + 1],
                 lru_wa=dwa, lru_ba=s_lru[0:1], lru_wx=dwx, lru_bx=s_lru[1:2], lru_lambda=s_lru[2:3],
                 mla_q_norm=s_q[0:1], mla_kv_norm=s_k[0:1])
    dmod = jnp.concatenate([s_pre[0:1], s_pre[1:2], s_post[0:1]], axis=1)
    return dx, big, small, dmod


_SMALL = ("norm_pre", "norm_post", "ret_gn", "lru_conv_w", "lru_conv_b", "lru_wa", "lru_ba", "lru_wx", "lru_bx", "lru_lambda",
          "mla_q_norm", "mla_kv_norm")
_WEIGHTS = ("ada_w", "ada_b", "norm_pre", "norm_post", "w_in", "ret_gn", "lru_conv_w", "lru_conv_b", "lru_wa", "lru_ba", "lru_wx",
            "lru_bx", "lru_lambda", "mla_q_norm", "mla_w_uq", "mla_kv_norm", "mla_w_ukv", "w_branch", "w_out")


def _step(cfg, x, c, positions, W, target, M1, V1):
    L, D = cfg.L, cfg.D
    xi, yi, ci = lax.axis_index("x"), lax.axis_index("y"), lax.axis_index("c")
    chip = 2 * xi + yi
    me = 2 * chip + ci

    c8 = jnp.concatenate([c, jnp.zeros((SUBLANES - 1, D), f32)], axis=0)
    c_all = _allgather8([c8], "gather_c")[0].reshape(N_DEV, SUBLANES, D)[:, 0]
    mod_sh, c_act = _ada_fwd(cfg, c_all, W["ada_w"])
    n_sh = mod_sh.shape[2]
    mod_half = lax.dynamic_slice_in_dim(mod_sh, ci * (n_sh // 2), n_sh // 2, axis=2).reshape(L * N_DEV, n_sh // 2)
    mod_all = _allgather8([mod_half], "gather_mod")[0].reshape(N_DEV, L, N_DEV, n_sh // 2)
    mod_all = mod_all.transpose(1, 2, 0, 3).reshape(L, N_DEV, 3 * D)
    mods = lax.dynamic_index_in_dim(mod_all, me, axis=1, keepdims=False) + W["ada_b"]

    (cos_r, sin_r), (cos_m, sin_m) = _rope_tables(cfg, positions)
    T = dict(cos_r=cos_r, sin_r=sin_r, cos_q=jnp.tile(cos_m, (1, cfg.MH)), sin_q=jnp.tile(sin_m, (1, cfg.MH)),
             cos_k=jnp.tile(cos_m, (1, 2)), sin_k=jnp.tile(sin_m, (1, 2)), ret_consts=_ret_consts(cfg))

    Ps, Rs = [], []
    act = x[0]
    started = _gather_weights_start(cfg, _my_halves(cfg, W, 0, ci), c)
    cw_all = _allgather8([_pad_rows(W["lru_conv_w"].reshape(L * CONV, -1))], "gather_conv")[0]
    cw_rows = cw_all.shape[0] // N_DEV
    cw_all = cw_all.reshape(4, 2, cw_rows, -1)[:, 0, :L * CONV].transpose(1, 0, 2).reshape(L, CONV, cfg.LW)
    after = mods
    for l in range(L):
        gathered = _gather_weights_end(cfg, started[2], started, after)
        small = {k: (W[k][l] if W[k][l].ndim > 1 else W[k][l][None, :]) for k in _SMALL if k != "lru_conv_w"}
        P = _prep_layer(cfg, gathered, small)
        P["lru_conv_w"] = cw_all[l]
        Ps.append(P)
        mod = mods[l:l + 1]
        if l + 1 < L:
            started = _gather_weights_start(cfg, _my_halves(cfg, W, l + 1, ci), gathered[-1])
            mod = mod + started[4][0, 0]
        act, R = _layer_fwd(cfg, act, mod, P, T)
        Rs.append(R)
        after = act

    dact, lsum = _loss_head(cfg, act, target[0])
    loss = lax.psum(lsum[0, 0], ("x", "y", "c"))

    big_g = [None] * L
    small_g = [None] * L
    dmods = [None] * L
    pending = None
    for l in range(L - 1, -1, -1):
        mod = mods[l:l + 1]
        if pending is not None:
            mod = mod + pending[4][0, 0]
        dact, grads, small_g[l], dmods[l] = _layer_bwd(cfg, dact, Rs[l], mod, Ps[l], T)
        if pending is not None:
            big_g[l + 1] = _reduce_scatter_end(cfg, pending, dact)
        pending = _reduce_scatter_start(cfg, *grads)

    tok = pending[4][0, 0]
    dmod = jnp.concatenate(dmods, axis=0) + tok
    parts = [dmod] + [small_g[l][k] for l in range(L) for k in _SMALL]
    packed = _row_pack(parts)
    allf = _allgather8([packed], "gather_small")[0].reshape(N_DEV, packed.shape[0], LANES)
    summed = _row_unpack(_sum_blocks(allf, f32, "sum_small"), parts)
    gsm = {k: jnp.stack([summed[1 + l * len(_SMALL) + i].reshape(W[k].shape[1:] if k != "lru_conv_w" else (CONV, cfg.LW))
                         for l in range(L)]) for i, k in enumerate(_SMALL)}
    ncw = cfg.LW // 4
    gsm["lru_conv_w"] = lax.dynamic_slice_in_dim(gsm["lru_conv_w"], chip * ncw, ncw, axis=2)
    gsm["ada_b"] = summed[0]
    dmod_all = allf[:, :dmod.size // LANES].reshape(N_DEV, L, 3 * D)
    dmod_sh = lax.dynamic_slice_in_dim(dmod_all, chip * n_sh, n_sh, axis=2).transpose(1, 0, 2)
    G = dict(gsm)
    G["ada_w"] = _ada_bwd(cfg, c_act.T, dmod_sh)
    delta, new_m, new_v = {}, {}, {}
    bigs = ("ada_w", "w_in") + tuple(name for name, *_ in _big_weights(cfg))
    shp = W["ada_w"].shape
    two = lambda a: a.reshape(-1, shp[-1])
    d, m_, v_ = _adamw(two(W["ada_w"]), two(G["ada_w"]), two(M1["ada_w"]), two(V1["ada_w"]), "adamw_ada_w")
    delta["ada_w"], new_m["ada_w"], new_v["ada_w"] = d.reshape(shp), m_.reshape(shp), v_.reshape(shp)
    smalls = [k for k in _WEIGHTS if k not in bigs]
    packs = [_row_pack([src[k] for k in smalls]) for src in (W, G, M1, V1)]
    outs = _adamw(*packs, "adamw_small")
    for dst, o in zip((delta, new_m, new_v), outs):
        for k, val in zip(smalls, _row_unpack(o, [W[k] for k in smalls])):
            dst[k] = val
    core = ci.astype(jnp.int32).reshape(1)
    tr_ = lambda a: a.transpose(0, 2, 1)

    def update(l0, l1, prev):
        res = {}
        for i, (name, *_) in enumerate(_big_weights(cfg)):
            res[name] = _adamw_big(W[name], M1[name], V1[name], l0, [big_g[l][0][i + 1] for l in range(l0, l1)],
                                   [big_g[l][1][i + 1] for l in range(l0, l1)], core, pending[4], "adamw_" + name,
                                   prev=prev and prev[name])
        res["w_in"] = _adamw_big(tr_(W["w_in"]), tr_(M1["w_in"]), tr_(V1["w_in"]), l0, [big_g[l][0][0] for l in range(l0, l1)],
                                 [big_g[l][1][0] for l in range(l0, l1)], core, pending[4], "adamw_w_in", half_cols=True,
                                 prev=prev and prev["w_in"])
        return res

    upper = update(1, L, None) if L > 1 else None
    big_g[0] = _reduce_scatter_end(cfg, pending, upper["w_in"][0] if upper else outs[0])
    res = update(0, 1, upper)
    for name, *_ in _big_weights(cfg):
        G[name], delta[name], new_m[name], new_v[name] = res[name]
    G["w_in"], delta["w_in"], new_m["w_in"], new_v["w_in"] = [tr_(o) for o in res["w_in"]]

    grad_x = dact[None]
    return (loss, grad_x, *[G[k] for k in _WEIGHTS], *[delta[k] for k in _WEIGHTS], *[new_m[k] for k in _WEIGHTS],
            *[new_v[k] for k in _WEIGHTS])


def _pad_rows(a):
    pad = -a.shape[0] % SUBLANES
    return jnp.concatenate([a, jnp.zeros((pad, a.shape[1]), a.dtype)], axis=0) if pad else a


def kernel(x, c, positions, ada_w, ada_b, norm_pre, norm_post, w_in, ret_gn, lru_conv_w, lru_conv_b, lru_wa, lru_ba, lru_wx, lru_bx, lru_lambda, mla_q_norm, mla_w_uq, mla_kv_norm, mla_w_ukv, w_branch, w_out, loss_target, m_ada_w, m_ada_b, m_norm_pre, m_norm_post, m_w_in, m_ret_gn, m_lru_conv_w, m_lru_conv_b, m_lru_wa, m_lru_ba, m_lru_wx, m_lru_bx, m_lru_lambda, m_mla_q_norm, m_mla_w_uq, m_mla_kv_norm, m_mla_w_ukv, m_w_branch, m_w_out, v_ada_w, v_ada_b, v_norm_pre, v_norm_post, v_w_in, v_ret_gn, v_lru_conv_w, v_lru_conv_b, v_lru_wa, v_lru_ba, v_lru_wx, v_lru_bx, v_lru_lambda, v_mla_q_norm, v_mla_w_uq, v_mla_kv_norm, v_mla_w_ukv, v_w_branch, v_w_out):
    W = dict(ada_w=ada_w, ada_b=ada_b, norm_pre=norm_pre, norm_post=norm_post, w_in=w_in, ret_gn=ret_gn, lru_conv_w=lru_conv_w,
             lru_conv_b=lru_conv_b, lru_wa=lru_wa, lru_ba=lru_ba, lru_wx=lru_wx, lru_bx=lru_bx, lru_lambda=lru_lambda,
             mla_q_norm=mla_q_norm, mla_w_uq=mla_w_uq, mla_kv_norm=mla_kv_norm, mla_w_ukv=mla_w_ukv, w_branch=w_branch, w_out=w_out)
    M1 = dict(ada_w=m_ada_w, ada_b=m_ada_b, norm_pre=m_norm_pre, norm_post=m_norm_post, w_in=m_w_in, ret_gn=m_ret_gn,
              lru_conv_w=m_lru_conv_w, lru_conv_b=m_lru_conv_b, lru_wa=m_lru_wa, lru_ba=m_lru_ba, lru_wx=m_lru_wx, lru_bx=m_lru_bx,
              lru_lambda=m_lru_lambda, mla_q_norm=m_mla_q_norm, mla_w_uq=m_mla_w_uq, mla_kv_norm=m_mla_kv_norm,
              mla_w_ukv=m_mla_w_ukv, w_branch=m_w_branch, w_out=m_w_out)
    V1 = dict(ada_w=v_ada_w, ada_b=v_ada_b, norm_pre=v_norm_pre, norm_post=v_norm_post, w_in=v_w_in, ret_gn=v_ret_gn,
              lru_conv_w=v_lru_conv_w, lru_conv_b=v_lru_conv_b, lru_wa=v_lru_wa, lru_ba=v_lru_ba, lru_wx=v_lru_wx, lru_bx=v_lru_bx,
              lru_lambda=v_lru_lambda, mla_q_norm=v_mla_q_norm, mla_w_uq=v_mla_w_uq, mla_kv_norm=v_mla_kv_norm,
              mla_w_ukv=v_mla_w_ukv, w_branch=v_w_branch, w_out=v_w_out)
    return _step(_CFG, x, c, positions, W, loss_target, M1, V1)
```

```python
import functools
import math
from typing import NamedTuple

import numpy as np
import jax
import jax.numpy as jnp
from jax import lax
from jax.experimental import pallas as pl
from jax.experimental.pallas import tpu as pltpu

f32 = jnp.float32
bf16 = jnp.bfloat16

NORM_EPS = 1e-6
ROPE_BASE = 10000.0
CHUNK = 64
HEAD = 128
ROPE = 64
CONV = 4
LRU_C = 8.0
ADAM_LR, ADAM_B1, ADAM_B2, ADAM_EPS, ADAM_WD, ADAM_STEP = 0.001, 0.9, 0.999, 1e-08, 0.01, 10

LANES = 128
SUBLANES = 8
VMEM_LIMIT = 56 * 1024 * 1024
MM_BUDGET = 40 * 1024 * 1024
N_DEV = 8
MESH = pl.DeviceIdType.MESH


class Cfg(NamedTuple):
    D: int = 2048
    S: int = 2048
    L: int = 4
    H: int = 8
    NB: int = 8
    MH: int = 8
    QL: int = 512
    KL: int = 512
    TR: int = 256
    TQ: int = 256

    @property
    def RW(self): return self.H * HEAD
    @property
    def LW(self): return self.NB * HEAD
    @property
    def MW(self): return self.MH * HEAD
    @property
    def o_rk(self): return self.RW
    @property
    def o_rv(self): return 2 * self.RW
    @property
    def o_rg(self): return 3 * self.RW
    @property
    def o_lx(self): return 4 * self.RW
    @property
    def o_lg(self): return 4 * self.RW + self.LW
    @property
    def o_mq(self): return 4 * self.RW + 2 * self.LW
    @property
    def o_mkv(self): return self.o_mq + self.QL
    @property
    def o_mg(self): return self.o_mkv + self.KL
    @property
    def o_merge(self): return self.o_mg + self.MW
    @property
    def o_mkr(self): return self.o_merge + 3 * self.D
    @property
    def NP(self): return -(-(self.o_mkr + ROPE) // 512) * 512
    @property
    def IN_WIDTH(self): return self.o_mkr + ROPE
    @property
    def QW(self): return self.MH * (HEAD + ROPE)
    @property
    def KVW(self): return self.MH * 2 * HEAD


_CFG = Cfg()


def _cparams(sem=None):
    return pltpu.CompilerParams(dimension_semantics=sem, vmem_limit_bytes=VMEM_LIMIT)


def _sigmoid(x):
    return jax.nn.sigmoid(x)


def _silu(x):
    return x * _sigmoid(x)


def _dsilu(x):
    s = _sigmoid(x)
    return s * (1.0 + x * (1.0 - s))


def _slab(rows, width, off):
    assert off % width == 0
    return pl.BlockSpec((rows, width), lambda i, _c=off // width: (i, _c))


def _row(width):
    return pl.BlockSpec((1, width), lambda i: (0, 0))


def _mm(a, b, out_dtype=f32, name="mm", mode="nn", tm=None):
    (M, K) = a.shape if mode != "tn" else a.shape[::-1]
    (K2, N) = b.shape if mode != "nt" else b.shape[::-1]
    assert K == K2
    tn = N if N <= 2048 else 512
    tk = K if K <= 2048 else 512
    assert N % tn == 0 and K % tk == 0
    osz = jnp.dtype(out_dtype).itemsize
    if tm is None:
        tm = M
        while 2 * tm * tk * 2 + 2 * tk * tn * 2 + 2 * tm * tn * osz + tm * tn * 4 > MM_BUDGET and tm % 16 == 0:
            tm //= 2
    assert M % tm == 0
    nk = K // tk
    dims = {"nn": (((1,), (0,)), ((), ())), "nt": (((1,), (1,)), ((), ())), "tn": (((0,), (0,)), ((), ()))}[mode]

    def dot(a_ref, b_ref):
        return lax.dot_general(a_ref[...].astype(bf16), b_ref[...].astype(bf16), dims, preferred_element_type=f32)

    if nk == 1:
        def body(a_ref, b_ref, o_ref):
            o_ref[...] = dot(a_ref, b_ref).astype(o_ref.dtype)
        scratch = []
    else:
        def body(a_ref, b_ref, o_ref, acc_ref):
            k = pl.program_id(2)

            @pl.when(k == 0)
            def _():
                acc_ref[...] = jnp.zeros_like(acc_ref)

            acc_ref[...] += dot(a_ref, b_ref)

            @pl.when(k == nk - 1)
            def _():
                o_ref[...] = acc_ref[...].astype(o_ref.dtype)
        scratch = [pltpu.VMEM((tm, tn), f32)]

    a_spec = pl.BlockSpec((tk, tm), lambda i, j, k: (k, i)) if mode == "tn" else pl.BlockSpec((tm, tk), lambda i, j, k: (i, k))
    b_spec = pl.BlockSpec((tn, tk), lambda i, j, k: (j, k)) if mode == "nt" else pl.BlockSpec((tk, tn), lambda i, j, k: (k, j))
    return pl.pallas_call(
        body, name=name,
        grid=(M // tm, N // tn, nk),
        in_specs=[a_spec, b_spec],
        out_specs=pl.BlockSpec((tm, tn), lambda i, j, k: (i, j)),
        out_shape=jax.ShapeDtypeStruct((M, N), out_dtype),
        scratch_shapes=scratch,
        compiler_params=_cparams(("parallel", "parallel", "arbitrary")),
    )(a, b)


def _ada_fwd(cfg, c_all, ada_w):
    L, D, n = ada_w.shape
    tn = n // 2 if (n // 2) % LANES == 0 else n

    def body(c_ref, w_ref, o_ref, ca_ref):
        ca = _silu(c_ref[...])
        ca_ref[...] = ca
        o_ref[0] = jnp.dot(ca.astype(bf16), w_ref[0].astype(bf16), preferred_element_type=f32)

    return pl.pallas_call(
        body, name="ada_fwd", grid=(L, n // tn),
        in_specs=[pl.BlockSpec((N_DEV, D), lambda l, j: (0, 0)), pl.BlockSpec((1, D, tn), lambda l, j: (l, 0, j))],
        out_specs=(pl.BlockSpec((1, N_DEV, tn), lambda l, j: (l, 0, j)), pl.BlockSpec((N_DEV, D), lambda l, j: (0, 0))),
        out_shape=(jax.ShapeDtypeStruct((L, N_DEV, n), f32), jax.ShapeDtypeStruct((N_DEV, D), f32)),
        compiler_params=_cparams(("arbitrary", "arbitrary")),
    )(c_all, ada_w)


def _ada_bwd(cfg, c_act_t, dmod):
    L, _, n = dmod.shape
    D = c_act_t.shape[0]
    tn = n // 2 if (n // 2) % LANES == 0 else n

    def body(c_ref, d_ref, o_ref):
        o_ref[0] = jnp.dot(c_ref[...].astype(bf16), d_ref[0].astype(bf16), preferred_element_type=f32)

    return pl.pallas_call(
        body, name="ada_bwd", grid=(L, n // tn),
        in_specs=[pl.BlockSpec((D, N_DEV), lambda l, j: (0, 0)), pl.BlockSpec((1, N_DEV, tn), lambda l, j: (l, 0, j))],
        out_specs=pl.BlockSpec((1, D, tn), lambda l, j: (l, 0, j)),
        out_shape=jax.ShapeDtypeStruct((L, D, n), f32),
        compiler_params=_cparams(("parallel", "parallel")),
    )(c_act_t, dmod)


def _prenorm_fwd(cfg, x, mod, gain):
    S, D, TR = cfg.S, cfg.D, cfg.TR

    def body(x_ref, mod_ref, g_ref, h_ref):
        x = x_ref[...]
        r = lax.rsqrt(jnp.mean(x * x, axis=-1, keepdims=True) + NORM_EPS)
        shift, scale = mod_ref[:, 0:D], mod_ref[:, D:2 * D]
        h_ref[...] = ((x * r) * g_ref[...] * (1.0 + scale) + shift).astype(bf16)

    return pl.pallas_call(
        body, name="prenorm_fwd", grid=(S // TR,),
        in_specs=[_slab(TR, D, 0), _row(3 * D), _row(D)],
        out_specs=_slab(TR, D, 0), out_shape=jax.ShapeDtypeStruct((S, D), bf16),
        compiler_params=_cparams(("parallel",)),
    )(x, mod, gain)


def _prenorm_bwd(cfg, x, dh, dres, mod, gain):
    S, D, TR = cfg.S, cfg.D, cfg.TR

    def body(x_ref, dh_ref, dres_ref, mod_ref, g_ref, dx_ref, sum_ref):
        i = pl.program_id(0)
        x, dh, g = x_ref[...], dh_ref[...], g_ref[...]
        scale = mod_ref[:, D:2 * D]
        r = lax.rsqrt(jnp.mean(x * x, axis=-1, keepdims=True) + NORM_EPS)
        xn = x * r
        t = dh * xn
        dxn = dh * (g * (1.0 + scale))
        dx_ref[...] = r * (dxn - xn * jnp.mean(dxn * xn, axis=-1, keepdims=True)) + dres_ref[...]
        part = jnp.concatenate([jnp.sum(dh, axis=0, keepdims=True), jnp.sum(t * g, axis=0, keepdims=True),
                                jnp.sum(t * (1.0 + scale), axis=0, keepdims=True), jnp.zeros((SUBLANES - 3, D), f32)], axis=0)

        @pl.when(i == 0)
        def _():
            sum_ref[...] = part

        @pl.when(i > 0)
        def _():
            sum_ref[...] += part

    return pl.pallas_call(
        body, name="prenorm_bwd", grid=(S // TR,),
        in_specs=[_slab(TR, D, 0), _slab(TR, D, 0), _slab(TR, D, 0), _row(3 * D), _row(D)],
        out_specs=(_slab(TR, D, 0), pl.BlockSpec((SUBLANES, D), lambda i: (0, 0))),
        out_shape=(jax.ShapeDtypeStruct((S, D), f32), jax.ShapeDtypeStruct((SUBLANES, D), f32)),
        compiler_params=_cparams(("arbitrary",)),
    )(x, dh, dres, mod, gain)


def _postnorm_fwd(cfg, x, y, mod, gain):
    S, D, TR = cfg.S, cfg.D, cfg.TR

    def body(x_ref, y_ref, mod_ref, g_ref, o_ref):
        y = y_ref[...]
        r = lax.rsqrt(jnp.mean(y * y, axis=-1, keepdims=True) + NORM_EPS)
        rg = mod_ref[:, 2 * D:3 * D]
        o_ref[...] = x_ref[...] + (1.0 + rg) * ((y * r) * g_ref[...])

    return pl.pallas_call(
        body, name="postnorm_fwd", grid=(S // TR,),
        in_specs=[_slab(TR, D, 0), _slab(TR, D, 0), _row(3 * D), _row(D)],
        out_specs=_slab(TR, D, 0), out_shape=jax.ShapeDtypeStruct((S, D), f32),
        compiler_params=_cparams(("parallel",)),
    )(x, y, mod, gain)


def _postnorm_bwd(cfg, dout, y, mod, gain):
    S, D, TR = cfg.S, cfg.D, cfg.TR

    def body(do_ref, y_ref, mod_ref, g_ref, dy_ref, sum_ref):
        i = pl.program_id(0)
        do, y, g = do_ref[...], y_ref[...], g_ref[...]
        rg = mod_ref[:, 2 * D:3 * D]
        r = lax.rsqrt(jnp.mean(y * y, axis=-1, keepdims=True) + NORM_EPS)
        yn = y * r
        t = do * yn
        dyn = do * ((1.0 + rg) * g)
        dy_ref[...] = (r * (dyn - yn * jnp.mean(dyn * yn, axis=-1, keepdims=True))).astype(bf16)
        part = jnp.concatenate([jnp.sum(t * g, axis=0, keepdims=True), jnp.sum(t * (1.0 + rg), axis=0, keepdims=True),
                                jnp.zeros((SUBLANES - 2, D), f32)], axis=0)

        @pl.when(i == 0)
        def _():
            sum_ref[...] = part

        @pl.when(i > 0)
        def _():
            sum_ref[...] += part

    return pl.pallas_call(
        body, name="postnorm_bwd", grid=(S // TR,),
        in_specs=[_slab(TR, D, 0), _slab(TR, D, 0), _row(3 * D), _row(D)],
        out_specs=(_slab(TR, D, 0), pl.BlockSpec((SUBLANES, D), lambda i: (0, 0))),
        out_shape=(jax.ShapeDtypeStruct((S, D), bf16), jax.ShapeDtypeStruct((SUBLANES, D), f32)),
        compiler_params=_cparams(("arbitrary",)),
    )(dout, y, mod, gain)


def _loss_head(cfg, y, target):
    S, D, TR = cfg.S, cfg.D, cfg.TR

    def body(y_ref, t_ref, d_ref, l_ref):
        i = pl.program_id(0)
        err = y_ref[...] - t_ref[...]
        d_ref[...] = err / D
        part = jnp.zeros((SUBLANES, LANES), f32) + 0.5 * jnp.sum(jnp.mean(err * err, axis=-1, keepdims=True))

        @pl.when(i == 0)
        def _():
            l_ref[...] = part

        @pl.when(i > 0)
        def _():
            l_ref[...] += part

    return pl.pallas_call(
        body, name="loss_head", grid=(S // TR,),
        in_specs=[_slab(TR, D, 0), _slab(TR, D, 0)],
        out_specs=(_slab(TR, D, 0), pl.BlockSpec((SUBLANES, LANES), lambda i: (0, 0))),
        out_shape=(jax.ShapeDtypeStruct((S, D), f32), jax.ShapeDtypeStruct((SUBLANES, LANES), f32)),
        compiler_params=_cparams(("arbitrary",)),
    )(y, target)


def _merge_fwd(cfg, proj, u0, u1, u2):
    S, D, TR = cfg.S, cfg.D, cfg.TR

    def body(l0, l1, l2, u0_ref, u1_ref, u2_ref, o_ref):
        o_ref[...] = (_sigmoid(l0[...]) * u0_ref[...] + _sigmoid(l1[...]) * u1_ref[...]
                      + _sigmoid(l2[...]) * u2_ref[...]).astype(bf16)

    return pl.pallas_call(
        body, name="merge_fwd", grid=(S // TR,),
        in_specs=[_slab(TR, D, cfg.o_merge + b * D) for b in range(3)] + [_slab(TR, D, 0)] * 3,
        out_specs=_slab(TR, D, 0), out_shape=jax.ShapeDtypeStruct((S, D), bf16),
        compiler_params=_cparams(("parallel",)),
    )(proj, proj, proj, u0, u1, u2)


def _merge_bwd(cfg, proj, dmerged, u0, u1, u2):
    S, D, TR = cfg.S, cfg.D, cfg.TR

    def body(l0, l1, l2, dm_ref, u0_ref, u1_ref, u2_ref, du0, du1, du2, dl_ref):
        dm = dm_ref[...]
        for b, (l, u, du) in enumerate(((l0, u0_ref, du0), (l1, u1_ref, du1), (l2, u2_ref, du2))):
            g = _sigmoid(l[...])
            du[...] = (dm * g).astype(bf16)
            dl_ref[:, b * D:(b + 1) * D] = (dm * u[...] * (g * (1.0 - g))).astype(bf16)

    return pl.pallas_call(
        body, name="merge_bwd", grid=(S // TR,),
        in_specs=[_slab(TR, D, cfg.o_merge + b * D) for b in range(3)] + [_slab(TR, D, 0)] * 4,
        out_specs=(_slab(TR, D, 0),) * 3 + (_slab(TR, 3 * D, 0),),
        out_shape=(jax.ShapeDtypeStruct((S, D), bf16),) * 3 + (jax.ShapeDtypeStruct((S, 3 * D), bf16),),
        compiler_params=_cparams(("parallel",)),
    )(proj, proj, proj, dmerged, u0, u1, u2)


def _rope128(x, c, s):
    return x * c + pltpu.roll(x, 64, axis=1) * s


def _rope128_t(dy, c, s):
    return dy * c + pltpu.roll(dy * s, 64, axis=1)


def _swap32(x):
    w = x.shape[1]
    lane = lax.broadcasted_iota(jnp.int32, x.shape, 1)
    return jnp.where((lane % 64) < 32, pltpu.roll(x, w - 32, axis=1), pltpu.roll(x, 32, axis=1))


def _rope64(x, c, s):
    return x * c + _swap32(x) * s


def _rope64_t(dy, c, s):
    return dy * c + _swap32(dy * s)


def _rope_tables(cfg, positions):
    pos = positions.astype(f32)[0][:, None]

    def tab(dim):
        inv_freq = ROPE_BASE ** (-jnp.arange(0, dim, 2, dtype=f32) / dim)
        ang = pos * inv_freq
        cos, sin = jnp.cos(ang), jnp.sin(ang)
        return jnp.concatenate([cos, cos], axis=1), jnp.concatenate([-sin, sin], axis=1)

    return tab(HEAD), tab(ROPE)


def _ret_consts(cfg):
    h = np.arange(cfg.H, dtype=np.float64)
    log_gamma = np.log1p(-np.exp2(-5.0 - h)).astype(np.float32)
    idx = np.arange(CHUNK, dtype=np.float32)
    intra = np.exp(log_gamma[:, None, None] * np.abs(idx[:, None] - idx[None, :]))
    kdec = np.exp(log_gamma[:, None] * (CHUNK - 1 - idx)[None, :])
    qdec = np.exp(log_gamma[:, None] * (idx + 1.0)[None, :])
    cdec = np.exp(log_gamma * CHUNK)
    bc = lambda a: jnp.asarray(np.broadcast_to(a[..., None], a.shape + (HEAD,)).astype(np.float32))
    return jnp.asarray(intra.astype(np.float32)), bc(kdec), bc(qdec), bc(cdec[:, None])


def _ret_core(cfg, q_raw, k_raw, v_raw, cos, sin, intra, kdec, qdec, cdec, p_ref):
    S = cfg.S
    NC = S // CHUNK
    q = _rope128(q_raw, cos, sin) * (HEAD ** -0.5)
    k = _rope128(k_raw, cos, sin)
    q3 = q.reshape(NC, CHUNK, HEAD)
    k3 = k.reshape(NC, CHUNK, HEAD)
    qb, kb = q3.astype(bf16), k3.astype(bf16)
    vb = v_raw.reshape(NC, CHUNK, HEAD).astype(bf16)
    sdb = (jnp.einsum('nid,njd->nij', qb, kb, preferred_element_type=f32) * intra[None]).astype(bf16)
    o_intra = jnp.einsum('nij,nje->nie', sdb, vb, preferred_element_type=f32)
    kdb = (k3 * kdec[None]).astype(bf16)
    kv = jnp.einsum('njd,nje->nde', kdb, vb, preferred_element_type=f32)
    p_ref[0] = jnp.zeros((HEAD, HEAD), f32)
    for n in range(1, NC):
        p_ref[n] = p_ref[n - 1] * cdec + kv[n - 1]
    pb = p_ref[...].astype(bf16)
    qdb = (q3 * qdec[None]).astype(bf16)
    o_inter = jnp.einsum('nid,nde->nie', qdb, pb, preferred_element_type=f32)
    o = (o_intra + o_inter).reshape(S, HEAD)
    return o, (qb, kb, vb, sdb, kdb, qdb, pb)


def _ret_specs(cfg):
    S = cfg.S
    hs = lambda off: pl.BlockSpec((S, HEAD), lambda h, _c=off // HEAD: (0, _c + h))
    full = pl.BlockSpec((S, HEAD), lambda h: (0, 0))
    consts = [pl.BlockSpec((None, CHUNK, CHUNK), lambda h: (h, 0, 0)), pl.BlockSpec((None, CHUNK, HEAD), lambda h: (h, 0, 0)),
              pl.BlockSpec((None, CHUNK, HEAD), lambda h: (h, 0, 0)), pl.BlockSpec((None, 1, HEAD), lambda h: (h, 0, 0))]
    gn = pl.BlockSpec((1, HEAD), lambda h: (0, h))
    return hs, full, consts, gn


def _ret_fwd(cfg, proj, gn, cos, sin, consts):
    S, NC = cfg.S, cfg.S // CHUNK
    hs, full, cspecs, gspec = _ret_specs(cfg)

    def body(q_ref, k_ref, v_ref, g_ref, gn_ref, cos_ref, sin_ref, intra, kdec, qdec, cdec, y_ref, p_ref):
        o, _ = _ret_core(cfg, q_ref[...], k_ref[...], v_ref[...], cos_ref[...], sin_ref[...],
                         intra[...], kdec[...], qdec[...], cdec[...], p_ref)
        mean = jnp.mean(o, axis=-1, keepdims=True)
        var = jnp.mean(jnp.square(o - mean), axis=-1, keepdims=True)
        z = ((o - mean) * lax.rsqrt(var + NORM_EPS)) * gn_ref[...]
        y_ref[...] = (z * _silu(g_ref[...])).astype(bf16)

    return pl.pallas_call(
        body, name="ret_fwd", grid=(cfg.H,),
        in_specs=[hs(0), hs(cfg.o_rk), hs(cfg.o_rv), hs(cfg.o_rg), gspec, full, full] + cspecs,
        out_specs=hs(0), out_shape=jax.ShapeDtypeStruct((S, cfg.RW), bf16),
        scratch_shapes=[pltpu.VMEM((NC, HEAD, HEAD), f32)],
        compiler_params=_cparams(("arbitrary",)),
    )(proj, proj, proj, proj, gn, cos, sin, *consts)


def _ret_bwd(cfg, proj, dy, gn, cos, sin, consts):
    S, NC = cfg.S, cfg.S // CHUNK
    hs, full, cspecs, gspec = _ret_specs(cfg)

    def body(q_ref, k_ref, v_ref, g_ref, dy_ref, gn_ref, cos_ref, sin_ref, intra_ref, kdec_ref, qdec_ref, cdec_ref,
             dq_ref, dk_ref, dv_ref, dg_ref, dgn_ref, p_ref, g_scr):
        cos, sin = cos_ref[...], sin_ref[...]
        intra, kdec, qdec, cdec = intra_ref[...], kdec_ref[...], qdec_ref[...], cdec_ref[...]
        o, (qb, kb, vb, sdb, kdb, qdb, pb) = _ret_core(cfg, q_ref[...], k_ref[...], v_ref[...], cos, sin,
                                                     intra, kdec, qdec, cdec, p_ref)
        gate, dy, gnv = g_ref[...], dy_ref[...], gn_ref[...]
        mean = jnp.mean(o, axis=-1, keepdims=True)
        rstd = lax.rsqrt(jnp.mean(jnp.square(o - mean), axis=-1, keepdims=True) + NORM_EPS)
        on = (o - mean) * rstd
        dz = dy * _silu(gate)
        dg_ref[...] = (dy * (on * gnv) * _dsilu(gate)).astype(bf16)
        dgn_ref[...] = jnp.sum(dz * on, axis=0, keepdims=True)
        don = dz * gnv
        do = rstd * (don - jnp.mean(don, axis=-1, keepdims=True) - on * jnp.mean(don * on, axis=-1, keepdims=True))
        dob = do.reshape(NC, CHUNK, HEAD).astype(bf16)
        dsb = (jnp.einsum('nie,nje->nij', dob, vb, preferred_element_type=f32) * intra[None]).astype(bf16)
        dv = jnp.einsum('nij,nie->nje', sdb, dob, preferred_element_type=f32)
        dq = jnp.einsum('nij,njd->nid', dsb, kb, preferred_element_type=f32)
        dk = jnp.einsum('nij,nid->njd', dsb, qb, preferred_element_type=f32)
        dq = dq + jnp.einsum('nie,nde->nid', dob, pb, preferred_element_type=f32) * qdec[None]
        dp = jnp.einsum('nid,nie->nde', qdb, dob, preferred_element_type=f32)
        g_scr[NC - 1] = jnp.zeros((HEAD, HEAD), f32)
        for n in range(NC - 2, -1, -1):
            g_scr[n] = dp[n + 1] + g_scr[n + 1] * cdec
        gb = g_scr[...].astype(bf16)
        dk = dk + jnp.einsum('nje,nde->njd', vb, gb, preferred_element_type=f32) * kdec[None]
        dv = dv + jnp.einsum('njd,nde->nje', kdb, gb, preferred_element_type=f32)
        dq_ref[...] = _rope128_t(dq.reshape(S, HEAD) * (HEAD ** -0.5), cos, sin).astype(bf16)
        dk_ref[...] = _rope128_t(dk.reshape(S, HEAD), cos, sin).astype(bf16)
        dv_ref[...] = dv.reshape(S, HEAD).astype(bf16)

    return pl.pallas_call(
        body, name="ret_bwd", grid=(cfg.H,),
        in_specs=[hs(0), hs(cfg.o_rk), hs(cfg.o_rv), hs(cfg.o_rg), hs(0), gspec, full, full] + cspecs,
        out_specs=(hs(0),) * 4 + (gspec,),
        out_shape=(jax.ShapeDtypeStruct((S, cfg.RW), bf16),) * 4 + (jax.ShapeDtypeStruct((1, cfg.RW), f32),),
        scratch_shapes=[pltpu.VMEM((NC, HEAD, HEAD), f32), pltpu.VMEM((NC, HEAD, HEAD), f32)],
        compiler_params=_cparams(("arbitrary",)),
    )(proj, proj, proj, proj, dy, gn, cos, sin, *consts)


def _expm1(x):
    small = x * (1.0 + x * (0.5 + x * (1.0 / 6.0 + x * (1.0 / 24.0 + x * (1.0 / 120.0)))))
    return jnp.where(jnp.abs(x) < 0.1, small, jnp.exp(x) - 1.0)


def _softplus(z):
    return jnp.maximum(z, 0.0) + jnp.log1p(jnp.exp(-jnp.abs(z)))


def _lru_conv(cfg, x_ref, halo_ref, cw, scr, first):
    TR = cfg.TR
    scr[0:SUBLANES, :] = jnp.where(first, 0.0, halo_ref[...])
    scr[SUBLANES:SUBLANES + TR, :] = x_ref[...]
    xc = scr[pl.ds(SUBLANES - (CONV - 1), TR), :] * cw[0:1, :]
    for j in range(1, CONV):
        xc = xc + scr[pl.ds(SUBLANES - (CONV - 1) + j, TR), :] * cw[j:j + 1, :]
    return xc


def _lru_pre(cfg, xc, wa_ref, wx_ref, ba, bx):
    xb = xc.astype(bf16)
    pa = jnp.concatenate([jnp.dot(xb[:, n * HEAD:(n + 1) * HEAD], wa_ref[n].astype(bf16), preferred_element_type=f32)
                          for n in range(cfg.NB)], axis=1) + ba
    px = jnp.concatenate([jnp.dot(xb[:, n * HEAD:(n + 1) * HEAD], wx_ref[n].astype(bf16), preferred_element_type=f32)
                          for n in range(cfg.NB)], axis=1) + bx
    return pa, px


def _lru_ab(pa, px, xc, lam):
    r, i = _sigmoid(pa), _sigmoid(px)
    log_a = (-LRU_C * r) * _softplus(-lam)
    a = jnp.exp(log_a)
    b = jnp.sqrt(-_expm1(2.0 * log_a)) * (i * xc)
    return a, b


def _lru_halo_specs(cfg, off, W):
    TR, S = cfg.TR, cfg.S
    nb = TR // SUBLANES
    cb = off // W
    main = pl.BlockSpec((TR, W), lambda i: (i, cb))
    prev = pl.BlockSpec((SUBLANES, W), lambda i: (jnp.maximum(i * nb - 1, 0), cb))
    nxt = pl.BlockSpec((SUBLANES, W), lambda i: (jnp.minimum((i + 1) * nb, S // SUBLANES - 1), cb))
    return main, prev, nxt


def _lru_gates(cfg, proj, cw, cb, wa, ba, wx, bx, lam):
    S, W, TR, NB = cfg.S, cfg.LW, cfg.TR, cfg.NB
    assert cfg.o_lx % W == 0
    main, prev, _ = _lru_halo_specs(cfg, cfg.o_lx, W)
    wspec = pl.BlockSpec((NB, HEAD, HEAD), lambda i: (0, 0, 0))

    def body(x_ref, halo_ref, cw_ref, cb_ref, wa_ref, ba_ref, wx_ref, bx_ref, lam_ref, a_ref, b_ref, scr):
        xc = _lru_conv(cfg, x_ref, halo_ref, cw_ref[...], scr, pl.program_id(0) == 0) + cb_ref[...]
        pa, px = _lru_pre(cfg, xc, wa_ref, wx_ref, ba_ref[...], bx_ref[...])
        a, b = _lru_ab(pa, px, xc, lam_ref[...])
        a_ref[...] = a
        b_ref[...] = b

    return pl.pallas_call(
        body, name="lru_gates", grid=(S // TR,),
        in_specs=[main, prev, pl.BlockSpec((CONV, W), lambda i: (0, 0)), _row(W), wspec, _row(W), wspec, _row(W), _row(W)],
        out_specs=(_slab(TR, W, 0),) * 2, out_shape=(jax.ShapeDtypeStruct((S, W), f32),) * 2,
        scratch_shapes=[pltpu.VMEM((TR + SUBLANES, W), f32)],
        compiler_params=_cparams(("parallel",)),
    )(proj, proj, cw, cb, wa, ba, wx, bx, lam)


def _lru_lane_block(cfg):
    return 256 if cfg.LW % 256 == 0 else LANES


def _lru_scan_fwd(cfg, proj, a, b):
    S, W = cfg.S, cfg.LW
    LB = _lru_lane_block(cfg)
    assert cfg.o_lg % LB == 0
    col = lambda off: pl.BlockSpec((S, LB), lambda j, _c=off // LB: (0, _c + j))

    def body(a_ref, b_ref, g_ref, h_ref, y_ref):
        def blk(t, h):
            r0 = pl.multiple_of(t * SUBLANES, SUBLANES)
            at, bt = a_ref[pl.ds(r0, SUBLANES), :], b_ref[pl.ds(r0, SUBLANES), :]
            rows = []
            for j in range(SUBLANES):
                h = at[j:j + 1, :] * h + bt[j:j + 1, :]
                rows.append(h)
            h_ref[pl.ds(r0, SUBLANES), :] = jnp.concatenate(rows, axis=0)
            return h

        lax.fori_loop(0, S // SUBLANES, blk, jnp.zeros((1, LB), f32))
        y_ref[...] = (h_ref[...] * _silu(g_ref[...])).astype(bf16)

    return pl.pallas_call(
        body, name="lru_scan_fwd", grid=(W // LB,),
        in_specs=[col(0), col(0), col(cfg.o_lg)],
        out_specs=(col(0), col(0)),
        out_shape=(jax.ShapeDtypeStruct((S, W), f32), jax.ShapeDtypeStruct((S, W), bf16)),
        compiler_params=_cparams(("parallel",)),
    )(a, b, proj)


def _lru_scan_bwd(cfg, proj, a, h, dy):
    S, W = cfg.S, cfg.LW
    LB = _lru_lane_block(cfg)
    col = lambda off: pl.BlockSpec((S, LB), lambda j, _c=off // LB: (0, _c + j))

    def body(a_ref, h_ref, dy_ref, g_ref, da_ref, db_ref, dg_ref):
        gate, dy = g_ref[...], dy_ref[...]
        dg_ref[...] = (dy * h_ref[...] * _dsilu(gate)).astype(bf16)
        da_ref[...] = dy * _silu(gate)

        def blk(t, carry):
            dh_next, a_next = carry
            r0 = pl.multiple_of((S // SUBLANES - 1 - t) * SUBLANES, SUBLANES)
            at, ct = a_ref[pl.ds(r0, SUBLANES), :], da_ref[pl.ds(r0, SUBLANES), :]
            rows = [None] * SUBLANES
            for j in range(SUBLANES - 1, -1, -1):
                dh_next = ct[j:j + 1, :] + a_next * dh_next
                a_next = at[j:j + 1, :]
                rows[j] = dh_next
            db_ref[pl.ds(r0, SUBLANES), :] = jnp.concatenate(rows, axis=0)
            return dh_next, a_next

        z = jnp.zeros((1, LB), f32)
        lax.fori_loop(0, S // SUBLANES, blk, (z, z))
        row = lax.broadcasted_iota(jnp.int32, (S, LB), 0)
        hprev = jnp.where(row == 0, 0.0, pltpu.roll(h_ref[...], 1, axis=0))
        da_ref[...] = db_ref[...] * hprev

    return pl.pallas_call(
        body, name="lru_scan_bwd", grid=(W // LB,),
        in_specs=[col(0), col(0), col(0), col(cfg.o_lg)],
        out_specs=(col(0),) * 3,
        out_shape=(jax.ShapeDtypeStruct((S, W), f32),) * 2 + (jax.ShapeDtypeStruct((S, W), bf16),),
        compiler_params=_cparams(("parallel",)),
    )(a, h, dy, proj)


def _lru_gates_bwd(cfg, proj, da, db, cw, cb, wa, ba, wx, bx, lam):
    S, W, TR, NB = cfg.S, cfg.LW, cfg.TR, cfg.NB
    main, prev, _ = _lru_halo_specs(cfg, cfg.o_lx, W)
    wspec = pl.BlockSpec((NB, HEAD, HEAD), lambda i: (0, 0, 0))

    def body(x_ref, halo_ref, da_ref, db_ref, cw_ref, cb_ref, wa_ref, ba_ref, wx_ref, bx_ref, lam_ref,
             dxc_ref, dwa_ref, dwx_ref, sum_ref, scr):
        i = pl.program_id(0)
        lam = lam_ref[...]
        xc = _lru_conv(cfg, x_ref, halo_ref, cw_ref[...], scr, i == 0) + cb_ref[...]
        pa, px = _lru_pre(cfg, xc, wa_ref, wx_ref, ba_ref[...], bx_ref[...])
        _, vjp = jax.vjp(_lru_ab, pa, px, xc, lam)
        dpa, dpx, dxc, dlam = vjp((da_ref[...], db_ref[...]))
        xb, dpab, dpxb = xc.astype(bf16), dpa.astype(bf16), dpx.astype(bf16)
        nt = (((1,), (1,)), ((), ()))
        tn = (((0,), (0,)), ((), ()))
        back = []
        dwa, dwx = [], []
        for n in range(NB):
            sl = slice(n * HEAD, (n + 1) * HEAD)
            back.append(lax.dot_general(dpab[:, sl], wa_ref[n].astype(bf16), nt, preferred_element_type=f32)
                        + lax.dot_general(dpxb[:, sl], wx_ref[n].astype(bf16), nt, preferred_element_type=f32))
            dwa.append(lax.dot_general(xb[:, sl], dpab[:, sl], tn, preferred_element_type=f32))
            dwx.append(lax.dot_general(xb[:, sl], dpxb[:, sl], tn, preferred_element_type=f32))
        dxc_ref[...] = dxc + jnp.concatenate(back, axis=1)
        part = jnp.concatenate([jnp.sum(dpa, axis=0, keepdims=True), jnp.sum(dpx, axis=0, keepdims=True), dlam,
                                jnp.zeros((SUBLANES - 3, W), f32)], axis=0)

        @pl.when(i == 0)
        def _():
            sum_ref[...] = part
            for n in range(NB):
                dwa_ref[n] = dwa[n]
                dwx_ref[n] = dwx[n]

        @pl.when(i > 0)
        def _():
            sum_ref[...] += part
            for n in range(NB):
                dwa_ref[n] += dwa[n]
                dwx_ref[n] += dwx[n]

    return pl.pallas_call(
        body, name="lru_gates_bwd", grid=(S // TR,),
        in_specs=[main, prev, _slab(TR, W, 0), _slab(TR, W, 0), pl.BlockSpec((CONV, W), lambda i: (0, 0)), _row(W),
                  wspec, _row(W), wspec, _row(W), _row(W)],
        out_specs=(_slab(TR, W, 0), wspec, wspec, pl.BlockSpec((SUBLANES, W), lambda i: (0, 0))),
        out_shape=(jax.ShapeDtypeStruct((S, W), f32), jax.ShapeDtypeStruct((NB, HEAD, HEAD), f32),
                   jax.ShapeDtypeStruct((NB, HEAD, HEAD), f32), jax.ShapeDtypeStruct((SUBLANES, W), f32)),
        scratch_shapes=[pltpu.VMEM((TR + SUBLANES, W), f32)],
        compiler_params=_cparams(("arbitrary",)),
    )(proj, proj, da, db, cw, cb, wa, ba, wx, bx, lam)


def _lru_conv_bwd(cfg, proj, dxc, cw):
    S, W, TR = cfg.S, cfg.LW, cfg.TR
    main, prev, _ = _lru_halo_specs(cfg, cfg.o_lx, W)
    dmain, _, dnext = _lru_halo_specs(cfg, 0, W)

    def body(x_ref, xhalo_ref, d_ref, dhalo_ref, cw_ref, dx_ref, sum_ref, xs, ds):
        i = pl.program_id(0)
        cw = cw_ref[...]
        d = d_ref[...]
        xs[0:SUBLANES, :] = jnp.where(i == 0, 0.0, xhalo_ref[...])
        xs[SUBLANES:SUBLANES + TR, :] = x_ref[...]
        ds[0:TR, :] = d
        ds[TR:TR + SUBLANES, :] = jnp.where(i == pl.num_programs(0) - 1, 0.0, dhalo_ref[...])
        dx = ds[pl.ds(CONV - 1, TR), :] * cw[0:1, :]
        parts = [jnp.sum(d * xs[pl.ds(SUBLANES - (CONV - 1), TR), :], axis=0, keepdims=True)]
        for j in range(1, CONV):
            dx = dx + ds[pl.ds(CONV - 1 - j, TR), :] * cw[j:j + 1, :]
            parts.append(jnp.sum(d * xs[pl.ds(SUBLANES - (CONV - 1) + j, TR), :], axis=0, keepdims=True))
        dx_ref[...] = dx.astype(bf16)
        part = jnp.concatenate(parts + [jnp.sum(d, axis=0, keepdims=True), jnp.zeros((SUBLANES - CONV - 1, W), f32)], axis=0)

        @pl.when(i == 0)
        def _():
            sum_ref[...] = part

        @pl.when(i > 0)
        def _():
            sum_ref[...] += part

    return pl.pallas_call(
        body, name="lru_conv_bwd", grid=(S // TR,),
        in_specs=[main, prev, dmain, dnext, pl.BlockSpec((CONV, W), lambda i: (0, 0))],
        out_specs=(_slab(TR, W, 0), pl.BlockSpec((SUBLANES, W), lambda i: (0, 0))),
        out_shape=(jax.ShapeDtypeStruct((S, W), bf16), jax.ShapeDtypeStruct((SUBLANES, W), f32)),
        scratch_shapes=[pltpu.VMEM((TR + SUBLANES, W), f32), pltpu.VMEM((TR + SUBLANES, W), f32)],
        compiler_params=_cparams(("arbitrary",)),
    )(proj, proj, dxc, dxc, cw)


def _rms(x, g):
    r = lax.rsqrt(jnp.mean(x * x, axis=-1, keepdims=True) + NORM_EPS)
    return (x * r) * g, r


def _mla_norm(cfg, proj, qg, kg):
    S, TR = cfg.S, cfg.TR

    def body(q_ref, k_ref, qg_ref, kg_ref, qn_ref, kn_ref):
        qn_ref[...] = _rms(q_ref[...], qg_ref[...])[0].astype(bf16)
        kn_ref[...] = _rms(k_ref[...], kg_ref[...])[0].astype(bf16)

    return pl.pallas_call(
        body, name="mla_norm", grid=(S // TR,),
        in_specs=[_slab(TR, cfg.QL, cfg.o_mq), _slab(TR, cfg.KL, cfg.o_mkv), _row(cfg.QL), _row(cfg.KL)],
        out_specs=(_slab(TR, cfg.QL, 0), _slab(TR, cfg.KL, 0)),
        out_shape=(jax.ShapeDtypeStruct((S, cfg.QL), bf16), jax.ShapeDtypeStruct((S, cfg.KL), bf16)),
        compiler_params=_cparams(("parallel",)),
    )(proj, proj, qg, kg)


def _mla_norm_bwd(cfg, proj, dqn, dkn, qg, kg):
    S, TR = cfg.S, cfg.TR

    def one(x, g, dn):
        r = lax.rsqrt(jnp.mean(x * x, axis=-1, keepdims=True) + NORM_EPS)
        xn = x * r
        dxn = dn * g
        dx = r * (dxn - xn * jnp.mean(dxn * xn, axis=-1, keepdims=True))
        return dx, jnp.sum(dn * xn, axis=0, keepdims=True)

    def body(q_ref, k_ref, dq_ref, dk_ref, qg_ref, kg_ref, dmq_ref, dmk_ref, sq_ref, sk_ref):
        i = pl.program_id(0)
        dq, gq = one(q_ref[...], qg_ref[...], dq_ref[...])
        dk, gk = one(k_ref[...], kg_ref[...], dk_ref[...])
        dmq_ref[...] = dq.astype(bf16)
        dmk_ref[...] = dk.astype(bf16)
        pq = jnp.concatenate([gq, jnp.zeros((SUBLANES - 1, cfg.QL), f32)], axis=0)
        pk = jnp.concatenate([gk, jnp.zeros((SUBLANES - 1, cfg.KL), f32)], axis=0)

        @pl.when(i == 0)
        def _():
            sq_ref[...] = pq
            sk_ref[...] = pk

        @pl.when(i > 0)
        def _():
            sq_ref[...] += pq
            sk_ref[...] += pk

    return pl.pallas_call(
        body, name="mla_norm_bwd", grid=(S // TR,),
        in_specs=[_slab(TR, cfg.QL, cfg.o_mq), _slab(TR, cfg.KL, cfg.o_mkv), _slab(TR, cfg.QL, 0), _slab(TR, cfg.KL, 0),
                  _row(cfg.QL), _row(cfg.KL)],
        out_specs=(_slab(TR, cfg.QL, 0), _slab(TR, cfg.KL, 0), pl.BlockSpec((SUBLANES, cfg.QL), lambda i: (0, 0)),
                   pl.BlockSpec((SUBLANES, cfg.KL), lambda i: (0, 0))),
        out_shape=(jax.ShapeDtypeStruct((S, cfg.QL), bf16), jax.ShapeDtypeStruct((S, cfg.KL), bf16),
                   jax.ShapeDtypeStruct((SUBLANES, cfg.QL), f32), jax.ShapeDtypeStruct((SUBLANES, cfg.KL), f32)),
        compiler_params=_cparams(("arbitrary",)),
    )(proj, proj, dqn, dkn, qg, kg)


def _mla_pack(cfg, proj, q, kv, cq, sq, ck, sk):
    S, TR, MH = cfg.S, cfg.TR, cfg.MH
    NW, RWD = MH * HEAD, MH * ROPE

    def body(q_ref, kv_ref, kr_ref, cq_ref, sq_ref, ck_ref, sk_ref, qo_ref, ko_ref, vo_ref):
        q, kv = q_ref[...], kv_ref[...]
        qr = _rope64(q[:, NW:], cq_ref[...], sq_ref[...])
        kr = _rope64(kr_ref[...], ck_ref[...], sk_ref[...]).astype(bf16)
        lane = lax.broadcasted_iota(jnp.int32, (TR, HEAD), 1)
        for h in range(MH):
            grp = qr[:, (h // 2) * HEAD:(h // 2 + 1) * HEAD]
            if h % 2:
                grp = pltpu.roll(grp, 64, axis=1)
            qo_ref[h] = jnp.concatenate([q[:, h * HEAD:(h + 1) * HEAD], jnp.where(lane < ROPE, grp, 0.0)], axis=1).astype(bf16)
            ko_ref[h] = jnp.concatenate([kv[:, 2 * h * HEAD:(2 * h + 1) * HEAD].astype(bf16), kr], axis=1)
            vo_ref[h] = kv[:, (2 * h + 1) * HEAD:(2 * h + 2) * HEAD].astype(bf16)

    hspec = lambda w: pl.BlockSpec((MH, TR, w), lambda i: (0, i, 0))
    return pl.pallas_call(
        body, name="mla_pack", grid=(S // TR,),
        in_specs=[_slab(TR, cfg.QW, 0), _slab(TR, cfg.KVW, 0), _slab(TR, HEAD, cfg.o_mkr),
                  _slab(TR, RWD, 0), _slab(TR, RWD, 0), _slab(TR, HEAD, 0), _slab(TR, HEAD, 0)],
        out_specs=(hspec(2 * HEAD), hspec(2 * HEAD), hspec(HEAD)),
        out_shape=(jax.ShapeDtypeStruct((MH, S, 2 * HEAD), bf16), jax.ShapeDtypeStruct((MH, S, 2 * HEAD), bf16),
                   jax.ShapeDtypeStruct((MH, S, HEAD), bf16)),
        compiler_params=_cparams(("parallel",)),
    )(q, kv, proj, cq, sq, ck, sk)


def _mla_unpack_bwd(cfg, dq3, dk3, dv3, cq, sq, ck, sk):
    S, TR, MH = cfg.S, cfg.TR, cfg.MH
    RWD = MH * ROPE

    def body(dq_ref, dk_ref, dv_ref, cq_ref, sq_ref, ck_ref, sk_ref, q_ref, kv_ref, kr_ref):
        lane = lax.broadcasted_iota(jnp.int32, (TR, HEAD), 1)
        nope, ropes, kvs = [], [], []
        dkr = jnp.zeros((TR, HEAD), f32)
        for h in range(MH):
            dq = dq_ref[h]
            nope.append(dq[:, :HEAD])
            part = jnp.where(lane < ROPE, dq[:, HEAD:], 0.0)
            if h % 2:
                ropes[-1] = ropes[-1] + pltpu.roll(part, 64, axis=1)
            else:
                ropes.append(part)
            dk = dk_ref[h]
            kvs += [dk[:, :HEAD], dv_ref[h]]
            dkr = dkr + dk[:, HEAD:]
        dqr = _rope64_t(jnp.concatenate(ropes, axis=1), cq_ref[...], sq_ref[...])
        q_ref[...] = jnp.concatenate(nope + [dqr], axis=1).astype(bf16)
        kv_ref[...] = jnp.concatenate(kvs, axis=1).astype(bf16)
        dkr = jnp.where(lane < ROPE, dkr, 0.0)
        kr_ref[...] = _rope64_t(dkr, ck_ref[...], sk_ref[...]).astype(bf16)

    hspec = lambda w: pl.BlockSpec((MH, TR, w), lambda i: (0, i, 0))
    return pl.pallas_call(
        body, name="mla_unpack_bwd", grid=(S // TR,),
        in_specs=[hspec(2 * HEAD), hspec(2 * HEAD), hspec(HEAD), _slab(TR, RWD, 0), _slab(TR, RWD, 0),
                  _slab(TR, HEAD, 0), _slab(TR, HEAD, 0)],
        out_specs=(_slab(TR, cfg.QW, 0), _slab(TR, cfg.KVW, 0), _slab(TR, HEAD, 0)),
        out_shape=(jax.ShapeDtypeStruct((S, cfg.QW), bf16), jax.ShapeDtypeStruct((S, cfg.KVW), bf16),
                   jax.ShapeDtypeStruct((S, HEAD), bf16)),
        compiler_params=_cparams(("parallel",)),
    )(dq3, dk3, dv3, cq, sq, ck, sk)


def _mla_probs(cfg, q, k, i):
    TQ, n = cfg.TQ, k.shape[0]
    nt = (((1,), (1,)), ((), ()))
    s = lax.dot_general(q, k, nt, preferred_element_type=f32) * ((HEAD + ROPE) ** -0.5)
    qc = (i * TQ + lax.broadcasted_iota(jnp.int32, (TQ, n), 0)) // CHUNK
    kc = lax.broadcasted_iota(jnp.int32, (TQ, n), 1) // CHUNK
    s = jnp.where(kc <= qc, s, -1e30)
    m = jnp.max(s, axis=-1, keepdims=True)
    e = jnp.exp(s - m)
    return e / jnp.sum(e, axis=-1, keepdims=True)


def _mla_attn_specs(cfg):
    S, TQ = cfg.S, cfg.TQ
    qs = lambda w: pl.BlockSpec((None, TQ, w), lambda h, i: (h, i, 0))
    ks = lambda w: pl.BlockSpec((None, S, w), lambda h, i: (h, 0, 0))
    hs = lambda off: pl.BlockSpec((TQ, HEAD), lambda h, i, _c=off // HEAD: (i, _c + h))
    return qs, ks, hs


def _mla_attn_fwd(cfg, proj, q3, k3, v3):
    S, TQ, MH = cfg.S, cfg.TQ, cfg.MH
    qs, ks, hs = _mla_attn_specs(cfg)

    def body(q_ref, k_ref, v_ref, g_ref, o_ref, y_ref):
        for i in range(S // TQ):
            @pl.when(pl.program_id(1) == i)
            def _(i=i):
                n = (i + 1) * TQ
                p = _mla_probs(cfg, q_ref[...], k_ref[0:n, :], i)
                o = jnp.dot(p.astype(bf16), v_ref[0:n, :], preferred_element_type=f32)
                o_ref[...] = o
                y_ref[...] = (o * _silu(g_ref[...])).astype(bf16)

    return pl.pallas_call(
        body, name="mla_attn_fwd", grid=(MH, S // TQ),
        in_specs=[qs(2 * HEAD), ks(2 * HEAD), ks(HEAD), hs(cfg.o_mg)],
        out_specs=(hs(0), hs(0)),
        out_shape=(jax.ShapeDtypeStruct((S, cfg.MW), f32), jax.ShapeDtypeStruct((S, cfg.MW), bf16)),
        compiler_params=_cparams(("parallel", "parallel")),
    )(q3, k3, v3, proj)


def _mla_attn_bwd(cfg, proj, q3, k3, v3, o, dy):
    S, TQ, MH = cfg.S, cfg.TQ, cfg.MH
    qs, ks, hs = _mla_attn_specs(cfg)

    def body(q_ref, k_ref, v_ref, g_ref, o_ref, dy_ref, dq_ref, dk_ref, dv_ref, dg_ref):
        q = q_ref[...]
        gate, dy, o = g_ref[...], dy_ref[...], o_ref[...]
        dg_ref[...] = (dy * o * _dsilu(gate)).astype(bf16)
        dob = (dy * _silu(gate)).astype(bf16)
        nt = (((1,), (1,)), ((), ()))
        tn = (((0,), (0,)), ((), ()))

        @pl.when(pl.program_id(1) == 0)
        def _():
            dk_ref[...] = jnp.zeros_like(dk_ref)
            dv_ref[...] = jnp.zeros_like(dv_ref)

        for i in range(S // TQ):
            @pl.when(pl.program_id(1) == i)
            def _(i=i):
                n = (i + 1) * TQ
                k, v = k_ref[0:n, :], v_ref[0:n, :]
                p = _mla_probs(cfg, q, k, i)
                dv_ref[0:n, :] += lax.dot_general(p.astype(bf16), dob, tn, preferred_element_type=f32)
                dp = lax.dot_general(dob, v, nt, preferred_element_type=f32)
                ds = (p * (dp - jnp.sum(dp * p, axis=-1, keepdims=True)) * ((HEAD + ROPE) ** -0.5)).astype(bf16)
                dq_ref[...] = jnp.dot(ds, k, preferred_element_type=f32)
                dk_ref[0:n, :] += lax.dot_general(ds, q, tn, preferred_element_type=f32)

    return pl.pallas_call(
        body, name="mla_attn_bwd", grid=(MH, S // TQ),
        in_specs=[qs(2 * HEAD), ks(2 * HEAD), ks(HEAD), hs(cfg.o_mg), hs(0), hs(0)],
        out_specs=(qs(2 * HEAD), ks(2 * HEAD), ks(HEAD), hs(0)),
        out_shape=(jax.ShapeDtypeStruct((MH, S, 2 * HEAD), f32), jax.ShapeDtypeStruct((MH, S, 2 * HEAD), f32),
                   jax.ShapeDtypeStruct((MH, S, HEAD), f32), jax.ShapeDtypeStruct((S, cfg.MW), bf16)),
        compiler_params=_cparams(("parallel", "arbitrary")),
    )(q3, k3, v3, proj, o, dy)


def _pick_rows(R, bytes_per_row):
    if R * bytes_per_row <= MM_BUDGET:
        return R
    best = None
    for t in range(16, R, 16):
        if R % t == 0 and t * bytes_per_row <= MM_BUDGET:
            best = t
    assert best is not None, (R, bytes_per_row)
    return best


def _adamw(w, g, m, v, name="adamw"):
    R, C = w.shape
    tr = _pick_rows(R, C * 4 * 7 * 2)
    c1 =1.0 - ADAM_B1 ** ADAM_STEP
    c2 = 1.0 - ADAM_B2 ** ADAM_STEP

    def body(w_ref, g_ref, m_ref, v_ref, d_ref, mo_ref, vo_ref):
        g = g_ref[...]
        m = ADAM_B1 * m_ref[...] + (1.0 - ADAM_B1) * g
        v = ADAM_B2 * v_ref[...] + (1.0 - ADAM_B2) * jnp.square(g)
        d_ref[...] = -ADAM_LR * ((m / c1) / (jnp.sqrt(v / c2) + ADAM_EPS) + ADAM_WD * w_ref[...])
        mo_ref[...] = m
        vo_ref[...] = v

    spec = pl.BlockSpec((tr, C), lambda i: (i, 0))
    return pl.pallas_call(
        body, name=name, grid=(R // tr,), in_specs=[spec] * 4, out_specs=(spec,) * 3,
        out_shape=(jax.ShapeDtypeStruct((R, C), f32),) * 3,
        compiler_params=_cparams(("parallel",)),
    )(w, g, m, v)


def _adamw_big(w, m, v, l0, mines, others, core, after, name, half_cols=False, prev=None):
    L, R, C = w.shape
    nl = len(mines)
    n_prev = 1 if prev is None else 5
    prev = (after,) + tuple(prev or ())
    hr, hc = (R, C // 2) if half_cols else (R // 2, C)
    tr = _pick_rows(hr, hc * 4 * (7 + 2 * nl) * 2)
    nt = hr // tr
    c1 = 1.0 - ADAM_B1 ** ADAM_STEP
    c2 = 1.0 - ADAM_B2 ** ADAM_STEP

    def body(core_ref, w_ref, m_ref, v_ref, *rest):
        g_refs, (go_ref, d_ref, mo_ref, vo_ref) = rest[:2 * nl], rest[2 * nl + n_prev:]
        l, h = pl.program_id(0), pl.program_id(1)
        own = h == core_ref[0]
        g = jnp.where(own, g_refs[0][...], g_refs[nl][...])
        for k in range(1, nl):
            g = jnp.where(l == k, jnp.where(own, g_refs[k][...], g_refs[nl + k][...]), g)
        m = ADAM_B1 * m_ref[...] + (1.0 - ADAM_B1) * g
        v = ADAM_B2 * v_ref[...] + (1.0 - ADAM_B2) * jnp.square(g)
        go_ref[...] = g
        d_ref[...] = -ADAM_LR * ((m / c1) / (jnp.sqrt(v / c2) + ADAM_EPS) + ADAM_WD * w_ref[...])
        mo_ref[...] = m
        vo_ref[...] = v

    if half_cols:
        lay = pl.BlockSpec((None, tr, hc), lambda l, h, i, core_ref: (l0 + l, i, h))
    else:
        lay = pl.BlockSpec((None, tr, hc), lambda l, h, i, core_ref: (l0 + l, h * nt + i, 0))
    gspec = lambda k: pl.BlockSpec((tr, hc), lambda l, h, i, core_ref: (jnp.where(l == k, i, 0), 0))
    return pl.pallas_call(
        body, name=name,
        grid_spec=pltpu.PrefetchScalarGridSpec(
            num_scalar_prefetch=1, grid=(nl, 2, nt),
            in_specs=[lay, lay, lay] + [gspec(k) for k in range(nl)] * 2 + _hbm_specs(n_prev), out_specs=(lay,) * 4),
        out_shape=(jax.ShapeDtypeStruct((L, R, C), f32),) * 4,
        input_output_aliases={5 + 2 * nl + k: k for k in range(n_prev - 1)},
        compiler_params=_cparams(("arbitrary", "arbitrary", "arbitrary")),
    )(core, w, m, v, *mines, *others, *prev)


def _sum_blocks(x, out_dtype, name):
    n, R, C = x.shape
    tr = _pick_rows(R, C * 4 * (n + 1) * 2)

    def body(x_ref, o_ref):
        acc = x_ref[0].astype(f32)
        for k in range(1, n):
            acc = acc + x_ref[k].astype(f32)
        o_ref[...] = acc.astype(o_ref.dtype)

    return pl.pallas_call(
        body, name=name, grid=(R // tr,),
        in_specs=[pl.BlockSpec((n, tr, C), lambda i: (0, i, 0))], out_specs=pl.BlockSpec((tr, C), lambda i: (i, 0)),
        out_shape=jax.ShapeDtypeStruct((R, C), out_dtype),
        compiler_params=_cparams(("parallel",)),
    )(x)


def _hbm_specs(n):
    return [pl.BlockSpec(memory_space=pl.ANY)] * n


def _row_map(cfg):
    nc, k0 = cfg.IN_WIDTH // 4, cfg.o_mg

    def padded(o):
        return o if o < k0 else (cfg.o_mkr + o - k0 if o < k0 + ROPE else o - ROPE)

    cuts = {0, nc}
    for q in range(4):
        cuts |= {b - q * nc for b in (k0, k0 + ROPE) if q * nc < b < (q + 1) * nc}
    cuts = sorted(cuts)
    return [((l0, l1 - l0), tuple(padded(q * nc + l0) for q in range(4))) for l0, l1 in zip(cuts[:-1], cuts[1:])]


def _chip_start(q, starts):
    st = starts[0]
    for i in range(1, 4):
        st = jnp.where(q == i, starts[i], st)
    return pl.multiple_of(st, 16)


def _allgather8(shards, name, cfg=None, zeros=None):
    na = len(shards)
    rmap = _row_map(cfg) if cfg is not None else []
    npc = max(len(rmap), 1)

    def body(*refs):
        x_refs, out_refs = refs[:na], refs[na + 1:2 * na + 1]
        send_sems, recv_sems, local_sems = refs[2 * na + 1:]
        x, y, c = lax.axis_index("x"), lax.axis_index("y"), lax.axis_index("c")
        me, sibling = (x, y, c), (x, y, 1 - c)
        chips = [(1 - x, y), (x, 1 - y), (1 - x, 1 - y)]

        def wins(a, px, py, pc):
            m, n = shards[a].shape
            if a == 0 and rmap:
                cols = pl.ds(pl.multiple_of(pc * n, n), n)
                return [(pl.ds(l0, cnt), out_refs[0].at[pl.ds(_chip_start(2 * px + py, starts), cnt), cols])
                        for (l0, cnt), starts in rmap]
            return [(pl.ds(0, m), out_refs[a].at[pl.ds((4 * px + 2 * py + pc) * m, m), :])]

        def copies(a, k, block, to, from_x):
            return [pltpu.make_async_remote_copy(
                src_ref=x_refs[a].at[rows, :] if from_x else win, dst_ref=win, send_sem=send_sems.at[a, k, p],
                recv_sem=recv_sems.at[a, k, p], device_id=to, device_id_type=MESH)
                for p, (rows, win) in enumerate(wins(a, *block))]

        mine = [pltpu.make_async_copy(x_refs[a].at[rows, :], win, local_sems.at[a, p])
                for a in range(na) for p, (rows, win) in enumerate(wins(a, *me))]
        if zeros is not None:
            nz = zeros.shape[0]
            mine.append(pltpu.make_async_copy(refs[na], out_refs[0].at[pl.ds(cfg.NP - nz, nz), :], local_sems.at[0, npc]))
        for cp in mine:
            cp.start()
        first = []
        for a in range(na):
            first += copies(a, 0, me, sibling, True)
            for j, chip in enumerate(chips):
                first += copies(a, 1 + j, me, (*chip, c), True)
        for cp in first:
            cp.start()
        passed = []
        for j, chip in enumerate(chips):
            for a in range(na):
                for cp in copies(a, 1 + j, (*chip, c), me, False):
                    cp.wait_recv()
                fwd = copies(a, 4 + j, (*chip, c), sibling, False)
                for cp in fwd:
                    cp.start()
                passed += fwd
        for a in range(na):
            for cp in copies(a, 0, sibling, me, False):
                cp.wait_recv()
        for j, chip in enumerate(chips):
            for a in range(na):
                for cp in copies(a, 4 + j, (*chip, 1 - c), me, False):
                    cp.wait_recv()
        for cp in first + passed:
            cp.wait_send()
        for cp in mine:
            cp.wait()

    out_shape = [jax.ShapeDtypeStruct((N_DEV * s.shape[0], s.shape[1]), s.dtype) for s in shards]
    if rmap:
        out_shape[0] = jax.ShapeDtypeStruct((cfg.NP, cfg.D), shards[0].dtype)
    z = zeros if zeros is not None else jnp.zeros((SUBLANES, LANES), f32)
    return pl.pallas_call(
        body, name=name, out_shape=out_shape,
        in_specs=_hbm_specs(na + 1), out_specs=_hbm_specs(na),
        scratch_shapes=[pltpu.SemaphoreType.DMA((na, 7, npc)), pltpu.SemaphoreType.DMA((na, 7, npc)),
                        pltpu.SemaphoreType.DMA((na, npc + 1))],
    )(*shards, z)


_SEM = pl.BlockSpec(memory_space=pltpu.SEMAPHORE)
_HBM = pl.BlockSpec(memory_space=pltpu.HBM)
_EFFECT = pltpu.SideEffectType.DATAFLOW_SIDE_EFFECTING


def _split_start(srcs, lands, after, plan, n, name):
    bufs = list(srcs) + list(lands)
    nb, ns = len(bufs), len(srcs)

    def body(*refs):
        send_sems, recv_sems = refs[nb + 1], refs[nb + 2]
        for k, (src, dst, _, dev) in enumerate(plan(refs[:ns], refs[ns:nb])):
            pltpu.make_async_remote_copy(src_ref=src, dst_ref=dst, send_sem=send_sems.at[k], recv_sem=recv_sems.at[k],
                                         device_id=dev, device_id_type=MESH).start()
        refs[-1][...] = jnp.zeros_like(refs[-1])

    out = pl.pallas_call(
        body, name=name,
        out_shape=(pltpu.SemaphoreType.DMA((n,)), pltpu.SemaphoreType.DMA((n,)), *[pltpu.HBM(b.shape, b.dtype) for b in bufs],
                   jax.ShapeDtypeStruct((SUBLANES, LANES), f32)),
        in_specs=[_HBM] * nb + [pl.BlockSpec(memory_space=pl.ANY)],
        out_specs=(_SEM, _SEM, *[_HBM] * nb, pl.BlockSpec(memory_space=pltpu.VMEM)),
        input_output_aliases={i: 2 + i for i in range(nb)},
        compiler_params=pltpu.CompilerParams(has_side_effects=_EFFECT),
    )(*[pltpu.with_memory_space_constraint(b, pltpu.HBM) for b in bufs], after)
    return out[0], out[1], list(out[2:2 + ns]), list(out[2 + ns:2 + nb]), out[-1]


def _split_wait(srcs, lands, send_sems, recv_sems, after, plan, name):
    bufs = list(srcs) + list(lands)
    nb, ns = len(bufs), len(srcs)

    def body(*refs):
        send, recv = refs[nb], refs[nb + 1]
        for k, (src, _, dst, dev) in enumerate(plan(refs[:ns], refs[ns:nb])):
            cp = pltpu.make_async_remote_copy(src_ref=src, dst_ref=dst, send_sem=send.at[k], recv_sem=recv.at[k],
                                              device_id=dev, device_id_type=MESH)
            cp.wait_send()
            cp.wait_recv()

    out = pl.pallas_call(
        body, name=name, out_shape=tuple(pltpu.HBM(b.shape, b.dtype) for b in bufs),
        in_specs=[_HBM] * nb + [_SEM, _SEM, pl.BlockSpec(memory_space=pl.ANY)], out_specs=tuple([_HBM] * nb),
        input_output_aliases={i: i for i in range(nb)},
        compiler_params=pltpu.CompilerParams(has_side_effects=_EFFECT),
    )(*bufs, send_sems, recv_sems, after)
    return list(out[:ns]), list(out[ns:])


def _weight_windows(cfg, shards, out_refs, px, py, pc):
    rmap = _row_map(cfg)
    m, n = shards[0].shape
    cols = pl.ds(pl.multiple_of(pc * n, n), n)
    wins = [[(pl.ds(l0, cnt), out_refs[0].at[pl.ds(_chip_start(2 * px + py, starts), cnt), cols]) for (l0, cnt), starts in rmap]]
    for a in range(1, len(shards)):
        m = shards[a].shape[0]
        wins.append([(pl.ds(0, m), out_refs[a].at[pl.ds((4 * px + 2 * py + pc) * m, m), :])])
    return wins


def _gather_shapes(cfg, shards):
    return [jax.ShapeDtypeStruct((cfg.NP, cfg.D), shards[0].dtype)] + \
           [jax.ShapeDtypeStruct((N_DEV * s.shape[0], s.shape[1]), s.dtype) for s in shards[1:]]


def _gather_plan(cfg, shards):
    def plan(x_refs, land_refs):
        x, y, c = lax.axis_index("x"), lax.axis_index("y"), lax.axis_index("c")
        mine = _weight_windows(cfg, shards, land_refs, x, y, c)
        out = []
        for peer in [(x, y, 1 - c), (1 - x, y, c), (x, 1 - y, c), (1 - x, 1 - y, c)]:
            theirs = _weight_windows(cfg, shards, land_refs, *peer)
            for a in range(len(shards)):
                for (rows, win), (_, win_in) in zip(mine[a], theirs[a]):
                    out.append((x_refs[a].at[rows, :], win, win_in, peer))
        return out
    return plan


def _gather_finish(cfg, shards, lands, name):
    na = len(shards)
    rmap = _row_map(cfg)
    npc = len(rmap)
    nz = cfg.NP - cfg.o_mkr - ROPE

    def body(*refs):
        x_refs, out_refs = refs[:na], refs[2 * na:3 * na]
        stage, zbuf = refs[3 * na:4 * na], refs[4 * na]
        send_sems, recv_sems, local_sems = refs[4 * na + 1:]
        x, y, c = lax.axis_index("x"), lax.axis_index("y"), lax.axis_index("c")
        sibling = (x, y, 1 - c)
        chips = [(1 - x, y), (x, 1 - y), (1 - x, 1 - y)]
        load = [pltpu.make_async_copy(x_refs[a], stage[a], local_sems.at[a, npc]) for a in range(na)]
        for cp in load:
            cp.start()
        passed = []
        for j, chip in enumerate(chips):
            wins = _weight_windows(cfg, shards, out_refs, *chip, c)
            for a in range(na):
                passed += [pltpu.make_async_remote_copy(src_ref=win, dst_ref=win, send_sem=send_sems.at[a, j, p],
                                                        recv_sem=recv_sems.at[a, j, p], device_id=sibling, device_id_type=MESH)
                           for p, (_, win) in enumerate(wins[a])]
        for cp in passed:
            cp.start()
        zbuf[...] = jnp.zeros_like(zbuf)
        for cp in load:
            cp.wait()
        own = _weight_windows(cfg, shards, out_refs, x, y, c)
        store = [pltpu.make_async_copy(stage[a].at[rows, :], win, local_sems.at[a, p])
                 for a in range(na) for p, (rows, win) in enumerate(own[a])]
        store.append(pltpu.make_async_copy(zbuf, out_refs[0].at[pl.ds(cfg.NP - nz, nz), :], local_sems.at[0, npc + 1]))
        for cp in store:
            cp.start()
        for j, chip in enumerate(chips):
            wins = _weight_windows(cfg, shards, out_refs, *chip, 1 - c)
            for a in range(na):
                for p, (_, win) in enumerate(wins[a]):
                    pltpu.make_async_remote_copy(src_ref=win, dst_ref=win, send_sem=send_sems.at[a, j, p],
                                                 recv_sem=recv_sems.at[a, j, p], device_id=sibling,
                                                 device_id_type=MESH).wait_recv()
        for cp in passed:
            cp.wait_send()
        for cp in store:
            cp.wait()

    return pl.pallas_call(
        body, name=name, out_shape=_gather_shapes(cfg, shards),
        in_specs=_hbm_specs(2 * na), out_specs=_hbm_specs(na),
        input_output_aliases={na + a: a for a in range(na)},
        scratch_shapes=[pltpu.VMEM(s.shape, s.dtype) for s in shards] + [pltpu.VMEM((nz, cfg.D), shards[0].dtype)]
        + [pltpu.SemaphoreType.DMA((na, 3, npc)), pltpu.SemaphoreType.DMA((na, 3, npc)), pltpu.SemaphoreType.DMA((na, npc + 2))],
        compiler_params=pltpu.CompilerParams(vmem_limit_bytes=VMEM_LIMIT),
    )(*shards, *lands)


def _gather_weights_start(cfg, shards, after):
    lands = [lax.empty(s.shape, s.dtype) for s in _gather_shapes(cfg, shards)]
    n = 4 * (len(_row_map(cfg)) + len(shards) - 1)
    return _split_start(shards, lands, after, _gather_plan(cfg, shards), n, "gather_w_start")


def _gather_weights_end(cfg, shards, started, after):
    send_sems, recv_sems, srcs, lands, _ = started
    srcs, lands = _split_wait(srcs, lands, send_sems, recv_sems, after, _gather_plan(cfg, shards), "gather_w_wait")
    return _gather_finish(cfg, srcs, lands, "gather_w_finish")


def _send_sibling(arrays, name):
    na = len(arrays)

    def body(*refs):
        x_refs, out_refs = refs[:na], refs[na:2 * na]
        send_sems, recv_sems = refs[2 * na:]
        sibling = (lax.axis_index("x"), lax.axis_index("y"), 1 - lax.axis_index("c"))
        cps = [pltpu.make_async_remote_copy(src_ref=x_refs[a], dst_ref=out_refs[a], send_sem=send_sems.at[a],
                                            recv_sem=recv_sems.at[a], device_id=sibling, device_id_type=MESH)
               for a in range(na)]
        for cp in cps:
            cp.start()
        for cp in cps:
            cp.wait()

    return pl.pallas_call(
        body, name=name, out_shape=[jax.ShapeDtypeStruct(x.shape, x.dtype) for x in arrays],
        in_specs=_hbm_specs(na), out_specs=_hbm_specs(na),
        scratch_shapes=[pltpu.SemaphoreType.DMA((na,)), pltpu.SemaphoreType.DMA((na,))],
    )(*arrays)


def _slot_pairs(cfg, p_refs, slot_refs, a, to_chip, slot):
    if a == 0:
        return [(p_refs[0].at[pl.ds(_chip_start(to_chip, starts), cnt), :], slot_refs[0].at[slot, pl.ds(l0, cnt), :])
                for (l0, cnt), starts in _row_map(cfg)]
    return [(p_refs[a].at[to_chip], slot_refs[a].at[slot])]


def _scatter_plan(cfg, na):
    def plan(p_refs, slot_refs):
        x, y, c = lax.axis_index("x"), lax.axis_index("y"), lax.axis_index("c")
        mychip = 2 * x + y
        out = []
        for cx, cy in [(1 - x, y), (x, 1 - y), (1 - x, 1 - y)]:
            q = 2 * cx + cy
            for a in range(na):
                for (src, dst), (_, dst_in) in zip(_slot_pairs(cfg, p_refs, slot_refs, a, q, mychip),
                                                   _slot_pairs(cfg, p_refs, slot_refs, a, mychip, q)):
                    out.append((src, dst, dst_in, (cx, cy, c)))
        return out
    return plan


def _slot_shapes(cfg, parts):
    return [jax.ShapeDtypeStruct((4, cfg.IN_WIDTH // 4, parts[0].shape[1]), parts[0].dtype)] + \
           [jax.ShapeDtypeStruct(p.shape, p.dtype) for p in parts[1:]]


def _place_own(cfg, parts, slots, name):
    na = len(parts)
    npc = len(_row_map(cfg))
    shapes = _slot_shapes(cfg, parts)

    def body(*refs):
        p_refs, out_refs = refs[:na], refs[2 * na:3 * na]
        stage, sems = refs[3 * na:4 * na], refs[4 * na]
        mychip = 2 * lax.axis_index("x") + lax.axis_index("y")
        moves = []
        for a in range(na):
            for p, (src, dst) in enumerate(_slot_pairs(cfg, p_refs, out_refs, a, mychip, mychip)):
                buf = stage[a].at[pl.ds(*_row_map(cfg)[p][0]), :] if a == 0 else stage[a]
                moves.append((pltpu.make_async_copy(src, buf, sems.at[a, p]), pltpu.make_async_copy(buf, dst, sems.at[a, npc + p])))
        for load, _ in moves:
            load.start()
        for load, store in moves:
            load.wait()
            store.start()
        for _, store in moves:
            store.wait()

    return pl.pallas_call(
        body, name=name, out_shape=shapes, in_specs=_hbm_specs(2 * na), out_specs=_hbm_specs(na),
        input_output_aliases={na + a: a for a in range(na)},
        scratch_shapes=[pltpu.VMEM(s.shape[1:], s.dtype) for s in shapes] + [pltpu.SemaphoreType.DMA((na, 2 * npc))],
        compiler_params=pltpu.CompilerParams(vmem_limit_bytes=VMEM_LIMIT),
    )(*parts, *slots)


def _pair_exchange(cfg, g_in_t, grads, name):
    na = 1 + len(grads)
    hd = cfg.D // 2

    def body(*refs):
        g_refs, out_refs = refs[:na], refs[na:2 * na]
        send_sems, recv_sems = refs[2 * na:]
        x, y, c = lax.axis_index("x"), lax.axis_index("y"), lax.axis_index("c")
        cps = [pltpu.make_async_remote_copy(
            src_ref=g_refs[0].at[:, pl.ds(pl.multiple_of((1 - c) * hd, hd), hd)], dst_ref=out_refs[0],
            send_sem=send_sems.at[0, 0], recv_sem=recv_sems.at[0, 0], device_id=(x, y, 1 - c), device_id_type=MESH)]
        for a in range(1, na):
            cps += [pltpu.make_async_remote_copy(src_ref=g_refs[a].at[q, 1 - c], dst_ref=out_refs[a].at[q],
                                                 send_sem=send_sems.at[a, q], recv_sem=recv_sems.at[a, q],
                                                 device_id=(x, y, 1 - c), device_id_type=MESH) for q in range(4)]
        for cp in cps:
            cp.start()
        for cp in cps:
            cp.wait()

    out_shape = [jax.ShapeDtypeStruct((cfg.NP, hd), g_in_t.dtype)] + \
                [jax.ShapeDtypeStruct((4,) + g.shape[2:], g.dtype) for g in grads]
    return pl.pallas_call(
        body, name=name, out_shape=out_shape, in_specs=_hbm_specs(na), out_specs=_hbm_specs(na),
        scratch_shapes=[pltpu.SemaphoreType.DMA((na, 4)), pltpu.SemaphoreType.DMA((na, 4))],
    )(g_in_t, *grads)


def _add_half(g, got, core, name):
    if g.ndim == 2:
        R, hd = got.shape
        tr = _pick_rows(R, hd * 4 * 3 * 2)
        grid = (R // tr,)
        g_spec = pl.BlockSpec((tr, hd), lambda i, core_ref: (i, core_ref[0]))
        o_spec = pl.BlockSpec((tr, hd), lambda i, core_ref: (i, 0))
    else:
        _, hr, nc = got.shape
        tr = _pick_rows(hr, nc * 4 * 3 * 2)
        grid = (4, hr // tr)
        g_spec = pl.BlockSpec((None, None, tr, nc), lambda q, i, core_ref: (q, core_ref[0], i, 0))
        o_spec = pl.BlockSpec((None, tr, nc), lambda q, i, core_ref: (q, i, 0))

    def body(core_ref, g_ref, got_ref, o_ref):
        o_ref[...] = (g_ref[...].astype(f32) + got_ref[...].astype(f32)).astype(o_ref.dtype)

    return pl.pallas_call(
        body, name=name,
        grid_spec=pltpu.PrefetchScalarGridSpec(num_scalar_prefetch=1, grid=grid, in_specs=[g_spec, o_spec], out_specs=o_spec),
        out_shape=jax.ShapeDtypeStruct(got.shape, bf16),
        compiler_params=_cparams(("arbitrary",) * len(grid)),
    )(core, g, got)


def _reduce_scatter_start(cfg, g_in_t, grads, after):
    core = lax.axis_index("c").astype(jnp.int32).reshape(1)
    got = _pair_exchange(cfg, g_in_t, grads, "rs_pair")
    part = [_add_half(g, h, core, "rs_add_pair") for g, h in zip([g_in_t] + list(grads), got)]
    slots = [lax.empty(s.shape, s.dtype) for s in _slot_shapes(cfg, part)]
    n = 3 * (len(_row_map(cfg)) + len(part) - 1)
    return _split_start(part, slots, after, _scatter_plan(cfg, len(part)), n, "rs_chips_start")


def _reduce_scatter_end(cfg, started, after):
    send_sems, recv_sems, parts, slots, _ = started
    parts, slots = _split_wait(parts, slots, send_sems, recv_sems, after, _scatter_plan(cfg, len(parts)), "rs_chips_wait")
    slots = _place_own(cfg, parts, slots, "rs_own")
    mine = [_sum_blocks(s, f32, "rs_add_chips") for s in slots]
    return mine, _send_sibling(mine, "rs_halves")


def _big_weights(cfg):
    return (("mla_w_uq", cfg.QL, cfg.QW, 1), ("mla_w_ukv", cfg.KL, cfg.KVW, 1),
            ("w_branch", cfg.RW + cfg.LW + cfg.MW, cfg.D, 0), ("w_out", cfg.D, cfg.D, 0))


def _half_shapes(cfg):
    out = []
    for _, r, c, ax in _big_weights(cfg):
        out.append((r // 2, c // 4) if ax == 1 else (r // 8, c))
    return out


def _my_halves(cfg, W, l, c):
    hd = cfg.D // 2
    out = [lax.dynamic_slice_in_dim(W["w_in"][l].T, c * hd, hd, axis=1).astype(bf16)]
    for (name, *_), (hr, nc) in zip(_big_weights(cfg), _half_shapes(cfg)):
        out.append(lax.dynamic_slice_in_dim(W[name][l], c * hr, hr, axis=0).astype(bf16))
    return out


def _uq_split(cfg, w):
    hw = HEAD + ROPE
    return jnp.concatenate([w[:, h * hw:h * hw + HEAD] for h in range(cfg.MH)]
                           + [w[:, h * hw + HEAD:(h + 1) * hw] for h in range(cfg.MH)], axis=1)


def _uq_join(cfg, g):
    n = cfg.MH * HEAD
    parts = []
    for h in range(cfg.MH):
        parts += [g[:, h * HEAD:(h + 1) * HEAD], g[:, n + h * ROPE:n + (h + 1) * ROPE]]
    return jnp.concatenate(parts, axis=1)


def _col_blocks(g):
    nc = g.shape[1] // 4
    return jnp.stack([g[:, q * nc:(q + 1) * nc] for q in range(4)])


def _row_pack(parts):
    rows = []
    for p in parts:
        r = p.reshape(-1, LANES)
        pad = -r.shape[0] % SUBLANES
        rows.append(jnp.concatenate([r, jnp.zeros((pad, LANES), r.dtype)], axis=0) if pad else r)
    return jnp.concatenate(rows, axis=0)


def _row_unpack(packed, like):
    out, off = [], 0
    for p in like:
        n = p.size // LANES
        out.append(packed[off:off + n].reshape(p.shape))
        off += -(-n // SUBLANES) * SUBLANES
    return out


def _prep_layer(cfg, full, small):
    w_in_t, w_uq, w_ukv, w_branch, w_out = full
    RW, LW = cfg.RW, cfg.LW
    P = dict(small)
    P["w_in_t"] = w_in_t
    P["w_uq"] = _uq_split(cfg, jnp.concatenate(list(w_uq.reshape(4, cfg.QL, -1)), axis=1))
    P["w_ukv"] = jnp.concatenate(list(w_ukv.reshape(4, cfg.KL, -1)), axis=1)
    P["wb"] = (w_branch[:RW], w_branch[RW:RW + LW], w_branch[RW + LW:])
    P["w_out"] = w_out
    return P


def _layer_fwd(cfg, x, mod, P, T):
    h = _prenorm_fwd(cfg, x, mod, P["norm_pre"])
    proj = _mm(h, P["w_in_t"], f32, "mm_proj", mode="nt")
    y_ret = _ret_fwd(cfg, proj, P["ret_gn"], T["cos_r"], T["sin_r"], T["ret_consts"])
    a, b = _lru_gates(cfg, proj, P["lru_conv_w"], P["lru_conv_b"], P["lru_wa"], P["lru_ba"], P["lru_wx"], P["lru_bx"],
                      P["lru_lambda"])
    hl, y_lru = _lru_scan_fwd(cfg, proj, a, b)
    qn, kn = _mla_norm(cfg, proj, P["mla_q_norm"], P["mla_kv_norm"])
    q = _mm(qn, P["w_uq"], f32, "mm_uq")
    kv = _mm(kn, P["w_ukv"], f32, "mm_ukv")
    q3, k3, v3 = _mla_pack(cfg, proj, q, kv, T["cos_q"], T["sin_q"], T["cos_k"], T["sin_k"])
    o, y_mla = _mla_attn_fwd(cfg, proj, q3, k3, v3)
    ys = (y_ret, y_lru, y_mla)
    us = tuple(_mm(yb, wb, f32, "mm_branch") for yb, wb in zip(ys, P["wb"]))
    merged = _merge_fwd(cfg, proj, *us)
    y = _mm(merged, P["w_out"], f32, "mm_out")
    out = _postnorm_fwd(cfg, x, y, mod, P["norm_post"])
    R = dict(x=x, h=h, proj=proj, ys=ys, a=a, hl=hl, qn=qn, kn=kn, q3=q3, k3=k3, v3=v3, o=o, us=us, merged=merged, y=y)
    return out, R


def _layer_bwd(cfg, dout, R, mod, P, T):
    proj = R["proj"]
    dy, s_post = _postnorm_bwd(cfg, dout, R["y"], mod, P["norm_post"])
    dmerged = _mm(dy, P["w_out"], f32, "mm_dmerged", mode="nt")
    g_out = _mm(R["merged"], dy, bf16, "mm_gw_out", mode="tn")
    du0, du1, du2, dlog = _merge_bwd(cfg, proj, dmerged, *R["us"])
    dus = (du0, du1, du2)
    dys = tuple(_mm(du, wb, f32, "mm_dbranch", mode="nt") for du, wb in zip(dus, P["wb"]))
    g_branch = jnp.concatenate([_mm(yb, du, bf16, "mm_gw_branch", mode="tn") for yb, du in zip(R["ys"], dus)], axis=0)
    drq, drk, drv, drg, dgn = _ret_bwd(cfg, proj, dys[0], P["ret_gn"], T["cos_r"], T["sin_r"], T["ret_consts"])
    da, db, dlg = _lru_scan_bwd(cfg, proj, R["a"], R["hl"], dys[1])
    dxc, dwa, dwx, s_lru = _lru_gates_bwd(cfg, proj, da, db, P["lru_conv_w"], P["lru_conv_b"], P["lru_wa"], P["lru_ba"],
                                          P["lru_wx"], P["lru_bx"], P["lru_lambda"])
    dlx, s_conv = _lru_conv_bwd(cfg, proj, dxc, P["lru_conv_w"])
    dq3, dk3, dv3, dmg = _mla_attn_bwd(cfg, proj, R["q3"], R["k3"], R["v3"], R["o"], dys[2])
    dq, dkv, dmkr = _mla_unpack_bwd(cfg, dq3, dk3, dv3, T["cos_q"], T["sin_q"], T["cos_k"], T["sin_k"])
    dqn = _mm(dq, P["w_uq"], f32, "mm_dqn", mode="nt")
    dkn = _mm(dkv, P["w_ukv"], f32, "mm_dkn", mode="nt")
    g_uq = _uq_join(cfg, _mm(R["qn"], dq, bf16, "mm_gw_uq", mode="tn"))
    g_ukv = _mm(R["kn"], dkv, bf16, "mm_gw_ukv", mode="tn")
    dmq, dmkv, s_q, s_k = _mla_norm_bwd(cfg, proj, dqn, dkn, P["mla_q_norm"], P["mla_kv_norm"])
    dproj = jnp.concatenate([drq, drk, drv, drg, dlx, dlg, dmq, dmkv, dmg, dlog, dmkr,
                             jnp.zeros((cfg.S, cfg.NP - cfg.o_mkr - HEAD), bf16)], axis=1)
    dh = _mm(dproj, P["w_in_t"], f32, "mm_dh")
    g_in_t = _mm(dproj, R["h"], bf16, "mm_gw_in", mode="tn", tm=512)
    dx, s_pre = _prenorm_bwd(cfg, R["x"], dh, dout, mod, P["norm_pre"])
    big = [_col_blocks(g_uq), _col_blocks(g_ukv), g_branch, g_out]
    big = (g_in_t, [g.reshape(4, 2, hr, nc) for g, (hr, nc) in zip(big, _half_shapes(cfg))])
    small = dict(norm_pre=s_pre[2:3], norm_post=s_post[1:2], ret_gn=dgn, lru_conv_w=s_conv[0:CONV], lru_conv_b=s_conv[CONV:CONV + 1],
                 lru_wa=dwa, lru_ba=s_lru[0:1], lru_wx=dwx, lru_bx=s_lru[1:2], lru_lambda=s_lru[2:3],
                 mla_q_norm=s_q[0:1], mla_kv_norm=s_k[0:1])
    dmod = jnp.concatenate([s_pre[0:1], s_pre[1:2], s_post[0:1]], axis=1)
    return dx, big, small, dmod


_SMALL = ("norm_pre", "norm_post", "ret_gn", "lru_conv_w", "lru_conv_b", "lru_wa", "lru_ba", "lru_wx", "lru_bx", "lru_lambda",
          "mla_q_norm", "mla_kv_norm")
_WEIGHTS = ("ada_w", "ada_b", "norm_pre", "norm_post", "w_in", "ret_gn", "lru_conv_w", "lru_conv_b", "lru_wa", "lru_ba", "lru_wx",
            "lru_bx", "lru_lambda", "mla_q_norm", "mla_w_uq", "mla_kv_norm", "mla_w_ukv", "w_branch", "w_out")


def _step(cfg, x, c, positions, W, target, M1, V1):
    L, D = cfg.L, cfg.D
    xi, yi, ci = lax.axis_index("x"), lax.axis_index("y"), lax.axis_index("c")
    chip = 2 * xi + yi
    me = 2 * chip + ci

    c8 = jnp.concatenate([c, jnp.zeros((SUBLANES - 1, D), f32)], axis=0)
    c_all = _allgather8([c8], "gather_c")[0].reshape(N_DEV, SUBLANES, D)[:, 0]
    mod_sh, c_act = _ada_fwd(cfg, c_all, W["ada_w"])
    n_sh = mod_sh.shape[2]
    mod_half = lax.dynamic_slice_in_dim(mod_sh, ci * (n_sh // 2), n_sh // 2, axis=2).reshape(L * N_DEV, n_sh // 2)
    mod_all = _allgather8([mod_half], "gather_mod")[0].reshape(N_DEV, L, N_DEV, n_sh // 2)
    mod_all = mod_all.transpose(1, 2, 0, 3).reshape(L, N_DEV, 3 * D)
    mods = lax.dynamic_index_in_dim(mod_all, me, axis=1, keepdims=False) + W["ada_b"]

    (cos_r, sin_r), (cos_m, sin_m) = _rope_tables(cfg, positions)
    T = dict(cos_r=cos_r, sin_r=sin_r, cos_q=jnp.tile(cos_m, (1, cfg.MH)), sin_q=jnp.tile(sin_m, (1, cfg.MH)),
             cos_k=jnp.tile(cos_m, (1, 2)), sin_k=jnp.tile(sin_m, (1, 2)), ret_consts=_ret_consts(cfg))

    Ps, Rs = [], []
    act = x[0]
    cw_all = _allgather8([_pad_rows(W["lru_conv_w"].reshape(L * CONV, -1) + 0.0 * mods[0, 0])], "gather_conv")[0]
    started = _gather_weights_start(cfg, _my_halves(cfg, W, 0, ci), cw_all)
    cw_rows = cw_all.shape[0] // N_DEV
    cw_all = cw_all.reshape(4, 2, cw_rows, -1)[:, 0, :L * CONV].transpose(1, 0, 2).reshape(L, CONV, cfg.LW)
    after = mods
    for l in range(L):
        gathered = _gather_weights_end(cfg, started[2], started, after)
        small = {k: (W[k][l] if W[k][l].ndim > 1 else W[k][l][None, :]) for k in _SMALL if k != "lru_conv_w"}
        P = _prep_layer(cfg, gathered, small)
        P["lru_conv_w"] = cw_all[l]
        Ps.append(P)
        mod = mods[l:l + 1]
        if l + 1 < L:
            started = _gather_weights_start(cfg, _my_halves(cfg, W, l + 1, ci), gathered[-1])
            mod = mod + started[4][0, 0]
        act, R = _layer_fwd(cfg, act, mod, P, T)
        Rs.append(R)
        after = act

    dact, lsum = _loss_head(cfg, act, target[0])
    loss = lax.psum(lsum[0, 0], ("x", "y", "c"))

    big_g = [None] * L
    small_g = [None] * L
    dmods = [None] * L
    pending = None
    for l in range(L - 1, -1, -1):
        mod = mods[l:l + 1]
        if pending is not None:
            mod = mod + pending[4][0, 0]
        dact, grads, small_g[l], dmods[l] = _layer_bwd(cfg, dact, Rs[l], mod, Ps[l], T)
        if pending is not None:
            big_g[l + 1] = _reduce_scatter_end(cfg, pending, dact)
        if l > 0:
            pending = _reduce_scatter_start(cfg, *grads, grads[1][-1])

    dmod = jnp.concatenate(dmods, axis=0)
    parts = [dmod] + [small_g[l][k] for l in range(L) for k in _SMALL]
    packed = _row_pack(parts)
    allf = _allgather8([packed], "gather_small")[0]
    pending = _reduce_scatter_start(cfg, *grads, allf)
    tok = pending[4][0, 0]
    allf = allf.reshape(N_DEV, packed.shape[0], LANES)
    summed = _row_unpack(_sum_blocks(allf, f32, "sum_small"), parts)
    gsm = {k: jnp.stack([summed[1 + l * len(_SMALL) + i].reshape(W[k].shape[1:] if k != "lru_conv_w" else (CONV, cfg.LW))
                         for l in range(L)]) for i, k in enumerate(_SMALL)}
    ncw = cfg.LW // 4
    gsm["lru_conv_w"] = lax.dynamic_slice_in_dim(gsm["lru_conv_w"], chip * ncw, ncw, axis=2)
    gsm["ada_b"] = summed[0]
    dmod_all = allf[:, :dmod.size // LANES].reshape(N_DEV, L, 3 * D)
    dmod_sh = lax.dynamic_slice_in_dim(dmod_all, chip * n_sh, n_sh, axis=2).transpose(1, 0, 2) + tok
    G = dict(gsm)
    G["ada_w"] = _ada_bwd(cfg, c_act.T, dmod_sh)
    delta, new_m, new_v = {}, {}, {}
    bigs = ("ada_w", "w_in") + tuple(name for name, *_ in _big_weights(cfg))
    shp = W["ada_w"].shape
    two = lambda a: a.reshape(-1, shp[-1])
    d, m_, v_ = _adamw(two(W["ada_w"]), two(G["ada_w"]), two(M1["ada_w"]), two(V1["ada_w"]), "adamw_ada_w")
    delta["ada_w"], new_m["ada_w"], new_v["ada_w"] = d.reshape(shp), m_.reshape(shp), v_.reshape(shp)
    smalls = [k for k in _WEIGHTS if k not in bigs]
    packs = [_row_pack([src[k] for k in smalls]) for src in (W, G, M1, V1)]
    packs[1] = packs[1] + tok
    outs = _adamw(*packs, "adamw_small")
    for dst, o in zip((delta, new_m, new_v), outs):
        for k, val in zip(smalls, _row_unpack(o, [W[k] for k in smalls])):
            dst[k] = val
    core = ci.astype(jnp.int32).reshape(1)
    tr_ = lambda a: a.transpose(0, 2, 1)

    def update(l0, l1, prev):
        res = {}
        for i, (name, *_) in enumerate(_big_weights(cfg)):
            res[name] = _adamw_big(W[name], M1[name], V1[name], l0, [big_g[l][0][i + 1] for l in range(l0, l1)],
                                   [big_g[l][1][i + 1] for l in range(l0, l1)], core, pending[4], "adamw_" + name,
                                   prev=prev and prev[name])
        res["w_in"] = _adamw_big(tr_(W["w_in"]), tr_(M1["w_in"]), tr_(V1["w_in"]), l0, [big_g[l][0][0] for l in range(l0, l1)],
                                 [big_g[l][1][0] for l in range(l0, l1)], core, pending[4], "adamw_w_in", half_cols=True,
                                 prev=prev and prev["w_in"])
        return res

    upper = update(1, L, None) if L > 1 else None
    big_g[0] = _reduce_scatter_end(cfg, pending, upper["w_in"][0] if upper else outs[0])
    res = update(0, 1, upper)
    for name, *_ in _big_weights(cfg):
        G[name], delta[name], new_m[name], new_v[name] = res[name]
    G["w_in"], delta["w_in"], new_m["w_in"], new_v["w_in"] = [tr_(o) for o in res["w_in"]]

    grad_x = dact[None]
    return (loss, grad_x, *[G[k] for k in _WEIGHTS], *[delta[k] for k in _WEIGHTS], *[new_m[k] for k in _WEIGHTS],
            *[new_v[k] for k in _WEIGHTS])


def _pad_rows(a):
    pad = -a.shape[0] % SUBLANES
    return jnp.concatenate([a, jnp.zeros((pad, a.shape[1]), a.dtype)], axis=0) if pad else a


def kernel(x, c, positions, ada_w, ada_b, norm_pre, norm_post, w_in, ret_gn, lru_conv_w, lru_conv_b, lru_wa, lru_ba, lru_wx, lru_bx, lru_lambda, mla_q_norm, mla_w_uq, mla_kv_norm, mla_w_ukv, w_branch, w_out, loss_target, m_ada_w, m_ada_b, m_norm_pre, m_norm_post, m_w_in, m_ret_gn, m_lru_conv_w, m_lru_conv_b, m_lru_wa, m_lru_ba, m_lru_wx, m_lru_bx, m_lru_lambda, m_mla_q_norm, m_mla_w_uq, m_mla_kv_norm, m_mla_w_ukv, m_w_branch, m_w_out, v_ada_w, v_ada_b, v_norm_pre, v_norm_post, v_w_in, v_ret_gn, v_lru_conv_w, v_lru_conv_b, v_lru_wa, v_lru_ba, v_lru_wx, v_lru_bx, v_lru_lambda, v_mla_q_norm, v_mla_w_uq, v_mla_kv_norm, v_mla_w_ukv, v_w_branch, v_w_out):
    W = dict(ada_w=ada_w, ada_b=ada_b, norm_pre=norm_pre, norm_post=norm_post, w_in=w_in, ret_gn=ret_gn, lru_conv_w=lru_conv_w,
             lru_conv_b=lru_conv_b, lru_wa=lru_wa, lru_ba=lru_ba, lru_wx=lru_wx, lru_bx=lru_bx, lru_lambda=lru_lambda,
             mla_q_norm=mla_q_norm, mla_w_uq=mla_w_uq, mla_kv_norm=mla_kv_norm, mla_w_ukv=mla_w_ukv, w_branch=w_branch, w_out=w_out)
    M1 = dict(ada_w=m_ada_w, ada_b=m_ada_b, norm_pre=m_norm_pre, norm_post=m_norm_post, w_in=m_w_in, ret_gn=m_ret_gn,
              lru_conv_w=m_lru_conv_w, lru_conv_b=m_lru_conv_b, lru_wa=m_lru_wa, lru_ba=m_lru_ba, lru_wx=m_lru_wx, lru_bx=m_lru_bx,
              lru_lambda=m_lru_lambda, mla_q_norm=m_mla_q_norm, mla_w_uq=m_mla_w_uq, mla_kv_norm=m_mla_kv_norm,
              mla_w_ukv=m_mla_w_ukv, w_branch=m_w_branch, w_out=m_w_out)
    V1 = dict(ada_w=v_ada_w, ada_b=v_ada_b, norm_pre=v_norm_pre, norm_post=v_norm_post, w_in=v_w_in, ret_gn=v_ret_gn,
              lru_conv_w=v_lru_conv_w, lru_conv_b=v_lru_conv_b, lru_wa=v_lru_wa, lru_ba=v_lru_ba, lru_wx=v_lru_wx, lru_bx=v_lru_bx,
              lru_lambda=v_lru_lambda, mla_q_norm=v_mla_q_norm, mla_w_uq=v_mla_w_uq, mla_kv_norm=v_mla_kv_norm,
              mla_w_ukv=v_mla_w_ukv, w_branch=v_w_branch, w_out=v_w_out)
    return _step(_CFG, x, c, positions, W, loss_target, M1, V1)
```

```python
from typing import NamedTuple

import numpy as np
import jax
import jax.numpy as jnp
from jax import lax
from jax.experimental import pallas as pl
from jax.experimental.pallas import tpu as pltpu

f32 = jnp.float32
bf16 = jnp.bfloat16

NORM_EPS = 1e-6
ROPE_BASE = 10000.0
CHUNK = 64
HEAD = 128
ROPE = 64
CONV = 4
LRU_C = 8.0
ADAM_LR, ADAM_B1, ADAM_B2, ADAM_EPS, ADAM_WD, ADAM_STEP = 0.001, 0.9, 0.999, 1e-08, 0.01, 10

LANES = 128
SUBLANES = 8
VMEM_LIMIT = 56 * 1024 * 1024
MM_BUDGET = 40 * 1024 * 1024
N_DEV = 8
MESH = pl.DeviceIdType.MESH


class Cfg(NamedTuple):
    D: int = 2048
    S: int = 2048
    L: int = 4
    H: int = 8
    NB: int = 8
    MH: int = 8
    QL: int = 512
    KL: int = 512
    TR: int = 256
    TQ: int = 256

    @property
    def RW(self): return self.H * HEAD
    @property
    def LW(self): return self.NB * HEAD
    @property
    def MW(self): return self.MH * HEAD
    @property
    def o_rk(self): return self.RW
    @property
    def o_rv(self): return 2 * self.RW
    @property
    def o_rg(self): return 3 * self.RW
    @property
    def o_lx(self): return 4 * self.RW
    @property
    def o_lg(self): return 4 * self.RW + self.LW
    @property
    def o_mq(self): return 4 * self.RW + 2 * self.LW
    @property
    def o_mkv(self): return self.o_mq + self.QL
    @property
    def o_mg(self): return self.o_mkv + self.KL
    @property
    def o_merge(self): return self.o_mg + self.MW
    @property
    def o_mkr(self): return self.o_merge + 3 * self.D
    @property
    def NP(self): return -(-(self.o_mkr + ROPE) // 512) * 512
    @property
    def IN_WIDTH(self): return self.o_mkr + ROPE
    @property
    def QW(self): return self.MH * (HEAD + ROPE)
    @property
    def KVW(self): return self.MH * 2 * HEAD


_CFG = Cfg()


def _cparams(sem=None):
    return pltpu.CompilerParams(dimension_semantics=sem, vmem_limit_bytes=VMEM_LIMIT)


def _sigmoid(x):
    return jax.nn.sigmoid(x)


def _silu(x):
    return x * _sigmoid(x)


def _dsilu(x):
    s = _sigmoid(x)
    return s * (1.0 + x * (1.0 - s))


def _slab(rows, width, off):
    assert off % width == 0
    return pl.BlockSpec((rows, width), lambda i, _c=off // width: (i, _c))


def _row(width):
    return pl.BlockSpec((1, width), lambda i: (0, 0))


def _mm(a, b, out_dtype=f32, name="mm", mode="nn", tm=None):
    (M, K) = a.shape if mode != "tn" else a.shape[::-1]
    (K2, N) = b.shape if mode != "nt" else b.shape[::-1]
    assert K == K2
    tn = N if N <= 2048 else 512
    tk = K if K <= 2048 else 512
    assert N % tn == 0 and K % tk == 0
    osz = jnp.dtype(out_dtype).itemsize
    if tm is None:
        tm = M
        while 2 * tm * tk * 2 + 2 * tk * tn * 2 + 2 * tm * tn * osz + tm * tn * 4 > MM_BUDGET and tm % 16 == 0:
            tm //= 2
    assert M % tm == 0
    nk = K // tk
    dims = {"nn": (((1,), (0,)), ((), ())), "nt": (((1,), (1,)), ((), ())), "tn": (((0,), (0,)), ((), ()))}[mode]

    def dot(a_ref, b_ref):
        return lax.dot_general(a_ref[...].astype(bf16), b_ref[...].astype(bf16), dims, preferred_element_type=f32)

    if nk == 1:
        def body(a_ref, b_ref, o_ref):
            o_ref[...] = dot(a_ref, b_ref).astype(o_ref.dtype)
        scratch = []
    else:
        def body(a_ref, b_ref, o_ref, acc_ref):
            k = pl.program_id(2)

            @pl.when(k == 0)
            def _():
                acc_ref[...] = jnp.zeros_like(acc_ref)

            acc_ref[...] += dot(a_ref, b_ref)

            @pl.when(k == nk - 1)
            def _():
                o_ref[...] = acc_ref[...].astype(o_ref.dtype)
        scratch = [pltpu.VMEM((tm, tn), f32)]

    a_spec = pl.BlockSpec((tk, tm), lambda i, j, k: (k, i)) if mode == "tn" else pl.BlockSpec((tm, tk), lambda i, j, k: (i, k))
    b_spec = pl.BlockSpec((tn, tk), lambda i, j, k: (j, k)) if mode == "nt" else pl.BlockSpec((tk, tn), lambda i, j, k: (k, j))
    return pl.pallas_call(
        body, name=name,
        grid=(M // tm, N // tn, nk),
        in_specs=[a_spec, b_spec],
        out_specs=pl.BlockSpec((tm, tn), lambda i, j, k: (i, j)),
        out_shape=jax.ShapeDtypeStruct((M, N), out_dtype),
        scratch_shapes=scratch,
        compiler_params=_cparams(("parallel", "parallel", "arbitrary")),
    )(a, b)


def _ada_fwd(cfg, c_all, ada_w):
    L, D, n = ada_w.shape
    tn = n // 2 if (n // 2) % LANES == 0 else n

    def body(c_ref, w_ref, o_ref, ca_ref):
        ca = _silu(c_ref[...])
        ca_ref[...] = ca
        o_ref[0] = jnp.dot(ca.astype(bf16), w_ref[0].astype(bf16), preferred_element_type=f32)

    return pl.pallas_call(
        body, name="ada_fwd", grid=(L, n // tn),
        in_specs=[pl.BlockSpec((N_DEV, D), lambda l, j: (0, 0)), pl.BlockSpec((1, D, tn), lambda l, j: (l, 0, j))],
        out_specs=(pl.BlockSpec((1, N_DEV, tn), lambda l, j: (l, 0, j)), pl.BlockSpec((N_DEV, D), lambda l, j: (0, 0))),
        out_shape=(jax.ShapeDtypeStruct((L, N_DEV, n), f32), jax.ShapeDtypeStruct((N_DEV, D), f32)),
        compiler_params=_cparams(("arbitrary", "arbitrary")),
    )(c_all, ada_w)


def _ada_bwd(cfg, c_act_t, dmod):
    L, _, n = dmod.shape
    D = c_act_t.shape[0]
    tn = n // 2 if (n // 2) % LANES == 0 else n

    def body(c_ref, d_ref, o_ref):
        o_ref[0] = jnp.dot(c_ref[...].astype(bf16), d_ref[0].astype(bf16), preferred_element_type=f32)

    return pl.pallas_call(
        body, name="ada_bwd", grid=(L, n // tn),
        in_specs=[pl.BlockSpec((D, N_DEV), lambda l, j: (0, 0)), pl.BlockSpec((1, N_DEV, tn), lambda l, j: (l, 0, j))],
        out_specs=pl.BlockSpec((1, D, tn), lambda l, j: (l, 0, j)),
        out_shape=jax.ShapeDtypeStruct((L, D, n), f32),
        compiler_params=_cparams(("parallel", "parallel")),
    )(c_act_t, dmod)


def _prenorm_fwd(cfg, x, mod, gain):
    S, D, TR = cfg.S, cfg.D, cfg.TR

    def body(x_ref, mod_ref, g_ref, h_ref):
        x = x_ref[...]
        r = lax.rsqrt(jnp.mean(x * x, axis=-1, keepdims=True) + NORM_EPS)
        shift, scale = mod_ref[:, 0:D], mod_ref[:, D:2 * D]
        h_ref[...] = ((x * r) * g_ref[...] * (1.0 + scale) + shift).astype(bf16)

    return pl.pallas_call(
        body, name="prenorm_fwd", grid=(S // TR,),
        in_specs=[_slab(TR, D, 0), _row(3 * D), _row(D)],
        out_specs=_slab(TR, D, 0), out_shape=jax.ShapeDtypeStruct((S, D), bf16),
        compiler_params=_cparams(("parallel",)),
    )(x, mod, gain)


def _prenorm_bwd(cfg, x, dh, dres, mod, gain):
    S, D, TR = cfg.S, cfg.D, cfg.TR

    def body(x_ref, dh_ref, dres_ref, mod_ref, g_ref, dx_ref, sum_ref):
        i = pl.program_id(0)
        x, dh, g = x_ref[...], dh_ref[...], g_ref[...]
        scale = mod_ref[:, D:2 * D]
        r = lax.rsqrt(jnp.mean(x * x, axis=-1, keepdims=True) + NORM_EPS)
        xn = x * r
        t = dh * xn
        dxn = dh * (g * (1.0 + scale))
        dx_ref[...] = r * (dxn - xn * jnp.mean(dxn * xn, axis=-1, keepdims=True)) + dres_ref[...]
        part = jnp.concatenate([jnp.sum(dh, axis=0, keepdims=True), jnp.sum(t * g, axis=0, keepdims=True),
                                jnp.sum(t * (1.0 + scale), axis=0, keepdims=True), jnp.zeros((SUBLANES - 3, D), f32)], axis=0)

        @pl.when(i == 0)
        def _():
            sum_ref[...] = part

        @pl.when(i > 0)
        def _():
            sum_ref[...] += part

    return pl.pallas_call(
        body, name="prenorm_bwd", grid=(S // TR,),
        in_specs=[_slab(TR, D, 0), _slab(TR, D, 0), _slab(TR, D, 0), _row(3 * D), _row(D)],
        out_specs=(_slab(TR, D, 0), pl.BlockSpec((SUBLANES, D), lambda i: (0, 0))),
        out_shape=(jax.ShapeDtypeStruct((S, D), f32), jax.ShapeDtypeStruct((SUBLANES, D), f32)),
        compiler_params=_cparams(("arbitrary",)),
    )(x, dh, dres, mod, gain)


def _postnorm_fwd(cfg, x, y, mod, gain):
    S, D, TR = cfg.S, cfg.D, cfg.TR

    def body(x_ref, y_ref, mod_ref, g_ref, o_ref):
        y = y_ref[...]
        r = lax.rsqrt(jnp.mean(y * y, axis=-1, keepdims=True) + NORM_EPS)
        rg = mod_ref[:, 2 * D:3 * D]
        o_ref[...] = x_ref[...] + (1.0 + rg) * ((y * r) * g_ref[...])

    return pl.pallas_call(
        body, name="postnorm_fwd", grid=(S // TR,),
        in_specs=[_slab(TR, D, 0), _slab(TR, D, 0), _row(3 * D), _row(D)],
        out_specs=_slab(TR, D, 0), out_shape=jax.ShapeDtypeStruct((S, D), f32),
        compiler_params=_cparams(("parallel",)),
    )(x, y, mod, gain)


def _postnorm_bwd(cfg, dout, y, mod, gain):
    S, D, TR = cfg.S, cfg.D, cfg.TR

    def body(do_ref, y_ref, mod_ref, g_ref, dy_ref, sum_ref):
        i = pl.program_id(0)
        do, y, g = do_ref[...], y_ref[...], g_ref[...]
        rg = mod_ref[:, 2 * D:3 * D]
        r = lax.rsqrt(jnp.mean(y * y, axis=-1, keepdims=True) + NORM_EPS)
        yn = y * r
        t = do * yn
        dyn = do * ((1.0 + rg) * g)
        dy_ref[...] = (r * (dyn - yn * jnp.mean(dyn * yn, axis=-1, keepdims=True))).astype(bf16)
        part = jnp.concatenate([jnp.sum(t * g, axis=0, keepdims=True), jnp.sum(t * (1.0 + rg), axis=0, keepdims=True),
                                jnp.zeros((SUBLANES - 2, D), f32)], axis=0)

        @pl.when(i == 0)
        def _():
            sum_ref[...] = part

        @pl.when(i > 0)
        def _():
            sum_ref[...] += part

    return pl.pallas_call(
        body, name="postnorm_bwd", grid=(S // TR,),
        in_specs=[_slab(TR, D, 0), _slab(TR, D, 0), _row(3 * D), _row(D)],
        out_specs=(_slab(TR, D, 0), pl.BlockSpec((SUBLANES, D), lambda i: (0, 0))),
        out_shape=(jax.ShapeDtypeStruct((S, D), bf16), jax.ShapeDtypeStruct((SUBLANES, D), f32)),
        compiler_params=_cparams(("arbitrary",)),
    )(dout, y, mod, gain)


def _loss_head(cfg, y, target):
    S, D, TR = cfg.S, cfg.D, cfg.TR

    def body(y_ref, t_ref, d_ref, l_ref):
        i = pl.program_id(0)
        err = y_ref[...] - t_ref[...]
        d_ref[...] = err / D
        part = jnp.zeros((SUBLANES, LANES), f32) + 0.5 * jnp.sum(jnp.mean(err * err, axis=-1, keepdims=True))

        @pl.when(i == 0)
        def _():
            l_ref[...] = part

        @pl.when(i > 0)
        def _():
            l_ref[...] += part

    return pl.pallas_call(
        body, name="loss_head", grid=(S // TR,),
        in_specs=[_slab(TR, D, 0), _slab(TR, D, 0)],
        out_specs=(_slab(TR, D, 0), pl.BlockSpec((SUBLANES, LANES), lambda i: (0, 0))),
        out_shape=(jax.ShapeDtypeStruct((S, D), f32), jax.ShapeDtypeStruct((SUBLANES, LANES), f32)),
        compiler_params=_cparams(("arbitrary",)),
    )(y, target)


def _merge_fwd(cfg, proj, u0, u1, u2):
    S, D, TR = cfg.S, cfg.D, cfg.TR

    def body(l0, l1, l2, u0_ref, u1_ref, u2_ref, o_ref):
        o_ref[...] = (_sigmoid(l0[...]) * u0_ref[...] + _sigmoid(l1[...]) * u1_ref[...]
                      + _sigmoid(l2[...]) * u2_ref[...]).astype(bf16)

    return pl.pallas_call(
        body, name="merge_fwd", grid=(S // TR,),
        in_specs=[_slab(TR, D, cfg.o_merge + b * D) for b in range(3)] + [_slab(TR, D, 0)] * 3,
        out_specs=_slab(TR, D, 0), out_shape=jax.ShapeDtypeStruct((S, D), bf16),
        compiler_params=_cparams(("parallel",)),
    )(proj, proj, proj, u0, u1, u2)


def _merge_bwd(cfg, proj, dmerged, u0, u1, u2):
    S, D, TR = cfg.S, cfg.D, cfg.TR

    def body(l0, l1, l2, dm_ref, u0_ref, u1_ref, u2_ref, du0, du1, du2, dl_ref):
        dm = dm_ref[...]
        for b, (l, u, du) in enumerate(((l0, u0_ref, du0), (l1, u1_ref, du1), (l2, u2_ref, du2))):
            g = _sigmoid(l[...])
            du[...] = (dm * g).astype(bf16)
            dl_ref[:, b * D:(b + 1) * D] = (dm * u[...] * (g * (1.0 - g))).astype(bf16)

    return pl.pallas_call(
        body, name="merge_bwd", grid=(S // TR,),
        in_specs=[_slab(TR, D, cfg.o_merge + b * D) for b in range(3)] + [_slab(TR, D, 0)] * 4,
        out_specs=(_slab(TR, D, 0),) * 3 + (_slab(TR, 3 * D, 0),),
        out_shape=(jax.ShapeDtypeStruct((S, D), bf16),) * 3 + (jax.ShapeDtypeStruct((S, 3 * D), bf16),),
        compiler_params=_cparams(("parallel",)),
    )(proj, proj, proj, dmerged, u0, u1, u2)


def _rope128(x, c, s):
    return x * c + pltpu.roll(x, 64, axis=1) * s


def _rope128_t(dy, c, s):
    return dy * c + pltpu.roll(dy * s, 64, axis=1)


def _swap32(x):
    w = x.shape[1]
    lane = lax.broadcasted_iota(jnp.int32, x.shape, 1)
    return jnp.where((lane % 64) < 32, pltpu.roll(x, w - 32, axis=1), pltpu.roll(x, 32, axis=1))


def _rope64(x, c, s):
    return x * c + _swap32(x) * s


def _rope64_t(dy, c, s):
    return dy * c + _swap32(dy * s)


def _rope_tables(cfg, positions):
    pos = positions.astype(f32)[0][:, None]

    def tab(dim):
        inv_freq = ROPE_BASE ** (-jnp.arange(0, dim, 2, dtype=f32) / dim)
        ang = pos * inv_freq
        cos, sin = jnp.cos(ang), jnp.sin(ang)
        return jnp.concatenate([cos, cos], axis=1), jnp.concatenate([-sin, sin], axis=1)

    return tab(HEAD), tab(ROPE)


def _ret_consts(cfg):
    h = np.arange(cfg.H, dtype=np.float64)
    log_gamma = np.log1p(-np.exp2(-5.0 - h)).astype(np.float32)
    idx = np.arange(CHUNK, dtype=np.float32)
    intra = np.exp(log_gamma[:, None, None] * np.abs(idx[:, None] - idx[None, :]))
    kdec = np.exp(log_gamma[:, None] * (CHUNK - 1 - idx)[None, :])
    qdec = np.exp(log_gamma[:, None] * (idx + 1.0)[None, :])
    cdec = np.exp(log_gamma * CHUNK)
    bc = lambda a: jnp.asarray(np.broadcast_to(a[..., None], a.shape + (HEAD,)).astype(np.float32))
    return jnp.asarray(intra.astype(np.float32)), bc(kdec), bc(qdec), bc(cdec[:, None])


def _ret_core(cfg, q_raw, k_raw, v_raw, cos, sin, intra, kdec, qdec, cdec, p_ref):
    S = cfg.S
    NC = S // CHUNK
    q = _rope128(q_raw, cos, sin) * (HEAD ** -0.5)
    k = _rope128(k_raw, cos, sin)
    q3 = q.reshape(NC, CHUNK, HEAD)
    k3 = k.reshape(NC, CHUNK, HEAD)
    qb, kb = q3.astype(bf16), k3.astype(bf16)
    vb = v_raw.reshape(NC, CHUNK, HEAD).astype(bf16)
    sdb = (jnp.einsum('nid,njd->nij', qb, kb, preferred_element_type=f32) * intra[None]).astype(bf16)
    o_intra = jnp.einsum('nij,nje->nie', sdb, vb, preferred_element_type=f32)
    kdb = (k3 * kdec[None]).astype(bf16)
    kv = jnp.einsum('njd,nje->nde', kdb, vb, preferred_element_type=f32)
    p_ref[0] = jnp.zeros((HEAD, HEAD), f32)
    for n in range(1, NC):
        p_ref[n] = p_ref[n - 1] * cdec + kv[n - 1]
    pb = p_ref[...].astype(bf16)
    qdb = (q3 * qdec[None]).astype(bf16)
    o_inter = jnp.einsum('nid,nde->nie', qdb, pb, preferred_element_type=f32)
    o = (o_intra + o_inter).reshape(S, HEAD)
    return o, (qb, kb, vb, sdb, kdb, qdb, pb)


def _ret_specs(cfg):
    S = cfg.S
    hs = lambda off: pl.BlockSpec((S, HEAD), lambda h, _c=off // HEAD: (0, _c + h))
    full = pl.BlockSpec((S, HEAD), lambda h: (0, 0))
    consts = [pl.BlockSpec((None, CHUNK, CHUNK), lambda h: (h, 0, 0)), pl.BlockSpec((None, CHUNK, HEAD), lambda h: (h, 0, 0)),
              pl.BlockSpec((None, CHUNK, HEAD), lambda h: (h, 0, 0)), pl.BlockSpec((None, 1, HEAD), lambda h: (h, 0, 0))]
    gn = pl.BlockSpec((1, HEAD), lambda h: (0, h))
    return hs, full, consts, gn


def _ret_fwd(cfg, proj, gn, cos, sin, consts):
    S, NC = cfg.S, cfg.S // CHUNK
    hs, full, cspecs, gspec = _ret_specs(cfg)

    def body(q_ref, k_ref, v_ref, g_ref, gn_ref, cos_ref, sin_ref, intra, kdec, qdec, cdec, y_ref, p_ref):
        o, _ = _ret_core(cfg, q_ref[...], k_ref[...], v_ref[...], cos_ref[...], sin_ref[...],
                         intra[...], kdec[...], qdec[...], cdec[...], p_ref)
        mean = jnp.mean(o, axis=-1, keepdims=True)
        var = jnp.mean(jnp.square(o - mean), axis=-1, keepdims=True)
        z = ((o - mean) * lax.rsqrt(var + NORM_EPS)) * gn_ref[...]
        y_ref[...] = (z * _silu(g_ref[...])).astype(bf16)

    return pl.pallas_call(
        body, name="ret_fwd", grid=(cfg.H,),
        in_specs=[hs(0), hs(cfg.o_rk), hs(cfg.o_rv), hs(cfg.o_rg), gspec, full, full] + cspecs,
        out_specs=hs(0), out_shape=jax.ShapeDtypeStruct((S, cfg.RW), bf16),
        scratch_shapes=[pltpu.VMEM((NC, HEAD, HEAD), f32)],
        compiler_params=_cparams(("arbitrary",)),
    )(proj, proj, proj, proj, gn, cos, sin, *consts)


def _ret_bwd(cfg, proj, dy, gn, cos, sin, consts):
    S, NC = cfg.S, cfg.S // CHUNK
    hs, full, cspecs, gspec = _ret_specs(cfg)

    def body(q_ref, k_ref, v_ref, g_ref, dy_ref, gn_ref, cos_ref, sin_ref, intra_ref, kdec_ref, qdec_ref, cdec_ref,
             dq_ref, dk_ref, dv_ref, dg_ref, dgn_ref, p_ref, g_scr):
        cos, sin = cos_ref[...], sin_ref[...]
        intra, kdec, qdec, cdec = intra_ref[...], kdec_ref[...], qdec_ref[...], cdec_ref[...]
        o, (qb, kb, vb, sdb, kdb, qdb, pb) = _ret_core(cfg, q_ref[...], k_ref[...], v_ref[...], cos, sin,
                                                     intra, kdec, qdec, cdec, p_ref)
        gate, dy, gnv = g_ref[...], dy_ref[...], gn_ref[...]
        mean = jnp.mean(o, axis=-1, keepdims=True)
        rstd = lax.rsqrt(jnp.mean(jnp.square(o - mean), axis=-1, keepdims=True) + NORM_EPS)
        on = (o - mean) * rstd
        dz = dy * _silu(gate)
        dg_ref[...] = (dy * (on * gnv) * _dsilu(gate)).astype(bf16)
        dgn_ref[...] = jnp.sum(dz * on, axis=0, keepdims=True)
        don = dz * gnv
        do = rstd * (don - jnp.mean(don, axis=-1, keepdims=True) - on * jnp.mean(don * on, axis=-1, keepdims=True))
        dob = do.reshape(NC, CHUNK, HEAD).astype(bf16)
        dsb = (jnp.einsum('nie,nje->nij', dob, vb, preferred_element_type=f32) * intra[None]).astype(bf16)
        dv = jnp.einsum('nij,nie->nje', sdb, dob, preferred_element_type=f32)
        dq = jnp.einsum('nij,njd->nid', dsb, kb, preferred_element_type=f32)
        dk = jnp.einsum('nij,nid->njd', dsb, qb, preferred_element_type=f32)
        dq = dq + jnp.einsum('nie,nde->nid', dob, pb, preferred_element_type=f32) * qdec[None]
        dp = jnp.einsum('nid,nie->nde', qdb, dob, preferred_element_type=f32)
        g_scr[NC - 1] = jnp.zeros((HEAD, HEAD), f32)
        for n in range(NC - 2, -1, -1):
            g_scr[n] = dp[n + 1] + g_scr[n + 1] * cdec
        gb = g_scr[...].astype(bf16)
        dk = dk + jnp.einsum('nje,nde->njd', vb, gb, preferred_element_type=f32) * kdec[None]
        dv = dv + jnp.einsum('njd,nde->nje', kdb, gb, preferred_element_type=f32)
        dq_ref[...] = _rope128_t(dq.reshape(S, HEAD) * (HEAD ** -0.5), cos, sin).astype(bf16)
        dk_ref[...] = _rope128_t(dk.reshape(S, HEAD), cos, sin).astype(bf16)
        dv_ref[...] = dv.reshape(S, HEAD).astype(bf16)

    return pl.pallas_call(
        body, name="ret_bwd", grid=(cfg.H,),
        in_specs=[hs(0), hs(cfg.o_rk), hs(cfg.o_rv), hs(cfg.o_rg), hs(0), gspec, full, full] + cspecs,
        out_specs=(hs(0),) * 4 + (gspec,),
        out_shape=(jax.ShapeDtypeStruct((S, cfg.RW), bf16),) * 4 + (jax.ShapeDtypeStruct((1, cfg.RW), f32),),
        scratch_shapes=[pltpu.VMEM((NC, HEAD, HEAD), f32), pltpu.VMEM((NC, HEAD, HEAD), f32)],
        compiler_params=_cparams(("arbitrary",)),
    )(proj, proj, proj, proj, dy, gn, cos, sin, *consts)


def _expm1(x):
    small = x * (1.0 + x * (0.5 + x * (1.0 / 6.0 + x * (1.0 / 24.0 + x * (1.0 / 120.0)))))
    return jnp.where(jnp.abs(x) < 0.1, small, jnp.exp(x) - 1.0)


def _softplus(z):
    return jnp.maximum(z, 0.0) + jnp.log1p(jnp.exp(-jnp.abs(z)))


def _lru_conv(cfg, x_ref, halo_ref, cw, scr, first):
    TR = cfg.TR
    scr[0:SUBLANES, :] = jnp.where(first, 0.0, halo_ref[...])
    scr[SUBLANES:SUBLANES + TR, :] = x_ref[...]
    xc = scr[pl.ds(SUBLANES - (CONV - 1), TR), :] * cw[0:1, :]
    for j in range(1, CONV):
        xc = xc + scr[pl.ds(SUBLANES - (CONV - 1) + j, TR), :] * cw[j:j + 1, :]
    return xc


def _lru_pre(cfg, xc, wa_ref, wx_ref, ba, bx):
    xb = xc.astype(bf16)
    pa = jnp.concatenate([jnp.dot(xb[:, n * HEAD:(n + 1) * HEAD], wa_ref[n].astype(bf16), preferred_element_type=f32)
                          for n in range(cfg.NB)], axis=1) + ba
    px = jnp.concatenate([jnp.dot(xb[:, n * HEAD:(n + 1) * HEAD], wx_ref[n].astype(bf16), preferred_element_type=f32)
                          for n in range(cfg.NB)], axis=1) + bx
    return pa, px


def _lru_ab(pa, px, xc, lam):
    r, i = _sigmoid(pa), _sigmoid(px)
    log_a = (-LRU_C * r) * _softplus(-lam)
    a = jnp.exp(log_a)
    b = jnp.sqrt(-_expm1(2.0 * log_a)) * (i * xc)
    return a, b


def _lru_halo_specs(cfg, off, W):
    TR, S = cfg.TR, cfg.S
    nb = TR // SUBLANES
    cb = off // W
    main = pl.BlockSpec((TR, W), lambda i: (i, cb))
    prev = pl.BlockSpec((SUBLANES, W), lambda i: (jnp.maximum(i * nb - 1, 0), cb))
    nxt = pl.BlockSpec((SUBLANES, W), lambda i: (jnp.minimum((i + 1) * nb, S // SUBLANES - 1), cb))
    return main, prev, nxt


def _lru_gates(cfg, proj, cw, cb, wa, ba, wx, bx, lam):
    S, W, TR, NB = cfg.S, cfg.LW, cfg.TR, cfg.NB
    assert cfg.o_lx % W == 0
    main, prev, _ = _lru_halo_specs(cfg, cfg.o_lx, W)
    wspec = pl.BlockSpec((NB, HEAD, HEAD), lambda i: (0, 0, 0))

    def body(x_ref, halo_ref, cw_ref, cb_ref, wa_ref, ba_ref, wx_ref, bx_ref, lam_ref, a_ref, b_ref, scr):
        xc = _lru_conv(cfg, x_ref, halo_ref, cw_ref[...], scr, pl.program_id(0) == 0) + cb_ref[...]
        pa, px = _lru_pre(cfg, xc, wa_ref, wx_ref, ba_ref[...], bx_ref[...])
        a, b = _lru_ab(pa, px, xc, lam_ref[...])
        a_ref[...] = a
        b_ref[...] = b

    return pl.pallas_call(
        body, name="lru_gates", grid=(S // TR,),
        in_specs=[main, prev, pl.BlockSpec((CONV, W), lambda i: (0, 0)), _row(W), wspec, _row(W), wspec, _row(W), _row(W)],
        out_specs=(_slab(TR, W, 0),) * 2, out_shape=(jax.ShapeDtypeStruct((S, W), f32),) * 2,
        scratch_shapes=[pltpu.VMEM((TR + SUBLANES, W), f32)],
        compiler_params=_cparams(("parallel",)),
    )(proj, proj, cw, cb, wa, ba, wx, bx, lam)


def _lru_lane_block(cfg):
    return 256 if cfg.LW % 256 == 0 else LANES


def _lru_scan_fwd(cfg, proj, a, b):
    S, W = cfg.S, cfg.LW
    LB = _lru_lane_block(cfg)
    assert cfg.o_lg % LB == 0
    col = lambda off: pl.BlockSpec((S, LB), lambda j, _c=off // LB: (0, _c + j))

    def body(a_ref, b_ref, g_ref, h_ref, y_ref):
        def blk(t, h):
            r0 = pl.multiple_of(t * SUBLANES, SUBLANES)
            at, bt = a_ref[pl.ds(r0, SUBLANES), :], b_ref[pl.ds(r0, SUBLANES), :]
            rows = []
            for j in range(SUBLANES):
                h = at[j:j + 1, :] * h + bt[j:j + 1, :]
                rows.append(h)
            h_ref[pl.ds(r0, SUBLANES), :] = jnp.concatenate(rows, axis=0)
            return h

        lax.fori_loop(0, S // SUBLANES, blk, jnp.zeros((1, LB), f32))
        y_ref[...] = (h_ref[...] * _silu(g_ref[...])).astype(bf16)

    return pl.pallas_call(
        body, name="lru_scan_fwd", grid=(W // LB,),
        in_specs=[col(0), col(0), col(cfg.o_lg)],
        out_specs=(col(0), col(0)),
        out_shape=(jax.ShapeDtypeStruct((S, W), f32), jax.ShapeDtypeStruct((S, W), bf16)),
        compiler_params=_cparams(("parallel",)),
    )(a, b, proj)


def _lru_scan_bwd(cfg, proj, a, h, dy):
    S, W = cfg.S, cfg.LW
    LB = _lru_lane_block(cfg)
    col = lambda off: pl.BlockSpec((S, LB), lambda j, _c=off // LB: (0, _c + j))

    def body(a_ref, h_ref, dy_ref, g_ref, da_ref, db_ref, dg_ref):
        gate, dy = g_ref[...], dy_ref[...]
        dg_ref[...] = (dy * h_ref[...] * _dsilu(gate)).astype(bf16)
        da_ref[...] = dy * _silu(gate)

        def blk(t, carry):
            dh_next, a_next = carry
            r0 = pl.multiple_of((S // SUBLANES - 1 - t) * SUBLANES, SUBLANES)
            at, ct = a_ref[pl.ds(r0, SUBLANES), :], da_ref[pl.ds(r0, SUBLANES), :]
            rows = [None] * SUBLANES
            for j in range(SUBLANES - 1, -1, -1):
                dh_next = ct[j:j + 1, :] + a_next * dh_next
                a_next = at[j:j + 1, :]
                rows[j] = dh_next
            db_ref[pl.ds(r0, SUBLANES), :] = jnp.concatenate(rows, axis=0)
            return dh_next, a_next

        z = jnp.zeros((1, LB), f32)
        lax.fori_loop(0, S // SUBLANES, blk, (z, z))
        row = lax.broadcasted_iota(jnp.int32, (S, LB), 0)
        hprev = jnp.where(row == 0, 0.0, pltpu.roll(h_ref[...], 1, axis=0))
        da_ref[...] = db_ref[...] * hprev

    return pl.pallas_call(
        body, name="lru_scan_bwd", grid=(W // LB,),
        in_specs=[col(0), col(0), col(0), col(cfg.o_lg)],
        out_specs=(col(0),) * 3,
        out_shape=(jax.ShapeDtypeStruct((S, W), f32),) * 2 + (jax.ShapeDtypeStruct((S, W), bf16),),
        compiler_params=_cparams(("parallel",)),
    )(a, h, dy, proj)


def _lru_gates_bwd(cfg, proj, da, db, cw, cb, wa, ba, wx, bx, lam):
    S, W, TR, NB = cfg.S, cfg.LW, cfg.TR, cfg.NB
    main, prev, _ = _lru_halo_specs(cfg, cfg.o_lx, W)
    wspec = pl.BlockSpec((NB, HEAD, HEAD), lambda i: (0, 0, 0))

    def body(x_ref, halo_ref, da_ref, db_ref, cw_ref, cb_ref, wa_ref, ba_ref, wx_ref, bx_ref, lam_ref,
             dxc_ref, dwa_ref, dwx_ref, sum_ref, scr):
        i = pl.program_id(0)
        lam = lam_ref[...]
        xc = _lru_conv(cfg, x_ref, halo_ref, cw_ref[...], scr, i == 0) + cb_ref[...]
        pa, px = _lru_pre(cfg, xc, wa_ref, wx_ref, ba_ref[...], bx_ref[...])
        _, vjp = jax.vjp(_lru_ab, pa, px, xc, lam)
        dpa, dpx, dxc, dlam = vjp((da_ref[...], db_ref[...]))
        xb, dpab, dpxb = xc.astype(bf16), dpa.astype(bf16), dpx.astype(bf16)
        nt = (((1,), (1,)), ((), ()))
        tn = (((0,), (0,)), ((), ()))
        back = []
        dwa, dwx = [], []
        for n in range(NB):
            sl = slice(n * HEAD, (n + 1) * HEAD)
            back.append(lax.dot_general(dpab[:, sl], wa_ref[n].astype(bf16), nt, preferred_element_type=f32)
                        + lax.dot_general(dpxb[:, sl], wx_ref[n].astype(bf16), nt, preferred_element_type=f32))
            dwa.append(lax.dot_general(xb[:, sl], dpab[:, sl], tn, preferred_element_type=f32))
            dwx.append(lax.dot_general(xb[:, sl], dpxb[:, sl], tn, preferred_element_type=f32))
        dxc_ref[...] = dxc + jnp.concatenate(back, axis=1)
        part = jnp.concatenate([jnp.sum(dpa, axis=0, keepdims=True), jnp.sum(dpx, axis=0, keepdims=True), dlam,
                                jnp.zeros((SUBLANES - 3, W), f32)], axis=0)

        @pl.when(i == 0)
        def _():
            sum_ref[...] = part
            for n in range(NB):
                dwa_ref[n] = dwa[n]
                dwx_ref[n] = dwx[n]

        @pl.when(i > 0)
        def _():
            sum_ref[...] += part
            for n in range(NB):
                dwa_ref[n] += dwa[n]
                dwx_ref[n] += dwx[n]

    return pl.pallas_call(
        body, name="lru_gates_bwd", grid=(S // TR,),
        in_specs=[main, prev, _slab(TR, W, 0), _slab(TR, W, 0), pl.BlockSpec((CONV, W), lambda i: (0, 0)), _row(W),
                  wspec, _row(W), wspec, _row(W), _row(W)],
        out_specs=(_slab(TR, W, 0), wspec, wspec, pl.BlockSpec((SUBLANES, W), lambda i: (0, 0))),
        out_shape=(jax.ShapeDtypeStruct((S, W), f32), jax.ShapeDtypeStruct((NB, HEAD, HEAD), f32),
                   jax.ShapeDtypeStruct((NB, HEAD, HEAD), f32), jax.ShapeDtypeStruct((SUBLANES, W), f32)),
        scratch_shapes=[pltpu.VMEM((TR + SUBLANES, W), f32)],
        compiler_params=_cparams(("arbitrary",)),
    )(proj, proj, da, db, cw, cb, wa, ba, wx, bx, lam)


def _lru_conv_bwd(cfg, proj, dxc, cw):
    S, W, TR = cfg.S, cfg.LW, cfg.TR
    main, prev, _ = _lru_halo_specs(cfg, cfg.o_lx, W)
    dmain, _, dnext = _lru_halo_specs(cfg, 0, W)

    def body(x_ref, xhalo_ref, d_ref, dhalo_ref, cw_ref, dx_ref, sum_ref, xs, ds):
        i = pl.program_id(0)
        cw = cw_ref[...]
        d = d_ref[...]
        xs[0:SUBLANES, :] = jnp.where(i == 0, 0.0, xhalo_ref[...])
        xs[SUBLANES:SUBLANES + TR, :] = x_ref[...]
        ds[0:TR, :] = d
        ds[TR:TR + SUBLANES, :] = jnp.where(i == pl.num_programs(0) - 1, 0.0, dhalo_ref[...])
        dx = ds[pl.ds(CONV - 1, TR), :] * cw[0:1, :]
        parts = [jnp.sum(d * xs[pl.ds(SUBLANES - (CONV - 1), TR), :], axis=0, keepdims=True)]
        for j in range(1, CONV):
            dx = dx + ds[pl.ds(CONV - 1 - j, TR), :] * cw[j:j + 1, :]
            parts.append(jnp.sum(d * xs[pl.ds(SUBLANES - (CONV - 1) + j, TR), :], axis=0, keepdims=True))
        dx_ref[...] = dx.astype(bf16)
        part = jnp.concatenate(parts + [jnp.sum(d, axis=0, keepdims=True), jnp.zeros((SUBLANES - CONV - 1, W), f32)], axis=0)

        @pl.when(i == 0)
        def _():
            sum_ref[...] = part

        @pl.when(i > 0)
        def _():
            sum_ref[...] += part

    return pl.pallas_call(
        body, name="lru_conv_bwd", grid=(S // TR,),
        in_specs=[main, prev, dmain, dnext, pl.BlockSpec((CONV, W), lambda i: (0, 0))],
        out_specs=(_slab(TR, W, 0), pl.BlockSpec((SUBLANES, W), lambda i: (0, 0))),
        out_shape=(jax.ShapeDtypeStruct((S, W), bf16), jax.ShapeDtypeStruct((SUBLANES, W), f32)),
        scratch_shapes=[pltpu.VMEM((TR + SUBLANES, W), f32), pltpu.VMEM((TR + SUBLANES, W), f32)],
        compiler_params=_cparams(("arbitrary",)),
    )(proj, proj, dxc, dxc, cw)


def _rms(x, g):
    r = lax.rsqrt(jnp.mean(x * x, axis=-1, keepdims=True) + NORM_EPS)
    return (x * r) * g, r


def _mla_norm(cfg, proj, qg, kg):
    S, TR = cfg.S, cfg.TR

    def body(q_ref, k_ref, qg_ref, kg_ref, qn_ref, kn_ref):
        qn_ref[...] = _rms(q_ref[...], qg_ref[...])[0].astype(bf16)
        kn_ref[...] = _rms(k_ref[...], kg_ref[...])[0].astype(bf16)

    return pl.pallas_call(
        body, name="mla_norm", grid=(S // TR,),
        in_specs=[_slab(TR, cfg.QL, cfg.o_mq), _slab(TR, cfg.KL, cfg.o_mkv), _row(cfg.QL), _row(cfg.KL)],
        out_specs=(_slab(TR, cfg.QL, 0), _slab(TR, cfg.KL, 0)),
        out_shape=(jax.ShapeDtypeStruct((S, cfg.QL), bf16), jax.ShapeDtypeStruct((S, cfg.KL), bf16)),
        compiler_params=_cparams(("parallel",)),
    )(proj, proj, qg, kg)


def _mla_norm_bwd(cfg, proj, dqn, dkn, qg, kg):
    S, TR = cfg.S, cfg.TR

    def one(x, g, dn):
        r = lax.rsqrt(jnp.mean(x * x, axis=-1, keepdims=True) + NORM_EPS)
        xn = x * r
        dxn = dn * g
        dx = r * (dxn - xn * jnp.mean(dxn * xn, axis=-1, keepdims=True))
        return dx, jnp.sum(dn * xn, axis=0, keepdims=True)

    def body(q_ref, k_ref, dq_ref, dk_ref, qg_ref, kg_ref, dmq_ref, dmk_ref, sq_ref, sk_ref):
        i = pl.program_id(0)
        dq, gq = one(q_ref[...], qg_ref[...], dq_ref[...])
        dk, gk = one(k_ref[...], kg_ref[...], dk_ref[...])
        dmq_ref[...] = dq.astype(bf16)
        dmk_ref[...] = dk.astype(bf16)
        pq = jnp.concatenate([gq, jnp.zeros((SUBLANES - 1, cfg.QL), f32)], axis=0)
        pk = jnp.concatenate([gk, jnp.zeros((SUBLANES - 1, cfg.KL), f32)], axis=0)

        @pl.when(i == 0)
        def _():
            sq_ref[...] = pq
            sk_ref[...] = pk

        @pl.when(i > 0)
        def _():
            sq_ref[...] += pq
            sk_ref[...] += pk

    return pl.pallas_call(
        body, name="mla_norm_bwd", grid=(S // TR,),
        in_specs=[_slab(TR, cfg.QL, cfg.o_mq), _slab(TR, cfg.KL, cfg.o_mkv), _slab(TR, cfg.QL, 0), _slab(TR, cfg.KL, 0),
                  _row(cfg.QL), _row(cfg.KL)],
        out_specs=(_slab(TR, cfg.QL, 0), _slab(TR, cfg.KL, 0), pl.BlockSpec((SUBLANES, cfg.QL), lambda i: (0, 0)),
                   pl.BlockSpec((SUBLANES, cfg.KL), lambda i: (0, 0))),
        out_shape=(jax.ShapeDtypeStruct((S, cfg.QL), bf16), jax.ShapeDtypeStruct((S, cfg.KL), bf16),
                   jax.ShapeDtypeStruct((SUBLANES, cfg.QL), f32), jax.ShapeDtypeStruct((SUBLANES, cfg.KL), f32)),
        compiler_params=_cparams(("arbitrary",)),
    )(proj, proj, dqn, dkn, qg, kg)


def _mla_pack(cfg, proj, q, kv, cq, sq, ck, sk):
    S, TR, MH = cfg.S, cfg.TR, cfg.MH
    NW, RWD = MH * HEAD, MH * ROPE

    def body(q_ref, kv_ref, kr_ref, cq_ref, sq_ref, ck_ref, sk_ref, qo_ref, ko_ref, vo_ref):
        q, kv = q_ref[...], kv_ref[...]
        qr = _rope64(q[:, NW:], cq_ref[...], sq_ref[...])
        kr = _rope64(kr_ref[...], ck_ref[...], sk_ref[...]).astype(bf16)
        lane = lax.broadcasted_iota(jnp.int32, (TR, HEAD), 1)
        for h in range(MH):
            grp = qr[:, (h // 2) * HEAD:(h // 2 + 1) * HEAD]
            if h % 2:
                grp = pltpu.roll(grp, 64, axis=1)
            qo_ref[h] = jnp.concatenate([q[:, h * HEAD:(h + 1) * HEAD], jnp.where(lane < ROPE, grp, 0.0)], axis=1).astype(bf16)
            ko_ref[h] = jnp.concatenate([kv[:, 2 * h * HEAD:(2 * h + 1) * HEAD].astype(bf16), kr], axis=1)
            vo_ref[h] = kv[:, (2 * h + 1) * HEAD:(2 * h + 2) * HEAD].astype(bf16)

    hspec = lambda w: pl.BlockSpec((MH, TR, w), lambda i: (0, i, 0))
    return pl.pallas_call(
        body, name="mla_pack", grid=(S // TR,),
        in_specs=[_slab(TR, cfg.QW, 0), _slab(TR, cfg.KVW, 0), _slab(TR, HEAD, cfg.o_mkr),
                  _slab(TR, RWD, 0), _slab(TR, RWD, 0), _slab(TR, HEAD, 0), _slab(TR, HEAD, 0)],
        out_specs=(hspec(2 * HEAD), hspec(2 * HEAD), hspec(HEAD)),
        out_shape=(jax.ShapeDtypeStruct((MH, S, 2 * HEAD), bf16), jax.ShapeDtypeStruct((MH, S, 2 * HEAD), bf16),
                   jax.ShapeDtypeStruct((MH, S, HEAD), bf16)),
        compiler_params=_cparams(("parallel",)),
    )(q, kv, proj, cq, sq, ck, sk)


def _mla_unpack_bwd(cfg, dq3, dk3, dv3, cq, sq, ck, sk):
    S, TR, MH = cfg.S, cfg.TR, cfg.MH
    RWD = MH * ROPE

    def body(dq_ref, dk_ref, dv_ref, cq_ref, sq_ref, ck_ref, sk_ref, q_ref, kv_ref, kr_ref):
        lane = lax.broadcasted_iota(jnp.int32, (TR, HEAD), 1)
        nope, ropes, kvs = [], [], []
        dkr = jnp.zeros((TR, HEAD), f32)
        for h in range(MH):
            dq = dq_ref[h]
            nope.append(dq[:, :HEAD])
            part = jnp.where(lane < ROPE, dq[:, HEAD:], 0.0)
            if h % 2:
                ropes[-1] = ropes[-1] + pltpu.roll(part, 64, axis=1)
            else:
                ropes.append(part)
            dk = dk_ref[h]
            kvs += [dk[:, :HEAD], dv_ref[h]]
            dkr = dkr + dk[:, HEAD:]
        dqr = _rope64_t(jnp.concatenate(ropes, axis=1), cq_ref[...], sq_ref[...])
        q_ref[...] = jnp.concatenate(nope + [dqr], axis=1).astype(bf16)
        kv_ref[...] = jnp.concatenate(kvs, axis=1).astype(bf16)
        dkr = jnp.where(lane < ROPE, dkr, 0.0)
        kr_ref[...] = _rope64_t(dkr, ck_ref[...], sk_ref[...]).astype(bf16)

    hspec = lambda w: pl.BlockSpec((MH, TR, w), lambda i: (0, i, 0))
    return pl.pallas_call(
        body, name="mla_unpack_bwd", grid=(S // TR,),
        in_specs=[hspec(2 * HEAD), hspec(2 * HEAD), hspec(HEAD), _slab(TR, RWD, 0), _slab(TR, RWD, 0),
                  _slab(TR, HEAD, 0), _slab(TR, HEAD, 0)],
        out_specs=(_slab(TR, cfg.QW, 0), _slab(TR, cfg.KVW, 0), _slab(TR, HEAD, 0)),
        out_shape=(jax.ShapeDtypeStruct((S, cfg.QW), bf16), jax.ShapeDtypeStruct((S, cfg.KVW), bf16),
                   jax.ShapeDtypeStruct((S, HEAD), bf16)),
        compiler_params=_cparams(("parallel",)),
    )(dq3, dk3, dv3, cq, sq, ck, sk)


def _mla_probs(cfg, q, k, i):
    TQ, n = cfg.TQ, k.shape[0]
    nt = (((1,), (1,)), ((), ()))
    s = lax.dot_general(q, k, nt, preferred_element_type=f32) * ((HEAD + ROPE) ** -0.5)
    qc = (i * TQ + lax.broadcasted_iota(jnp.int32, (TQ, n), 0)) // CHUNK
    kc = lax.broadcasted_iota(jnp.int32, (TQ, n), 1) // CHUNK
    s = jnp.where(kc <= qc, s, -1e30)
    m = jnp.max(s, axis=-1, keepdims=True)
    e = jnp.exp(s - m)
    return e / jnp.sum(e, axis=-1, keepdims=True)


def _mla_attn_specs(cfg):
    S, TQ = cfg.S, cfg.TQ
    qs = lambda w: pl.BlockSpec((None, TQ, w), lambda h, i: (h, i, 0))
    ks = lambda w: pl.BlockSpec((None, S, w), lambda h, i: (h, 0, 0))
    hs = lambda off: pl.BlockSpec((TQ, HEAD), lambda h, i, _c=off // HEAD: (i, _c + h))
    return qs, ks, hs


def _mla_attn_fwd(cfg, proj, q3, k3, v3):
    S, TQ, MH = cfg.S, cfg.TQ, cfg.MH
    qs, ks, hs = _mla_attn_specs(cfg)

    def body(q_ref, k_ref, v_ref, g_ref, o_ref, y_ref):
        for i in range(S // TQ):
            @pl.when(pl.program_id(1) == i)
            def _(i=i):
                n = (i + 1) * TQ
                p = _mla_probs(cfg, q_ref[...], k_ref[0:n, :], i)
                o = jnp.dot(p.astype(bf16), v_ref[0:n, :], preferred_element_type=f32)
                o_ref[...] = o
                y_ref[...] = (o * _silu(g_ref[...])).astype(bf16)

    return pl.pallas_call(
        body, name="mla_attn_fwd", grid=(MH, S // TQ),
        in_specs=[qs(2 * HEAD), ks(2 * HEAD), ks(HEAD), hs(cfg.o_mg)],
        out_specs=(hs(0), hs(0)),
        out_shape=(jax.ShapeDtypeStruct((S, cfg.MW), f32), jax.ShapeDtypeStruct((S, cfg.MW), bf16)),
        compiler_params=_cparams(("parallel", "parallel")),
    )(q3, k3, v3, proj)


def _mla_attn_bwd(cfg, proj, q3, k3, v3, o, dy):
    S, TQ, MH = cfg.S, cfg.TQ, cfg.MH
    qs, ks, hs = _mla_attn_specs(cfg)

    def body(q_ref, k_ref, v_ref, g_ref, o_ref, dy_ref, dq_ref, dk_ref, dv_ref, dg_ref):
        q = q_ref[...]
        gate, dy, o = g_ref[...], dy_ref[...], o_ref[...]
        dg_ref[...] = (dy * o * _dsilu(gate)).astype(bf16)
        dob = (dy * _silu(gate)).astype(bf16)
        nt = (((1,), (1,)), ((), ()))
        tn = (((0,), (0,)), ((), ()))

        @pl.when(pl.program_id(1) == 0)
        def _():
            dk_ref[...] = jnp.zeros_like(dk_ref)
            dv_ref[...] = jnp.zeros_like(dv_ref)

        for i in range(S // TQ):
            @pl.when(pl.program_id(1) == i)
            def _(i=i):
                n = (i + 1) * TQ
                k, v = k_ref[0:n, :], v_ref[0:n, :]
                p = _mla_probs(cfg, q, k, i)
                dv_ref[0:n, :] += lax.dot_general(p.astype(bf16), dob, tn, preferred_element_type=f32)
                dp = lax.dot_general(dob, v, nt, preferred_element_type=f32)
                ds = (p * (dp - jnp.sum(dp * p, axis=-1, keepdims=True)) * ((HEAD + ROPE) ** -0.5)).astype(bf16)
                dq_ref[...] = jnp.dot(ds, k, preferred_element_type=f32)
                dk_ref[0:n, :] += lax.dot_general(ds, q, tn, preferred_element_type=f32)

    return pl.pallas_call(
        body, name="mla_attn_bwd", grid=(MH, S // TQ),
        in_specs=[qs(2 * HEAD), ks(2 * HEAD), ks(HEAD), hs(cfg.o_mg), hs(0), hs(0)],
        out_specs=(qs(2 * HEAD), ks(2 * HEAD), ks(HEAD), hs(0)),
        out_shape=(jax.ShapeDtypeStruct((MH, S, 2 * HEAD), f32), jax.ShapeDtypeStruct((MH, S, 2 * HEAD), f32),
                   jax.ShapeDtypeStruct((MH, S, HEAD), f32), jax.ShapeDtypeStruct((S, cfg.MW), bf16)),
        compiler_params=_cparams(("parallel", "arbitrary")),
    )(q3, k3, v3, proj, o, dy)


def _pick_rows(R, bytes_per_row):
    if R * bytes_per_row <= MM_BUDGET:
        return R
    best = None
    for t in range(16, R, 16):
        if R % t == 0 and t * bytes_per_row <= MM_BUDGET:
            best = t
    assert best is not None, (R, bytes_per_row)
    return best


def _adamw(w, g, m, v, name="adamw"):
    R, C = w.shape
    tr = _pick_rows(R, C * 4 * 7 * 2)
    c1 =1.0 - ADAM_B1 ** ADAM_STEP
    c2 = 1.0 - ADAM_B2 ** ADAM_STEP

    def body(w_ref, g_ref, m_ref, v_ref, d_ref, mo_ref, vo_ref):
        g = g_ref[...]
        m = ADAM_B1 * m_ref[...] + (1.0 - ADAM_B1) * g
        v = ADAM_B2 * v_ref[...] + (1.0 - ADAM_B2) * jnp.square(g)
        d_ref[...] = -ADAM_LR * ((m / c1) / (jnp.sqrt(v / c2) + ADAM_EPS) + ADAM_WD * w_ref[...])
        mo_ref[...] = m
        vo_ref[...] = v

    spec = pl.BlockSpec((tr, C), lambda i: (i, 0))
    return pl.pallas_call(
        body, name=name, grid=(R // tr,), in_specs=[spec] * 4, out_specs=(spec,) * 3,
        out_shape=(jax.ShapeDtypeStruct((R, C), f32),) * 3,
        compiler_params=_cparams(("parallel",)),
    )(w, g, m, v)


def _adamw_big(w, m, v, l0, mines, others, core, after, name, half_cols=False, prev=None):
    L, R, C = w.shape
    nl = len(mines)
    n_prev = 1 if prev is None else 5
    prev = (after,) + tuple(prev or ())
    hr, hc = (R, C // 2) if half_cols else (R // 2, C)
    tr = _pick_rows(hr, hc * 4 * (7 + 2 * nl) * 2)
    nt = hr // tr
    c1 = 1.0 - ADAM_B1 ** ADAM_STEP
    c2 = 1.0 - ADAM_B2 ** ADAM_STEP

    def body(core_ref, w_ref, m_ref, v_ref, *rest):
        g_refs, (go_ref, d_ref, mo_ref, vo_ref) = rest[:2 * nl], rest[2 * nl + n_prev:]
        l, h = pl.program_id(0), pl.program_id(1)
        own = h == core_ref[0]
        g = jnp.where(own, g_refs[0][...], g_refs[nl][...])
        for k in range(1, nl):
            g = jnp.where(l == k, jnp.where(own, g_refs[k][...], g_refs[nl + k][...]), g)
        m = ADAM_B1 * m_ref[...] + (1.0 - ADAM_B1) * g
        v = ADAM_B2 * v_ref[...] + (1.0 - ADAM_B2) * jnp.square(g)
        go_ref[...] = g
        d_ref[...] = -ADAM_LR * ((m / c1) / (jnp.sqrt(v / c2) + ADAM_EPS) + ADAM_WD * w_ref[...])
        mo_ref[...] = m
        vo_ref[...] = v

    if half_cols:
        lay = pl.BlockSpec((None, tr, hc), lambda l, h, i, core_ref: (l0 + l, i, h))
    else:
        lay = pl.BlockSpec((None, tr, hc), lambda l, h, i, core_ref: (l0 + l, h * nt + i, 0))
    gspec = lambda k: pl.BlockSpec((tr, hc), lambda l, h, i, core_ref: (jnp.where(l == k, i, 0), 0))
    return pl.pallas_call(
        body, name=name,
        grid_spec=pltpu.PrefetchScalarGridSpec(
            num_scalar_prefetch=1, grid=(nl, 2, nt),
            in_specs=[lay, lay, lay] + [gspec(k) for k in range(nl)] * 2 + _hbm_specs(n_prev), out_specs=(lay,) * 4),
        out_shape=(jax.ShapeDtypeStruct((L, R, C), f32),) * 4,
        input_output_aliases={5 + 2 * nl + k: k for k in range(n_prev - 1)},
        compiler_params=_cparams(("arbitrary", "arbitrary", "arbitrary")),
    )(core, w, m, v, *mines, *others, *prev)


def _sum_blocks(x, out_dtype, name):
    n, R, C = x.shape
    tr = _pick_rows(R, C * 4 * (n + 1) * 2)

    def body(x_ref, o_ref):
        acc = x_ref[0].astype(f32)
        for k in range(1, n):
            acc = acc + x_ref[k].astype(f32)
        o_ref[...] = acc.astype(o_ref.dtype)

    return pl.pallas_call(
        body, name=name, grid=(R // tr,),
        in_specs=[pl.BlockSpec((n, tr, C), lambda i: (0, i, 0))], out_specs=pl.BlockSpec((tr, C), lambda i: (i, 0)),
        out_shape=jax.ShapeDtypeStruct((R, C), out_dtype),
        compiler_params=_cparams(("parallel",)),
    )(x)


def _hbm_specs(n):
    return [pl.BlockSpec(memory_space=pl.ANY)] * n


def _row_map(cfg):
    nc, k0 = cfg.IN_WIDTH // 4, cfg.o_mg

    def padded(o):
        return o if o < k0 else (cfg.o_mkr + o - k0 if o < k0 + ROPE else o - ROPE)

    cuts = {0, nc}
    for q in range(4):
        cuts |= {b - q * nc for b in (k0, k0 + ROPE) if q * nc < b < (q + 1) * nc}
    cuts = sorted(cuts)
    return [((l0, l1 - l0), tuple(padded(q * nc + l0) for q in range(4))) for l0, l1 in zip(cuts[:-1], cuts[1:])]


def _chip_start(q, starts):
    st = starts[0]
    for i in range(1, 4):
        st = jnp.where(q == i, starts[i], st)
    return pl.multiple_of(st, 16)


def _allgather8(x_shard, name, after=None):
    m, n = x_shard.shape

    def body(x_ref, _, out_ref, send_sems, recv_sems, local_sem):
        x, y, c = lax.axis_index("x"), lax.axis_index("y"), lax.axis_index("c")
        me, sibling = (x, y, c), (x, y, 1 - c)
        chips = [(1 - x, y), (x, 1 - y), (1 - x, 1 - y)]

        def rows(px, py, pc):
            return out_ref.at[pl.ds((4 * px + 2 * py + pc) * m, m), :]

        def copy(k, block, to, src=None):
            return pltpu.make_async_remote_copy(
                src_ref=rows(*block) if src is None else src, dst_ref=rows(*block),
                send_sem=send_sems.at[k], recv_sem=recv_sems.at[k], device_id=to, device_id_type=MESH)

        mine = pltpu.make_async_copy(x_ref, rows(*me), local_sem)
        mine.start()
        first = [copy(0, me, sibling, src=x_ref)] + [copy(1 + j, me, (*chip, c), src=x_ref) for j, chip in enumerate(chips)]
        for cp in first:
            cp.start()
        passed = [copy(4 + j, (*chip, c), sibling) for j, chip in enumerate(chips)]
        for j, chip in enumerate(chips):
            copy(1 + j, (*chip, c), me).wait_recv()
            passed[j].start()
        copy(0, sibling, me).wait_recv()
        for j, chip in enumerate(chips):
            copy(4 + j, (*chip, 1 - c), me).wait_recv()
        for cp in first + passed:
            cp.wait_send()
        mine.wait()

    return pl.pallas_call(
        body, name=name, out_shape=jax.ShapeDtypeStruct((N_DEV * m, n), x_shard.dtype),
        in_specs=_hbm_specs(2), out_specs=pl.BlockSpec(memory_space=pl.ANY),
        scratch_shapes=[pltpu.SemaphoreType.DMA((7,)), pltpu.SemaphoreType.DMA((7,)), pltpu.SemaphoreType.DMA],
    )(x_shard, after if after is not None else jnp.zeros((SUBLANES, LANES), f32))


_SEM = pl.BlockSpec(memory_space=pltpu.SEMAPHORE)
_HBM = pl.BlockSpec(memory_space=pltpu.HBM)
_EFFECT = pltpu.SideEffectType.DATAFLOW_SIDE_EFFECTING


def _split_start(srcs, lands, after, plan, n, name):
    bufs = list(srcs) + list(lands)
    nb, ns = len(bufs), len(srcs)

    def body(*refs):
        send_sems, recv_sems = refs[nb + 1], refs[nb + 2]
        for k, (src, dst, _, dev) in enumerate(plan(refs[:ns], refs[ns:nb])):
            pltpu.make_async_remote_copy(src_ref=src, dst_ref=dst, send_sem=send_sems.at[k], recv_sem=recv_sems.at[k],
                                         device_id=dev, device_id_type=MESH).start()
        refs[-1][...] = jnp.zeros_like(refs[-1])

    out = pl.pallas_call(
        body, name=name,
        out_shape=(pltpu.SemaphoreType.DMA((n,)), pltpu.SemaphoreType.DMA((n,)), *[pltpu.HBM(b.shape, b.dtype) for b in bufs],
                   jax.ShapeDtypeStruct((SUBLANES, LANES), f32)),
        in_specs=[_HBM] * nb + [pl.BlockSpec(memory_space=pl.ANY)],
        out_specs=(_SEM, _SEM, *[_HBM] * nb, pl.BlockSpec(memory_space=pltpu.VMEM)),
        input_output_aliases={i: 2 + i for i in range(nb)},
        compiler_params=pltpu.CompilerParams(has_side_effects=_EFFECT),
    )(*[pltpu.with_memory_space_constraint(b, pltpu.HBM) for b in bufs], after)
    return out[0], out[1], list(out[2:2 + ns]), list(out[2 + ns:2 + nb]), out[-1]


def _split_wait(srcs, lands, send_sems, recv_sems, after, plan, name):
    bufs = list(srcs) + list(lands)
    nb, ns = len(bufs), len(srcs)

    def body(*refs):
        send, recv = refs[nb], refs[nb + 1]
        for k, (src, _, dst, dev) in enumerate(plan(refs[:ns], refs[ns:nb])):
            cp = pltpu.make_async_remote_copy(src_ref=src, dst_ref=dst, send_sem=send.at[k], recv_sem=recv.at[k],
                                              device_id=dev, device_id_type=MESH)
            cp.wait_send()
            cp.wait_recv()

    out = pl.pallas_call(
        body, name=name, out_shape=tuple(pltpu.HBM(b.shape, b.dtype) for b in bufs),
        in_specs=[_HBM] * nb + [_SEM, _SEM, pl.BlockSpec(memory_space=pl.ANY)], out_specs=tuple([_HBM] * nb),
        input_output_aliases={i: i for i in range(nb)},
        compiler_params=pltpu.CompilerParams(has_side_effects=_EFFECT),
    )(*bufs, send_sems, recv_sems, after)
    return list(out[:ns]), list(out[ns:])


def _weight_windows(cfg, shards, out_refs, px, py, pc):
    rmap = _row_map(cfg)
    m, n = shards[0].shape
    cols = pl.ds(pl.multiple_of(pc * n, n), n)
    wins = [[(pl.ds(l0, cnt), out_refs[0].at[pl.ds(_chip_start(2 * px + py, starts), cnt), cols]) for (l0, cnt), starts in rmap]]
    for a in range(1, len(shards)):
        m = shards[a].shape[0]
        wins.append([(pl.ds(0, m), out_refs[a].at[pl.ds((4 * px + 2 * py + pc) * m, m), :])])
    return wins


def _gather_shapes(cfg, shards):
    return [jax.ShapeDtypeStruct((cfg.NP, cfg.D), shards[0].dtype)] + \
           [jax.ShapeDtypeStruct((N_DEV * s.shape[0], s.shape[1]), s.dtype) for s in shards[1:]]


def _gather_plan(cfg, shards):
    def plan(x_refs, land_refs):
        x, y, c = lax.axis_index("x"), lax.axis_index("y"), lax.axis_index("c")
        mine = _weight_windows(cfg, shards, land_refs, x, y, c)
        out = []
        for peer in [(x, y, 1 - c), (1 - x, y, c), (x, 1 - y, c), (1 - x, 1 - y, c)]:
            theirs = _weight_windows(cfg, shards, land_refs, *peer)
            for a in range(len(shards)):
                for (rows, win), (_, win_in) in zip(mine[a], theirs[a]):
                    out.append((x_refs[a].at[rows, :], win, win_in, peer))
        return out
    return plan


def _gather_finish(cfg, shards, lands, name):
    na = len(shards)
    rmap = _row_map(cfg)
    npc = len(rmap)
    nz = cfg.NP - cfg.o_mkr - ROPE

    def body(*refs):
        x_refs, out_refs = refs[:na], refs[2 * na:3 * na]
        stage, zbuf = refs[3 * na:4 * na], refs[4 * na]
        send_sems, recv_sems, local_sems = refs[4 * na + 1:]
        x, y, c = lax.axis_index("x"), lax.axis_index("y"), lax.axis_index("c")
        sibling = (x, y, 1 - c)
        chips = [(1 - x, y), (x, 1 - y), (1 - x, 1 - y)]
        load = [pltpu.make_async_copy(x_refs[a], stage[a], local_sems.at[a, npc]) for a in range(na)]
        for cp in load:
            cp.start()
        passed = []
        for j, chip in enumerate(chips):
            wins = _weight_windows(cfg, shards, out_refs, *chip, c)
            for a in range(na):
                passed += [pltpu.make_async_remote_copy(src_ref=win, dst_ref=win, send_sem=send_sems.at[a, j, p],
                                                        recv_sem=recv_sems.at[a, j, p], device_id=sibling, device_id_type=MESH)
                           for p, (_, win) in enumerate(wins[a])]
        for cp in passed:
            cp.start()
        zbuf[...] = jnp.zeros_like(zbuf)
        for cp in load:
            cp.wait()
        own = _weight_windows(cfg, shards, out_refs, x, y, c)
        store = [pltpu.make_async_copy(stage[a].at[rows, :], win, local_sems.at[a, p])
                 for a in range(na) for p, (rows, win) in enumerate(own[a])]
        store.append(pltpu.make_async_copy(zbuf, out_refs[0].at[pl.ds(cfg.NP - nz, nz), :], local_sems.at[0, npc + 1]))
        for cp in store:
            cp.start()
        for j, chip in enumerate(chips):
            wins = _weight_windows(cfg, shards, out_refs, *chip, 1 - c)
            for a in range(na):
                for p, (_, win) in enumerate(wins[a]):
                    pltpu.make_async_remote_copy(src_ref=win, dst_ref=win, send_sem=send_sems.at[a, j, p],
                                                 recv_sem=recv_sems.at[a, j, p], device_id=sibling,
                                                 device_id_type=MESH).wait_recv()
        for cp in passed:
            cp.wait_send()
        for cp in store:
            cp.wait()

    return pl.pallas_call(
        body, name=name, out_shape=_gather_shapes(cfg, shards),
        in_specs=_hbm_specs(2 * na), out_specs=_hbm_specs(na),
        input_output_aliases={na + a: a for a in range(na)},
        scratch_shapes=[pltpu.VMEM(s.shape, s.dtype) for s in shards] + [pltpu.VMEM((nz, cfg.D), shards[0].dtype)]
        + [pltpu.SemaphoreType.DMA((na, 3, npc)), pltpu.SemaphoreType.DMA((na, 3, npc)), pltpu.SemaphoreType.DMA((na, npc + 2))],
        compiler_params=pltpu.CompilerParams(vmem_limit_bytes=VMEM_LIMIT),
    )(*shards, *lands)


def _gather_weights_start(cfg, shards, after):
    lands = [lax.empty(s.shape, s.dtype) for s in _gather_shapes(cfg, shards)]
    n = 4 * (len(_row_map(cfg)) + len(shards) - 1)
    return _split_start(shards, lands, after, _gather_plan(cfg, shards), n, "gather_w_start")


def _gather_weights_end(cfg, shards, started, after):
    send_sems, recv_sems, srcs, lands, _ = started
    srcs, lands = _split_wait(srcs, lands, send_sems, recv_sems, after, _gather_plan(cfg, shards), "gather_w_wait")
    return _gather_finish(cfg, srcs, lands, "gather_w_finish")


def _send_sibling(arrays, name):
    na = len(arrays)

    def body(*refs):
        x_refs, out_refs = refs[:na], refs[na:2 * na]
        send_sems, recv_sems = refs[2 * na:]
        sibling = (lax.axis_index("x"), lax.axis_index("y"), 1 - lax.axis_index("c"))
        cps = [pltpu.make_async_remote_copy(src_ref=x_refs[a], dst_ref=out_refs[a], send_sem=send_sems.at[a],
                                            recv_sem=recv_sems.at[a], device_id=sibling, device_id_type=MESH)
               for a in range(na)]
        for cp in cps:
            cp.start()
        for cp in cps:
            cp.wait()

    return pl.pallas_call(
        body, name=name, out_shape=[jax.ShapeDtypeStruct(x.shape, x.dtype) for x in arrays],
        in_specs=_hbm_specs(na), out_specs=_hbm_specs(na),
        scratch_shapes=[pltpu.SemaphoreType.DMA((na,)), pltpu.SemaphoreType.DMA((na,))],
    )(*arrays)


def _slot_pairs(cfg, p_refs, slot_refs, a, to_chip, slot):
    if a == 0:
        return [(p_refs[0].at[pl.ds(_chip_start(to_chip, starts), cnt), :], slot_refs[0].at[slot, pl.ds(l0, cnt), :])
                for (l0, cnt), starts in _row_map(cfg)]
    return [(p_refs[a].at[to_chip], slot_refs[a].at[slot])]


def _scatter_plan(cfg, na):
    def plan(p_refs, slot_refs):
        x, y, c = lax.axis_index("x"), lax.axis_index("y"), lax.axis_index("c")
        mychip = 2 * x + y
        out = []
        for cx, cy in [(1 - x, y), (x, 1 - y), (1 - x, 1 - y)]:
            q = 2 * cx + cy
            for a in range(na):
                for (src, dst), (_, dst_in) in zip(_slot_pairs(cfg, p_refs, slot_refs, a, q, mychip),
                                                   _slot_pairs(cfg, p_refs, slot_refs, a, mychip, q)):
                    out.append((src, dst, dst_in, (cx, cy, c)))
        return out
    return plan


def _slot_shapes(cfg, parts):
    return [jax.ShapeDtypeStruct((4, cfg.IN_WIDTH // 4, parts[0].shape[1]), parts[0].dtype)] + \
           [jax.ShapeDtypeStruct(p.shape, p.dtype) for p in parts[1:]]


def _place_own(cfg, parts, slots, name):
    na = len(parts)
    npc = len(_row_map(cfg))
    shapes = _slot_shapes(cfg, parts)

    def body(*refs):
        p_refs, out_refs = refs[:na], refs[2 * na:3 * na]
        stage, sems = refs[3 * na:4 * na], refs[4 * na]
        mychip = 2 * lax.axis_index("x") + lax.axis_index("y")
        moves = []
        for a in range(na):
            for p, (src, dst) in enumerate(_slot_pairs(cfg, p_refs, out_refs, a, mychip, mychip)):
                buf = stage[a].at[pl.ds(*_row_map(cfg)[p][0]), :] if a == 0 else stage[a]
                moves.append((pltpu.make_async_copy(src, buf, sems.at[a, p]), pltpu.make_async_copy(buf, dst, sems.at[a, npc + p])))
        for load, _ in moves:
            load.start()
        for load, store in moves:
            load.wait()
            store.start()
        for _, store in moves:
            store.wait()

    return pl.pallas_call(
        body, name=name, out_shape=shapes, in_specs=_hbm_specs(2 * na), out_specs=_hbm_specs(na),
        input_output_aliases={na + a: a for a in range(na)},
        scratch_shapes=[pltpu.VMEM(s.shape[1:], s.dtype) for s in shapes] + [pltpu.SemaphoreType.DMA((na, 2 * npc))],
        compiler_params=pltpu.CompilerParams(vmem_limit_bytes=VMEM_LIMIT),
    )(*parts, *slots)


def _add_half(g, got, core, name):
    if g.ndim == 2:
        R, hd = got.shape
        tr = _pick_rows(R, hd * 4 * 3 * 2)
        grid = (R // tr,)
        g_spec = pl.BlockSpec((tr, hd), lambda i, core_ref: (i, core_ref[0]))
        o_spec = pl.BlockSpec((tr, hd), lambda i, core_ref: (i, 0))
    else:
        _, hr, nc = got.shape
        tr = _pick_rows(hr, nc * 4 * 3 * 2)
        grid = (4, hr // tr)
        g_spec = pl.BlockSpec((None, None, tr, nc), lambda q, i, core_ref: (q, core_ref[0], i, 0))
        o_spec = pl.BlockSpec((None, tr, nc), lambda q, i, core_ref: (q, i, 0))

    def body(core_ref, g_ref, got_ref, o_ref):
        o_ref[...] = (g_ref[...].astype(f32) + got_ref[...].astype(f32)).astype(o_ref.dtype)

    return pl.pallas_call(
        body, name=name,
        grid_spec=pltpu.PrefetchScalarGridSpec(num_scalar_prefetch=1, grid=grid, in_specs=[g_spec, o_spec], out_specs=o_spec),
        out_shape=jax.ShapeDtypeStruct(got.shape, bf16),
        compiler_params=_cparams(("arbitrary",) * len(grid)),
    )(core, g, got)


def _pair_exchange(cfg, g_in_t, grads, name):
    na = 1 + len(grads)
    hd = cfg.D // 2

    def body(*refs):
        g_refs, out_refs = refs[:na], refs[na:2 * na]
        send_sems, recv_sems = refs[2 * na:]
        x, y, c = lax.axis_index("x"), lax.axis_index("y"), lax.axis_index("c")
        cps = [pltpu.make_async_remote_copy(
            src_ref=g_refs[0].at[:, pl.ds(pl.multiple_of((1 - c) * hd, hd), hd)], dst_ref=out_refs[0],
            send_sem=send_sems.at[0, 0], recv_sem=recv_sems.at[0, 0], device_id=(x, y, 1 - c), device_id_type=MESH)]
        for a in range(1, na):
            cps += [pltpu.make_async_remote_copy(src_ref=g_refs[a].at[q, 1 - c], dst_ref=out_refs[a].at[q],
                                                 send_sem=send_sems.at[a, q], recv_sem=recv_sems.at[a, q],
                                                 device_id=(x, y, 1 - c), device_id_type=MESH) for q in range(4)]
        for cp in cps:
            cp.start()
        for cp in cps:
            cp.wait()

    out_shape = [jax.ShapeDtypeStruct((cfg.NP, hd), g_in_t.dtype)] + \
                [jax.ShapeDtypeStruct((4,) + g.shape[2:], g.dtype) for g in grads]
    return pl.pallas_call(
        body, name=name, out_shape=out_shape, in_specs=_hbm_specs(na), out_specs=_hbm_specs(na),
        scratch_shapes=[pltpu.SemaphoreType.DMA((na, 4)), pltpu.SemaphoreType.DMA((na, 4))],
    )(g_in_t, *grads)


def _reduce_scatter_start(cfg, g_in_t, grads, after):
    core = lax.axis_index("c").astype(jnp.int32).reshape(1)
    got = _pair_exchange(cfg, g_in_t, grads, "rs_pair")
    part = [_add_half(g, h, core, "rs_add_pair") for g, h in zip([g_in_t] + list(grads), got)]
    slots = [lax.empty(s.shape, s.dtype) for s in _slot_shapes(cfg, part)]
    n = 3 * (len(_row_map(cfg)) + len(part) - 1)
    return _split_start(part, slots, after, _scatter_plan(cfg, len(part)), n, "rs_chips_start")


def _reduce_scatter_end(cfg, started, after):
    send_sems, recv_sems, parts, slots, _ = started
    parts, slots = _split_wait(parts, slots, send_sems, recv_sems, after, _scatter_plan(cfg, len(parts)), "rs_chips_wait")
    slots = _place_own(cfg, parts, slots, "rs_own")
    mine = [_sum_blocks(s, f32, "rs_add_chips") for s in slots]
    return mine, _send_sibling(mine, "rs_halves")


def _big_weights(cfg):
    return (("mla_w_uq", cfg.QL, cfg.QW, 1), ("mla_w_ukv", cfg.KL, cfg.KVW, 1),
            ("w_branch", cfg.RW + cfg.LW + cfg.MW, cfg.D, 0), ("w_out", cfg.D, cfg.D, 0))


def _half_shapes(cfg):
    out = []
    for _, r, c, ax in _big_weights(cfg):
        out.append((r // 2, c // 4) if ax == 1 else (r // 8, c))
    return out


def _my_halves(cfg, W, l, c):
    hd = cfg.D // 2
    out = [lax.dynamic_slice_in_dim(W["w_in"][l].T, c * hd, hd, axis=1).astype(bf16)]
    for (name, *_), (hr, nc) in zip(_big_weights(cfg), _half_shapes(cfg)):
        out.append(lax.dynamic_slice_in_dim(W[name][l], c * hr, hr, axis=0).astype(bf16))
    return out


def _uq_split(cfg, w):
    hw = HEAD + ROPE
    return jnp.concatenate([w[:, h * hw:h * hw + HEAD] for h in range(cfg.MH)]
                           + [w[:, h * hw + HEAD:(h + 1) * hw] for h in range(cfg.MH)], axis=1)


def _uq_join(cfg, g):
    n = cfg.MH * HEAD
    parts = []
    for h in range(cfg.MH):
        parts += [g[:, h * HEAD:(h + 1) * HEAD], g[:, n + h * ROPE:n + (h + 1) * ROPE]]
    return jnp.concatenate(parts, axis=1)


def _col_blocks(g):
    nc = g.shape[1] // 4
    return jnp.stack([g[:, q * nc:(q + 1) * nc] for q in range(4)])


def _row_pack(parts):
    rows = []
    for p in parts:
        r = p.reshape(-1, LANES)
        pad = -r.shape[0] % SUBLANES
        rows.append(jnp.concatenate([r, jnp.zeros((pad, LANES), r.dtype)], axis=0) if pad else r)
    return jnp.concatenate(rows, axis=0)


def _row_unpack(packed, like):
    out, off = [], 0
    for p in like:
        n = p.size // LANES
        out.append(packed[off:off + n].reshape(p.shape))
        off += -(-n // SUBLANES) * SUBLANES
    return out


def _prep_layer(cfg, full, small):
    w_in_t, w_uq, w_ukv, w_branch, w_out = full
    RW, LW = cfg.RW, cfg.LW
    P = dict(small)
    P["w_in_t"] = w_in_t
    P["w_uq"] = _uq_split(cfg, jnp.concatenate(list(w_uq.reshape(4, cfg.QL, -1)), axis=1))
    P["w_ukv"] = jnp.concatenate(list(w_ukv.reshape(4, cfg.KL, -1)), axis=1)
    P["wb"] = (w_branch[:RW], w_branch[RW:RW + LW], w_branch[RW + LW:])
    P["w_out"] = w_out
    return P


def _layer_fwd(cfg, x, mod, P, T):
    h = _prenorm_fwd(cfg, x, mod, P["norm_pre"])
    proj = _mm(h, P["w_in_t"], f32, "mm_proj", mode="nt")
    y_ret = _ret_fwd(cfg, proj, P["ret_gn"], T["cos_r"], T["sin_r"], T["ret_consts"])
    a, b = _lru_gates(cfg, proj, P["lru_conv_w"], P["lru_conv_b"], P["lru_wa"], P["lru_ba"], P["lru_wx"], P["lru_bx"],
                      P["lru_lambda"])
    hl, y_lru = _lru_scan_fwd(cfg, proj, a, b)
    qn, kn = _mla_norm(cfg, proj, P["mla_q_norm"], P["mla_kv_norm"])
    q = _mm(qn, P["w_uq"], f32, "mm_uq")
    kv = _mm(kn, P["w_ukv"], f32, "mm_ukv")
    q3, k3, v3 = _mla_pack(cfg, proj, q, kv, T["cos_q"], T["sin_q"], T["cos_k"], T["sin_k"])
    o, y_mla = _mla_attn_fwd(cfg, proj, q3, k3, v3)
    ys = (y_ret, y_lru, y_mla)
    us = tuple(_mm(yb, wb, f32, "mm_branch") for yb, wb in zip(ys, P["wb"]))
    merged = _merge_fwd(cfg, proj, *us)
    y = _mm(merged, P["w_out"], f32, "mm_out")
    out = _postnorm_fwd(cfg, x, y, mod, P["norm_post"])
    R = dict(x=x, h=h, proj=proj, ys=ys, a=a, hl=hl, qn=qn, kn=kn, q3=q3, k3=k3, v3=v3, o=o, us=us, merged=merged, y=y)
    return out, R


def _layer_bwd(cfg, dout, R, mod, P, T):
    proj = R["proj"]
    dy, s_post = _postnorm_bwd(cfg, dout, R["y"], mod, P["norm_post"])
    dmerged = _mm(dy, P["w_out"], f32, "mm_dmerged", mode="nt")
    g_out = _mm(R["merged"], dy, bf16, "mm_gw_out", mode="tn")
    du0, du1, du2, dlog = _merge_bwd(cfg, proj, dmerged, *R["us"])
    dus = (du0, du1, du2)
    dys = tuple(_mm(du, wb, f32, "mm_dbranch", mode="nt") for du, wb in zip(dus, P["wb"]))
    g_branch = jnp.concatenate([_mm(yb, du, bf16, "mm_gw_branch", mode="tn") for yb, du in zip(R["ys"], dus)], axis=0)
    drq, drk, drv, drg, dgn = _ret_bwd(cfg, proj, dys[0], P["ret_gn"], T["cos_r"], T["sin_r"], T["ret_consts"])
    da, db, dlg = _lru_scan_bwd(cfg, proj, R["a"], R["hl"], dys[1])
    dxc, dwa, dwx, s_lru = _lru_gates_bwd(cfg, proj, da, db, P["lru_conv_w"], P["lru_conv_b"], P["lru_wa"], P["lru_ba"],
                                          P["lru_wx"], P["lru_bx"], P["lru_lambda"])
    dlx, s_conv = _lru_conv_bwd(cfg, proj, dxc, P["lru_conv_w"])
    dq3, dk3, dv3, dmg = _mla_attn_bwd(cfg, proj, R["q3"], R["k3"], R["v3"], R["o"], dys[2])
    dq, dkv, dmkr = _mla_unpack_bwd(cfg, dq3, dk3, dv3, T["cos_q"], T["sin_q"], T["cos_k"], T["sin_k"])
    dqn = _mm(dq, P["w_uq"], f32, "mm_dqn", mode="nt")
    dkn = _mm(dkv, P["w_ukv"], f32, "mm_dkn", mode="nt")
    g_uq = _uq_join(cfg, _mm(R["qn"], dq, bf16, "mm_gw_uq", mode="tn"))
    g_ukv = _mm(R["kn"], dkv, bf16, "mm_gw_ukv", mode="tn")
    dmq, dmkv, s_q, s_k = _mla_norm_bwd(cfg, proj, dqn, dkn, P["mla_q_norm"], P["mla_kv_norm"])
    dproj = jnp.concatenate([drq, drk, drv, drg, dlx, dlg, dmq, dmkv, dmg, dlog, dmkr,
                             jnp.zeros((cfg.S, cfg.NP - cfg.o_mkr - HEAD), bf16)], axis=1)
    dh = _mm(dproj, P["w_in_t"], f32, "mm_dh")
    g_in_t = _mm(dproj, R["h"], bf16, "mm_gw_in", mode="tn", tm=512)
    dx, s_pre = _prenorm_bwd(cfg, R["x"], dh, dout, mod, P["norm_pre"])
    big = [_col_blocks(g_uq), _col_blocks(g_ukv), g_branch, g_out]
    big = (g_in_t, [g.reshape(4, 2, hr, nc) for g, (hr, nc) in zip(big, _half_shapes(cfg))])
    small = dict(norm_pre=s_pre[2:3], norm_post=s_post[1:2], ret_gn=dgn, lru_conv_w=s_conv[0:CONV], lru_conv_b=s_conv[CONV:CONV + 1],
                 lru_wa=dwa, lru_ba=s_lru[0:1], lru_wx=dwx, lru_bx=s_lru[1:2], lru_lambda=s_lru[2:3],
                 mla_q_norm=s_q[0:1], mla_kv_norm=s_k[0:1])
    dmod = jnp.concatenate([s_pre[0:1], s_pre[1:2], s_post[0:1]], axis=1)
    return dx, big, small, dmod


_SMALL = ("norm_pre", "norm_post", "ret_gn", "lru_conv_w", "lru_conv_b", "lru_wa", "lru_ba", "lru_wx", "lru_bx", "lru_lambda",
          "mla_q_norm", "mla_kv_norm")
_WEIGHTS = ("ada_w", "ada_b", "norm_pre", "norm_post", "w_in", "ret_gn", "lru_conv_w", "lru_conv_b", "lru_wa", "lru_ba", "lru_wx",
            "lru_bx", "lru_lambda", "mla_q_norm", "mla_w_uq", "mla_kv_norm", "mla_w_ukv", "w_branch", "w_out")


def _step(cfg, x, c, positions, W, target, M1, V1):
    L, D = cfg.L, cfg.D
    xi, yi, ci = lax.axis_index("x"), lax.axis_index("y"), lax.axis_index("c")
    chip = 2 * xi + yi
    me = 2 * chip + ci

    c8 = jnp.concatenate([c, jnp.zeros((SUBLANES - 1, D), f32)], axis=0)
    c_all = _allgather8(c8, "gather_c").reshape(N_DEV, SUBLANES, D)[:, 0]
    mod_sh, c_act = _ada_fwd(cfg, c_all, W["ada_w"])
    n_sh = mod_sh.shape[2]
    mod_half = lax.dynamic_slice_in_dim(mod_sh, ci * (n_sh // 2), n_sh // 2, axis=2).reshape(L * N_DEV, n_sh // 2)
    mod_all = _allgather8(mod_half, "gather_mod").reshape(N_DEV, L, N_DEV, n_sh // 2)
    mod_all = mod_all.transpose(1, 2, 0, 3).reshape(L, N_DEV, 3 * D)
    mods = lax.dynamic_index_in_dim(mod_all, me, axis=1, keepdims=False) + W["ada_b"]

    (cos_r, sin_r), (cos_m, sin_m) = _rope_tables(cfg, positions)
    T = dict(cos_r=cos_r, sin_r=sin_r, cos_q=jnp.tile(cos_m, (1, cfg.MH)), sin_q=jnp.tile(sin_m, (1, cfg.MH)),
             cos_k=jnp.tile(cos_m, (1, 2)), sin_k=jnp.tile(sin_m, (1, 2)), ret_consts=_ret_consts(cfg))

    Ps, Rs = [], []
    act = x[0]
    cw_all = _allgather8(_pad_rows(W["lru_conv_w"].reshape(L * CONV, -1)), "gather_conv", after=mods)
    started = _gather_weights_start(cfg, _my_halves(cfg, W, 0, ci), cw_all)
    cw_rows = cw_all.shape[0] // N_DEV
    cw_all = cw_all.reshape(4, 2, cw_rows, -1)[:, 0, :L * CONV].transpose(1, 0, 2).reshape(L, CONV, cfg.LW)
    after = mods
    for l in range(L):
        gathered = _gather_weights_end(cfg, started[2], started, after)
        small = {k: (W[k][l] if W[k][l].ndim > 1 else W[k][l][None, :]) for k in _SMALL if k != "lru_conv_w"}
        P = _prep_layer(cfg, gathered, small)
        P["lru_conv_w"] = cw_all[l]
        Ps.append(P)
        mod = mods[l:l + 1]
        if l + 1 < L:
            started = _gather_weights_start(cfg, _my_halves(cfg, W, l + 1, ci), gathered[-1])
            mod = mod + started[4][0, 0]
        act, R = _layer_fwd(cfg, act, mod, P, T)
        Rs.append(R)
        after = act

    dact, lsum = _loss_head(cfg, act, target[0])
    loss = lax.psum(lsum[0, 0], ("x", "y", "c"))

    big_g = [None] * L
    small_g = [None] * L
    dmods = [None] * L
    pending = None
    for l in range(L - 1, -1, -1):
        mod = mods[l:l + 1]
        if pending is not None:
            mod = mod + pending[4][0, 0]
        dact, grads, small_g[l], dmods[l] = _layer_bwd(cfg, dact, Rs[l], mod, Ps[l], T)
        if pending is not None:
            big_g[l + 1] = _reduce_scatter_end(cfg, pending, dact)
        if l > 0:
            pending = _reduce_scatter_start(cfg, *grads, grads[1][-1])

    dmod = jnp.concatenate(dmods, axis=0)
    parts = [dmod] + [small_g[l][k] for l in range(L) for k in _SMALL]
    packed = _row_pack(parts)
    allf = _allgather8(packed, "gather_small")
    pending = _reduce_scatter_start(cfg, *grads, allf)
    tok = pending[4][0, 0]
    allf = allf.reshape(N_DEV, packed.shape[0], LANES)
    summed = _row_unpack(_sum_blocks(allf, f32, "sum_small"), parts)
    gsm = {k: jnp.stack([summed[1 + l * len(_SMALL) + i].reshape(W[k].shape[1:] if k != "lru_conv_w" else (CONV, cfg.LW))
                         for l in range(L)]) for i, k in enumerate(_SMALL)}
    ncw = cfg.LW // 4
    gsm["lru_conv_w"] = lax.dynamic_slice_in_dim(gsm["lru_conv_w"], chip * ncw, ncw, axis=2)
    gsm["ada_b"] = summed[0]
    dmod_all = allf[:, :dmod.size // LANES].reshape(N_DEV, L, 3 * D)
    dmod_sh = lax.dynamic_slice_in_dim(dmod_all, chip * n_sh, n_sh, axis=2).transpose(1, 0, 2) + tok
    G = dict(gsm)
    G["ada_w"] = _ada_bwd(cfg, c_act.T, dmod_sh)
    delta, new_m, new_v = {}, {}, {}
    bigs = ("ada_w", "w_in") + tuple(name for name, *_ in _big_weights(cfg))
    shp = W["ada_w"].shape
    two = lambda a: a.reshape(-1, shp[-1])
    d, m_, v_ = _adamw(two(W["ada_w"]), two(G["ada_w"]), two(M1["ada_w"]), two(V1["ada_w"]), "adamw_ada_w")
    delta["ada_w"], new_m["ada_w"], new_v["ada_w"] = d.reshape(shp), m_.reshape(shp), v_.reshape(shp)
    smalls = [k for k in _WEIGHTS if k not in bigs]
    packs = [_row_pack([src[k] for k in smalls]) for src in (W, G, M1, V1)]
    packs[1] = packs[1] + tok
    outs = _adamw(*packs, "adamw_small")
    for dst, o in zip((delta, new_m, new_v), outs):
        for k, val in zip(smalls, _row_unpack(o, [W[k] for k in smalls])):
            dst[k] = val
    core = ci.astype(jnp.int32).reshape(1)
    tr_ = lambda a: a.transpose(0, 2, 1)

    def update(l0, l1, prev):
        res = {}
        for i, (name, *_) in enumerate(_big_weights(cfg)):
            res[name] = _adamw_big(W[name], M1[name], V1[name], l0, [big_g[l][0][i + 1] for l in range(l0, l1)],
                                   [big_g[l][1][i + 1] for l in range(l0, l1)], core, pending[4], "adamw_" + name,
                                   prev=prev and prev[name])
        res["w_in"] = _adamw_big(tr_(W["w_in"]), tr_(M1["w_in"]), tr_(V1["w_in"]), l0, [big_g[l][0][0] for l in range(l0, l1)],
                                 [big_g[l][1][0] for l in range(l0, l1)], core, pending[4], "adamw_w_in", half_cols=True,
                                 prev=prev and prev["w_in"])
        return res

    upper = update(1, L, None) if L > 1 else None
    big_g[0] = _reduce_scatter_end(cfg, pending, upper["w_in"][0] if upper else outs[0])
    res = update(0, 1, upper)
    for name, *_ in _big_weights(cfg):
        G[name], delta[name], new_m[name], new_v[name] = res[name]
    G["w_in"], delta["w_in"], new_m["w_in"], new_v["w_in"] = [tr_(o) for o in res["w_in"]]

    grad_x = dact[None]
    return (loss, grad_x, *[G[k] for k in _WEIGHTS], *[delta[k] for k in _WEIGHTS], *[new_m[k] for k in _WEIGHTS],
            *[new_v[k] for k in _WEIGHTS])


def _pad_rows(a):
    pad = -a.shape[0] % SUBLANES
    return jnp.concatenate([a, jnp.zeros((pad, a.shape[1]), a.dtype)], axis=0) if pad else a


def kernel(x, c, positions, ada_w, ada_b, norm_pre, norm_post, w_in, ret_gn, lru_conv_w, lru_conv_b, lru_wa, lru_ba, lru_wx, lru_bx, lru_lambda, mla_q_norm, mla_w_uq, mla_kv_norm, mla_w_ukv, w_branch, w_out, loss_target, m_ada_w, m_ada_b, m_norm_pre, m_norm_post, m_w_in, m_ret_gn, m_lru_conv_w, m_lru_conv_b, m_lru_wa, m_lru_ba, m_lru_wx, m_lru_bx, m_lru_lambda, m_mla_q_norm, m_mla_w_uq, m_mla_kv_norm, m_mla_w_ukv, m_w_branch, m_w_out, v_ada_w, v_ada_b, v_norm_pre, v_norm_post, v_w_in, v_ret_gn, v_lru_conv_w, v_lru_conv_b, v_lru_wa, v_lru_ba, v_lru_wx, v_lru_bx, v_lru_lambda, v_mla_q_norm, v_mla_w_uq, v_mla_kv_norm, v_mla_w_ukv, v_w_branch, v_w_out):
    W = dict(ada_w=ada_w, ada_b=ada_b, norm_pre=norm_pre, norm_post=norm_post, w_in=w_in, ret_gn=ret_gn, lru_conv_w=lru_conv_w,
             lru_conv_b=lru_conv_b, lru_wa=lru_wa, lru_ba=lru_ba, lru_wx=lru_wx, lru_bx=lru_bx, lru_lambda=lru_lambda,
             mla_q_norm=mla_q_norm, mla_w_uq=mla_w_uq, mla_kv_norm=mla_kv_norm, mla_w_ukv=mla_w_ukv, w_branch=w_branch, w_out=w_out)
    M1 = dict(ada_w=m_ada_w, ada_b=m_ada_b, norm_pre=m_norm_pre, norm_post=m_norm_post, w_in=m_w_in, ret_gn=m_ret_gn,
              lru_conv_w=m_lru_conv_w, lru_conv_b=m_lru_conv_b, lru_wa=m_lru_wa, lru_ba=m_lru_ba, lru_wx=m_lru_wx, lru_bx=m_lru_bx,
              lru_lambda=m_lru_lambda, mla_q_norm=m_mla_q_norm, mla_w_uq=m_mla_w_uq, mla_kv_norm=m_mla_kv_norm,
              mla_w_ukv=m_mla_w_ukv, w_branch=m_w_branch, w_out=m_w_out)
    V1 = dict(ada_w=v_ada_w, ada_b=v_ada_b, norm_pre=v_norm_pre, norm_post=v_norm_post, w_in=v_w_in, ret_gn=v_ret_gn,
              lru_conv_w=v_lru_conv_w, lru_conv_b=v_lru_conv_b, lru_wa=v_lru_wa, lru_ba=v_lru_ba, lru_wx=v_lru_wx, lru_bx=v_lru_bx,
              lru_lambda=v_lru_lambda, mla_q_norm=v_mla_q_norm, mla_w_uq=v_mla_w_uq, mla_kv_norm=v_mla_kv_norm,
              mla_w_ukv=v_mla_w_ukv, w_branch=v_w_branch, w_out=v_w_out)
    return _step(_CFG, x, c, positions, W, loss_target, M1, V1)
```

```python
from typing import NamedTuple

import numpy as np
import jax
import jax.numpy as jnp
from jax import lax
from jax.experimental import pallas as pl
from jax.experimental.pallas import tpu as pltpu

f32 = jnp.float32
bf16 = jnp.bfloat16

NORM_EPS = 1e-6
ROPE_BASE = 10000.0
CHUNK = 64
HEAD = 128
ROPE = 64
CONV = 4
LRU_C = 8.0
ADAM_LR, ADAM_B1, ADAM_B2, ADAM_EPS, ADAM_WD, ADAM_STEP = 0.001, 0.9, 0.999, 1e-08, 0.01, 10

LANES = 128
SUBLANES = 8
VMEM_LIMIT = 56 * 1024 * 1024
MM_BUDGET = 40 * 1024 * 1024
N_DEV = 8
MESH = pl.DeviceIdType.MESH


class Cfg(NamedTuple):
    D: int = 2048
    S: int = 2048
    L: int = 4
    H: int = 8
    NB: int = 8
    MH: int = 8
    QL: int = 512
    KL: int = 512
    TR: int = 256
    TQ: int = 512

    @property
    def RW(self): return self.H * HEAD
    @property
    def LW(self): return self.NB * HEAD
    @property
    def MW(self): return self.MH * HEAD
    @property
    def o_rk(self): return self.RW
    @property
    def o_rv(self): return 2 * self.RW
    @property
    def o_rg(self): return 3 * self.RW
    @property
    def o_lx(self): return 4 * self.RW
    @property
    def o_lg(self): return 4 * self.RW + self.LW
    @property
    def o_mq(self): return 4 * self.RW + 2 * self.LW
    @property
    def o_mkv(self): return self.o_mq + self.QL
    @property
    def o_mg(self): return self.o_mkv + self.KL
    @property
    def o_merge(self): return self.o_mg + self.MW
    @property
    def o_mkr(self): return self.o_merge + 3 * self.D
    @property
    def NP(self): return -(-(self.o_mkr + ROPE) // 512) * 512
    @property
    def IN_WIDTH(self): return self.o_mkr + ROPE
    @property
    def QW(self): return self.MH * (HEAD + ROPE)
    @property
    def KVW(self): return self.MH * 2 * HEAD


_CFG = Cfg()


def _cparams(sem=None):
    return pltpu.CompilerParams(dimension_semantics=sem, vmem_limit_bytes=VMEM_LIMIT)


def _sigmoid(x):
    return jax.nn.sigmoid(x)


def _silu(x):
    return x * _sigmoid(x)


def _dsilu(x):
    s = _sigmoid(x)
    return s * (1.0 + x * (1.0 - s))


def _slab(rows, width, off):
    assert off % width == 0
    return pl.BlockSpec((rows, width), lambda i, _c=off // width: (i, _c))


def _row(width):
    return pl.BlockSpec((1, width), lambda i: (0, 0))


def _mm(a, b, out_dtype=f32, name="mm", mode="nn", tm=None):
    (M, K) = a.shape if mode != "tn" else a.shape[::-1]
    (K2, N) = b.shape if mode != "nt" else b.shape[::-1]
    assert K == K2
    tn = N if N <= 2048 else 512
    tk = K if K <= 2048 else 512
    assert N % tn == 0 and K % tk == 0
    osz = jnp.dtype(out_dtype).itemsize
    if tm is None:
        tm = M
        while 2 * tm * tk * 2 + 2 * tk * tn * 2 + 2 * tm * tn * osz + tm * tn * 4 > MM_BUDGET and tm % 16 == 0:
            tm //= 2
    assert M % tm == 0
    nk = K // tk
    dims = {"nn": (((1,), (0,)), ((), ())), "nt": (((1,), (1,)), ((), ())), "tn": (((0,), (0,)), ((), ()))}[mode]

    def dot(a_ref, b_ref):
        return lax.dot_general(a_ref[...].astype(bf16), b_ref[...].astype(bf16), dims, preferred_element_type=f32)

    if nk == 1:
        def body(a_ref, b_ref, o_ref):
            o_ref[...] = dot(a_ref, b_ref).astype(o_ref.dtype)
        scratch = []
    else:
        def body(a_ref, b_ref, o_ref, acc_ref):
            k = pl.program_id(2)

            @pl.when(k == 0)
            def _():
                acc_ref[...] = jnp.zeros_like(acc_ref)

            acc_ref[...] += dot(a_ref, b_ref)

            @pl.when(k == nk - 1)
            def _():
                o_ref[...] = acc_ref[...].astype(o_ref.dtype)
        scratch = [pltpu.VMEM((tm, tn), f32)]

    a_spec = pl.BlockSpec((tk, tm), lambda i, j, k: (k, i)) if mode == "tn" else pl.BlockSpec((tm, tk), lambda i, j, k: (i, k))
    b_spec = pl.BlockSpec((tn, tk), lambda i, j, k: (j, k)) if mode == "nt" else pl.BlockSpec((tk, tn), lambda i, j, k: (k, j))
    return pl.pallas_call(
        body, name=name,
        grid=(M // tm, N // tn, nk),
        in_specs=[a_spec, b_spec],
        out_specs=pl.BlockSpec((tm, tn), lambda i, j, k: (i, j)),
        out_shape=jax.ShapeDtypeStruct((M, N), out_dtype),
        scratch_shapes=scratch,
        compiler_params=_cparams(("parallel", "parallel", "arbitrary")),
    )(a, b)


def _ada_fwd(cfg, c_all, ada_w):
    L, D, n = ada_w.shape
    tn = n // 2 if (n // 2) % LANES == 0 else n

    def body(c_ref, w_ref, o_ref, ca_ref):
        ca = _silu(c_ref[...])
        ca_ref[...] = ca
        o_ref[0] = jnp.dot(ca.astype(bf16), w_ref[0].astype(bf16), preferred_element_type=f32)

    return pl.pallas_call(
        body, name="ada_fwd", grid=(L, n // tn),
        in_specs=[pl.BlockSpec((N_DEV, D), lambda l, j: (0, 0)), pl.BlockSpec((1, D, tn), lambda l, j: (l, 0, j))],
        out_specs=(pl.BlockSpec((1, N_DEV, tn), lambda l, j: (l, 0, j)), pl.BlockSpec((N_DEV, D), lambda l, j: (0, 0))),
        out_shape=(jax.ShapeDtypeStruct((L, N_DEV, n), f32), jax.ShapeDtypeStruct((N_DEV, D), f32)),
        compiler_params=_cparams(("arbitrary", "arbitrary")),
    )(c_all, ada_w)


def _ada_bwd(cfg, c_act_t, dmod):
    L, _, n = dmod.shape
    D = c_act_t.shape[0]
    tn = n // 2 if (n // 2) % LANES == 0 else n

    def body(c_ref, d_ref, o_ref):
        o_ref[0] = jnp.dot(c_ref[...].astype(bf16), d_ref[0].astype(bf16), preferred_element_type=f32)

    return pl.pallas_call(
        body, name="ada_bwd", grid=(L, n // tn),
        in_specs=[pl.BlockSpec((D, N_DEV), lambda l, j: (0, 0)), pl.BlockSpec((1, N_DEV, tn), lambda l, j: (l, 0, j))],
        out_specs=pl.BlockSpec((1, D, tn), lambda l, j: (l, 0, j)),
        out_shape=jax.ShapeDtypeStruct((L, D, n), f32),
        compiler_params=_cparams(("parallel", "parallel")),
    )(c_act_t, dmod)


def _prenorm_fwd(cfg, x, mod, gain):
    S, D, TR = cfg.S, cfg.D, cfg.TR

    def body(x_ref, mod_ref, g_ref, h_ref):
        x = x_ref[...]
        r = lax.rsqrt(jnp.mean(x * x, axis=-1, keepdims=True) + NORM_EPS)
        shift, scale = mod_ref[:, 0:D], mod_ref[:, D:2 * D]
        h_ref[...] = ((x * r) * g_ref[...] * (1.0 + scale) + shift).astype(bf16)

    return pl.pallas_call(
        body, name="prenorm_fwd", grid=(S // TR,),
        in_specs=[_slab(TR, D, 0), _row(3 * D), _row(D)],
        out_specs=_slab(TR, D, 0), out_shape=jax.ShapeDtypeStruct((S, D), bf16),
        compiler_params=_cparams(("parallel",)),
    )(x, mod, gain)


def _prenorm_bwd(cfg, x, dh, dres, mod, gain):
    S, D, TR = cfg.S, cfg.D, cfg.TR

    def body(x_ref, dh_ref, dres_ref, mod_ref, g_ref, dx_ref, sum_ref):
        i = pl.program_id(0)
        x, dh, g = x_ref[...], dh_ref[...], g_ref[...]
        scale = mod_ref[:, D:2 * D]
        r = lax.rsqrt(jnp.mean(x * x, axis=-1, keepdims=True) + NORM_EPS)
        xn = x * r
        t = dh * xn
        dxn = dh * (g * (1.0 + scale))
        dx_ref[...] = r * (dxn - xn * jnp.mean(dxn * xn, axis=-1, keepdims=True)) + dres_ref[...]
        part = jnp.concatenate([jnp.sum(dh, axis=0, keepdims=True), jnp.sum(t * g, axis=0, keepdims=True),
                                jnp.sum(t * (1.0 + scale), axis=0, keepdims=True), jnp.zeros((SUBLANES - 3, D), f32)], axis=0)

        @pl.when(i == 0)
        def _():
            sum_ref[...] = part

        @pl.when(i > 0)
        def _():
            sum_ref[...] += part

    return pl.pallas_call(
        body, name="prenorm_bwd", grid=(S // TR,),
        in_specs=[_slab(TR, D, 0), _slab(TR, D, 0), _slab(TR, D, 0), _row(3 * D), _row(D)],
        out_specs=(_slab(TR, D, 0), pl.BlockSpec((SUBLANES, D), lambda i: (0, 0))),
        out_shape=(jax.ShapeDtypeStruct((S, D), f32), jax.ShapeDtypeStruct((SUBLANES, D), f32)),
        compiler_params=_cparams(("arbitrary",)),
    )(x, dh, dres, mod, gain)


def _postnorm_fwd(cfg, x, y, mod, gain):
    S, D, TR = cfg.S, cfg.D, cfg.TR

    def body(x_ref, y_ref, mod_ref, g_ref, o_ref):
        y = y_ref[...]
        r = lax.rsqrt(jnp.mean(y * y, axis=-1, keepdims=True) + NORM_EPS)
        rg = mod_ref[:, 2 * D:3 * D]
        o_ref[...] = x_ref[...] + (1.0 + rg) * ((y * r) * g_ref[...])

    return pl.pallas_call(
        body, name="postnorm_fwd", grid=(S // TR,),
        in_specs=[_slab(TR, D, 0), _slab(TR, D, 0), _row(3 * D), _row(D)],
        out_specs=_slab(TR, D, 0), out_shape=jax.ShapeDtypeStruct((S, D), f32),
        compiler_params=_cparams(("parallel",)),
    )(x, y, mod, gain)


def _postnorm_bwd(cfg, dout, y, mod, gain):
    S, D, TR = cfg.S, cfg.D, cfg.TR

    def body(do_ref, y_ref, mod_ref, g_ref, dy_ref, sum_ref):
        i = pl.program_id(0)
        do, y, g = do_ref[...], y_ref[...], g_ref[...]
        rg = mod_ref[:, 2 * D:3 * D]
        r = lax.rsqrt(jnp.mean(y * y, axis=-1, keepdims=True) + NORM_EPS)
        yn = y * r
        t = do * yn
        dyn = do * ((1.0 + rg) * g)
        dy_ref[...] = (r * (dyn - yn * jnp.mean(dyn * yn, axis=-1, keepdims=True))).astype(bf16)
        part = jnp.concatenate([jnp.sum(t * g, axis=0, keepdims=True), jnp.sum(t * (1.0 + rg), axis=0, keepdims=True),
                                jnp.zeros((SUBLANES - 2, D), f32)], axis=0)

        @pl.when(i == 0)
        def _():
            sum_ref[...] = part

        @pl.when(i > 0)
        def _():
            sum_ref[...] += part

    return pl.pallas_call(
        body, name="postnorm_bwd", grid=(S // TR,),
        in_specs=[_slab(TR, D, 0), _slab(TR, D, 0), _row(3 * D), _row(D)],
        out_specs=(_slab(TR, D, 0), pl.BlockSpec((SUBLANES, D), lambda i: (0, 0))),
        out_shape=(jax.ShapeDtypeStruct((S, D), bf16), jax.ShapeDtypeStruct((SUBLANES, D), f32)),
        compiler_params=_cparams(("arbitrary",)),
    )(dout, y, mod, gain)


def _loss_head(cfg, y, target):
    S, D, TR = cfg.S, cfg.D, cfg.TR

    def body(y_ref, t_ref, d_ref, l_ref):
        i = pl.program_id(0)
        err = y_ref[...] - t_ref[...]
        d_ref[...] = err / D
        part = jnp.zeros((SUBLANES, LANES), f32) + 0.5 * jnp.sum(jnp.mean(err * err, axis=-1, keepdims=True))

        @pl.when(i == 0)
        def _():
            l_ref[...] = part

        @pl.when(i > 0)
        def _():
            l_ref[...] += part

    return pl.pallas_call(
        body, name="loss_head", grid=(S // TR,),
        in_specs=[_slab(TR, D, 0), _slab(TR, D, 0)],
        out_specs=(_slab(TR, D, 0), pl.BlockSpec((SUBLANES, LANES), lambda i: (0, 0))),
        out_shape=(jax.ShapeDtypeStruct((S, D), f32), jax.ShapeDtypeStruct((SUBLANES, LANES), f32)),
        compiler_params=_cparams(("arbitrary",)),
    )(y, target)


def _merge_fwd(cfg, proj, u0, u1, u2):
    S, D, TR = cfg.S, cfg.D, cfg.TR

    def body(l0, l1, l2, u0_ref, u1_ref, u2_ref, o_ref):
        o_ref[...] = (_sigmoid(l0[...]) * u0_ref[...] + _sigmoid(l1[...]) * u1_ref[...]
                      + _sigmoid(l2[...]) * u2_ref[...]).astype(bf16)

    return pl.pallas_call(
        body, name="merge_fwd", grid=(S // TR,),
        in_specs=[_slab(TR, D, cfg.o_merge + b * D) for b in range(3)] + [_slab(TR, D, 0)] * 3,
        out_specs=_slab(TR, D, 0), out_shape=jax.ShapeDtypeStruct((S, D), bf16),
        compiler_params=_cparams(("parallel",)),
    )(proj, proj, proj, u0, u1, u2)


def _merge_bwd(cfg, proj, dmerged, u0, u1, u2):
    S, D, TR = cfg.S, cfg.D, cfg.TR

    def body(l0, l1, l2, dm_ref, u0_ref, u1_ref, u2_ref, du0, du1, du2, dl_ref):
        dm = dm_ref[...]
        for b, (l, u, du) in enumerate(((l0, u0_ref, du0), (l1, u1_ref, du1), (l2, u2_ref, du2))):
            g = _sigmoid(l[...])
            du[...] = (dm * g).astype(bf16)
            dl_ref[:, b * D:(b + 1) * D] = (dm * u[...] * (g * (1.0 - g))).astype(bf16)

    return pl.pallas_call(
        body, name="merge_bwd", grid=(S // TR,),
        in_specs=[_slab(TR, D, cfg.o_merge + b * D) for b in range(3)] + [_slab(TR, D, 0)] * 4,
        out_specs=(_slab(TR, D, 0),) * 3 + (_slab(TR, 3 * D, 0),),
        out_shape=(jax.ShapeDtypeStruct((S, D), bf16),) * 3 + (jax.ShapeDtypeStruct((S, 3 * D), bf16),),
        compiler_params=_cparams(("parallel",)),
    )(proj, proj, proj, dmerged, u0, u1, u2)


def _rope128(x, c, s):
    return x * c + pltpu.roll(x, 64, axis=1) * s


def _rope128_t(dy, c, s):
    return dy * c + pltpu.roll(dy * s, 64, axis=1)


def _swap32(x):
    w = x.shape[1]
    lane = lax.broadcasted_iota(jnp.int32, x.shape, 1)
    return jnp.where((lane % 64) < 32, pltpu.roll(x, w - 32, axis=1), pltpu.roll(x, 32, axis=1))


def _rope64(x, c, s):
    return x * c + _swap32(x) * s


def _rope64_t(dy, c, s):
    return dy * c + _swap32(dy * s)


def _rope_tables(cfg, positions):
    pos = positions.astype(f32)[0][:, None]

    def tab(dim):
        inv_freq = ROPE_BASE ** (-jnp.arange(0, dim, 2, dtype=f32) / dim)
        ang = pos * inv_freq
        cos, sin = jnp.cos(ang), jnp.sin(ang)
        return jnp.concatenate([cos, cos], axis=1), jnp.concatenate([-sin, sin], axis=1)

    return tab(HEAD), tab(ROPE)


def _ret_consts(cfg):
    h = np.arange(cfg.H, dtype=np.float64)
    log_gamma = np.log1p(-np.exp2(-5.0 - h)).astype(np.float32)
    idx = np.arange(CHUNK, dtype=np.float32)
    intra = np.exp(log_gamma[:, None, None] * np.abs(idx[:, None] - idx[None, :]))
    kdec = np.exp(log_gamma[:, None] * (CHUNK - 1 - idx)[None, :])
    qdec = np.exp(log_gamma[:, None] * (idx + 1.0)[None, :])
    cdec = np.exp(log_gamma * CHUNK)
    bc = lambda a: jnp.asarray(np.broadcast_to(a[..., None], a.shape + (HEAD,)).astype(np.float32))
    return jnp.asarray(intra.astype(np.float32)), bc(kdec), bc(qdec), bc(cdec[:, None])


def _ret_core(cfg, q_raw, k_raw, v_raw, cos, sin, intra, kdec, qdec, cdec, p_ref):
    S = cfg.S
    NC = S // CHUNK
    q = _rope128(q_raw, cos, sin) * (HEAD ** -0.5)
    k = _rope128(k_raw, cos, sin)
    q3 = q.reshape(NC, CHUNK, HEAD)
    k3 = k.reshape(NC, CHUNK, HEAD)
    qb, kb = q3.astype(bf16), k3.astype(bf16)
    vb = v_raw.reshape(NC, CHUNK, HEAD).astype(bf16)
    sdb = (jnp.einsum('nid,njd->nij', qb, kb, preferred_element_type=f32) * intra[None]).astype(bf16)
    o_intra = jnp.einsum('nij,nje->nie', sdb, vb, preferred_element_type=f32)
    kdb = (k3 * kdec[None]).astype(bf16)
    kv = jnp.einsum('njd,nje->nde', kdb, vb, preferred_element_type=f32)
    p_ref[0] = jnp.zeros((HEAD, HEAD), f32)
    for n in range(1, NC):
        p_ref[n] = p_ref[n - 1] * cdec + kv[n - 1]
    pb = p_ref[...].astype(bf16)
    qdb = (q3 * qdec[None]).astype(bf16)
    o_inter = jnp.einsum('nid,nde->nie', qdb, pb, preferred_element_type=f32)
    o = (o_intra + o_inter).reshape(S, HEAD)
    return o, (qb, kb, vb, sdb, kdb, qdb, pb)


def _ret_specs(cfg):
    S = cfg.S
    hs = lambda off: pl.BlockSpec((S, HEAD), lambda h, _c=off // HEAD: (0, _c + h))
    full = pl.BlockSpec((S, HEAD), lambda h: (0, 0))
    consts = [pl.BlockSpec((None, CHUNK, CHUNK), lambda h: (h, 0, 0)), pl.BlockSpec((None, CHUNK, HEAD), lambda h: (h, 0, 0)),
              pl.BlockSpec((None, CHUNK, HEAD), lambda h: (h, 0, 0)), pl.BlockSpec((None, 1, HEAD), lambda h: (h, 0, 0))]
    gn = pl.BlockSpec((1, HEAD), lambda h: (0, h))
    return hs, full, consts, gn


def _ret_fwd(cfg, proj, gn, cos, sin, consts):
    S, NC = cfg.S, cfg.S // CHUNK
    hs, full, cspecs, gspec = _ret_specs(cfg)

    def body(q_ref, k_ref, v_ref, g_ref, gn_ref, cos_ref, sin_ref, intra, kdec, qdec, cdec, y_ref, p_ref):
        o, _ = _ret_core(cfg, q_ref[...], k_ref[...], v_ref[...], cos_ref[...], sin_ref[...],
                         intra[...], kdec[...], qdec[...], cdec[...], p_ref)
        mean = jnp.mean(o, axis=-1, keepdims=True)
        var = jnp.mean(jnp.square(o - mean), axis=-1, keepdims=True)
        z = ((o - mean) * lax.rsqrt(var + NORM_EPS)) * gn_ref[...]
        y_ref[...] = (z * _silu(g_ref[...])).astype(bf16)

    return pl.pallas_call(
        body, name="ret_fwd", grid=(cfg.H,),
        in_specs=[hs(0), hs(cfg.o_rk), hs(cfg.o_rv), hs(cfg.o_rg), gspec, full, full] + cspecs,
        out_specs=hs(0), out_shape=jax.ShapeDtypeStruct((S, cfg.RW), bf16),
        scratch_shapes=[pltpu.VMEM((NC, HEAD, HEAD), f32)],
        compiler_params=_cparams(("arbitrary",)),
    )(proj, proj, proj, proj, gn, cos, sin, *consts)


def _ret_bwd(cfg, proj, dy, gn, cos, sin, consts):
    S, NC = cfg.S, cfg.S // CHUNK
    hs, full, cspecs, gspec = _ret_specs(cfg)

    def body(q_ref, k_ref, v_ref, g_ref, dy_ref, gn_ref, cos_ref, sin_ref, intra_ref, kdec_ref, qdec_ref, cdec_ref,
             dq_ref, dk_ref, dv_ref, dg_ref, dgn_ref, p_ref, g_scr):
        cos, sin = cos_ref[...], sin_ref[...]
        intra, kdec, qdec, cdec = intra_ref[...], kdec_ref[...], qdec_ref[...], cdec_ref[...]
        o, (qb, kb, vb, sdb, kdb, qdb, pb) = _ret_core(cfg, q_ref[...], k_ref[...], v_ref[...], cos, sin,
                                                     intra, kdec, qdec, cdec, p_ref)
        gate, dy, gnv = g_ref[...], dy_ref[...], gn_ref[...]
        mean = jnp.mean(o, axis=-1, keepdims=True)
        rstd = lax.rsqrt(jnp.mean(jnp.square(o - mean), axis=-1, keepdims=True) + NORM_EPS)
        on = (o - mean) * rstd
        dz = dy * _silu(gate)
        dg_ref[...] = (dy * (on * gnv) * _dsilu(gate)).astype(bf16)
        dgn_ref[...] = jnp.sum(dz * on, axis=0, keepdims=True)
        don = dz * gnv
        do = rstd * (don - jnp.mean(don, axis=-1, keepdims=True) - on * jnp.mean(don * on, axis=-1, keepdims=True))
        dob = do.reshape(NC, CHUNK, HEAD).astype(bf16)
        dsb = (jnp.einsum('nie,nje->nij', dob, vb, preferred_element_type=f32) * intra[None]).astype(bf16)
        dv = jnp.einsum('nij,nie->nje', sdb, dob, preferred_element_type=f32)
        dq = jnp.einsum('nij,njd->nid', dsb, kb, preferred_element_type=f32)
        dk = jnp.einsum('nij,nid->njd', dsb, qb, preferred_element_type=f32)
        dq = dq + jnp.einsum('nie,nde->nid', dob, pb, preferred_element_type=f32) * qdec[None]
        dp = jnp.einsum('nid,nie->nde', qdb, dob, preferred_element_type=f32)
        g_scr[NC - 1] = jnp.zeros((HEAD, HEAD), f32)
        for n in range(NC - 2, -1, -1):
            g_scr[n] = dp[n + 1] + g_scr[n + 1] * cdec
        gb = g_scr[...].astype(bf16)
        dk = dk + jnp.einsum('nje,nde->njd', vb, gb, preferred_element_type=f32) * kdec[None]
        dv = dv + jnp.einsum('njd,nde->nje', kdb, gb, preferred_element_type=f32)
        dq_ref[...] = _rope128_t(dq.reshape(S, HEAD) * (HEAD ** -0.5), cos, sin).astype(bf16)
        dk_ref[...] = _rope128_t(dk.reshape(S, HEAD), cos, sin).astype(bf16)
        dv_ref[...] = dv.reshape(S, HEAD).astype(bf16)

    return pl.pallas_call(
        body, name="ret_bwd", grid=(cfg.H,),
        in_specs=[hs(0), hs(cfg.o_rk), hs(cfg.o_rv), hs(cfg.o_rg), hs(0), gspec, full, full] + cspecs,
        out_specs=(hs(0),) * 4 + (gspec,),
        out_shape=(jax.ShapeDtypeStruct((S, cfg.RW), bf16),) * 4 + (jax.ShapeDtypeStruct((1, cfg.RW), f32),),
        scratch_shapes=[pltpu.VMEM((NC, HEAD, HEAD), f32), pltpu.VMEM((NC, HEAD, HEAD), f32)],
        compiler_params=_cparams(("arbitrary",)),
    )(proj, proj, proj, proj, dy, gn, cos, sin, *consts)


def _expm1(x):
    small = x * (1.0 + x * (0.5 + x * (1.0 / 6.0 + x * (1.0 / 24.0 + x * (1.0 / 120.0)))))
    return jnp.where(jnp.abs(x) < 0.1, small, jnp.exp(x) - 1.0)


def _softplus(z):
    return jnp.maximum(z, 0.0) + jnp.log1p(jnp.exp(-jnp.abs(z)))


def _lru_conv(cfg, x_ref, halo_ref, cw, scr, first):
    TR = cfg.TR
    scr[0:SUBLANES, :] = jnp.where(first, 0.0, halo_ref[...])
    scr[SUBLANES:SUBLANES + TR, :] = x_ref[...]
    xc = scr[pl.ds(SUBLANES - (CONV - 1), TR), :] * cw[0:1, :]
    for j in range(1, CONV):
        xc = xc + scr[pl.ds(SUBLANES - (CONV - 1) + j, TR), :] * cw[j:j + 1, :]
    return xc


def _lru_pre(cfg, xc, wa_ref, wx_ref, ba, bx):
    xb = xc.astype(bf16)
    pa = jnp.concatenate([jnp.dot(xb[:, n * HEAD:(n + 1) * HEAD], wa_ref[n].astype(bf16), preferred_element_type=f32)
                          for n in range(cfg.NB)], axis=1) + ba
    px = jnp.concatenate([jnp.dot(xb[:, n * HEAD:(n + 1) * HEAD], wx_ref[n].astype(bf16), preferred_element_type=f32)
                          for n in range(cfg.NB)], axis=1) + bx
    return pa, px


def _lru_ab(pa, px, xc, lam):
    r, i = _sigmoid(pa), _sigmoid(px)
    log_a = (-LRU_C * r) * _softplus(-lam)
    a = jnp.exp(log_a)
    b = jnp.sqrt(-_expm1(2.0 * log_a)) * (i * xc)
    return a, b


def _lru_halo_specs(cfg, off, W):
    TR, S = cfg.TR, cfg.S
    nb = TR // SUBLANES
    cb = off // W
    main = pl.BlockSpec((TR, W), lambda i: (i, cb))
    prev = pl.BlockSpec((SUBLANES, W), lambda i: (jnp.maximum(i * nb - 1, 0), cb))
    nxt = pl.BlockSpec((SUBLANES, W), lambda i: (jnp.minimum((i + 1) * nb, S // SUBLANES - 1), cb))
    return main, prev, nxt


def _lru_gates(cfg, proj, cw, cb, wa, ba, wx, bx, lam):
    S, W, TR, NB = cfg.S, cfg.LW, cfg.TR, cfg.NB
    assert cfg.o_lx % W == 0
    main, prev, _ = _lru_halo_specs(cfg, cfg.o_lx, W)
    wspec = pl.BlockSpec((NB, HEAD, HEAD), lambda i: (0, 0, 0))

    def body(x_ref, halo_ref, cw_ref, cb_ref, wa_ref, ba_ref, wx_ref, bx_ref, lam_ref, a_ref, b_ref, scr):
        xc = _lru_conv(cfg, x_ref, halo_ref, cw_ref[...], scr, pl.program_id(0) == 0) + cb_ref[...]
        pa, px = _lru_pre(cfg, xc, wa_ref, wx_ref, ba_ref[...], bx_ref[...])
        a, b = _lru_ab(pa, px, xc, lam_ref[...])
        a_ref[...] = a
        b_ref[...] = b

    return pl.pallas_call(
        body, name="lru_gates", grid=(S // TR,),
        in_specs=[main, prev, pl.BlockSpec((CONV, W), lambda i: (0, 0)), _row(W), wspec, _row(W), wspec, _row(W), _row(W)],
        out_specs=(_slab(TR, W, 0),) * 2, out_shape=(jax.ShapeDtypeStruct((S, W), f32),) * 2,
        scratch_shapes=[pltpu.VMEM((TR + SUBLANES, W), f32)],
        compiler_params=_cparams(("parallel",)),
    )(proj, proj, cw, cb, wa, ba, wx, bx, lam)


def _lru_lane_block(cfg):
    return 256 if cfg.LW % 256 == 0 else LANES


def _lru_scan_fwd(cfg, proj, a, b):
    S, W = cfg.S, cfg.LW
    LB = _lru_lane_block(cfg)
    assert cfg.o_lg % LB == 0
    col = lambda off: pl.BlockSpec((S, LB), lambda j, _c=off // LB: (0, _c + j))

    def body(a_ref, b_ref, g_ref, h_ref, y_ref):
        def blk(t, h):
            r0 = pl.multiple_of(t * SUBLANES, SUBLANES)
            at, bt = a_ref[pl.ds(r0, SUBLANES), :], b_ref[pl.ds(r0, SUBLANES), :]
            rows = []
            for j in range(SUBLANES):
                h = at[j:j + 1, :] * h + bt[j:j + 1, :]
                rows.append(h)
            h_ref[pl.ds(r0, SUBLANES), :] = jnp.concatenate(rows, axis=0)
            return h

        lax.fori_loop(0, S // SUBLANES, blk, jnp.zeros((1, LB), f32))
        y_ref[...] = (h_ref[...] * _silu(g_ref[...])).astype(bf16)

    return pl.pallas_call(
        body, name="lru_scan_fwd", grid=(W // LB,),
        in_specs=[col(0), col(0), col(cfg.o_lg)],
        out_specs=(col(0), col(0)),
        out_shape=(jax.ShapeDtypeStruct((S, W), f32), jax.ShapeDtypeStruct((S, W), bf16)),
        compiler_params=_cparams(("parallel",)),
    )(a, b, proj)


def _lru_scan_bwd(cfg, proj, a, h, dy):
    S, W = cfg.S, cfg.LW
    LB = _lru_lane_block(cfg)
    col = lambda off: pl.BlockSpec((S, LB), lambda j, _c=off // LB: (0, _c + j))

    def body(a_ref, h_ref, dy_ref, g_ref, da_ref, db_ref, dg_ref):
        gate, dy = g_ref[...], dy_ref[...]
        dg_ref[...] = (dy * h_ref[...] * _dsilu(gate)).astype(bf16)
        da_ref[...] = dy * _silu(gate)

        def blk(t, carry):
            dh_next, a_next = carry
            r0 = pl.multiple_of((S // SUBLANES - 1 - t) * SUBLANES, SUBLANES)
            at, ct = a_ref[pl.ds(r0, SUBLANES), :], da_ref[pl.ds(r0, SUBLANES), :]
            rows = [None] * SUBLANES
            for j in range(SUBLANES - 1, -1, -1):
                dh_next = ct[j:j + 1, :] + a_next * dh_next
                a_next = at[j:j + 1, :]
                rows[j] = dh_next
            db_ref[pl.ds(r0, SUBLANES), :] = jnp.concatenate(rows, axis=0)
            return dh_next, a_next

        z = jnp.zeros((1, LB), f32)
        lax.fori_loop(0, S // SUBLANES, blk, (z, z))
        row = lax.broadcasted_iota(jnp.int32, (S, LB), 0)
        hprev = jnp.where(row == 0, 0.0, pltpu.roll(h_ref[...], 1, axis=0))
        da_ref[...] = db_ref[...] * hprev

    return pl.pallas_call(
        body, name="lru_scan_bwd", grid=(W // LB,),
        in_specs=[col(0), col(0), col(0), col(cfg.o_lg)],
        out_specs=(col(0),) * 3,
        out_shape=(jax.ShapeDtypeStruct((S, W), f32),) * 2 + (jax.ShapeDtypeStruct((S, W), bf16),),
        compiler_params=_cparams(("parallel",)),
    )(a, h, dy, proj)


def _lru_gates_bwd(cfg, proj, da, db, cw, cb, wa, ba, wx, bx, lam):
    S, W, TR, NB = cfg.S, cfg.LW, cfg.TR, cfg.NB
    main, prev, _ = _lru_halo_specs(cfg, cfg.o_lx, W)
    wspec = pl.BlockSpec((NB, HEAD, HEAD), lambda i: (0, 0, 0))

    def body(x_ref, halo_ref, da_ref, db_ref, cw_ref, cb_ref, wa_ref, ba_ref, wx_ref, bx_ref, lam_ref,
             dxc_ref, dwa_ref, dwx_ref, sum_ref, scr):
        i = pl.program_id(0)
        lam = lam_ref[...]
        xc = _lru_conv(cfg, x_ref, halo_ref, cw_ref[...], scr, i == 0) + cb_ref[...]
        pa, px = _lru_pre(cfg, xc, wa_ref, wx_ref, ba_ref[...], bx_ref[...])
        _, vjp = jax.vjp(_lru_ab, pa, px, xc, lam)
        dpa, dpx, dxc, dlam = vjp((da_ref[...], db_ref[...]))
        xb, dpab, dpxb = xc.astype(bf16), dpa.astype(bf16), dpx.astype(bf16)
        nt = (((1,), (1,)), ((), ()))
        tn = (((0,), (0,)), ((), ()))
        back = []
        dwa, dwx = [], []
        for n in range(NB):
            sl = slice(n * HEAD, (n + 1) * HEAD)
            back.append(lax.dot_general(dpab[:, sl], wa_ref[n].astype(bf16), nt, preferred_element_type=f32)
                        + lax.dot_general(dpxb[:, sl], wx_ref[n].astype(bf16), nt, preferred_element_type=f32))
            dwa.append(lax.dot_general(xb[:, sl], dpab[:, sl], tn, preferred_element_type=f32))
            dwx.append(lax.dot_general(xb[:, sl], dpxb[:, sl], tn, preferred_element_type=f32))
        dxc_ref[...] = dxc + jnp.concatenate(back, axis=1)
        part = jnp.concatenate([jnp.sum(dpa, axis=0, keepdims=True), jnp.sum(dpx, axis=0, keepdims=True), dlam,
                                jnp.zeros((SUBLANES - 3, W), f32)], axis=0)

        @pl.when(i == 0)
        def _():
            sum_ref[...] = part
            for n in range(NB):
                dwa_ref[n] = dwa[n]
                dwx_ref[n] = dwx[n]

        @pl.when(i > 0)
        def _():
            sum_ref[...] += part
            for n in range(NB):
                dwa_ref[n] += dwa[n]
                dwx_ref[n] += dwx[n]

    return pl.pallas_call(
        body, name="lru_gates_bwd", grid=(S // TR,),
        in_specs=[main, prev, _slab(TR, W, 0), _slab(TR, W, 0), pl.BlockSpec((CONV, W), lambda i: (0, 0)), _row(W),
                  wspec, _row(W), wspec, _row(W), _row(W)],
        out_specs=(_slab(TR, W, 0), wspec, wspec, pl.BlockSpec((SUBLANES, W), lambda i: (0, 0))),
        out_shape=(jax.ShapeDtypeStruct((S, W), f32), jax.ShapeDtypeStruct((NB, HEAD, HEAD), f32),
                   jax.ShapeDtypeStruct((NB, HEAD, HEAD), f32), jax.ShapeDtypeStruct((SUBLANES, W), f32)),
        scratch_shapes=[pltpu.VMEM((TR + SUBLANES, W), f32)],
        compiler_params=_cparams(("arbitrary",)),
    )(proj, proj, da, db, cw, cb, wa, ba, wx, bx, lam)


def _lru_conv_bwd(cfg, proj, dxc, cw):
    S, W, TR = cfg.S, cfg.LW, cfg.TR
    main, prev, _ = _lru_halo_specs(cfg, cfg.o_lx, W)
    dmain, _, dnext = _lru_halo_specs(cfg, 0, W)

    def body(x_ref, xhalo_ref, d_ref, dhalo_ref, cw_ref, dx_ref, sum_ref, xs, ds):
        i = pl.program_id(0)
        cw = cw_ref[...]
        d = d_ref[...]
        xs[0:SUBLANES, :] = jnp.where(i == 0, 0.0, xhalo_ref[...])
        xs[SUBLANES:SUBLANES + TR, :] = x_ref[...]
        ds[0:TR, :] = d
        ds[TR:TR + SUBLANES, :] = jnp.where(i == pl.num_programs(0) - 1, 0.0, dhalo_ref[...])
        dx = ds[pl.ds(CONV - 1, TR), :] * cw[0:1, :]
        parts = [jnp.sum(d * xs[pl.ds(SUBLANES - (CONV - 1), TR), :], axis=0, keepdims=True)]
        for j in range(1, CONV):
            dx = dx + ds[pl.ds(CONV - 1 - j, TR), :] * cw[j:j + 1, :]
            parts.append(jnp.sum(d * xs[pl.ds(SUBLANES - (CONV - 1) + j, TR), :], axis=0, keepdims=True))
        dx_ref[...] = dx.astype(bf16)
        part = jnp.concatenate(parts + [jnp.sum(d, axis=0, keepdims=True), jnp.zeros((SUBLANES - CONV - 1, W), f32)], axis=0)

        @pl.when(i == 0)
        def _():
            sum_ref[...] = part

        @pl.when(i > 0)
        def _():
            sum_ref[...] += part

    return pl.pallas_call(
        body, name="lru_conv_bwd", grid=(S // TR,),
        in_specs=[main, prev, dmain, dnext, pl.BlockSpec((CONV, W), lambda i: (0, 0))],
        out_specs=(_slab(TR, W, 0), pl.BlockSpec((SUBLANES, W), lambda i: (0, 0))),
        out_shape=(jax.ShapeDtypeStruct((S, W), bf16), jax.ShapeDtypeStruct((SUBLANES, W), f32)),
        scratch_shapes=[pltpu.VMEM((TR + SUBLANES, W), f32), pltpu.VMEM((TR + SUBLANES, W), f32)],
        compiler_params=_cparams(("arbitrary",)),
    )(proj, proj, dxc, dxc, cw)


def _rms(x, g):
    r = lax.rsqrt(jnp.mean(x * x, axis=-1, keepdims=True) + NORM_EPS)
    return (x * r) * g, r


def _mla_norm(cfg, proj, qg, kg):
    S, TR = cfg.S, cfg.TR

    def body(q_ref, k_ref, qg_ref, kg_ref, qn_ref, kn_ref):
        qn_ref[...] = _rms(q_ref[...], qg_ref[...])[0].astype(bf16)
        kn_ref[...] = _rms(k_ref[...], kg_ref[...])[0].astype(bf16)

    return pl.pallas_call(
        body, name="mla_norm", grid=(S // TR,),
        in_specs=[_slab(TR, cfg.QL, cfg.o_mq), _slab(TR, cfg.KL, cfg.o_mkv), _row(cfg.QL), _row(cfg.KL)],
        out_specs=(_slab(TR, cfg.QL, 0), _slab(TR, cfg.KL, 0)),
        out_shape=(jax.ShapeDtypeStruct((S, cfg.QL), bf16), jax.ShapeDtypeStruct((S, cfg.KL), bf16)),
        compiler_params=_cparams(("parallel",)),
    )(proj, proj, qg, kg)


def _mla_norm_bwd(cfg, proj, dqn, dkn, qg, kg):
    S, TR = cfg.S, cfg.TR

    def one(x, g, dn):
        r = lax.rsqrt(jnp.mean(x * x, axis=-1, keepdims=True) + NORM_EPS)
        xn = x * r
        dxn = dn * g
        dx = r * (dxn - xn * jnp.mean(dxn * xn, axis=-1, keepdims=True))
        return dx, jnp.sum(dn * xn, axis=0, keepdims=True)

    def body(q_ref, k_ref, dq_ref, dk_ref, qg_ref, kg_ref, dmq_ref, dmk_ref, sq_ref, sk_ref):
        i = pl.program_id(0)
        dq, gq = one(q_ref[...], qg_ref[...], dq_ref[...])
        dk, gk = one(k_ref[...], kg_ref[...], dk_ref[...])
        dmq_ref[...] = dq.astype(bf16)
        dmk_ref[...] = dk.astype(bf16)
        pq = jnp.concatenate([gq, jnp.zeros((SUBLANES - 1, cfg.QL), f32)], axis=0)
        pk = jnp.concatenate([gk, jnp.zeros((SUBLANES - 1, cfg.KL), f32)], axis=0)

        @pl.when(i == 0)
        def _():
            sq_ref[...] = pq
            sk_ref[...] = pk

        @pl.when(i > 0)
        def _():
            sq_ref[...] += pq
            sk_ref[...] += pk

    return pl.pallas_call(
        body, name="mla_norm_bwd", grid=(S // TR,),
        in_specs=[_slab(TR, cfg.QL, cfg.o_mq), _slab(TR, cfg.KL, cfg.o_mkv), _slab(TR, cfg.QL, 0), _slab(TR, cfg.KL, 0),
                  _row(cfg.QL), _row(cfg.KL)],
        out_specs=(_slab(TR, cfg.QL, 0), _slab(TR, cfg.KL, 0), pl.BlockSpec((SUBLANES, cfg.QL), lambda i: (0, 0)),
                   pl.BlockSpec((SUBLANES, cfg.KL), lambda i: (0, 0))),
        out_shape=(jax.ShapeDtypeStruct((S, cfg.QL), bf16), jax.ShapeDtypeStruct((S, cfg.KL), bf16),
                   jax.ShapeDtypeStruct((SUBLANES, cfg.QL), f32), jax.ShapeDtypeStruct((SUBLANES, cfg.KL), f32)),
        compiler_params=_cparams(("arbitrary",)),
    )(proj, proj, dqn, dkn, qg, kg)


def _mla_pack(cfg, proj, q, kv, cq, sq, ck, sk):
    S, TR, MH = cfg.S, cfg.TR, cfg.MH
    NW, RWD = MH * HEAD, MH * ROPE

    def body(q_ref, kv_ref, kr_ref, cq_ref, sq_ref, ck_ref, sk_ref, qo_ref, ko_ref, vo_ref):
        q, kv = q_ref[...], kv_ref[...]
        qr = _rope64(q[:, NW:], cq_ref[...], sq_ref[...])
        kr = _rope64(kr_ref[...], ck_ref[...], sk_ref[...]).astype(bf16)
        lane = lax.broadcasted_iota(jnp.int32, (TR, HEAD), 1)
        for h in range(MH):
            grp = qr[:, (h // 2) * HEAD:(h // 2 + 1) * HEAD]
            if h % 2:
                grp = pltpu.roll(grp, 64, axis=1)
            qo_ref[h] = jnp.concatenate([q[:, h * HEAD:(h + 1) * HEAD], jnp.where(lane < ROPE, grp, 0.0)], axis=1).astype(bf16)
            ko_ref[h] = jnp.concatenate([kv[:, 2 * h * HEAD:(2 * h + 1) * HEAD].astype(bf16), kr], axis=1)
            vo_ref[h] = kv[:, (2 * h + 1) * HEAD:(2 * h + 2) * HEAD].astype(bf16)

    hspec = lambda w: pl.BlockSpec((MH, TR, w), lambda i: (0, i, 0))
    return pl.pallas_call(
        body, name="mla_pack", grid=(S // TR,),
        in_specs=[_slab(TR, cfg.QW, 0), _slab(TR, cfg.KVW, 0), _slab(TR, HEAD, cfg.o_mkr),
                  _slab(TR, RWD, 0), _slab(TR, RWD, 0), _slab(TR, HEAD, 0), _slab(TR, HEAD, 0)],
        out_specs=(hspec(2 * HEAD), hspec(2 * HEAD), hspec(HEAD)),
        out_shape=(jax.ShapeDtypeStruct((MH, S, 2 * HEAD), bf16), jax.ShapeDtypeStruct((MH, S, 2 * HEAD), bf16),
                   jax.ShapeDtypeStruct((MH, S, HEAD), bf16)),
        compiler_params=_cparams(("parallel",)),
    )(q, kv, proj, cq, sq, ck, sk)


def _mla_unpack_bwd(cfg, dq3, dk3, dv3, cq, sq, ck, sk):
    S, TR, MH = cfg.S, cfg.TR, cfg.MH
    RWD = MH * ROPE

    def body(dq_ref, dk_ref, dv_ref, cq_ref, sq_ref, ck_ref, sk_ref, q_ref, kv_ref, kr_ref):
        lane = lax.broadcasted_iota(jnp.int32, (TR, HEAD), 1)
        nope, ropes, kvs = [], [], []
        dkr = jnp.zeros((TR, HEAD), f32)
        for h in range(MH):
            dq = dq_ref[h]
            nope.append(dq[:, :HEAD])
            part = jnp.where(lane < ROPE, dq[:, HEAD:], 0.0)
            if h % 2:
                ropes[-1] = ropes[-1] + pltpu.roll(part, 64, axis=1)
            else:
                ropes.append(part)
            dk = dk_ref[h]
            kvs += [dk[:, :HEAD], dv_ref[h]]
            dkr = dkr + dk[:, HEAD:]
        dqr = _rope64_t(jnp.concatenate(ropes, axis=1), cq_ref[...], sq_ref[...])
        q_ref[...] = jnp.concatenate(nope + [dqr], axis=1).astype(bf16)
        kv_ref[...] = jnp.concatenate(kvs, axis=1).astype(bf16)
        dkr = jnp.where(lane < ROPE, dkr, 0.0)
        kr_ref[...] = _rope64_t(dkr, ck_ref[...], sk_ref[...]).astype(bf16)

    hspec = lambda w: pl.BlockSpec((MH, TR, w), lambda i: (0, i, 0))
    return pl.pallas_call(
        body, name="mla_unpack_bwd", grid=(S // TR,),
        in_specs=[hspec(2 * HEAD), hspec(2 * HEAD), hspec(HEAD), _slab(TR, RWD, 0), _slab(TR, RWD, 0),
                  _slab(TR, HEAD, 0), _slab(TR, HEAD, 0)],
        out_specs=(_slab(TR, cfg.QW, 0), _slab(TR, cfg.KVW, 0), _slab(TR, HEAD, 0)),
        out_shape=(jax.ShapeDtypeStruct((S, cfg.QW), bf16), jax.ShapeDtypeStruct((S, cfg.KVW), bf16),
                   jax.ShapeDtypeStruct((S, HEAD), bf16)),
        compiler_params=_cparams(("parallel",)),
    )(dq3, dk3, dv3, cq, sq, ck, sk)


def _mla_probs(cfg, q, k, i):
    TQ, n = cfg.TQ, k.shape[0]
    nt = (((1,), (1,)), ((), ()))
    s = lax.dot_general(q, k, nt, preferred_element_type=f32) * ((HEAD + ROPE) ** -0.5)
    qc = (i * TQ + lax.broadcasted_iota(jnp.int32, (TQ, n), 0)) // CHUNK
    kc = lax.broadcasted_iota(jnp.int32, (TQ, n), 1) // CHUNK
    s = jnp.where(kc <= qc, s, -1e30)
    m = jnp.max(s, axis=-1, keepdims=True)
    e = jnp.exp(s - m)
    return e / jnp.sum(e, axis=-1, keepdims=True)


def _mla_attn_specs(cfg):
    S, TQ = cfg.S, cfg.TQ
    qs = lambda w: pl.BlockSpec((None, TQ, w), lambda h, i: (h, i, 0))
    ks = lambda w: pl.BlockSpec((None, S, w), lambda h, i: (h, 0, 0))
    hs = lambda off: pl.BlockSpec((TQ, HEAD), lambda h, i, _c=off // HEAD: (i, _c + h))
    return qs, ks, hs


def _mla_attn_fwd(cfg, proj, q3, k3, v3):
    S, TQ, MH = cfg.S, cfg.TQ, cfg.MH
    qs, ks, hs = _mla_attn_specs(cfg)

    def body(q_ref, k_ref, v_ref, g_ref, o_ref, y_ref):
        for i in range(S // TQ):
            @pl.when(pl.program_id(1) == i)
            def _(i=i):
                n = (i + 1) * TQ
                p = _mla_probs(cfg, q_ref[...], k_ref[0:n, :], i)
                o = jnp.dot(p.astype(bf16), v_ref[0:n, :], preferred_element_type=f32)
                o_ref[...] = o
                y_ref[...] = (o * _silu(g_ref[...])).astype(bf16)

    return pl.pallas_call(
        body, name="mla_attn_fwd", grid=(MH, S // TQ),
        in_specs=[qs(2 * HEAD), ks(2 * HEAD), ks(HEAD), hs(cfg.o_mg)],
        out_specs=(hs(0), hs(0)),
        out_shape=(jax.ShapeDtypeStruct((S, cfg.MW), f32), jax.ShapeDtypeStruct((S, cfg.MW), bf16)),
        compiler_params=_cparams(("parallel", "parallel")),
    )(q3, k3, v3, proj)


def _mla_attn_bwd(cfg, proj, q3, k3, v3, o, dy):
    S, TQ, MH = cfg.S, cfg.TQ, cfg.MH
    qs, ks, hs = _mla_attn_specs(cfg)

    def body(q_ref, k_ref, v_ref, g_ref, o_ref, dy_ref, dq_ref, dk_ref, dv_ref, dg_ref):
        q = q_ref[...]
        gate, dy, o = g_ref[...], dy_ref[...], o_ref[...]
        dg_ref[...] = (dy * o * _dsilu(gate)).astype(bf16)
        dob = (dy * _silu(gate)).astype(bf16)
        nt = (((1,), (1,)), ((), ()))
        tn = (((0,), (0,)), ((), ()))

        @pl.when(pl.program_id(1) == 0)
        def _():
            dk_ref[...] = jnp.zeros_like(dk_ref)
            dv_ref[...] = jnp.zeros_like(dv_ref)

        for i in range(S // TQ):
            @pl.when(pl.program_id(1) == i)
            def _(i=i):
                n = (i + 1) * TQ
                k, v = k_ref[0:n, :], v_ref[0:n, :]
                p = _mla_probs(cfg, q, k, i)
                dv_ref[0:n, :] += lax.dot_general(p.astype(bf16), dob, tn, preferred_element_type=f32)
                dp = lax.dot_general(dob, v, nt, preferred_element_type=f32)
                ds = (p * (dp - jnp.sum(dp * p, axis=-1, keepdims=True)) * ((HEAD + ROPE) ** -0.5)).astype(bf16)
                dq_ref[...] = jnp.dot(ds, k, preferred_element_type=f32)
                dk_ref[0:n, :] += lax.dot_general(ds, q, tn, preferred_element_type=f32)

    return pl.pallas_call(
        body, name="mla_attn_bwd", grid=(MH, S // TQ),
        in_specs=[qs(2 * HEAD), ks(2 * HEAD), ks(HEAD), hs(cfg.o_mg), hs(0), hs(0)],
        out_specs=(qs(2 * HEAD), ks(2 * HEAD), ks(HEAD), hs(0)),
        out_shape=(jax.ShapeDtypeStruct((MH, S, 2 * HEAD), f32), jax.ShapeDtypeStruct((MH, S, 2 * HEAD), f32),
                   jax.ShapeDtypeStruct((MH, S, HEAD), f32), jax.ShapeDtypeStruct((S, cfg.MW), bf16)),
        compiler_params=_cparams(("parallel", "arbitrary")),
    )(q3, k3, v3, proj, o, dy)


def _pick_rows(R, bytes_per_row):
    if R * bytes_per_row <= MM_BUDGET:
        return R
    best = None
    for t in range(16, R, 16):
        if R % t == 0 and t * bytes_per_row <= MM_BUDGET:
            best = t
    assert best is not None, (R, bytes_per_row)
    return best


def _adamw(w, g, m, v, name="adamw"):
    R, C = w.shape
    tr = _pick_rows(R, C * 4 * 7 * 2)
    c1 =1.0 - ADAM_B1 ** ADAM_STEP
    c2 = 1.0 - ADAM_B2 ** ADAM_STEP

    def body(w_ref, g_ref, m_ref, v_ref, d_ref, mo_ref, vo_ref):
        g = g_ref[...]
        m = ADAM_B1 * m_ref[...] + (1.0 - ADAM_B1) * g
        v = ADAM_B2 * v_ref[...] + (1.0 - ADAM_B2) * jnp.square(g)
        d_ref[...] = -ADAM_LR * ((m / c1) / (jnp.sqrt(v / c2) + ADAM_EPS) + ADAM_WD * w_ref[...])
        mo_ref[...] = m
        vo_ref[...] = v

    spec = pl.BlockSpec((tr, C), lambda i: (i, 0))
    return pl.pallas_call(
        body, name=name, grid=(R // tr,), in_specs=[spec] * 4, out_specs=(spec,) * 3,
        out_shape=(jax.ShapeDtypeStruct((R, C), f32),) * 3,
        compiler_params=_cparams(("parallel",)),
    )(w, g, m, v)


def _adamw_big(w, m, v, l0, mines, others, core, after, name, half_cols=False, prev=None):
    L, R, C = w.shape
    nl = len(mines)
    n_prev = 1 if prev is None else 5
    prev = (after,) + tuple(prev or ())
    hr, hc = (R, C // 2) if half_cols else (R // 2, C)
    tr = _pick_rows(hr, hc * 4 * (7 + 2 * nl) * 2)
    nt = hr // tr
    c1 = 1.0 - ADAM_B1 ** ADAM_STEP
    c2 = 1.0 - ADAM_B2 ** ADAM_STEP

    def body(core_ref, w_ref, m_ref, v_ref, *rest):
        g_refs, (go_ref, d_ref, mo_ref, vo_ref) = rest[:2 * nl], rest[2 * nl + n_prev:]
        l, h = pl.program_id(0), pl.program_id(1)
        own = h == core_ref[0]
        g = jnp.where(own, g_refs[0][...], g_refs[nl][...])
        for k in range(1, nl):
            g = jnp.where(l == k, jnp.where(own, g_refs[k][...], g_refs[nl + k][...]), g)
        m = ADAM_B1 * m_ref[...] + (1.0 - ADAM_B1) * g
        v = ADAM_B2 * v_ref[...] + (1.0 - ADAM_B2) * jnp.square(g)
        go_ref[...] = g
        d_ref[...] = -ADAM_LR * ((m / c1) / (jnp.sqrt(v / c2) + ADAM_EPS) + ADAM_WD * w_ref[...])
        mo_ref[...] = m
        vo_ref[...] = v

    if half_cols:
        lay = pl.BlockSpec((None, tr, hc), lambda l, h, i, core_ref: (l0 + l, i, h))
    else:
        lay = pl.BlockSpec((None, tr, hc), lambda l, h, i, core_ref: (l0 + l, h * nt + i, 0))
    gspec = lambda k: pl.BlockSpec((tr, hc), lambda l, h, i, core_ref: (jnp.where(l == k, i, 0), 0))
    return pl.pallas_call(
        body, name=name,
        grid_spec=pltpu.PrefetchScalarGridSpec(
            num_scalar_prefetch=1, grid=(nl, 2, nt),
            in_specs=[lay, lay, lay] + [gspec(k) for k in range(nl)] * 2 + _hbm_specs(n_prev), out_specs=(lay,) * 4),
        out_shape=(jax.ShapeDtypeStruct((L, R, C), f32),) * 4,
        input_output_aliases={5 + 2 * nl + k: k for k in range(n_prev - 1)},
        compiler_params=_cparams(("arbitrary", "arbitrary", "arbitrary")),
    )(core, w, m, v, *mines, *others, *prev)


def _sum_blocks(x, out_dtype, name):
    n, R, C = x.shape
    tr = _pick_rows(R, C * 4 * (n + 1) * 2)

    def body(x_ref, o_ref):
        acc = x_ref[0].astype(f32)
        for k in range(1, n):
            acc = acc + x_ref[k].astype(f32)
        o_ref[...] = acc.astype(o_ref.dtype)

    return pl.pallas_call(
        body, name=name, grid=(R // tr,),
        in_specs=[pl.BlockSpec((n, tr, C), lambda i: (0, i, 0))], out_specs=pl.BlockSpec((tr, C), lambda i: (i, 0)),
        out_shape=jax.ShapeDtypeStruct((R, C), out_dtype),
        compiler_params=_cparams(("parallel",)),
    )(x)


def _hbm_specs(n):
    return [pl.BlockSpec(memory_space=pl.ANY)] * n


def _row_map(cfg):
    nc, k0 = cfg.IN_WIDTH // 4, cfg.o_mg

    def padded(o):
        return o if o < k0 else (cfg.o_mkr + o - k0 if o < k0 + ROPE else o - ROPE)

    cuts = {0, nc}
    for q in range(4):
        cuts |= {b - q * nc for b in (k0, k0 + ROPE) if q * nc < b < (q + 1) * nc}
    cuts = sorted(cuts)
    return [((l0, l1 - l0), tuple(padded(q * nc + l0) for q in range(4))) for l0, l1 in zip(cuts[:-1], cuts[1:])]


def _chip_start(q, starts):
    st = starts[0]
    for i in range(1, 4):
        st = jnp.where(q == i, starts[i], st)
    return pl.multiple_of(st, 16)


def _allgather8(x_shard, name, after=None):
    m, n = x_shard.shape

    def body(x_ref, _, out_ref, send_sems, recv_sems, local_sem):
        x, y, c = lax.axis_index("x"), lax.axis_index("y"), lax.axis_index("c")
        me, sibling = (x, y, c), (x, y, 1 - c)
        chips = [(1 - x, y), (x, 1 - y), (1 - x, 1 - y)]

        def rows(px, py, pc):
            return out_ref.at[pl.ds((4 * px + 2 * py + pc) * m, m), :]

        def copy(k, block, to, src=None):
            return pltpu.make_async_remote_copy(
                src_ref=rows(*block) if src is None else src, dst_ref=rows(*block),
                send_sem=send_sems.at[k], recv_sem=recv_sems.at[k], device_id=to, device_id_type=MESH)

        mine = pltpu.make_async_copy(x_ref, rows(*me), local_sem)
        mine.start()
        first = [copy(0, me, sibling, src=x_ref)] + [copy(1 + j, me, (*chip, c), src=x_ref) for j, chip in enumerate(chips)]
        for cp in first:
            cp.start()
        passed = [copy(4 + j, (*chip, c), sibling) for j, chip in enumerate(chips)]
        for j, chip in enumerate(chips):
            copy(1 + j, (*chip, c), me).wait_recv()
            passed[j].start()
        copy(0, sibling, me).wait_recv()
        for j, chip in enumerate(chips):
            copy(4 + j, (*chip, 1 - c), me).wait_recv()
        for cp in first + passed:
            cp.wait_send()
        mine.wait()

    return pl.pallas_call(
        body, name=name, out_shape=jax.ShapeDtypeStruct((N_DEV * m, n), x_shard.dtype),
        in_specs=_hbm_specs(2), out_specs=pl.BlockSpec(memory_space=pl.ANY),
        scratch_shapes=[pltpu.SemaphoreType.DMA((7,)), pltpu.SemaphoreType.DMA((7,)), pltpu.SemaphoreType.DMA],
    )(x_shard, after if after is not None else jnp.zeros((SUBLANES, LANES), f32))


_SEM = pl.BlockSpec(memory_space=pltpu.SEMAPHORE)
_HBM = pl.BlockSpec(memory_space=pltpu.HBM)
_EFFECT = pltpu.SideEffectType.DATAFLOW_SIDE_EFFECTING


def _split_start(srcs, lands, after, plan, n, name):
    bufs = list(srcs) + list(lands)
    nb, ns = len(bufs), len(srcs)

    def body(*refs):
        send_sems, recv_sems = refs[nb + 1], refs[nb + 2]
        for k, (src, dst, _, dev) in enumerate(plan(refs[:ns], refs[ns:nb])):
            pltpu.make_async_remote_copy(src_ref=src, dst_ref=dst, send_sem=send_sems.at[k], recv_sem=recv_sems.at[k],
                                         device_id=dev, device_id_type=MESH).start()
        refs[-1][...] = jnp.zeros_like(refs[-1])

    out = pl.pallas_call(
        body, name=name,
        out_shape=(pltpu.SemaphoreType.DMA((n,)), pltpu.SemaphoreType.DMA((n,)), *[pltpu.HBM(b.shape, b.dtype) for b in bufs],
                   jax.ShapeDtypeStruct((SUBLANES, LANES), f32)),
        in_specs=[_HBM] * nb + [pl.BlockSpec(memory_space=pl.ANY)],
        out_specs=(_SEM, _SEM, *[_HBM] * nb, pl.BlockSpec(memory_space=pltpu.VMEM)),
        input_output_aliases={i: 2 + i for i in range(nb)},
        compiler_params=pltpu.CompilerParams(has_side_effects=_EFFECT),
    )(*[pltpu.with_memory_space_constraint(b, pltpu.HBM) for b in bufs], after)
    return out[0], out[1], list(out[2:2 + ns]), list(out[2 + ns:2 + nb]), out[-1]


def _split_wait(srcs, lands, send_sems, recv_sems, after, plan, name):
    bufs = list(srcs) + list(lands)
    nb, ns = len(bufs), len(srcs)

    def body(*refs):
        send, recv = refs[nb], refs[nb + 1]
        for k, (src, _, dst, dev) in enumerate(plan(refs[:ns], refs[ns:nb])):
            cp = pltpu.make_async_remote_copy(src_ref=src, dst_ref=dst, send_sem=send.at[k], recv_sem=recv.at[k],
                                              device_id=dev, device_id_type=MESH)
            cp.wait_send()
            cp.wait_recv()

    out = pl.pallas_call(
        body, name=name, out_shape=tuple(pltpu.HBM(b.shape, b.dtype) for b in bufs),
        in_specs=[_HBM] * nb + [_SEM, _SEM, pl.BlockSpec(memory_space=pl.ANY)], out_specs=tuple([_HBM] * nb),
        input_output_aliases={i: i for i in range(nb)},
        compiler_params=pltpu.CompilerParams(has_side_effects=_EFFECT),
    )(*bufs, send_sems, recv_sems, after)
    return list(out[:ns]), list(out[ns:])


def _weight_windows(cfg, shards, out_refs, px, py, pc):
    rmap = _row_map(cfg)
    m, n = shards[0].shape
    cols = pl.ds(pl.multiple_of(pc * n, n), n)
    wins = [[(pl.ds(l0, cnt), out_refs[0].at[pl.ds(_chip_start(2 * px + py, starts), cnt), cols]) for (l0, cnt), starts in rmap]]
    for a in range(1, len(shards)):
        m = shards[a].shape[0]
        wins.append([(pl.ds(0, m), out_refs[a].at[pl.ds((4 * px + 2 * py + pc) * m, m), :])])
    return wins


def _gather_shapes(cfg, shards):
    return [jax.ShapeDtypeStruct((cfg.NP, cfg.D), shards[0].dtype)] + \
           [jax.ShapeDtypeStruct((N_DEV * s.shape[0], s.shape[1]), s.dtype) for s in shards[1:]]


def _gather_plan(cfg, shards):
    def plan(x_refs, land_refs):
        x, y, c = lax.axis_index("x"), lax.axis_index("y"), lax.axis_index("c")
        mine = _weight_windows(cfg, shards, land_refs, x, y, c)
        out = []
        for peer in [(x, y, 1 - c), (1 - x, y, c), (x, 1 - y, c), (1 - x, 1 - y, c)]:
            theirs = _weight_windows(cfg, shards, land_refs, *peer)
            for a in range(len(shards)):
                for (rows, win), (_, win_in) in zip(mine[a], theirs[a]):
                    out.append((x_refs[a].at[rows, :], win, win_in, peer))
        return out
    return plan


def _gather_finish(cfg, shards, lands, name):
    na = len(shards)
    rmap = _row_map(cfg)
    npc = len(rmap)
    nz = cfg.NP - cfg.o_mkr - ROPE

    def body(*refs):
        x_refs, out_refs = refs[:na], refs[2 * na:3 * na]
        stage, zbuf = refs[3 * na:4 * na], refs[4 * na]
        send_sems, recv_sems, local_sems = refs[4 * na + 1:]
        x, y, c = lax.axis_index("x"), lax.axis_index("y"), lax.axis_index("c")
        sibling = (x, y, 1 - c)
        chips = [(1 - x, y), (x, 1 - y), (1 - x, 1 - y)]
        load = [pltpu.make_async_copy(x_refs[a], stage[a], local_sems.at[a, npc]) for a in range(na)]
        for cp in load:
            cp.start()
        passed = []
        for j, chip in enumerate(chips):
            wins = _weight_windows(cfg, shards, out_refs, *chip, c)
            for a in range(na):
                passed += [pltpu.make_async_remote_copy(src_ref=win, dst_ref=win, send_sem=send_sems.at[a, j, p],
                                                        recv_sem=recv_sems.at[a, j, p], device_id=sibling, device_id_type=MESH)
                           for p, (_, win) in enumerate(wins[a])]
        for cp in passed:
            cp.start()
        zbuf[...] = jnp.zeros_like(zbuf)
        for cp in load:
            cp.wait()
        own = _weight_windows(cfg, shards, out_refs, x, y, c)
        store = [pltpu.make_async_copy(stage[a].at[rows, :], win, local_sems.at[a, p])
                 for a in range(na) for p, (rows, win) in enumerate(own[a])]
        store.append(pltpu.make_async_copy(zbuf, out_refs[0].at[pl.ds(cfg.NP - nz, nz), :], local_sems.at[0, npc + 1]))
        for cp in store:
            cp.start()
        for j, chip in enumerate(chips):
            wins = _weight_windows(cfg, shards, out_refs, *chip, 1 - c)
            for a in range(na):
                for p, (_, win) in enumerate(wins[a]):
                    pltpu.make_async_remote_copy(src_ref=win, dst_ref=win, send_sem=send_sems.at[a, j, p],
                                                 recv_sem=recv_sems.at[a, j, p], device_id=sibling,
                                                 device_id_type=MESH).wait_recv()
        for cp in passed:
            cp.wait_send()
        for cp in store:
            cp.wait()

    return pl.pallas_call(
        body, name=name, out_shape=_gather_shapes(cfg, shards),
        in_specs=_hbm_specs(2 * na), out_specs=_hbm_specs(na),
        input_output_aliases={na + a: a for a in range(na)},
        scratch_shapes=[pltpu.VMEM(s.shape, s.dtype) for s in shards] + [pltpu.VMEM((nz, cfg.D), shards[0].dtype)]
        + [pltpu.SemaphoreType.DMA((na, 3, npc)), pltpu.SemaphoreType.DMA((na, 3, npc)), pltpu.SemaphoreType.DMA((na, npc + 2))],
        compiler_params=pltpu.CompilerParams(vmem_limit_bytes=VMEM_LIMIT),
    )(*shards, *lands)


def _gather_weights_start(cfg, shards, after):
    lands = [lax.empty(s.shape, s.dtype) for s in _gather_shapes(cfg, shards)]
    n = 4 * (len(_row_map(cfg)) + len(shards) - 1)
    return _split_start(shards, lands, after, _gather_plan(cfg, shards), n, "gather_w_start")


def _gather_weights_end(cfg, shards, started, after):
    send_sems, recv_sems, srcs, lands, _ = started
    srcs, lands = _split_wait(srcs, lands, send_sems, recv_sems, after, _gather_plan(cfg, shards), "gather_w_wait")
    return _gather_finish(cfg, srcs, lands, "gather_w_finish")


def _send_sibling(arrays, name):
    na = len(arrays)

    def body(*refs):
        x_refs, out_refs = refs[:na], refs[na:2 * na]
        send_sems, recv_sems = refs[2 * na:]
        sibling = (lax.axis_index("x"), lax.axis_index("y"), 1 - lax.axis_index("c"))
        cps = [pltpu.make_async_remote_copy(src_ref=x_refs[a], dst_ref=out_refs[a], send_sem=send_sems.at[a],
                                            recv_sem=recv_sems.at[a], device_id=sibling, device_id_type=MESH)
               for a in range(na)]
        for cp in cps:
            cp.start()
        for cp in cps:
            cp.wait()

    return pl.pallas_call(
        body, name=name, out_shape=[jax.ShapeDtypeStruct(x.shape, x.dtype) for x in arrays],
        in_specs=_hbm_specs(na), out_specs=_hbm_specs(na),
        scratch_shapes=[pltpu.SemaphoreType.DMA((na,)), pltpu.SemaphoreType.DMA((na,))],
    )(*arrays)


def _slot_pairs(cfg, p_refs, slot_refs, a, to_chip, slot):
    if a == 0:
        return [(p_refs[0].at[pl.ds(_chip_start(to_chip, starts), cnt), :], slot_refs[0].at[slot, pl.ds(l0, cnt), :])
                for (l0, cnt), starts in _row_map(cfg)]
    return [(p_refs[a].at[to_chip], slot_refs[a].at[slot])]


def _scatter_plan(cfg, na):
    def plan(p_refs, slot_refs):
        x, y, c = lax.axis_index("x"), lax.axis_index("y"), lax.axis_index("c")
        mychip = 2 * x + y
        out = []
        for cx, cy in [(1 - x, y), (x, 1 - y), (1 - x, 1 - y)]:
            q = 2 * cx + cy
            for a in range(na):
                for (src, dst), (_, dst_in) in zip(_slot_pairs(cfg, p_refs, slot_refs, a, q, mychip),
                                                   _slot_pairs(cfg, p_refs, slot_refs, a, mychip, q)):
                    out.append((src, dst, dst_in, (cx, cy, c)))
        return out
    return plan


def _slot_shapes(cfg, parts):
    return [jax.ShapeDtypeStruct((4, cfg.IN_WIDTH // 4, parts[0].shape[1]), parts[0].dtype)] + \
           [jax.ShapeDtypeStruct(p.shape, p.dtype) for p in parts[1:]]


def _place_own(cfg, parts, slots, name):
    na = len(parts)
    npc = len(_row_map(cfg))
    shapes = _slot_shapes(cfg, parts)

    def body(*refs):
        p_refs, out_refs = refs[:na], refs[2 * na:3 * na]
        stage, sems = refs[3 * na:4 * na], refs[4 * na]
        mychip = 2 * lax.axis_index("x") + lax.axis_index("y")
        moves = []
        for a in range(na):
            for p, (src, dst) in enumerate(_slot_pairs(cfg, p_refs, out_refs, a, mychip, mychip)):
                buf = stage[a].at[pl.ds(*_row_map(cfg)[p][0]), :] if a == 0 else stage[a]
                moves.append((pltpu.make_async_copy(src, buf, sems.at[a, p]), pltpu.make_async_copy(buf, dst, sems.at[a, npc + p])))
        for load, _ in moves:
            load.start()
        for load, store in moves:
            load.wait()
            store.start()
        for _, store in moves:
            store.wait()

    return pl.pallas_call(
        body, name=name, out_shape=shapes, in_specs=_hbm_specs(2 * na), out_specs=_hbm_specs(na),
        input_output_aliases={na + a: a for a in range(na)},
        scratch_shapes=[pltpu.VMEM(s.shape[1:], s.dtype) for s in shapes] + [pltpu.SemaphoreType.DMA((na, 2 * npc))],
        compiler_params=pltpu.CompilerParams(vmem_limit_bytes=VMEM_LIMIT),
    )(*parts, *slots)


def _add_half(g, got, core, name):
    if g.ndim == 2:
        R, hd = got.shape
        tr = _pick_rows(R, hd * 4 * 3 * 2)
        grid = (R // tr,)
        g_spec = pl.BlockSpec((tr, hd), lambda i, core_ref: (i, core_ref[0]))
        o_spec = pl.BlockSpec((tr, hd), lambda i, core_ref: (i, 0))
    else:
        _, hr, nc = got.shape
        tr = _pick_rows(hr, nc * 4 * 3 * 2)
        grid = (4, hr // tr)
        g_spec = pl.BlockSpec((None, None, tr, nc), lambda q, i, core_ref: (q, core_ref[0], i, 0))
        o_spec = pl.BlockSpec((None, tr, nc), lambda q, i, core_ref: (q, i, 0))

    def body(core_ref, g_ref, got_ref, o_ref):
        o_ref[...] = (g_ref[...].astype(f32) + got_ref[...].astype(f32)).astype(o_ref.dtype)

    return pl.pallas_call(
        body, name=name,
        grid_spec=pltpu.PrefetchScalarGridSpec(num_scalar_prefetch=1, grid=grid, in_specs=[g_spec, o_spec], out_specs=o_spec),
        out_shape=jax.ShapeDtypeStruct(got.shape, bf16),
        compiler_params=_cparams(("arbitrary",) * len(grid)),
    )(core, g, got)


def _pair_exchange(cfg, g_in_t, grads, name):
    na = 1 + len(grads)
    hd = cfg.D // 2

    def body(*refs):
        g_refs, out_refs = refs[:na], refs[na:2 * na]
        send_sems, recv_sems = refs[2 * na:]
        x, y, c = lax.axis_index("x"), lax.axis_index("y"), lax.axis_index("c")
        cps = [pltpu.make_async_remote_copy(
            src_ref=g_refs[0].at[:, pl.ds(pl.multiple_of((1 - c) * hd, hd), hd)], dst_ref=out_refs[0],
            send_sem=send_sems.at[0, 0], recv_sem=recv_sems.at[0, 0], device_id=(x, y, 1 - c), device_id_type=MESH)]
        for a in range(1, na):
            cps += [pltpu.make_async_remote_copy(src_ref=g_refs[a].at[q, 1 - c], dst_ref=out_refs[a].at[q],
                                                 send_sem=send_sems.at[a, q], recv_sem=recv_sems.at[a, q],
                                                 device_id=(x, y, 1 - c), device_id_type=MESH) for q in range(4)]
        for cp in cps:
            cp.start()
        for cp in cps:
            cp.wait()

    out_shape = [jax.ShapeDtypeStruct((cfg.NP, hd), g_in_t.dtype)] + \
                [jax.ShapeDtypeStruct((4,) + g.shape[2:], g.dtype) for g in grads]
    return pl.pallas_call(
        body, name=name, out_shape=out_shape, in_specs=_hbm_specs(na), out_specs=_hbm_specs(na),
        scratch_shapes=[pltpu.SemaphoreType.DMA((na, 4)), pltpu.SemaphoreType.DMA((na, 4))],
    )(g_in_t, *grads)


def _reduce_scatter_start(cfg, g_in_t, grads, after):
    core = lax.axis_index("c").astype(jnp.int32).reshape(1)
    got = _pair_exchange(cfg, g_in_t, grads, "rs_pair")
    part = [_add_half(g, h, core, "rs_add_pair") for g, h in zip([g_in_t] + list(grads), got)]
    slots = [lax.empty(s.shape, s.dtype) for s in _slot_shapes(cfg, part)]
    n = 3 * (len(_row_map(cfg)) + len(part) - 1)
    return _split_start(part, slots, after, _scatter_plan(cfg, len(part)), n, "rs_chips_start")


def _reduce_scatter_end(cfg, started, after):
    send_sems, recv_sems, parts, slots, _ = started
    parts, slots = _split_wait(parts, slots, send_sems, recv_sems, after, _scatter_plan(cfg, len(parts)), "rs_chips_wait")
    slots = _place_own(cfg, parts, slots, "rs_own")
    mine = [_sum_blocks(s, f32, "rs_add_chips") for s in slots]
    return mine, _send_sibling(mine, "rs_halves")


def _big_weights(cfg):
    return (("mla_w_uq", cfg.QL, cfg.QW, 1), ("mla_w_ukv", cfg.KL, cfg.KVW, 1),
            ("w_branch", cfg.RW + cfg.LW + cfg.MW, cfg.D, 0), ("w_out", cfg.D, cfg.D, 0))


def _half_shapes(cfg):
    out = []
    for _, r, c, ax in _big_weights(cfg):
        out.append((r // 2, c // 4) if ax == 1 else (r // 8, c))
    return out


def _my_halves(cfg, W, l, c):
    hd = cfg.D // 2
    out = [lax.dynamic_slice_in_dim(W["w_in"][l].T, c * hd, hd, axis=1).astype(bf16)]
    for (name, *_), (hr, nc) in zip(_big_weights(cfg), _half_shapes(cfg)):
        out.append(lax.dynamic_slice_in_dim(W[name][l], c * hr, hr, axis=0).astype(bf16))
    return out


def _uq_split(cfg, w):
    hw = HEAD + ROPE
    return jnp.concatenate([w[:, h * hw:h * hw + HEAD] for h in range(cfg.MH)]
                           + [w[:, h * hw + HEAD:(h + 1) * hw] for h in range(cfg.MH)], axis=1)


def _uq_join(cfg, g):
    n = cfg.MH * HEAD
    parts = []
    for h in range(cfg.MH):
        parts += [g[:, h * HEAD:(h + 1) * HEAD], g[:, n + h * ROPE:n + (h + 1) * ROPE]]
    return jnp.concatenate(parts, axis=1)


def _col_blocks(g):
    nc = g.shape[1] // 4
    return jnp.stack([g[:, q * nc:(q + 1) * nc] for q in range(4)])


def _row_pack(parts):
    rows = []
    for p in parts:
        r = p.reshape(-1, LANES)
        pad = -r.shape[0] % SUBLANES
        rows.append(jnp.concatenate([r, jnp.zeros((pad, LANES), r.dtype)], axis=0) if pad else r)
    return jnp.concatenate(rows, axis=0)


def _row_unpack(packed, like):
    out, off = [], 0
    for p in like:
        n = p.size // LANES
        out.append(packed[off:off + n].reshape(p.shape))
        off += -(-n // SUBLANES) * SUBLANES
    return out


def _prep_layer(cfg, full, small):
    w_in_t, w_uq, w_ukv, w_branch, w_out = full
    RW, LW = cfg.RW, cfg.LW
    P = dict(small)
    P["w_in_t"] = w_in_t
    P["w_uq"] = _uq_split(cfg, jnp.concatenate(list(w_uq.reshape(4, cfg.QL, -1)), axis=1))
    P["w_ukv"] = jnp.concatenate(list(w_ukv.reshape(4, cfg.KL, -1)), axis=1)
    P["wb"] = (w_branch[:RW], w_branch[RW:RW + LW], w_branch[RW + LW:])
    P["w_out"] = w_out
    return P


def _layer_fwd(cfg, x, mod, P, T):
    h = _prenorm_fwd(cfg, x, mod, P["norm_pre"])
    proj = _mm(h, P["w_in_t"], f32, "mm_proj", mode="nt")
    y_ret = _ret_fwd(cfg, proj, P["ret_gn"], T["cos_r"], T["sin_r"], T["ret_consts"])
    a, b = _lru_gates(cfg, proj, P["lru_conv_w"], P["lru_conv_b"], P["lru_wa"], P["lru_ba"], P["lru_wx"], P["lru_bx"],
                      P["lru_lambda"])
    hl, y_lru = _lru_scan_fwd(cfg, proj, a, b)
    qn, kn = _mla_norm(cfg, proj, P["mla_q_norm"], P["mla_kv_norm"])
    q = _mm(qn, P["w_uq"], f32, "mm_uq")
    kv = _mm(kn, P["w_ukv"], f32, "mm_ukv")
    q3, k3, v3 = _mla_pack(cfg, proj, q, kv, T["cos_q"], T["sin_q"], T["cos_k"], T["sin_k"])
    o, y_mla = _mla_attn_fwd(cfg, proj, q3, k3, v3)
    ys = (y_ret, y_lru, y_mla)
    us = tuple(_mm(yb, wb, f32, "mm_branch") for yb, wb in zip(ys, P["wb"]))
    merged = _merge_fwd(cfg, proj, *us)
    y = _mm(merged, P["w_out"], f32, "mm_out")
    out = _postnorm_fwd(cfg, x, y, mod, P["norm_post"])
    R = dict(x=x, h=h, proj=proj, ys=ys, a=a, hl=hl, qn=qn, kn=kn, q3=q3, k3=k3, v3=v3, o=o, us=us, merged=merged, y=y)
    return out, R


def _layer_bwd(cfg, dout, R, mod, P, T):
    proj = R["proj"]
    dy, s_post = _postnorm_bwd(cfg, dout, R["y"], mod, P["norm_post"])
    dmerged = _mm(dy, P["w_out"], f32, "mm_dmerged", mode="nt")
    g_out = _mm(R["merged"], dy, bf16, "mm_gw_out", mode="tn")
    du0, du1, du2, dlog = _merge_bwd(cfg, proj, dmerged, *R["us"])
    dus = (du0, du1, du2)
    dys = tuple(_mm(du, wb, f32, "mm_dbranch", mode="nt") for du, wb in zip(dus, P["wb"]))
    g_branch = jnp.concatenate([_mm(yb, du, bf16, "mm_gw_branch", mode="tn") for yb, du in zip(R["ys"], dus)], axis=0)
    drq, drk, drv, drg, dgn = _ret_bwd(cfg, proj, dys[0], P["ret_gn"], T["cos_r"], T["sin_r"], T["ret_consts"])
    da, db, dlg = _lru_scan_bwd(cfg, proj, R["a"], R["hl"], dys[1])
    dxc, dwa, dwx, s_lru = _lru_gates_bwd(cfg, proj, da, db, P["lru_conv_w"], P["lru_conv_b"], P["lru_wa"], P["lru_ba"],
                                          P["lru_wx"], P["lru_bx"], P["lru_lambda"])
    dlx, s_conv = _lru_conv_bwd(cfg, proj, dxc, P["lru_conv_w"])
    dq3, dk3, dv3, dmg = _mla_attn_bwd(cfg, proj, R["q3"], R["k3"], R["v3"], R["o"], dys[2])
    dq, dkv, dmkr = _mla_unpack_bwd(cfg, dq3, dk3, dv3, T["cos_q"], T["sin_q"], T["cos_k"], T["sin_k"])
    dqn = _mm(dq, P["w_uq"], f32, "mm_dqn", mode="nt")
    dkn = _mm(dkv, P["w_ukv"], f32, "mm_dkn", mode="nt")
    g_uq = _uq_join(cfg, _mm(R["qn"], dq, bf16, "mm_gw_uq", mode="tn"))
    g_ukv = _mm(R["kn"], dkv, bf16, "mm_gw_ukv", mode="tn")
    dmq, dmkv, s_q, s_k = _mla_norm_bwd(cfg, proj, dqn, dkn, P["mla_q_norm"], P["mla_kv_norm"])
    dproj = jnp.concatenate([drq, drk, drv, drg, dlx, dlg, dmq, dmkv, dmg, dlog, dmkr,
                             jnp.zeros((cfg.S, cfg.NP - cfg.o_mkr - HEAD), bf16)], axis=1)
    dh = _mm(dproj, P["w_in_t"], f32, "mm_dh")
    g_in_t = _mm(dproj, R["h"], bf16, "mm_gw_in", mode="tn", tm=512)
    dx, s_pre = _prenorm_bwd(cfg, R["x"], dh, dout, mod, P["norm_pre"])
    big = [_col_blocks(g_uq), _col_blocks(g_ukv), g_branch, g_out]
    big = (g_in_t, [g.reshape(4, 2, hr, nc) for g, (hr, nc) in zip(big, _half_shapes(cfg))])
    small = dict(norm_pre=s_pre[2:3], norm_post=s_post[1:2], ret_gn=dgn, lru_conv_w=s_conv[0:CONV], lru_conv_b=s_conv[CONV:CONV + 1],
                 lru_wa=dwa, lru_ba=s_lru[0:1], lru_wx=dwx, lru_bx=s_lru[1:2], lru_lambda=s_lru[2:3],
                 mla_q_norm=s_q[0:1], mla_kv_norm=s_k[0:1])
    dmod = jnp.concatenate([s_pre[0:1], s_pre[1:2], s_post[0:1]], axis=1)
    return dx, big, small, dmod


_SMALL = ("norm_pre", "norm_post", "ret_gn", "lru_conv_w", "lru_conv_b", "lru_wa", "lru_ba", "lru_wx", "lru_bx", "lru_lambda",
          "mla_q_norm", "mla_kv_norm")
_WEIGHTS = ("ada_w", "ada_b", "norm_pre", "norm_post", "w_in", "ret_gn", "lru_conv_w", "lru_conv_b", "lru_wa", "lru_ba", "lru_wx",
            "lru_bx", "lru_lambda", "mla_q_norm", "mla_w_uq", "mla_kv_norm", "mla_w_ukv", "w_branch", "w_out")


def _step(cfg, x, c, positions, W, target, M1, V1):
    L, D = cfg.L, cfg.D
    xi, yi, ci = lax.axis_index("x"), lax.axis_index("y"), lax.axis_index("c")
    chip = 2 * xi + yi
    me = 2 * chip + ci

    c8 = jnp.concatenate([c, jnp.zeros((SUBLANES - 1, D), f32)], axis=0)
    c_all = _allgather8(c8, "gather_c").reshape(N_DEV, SUBLANES, D)[:, 0]
    mod_sh, c_act = _ada_fwd(cfg, c_all, W["ada_w"])
    n_sh = mod_sh.shape[2]
    mod_half = lax.dynamic_slice_in_dim(mod_sh, ci * (n_sh // 2), n_sh // 2, axis=2).reshape(L * N_DEV, n_sh // 2)
    mod_all = _allgather8(mod_half, "gather_mod").reshape(N_DEV, L, N_DEV, n_sh // 2)
    mod_all = mod_all.transpose(1, 2, 0, 3).reshape(L, N_DEV, 3 * D)
    mods = lax.dynamic_index_in_dim(mod_all, me, axis=1, keepdims=False) + W["ada_b"]

    (cos_r, sin_r), (cos_m, sin_m) = _rope_tables(cfg, positions)
    T = dict(cos_r=cos_r, sin_r=sin_r, cos_q=jnp.tile(cos_m, (1, cfg.MH)), sin_q=jnp.tile(sin_m, (1, cfg.MH)),
             cos_k=jnp.tile(cos_m, (1, 2)), sin_k=jnp.tile(sin_m, (1, 2)), ret_consts=_ret_consts(cfg))

    Ps, Rs = [], []
    act = x[0]
    cw_all = _allgather8(_pad_rows(W["lru_conv_w"].reshape(L * CONV, -1)), "gather_conv", after=mods)
    started = _gather_weights_start(cfg, _my_halves(cfg, W, 0, ci), cw_all)
    cw_rows = cw_all.shape[0] // N_DEV
    cw_all = cw_all.reshape(4, 2, cw_rows, -1)[:, 0, :L * CONV].transpose(1, 0, 2).reshape(L, CONV, cfg.LW)
    after = mods
    for l in range(L):
        gathered = _gather_weights_end(cfg, started[2], started, after)
        small = {k: (W[k][l] if W[k][l].ndim > 1 else W[k][l][None, :]) for k in _SMALL if k != "lru_conv_w"}
        P = _prep_layer(cfg, gathered, small)
        P["lru_conv_w"] = cw_all[l]
        Ps.append(P)
        mod = mods[l:l + 1]
        if l + 1 < L:
            started = _gather_weights_start(cfg, _my_halves(cfg, W, l + 1, ci), gathered[-1])
            mod = mod + started[4][0, 0]
        act, R = _layer_fwd(cfg, act, mod, P, T)
        Rs.append(R)
        after = act

    dact, lsum = _loss_head(cfg, act, target[0])
    loss = lax.psum(lsum[0, 0], ("x", "y", "c"))

    big_g = [None] * L
    small_g = [None] * L
    dmods = [None] * L
    pending = None
    for l in range(L - 1, -1, -1):
        mod = mods[l:l + 1]
        if pending is not None:
            mod = mod + pending[4][0, 0]
        dact, grads, small_g[l], dmods[l] = _layer_bwd(cfg, dact, Rs[l], mod, Ps[l], T)
        if pending is not None:
            big_g[l + 1] = _reduce_scatter_end(cfg, pending, dact)
        if l > 0:
            pending = _reduce_scatter_start(cfg, *grads, grads[1][-1])

    dmod = jnp.concatenate(dmods, axis=0)
    parts = [dmod] + [small_g[l][k] for l in range(L) for k in _SMALL]
    packed = _row_pack(parts)
    allf = _allgather8(packed, "gather_small")
    pending = _reduce_scatter_start(cfg, *grads, allf)
    tok = pending[4][0, 0]
    allf = allf.reshape(N_DEV, packed.shape[0], LANES)
    summed = _row_unpack(_sum_blocks(allf, f32, "sum_small"), parts)
    gsm = {k: jnp.stack([summed[1 + l * len(_SMALL) + i].reshape(W[k].shape[1:] if k != "lru_conv_w" else (CONV, cfg.LW))
                         for l in range(L)]) for i, k in enumerate(_SMALL)}
    ncw = cfg.LW // 4
    gsm["lru_conv_w"] = lax.dynamic_slice_in_dim(gsm["lru_conv_w"], chip * ncw, ncw, axis=2)
    gsm["ada_b"] = summed[0]
    dmod_all = allf[:, :dmod.size // LANES].reshape(N_DEV, L, 3 * D)
    dmod_sh = lax.dynamic_slice_in_dim(dmod_all, chip * n_sh, n_sh, axis=2).transpose(1, 0, 2) + tok
    G = dict(gsm)
    G["ada_w"] = _ada_bwd(cfg, c_act.T, dmod_sh)
    delta, new_m, new_v = {}, {}, {}
    bigs = ("ada_w", "w_in") + tuple(name for name, *_ in _big_weights(cfg))
    shp = W["ada_w"].shape
    two = lambda a: a.reshape(-1, shp[-1])
    d, m_, v_ = _adamw(two(W["ada_w"]), two(G["ada_w"]), two(M1["ada_w"]), two(V1["ada_w"]), "adamw_ada_w")
    delta["ada_w"], new_m["ada_w"], new_v["ada_w"] = d.reshape(shp), m_.reshape(shp), v_.reshape(shp)
    smalls = [k for k in _WEIGHTS if k not in bigs]
    packs = [_row_pack([src[k] for k in smalls]) for src in (W, G, M1, V1)]
    packs[1] = packs[1] + tok
    outs = _adamw(*packs, "adamw_small")
    for dst, o in zip((delta, new_m, new_v), outs):
        for k, val in zip(smalls, _row_unpack(o, [W[k] for k in smalls])):
            dst[k] = val
    core = ci.astype(jnp.int32).reshape(1)
    tr_ = lambda a: a.transpose(0, 2, 1)

    def update(l0, l1, prev):
        res = {}
        for i, (name, *_) in enumerate(_big_weights(cfg)):
            res[name] = _adamw_big(W[name], M1[name], V1[name], l0, [big_g[l][0][i + 1] for l in range(l0, l1)],
                                   [big_g[l][1][i + 1] for l in range(l0, l1)], core, pending[4], "adamw_" + name,
                                   prev=prev and prev[name])
        res["w_in"] = _adamw_big(tr_(W["w_in"]), tr_(M1["w_in"]), tr_(V1["w_in"]), l0, [big_g[l][0][0] for l in range(l0, l1)],
                                 [big_g[l][1][0] for l in range(l0, l1)], core, pending[4], "adamw_w_in", half_cols=True,
                                 prev=prev and prev["w_in"])
        return res

    upper = update(1, L, None) if L > 1 else None
    big_g[0] = _reduce_scatter_end(cfg, pending, upper["w_in"][0] if upper else outs[0])
    res = update(0, 1, upper)
    for name, *_ in _big_weights(cfg):
        G[name], delta[name], new_m[name], new_v[name] = res[name]
    G["w_in"], delta["w_in"], new_m["w_in"], new_v["w_in"] = [tr_(o) for o in res["w_in"]]

    grad_x = dact[None]
    return (loss, grad_x, *[G[k] for k in _WEIGHTS], *[delta[k] for k in _WEIGHTS], *[new_m[k] for k in _WEIGHTS],
            *[new_v[k] for k in _WEIGHTS])


def _pad_rows(a):
    pad = -a.shape[0] % SUBLANES
    return jnp.concatenate([a, jnp.zeros((pad, a.shape[1]), a.dtype)], axis=0) if pad else a


def kernel(x, c, positions, ada_w, ada_b, norm_pre, norm_post, w_in, ret_gn, lru_conv_w, lru_conv_b, lru_wa, lru_ba, lru_wx, lru_bx, lru_lambda, mla_q_norm, mla_w_uq, mla_kv_norm, mla_w_ukv, w_branch, w_out, loss_target, m_ada_w, m_ada_b, m_norm_pre, m_norm_post, m_w_in, m_ret_gn, m_lru_conv_w, m_lru_conv_b, m_lru_wa, m_lru_ba, m_lru_wx, m_lru_bx, m_lru_lambda, m_mla_q_norm, m_mla_w_uq, m_mla_kv_norm, m_mla_w_ukv, m_w_branch, m_w_out, v_ada_w, v_ada_b, v_norm_pre, v_norm_post, v_w_in, v_ret_gn, v_lru_conv_w, v_lru_conv_b, v_lru_wa, v_lru_ba, v_lru_wx, v_lru_bx, v_lru_lambda, v_mla_q_norm, v_mla_w_uq, v_mla_kv_norm, v_mla_w_ukv, v_w_branch, v_w_out):
    W = dict(ada_w=ada_w, ada_b=ada_b, norm_pre=norm_pre, norm_post=norm_post, w_in=w_in, ret_gn=ret_gn, lru_conv_w=lru_conv_w,
             lru_conv_b=lru_conv_b, lru_wa=lru_wa, lru_ba=lru_ba, lru_wx=lru_wx, lru_bx=lru_bx, lru_lambda=lru_lambda,
             mla_q_norm=mla_q_norm, mla_w_uq=mla_w_uq, mla_kv_norm=mla_kv_norm, mla_w_ukv=mla_w_ukv, w_branch=w_branch, w_out=w_out)
    M1 = dict(ada_w=m_ada_w, ada_b=m_ada_b, norm_pre=m_norm_pre, norm_post=m_norm_post, w_in=m_w_in, ret_gn=m_ret_gn,
              lru_conv_w=m_lru_conv_w, lru_conv_b=m_lru_conv_b, lru_wa=m_lru_wa, lru_ba=m_lru_ba, lru_wx=m_lru_wx, lru_bx=m_lru_bx,
              lru_lambda=m_lru_lambda, mla_q_norm=m_mla_q_norm, mla_w_uq=m_mla_w_uq, mla_kv_norm=m_mla_kv_norm,
              mla_w_ukv=m_mla_w_ukv, w_branch=m_w_branch, w_out=m_w_out)
    V1 = dict(ada_w=v_ada_w, ada_b=v_ada_b, norm_pre=v_norm_pre, norm_post=v_norm_post, w_in=v_w_in, ret_gn=v_ret_gn,
              lru_conv_w=v_lru_conv_w, lru_conv_b=v_lru_conv_b, lru_wa=v_lru_wa, lru_ba=v_lru_ba, lru_wx=v_lru_wx, lru_bx=v_lru_bx,
              lru_lambda=v_lru_lambda, mla_q_norm=v_mla_q_norm, mla_w_uq=v_mla_w_uq, mla_kv_norm=v_mla_kv_norm,
              mla_w_ukv=v_mla_w_ukv, w_branch=v_w_branch, w_out=v_w_out)
    return _step(_CFG, x, c, positions, W, loss_target, M1, V1)
```

```python
from typing import NamedTuple

import numpy as np
import jax
import jax.numpy as jnp
from jax import lax
from jax.experimental import pallas as pl
from jax.experimental.pallas import tpu as pltpu

f32 = jnp.float32
bf16 = jnp.bfloat16

NORM_EPS = 1e-6
ROPE_BASE = 10000.0
CHUNK = 64
HEAD = 128
ROPE = 64
CONV = 4
LRU_C = 8.0
ADAM_LR, ADAM_B1, ADAM_B2, ADAM_EPS, ADAM_WD, ADAM_STEP = 0.001, 0.9, 0.999, 1e-08, 0.01, 10

LANES = 128
SUBLANES = 8
VMEM_LIMIT = 56 * 1024 * 1024
MM_BUDGET = 40 * 1024 * 1024
N_DEV = 8
MESH = pl.DeviceIdType.MESH


class Cfg(NamedTuple):
    D: int = 2048
    S: int = 2048
    L: int = 4
    H: int = 8
    NB: int = 8
    MH: int = 8
    QL: int = 512
    KL: int = 512
    TR: int = 256
    TQ: int = 512

    @property
    def RW(self): return self.H * HEAD
    @property
    def LW(self): return self.NB * HEAD
    @property
    def MW(self): return self.MH * HEAD
    @property
    def o_rk(self): return self.RW
    @property
    def o_rv(self): return 2 * self.RW
    @property
    def o_rg(self): return 3 * self.RW
    @property
    def o_lx(self): return 4 * self.RW
    @property
    def o_lg(self): return 4 * self.RW + self.LW
    @property
    def o_mq(self): return 4 * self.RW + 2 * self.LW
    @property
    def o_mkv(self): return self.o_mq + self.QL
    @property
    def o_mg(self): return self.o_mkv + self.KL
    @property
    def o_merge(self): return self.o_mg + self.MW
    @property
    def o_mkr(self): return self.o_merge + 3 * self.D
    @property
    def NP(self): return -(-(self.o_mkr + ROPE) // 512) * 512
    @property
    def IN_WIDTH(self): return self.o_mkr + ROPE
    @property
    def QW(self): return self.MH * (HEAD + ROPE)
    @property
    def KVW(self): return self.MH * 2 * HEAD


_CFG = Cfg()


def _cparams(sem=None):
    return pltpu.CompilerParams(dimension_semantics=sem, vmem_limit_bytes=VMEM_LIMIT)


def _sigmoid(x):
    return jax.nn.sigmoid(x)


def _silu(x):
    return x * _sigmoid(x)


def _dsilu(x):
    s = _sigmoid(x)
    return s * (1.0 + x * (1.0 - s))


def _slab(rows, width, off):
    assert off % width == 0
    return pl.BlockSpec((rows, width), lambda i, _c=off // width: (i, _c))


def _row(width):
    return pl.BlockSpec((1, width), lambda i: (0, 0))


def _mm(a, b, out_dtype=f32, name="mm", mode="nn", tm=None):
    (M, K) = a.shape if mode != "tn" else a.shape[::-1]
    (K2, N) = b.shape if mode != "nt" else b.shape[::-1]
    assert K == K2
    tn = N if N <= 2048 else 512
    tk = K if K <= 2048 else 512
    assert N % tn == 0 and K % tk == 0
    osz = jnp.dtype(out_dtype).itemsize
    if tm is None:
        tm = M
        while 2 * tm * tk * 2 + 2 * tk * tn * 2 + 2 * tm * tn * osz + tm * tn * 4 > MM_BUDGET and tm % 16 == 0:
            tm //= 2
    assert M % tm == 0
    nk = K // tk
    dims = {"nn": (((1,), (0,)), ((), ())), "nt": (((1,), (1,)), ((), ())), "tn": (((0,), (0,)), ((), ()))}[mode]

    def dot(a_ref, b_ref):
        return lax.dot_general(a_ref[...].astype(bf16), b_ref[...].astype(bf16), dims, preferred_element_type=f32)

    if nk == 1:
        def body(a_ref, b_ref, o_ref):
            o_ref[...] = dot(a_ref, b_ref).astype(o_ref.dtype)
        scratch = []
    else:
        def body(a_ref, b_ref, o_ref, acc_ref):
            k = pl.program_id(2)

            @pl.when(k == 0)
            def _():
                acc_ref[...] = jnp.zeros_like(acc_ref)

            acc_ref[...] += dot(a_ref, b_ref)

            @pl.when(k == nk - 1)
            def _():
                o_ref[...] = acc_ref[...].astype(o_ref.dtype)
        scratch = [pltpu.VMEM((tm, tn), f32)]

    a_spec = pl.BlockSpec((tk, tm), lambda i, j, k: (k, i)) if mode == "tn" else pl.BlockSpec((tm, tk), lambda i, j, k: (i, k))
    b_spec = pl.BlockSpec((tn, tk), lambda i, j, k: (j, k)) if mode == "nt" else pl.BlockSpec((tk, tn), lambda i, j, k: (k, j))
    return pl.pallas_call(
        body, name=name,
        grid=(M // tm, N // tn, nk),
        in_specs=[a_spec, b_spec],
        out_specs=pl.BlockSpec((tm, tn), lambda i, j, k: (i, j)),
        out_shape=jax.ShapeDtypeStruct((M, N), out_dtype),
        scratch_shapes=scratch,
        compiler_params=_cparams(("parallel", "parallel", "arbitrary")),
    )(a, b)


def _ada_fwd(cfg, c_all, ada_w):
    L, D, n = ada_w.shape
    tn = n // 2 if (n // 2) % LANES == 0 else n

    def body(c_ref, w_ref, o_ref, ca_ref):
        ca = _silu(c_ref[...])
        ca_ref[...] = ca
        o_ref[0] = jnp.dot(ca.astype(bf16), w_ref[0].astype(bf16), preferred_element_type=f32)

    return pl.pallas_call(
        body, name="ada_fwd", grid=(L, n // tn),
        in_specs=[pl.BlockSpec((N_DEV, D), lambda l, j: (0, 0)), pl.BlockSpec((1, D, tn), lambda l, j: (l, 0, j))],
        out_specs=(pl.BlockSpec((1, N_DEV, tn), lambda l, j: (l, 0, j)), pl.BlockSpec((N_DEV, D), lambda l, j: (0, 0))),
        out_shape=(jax.ShapeDtypeStruct((L, N_DEV, n), f32), jax.ShapeDtypeStruct((N_DEV, D), f32)),
        compiler_params=_cparams(("arbitrary", "arbitrary")),
    )(c_all, ada_w)


def _ada_bwd(cfg, c_act_t, dmod):
    L, _, n = dmod.shape
    D = c_act_t.shape[0]
    tn = n // 2 if (n // 2) % LANES == 0 else n

    def body(c_ref, d_ref, o_ref):
        o_ref[0] = jnp.dot(c_ref[...].astype(bf16), d_ref[0].astype(bf16), preferred_element_type=f32)

    return pl.pallas_call(
        body, name="ada_bwd", grid=(L, n // tn),
        in_specs=[pl.BlockSpec((D, N_DEV), lambda l, j: (0, 0)), pl.BlockSpec((1, N_DEV, tn), lambda l, j: (l, 0, j))],
        out_specs=pl.BlockSpec((1, D, tn), lambda l, j: (l, 0, j)),
        out_shape=jax.ShapeDtypeStruct((L, D, n), f32),
        compiler_params=_cparams(("parallel", "parallel")),
    )(c_act_t, dmod)


def _prenorm_fwd(cfg, x, mod, gain):
    S, D, TR = cfg.S, cfg.D, cfg.TR

    def body(x_ref, mod_ref, g_ref, h_ref):
        x = x_ref[...]
        r = lax.rsqrt(jnp.mean(x * x, axis=-1, keepdims=True) + NORM_EPS)
        shift, scale = mod_ref[:, 0:D], mod_ref[:, D:2 * D]
        h_ref[...] = ((x * r) * g_ref[...] * (1.0 + scale) + shift).astype(bf16)

    return pl.pallas_call(
        body, name="prenorm_fwd", grid=(S // TR,),
        in_specs=[_slab(TR, D, 0), _row(3 * D), _row(D)],
        out_specs=_slab(TR, D, 0), out_shape=jax.ShapeDtypeStruct((S, D), bf16),
        compiler_params=_cparams(("parallel",)),
    )(x, mod, gain)


def _prenorm_bwd(cfg, x, dh, dres, mod, gain):
    S, D, TR = cfg.S, cfg.D, cfg.TR

    def body(x_ref, dh_ref, dres_ref, mod_ref, g_ref, dx_ref, sum_ref):
        i = pl.program_id(0)
        x, dh, g = x_ref[...], dh_ref[...], g_ref[...]
        scale = mod_ref[:, D:2 * D]
        r = lax.rsqrt(jnp.mean(x * x, axis=-1, keepdims=True) + NORM_EPS)
        xn = x * r
        t = dh * xn
        dxn = dh * (g * (1.0 + scale))
        dx_ref[...] = r * (dxn - xn * jnp.mean(dxn * xn, axis=-1, keepdims=True)) + dres_ref[...]
        part = jnp.concatenate([jnp.sum(dh, axis=0, keepdims=True), jnp.sum(t * g, axis=0, keepdims=True),
                                jnp.sum(t * (1.0 + scale), axis=0, keepdims=True), jnp.zeros((SUBLANES - 3, D), f32)], axis=0)

        @pl.when(i == 0)
        def _():
            sum_ref[...] = part

        @pl.when(i > 0)
        def _():
            sum_ref[...] += part

    return pl.pallas_call(
        body, name="prenorm_bwd", grid=(S // TR,),
        in_specs=[_slab(TR, D, 0), _slab(TR, D, 0), _slab(TR, D, 0), _row(3 * D), _row(D)],
        out_specs=(_slab(TR, D, 0), pl.BlockSpec((SUBLANES, D), lambda i: (0, 0))),
        out_shape=(jax.ShapeDtypeStruct((S, D), f32), jax.ShapeDtypeStruct((SUBLANES, D), f32)),
        compiler_params=_cparams(("arbitrary",)),
    )(x, dh, dres, mod, gain)


def _postnorm_fwd(cfg, x, y, mod, gain):
    S, D, TR = cfg.S, cfg.D, cfg.TR

    def body(x_ref, y_ref, mod_ref, g_ref, o_ref):
        y = y_ref[...]
        r = lax.rsqrt(jnp.mean(y * y, axis=-1, keepdims=True) + NORM_EPS)
        rg = mod_ref[:, 2 * D:3 * D]
        o_ref[...] = x_ref[...] + (1.0 + rg) * ((y * r) * g_ref[...])

    return pl.pallas_call(
        body, name="postnorm_fwd", grid=(S // TR,),
        in_specs=[_slab(TR, D, 0), _slab(TR, D, 0), _row(3 * D), _row(D)],
        out_specs=_slab(TR, D, 0), out_shape=jax.ShapeDtypeStruct((S, D), f32),
        compiler_params=_cparams(("parallel",)),
    )(x, y, mod, gain)


def _postnorm_bwd(cfg, dout, y, mod, gain):
    S, D, TR = cfg.S, cfg.D, cfg.TR

    def body(do_ref, y_ref, mod_ref, g_ref, dy_ref, sum_ref):
        i = pl.program_id(0)
        do, y, g = do_ref[...], y_ref[...], g_ref[...]
        rg = mod_ref[:, 2 * D:3 * D]
        r = lax.rsqrt(jnp.mean(y * y, axis=-1, keepdims=True) + NORM_EPS)
        yn = y * r
        t = do * yn
        dyn = do * ((1.0 + rg) * g)
        dy_ref[...] = (r * (dyn - yn * jnp.mean(dyn * yn, axis=-1, keepdims=True))).astype(bf16)
        part = jnp.concatenate([jnp.sum(t * g, axis=0, keepdims=True), jnp.sum(t * (1.0 + rg), axis=0, keepdims=True),
                                jnp.zeros((SUBLANES - 2, D), f32)], axis=0)

        @pl.when(i == 0)
        def _():
            sum_ref[...] = part

        @pl.when(i > 0)
        def _():
            sum_ref[...] += part

    return pl.pallas_call(
        body, name="postnorm_bwd", grid=(S // TR,),
        in_specs=[_slab(TR, D, 0), _slab(TR, D, 0), _row(3 * D), _row(D)],
        out_specs=(_slab(TR, D, 0), pl.BlockSpec((SUBLANES, D), lambda i: (0, 0))),
        out_shape=(jax.ShapeDtypeStruct((S, D), bf16), jax.ShapeDtypeStruct((SUBLANES, D), f32)),
        compiler_params=_cparams(("arbitrary",)),
    )(dout, y, mod, gain)


def _loss_head(cfg, y, target):
    S, D, TR = cfg.S, cfg.D, cfg.TR

    def body(y_ref, t_ref, d_ref, l_ref):
        i = pl.program_id(0)
        err = y_ref[...] - t_ref[...]
        d_ref[...] = err / D
        part = jnp.zeros((SUBLANES, LANES), f32) + 0.5 * jnp.sum(jnp.mean(err * err, axis=-1, keepdims=True))

        @pl.when(i == 0)
        def _():
            l_ref[...] = part

        @pl.when(i > 0)
        def _():
            l_ref[...] += part

    return pl.pallas_call(
        body, name="loss_head", grid=(S // TR,),
        in_specs=[_slab(TR, D, 0), _slab(TR, D, 0)],
        out_specs=(_slab(TR, D, 0), pl.BlockSpec((SUBLANES, LANES), lambda i: (0, 0))),
        out_shape=(jax.ShapeDtypeStruct((S, D), f32), jax.ShapeDtypeStruct((SUBLANES, LANES), f32)),
        compiler_params=_cparams(("arbitrary",)),
    )(y, target)


def _merge_fwd(cfg, proj, u0, u1, u2):
    S, D, TR = cfg.S, cfg.D, cfg.TR

    def body(l0, l1, l2, u0_ref, u1_ref, u2_ref, o_ref):
        o_ref[...] = (_sigmoid(l0[...]) * u0_ref[...] + _sigmoid(l1[...]) * u1_ref[...]
                      + _sigmoid(l2[...]) * u2_ref[...]).astype(bf16)

    return pl.pallas_call(
        body, name="merge_fwd", grid=(S // TR,),
        in_specs=[_slab(TR, D, cfg.o_merge + b * D) for b in range(3)] + [_slab(TR, D, 0)] * 3,
        out_specs=_slab(TR, D, 0), out_shape=jax.ShapeDtypeStruct((S, D), bf16),
        compiler_params=_cparams(("parallel",)),
    )(proj, proj, proj, u0, u1, u2)


def _merge_bwd(cfg, proj, dmerged, u0, u1, u2):
    S, D, TR = cfg.S, cfg.D, cfg.TR

    def body(l0, l1, l2, dm_ref, u0_ref, u1_ref, u2_ref, du0, du1, du2, dl_ref):
        dm = dm_ref[...]
        for b, (l, u, du) in enumerate(((l0, u0_ref, du0), (l1, u1_ref, du1), (l2, u2_ref, du2))):
            g = _sigmoid(l[...])
            du[...] = (dm * g).astype(bf16)
            dl_ref[:, b * D:(b + 1) * D] = (dm * u[...] * (g * (1.0 - g))).astype(bf16)

    return pl.pallas_call(
        body, name="merge_bwd", grid=(S // TR,),
        in_specs=[_slab(TR, D, cfg.o_merge + b * D) for b in range(3)] + [_slab(TR, D, 0)] * 4,
        out_specs=(_slab(TR, D, 0),) * 3 + (_slab(TR, 3 * D, 0),),
        out_shape=(jax.ShapeDtypeStruct((S, D), bf16),) * 3 + (jax.ShapeDtypeStruct((S, 3 * D), bf16),),
        compiler_params=_cparams(("parallel",)),
    )(proj, proj, proj, dmerged, u0, u1, u2)


def _rope128(x, c, s):
    return x * c + pltpu.roll(x, 64, axis=1) * s


def _rope128_t(dy, c, s):
    return dy * c + pltpu.roll(dy * s, 64, axis=1)


def _swap32(x):
    w = x.shape[1]
    lane = lax.broadcasted_iota(jnp.int32, x.shape, 1)
    return jnp.where((lane % 64) < 32, pltpu.roll(x, w - 32, axis=1), pltpu.roll(x, 32, axis=1))


def _rope64(x, c, s):
    return x * c + _swap32(x) * s


def _rope64_t(dy, c, s):
    return dy * c + _swap32(dy * s)


def _rope_tables(cfg, positions):
    pos = positions.astype(f32)[0][:, None]

    def tab(dim):
        inv_freq = ROPE_BASE ** (-jnp.arange(0, dim, 2, dtype=f32) / dim)
        ang = pos * inv_freq
        cos, sin = jnp.cos(ang), jnp.sin(ang)
        return jnp.concatenate([cos, cos], axis=1), jnp.concatenate([-sin, sin], axis=1)

    return tab(HEAD), tab(ROPE)


def _ret_consts(cfg):
    h = np.arange(cfg.H, dtype=np.float64)
    log_gamma = np.log1p(-np.exp2(-5.0 - h)).astype(np.float32)
    idx = np.arange(CHUNK, dtype=np.float32)
    intra = np.exp(log_gamma[:, None, None] * np.abs(idx[:, None] - idx[None, :]))
    kdec = np.exp(log_gamma[:, None] * (CHUNK - 1 - idx)[None, :])
    qdec = np.exp(log_gamma[:, None] * (idx + 1.0)[None, :])
    cdec = np.exp(log_gamma * CHUNK)
    bc = lambda a: jnp.asarray(np.broadcast_to(a[..., None], a.shape + (HEAD,)).astype(np.float32))
    return jnp.asarray(intra.astype(np.float32)), bc(kdec), bc(qdec), bc(cdec[:, None])


def _ret_core(cfg, q_raw, k_raw, v_raw, cos, sin, intra, kdec, qdec, cdec, p_ref):
    S = cfg.S
    NC = S // CHUNK
    q = _rope128(q_raw, cos, sin) * (HEAD ** -0.5)
    k = _rope128(k_raw, cos, sin)
    q3 = q.reshape(NC, CHUNK, HEAD)
    k3 = k.reshape(NC, CHUNK, HEAD)
    qb, kb = q3.astype(bf16), k3.astype(bf16)
    vb = v_raw.reshape(NC, CHUNK, HEAD).astype(bf16)
    sdb = (jnp.einsum('nid,njd->nij', qb, kb, preferred_element_type=f32) * intra[None]).astype(bf16)
    o_intra = jnp.einsum('nij,nje->nie', sdb, vb, preferred_element_type=f32)
    kdb = (k3 * kdec[None]).astype(bf16)
    kv = jnp.einsum('njd,nje->nde', kdb, vb, preferred_element_type=f32)
    p_ref[0] = jnp.zeros((HEAD, HEAD), f32)
    for n in range(1, NC):
        p_ref[n] = p_ref[n - 1] * cdec + kv[n - 1]
    pb = p_ref[...].astype(bf16)
    qdb = (q3 * qdec[None]).astype(bf16)
    o_inter = jnp.einsum('nid,nde->nie', qdb, pb, preferred_element_type=f32)
    o = (o_intra + o_inter).reshape(S, HEAD)
    return o, (qb, kb, vb, sdb, kdb, qdb, pb)


def _ret_specs(cfg):
    S = cfg.S
    hs = lambda off: pl.BlockSpec((S, HEAD), lambda h, _c=off // HEAD: (0, _c + h))
    full = pl.BlockSpec((S, HEAD), lambda h: (0, 0))
    consts = [pl.BlockSpec((None, CHUNK, CHUNK), lambda h: (h, 0, 0)), pl.BlockSpec((None, CHUNK, HEAD), lambda h: (h, 0, 0)),
              pl.BlockSpec((None, CHUNK, HEAD), lambda h: (h, 0, 0)), pl.BlockSpec((None, 1, HEAD), lambda h: (h, 0, 0))]
    gn = pl.BlockSpec((1, HEAD), lambda h: (0, h))
    return hs, full, consts, gn


def _ret_fwd(cfg, proj, gn, cos, sin, consts):
    S, NC = cfg.S, cfg.S // CHUNK
    hs, full, cspecs, gspec = _ret_specs(cfg)

    def body(q_ref, k_ref, v_ref, g_ref, gn_ref, cos_ref, sin_ref, intra, kdec, qdec, cdec, y_ref, p_ref):
        o, _ = _ret_core(cfg, q_ref[...], k_ref[...], v_ref[...], cos_ref[...], sin_ref[...],
                         intra[...], kdec[...], qdec[...], cdec[...], p_ref)
        mean = jnp.mean(o, axis=-1, keepdims=True)
        var = jnp.mean(jnp.square(o - mean), axis=-1, keepdims=True)
        z = ((o - mean) * lax.rsqrt(var + NORM_EPS)) * gn_ref[...]
        y_ref[...] = (z * _silu(g_ref[...])).astype(bf16)

    return pl.pallas_call(
        body, name="ret_fwd", grid=(cfg.H,),
        in_specs=[hs(0), hs(cfg.o_rk), hs(cfg.o_rv), hs(cfg.o_rg), gspec, full, full] + cspecs,
        out_specs=hs(0), out_shape=jax.ShapeDtypeStruct((S, cfg.RW), bf16),
        scratch_shapes=[pltpu.VMEM((NC, HEAD, HEAD), f32)],
        compiler_params=_cparams(("arbitrary",)),
    )(proj, proj, proj, proj, gn, cos, sin, *consts)


def _ret_bwd(cfg, proj, dy, gn, cos, sin, consts):
    S, NC = cfg.S, cfg.S // CHUNK
    hs, full, cspecs, gspec = _ret_specs(cfg)

    def body(q_ref, k_ref, v_ref, g_ref, dy_ref, gn_ref, cos_ref, sin_ref, intra_ref, kdec_ref, qdec_ref, cdec_ref,
             dq_ref, dk_ref, dv_ref, dg_ref, dgn_ref, p_ref, g_scr):
        cos, sin = cos_ref[...], sin_ref[...]
        intra, kdec, qdec, cdec = intra_ref[...], kdec_ref[...], qdec_ref[...], cdec_ref[...]
        o, (qb, kb, vb, sdb, kdb, qdb, pb) = _ret_core(cfg, q_ref[...], k_ref[...], v_ref[...], cos, sin,
                                                     intra, kdec, qdec, cdec, p_ref)
        gate, dy, gnv = g_ref[...], dy_ref[...], gn_ref[...]
        mean = jnp.mean(o, axis=-1, keepdims=True)
        rstd = lax.rsqrt(jnp.mean(jnp.square(o - mean), axis=-1, keepdims=True) + NORM_EPS)
        on = (o - mean) * rstd
        dz = dy * _silu(gate)
        dg_ref[...] = (dy * (on * gnv) * _dsilu(gate)).astype(bf16)
        dgn_ref[...] = jnp.sum(dz * on, axis=0, keepdims=True)
        don = dz * gnv
        do = rstd * (don - jnp.mean(don, axis=-1, keepdims=True) - on * jnp.mean(don * on, axis=-1, keepdims=True))
        dob = do.reshape(NC, CHUNK, HEAD).astype(bf16)
        dsb = (jnp.einsum('nie,nje->nij', dob, vb, preferred_element_type=f32) * intra[None]).astype(bf16)
        dv = jnp.einsum('nij,nie->nje', sdb, dob, preferred_element_type=f32)
        dq = jnp.einsum('nij,njd->nid', dsb, kb, preferred_element_type=f32)
        dk = jnp.einsum('nij,nid->njd', dsb, qb, preferred_element_type=f32)
        dq = dq + jnp.einsum('nie,nde->nid', dob, pb, preferred_element_type=f32) * qdec[None]
        dp = jnp.einsum('nid,nie->nde', qdb, dob, preferred_element_type=f32)
        g_scr[NC - 1] = jnp.zeros((HEAD, HEAD), f32)
        for n in range(NC - 2, -1, -1):
            g_scr[n] = dp[n + 1] + g_scr[n + 1] * cdec
        gb = g_scr[...].astype(bf16)
        dk = dk + jnp.einsum('nje,nde->njd', vb, gb, preferred_element_type=f32) * kdec[None]
        dv = dv + jnp.einsum('njd,nde->nje', kdb, gb, preferred_element_type=f32)
        dq_ref[...] = _rope128_t(dq.reshape(S, HEAD) * (HEAD ** -0.5), cos, sin).astype(bf16)
        dk_ref[...] = _rope128_t(dk.reshape(S, HEAD), cos, sin).astype(bf16)
        dv_ref[...] = dv.reshape(S, HEAD).astype(bf16)

    return pl.pallas_call(
        body, name="ret_bwd", grid=(cfg.H,),
        in_specs=[hs(0), hs(cfg.o_rk), hs(cfg.o_rv), hs(cfg.o_rg), hs(0), gspec, full, full] + cspecs,
        out_specs=(hs(0),) * 4 + (gspec,),
        out_shape=(jax.ShapeDtypeStruct((S, cfg.RW), bf16),) * 4 + (jax.ShapeDtypeStruct((1, cfg.RW), f32),),
        scratch_shapes=[pltpu.VMEM((NC, HEAD, HEAD), f32), pltpu.VMEM((NC, HEAD, HEAD), f32)],
        compiler_params=_cparams(("arbitrary",)),
    )(proj, proj, proj, proj, dy, gn, cos, sin, *consts)


def _expm1(x):
    small = x * (1.0 + x * (0.5 + x * (1.0 / 6.0 + x * (1.0 / 24.0 + x * (1.0 / 120.0)))))
    return jnp.where(jnp.abs(x) < 0.1, small, jnp.exp(x) - 1.0)


def _softplus(z):
    return jnp.maximum(z, 0.0) + jnp.log1p(jnp.exp(-jnp.abs(z)))


def _lru_conv(cfg, x_ref, halo_ref, cw, scr, first):
    TR = cfg.TR
    scr[0:SUBLANES, :] = jnp.where(first, 0.0, halo_ref[...])
    scr[SUBLANES:SUBLANES + TR, :] = x_ref[...]
    xc = scr[pl.ds(SUBLANES - (CONV - 1), TR), :] * cw[0:1, :]
    for j in range(1, CONV):
        xc = xc + scr[pl.ds(SUBLANES - (CONV - 1) + j, TR), :] * cw[j:j + 1, :]
    return xc


def _lru_pre(cfg, xc, wa_ref, wx_ref, ba, bx):
    xb = xc.astype(bf16)
    pa = jnp.concatenate([jnp.dot(xb[:, n * HEAD:(n + 1) * HEAD], wa_ref[n].astype(bf16), preferred_element_type=f32)
                          for n in range(cfg.NB)], axis=1) + ba
    px = jnp.concatenate([jnp.dot(xb[:, n * HEAD:(n + 1) * HEAD], wx_ref[n].astype(bf16), preferred_element_type=f32)
                          for n in range(cfg.NB)], axis=1) + bx
    return pa, px


def _lru_ab(pa, px, xc, lam):
    r, i = _sigmoid(pa), _sigmoid(px)
    log_a = (-LRU_C * r) * _softplus(-lam)
    a = jnp.exp(log_a)
    b = jnp.sqrt(-_expm1(2.0 * log_a)) * (i * xc)
    return a, b


def _lru_halo_specs(cfg, off, W):
    TR, S = cfg.TR, cfg.S
    nb = TR // SUBLANES
    cb = off // W
    main = pl.BlockSpec((TR, W), lambda i: (i, cb))
    prev = pl.BlockSpec((SUBLANES, W), lambda i: (jnp.maximum(i * nb - 1, 0), cb))
    nxt = pl.BlockSpec((SUBLANES, W), lambda i: (jnp.minimum((i + 1) * nb, S // SUBLANES - 1), cb))
    return main, prev, nxt


def _lru_gates(cfg, proj, cw, cb, wa, ba, wx, bx, lam):
    S, W, TR, NB = cfg.S, cfg.LW, cfg.TR, cfg.NB
    assert cfg.o_lx % W == 0
    main, prev, _ = _lru_halo_specs(cfg, cfg.o_lx, W)
    wspec = pl.BlockSpec((NB, HEAD, HEAD), lambda i: (0, 0, 0))

    def body(x_ref, halo_ref, cw_ref, cb_ref, wa_ref, ba_ref, wx_ref, bx_ref, lam_ref, a_ref, b_ref, scr):
        xc = _lru_conv(cfg, x_ref, halo_ref, cw_ref[...], scr, pl.program_id(0) == 0) + cb_ref[...]
        pa, px = _lru_pre(cfg, xc, wa_ref, wx_ref, ba_ref[...], bx_ref[...])
        a, b = _lru_ab(pa, px, xc, lam_ref[...])
        a_ref[...] = a
        b_ref[...] = b

    return pl.pallas_call(
        body, name="lru_gates", grid=(S // TR,),
        in_specs=[main, prev, pl.BlockSpec((CONV, W), lambda i: (0, 0)), _row(W), wspec, _row(W), wspec, _row(W), _row(W)],
        out_specs=(_slab(TR, W, 0),) * 2, out_shape=(jax.ShapeDtypeStruct((S, W), f32),) * 2,
        scratch_shapes=[pltpu.VMEM((TR + SUBLANES, W), f32)],
        compiler_params=_cparams(("parallel",)),
    )(proj, proj, cw, cb, wa, ba, wx, bx, lam)


def _lru_lane_block(cfg):
    return 256 if cfg.LW % 256 == 0 else LANES


def _lru_scan_fwd(cfg, proj, a, b):
    S, W = cfg.S, cfg.LW
    LB = _lru_lane_block(cfg)
    assert cfg.o_lg % LB == 0
    col = lambda off: pl.BlockSpec((S, LB), lambda j, _c=off // LB: (0, _c + j))

    def body(a_ref, b_ref, g_ref, h_ref, y_ref):
        def blk(t, h):
            r0 = pl.multiple_of(t * SUBLANES, SUBLANES)
            at, bt = a_ref[pl.ds(r0, SUBLANES), :], b_ref[pl.ds(r0, SUBLANES), :]
            rows = []
            for j in range(SUBLANES):
                h = at[j:j + 1, :] * h + bt[j:j + 1, :]
                rows.append(h)
            h_ref[pl.ds(r0, SUBLANES), :] = jnp.concatenate(rows, axis=0)
            return h

        lax.fori_loop(0, S // SUBLANES, blk, jnp.zeros((1, LB), f32))
        y_ref[...] = (h_ref[...] * _silu(g_ref[...])).astype(bf16)

    return pl.pallas_call(
        body, name="lru_scan_fwd", grid=(W // LB,),
        in_specs=[col(0), col(0), col(cfg.o_lg)],
        out_specs=(col(0), col(0)),
        out_shape=(jax.ShapeDtypeStruct((S, W), f32), jax.ShapeDtypeStruct((S, W), bf16)),
        compiler_params=_cparams(("parallel",)),
    )(a, b, proj)


def _lru_scan_bwd(cfg, proj, a, h, dy):
    S, W = cfg.S, cfg.LW
    LB = _lru_lane_block(cfg)
    col = lambda off: pl.BlockSpec((S, LB), lambda j, _c=off // LB: (0, _c + j))

    def body(a_ref, h_ref, dy_ref, g_ref, da_ref, db_ref, dg_ref):
        gate, dy = g_ref[...], dy_ref[...]
        dg_ref[...] = (dy * h_ref[...] * _dsilu(gate)).astype(bf16)
        da_ref[...] = dy * _silu(gate)

        def blk(t, carry):
            dh_next, a_next = carry
            r0 = pl.multiple_of((S // SUBLANES - 1 - t) * SUBLANES, SUBLANES)
            at, ct = a_ref[pl.ds(r0, SUBLANES), :], da_ref[pl.ds(r0, SUBLANES), :]
            rows = [None] * SUBLANES
            for j in range(SUBLANES - 1, -1, -1):
                dh_next = ct[j:j + 1, :] + a_next * dh_next
                a_next = at[j:j + 1, :]
                rows[j] = dh_next
            db_ref[pl.ds(r0, SUBLANES), :] = jnp.concatenate(rows, axis=0)
            return dh_next, a_next

        z = jnp.zeros((1, LB), f32)
        lax.fori_loop(0, S // SUBLANES, blk, (z, z))
        row = lax.broadcasted_iota(jnp.int32, (S, LB), 0)
        hprev = jnp.where(row == 0, 0.0, pltpu.roll(h_ref[...], 1, axis=0))
        da_ref[...] = db_ref[...] * hprev

    return pl.pallas_call(
        body, name="lru_scan_bwd", grid=(W // LB,),
        in_specs=[col(0), col(0), col(0), col(cfg.o_lg)],
        out_specs=(col(0),) * 3,
        out_shape=(jax.ShapeDtypeStruct((S, W), f32),) * 2 + (jax.ShapeDtypeStruct((S, W), bf16),),
        compiler_params=_cparams(("parallel",)),
    )(a, h, dy, proj)


def _lru_gates_bwd(cfg, proj, da, db, cw, cb, wa, ba, wx, bx, lam):
    S, W, TR, NB = cfg.S, cfg.LW, cfg.TR, cfg.NB
    main, prev, _ = _lru_halo_specs(cfg, cfg.o_lx, W)
    wspec = pl.BlockSpec((NB, HEAD, HEAD), lambda i: (0, 0, 0))

    def body(x_ref, halo_ref, da_ref, db_ref, cw_ref, cb_ref, wa_ref, ba_ref, wx_ref, bx_ref, lam_ref,
             dxc_ref, dwa_ref, dwx_ref, sum_ref, scr):
        i = pl.program_id(0)
        lam = lam_ref[...]
        xc = _lru_conv(cfg, x_ref, halo_ref, cw_ref[...], scr, i == 0) + cb_ref[...]
        pa, px = _lru_pre(cfg, xc, wa_ref, wx_ref, ba_ref[...], bx_ref[...])
        _, vjp = jax.vjp(_lru_ab, pa, px, xc, lam)
        dpa, dpx, dxc, dlam = vjp((da_ref[...], db_ref[...]))
        xb, dpab, dpxb = xc.astype(bf16), dpa.astype(bf16), dpx.astype(bf16)
        nt = (((1,), (1,)), ((), ()))
        tn = (((0,), (0,)), ((), ()))
        back = []
        dwa, dwx = [], []
        for n in range(NB):
            sl = slice(n * HEAD, (n + 1) * HEAD)
            back.append(lax.dot_general(dpab[:, sl], wa_ref[n].astype(bf16), nt, preferred_element_type=f32)
                        + lax.dot_general(dpxb[:, sl], wx_ref[n].astype(bf16), nt, preferred_element_type=f32))
            dwa.append(lax.dot_general(xb[:, sl], dpab[:, sl], tn, preferred_element_type=f32))
            dwx.append(lax.dot_general(xb[:, sl], dpxb[:, sl], tn, preferred_element_type=f32))
        dxc_ref[...] = dxc + jnp.concatenate(back, axis=1)
        part = jnp.concatenate([jnp.sum(dpa, axis=0, keepdims=True), jnp.sum(dpx, axis=0, keepdims=True), dlam,
                                jnp.zeros((SUBLANES - 3, W), f32)], axis=0)

        @pl.when(i == 0)
        def _():
            sum_ref[...] = part
            for n in range(NB):
                dwa_ref[n] = dwa[n]
                dwx_ref[n] = dwx[n]

        @pl.when(i > 0)
        def _():
            sum_ref[...] += part
            for n in range(NB):
                dwa_ref[n] += dwa[n]
                dwx_ref[n] += dwx[n]

    return pl.pallas_call(
        body, name="lru_gates_bwd", grid=(S // TR,),
        in_specs=[main, prev, _slab(TR, W, 0), _slab(TR, W, 0), pl.BlockSpec((CONV, W), lambda i: (0, 0)), _row(W),
                  wspec, _row(W), wspec, _row(W), _row(W)],
        out_specs=(_slab(TR, W, 0), wspec, wspec, pl.BlockSpec((SUBLANES, W), lambda i: (0, 0))),
        out_shape=(jax.ShapeDtypeStruct((S, W), f32), jax.ShapeDtypeStruct((NB, HEAD, HEAD), f32),
                   jax.ShapeDtypeStruct((NB, HEAD, HEAD), f32), jax.ShapeDtypeStruct((SUBLANES, W), f32)),
        scratch_shapes=[pltpu.VMEM((TR + SUBLANES, W), f32)],
        compiler_params=_cparams(("arbitrary",)),
    )(proj, proj, da, db, cw, cb, wa, ba, wx, bx, lam)


def _lru_conv_bwd(cfg, proj, dxc, cw):
    S, W, TR = cfg.S, cfg.LW, cfg.TR
    main, prev, _ = _lru_halo_specs(cfg, cfg.o_lx, W)
    dmain, _, dnext = _lru_halo_specs(cfg, 0, W)

    def body(x_ref, xhalo_ref, d_ref, dhalo_ref, cw_ref, dx_ref, sum_ref, xs, ds):
        i = pl.program_id(0)
        cw = cw_ref[...]
        d = d_ref[...]
        xs[0:SUBLANES, :] = jnp.where(i == 0, 0.0, xhalo_ref[...])
        xs[SUBLANES:SUBLANES + TR, :] = x_ref[...]
        ds[0:TR, :] = d
        ds[TR:TR + SUBLANES, :] = jnp.where(i == pl.num_programs(0) - 1, 0.0, dhalo_ref[...])
        dx = ds[pl.ds(CONV - 1, TR), :] * cw[0:1, :]
        parts = [jnp.sum(d * xs[pl.ds(SUBLANES - (CONV - 1), TR), :], axis=0, keepdims=True)]
        for j in range(1, CONV):
            dx = dx + ds[pl.ds(CONV - 1 - j, TR), :] * cw[j:j + 1, :]
            parts.append(jnp.sum(d * xs[pl.ds(SUBLANES - (CONV - 1) + j, TR), :], axis=0, keepdims=True))
        dx_ref[...] = dx.astype(bf16)
        part = jnp.concatenate(parts + [jnp.sum(d, axis=0, keepdims=True), jnp.zeros((SUBLANES - CONV - 1, W), f32)], axis=0)

        @pl.when(i == 0)
        def _():
            sum_ref[...] = part

        @pl.when(i > 0)
        def _():
            sum_ref[...] += part

    return pl.pallas_call(
        body, name="lru_conv_bwd", grid=(S // TR,),
        in_specs=[main, prev, dmain, dnext, pl.BlockSpec((CONV, W), lambda i: (0, 0))],
        out_specs=(_slab(TR, W, 0), pl.BlockSpec((SUBLANES, W), lambda i: (0, 0))),
        out_shape=(jax.ShapeDtypeStruct((S, W), bf16), jax.ShapeDtypeStruct((SUBLANES, W), f32)),
        scratch_shapes=[pltpu.VMEM((TR + SUBLANES, W), f32), pltpu.VMEM((TR + SUBLANES, W), f32)],
        compiler_params=_cparams(("arbitrary",)),
    )(proj, proj, dxc, dxc, cw)


def _rms(x, g):
    r = lax.rsqrt(jnp.mean(x * x, axis=-1, keepdims=True) + NORM_EPS)
    return (x * r) * g, r


def _mla_norm(cfg, proj, qg, kg):
    S, TR = cfg.S, cfg.TR

    def body(q_ref, k_ref, qg_ref, kg_ref, qn_ref, kn_ref):
        qn_ref[...] = _rms(q_ref[...], qg_ref[...])[0].astype(bf16)
        kn_ref[...] = _rms(k_ref[...], kg_ref[...])[0].astype(bf16)

    return pl.pallas_call(
        body, name="mla_norm", grid=(S // TR,),
        in_specs=[_slab(TR, cfg.QL, cfg.o_mq), _slab(TR, cfg.KL, cfg.o_mkv), _row(cfg.QL), _row(cfg.KL)],
        out_specs=(_slab(TR, cfg.QL, 0), _slab(TR, cfg.KL, 0)),
        out_shape=(jax.ShapeDtypeStruct((S, cfg.QL), bf16), jax.ShapeDtypeStruct((S, cfg.KL), bf16)),
        compiler_params=_cparams(("parallel",)),
    )(proj, proj, qg, kg)


def _mla_norm_bwd(cfg, proj, dqn, dkn, qg, kg):
    S, TR = cfg.S, cfg.TR

    def one(x, g, dn):
        r = lax.rsqrt(jnp.mean(x * x, axis=-1, keepdims=True) + NORM_EPS)
        xn = x * r
        dxn = dn * g
        dx = r * (dxn - xn * jnp.mean(dxn * xn, axis=-1, keepdims=True))
        return dx, jnp.sum(dn * xn, axis=0, keepdims=True)

    def body(q_ref, k_ref, dq_ref, dk_ref, qg_ref, kg_ref, dmq_ref, dmk_ref, sq_ref, sk_ref):
        i = pl.program_id(0)
        dq, gq = one(q_ref[...], qg_ref[...], dq_ref[...])
        dk, gk = one(k_ref[...], kg_ref[...], dk_ref[...])
        dmq_ref[...] = dq.astype(bf16)
        dmk_ref[...] = dk.astype(bf16)
        pq = jnp.concatenate([gq, jnp.zeros((SUBLANES - 1, cfg.QL), f32)], axis=0)
        pk = jnp.concatenate([gk, jnp.zeros((SUBLANES - 1, cfg.KL), f32)], axis=0)

        @pl.when(i == 0)
        def _():
            sq_ref[...] = pq
            sk_ref[...] = pk

        @pl.when(i > 0)
        def _():
            sq_ref[...] += pq
            sk_ref[...] += pk

    return pl.pallas_call(
        body, name="mla_norm_bwd", grid=(S // TR,),
        in_specs=[_slab(TR, cfg.QL, cfg.o_mq), _slab(TR, cfg.KL, cfg.o_mkv), _slab(TR, cfg.QL, 0), _slab(TR, cfg.KL, 0),
                  _row(cfg.QL), _row(cfg.KL)],
        out_specs=(_slab(TR, cfg.QL, 0), _slab(TR, cfg.KL, 0), pl.BlockSpec((SUBLANES, cfg.QL), lambda i: (0, 0)),
                   pl.BlockSpec((SUBLANES, cfg.KL), lambda i: (0, 0))),
        out_shape=(jax.ShapeDtypeStruct((S, cfg.QL), bf16), jax.ShapeDtypeStruct((S, cfg.KL), bf16),
                   jax.ShapeDtypeStruct((SUBLANES, cfg.QL), f32), jax.ShapeDtypeStruct((SUBLANES, cfg.KL), f32)),
        compiler_params=_cparams(("arbitrary",)),
    )(proj, proj, dqn, dkn, qg, kg)


def _mla_pack(cfg, proj, q, kv, cq, sq, ck, sk):
    S, TR, MH = cfg.S, cfg.TR, cfg.MH
    NW, RWD = MH * HEAD, MH * ROPE

    def body(q_ref, kv_ref, kr_ref, cq_ref, sq_ref, ck_ref, sk_ref, qo_ref, ko_ref, vo_ref):
        q, kv = q_ref[...], kv_ref[...]
        qr = _rope64(q[:, NW:], cq_ref[...], sq_ref[...])
        kr = _rope64(kr_ref[...], ck_ref[...], sk_ref[...]).astype(bf16)
        lane = lax.broadcasted_iota(jnp.int32, (TR, HEAD), 1)
        for h in range(MH):
            grp = qr[:, (h // 2) * HEAD:(h // 2 + 1) * HEAD]
            if h % 2:
                grp = pltpu.roll(grp, 64, axis=1)
            qo_ref[h] = jnp.concatenate([q[:, h * HEAD:(h + 1) * HEAD], jnp.where(lane < ROPE, grp, 0.0)], axis=1).astype(bf16)
            ko_ref[h] = jnp.concatenate([kv[:, 2 * h * HEAD:(2 * h + 1) * HEAD].astype(bf16), kr], axis=1)
            vo_ref[h] = kv[:, (2 * h + 1) * HEAD:(2 * h + 2) * HEAD].astype(bf16)

    hspec = lambda w: pl.BlockSpec((MH, TR, w), lambda i: (0, i, 0))
    return pl.pallas_call(
        body, name="mla_pack", grid=(S // TR,),
        in_specs=[_slab(TR, cfg.QW, 0), _slab(TR, cfg.KVW, 0), _slab(TR, HEAD, cfg.o_mkr),
                  _slab(TR, RWD, 0), _slab(TR, RWD, 0), _slab(TR, HEAD, 0), _slab(TR, HEAD, 0)],
        out_specs=(hspec(2 * HEAD), hspec(2 * HEAD), hspec(HEAD)),
        out_shape=(jax.ShapeDtypeStruct((MH, S, 2 * HEAD), bf16), jax.ShapeDtypeStruct((MH, S, 2 * HEAD), bf16),
                   jax.ShapeDtypeStruct((MH, S, HEAD), bf16)),
        compiler_params=_cparams(("parallel",)),
    )(q, kv, proj, cq, sq, ck, sk)


def _mla_unpack_bwd(cfg, dq3, dk3, dv3, cq, sq, ck, sk):
    S, TR, MH = cfg.S, cfg.TR, cfg.MH
    RWD = MH * ROPE

    def body(dq_ref, dk_ref, dv_ref, cq_ref, sq_ref, ck_ref, sk_ref, q_ref, kv_ref, kr_ref):
        lane = lax.broadcasted_iota(jnp.int32, (TR, HEAD), 1)
        nope, ropes, kvs = [], [], []
        dkr = jnp.zeros((TR, HEAD), f32)
        for h in range(MH):
            dq = dq_ref[h]
            nope.append(dq[:, :HEAD])
            part = jnp.where(lane < ROPE, dq[:, HEAD:], 0.0)
            if h % 2:
                ropes[-1] = ropes[-1] + pltpu.roll(part, 64, axis=1)
            else:
                ropes.append(part)
            dk = dk_ref[h]
            kvs += [dk[:, :HEAD], dv_ref[h]]
            dkr = dkr + dk[:, HEAD:]
        dqr = _rope64_t(jnp.concatenate(ropes, axis=1), cq_ref[...], sq_ref[...])
        q_ref[...] = jnp.concatenate(nope + [dqr], axis=1).astype(bf16)
        kv_ref[...] = jnp.concatenate(kvs, axis=1).astype(bf16)
        dkr = jnp.where(lane < ROPE, dkr, 0.0)
        kr_ref[...] = _rope64_t(dkr, ck_ref[...], sk_ref[...]).astype(bf16)

    hspec = lambda w: pl.BlockSpec((MH, TR, w), lambda i: (0, i, 0))
    return pl.pallas_call(
        body, name="mla_unpack_bwd", grid=(S // TR,),
        in_specs=[hspec(2 * HEAD), hspec(2 * HEAD), hspec(HEAD), _slab(TR, RWD, 0), _slab(TR, RWD, 0),
                  _slab(TR, HEAD, 0), _slab(TR, HEAD, 0)],
        out_specs=(_slab(TR, cfg.QW, 0), _slab(TR, cfg.KVW, 0), _slab(TR, HEAD, 0)),
        out_shape=(jax.ShapeDtypeStruct((S, cfg.QW), bf16), jax.ShapeDtypeStruct((S, cfg.KVW), bf16),
                   jax.ShapeDtypeStruct((S, HEAD), bf16)),
        compiler_params=_cparams(("parallel",)),
    )(dq3, dk3, dv3, cq, sq, ck, sk)


def _mla_probs(cfg, q, k, i):
    TQ, n = cfg.TQ, k.shape[0]
    nt = (((1,), (1,)), ((), ()))
    s = lax.dot_general(q, k, nt, preferred_element_type=f32) * ((HEAD + ROPE) ** -0.5)
    qc = (i * TQ + lax.broadcasted_iota(jnp.int32, (TQ, n), 0)) // CHUNK
    kc = lax.broadcasted_iota(jnp.int32, (TQ, n), 1) // CHUNK
    s = jnp.where(kc <= qc, s, -1e30)
    m = jnp.max(s, axis=-1, keepdims=True)
    e = jnp.exp(s - m)
    return e / jnp.sum(e, axis=-1, keepdims=True)


def _mla_attn_specs(cfg):
    S, TQ = cfg.S, cfg.TQ
    qs = lambda w: pl.BlockSpec((None, TQ, w), lambda h, i: (h, i, 0))
    ks = lambda w: pl.BlockSpec((None, S, w), lambda h, i: (h, 0, 0))
    hs = lambda off: pl.BlockSpec((TQ, HEAD), lambda h, i, _c=off // HEAD: (i, _c + h))
    return qs, ks, hs


def _mla_attn_fwd(cfg, proj, q3, k3, v3):
    S, TQ, MH = cfg.S, cfg.TQ, cfg.MH
    qs, ks, hs = _mla_attn_specs(cfg)

    def body(q_ref, k_ref, v_ref, g_ref, o_ref, y_ref):
        for i in range(S // TQ):
            @pl.when(pl.program_id(1) == i)
            def _(i=i):
                n = (i + 1) * TQ
                p = _mla_probs(cfg, q_ref[...], k_ref[0:n, :], i)
                o = jnp.dot(p.astype(bf16), v_ref[0:n, :], preferred_element_type=f32)
                o_ref[...] = o
                y_ref[...] = (o * _silu(g_ref[...])).astype(bf16)

    return pl.pallas_call(
        body, name="mla_attn_fwd", grid=(MH, S // TQ),
        in_specs=[qs(2 * HEAD), ks(2 * HEAD), ks(HEAD), hs(cfg.o_mg)],
        out_specs=(hs(0), hs(0)),
        out_shape=(jax.ShapeDtypeStruct((S, cfg.MW), f32), jax.ShapeDtypeStruct((S, cfg.MW), bf16)),
        compiler_params=_cparams(("parallel", "parallel")),
    )(q3, k3, v3, proj)


def _mla_attn_bwd(cfg, proj, q3, k3, v3, o, dy):
    S, TQ, MH = cfg.S, cfg.TQ, cfg.MH
    qs, ks, hs = _mla_attn_specs(cfg)

    def body(q_ref, k_ref, v_ref, g_ref, o_ref, dy_ref, dq_ref, dk_ref, dv_ref, dg_ref):
        q = q_ref[...]
        gate, dy, o = g_ref[...], dy_ref[...], o_ref[...]
        dg_ref[...] = (dy * o * _dsilu(gate)).astype(bf16)
        dob = (dy * _silu(gate)).astype(bf16)
        nt = (((1,), (1,)), ((), ()))
        tn = (((0,), (0,)), ((), ()))

        @pl.when(pl.program_id(1) == 0)
        def _():
            dk_ref[...] = jnp.zeros_like(dk_ref)
            dv_ref[...] = jnp.zeros_like(dv_ref)

        for i in range(S // TQ):
            @pl.when(pl.program_id(1) == i)
            def _(i=i):
                n = (i + 1) * TQ
                k, v = k_ref[0:n, :], v_ref[0:n, :]
                p = _mla_probs(cfg, q, k, i)
                dv_ref[0:n, :] += lax.dot_general(p.astype(bf16), dob, tn, preferred_element_type=f32)
                dp = lax.dot_general(dob, v, nt, preferred_element_type=f32)
                ds = (p * (dp - jnp.sum(dp * p, axis=-1, keepdims=True)) * ((HEAD + ROPE) ** -0.5)).astype(bf16)
                dq_ref[...] = jnp.dot(ds, k, preferred_element_type=f32)
                dk_ref[0:n, :] += lax.dot_general(ds, q, tn, preferred_element_type=f32)

    return pl.pallas_call(
        body, name="mla_attn_bwd", grid=(MH, S // TQ),
        in_specs=[qs(2 * HEAD), ks(2 * HEAD), ks(HEAD), hs(cfg.o_mg), hs(0), hs(0)],
        out_specs=(qs(2 * HEAD), ks(2 * HEAD), ks(HEAD), hs(0)),
        out_shape=(jax.ShapeDtypeStruct((MH, S, 2 * HEAD), f32), jax.ShapeDtypeStruct((MH, S, 2 * HEAD), f32),
                   jax.ShapeDtypeStruct((MH, S, HEAD), f32), jax.ShapeDtypeStruct((S, cfg.MW), bf16)),
        compiler_params=_cparams(("parallel", "arbitrary")),
    )(q3, k3, v3, proj, o, dy)


def _pick_rows(R, bytes_per_row):
    if R * bytes_per_row <= MM_BUDGET:
        return R
    best = None
    for t in range(16, R, 16):
        if R % t == 0 and t * bytes_per_row <= MM_BUDGET:
            best = t
    assert best is not None, (R, bytes_per_row)
    return best


def _adamw(w, g, m, v, name="adamw"):
    R, C = w.shape
    tr = _pick_rows(R, C * 4 * 7 * 2)
    c1 =1.0 - ADAM_B1 ** ADAM_STEP
    c2 = 1.0 - ADAM_B2 ** ADAM_STEP

    def body(w_ref, g_ref, m_ref, v_ref, d_ref, mo_ref, vo_ref):
        g = g_ref[...]
        m = ADAM_B1 * m_ref[...] + (1.0 - ADAM_B1) * g
        v = ADAM_B2 * v_ref[...] + (1.0 - ADAM_B2) * jnp.square(g)
        d_ref[...] = -ADAM_LR * ((m / c1) / (jnp.sqrt(v / c2) + ADAM_EPS) + ADAM_WD * w_ref[...])
        mo_ref[...] = m
        vo_ref[...] = v

    spec = pl.BlockSpec((tr, C), lambda i: (i, 0))
    return pl.pallas_call(
        body, name=name, grid=(R // tr,), in_specs=[spec] * 4, out_specs=(spec,) * 3,
        out_shape=(jax.ShapeDtypeStruct((R, C), f32),) * 3,
        compiler_params=_cparams(("parallel",)),
    )(w, g, m, v)


def _adamw_big(w, m, v, l0, mines, others, core, after, name, half_cols=False, prev=None):
    L, R, C = w.shape
    nl = len(mines)
    n_prev = 1 if prev is None else 5
    prev = (after,) + tuple(prev or ())
    hr, hc = (R, C // 2) if half_cols else (R // 2, C)
    tr = _pick_rows(hr, hc * 4 * (7 + 2 * nl) * 2)
    nt = hr // tr
    c1 = 1.0 - ADAM_B1 ** ADAM_STEP
    c2 = 1.0 - ADAM_B2 ** ADAM_STEP

    def body(core_ref, w_ref, m_ref, v_ref, *rest):
        g_refs, (go_ref, d_ref, mo_ref, vo_ref) = rest[:2 * nl], rest[2 * nl + n_prev:]
        l, h = pl.program_id(0), pl.program_id(1)
        own = h == core_ref[0]
        g = jnp.where(own, g_refs[0][...], g_refs[nl][...])
        for k in range(1, nl):
            g = jnp.where(l == k, jnp.where(own, g_refs[k][...], g_refs[nl + k][...]), g)
        m = ADAM_B1 * m_ref[...] + (1.0 - ADAM_B1) * g
        v = ADAM_B2 * v_ref[...] + (1.0 - ADAM_B2) * jnp.square(g)
        go_ref[...] = g
        d_ref[...] = -ADAM_LR * ((m / c1) / (jnp.sqrt(v / c2) + ADAM_EPS) + ADAM_WD * w_ref[...])
        mo_ref[...] = m
        vo_ref[...] = v

    if half_cols:
        lay = pl.BlockSpec((None, tr, hc), lambda l, h, i, core_ref: (l0 + l, i, h))
    else:
        lay = pl.BlockSpec((None, tr, hc), lambda l, h, i, core_ref: (l0 + l, h * nt + i, 0))
    gspec = lambda k: pl.BlockSpec((tr, hc), lambda l, h, i, core_ref: (jnp.where(l == k, i, 0), 0))
    return pl.pallas_call(
        body, name=name,
        grid_spec=pltpu.PrefetchScalarGridSpec(
            num_scalar_prefetch=1, grid=(nl, 2, nt),
            in_specs=[lay, lay, lay] + [gspec(k) for k in range(nl)] * 2 + _hbm_specs(n_prev), out_specs=(lay,) * 4),
        out_shape=(jax.ShapeDtypeStruct((L, R, C), f32),) * 4,
        input_output_aliases={5 + 2 * nl + k: k for k in range(n_prev - 1)},
        compiler_params=_cparams(("arbitrary", "arbitrary", "arbitrary")),
    )(core, w, m, v, *mines, *others, *prev)


def _sum_blocks(x, out_dtype, name):
    n, R, C = x.shape
    tr = _pick_rows(R, C * 4 * (n + 1) * 2)

    def body(x_ref, o_ref):
        acc = x_ref[0].astype(f32)
        for k in range(1, n):
            acc = acc + x_ref[k].astype(f32)
        o_ref[...] = acc.astype(o_ref.dtype)

    return pl.pallas_call(
        body, name=name, grid=(R // tr,),
        in_specs=[pl.BlockSpec((n, tr, C), lambda i: (0, i, 0))], out_specs=pl.BlockSpec((tr, C), lambda i: (i, 0)),
        out_shape=jax.ShapeDtypeStruct((R, C), out_dtype),
        compiler_params=_cparams(("parallel",)),
    )(x)


def _hbm_specs(n):
    return [pl.BlockSpec(memory_space=pl.ANY)] * n


def _row_map(cfg):
    nc, k0 = cfg.IN_WIDTH // 4, cfg.o_mg

    def padded(o):
        return o if o < k0 else (cfg.o_mkr + o - k0 if o < k0 + ROPE else o - ROPE)

    cuts = {0, nc}
    for q in range(4):
        cuts |= {b - q * nc for b in (k0, k0 + ROPE) if q * nc < b < (q + 1) * nc}
    cuts = sorted(cuts)
    return [((l0, l1 - l0), tuple(padded(q * nc + l0) for q in range(4))) for l0, l1 in zip(cuts[:-1], cuts[1:])]


def _chip_start(q, starts):
    st = starts[0]
    for i in range(1, 4):
        st = jnp.where(q == i, starts[i], st)
    return pl.multiple_of(st, 16)


def _allgather8(x_shard, name, after=None):
    m, n = x_shard.shape

    def body(x_ref, _, out_ref, send_sems, recv_sems, local_sem):
        x, y, c = lax.axis_index("x"), lax.axis_index("y"), lax.axis_index("c")
        me, sibling = (x, y, c), (x, y, 1 - c)
        chips = [(1 - x, y), (x, 1 - y), (1 - x, 1 - y)]

        def rows(px, py, pc):
            return out_ref.at[pl.ds((4 * px + 2 * py + pc) * m, m), :]

        def copy(k, block, to, src=None):
            return pltpu.make_async_remote_copy(
                src_ref=rows(*block) if src is None else src, dst_ref=rows(*block),
                send_sem=send_sems.at[k], recv_sem=recv_sems.at[k], device_id=to, device_id_type=MESH)

        mine = pltpu.make_async_copy(x_ref, rows(*me), local_sem)
        mine.start()
        first = [copy(0, me, sibling, src=x_ref)] + [copy(1 + j, me, (*chip, c), src=x_ref) for j, chip in enumerate(chips)]
        for cp in first:
            cp.start()
        passed = [copy(4 + j, (*chip, c), sibling) for j, chip in enumerate(chips)]
        for j, chip in enumerate(chips):
            copy(1 + j, (*chip, c), me).wait_recv()
            passed[j].start()
        copy(0, sibling, me).wait_recv()
        for j, chip in enumerate(chips):
            copy(4 + j, (*chip, 1 - c), me).wait_recv()
        for cp in first + passed:
            cp.wait_send()
        mine.wait()

    return pl.pallas_call(
        body, name=name, out_shape=jax.ShapeDtypeStruct((N_DEV * m, n), x_shard.dtype),
        in_specs=_hbm_specs(2), out_specs=pl.BlockSpec(memory_space=pl.ANY),
        scratch_shapes=[pltpu.SemaphoreType.DMA((7,)), pltpu.SemaphoreType.DMA((7,)), pltpu.SemaphoreType.DMA],
    )(x_shard, after if after is not None else jnp.zeros((SUBLANES, LANES), f32))


_SEM = pl.BlockSpec(memory_space=pltpu.SEMAPHORE)
_HBM = pl.BlockSpec(memory_space=pltpu.HBM)
_EFFECT = pltpu.SideEffectType.DATAFLOW_SIDE_EFFECTING


def _split_start(srcs, lands, after, plan, n, name):
    bufs = list(srcs) + list(lands)
    nb, ns = len(bufs), len(srcs)

    def body(*refs):
        send_sems, recv_sems = refs[nb + 1], refs[nb + 2]
        for k, (src, dst, _, dev) in enumerate(plan(refs[:ns], refs[ns:nb])):
            pltpu.make_async_remote_copy(src_ref=src, dst_ref=dst, send_sem=send_sems.at[k], recv_sem=recv_sems.at[k],
                                         device_id=dev, device_id_type=MESH).start()
        refs[-1][...] = jnp.zeros_like(refs[-1])

    out = pl.pallas_call(
        body, name=name,
        out_shape=(pltpu.SemaphoreType.DMA((n,)), pltpu.SemaphoreType.DMA((n,)), *[pltpu.HBM(b.shape, b.dtype) for b in bufs],
                   jax.ShapeDtypeStruct((SUBLANES, LANES), f32)),
        in_specs=[_HBM] * nb + [pl.BlockSpec(memory_space=pl.ANY)],
        out_specs=(_SEM, _SEM, *[_HBM] * nb, pl.BlockSpec(memory_space=pltpu.VMEM)),
        input_output_aliases={i: 2 + i for i in range(nb)},
        compiler_params=pltpu.CompilerParams(has_side_effects=_EFFECT),
    )(*[pltpu.with_memory_space_constraint(b, pltpu.HBM) for b in bufs], after)
    return out[0], out[1], list(out[2:2 + ns]), list(out[2 + ns:2 + nb]), out[-1]


def _split_wait(srcs, lands, send_sems, recv_sems, after, plan, name):
    bufs = list(srcs) + list(lands)
    nb, ns = len(bufs), len(srcs)

    def body(*refs):
        send, recv = refs[nb], refs[nb + 1]
        for k, (src, _, dst, dev) in enumerate(plan(refs[:ns], refs[ns:nb])):
            cp = pltpu.make_async_remote_copy(src_ref=src, dst_ref=dst, send_sem=send.at[k], recv_sem=recv.at[k],
                                              device_id=dev, device_id_type=MESH)
            cp.wait_send()
            cp.wait_recv()

    out = pl.pallas_call(
        body, name=name, out_shape=tuple(pltpu.HBM(b.shape, b.dtype) for b in bufs),
        in_specs=[_HBM] * nb + [_SEM, _SEM, pl.BlockSpec(memory_space=pl.ANY)], out_specs=tuple([_HBM] * nb),
        input_output_aliases={i: i for i in range(nb)},
        compiler_params=pltpu.CompilerParams(has_side_effects=_EFFECT),
    )(*bufs, send_sems, recv_sems, after)
    return list(out[:ns]), list(out[ns:])


def _weight_windows(cfg, shards, out_refs, px, py, pc):
    rmap = _row_map(cfg)
    m, n = shards[0].shape
    cols = pl.ds(pl.multiple_of(pc * n, n), n)
    wins = [[(pl.ds(l0, cnt), out_refs[0].at[pl.ds(_chip_start(2 * px + py, starts), cnt), cols]) for (l0, cnt), starts in rmap]]
    for a in range(1, len(shards)):
        m = shards[a].shape[0]
        wins.append([(pl.ds(0, m), out_refs[a].at[pl.ds((4 * px + 2 * py + pc) * m, m), :])])
    return wins


def _gather_shapes(cfg, shards):
    return [jax.ShapeDtypeStruct((cfg.NP, cfg.D), shards[0].dtype)] + \
           [jax.ShapeDtypeStruct((N_DEV * s.shape[0], s.shape[1]), s.dtype) for s in shards[1:]]


def _gather_plan(cfg, shards):
    def plan(x_refs, land_refs):
        x, y, c = lax.axis_index("x"), lax.axis_index("y"), lax.axis_index("c")
        mine = _weight_windows(cfg, shards, land_refs, x, y, c)
        out = []
        for peer in [(x, y, 1 - c), (1 - x, y, c), (x, 1 - y, c), (1 - x, 1 - y, c)]:
            theirs = _weight_windows(cfg, shards, land_refs, *peer)
            for a in range(len(shards)):
                for (rows, win), (_, win_in) in zip(mine[a], theirs[a]):
                    out.append((x_refs[a].at[rows, :], win, win_in, peer))
        return out
    return plan


def _gather_finish(cfg, shards, lands, name):
    na = len(shards)
    rmap = _row_map(cfg)
    npc = len(rmap)
    nz = cfg.NP - cfg.o_mkr - ROPE

    def body(*refs):
        x_refs, out_refs = refs[:na], refs[2 * na:3 * na]
        stage, zbuf = refs[3 * na:4 * na], refs[4 * na]
        send_sems, recv_sems, local_sems = refs[4 * na + 1:]
        x, y, c = lax.axis_index("x"), lax.axis_index("y"), lax.axis_index("c")
        sibling = (x, y, 1 - c)
        chips = [(1 - x, y), (x, 1 - y), (1 - x, 1 - y)]
        load = [pltpu.make_async_copy(x_refs[a], stage[a], local_sems.at[a, npc]) for a in range(na)]
        for cp in load:
            cp.start()
        passed = []
        for j, chip in enumerate(chips):
            wins = _weight_windows(cfg, shards, out_refs, *chip, c)
            for a in range(na):
                passed += [pltpu.make_async_remote_copy(src_ref=win, dst_ref=win, send_sem=send_sems.at[a, j, p],
                                                        recv_sem=recv_sems.at[a, j, p], device_id=sibling, device_id_type=MESH)
                           for p, (_, win) in enumerate(wins[a])]
        for cp in passed:
            cp.start()
        zbuf[...] = jnp.zeros_like(zbuf)
        for cp in load:
            cp.wait()
        own = _weight_windows(cfg, shards, out_refs, x, y, c)
        store = [pltpu.make_async_copy(stage[a].at[rows, :], win, local_sems.at[a, p])
                 for a in range(na) for p, (rows, win) in enumerate(own[a])]
        store.append(pltpu.make_async_copy(zbuf, out_refs[0].at[pl.ds(cfg.NP - nz, nz), :], local_sems.at[0, npc + 1]))
        for cp in store:
            cp.start()
        for j, chip in enumerate(chips):
            wins = _weight_windows(cfg, shards, out_refs, *chip, 1 - c)
            for a in range(na):
                for p, (_, win) in enumerate(wins[a]):
                    pltpu.make_async_remote_copy(src_ref=win, dst_ref=win, send_sem=send_sems.at[a, j, p],
                                                 recv_sem=recv_sems.at[a, j, p], device_id=sibling,
                                                 device_id_type=MESH).wait_recv()
        for cp in passed:
            cp.wait_send()
        for cp in store:
            cp.wait()

    return pl.pallas_call(
        body, name=name, out_shape=_gather_shapes(cfg, shards),
        in_specs=_hbm_specs(2 * na), out_specs=_hbm_specs(na),
        input_output_aliases={na + a: a for a in range(na)},
        scratch_shapes=[pltpu.VMEM(s.shape, s.dtype) for s in shards] + [pltpu.VMEM((nz, cfg.D), shards[0].dtype)]
        + [pltpu.SemaphoreType.DMA((na, 3, npc)), pltpu.SemaphoreType.DMA((na, 3, npc)), pltpu.SemaphoreType.DMA((na, npc + 2))],
        compiler_params=pltpu.CompilerParams(vmem_limit_bytes=VMEM_LIMIT),
    )(*shards, *lands)


def _gather_weights_start(cfg, shards, after):
    lands = [lax.empty(s.shape, s.dtype) for s in _gather_shapes(cfg, shards)]
    n = 4 * (len(_row_map(cfg)) + len(shards) - 1)
    return _split_start(shards, lands, after, _gather_plan(cfg, shards), n, "gather_w_start")


def _gather_weights_end(cfg, shards, started, after):
    send_sems, recv_sems, srcs, lands, _ = started
    srcs, lands = _split_wait(srcs, lands, send_sems, recv_sems, after, _gather_plan(cfg, shards), "gather_w_wait")
    return _gather_finish(cfg, srcs, lands, "gather_w_finish")


def _send_sibling(arrays, name):
    na = len(arrays)

    def body(*refs):
        x_refs, out_refs = refs[:na], refs[na:2 * na]
        send_sems, recv_sems = refs[2 * na:]
        sibling = (lax.axis_index("x"), lax.axis_index("y"), 1 - lax.axis_index("c"))
        cps = [pltpu.make_async_remote_copy(src_ref=x_refs[a], dst_ref=out_refs[a], send_sem=send_sems.at[a],
                                            recv_sem=recv_sems.at[a], device_id=sibling, device_id_type=MESH)
               for a in range(na)]
        for cp in cps:
            cp.start()
        for cp in cps:
            cp.wait()

    return pl.pallas_call(
        body, name=name, out_shape=[jax.ShapeDtypeStruct(x.shape, x.dtype) for x in arrays],
        in_specs=_hbm_specs(na), out_specs=_hbm_specs(na),
        scratch_shapes=[pltpu.SemaphoreType.DMA((na,)), pltpu.SemaphoreType.DMA((na,))],
    )(*arrays)


def _slot_pairs(cfg, p_refs, slot_refs, a, to_chip, slot):
    if a == 0:
        return [(p_refs[0].at[pl.ds(_chip_start(to_chip, starts), cnt), :], slot_refs[0].at[slot, pl.ds(l0, cnt), :])
                for (l0, cnt), starts in _row_map(cfg)]
    return [(p_refs[a].at[to_chip], slot_refs[a].at[slot])]


def _scatter_plan(cfg, na):
    def plan(p_refs, slot_refs):
        x, y, c = lax.axis_index("x"), lax.axis_index("y"), lax.axis_index("c")
        mychip = 2 * x + y
        out = []
        for cx, cy in [(1 - x, y), (x, 1 - y), (1 - x, 1 - y)]:
            q = 2 * cx + cy
            for a in range(na):
                for (src, dst), (_, dst_in) in zip(_slot_pairs(cfg, p_refs, slot_refs, a, q, mychip),
                                                   _slot_pairs(cfg, p_refs, slot_refs, a, mychip, q)):
                    out.append((src, dst, dst_in, (cx, cy, c)))
        return out
    return plan


def _slot_shapes(cfg, parts):
    return [jax.ShapeDtypeStruct((4, cfg.IN_WIDTH // 4, parts[0].shape[1]), parts[0].dtype)] + \
           [jax.ShapeDtypeStruct(p.shape, p.dtype) for p in parts[1:]]


def _place_own(cfg, parts, slots, name):
    na = len(parts)
    npc = len(_row_map(cfg))
    shapes = _slot_shapes(cfg, parts)

    def body(*refs):
        p_refs, out_refs = refs[:na], refs[2 * na:3 * na]
        stage, sems = refs[3 * na:4 * na], refs[4 * na]
        mychip = 2 * lax.axis_index("x") + lax.axis_index("y")
        moves = []
        for a in range(na):
            for p, (src, dst) in enumerate(_slot_pairs(cfg, p_refs, out_refs, a, mychip, mychip)):
                buf = stage[a].at[pl.ds(*_row_map(cfg)[p][0]), :] if a == 0 else stage[a]
                moves.append((pltpu.make_async_copy(src, buf, sems.at[a, p]), pltpu.make_async_copy(buf, dst, sems.at[a, npc + p])))
        for load, _ in moves:
            load.start()
        for load, store in moves:
            load.wait()
            store.start()
        for _, store in moves:
            store.wait()

    return pl.pallas_call(
        body, name=name, out_shape=shapes, in_specs=_hbm_specs(2 * na), out_specs=_hbm_specs(na),
        input_output_aliases={na + a: a for a in range(na)},
        scratch_shapes=[pltpu.VMEM(s.shape[1:], s.dtype) for s in shapes] + [pltpu.SemaphoreType.DMA((na, 2 * npc))],
        compiler_params=pltpu.CompilerParams(vmem_limit_bytes=VMEM_LIMIT),
    )(*parts, *slots)


def _add_half(g, got, core, name):
    if g.ndim == 2:
        R, hd = got.shape
        tr = _pick_rows(R, hd * 4 * 3 * 2)
        grid = (R // tr,)
        g_spec = pl.BlockSpec((tr, hd), lambda i, core_ref: (i, core_ref[0]))
        o_spec = pl.BlockSpec((tr, hd), lambda i, core_ref: (i, 0))
    else:
        _, hr, nc = got.shape
        tr = _pick_rows(hr, nc * 4 * 3 * 2)
        grid = (4, hr // tr)
        g_spec = pl.BlockSpec((None, None, tr, nc), lambda q, i, core_ref: (q, core_ref[0], i, 0))
        o_spec = pl.BlockSpec((None, tr, nc), lambda q, i, core_ref: (q, i, 0))

    def body(core_ref, g_ref, got_ref, o_ref):
        o_ref[...] = (g_ref[...].astype(f32) + got_ref[...].astype(f32)).astype(o_ref.dtype)

    return pl.pallas_call(
        body, name=name,
        grid_spec=pltpu.PrefetchScalarGridSpec(num_scalar_prefetch=1, grid=grid, in_specs=[g_spec, o_spec], out_specs=o_spec),
        out_shape=jax.ShapeDtypeStruct(got.shape, bf16),
        compiler_params=_cparams(("arbitrary",) * len(grid)),
    )(core, g, got)


def _pair_exchange(cfg, g_in_t, grads, name):
    na = 1 + len(grads)
    hd = cfg.D // 2

    def body(*refs):
        g_refs, out_refs = refs[:na], refs[na:2 * na]
        send_sems, recv_sems = refs[2 * na:]
        x, y, c = lax.axis_index("x"), lax.axis_index("y"), lax.axis_index("c")
        cps = [pltpu.make_async_remote_copy(
            src_ref=g_refs[0].at[:, pl.ds(pl.multiple_of((1 - c) * hd, hd), hd)], dst_ref=out_refs[0],
            send_sem=send_sems.at[0, 0], recv_sem=recv_sems.at[0, 0], device_id=(x, y, 1 - c), device_id_type=MESH)]
        for a in range(1, na):
            cps += [pltpu.make_async_remote_copy(src_ref=g_refs[a].at[q, 1 - c], dst_ref=out_refs[a].at[q],
                                                 send_sem=send_sems.at[a, q], recv_sem=recv_sems.at[a, q],
                                                 device_id=(x, y, 1 - c), device_id_type=MESH) for q in range(4)]
        for cp in cps:
            cp.start()
        for cp in cps:
            cp.wait()

    out_shape = [jax.ShapeDtypeStruct((cfg.NP, hd), g_in_t.dtype)] + \
                [jax.ShapeDtypeStruct((4,) + g.shape[2:], g.dtype) for g in grads]
    return pl.pallas_call(
        body, name=name, out_shape=out_shape, in_specs=_hbm_specs(na), out_specs=_hbm_specs(na),
        scratch_shapes=[pltpu.SemaphoreType.DMA((na, 4)), pltpu.SemaphoreType.DMA((na, 4))],
    )(g_in_t, *grads)


def _reduce_scatter_start(cfg, g_in_t, grads, after):
    core = lax.axis_index("c").astype(jnp.int32).reshape(1)
    got = _pair_exchange(cfg, g_in_t, grads, "rs_pair")
    part = [_add_half(g, h, core, "rs_add_pair") for g, h in zip([g_in_t] + list(grads), got)]
    slots = [lax.empty(s.shape, s.dtype) for s in _slot_shapes(cfg, part)]
    n = 3 * (len(_row_map(cfg)) + len(part) - 1)
    return _split_start(part, slots, after, _scatter_plan(cfg, len(part)), n, "rs_chips_start")


def _reduce_scatter_end(cfg, started, after):
    send_sems, recv_sems, parts, slots, _ = started
    parts, slots = _split_wait(parts, slots, send_sems, recv_sems, after, _scatter_plan(cfg, len(parts)), "rs_chips_wait")
    slots = _place_own(cfg, parts, slots, "rs_own")
    mine = [_sum_blocks(s, f32, "rs_add_chips") for s in slots]
    return mine, _send_sibling(mine, "rs_halves")


def _big_weights(cfg):
    return (("mla_w_uq", cfg.QL, cfg.QW, 1), ("mla_w_ukv", cfg.KL, cfg.KVW, 1),
            ("w_branch", cfg.RW + cfg.LW + cfg.MW, cfg.D, 0), ("w_out", cfg.D, cfg.D, 0))


def _half_shapes(cfg):
    out = []
    for _, r, c, ax in _big_weights(cfg):
        out.append((r // 2, c // 4) if ax == 1 else (r // 8, c))
    return out


def _my_halves(cfg, W, l, c):
    hd = cfg.D // 2
    out = [lax.dynamic_slice_in_dim(W["w_in"][l].T, c * hd, hd, axis=1).astype(bf16)]
    for (name, *_), (hr, nc) in zip(_big_weights(cfg), _half_shapes(cfg)):
        out.append(lax.dynamic_slice_in_dim(W[name][l], c * hr, hr, axis=0).astype(bf16))
    return out


def _uq_split(cfg, w):
    hw = HEAD + ROPE
    return jnp.concatenate([w[:, h * hw:h * hw + HEAD] for h in range(cfg.MH)]
                           + [w[:, h * hw + HEAD:(h + 1) * hw] for h in range(cfg.MH)], axis=1)


def _uq_join(cfg, g):
    n = cfg.MH * HEAD
    parts = []
    for h in range(cfg.MH):
        parts += [g[:, h * HEAD:(h + 1) * HEAD], g[:, n + h * ROPE:n + (h + 1) * ROPE]]
    return jnp.concatenate(parts, axis=1)


def _col_blocks(g):
    nc = g.shape[1] // 4
    return jnp.stack([g[:, q * nc:(q + 1) * nc] for q in range(4)])


def _row_pack(parts):
    rows = []
    for p in parts:
        r = p.reshape(-1, LANES)
        pad = -r.shape[0] % SUBLANES
        rows.append(jnp.concatenate([r, jnp.zeros((pad, LANES), r.dtype)], axis=0) if pad else r)
    return jnp.concatenate(rows, axis=0)


def _row_unpack(packed, like):
    out, off = [], 0
    for p in like:
        n = p.size // LANES
        out.append(packed[off:off + n].reshape(p.shape))
        off += -(-n // SUBLANES) * SUBLANES
    return out


def _prep_layer(cfg, full, small):
    w_in_t, w_uq, w_ukv, w_branch, w_out = full
    RW, LW = cfg.RW, cfg.LW
    P = dict(small)
    P["w_in_t"] = w_in_t
    P["w_uq"] = _uq_split(cfg, jnp.concatenate(list(w_uq.reshape(4, cfg.QL, -1)), axis=1))
    P["w_ukv"] = jnp.concatenate(list(w_ukv.reshape(4, cfg.KL, -1)), axis=1)
    P["wb"] = (w_branch[:RW], w_branch[RW:RW + LW], w_branch[RW + LW:])
    P["w_out"] = w_out
    return P


def _layer_fwd(cfg, x, mod, P, T):
    h = _prenorm_fwd(cfg, x, mod, P["norm_pre"])
    proj = _mm(h, P["w_in_t"], f32, "mm_proj", mode="nt")
    y_ret = _ret_fwd(cfg, proj, P["ret_gn"], T["cos_r"], T["sin_r"], T["ret_consts"])
    a, b = _lru_gates(cfg, proj, P["lru_conv_w"], P["lru_conv_b"], P["lru_wa"], P["lru_ba"], P["lru_wx"], P["lru_bx"],
                      P["lru_lambda"])
    hl, y_lru = _lru_scan_fwd(cfg, proj, a, b)
    qn, kn = _mla_norm(cfg, proj, P["mla_q_norm"], P["mla_kv_norm"])
    q = _mm(qn, P["w_uq"], f32, "mm_uq")
    kv = _mm(kn, P["w_ukv"], f32, "mm_ukv")
    q3, k3, v3 = _mla_pack(cfg, proj, q, kv, T["cos_q"], T["sin_q"], T["cos_k"], T["sin_k"])
    o, y_mla = _mla_attn_fwd(cfg, proj, q3, k3, v3)
    ys = (y_ret, y_lru, y_mla)
    us = tuple(_mm(yb, wb, bf16, "mm_branch") for yb, wb in zip(ys, P["wb"]))
    merged = _merge_fwd(cfg, proj, *us)
    y = _mm(merged, P["w_out"], f32, "mm_out")
    out = _postnorm_fwd(cfg, x, y, mod, P["norm_post"])
    R = dict(x=x, h=h, proj=proj, ys=ys, a=a, hl=hl, qn=qn, kn=kn, q3=q3, k3=k3, v3=v3, o=o, us=us, merged=merged, y=y)
    return out, R


def _layer_bwd(cfg, dout, R, mod, P, T):
    proj = R["proj"]
    dy, s_post = _postnorm_bwd(cfg, dout, R["y"], mod, P["norm_post"])
    dmerged = _mm(dy, P["w_out"], f32, "mm_dmerged", mode="nt")
    g_out = _mm(R["merged"], dy, bf16, "mm_gw_out", mode="tn")
    du0, du1, du2, dlog = _merge_bwd(cfg, proj, dmerged, *R["us"])
    dus = (du0, du1, du2)
    dys = tuple(_mm(du, wb, f32, "mm_dbranch", mode="nt") for du, wb in zip(dus, P["wb"]))
    g_branch = jnp.concatenate([_mm(yb, du, bf16, "mm_gw_branch", mode="tn") for yb, du in zip(R["ys"], dus)], axis=0)
    drq, drk, drv, drg, dgn = _ret_bwd(cfg, proj, dys[0], P["ret_gn"], T["cos_r"], T["sin_r"], T["ret_consts"])
    da, db, dlg = _lru_scan_bwd(cfg, proj, R["a"], R["hl"], dys[1])
    dxc, dwa, dwx, s_lru = _lru_gates_bwd(cfg, proj, da, db, P["lru_conv_w"], P["lru_conv_b"], P["lru_wa"], P["lru_ba"],
                                          P["lru_wx"], P["lru_bx"], P["lru_lambda"])
    dlx, s_conv = _lru_conv_bwd(cfg, proj, dxc, P["lru_conv_w"])
    dq3, dk3, dv3, dmg = _mla_attn_bwd(cfg, proj, R["q3"], R["k3"], R["v3"], R["o"], dys[2])
    dq, dkv, dmkr = _mla_unpack_bwd(cfg, dq3, dk3, dv3, T["cos_q"], T["sin_q"], T["cos_k"], T["sin_k"])
    dqn = _mm(dq, P["w_uq"], f32, "mm_dqn", mode="nt")
    dkn = _mm(dkv, P["w_ukv"], f32, "mm_dkn", mode="nt")
    g_uq = _uq_join(cfg, _mm(R["qn"], dq, bf16, "mm_gw_uq", mode="tn"))
    g_ukv = _mm(R["kn"], dkv, bf16, "mm_gw_ukv", mode="tn")
    dmq, dmkv, s_q, s_k = _mla_norm_bwd(cfg, proj, dqn, dkn, P["mla_q_norm"], P["mla_kv_norm"])
    dproj = jnp.concatenate([drq, drk, drv, drg, dlx, dlg, dmq, dmkv, dmg, dlog, dmkr,
                             jnp.zeros((cfg.S, cfg.NP - cfg.o_mkr - HEAD), bf16)], axis=1)
    dh = _mm(dproj, P["w_in_t"], f32, "mm_dh")
    g_in_t = _mm(dproj, R["h"], bf16, "mm_gw_in", mode="tn", tm=512)
    dx, s_pre = _prenorm_bwd(cfg, R["x"], dh, dout, mod, P["norm_pre"])
    big = [_col_blocks(g_uq), _col_blocks(g_ukv), g_branch, g_out]
    big = (g_in_t, [g.reshape(4, 2, hr, nc) for g, (hr, nc) in zip(big, _half_shapes(cfg))])
    small = dict(norm_pre=s_pre[2:3], norm_post=s_post[1:2], ret_gn=dgn, lru_conv_w=s_conv[0:CONV], lru_conv_b=s_conv[CONV:CONV + 1],
                 lru_wa=dwa, lru_ba=s_lru[0:1], lru_wx=dwx, lru_bx=s_lru[1:2], lru_lambda=s_lru[2:3],
                 mla_q_norm=s_q[0:1], mla_kv_norm=s_k[0:1])
    dmod = jnp.concatenate([s_pre[0:1], s_pre[1:2], s_post[0:1]], axis=1)
    return dx, big, small, dmod


_SMALL = ("norm_pre", "norm_post", "ret_gn", "lru_conv_w", "lru_conv_b", "lru_wa", "lru_ba", "lru_wx", "lru_bx", "lru_lambda",
          "mla_q_norm", "mla_kv_norm")
_WEIGHTS = ("ada_w", "ada_b", "norm_pre", "norm_post", "w_in", "ret_gn", "lru_conv_w", "lru_conv_b", "lru_wa", "lru_ba", "lru_wx",
            "lru_bx", "lru_lambda", "mla_q_norm", "mla_w_uq", "mla_kv_norm", "mla_w_ukv", "w_branch", "w_out")


def _step(cfg, x, c, positions, W, target, M1, V1):
    L, D = cfg.L, cfg.D
    xi, yi, ci = lax.axis_index("x"), lax.axis_index("y"), lax.axis_index("c")
    chip = 2 * xi + yi
    me = 2 * chip + ci

    c8 = jnp.concatenate([c, jnp.zeros((SUBLANES - 1, D), f32)], axis=0)
    c_all = _allgather8(c8, "gather_c").reshape(N_DEV, SUBLANES, D)[:, 0]
    mod_sh, c_act = _ada_fwd(cfg, c_all, W["ada_w"])
    n_sh = mod_sh.shape[2]
    mod_half = lax.dynamic_slice_in_dim(mod_sh, ci * (n_sh // 2), n_sh // 2, axis=2).reshape(L * N_DEV, n_sh // 2)
    mod_all = _allgather8(mod_half, "gather_mod").reshape(N_DEV, L, N_DEV, n_sh // 2)
    mod_all = mod_all.transpose(1, 2, 0, 3).reshape(L, N_DEV, 3 * D)
    mods = lax.dynamic_index_in_dim(mod_all, me, axis=1, keepdims=False) + W["ada_b"]

    (cos_r, sin_r), (cos_m, sin_m) = _rope_tables(cfg, positions)
    T = dict(cos_r=cos_r, sin_r=sin_r, cos_q=jnp.tile(cos_m, (1, cfg.MH)), sin_q=jnp.tile(sin_m, (1, cfg.MH)),
             cos_k=jnp.tile(cos_m, (1, 2)), sin_k=jnp.tile(sin_m, (1, 2)), ret_consts=_ret_consts(cfg))

    Ps, Rs = [], []
    act = x[0]
    cw_all = _allgather8(_pad_rows(W["lru_conv_w"].reshape(L * CONV, -1)), "gather_conv", after=mods)
    started = _gather_weights_start(cfg, _my_halves(cfg, W, 0, ci), cw_all)
    cw_rows = cw_all.shape[0] // N_DEV
    cw_all = cw_all.reshape(4, 2, cw_rows, -1)[:, 0, :L * CONV].transpose(1, 0, 2).reshape(L, CONV, cfg.LW)
    after = mods
    for l in range(L):
        gathered = _gather_weights_end(cfg, started[2], started, after)
        small = {k: (W[k][l] if W[k][l].ndim > 1 else W[k][l][None, :]) for k in _SMALL if k != "lru_conv_w"}
        P = _prep_layer(cfg, gathered, small)
        P["lru_conv_w"] = cw_all[l]
        Ps.append(P)
        mod = mods[l:l + 1]
        if l + 1 < L:
            started = _gather_weights_start(cfg, _my_halves(cfg, W, l + 1, ci), gathered[-1])
            mod = mod + started[4][0, 0]
        act, R = _layer_fwd(cfg, act, mod, P, T)
        Rs.append(R)
        after = act

    dact, lsum = _loss_head(cfg, act, target[0])
    loss = lax.psum(lsum[0, 0], ("x", "y", "c"))

    big_g = [None] * L
    small_g = [None] * L
    dmods = [None] * L
    pending = None
    for l in range(L - 1, -1, -1):
        mod = mods[l:l + 1]
        if pending is not None:
            mod = mod + pending[4][0, 0]
        dact, grads, small_g[l], dmods[l] = _layer_bwd(cfg, dact, Rs[l], mod, Ps[l], T)
        if pending is not None:
            big_g[l + 1] = _reduce_scatter_end(cfg, pending, dact)
        if l > 0:
            pending = _reduce_scatter_start(cfg, *grads, grads[1][-1])

    dmod = jnp.concatenate(dmods, axis=0)
    parts = [dmod] + [small_g[l][k] for l in range(L) for k in _SMALL]
    packed = _row_pack(parts)
    allf = _allgather8(packed, "gather_small")
    pending = _reduce_scatter_start(cfg, *grads, allf)
    tok = pending[4][0, 0]
    allf = allf.reshape(N_DEV, packed.shape[0], LANES)
    summed = _row_unpack(_sum_blocks(allf, f32, "sum_small"), parts)
    gsm = {k: jnp.stack([summed[1 + l * len(_SMALL) + i].reshape(W[k].shape[1:] if k != "lru_conv_w" else (CONV, cfg.LW))
                         for l in range(L)]) for i, k in enumerate(_SMALL)}
    ncw = cfg.LW // 4
    gsm["lru_conv_w"] = lax.dynamic_slice_in_dim(gsm["lru_conv_w"], chip * ncw, ncw, axis=2)
    gsm["ada_b"] = summed[0]
    dmod_all = allf[:, :dmod.size // LANES].reshape(N_DEV, L, 3 * D)
    dmod_sh = lax.dynamic_slice_in_dim(dmod_all, chip * n_sh, n_sh, axis=2).transpose(1, 0, 2) + tok
    G = dict(gsm)
    G["ada_w"] = _ada_bwd(cfg, c_act.T, dmod_sh)
    delta, new_m, new_v = {}, {}, {}
    bigs = ("ada_w", "w_in") + tuple(name for name, *_ in _big_weights(cfg))
    shp = W["ada_w"].shape
    two = lambda a: a.reshape(-1, shp[-1])
    d, m_, v_ = _adamw(two(W["ada_w"]), two(G["ada_w"]), two(M1["ada_w"]), two(V1["ada_w"]), "adamw_ada_w")
    delta["ada_w"], new_m["ada_w"], new_v["ada_w"] = d.reshape(shp), m_.reshape(shp), v_.reshape(shp)
    smalls = [k for k in _WEIGHTS if k not in bigs]
    packs = [_row_pack([src[k] for k in smalls]) for src in (W, G, M1, V1)]
    packs[1] = packs[1] + tok
    outs = _adamw(*packs, "adamw_small")
    for dst, o in zip((delta, new_m, new_v), outs):
        for k, val in zip(smalls, _row_unpack(o, [W[k] for k in smalls])):
            dst[k] = val
    core = ci.astype(jnp.int32).reshape(1)
    tr_ = lambda a: a.transpose(0, 2, 1)

    def update(l0, l1, prev):
        res = {}
        for i, (name, *_) in enumerate(_big_weights(cfg)):
            res[name] = _adamw_big(W[name], M1[name], V1[name], l0, [big_g[l][0][i + 1] for l in range(l0, l1)],
                                   [big_g[l][1][i + 1] for l in range(l0, l1)], core, pending[4], "adamw_" + name,
                                   prev=prev and prev[name])
        res["w_in"] = _adamw_big(tr_(W["w_in"]), tr_(M1["w_in"]), tr_(V1["w_in"]), l0, [big_g[l][0][0] for l in range(l0, l1)],
                                 [big_g[l][1][0] for l in range(l0, l1)], core, pending[4], "adamw_w_in", half_cols=True,
                                 prev=prev and prev["w_in"])
        return res

    upper = update(1, L, None) if L > 1 else None
    big_g[0] = _reduce_scatter_end(cfg, pending, upper["w_in"][0] if upper else outs[0])
    res = update(0, 1, upper)
    for name, *_ in _big_weights(cfg):
        G[name], delta[name], new_m[name], new_v[name] = res[name]
    G["w_in"], delta["w_in"], new_m["w_in"], new_v["w_in"] = [tr_(o) for o in res["w_in"]]

    grad_x = dact[None]
    return (loss, grad_x, *[G[k] for k in _WEIGHTS], *[delta[k] for k in _WEIGHTS], *[new_m[k] for k in _WEIGHTS],
            *[new_v[k] for k in _WEIGHTS])


def _pad_rows(a):
    pad = -a.shape[0] % SUBLANES
    return jnp.concatenate([a, jnp.zeros((pad, a.shape[1]), a.dtype)], axis=0) if pad else a


def kernel(x, c, positions, ada_w, ada_b, norm_pre, norm_post, w_in, ret_gn, lru_conv_w, lru_conv_b, lru_wa, lru_ba, lru_wx, lru_bx, lru_lambda, mla_q_norm, mla_w_uq, mla_kv_norm, mla_w_ukv, w_branch, w_out, loss_target, m_ada_w, m_ada_b, m_norm_pre, m_norm_post, m_w_in, m_ret_gn, m_lru_conv_w, m_lru_conv_b, m_lru_wa, m_lru_ba, m_lru_wx, m_lru_bx, m_lru_lambda, m_mla_q_norm, m_mla_w_uq, m_mla_kv_norm, m_mla_w_ukv, m_w_branch, m_w_out, v_ada_w, v_ada_b, v_norm_pre, v_norm_post, v_w_in, v_ret_gn, v_lru_conv_w, v_lru_conv_b, v_lru_wa, v_lru_ba, v_lru_wx, v_lru_bx, v_lru_lambda, v_mla_q_norm, v_mla_w_uq, v_mla_kv_norm, v_mla_w_ukv, v_w_branch, v_w_out):
    W = dict(ada_w=ada_w, ada_b=ada_b, norm_pre=norm_pre, norm_post=norm_post, w_in=w_in, ret_gn=ret_gn, lru_conv_w=lru_conv_w,
             lru_conv_b=lru_conv_b, lru_wa=lru_wa, lru_ba=lru_ba, lru_wx=lru_wx, lru_bx=lru_bx, lru_lambda=lru_lambda,
             mla_q_norm=mla_q_norm, mla_w_uq=mla_w_uq, mla_kv_norm=mla_kv_norm, mla_w_ukv=mla_w_ukv, w_branch=w_branch, w_out=w_out)
    M1 = dict(ada_w=m_ada_w, ada_b=m_ada_b, norm_pre=m_norm_pre, norm_post=m_norm_post, w_in=m_w_in, ret_gn=m_ret_gn,
              lru_conv_w=m_lru_conv_w, lru_conv_b=m_lru_conv_b, lru_wa=m_lru_wa, lru_ba=m_lru_ba, lru_wx=m_lru_wx, lru_bx=m_lru_bx,
              lru_lambda=m_lru_lambda, mla_q_norm=m_mla_q_norm, mla_w_uq=m_mla_w_uq, mla_kv_norm=m_mla_kv_norm,
              mla_w_ukv=m_mla_w_ukv, w_branch=m_w_branch, w_out=m_w_out)
    V1 = dict(ada_w=v_ada_w, ada_b=v_ada_b, norm_pre=v_norm_pre, norm_post=v_norm_post, w_in=v_w_in, ret_gn=v_ret_gn,
              lru_conv_w=v_lru_conv_w, lru_conv_b=v_lru_conv_b, lru_wa=v_lru_wa, lru_ba=v_lru_ba, lru_wx=v_lru_wx, lru_bx=v_lru_bx,
              lru_lambda=v_lru_lambda, mla_q_norm=v_mla_q_norm, mla_w_uq=v_mla_w_uq, mla_kv_norm=v_mla_kv_norm,
              mla_w_ukv=v_mla_w_ukv, w_branch=v_w_branch, w_out=v_w_out)
    return _step(_CFG, x, c, positions, W, loss_target, M1, V1)
```
